```python
import math
import jax, jax.numpy as jnp
from jax import lax
import numpy as np

D_MODEL = 1024
BATCH = 32
SEQ = 2048
DEPTH = 1

N_MEM = 256
D_MIX = D_MODEL
CHUNK = 128
GM_GROUPS = 4
GM_WIDTH = D_MIX // 2
GM_DIM = GM_WIDTH // GM_GROUPS
SB_WIDTH = D_MIX - GM_WIDTH
SB_HEADS = 8
SB_HEAD_DIM = SB_WIDTH // SB_HEADS
Q_BLOCK = 128
IN_COLS = 2 * GM_WIDTH + 3 * SB_WIDTH
X_HEADS = 4
X_HEAD_DIM = D_MODEL // X_HEADS
D_FF = 4 * D_MODEL
EPS = 1e-6

kernel_name = "hybrid_gmlp_stickbreaking_memxattn_block"


def rms_norm(x, g):
    xf = x.astype(jnp.float32)
    y = xf * lax.rsqrt(jnp.mean(xf * xf, axis=-1, keepdims=True) + EPS)
    return (y * g.astype(jnp.float32)).astype(x.dtype)


def head_rms(x):
    xf = x.astype(jnp.float32)
    return (xf * lax.rsqrt(jnp.mean(xf * xf, axis=-1, keepdims=True) + EPS)).astype(x.dtype)


def spatial_gating(u, v, v_norm_g, w_spatial, b_spatial):
    b, s, _ = u.shape
    n_chunks = s // CHUNK
    u = u.reshape(b, n_chunks, CHUNK, GM_GROUPS, GM_DIM)
    v = v.reshape(b, n_chunks, CHUNK, GM_GROUPS, GM_DIM)
    v = rms_norm(v, v_norm_g.reshape(GM_GROUPS, GM_DIM))
    mask = jnp.tril(jnp.ones((CHUNK, CHUNK), dtype=bool))
    w = jnp.where(mask[None], w_spatial, jnp.zeros_like(w_spatial))
    mixed = jnp.einsum('gts,bcsgd->bctgd', w, v) + b_spatial.T[None, None, :, :, None]
    return u * mixed


def stick_breaking_attention(q, k, v):
    s_len = q.shape[2]
    scale = 1.0 / math.sqrt(SB_HEAD_DIM)
    outs = []
    for blk in range(s_len // Q_BLOCK):
        q0 = blk * Q_BLOCK
        k_end = q0 + Q_BLOCK
        qb = q[:, :, q0:k_end]
        kb = k[:, :, :k_end]
        vb = v[:, :, :k_end]
        z = jnp.einsum('bhtd,bhsd->bhts', qb, kb, preferred_element_type=jnp.float32) * scale
        t_idx = q0 + jnp.arange(Q_BLOCK)[:, None]
        s_idx = jnp.arange(k_end)[None, :]
        causal = s_idx < t_idx
        log_beta = jax.nn.log_sigmoid(z)
        log_1m = jnp.where(causal, jax.nn.log_sigmoid(-z), 0.0)
        cum = jnp.cumsum(log_1m, axis=-1)
        total = cum[..., -1:]
        log_a = log_beta + (total - cum)
        a = jnp.where(causal, jnp.exp(log_a), 0.0)
        outs.append(jnp.einsum('bhts,bhsd->bhtd', a.astype(vb.dtype), vb))
    return jnp.concatenate(outs, axis=2)


def mixer(xn, w_in, gm_v_norm_g, w_spatial, b_spatial, head_norm_g, w_out):
    b, s, _ = xn.shape
    proj = xn @ w_in
    u = jax.nn.gelu(proj[..., :GM_WIDTH])
    gv = jax.nn.gelu(proj[..., GM_WIDTH:2 * GM_WIDTH])
    qkv = proj[..., 2 * GM_WIDTH:].reshape(b, s, 3, SB_HEADS, SB_HEAD_DIM)
    q = qkv[:, :, 0].transpose(0, 2, 1, 3)
    k = qkv[:, :, 1].transpose(0, 2, 1, 3)
    v = qkv[:, :, 2].transpose(0, 2, 1, 3)
    a_out = head_rms(spatial_gating(u, gv, gm_v_norm_g, w_spatial, b_spatial)).reshape(b, s, GM_WIDTH)
    sb = stick_breaking_attention(q, k, v).transpose(0, 2, 1, 3)
    b_out = head_rms(sb).reshape(b, s, SB_WIDTH)
    merged = jnp.concatenate([a_out, b_out], axis=-1) * head_norm_g
    return merged @ w_out


def cross_attention(hn, mem_n, w_cq, w_ckv, w_co):
    b, s, _ = hn.shape
    m = mem_n.shape[1]
    q = (hn @ w_cq).reshape(b, s, X_HEADS, X_HEAD_DIM)
    kv = (mem_n @ w_ckv).reshape(b, m, 2, X_HEADS, X_HEAD_DIM)
    k, v = kv[:, :, 0], kv[:, :, 1]
    scores = jnp.einsum('bshd,bmhd->bhsm', q, k, preferred_element_type=jnp.float32) / math.sqrt(X_HEAD_DIM)
    p = jax.nn.softmax(scores, axis=-1)
    o = jnp.einsum('bhsm,bmhd->bshd', p.astype(v.dtype), v).reshape(b, s, D_MODEL)
    return o @ w_co


def sq_relu_mlp(hn, w_ff1, w_ff2):
    h = jax.nn.relu(hn @ w_ff1)
    return (h * h) @ w_ff2


def _fwd_setup_inputs(seed: int = 0) -> dict:
    key = jax.random.key(seed)
    ks = jax.random.split(key, 20)
    f32 = jnp.float32

    def nrm(k, shape, scale):
        return jax.random.normal(k, shape, f32) * scale

    def gain(k, shape):
        return 1.0 + 0.02 * jax.random.normal(k, shape, f32)

    L = DEPTH
    return {
        "x": jax.random.normal(ks[0], (BATCH, SEQ, D_MODEL), f32),
        "mem": jax.random.normal(ks[1], (BATCH, N_MEM, D_MODEL), f32),
        "norm_mix_g": gain(ks[2], (L, D_MODEL)),
        "w_in": nrm(ks[3], (L, D_MODEL, IN_COLS), D_MODEL ** -0.5),
        "gm_v_norm_g": gain(ks[4], (L, GM_WIDTH)),
        "w_spatial": nrm(ks[5], (L, GM_GROUPS, CHUNK, CHUNK), 0.5 * CHUNK ** -0.5),
        "b_spatial": 1.0 + 0.01 * jax.random.normal(ks[6], (L, GM_GROUPS, CHUNK), f32),
        "head_norm_g": gain(ks[7], (L, D_MIX)),
        "w_out": nrm(ks[8], (L, D_MIX, D_MODEL), D_MIX ** -0.5),
        "norm_cross_g": gain(ks[9], (L, D_MODEL)),
        "norm_mem_g": gain(ks[10], (L, D_MODEL)),
        "w_cq": nrm(ks[11], (L, D_MODEL, D_MODEL), D_MODEL ** -0.5),
        "w_ckv": nrm(ks[12], (L, D_MODEL, 2 * D_MODEL), D_MODEL ** -0.5),
        "w_co": nrm(ks[13], (L, D_MODEL, D_MODEL), D_MODEL ** -0.5),
        "norm_ffn_g": gain(ks[14], (L, D_MODEL)),
        "w_ff1": nrm(ks[15], (L, D_MODEL, D_FF), D_MODEL ** -0.5),
        "w_ff2": nrm(ks[16], (L, D_FF, D_MODEL), D_FF ** -0.5),
        "norm_final_g": gain(ks[17], (D_MODEL,)),
    }


def _fwd_reference(x, mem, norm_mix_g, w_in, gm_v_norm_g, w_spatial, b_spatial, head_norm_g, w_out,
              norm_cross_g, norm_mem_g, w_cq, w_ckv, w_co, norm_ffn_g, w_ff1, w_ff2, norm_final_g):
    h = x
    for l in range(DEPTH):
        xn = rms_norm(h, norm_mix_g[l])
        h = h + mixer(xn, w_in[l], gm_v_norm_g[l], w_spatial[l], b_spatial[l], head_norm_g[l], w_out[l])
        hn = rms_norm(h, norm_cross_g[l])
        mem_n = rms_norm(mem, norm_mem_g[l])
        h = h + cross_attention(hn, mem_n, w_cq[l], w_ckv[l], w_co[l])
        hn = rms_norm(h, norm_ffn_g[l])
        h = h + sq_relu_mlp(hn, w_ff1[l], w_ff2[l])
    return rms_norm(h, norm_final_g)


import jax as _jax
import jax.numpy as _jnp

TWIN_FORMAT = 'train_step'
FWD_PARAMS = ['x', 'mem', 'norm_mix_g', 'w_in', 'gm_v_norm_g', 'w_spatial', 'b_spatial', 'head_norm_g', 'w_out', 'norm_cross_g', 'norm_mem_g', 'w_cq', 'w_ckv', 'w_co', 'norm_ffn_g', 'w_ff1', 'w_ff2', 'norm_final_g']
TWIN_WEIGHTS = ['norm_mix_g', 'w_in', 'gm_v_norm_g', 'w_spatial', 'b_spatial', 'head_norm_g', 'w_out', 'norm_cross_g', 'norm_mem_g', 'w_cq', 'w_ckv', 'w_co', 'norm_ffn_g', 'w_ff1', 'w_ff2', 'norm_final_g']
TWIN_DIFF_INPUT = 'x'
TWIN_INPUTS = ['x', 'mem', 'norm_mix_g', 'w_in', 'gm_v_norm_g', 'w_spatial', 'b_spatial', 'head_norm_g', 'w_out', 'norm_cross_g', 'norm_mem_g', 'w_cq', 'w_ckv', 'w_co', 'norm_ffn_g', 'w_ff1', 'w_ff2', 'norm_final_g', 'loss_target', 'm_norm_mix_g', 'm_w_in', 'm_gm_v_norm_g', 'm_w_spatial', 'm_b_spatial', 'm_head_norm_g', 'm_w_out', 'm_norm_cross_g', 'm_norm_mem_g', 'm_w_cq', 'm_w_ckv', 'm_w_co', 'm_norm_ffn_g', 'm_w_ff1', 'm_w_ff2', 'm_norm_final_g', 'v_norm_mix_g', 'v_w_in', 'v_gm_v_norm_g', 'v_w_spatial', 'v_b_spatial', 'v_head_norm_g', 'v_w_out', 'v_norm_cross_g', 'v_norm_mem_g', 'v_w_cq', 'v_w_ckv', 'v_w_co', 'v_norm_ffn_g', 'v_w_ff1', 'v_w_ff2', 'v_norm_final_g']
TWIN_OUTPUTS = ['loss', 'grad_x', 'grad_norm_mix_g', 'grad_w_in', 'grad_gm_v_norm_g', 'grad_w_spatial', 'grad_b_spatial', 'grad_head_norm_g', 'grad_w_out', 'grad_norm_cross_g', 'grad_norm_mem_g', 'grad_w_cq', 'grad_w_ckv', 'grad_w_co', 'grad_norm_ffn_g', 'grad_w_ff1', 'grad_w_ff2', 'grad_norm_final_g', 'delta_norm_mix_g', 'delta_w_in', 'delta_gm_v_norm_g', 'delta_w_spatial', 'delta_b_spatial', 'delta_head_norm_g', 'delta_w_out', 'delta_norm_cross_g', 'delta_norm_mem_g', 'delta_w_cq', 'delta_w_ckv', 'delta_w_co', 'delta_norm_ffn_g', 'delta_w_ff1', 'delta_w_ff2', 'delta_norm_final_g', 'new_m_norm_mix_g', 'new_m_w_in', 'new_m_gm_v_norm_g', 'new_m_w_spatial', 'new_m_b_spatial', 'new_m_head_norm_g', 'new_m_w_out', 'new_m_norm_cross_g', 'new_m_norm_mem_g', 'new_m_w_cq', 'new_m_w_ckv', 'new_m_w_co', 'new_m_norm_ffn_g', 'new_m_w_ff1', 'new_m_w_ff2', 'new_m_norm_final_g', 'new_v_norm_mix_g', 'new_v_w_in', 'new_v_gm_v_norm_g', 'new_v_w_spatial', 'new_v_b_spatial', 'new_v_head_norm_g', 'new_v_w_out', 'new_v_norm_cross_g', 'new_v_norm_mem_g', 'new_v_w_cq', 'new_v_w_ckv', 'new_v_w_co', 'new_v_norm_ffn_g', 'new_v_w_ff1', 'new_v_w_ff2', 'new_v_norm_final_g']
TWIN_LEAF_KINDS = {'loss': 'loss', 'grad_x': 'grad_x', 'grad_norm_mix_g': 'grad_w', 'grad_w_in': 'grad_w', 'grad_gm_v_norm_g': 'grad_w', 'grad_w_spatial': 'grad_w', 'grad_b_spatial': 'grad_w', 'grad_head_norm_g': 'grad_w', 'grad_w_out': 'grad_w', 'grad_norm_cross_g': 'grad_w', 'grad_norm_mem_g': 'grad_w', 'grad_w_cq': 'grad_w', 'grad_w_ckv': 'grad_w', 'grad_w_co': 'grad_w', 'grad_norm_ffn_g': 'grad_w', 'grad_w_ff1': 'grad_w', 'grad_w_ff2': 'grad_w', 'grad_norm_final_g': 'grad_w', 'delta_norm_mix_g': 'delta_w', 'delta_w_in': 'delta_w', 'delta_gm_v_norm_g': 'delta_w', 'delta_w_spatial': 'delta_w', 'delta_b_spatial': 'delta_w', 'delta_head_norm_g': 'delta_w', 'delta_w_out': 'delta_w', 'delta_norm_cross_g': 'delta_w', 'delta_norm_mem_g': 'delta_w', 'delta_w_cq': 'delta_w', 'delta_w_ckv': 'delta_w', 'delta_w_co': 'delta_w', 'delta_norm_ffn_g': 'delta_w', 'delta_w_ff1': 'delta_w', 'delta_w_ff2': 'delta_w', 'delta_norm_final_g': 'delta_w', 'new_m_norm_mix_g': 'new_m', 'new_m_w_in': 'new_m', 'new_m_gm_v_norm_g': 'new_m', 'new_m_w_spatial': 'new_m', 'new_m_b_spatial': 'new_m', 'new_m_head_norm_g': 'new_m', 'new_m_w_out': 'new_m', 'new_m_norm_cross_g': 'new_m', 'new_m_norm_mem_g': 'new_m', 'new_m_w_cq': 'new_m', 'new_m_w_ckv': 'new_m', 'new_m_w_co': 'new_m', 'new_m_norm_ffn_g': 'new_m', 'new_m_w_ff1': 'new_m', 'new_m_w_ff2': 'new_m', 'new_m_norm_final_g': 'new_m', 'new_v_norm_mix_g': 'new_v', 'new_v_w_in': 'new_v', 'new_v_gm_v_norm_g': 'new_v', 'new_v_w_spatial': 'new_v', 'new_v_b_spatial': 'new_v', 'new_v_head_norm_g': 'new_v', 'new_v_w_out': 'new_v', 'new_v_norm_cross_g': 'new_v', 'new_v_norm_mem_g': 'new_v', 'new_v_w_cq': 'new_v', 'new_v_w_ckv': 'new_v', 'new_v_w_co': 'new_v', 'new_v_norm_ffn_g': 'new_v', 'new_v_w_ff1': 'new_v', 'new_v_w_ff2': 'new_v', 'new_v_norm_final_g': 'new_v'}


def _forward(args):
    return _fwd_reference(*[args[k] for k in FWD_PARAMS])


def _output_shape():
    out = _jax.eval_shape(lambda: _forward(_fwd_setup_inputs(0)))
    return out.shape, out.dtype

N_MICROBATCH = 1
ADAM_LR = 0.001
ADAM_B1 = 0.9
ADAM_B2 = 0.999
ADAM_EPS = 1e-08
ADAM_WD = 0.01
ADAM_STEP = 10
PER_EXAMPLE_BATCH_AXIS = {'x': 0, 'mem': 0, 'loss_target': 0}
SHARED_INPUTS = []
_WEIGHT_DTYPES = {'norm_mix_g': _jnp.float32, 'w_in': _jnp.float32, 'gm_v_norm_g': _jnp.float32, 'w_spatial': _jnp.float32, 'b_spatial': _jnp.float32, 'head_norm_g': _jnp.float32, 'w_out': _jnp.float32, 'norm_cross_g': _jnp.float32, 'norm_mem_g': _jnp.float32, 'w_cq': _jnp.float32, 'w_ckv': _jnp.float32, 'w_co': _jnp.float32, 'norm_ffn_g': _jnp.float32, 'w_ff1': _jnp.float32, 'w_ff2': _jnp.float32, 'norm_final_g': _jnp.float32}
MOMENT_SCALE = {'norm_mix_g': 2.274243e-01, 'w_in': 1.388109e-01, 'gm_v_norm_g': 6.817023e-02, 'w_spatial': 1.213475e-01, 'b_spatial': 6.509032e-02, 'head_norm_g': 1.895012e-01, 'w_out': 1.973228e-01, 'norm_cross_g': 2.071194e-02, 'norm_mem_g': 3.033148e-02, 'w_cq': 1.972453e-02, 'w_ckv': 2.035658e-02, 'w_co': 2.087570e-02, 'norm_ffn_g': 2.126073e-01, 'w_ff1': 9.566438e-02, 'w_ff2': 1.917267e-01, 'norm_final_g': 6.459111e+01}


def _to_microbatches(a, axis):
    t = _jnp.moveaxis(a, axis, 0)
    t = t.reshape((N_MICROBATCH, t.shape[0] // N_MICROBATCH) + t.shape[1:])
    return _jnp.moveaxis(t, 1, axis + 1)


def setup_inputs(seed: int = 0) -> dict:
    inp = _fwd_setup_inputs(seed)
    key = _jax.random.fold_in(_jax.random.key(seed), 7919)
    shape, _ = _output_shape()
    out = dict(inp)
    out["loss_target"] = _jax.random.normal(_jax.random.fold_in(key, 0), shape, _jnp.float32)
    for i, name in enumerate(TWIN_WEIGHTS):
        w = inp[name].astype(_jnp.float32)
        if MOMENT_SCALE is None:
            s = _jnp.sqrt(_jnp.mean(_jnp.square(w)) + 1e-30)
        else:
            s = MOMENT_SCALE[name]
        km, kv = _jax.random.split(_jax.random.fold_in(key, i + 1))
        out[name] = w
        out["m_" + name] = s * _jax.random.normal(km, w.shape, _jnp.float32)
        out["v_" + name] = (s * s) * _jax.random.uniform(kv, w.shape, _jnp.float32, 0.5, 1.5)
    if N_MICROBATCH > 1:
        for name, axis in PER_EXAMPLE_BATCH_AXIS.items():
            out[name] = _to_microbatches(out[name], axis)
    return {'x': out['x'], 'mem': out['mem'], 'norm_mix_g': out['norm_mix_g'], 'w_in': out['w_in'], 'gm_v_norm_g': out['gm_v_norm_g'], 'w_spatial': out['w_spatial'], 'b_spatial': out['b_spatial'], 'head_norm_g': out['head_norm_g'], 'w_out': out['w_out'], 'norm_cross_g': out['norm_cross_g'], 'norm_mem_g': out['norm_mem_g'], 'w_cq': out['w_cq'], 'w_ckv': out['w_ckv'], 'w_co': out['w_co'], 'norm_ffn_g': out['norm_ffn_g'], 'w_ff1': out['w_ff1'], 'w_ff2': out['w_ff2'], 'norm_final_g': out['norm_final_g'], 'loss_target': out['loss_target'], 'm_norm_mix_g': out['m_norm_mix_g'], 'm_w_in': out['m_w_in'], 'm_gm_v_norm_g': out['m_gm_v_norm_g'], 'm_w_spatial': out['m_w_spatial'], 'm_b_spatial': out['m_b_spatial'], 'm_head_norm_g': out['m_head_norm_g'], 'm_w_out': out['m_w_out'], 'm_norm_cross_g': out['m_norm_cross_g'], 'm_norm_mem_g': out['m_norm_mem_g'], 'm_w_cq': out['m_w_cq'], 'm_w_ckv': out['m_w_ckv'], 'm_w_co': out['m_w_co'], 'm_norm_ffn_g': out['m_norm_ffn_g'], 'm_w_ff1': out['m_w_ff1'], 'm_w_ff2': out['m_w_ff2'], 'm_norm_final_g': out['m_norm_final_g'], 'v_norm_mix_g': out['v_norm_mix_g'], 'v_w_in': out['v_w_in'], 'v_gm_v_norm_g': out['v_gm_v_norm_g'], 'v_w_spatial': out['v_w_spatial'], 'v_b_spatial': out['v_b_spatial'], 'v_head_norm_g': out['v_head_norm_g'], 'v_w_out': out['v_w_out'], 'v_norm_cross_g': out['v_norm_cross_g'], 'v_norm_mem_g': out['v_norm_mem_g'], 'v_w_cq': out['v_w_cq'], 'v_w_ckv': out['v_w_ckv'], 'v_w_co': out['v_w_co'], 'v_norm_ffn_g': out['v_norm_ffn_g'], 'v_w_ff1': out['v_w_ff1'], 'v_w_ff2': out['v_w_ff2'], 'v_norm_final_g': out['v_norm_final_g']}


def _loss(weights, diff, rest, loss_target):
    with _jax.named_scope("forward"):
        args = {**rest, TWIN_DIFF_INPUT: diff, **{k: w.astype(_WEIGHT_DTYPES[k]) for k, w in weights.items()}}
        y = _forward(args)
    with _jax.named_scope("loss_head"):
        err = _jnp.square(y.astype(_jnp.float32) - loss_target)
        return 0.5 * _jnp.sum(_jnp.mean(err, axis=-1)) if err.ndim else 0.5 * err


def _adamw(w, g, m, v):
    m = ADAM_B1 * m + (1.0 - ADAM_B1) * g
    v = ADAM_B2 * v + (1.0 - ADAM_B2) * _jnp.square(g)
    m_hat = m / (1.0 - ADAM_B1 ** ADAM_STEP)
    v_hat = v / (1.0 - ADAM_B2 ** ADAM_STEP)
    delta = -ADAM_LR * (m_hat / (_jnp.sqrt(v_hat) + ADAM_EPS) + ADAM_WD * w)
    return delta, m, v


def reference(x, mem, norm_mix_g, w_in, gm_v_norm_g, w_spatial, b_spatial, head_norm_g, w_out, norm_cross_g, norm_mem_g, w_cq, w_ckv, w_co, norm_ffn_g, w_ff1, w_ff2, norm_final_g, loss_target, m_norm_mix_g, m_w_in, m_gm_v_norm_g, m_w_spatial, m_b_spatial, m_head_norm_g, m_w_out, m_norm_cross_g, m_norm_mem_g, m_w_cq, m_w_ckv, m_w_co, m_norm_ffn_g, m_w_ff1, m_w_ff2, m_norm_final_g, v_norm_mix_g, v_w_in, v_gm_v_norm_g, v_w_spatial, v_b_spatial, v_head_norm_g, v_w_out, v_norm_cross_g, v_norm_mem_g, v_w_cq, v_w_ckv, v_w_co, v_norm_ffn_g, v_w_ff1, v_w_ff2, v_norm_final_g):
    given = dict(x=x, mem=mem, norm_mix_g=norm_mix_g, w_in=w_in, gm_v_norm_g=gm_v_norm_g, w_spatial=w_spatial, b_spatial=b_spatial, head_norm_g=head_norm_g, w_out=w_out, norm_cross_g=norm_cross_g, norm_mem_g=norm_mem_g, w_cq=w_cq, w_ckv=w_ckv, w_co=w_co, norm_ffn_g=norm_ffn_g, w_ff1=w_ff1, w_ff2=w_ff2, norm_final_g=norm_final_g, loss_target=loss_target, m_norm_mix_g=m_norm_mix_g, m_w_in=m_w_in, m_gm_v_norm_g=m_gm_v_norm_g, m_w_spatial=m_w_spatial, m_b_spatial=m_b_spatial, m_head_norm_g=m_head_norm_g, m_w_out=m_w_out, m_norm_cross_g=m_norm_cross_g, m_norm_mem_g=m_norm_mem_g, m_w_cq=m_w_cq, m_w_ckv=m_w_ckv, m_w_co=m_w_co, m_norm_ffn_g=m_norm_ffn_g, m_w_ff1=m_w_ff1, m_w_ff2=m_w_ff2, m_norm_final_g=m_norm_final_g, v_norm_mix_g=v_norm_mix_g, v_w_in=v_w_in, v_gm_v_norm_g=v_gm_v_norm_g, v_w_spatial=v_w_spatial, v_b_spatial=v_b_spatial, v_head_norm_g=v_head_norm_g, v_w_out=v_w_out, v_norm_cross_g=v_norm_cross_g, v_norm_mem_g=v_norm_mem_g, v_w_cq=v_w_cq, v_w_ckv=v_w_ckv, v_w_co=v_w_co, v_norm_ffn_g=v_norm_ffn_g, v_w_ff1=v_w_ff1, v_w_ff2=v_w_ff2, v_norm_final_g=v_norm_final_g)
    weights = {n: given[n] for n in TWIN_WEIGHTS}
    shared = {n: given[n] for n in SHARED_INPUTS}
    per_example = {n: given[n] for n in ['x', 'mem']}
    grad_fn = _jax.value_and_grad(_loss, argnums=(0, 1))

    def one_microbatch(ex, loss_target):
        ex = dict(ex)
        diff = ex.pop(TWIN_DIFF_INPUT)
        return grad_fn(weights, diff, {**shared, **ex}, loss_target)

    if N_MICROBATCH == 1:
        loss, (grad_w, grad_x) = one_microbatch(per_example, given["loss_target"])
    else:
        def body(carry, xs):
            loss_sum, grad_sum = carry
            l_k, (gw_k, gx_k) = one_microbatch(xs[0], xs[1])
            with _jax.named_scope("update"):
                return (loss_sum + l_k, _jax.tree.map(_jnp.add, grad_sum, gw_k)), gx_k

        init = (_jnp.zeros((), _jnp.float32), _jax.tree.map(_jnp.zeros_like, weights))
        (loss, grad_w), grad_x = _jax.lax.scan(body, init, (per_example, given["loss_target"]))
    with _jax.named_scope("update"):
        delta_w, new_m, new_v = {}, {}, {}
        for n in TWIN_WEIGHTS:
            delta_w[n], new_m[n], new_v[n] = _adamw(weights[n], grad_w[n], given["m_" + n], given["v_" + n])
    return (loss, grad_x, *[grad_w[n] for n in TWIN_WEIGHTS], *[delta_w[n] for n in TWIN_WEIGHTS],
            *[new_m[n] for n in TWIN_WEIGHTS], *[new_v[n] for n in TWIN_WEIGHTS])
```

```python
import functools
import math

import jax
import jax.numpy as jnp
from jax import lax
from jax.experimental import pallas as pl
from jax.experimental.pallas import tpu as pltpu

F32 = jnp.float32
BF16 = jnp.bfloat16
EPS = 1e-6
N_DEV = 8
LANES = 128
CHUNK = 128
GM_GROUPS = 4
GM_WIDTH = 512
SB_PAIRS = 4
SB_HEAD_DIM = 64
SB_SCALE = 0.125
SB_TILE = 128
X_HEADS = 4
X_HEAD_DIM = 256
X_SCALE = 1.0 / 16.0
VMEM_LIMIT = 56 * 1024 * 1024
ADAM_LR, ADAM_B1, ADAM_B2, ADAM_EPS, ADAM_WD, ADAM_STEP = 0.001, 0.9, 0.999, 1e-08, 0.01, 10
MESH = pl.DeviceIdType.MESH


def _params(n_axes):
    return pltpu.CompilerParams(dimension_semantics=("arbitrary",) * n_axes, vmem_limit_bytes=VMEM_LIMIT)


def _dot(a, b, dims):
    return lax.dot_general(a, b, (dims, ((), ())), preferred_element_type=F32)


def _nn(a, b):
    return _dot(a, b, ((1,), (0,)))


def _nt(a, b):
    return _dot(a, b, ((1,), (1,)))


def _tn(a, b):
    return _dot(a, b, ((0,), (0,)))


_MODES = {"nn": _nn, "nt": _nt, "tn": _tn}


def _rstd(x):
    return lax.rsqrt(jnp.mean(x * x, axis=-1, keepdims=True) + EPS)


def _gelu(x):
    c = math.sqrt(2.0 / math.pi)
    t = jnp.tanh(c * (x + 0.044715 * x * x * x))
    return 0.5 * x * (1.0 + t)


def _gelu_grad(x):
    c = math.sqrt(2.0 / math.pi)
    t = jnp.tanh(c * (x + 0.044715 * x * x * x))
    return 0.5 * (1.0 + t) + 0.5 * x * (1.0 - t * t) * c * (1.0 + 3 * 0.044715 * x * x)


def _split_bf16(x):
    hi = x.astype(BF16)
    lo = (x - hi.astype(F32)).astype(BF16)
    return hi, lo


def _mm(a, b, *, mode, out_dtype, name, tm=1024, tn=1024, tk=1024, a_fn=None, epi=None, epi_ins=(), vec_ins=(), aux=False):
    if mode == "nn":
        (m, k), (k2, n) = a.shape, b.shape
    elif mode == "nt":
        (m, k), (n, k2) = a.shape, b.shape
    else:
        (k, m), (k2, n) = a.shape, b.shape
    assert k == k2, (a.shape, b.shape, mode)
    tm, tn, tk = min(tm, m), min(tn, n), min(tk, k)
    assert m % tm == 0 and n % tn == 0 and k % tk == 0, (m, n, k, tm, tn, tk)
    n_m, n_n, n_k = m // tm, n // tn, k // tk
    assert not aux or n_n == 1
    dot = _MODES[mode]
    n_epi, n_vec = len(epi_ins), len(vec_ins)

    def body(*refs):
        a_ref, b_ref = refs[:2]
        epi_refs = refs[2:2 + n_epi + n_vec]
        outs = refs[2 + n_epi + n_vec:]
        o_ref = outs[0]
        aux_ref = outs[1] if aux else None
        acc_ref = outs[-1] if n_k > 1 else None
        i, kk = pl.program_id(0), pl.program_id(2)
        av = a_ref[...]
        if a_fn is not None:
            av = a_fn(av)
        part = dot(av.astype(BF16), b_ref[...].astype(BF16))

        def finish(acc):
            if epi is None:
                o_ref[...] = acc.astype(out_dtype)
                return
            res = epi(acc, *[r[...] for r in epi_refs])
            if aux:
                res, row = res

                @pl.when(i == 0)
                def _():
                    aux_ref[...] = row

                @pl.when(i != 0)
                def _():
                    aux_ref[...] += row
            o_ref[...] = res.astype(out_dtype)

        if n_k == 1:
            finish(part)
        else:
            @pl.when(kk == 0)
            def _():
                acc_ref[...] = part

            @pl.when(kk != 0)
            def _():
                acc_ref[...] += part

            @pl.when(kk == n_k - 1)
            def _():
                finish(acc_ref[...])

    if mode == "tn":
        a_spec = pl.BlockSpec((tk, tm), lambda i, j, kk: (kk, i))
    else:
        a_spec = pl.BlockSpec((tm, tk), lambda i, j, kk: (i, kk))
    if mode == "nt":
        b_spec = pl.BlockSpec((tn, tk), lambda i, j, kk: (j, kk))
    else:
        b_spec = pl.BlockSpec((tk, tn), lambda i, j, kk: (kk, j))
    tile_spec = pl.BlockSpec((tm, tn), lambda i, j, kk: (i, j))
    row_spec = pl.BlockSpec((1, tn), lambda i, j, kk: (0, j))
    out_shape = [jax.ShapeDtypeStruct((m, n), out_dtype)]
    out_specs = [tile_spec]
    if aux:
        out_shape.append(jax.ShapeDtypeStruct((1, n), F32))
        out_specs.append(row_spec)
    res = pl.pallas_call(
        body, name=name, grid=(n_m, n_n, n_k),
        in_specs=[a_spec, b_spec] + [tile_spec] * n_epi + [row_spec] * n_vec,
        out_specs=out_specs, out_shape=out_shape,
        scratch_shapes=[pltpu.VMEM((tm, tn), F32)] if n_k > 1 else [],
        compiler_params=_params(3),
    )(a, b, *epi_ins, *vec_ins)
    return res if aux else res[0]


def _norm_mm(x, g, w, *, mode, name, tm=1024, tn=1024):
    m, d = x.shape
    n = w.shape[0] if mode == "nt" else w.shape[1]
    tm, tn = min(tm, m), min(tn, n)
    assert m % tm == 0 and n % tn == 0
    dot = _MODES[mode]

    def body(x_ref, g_ref, w_ref, o_ref, xn_ref, xn_s):
        @pl.when(pl.program_id(1) == 0)
        def _():
            xv = x_ref[...]
            xn = (xv * _rstd(xv) * g_ref[...]).astype(BF16)
            xn_s[...] = xn
            xn_ref[...] = xn

        o_ref[...] = dot(xn_s[...], w_ref[...]).astype(BF16)

    w_spec = pl.BlockSpec((tn, d), lambda i, j: (j, 0)) if mode == "nt" else pl.BlockSpec((d, tn), lambda i, j: (0, j))
    return pl.pallas_call(
        body, name=name, grid=(m // tm, n // tn),
        in_specs=[pl.BlockSpec((tm, d), lambda i, j: (i, 0)), pl.BlockSpec((1, d), lambda i, j: (0, 0)), w_spec],
        out_specs=[pl.BlockSpec((tm, tn), lambda i, j: (i, j)), pl.BlockSpec((tm, d), lambda i, j: (i, 0))],
        out_shape=[jax.ShapeDtypeStruct((m, n), BF16), jax.ShapeDtypeStruct((m, d), BF16)],
        scratch_shapes=[pltpu.VMEM((tm, d), BF16)],
        compiler_params=_params(2),
    )(x, g, w)


def _epi_residual(acc, res):
    return res + acc


def _epi_relu2_grad(acc, pre):
    return acc * (2.0 * jnp.maximum(pre.astype(F32), 0.0))


def _relu2(pre):
    r = jnp.maximum(pre.astype(F32), 0.0)
    return r * r


def _epi_rms_bwd(acc, h, dres, g):
    r = _rstd(h)
    xh = h * r
    dxh = acc * g
    dh = dres + r * (dxh - xh * jnp.mean(dxh * xh, axis=-1, keepdims=True))
    return dh, jnp.sum(acc * xh, axis=0, keepdims=True)


def _epi_rms_gain_only(acc, h, g):
    return acc, jnp.sum(acc * (h * _rstd(h)), axis=0, keepdims=True)


def _tril(n):
    row = lax.broadcasted_iota(jnp.int32, (n, n), 0)
    col = lax.broadcasted_iota(jnp.int32, (n, n), 1)
    return col <= row


def _gmlp_fwd(proj, w_sp, b_sp_t, gv, hg, *, rows=512):
    t = proj.shape[0]
    rows = min(rows, t)
    n_c = rows // CHUNK

    def body(u_ref, v_ref, w_ref, bt_ref, gv_ref, hg_ref, m_ref):
        keep = _tril(CHUNK)
        for g in range(GM_GROUPS):
            cols = slice(g * LANES, (g + 1) * LANES)
            wg = jnp.where(keep, w_ref[g], 0.0).astype(BF16)
            u = _gelu(u_ref[:, cols].astype(F32))
            v = _gelu(v_ref[:, cols].astype(F32))
            vn = (v * _rstd(v) * gv_ref[:, cols]).astype(BF16)
            bias = bt_ref[:, g:g + 1]
            for c in range(n_c):
                rs = slice(c * CHUNK, (c + 1) * CHUNK)
                mixed = _nn(wg, vn[rs]) + bias
                a = u[rs] * mixed
                m_ref[rs, cols] = (a * _rstd(a) * hg_ref[:, cols]).astype(BF16)

    full = lambda shape: pl.BlockSpec(shape, lambda i: (0,) * len(shape))
    return pl.pallas_call(
        body, name="gmlp_fwd", grid=(t // rows,),
        in_specs=[pl.BlockSpec((rows, GM_WIDTH), lambda i: (i, 0)), pl.BlockSpec((rows, GM_WIDTH), lambda i: (i, 1)),
                  full((GM_GROUPS, CHUNK, CHUNK)), full((CHUNK, GM_GROUPS)), full((1, GM_WIDTH)), full((1, GM_WIDTH))],
        out_specs=pl.BlockSpec((rows, GM_WIDTH), lambda i: (i, 0)),
        out_shape=jax.ShapeDtypeStruct((t, 2 * GM_WIDTH), BF16),
        compiler_params=_params(1),
    )(proj, proj, w_sp, b_sp_t, gv, hg)


def _gmlp_bwd(proj, dmerged, w_sp, b_sp_t, gv, hg, *, rows=512):
    t = proj.shape[0]
    rows = min(rows, t)
    n_c = rows // CHUNK
    n_steps = t // rows

    def body(u_ref, v_ref, dm_ref, w_ref, bt_ref, gv_ref, hg_ref, dp_ref, dw_ref, dbt_ref, dgv_ref, dhg_ref, db_acc):
        step = pl.program_id(0)
        keep = _tril(CHUNK)

        @pl.when(step == 0)
        def _():
            dw_ref[...] = jnp.zeros_like(dw_ref)
            db_acc[...] = jnp.zeros_like(db_acc)
            dgv_ref[...] = jnp.zeros_like(dgv_ref)
            dhg_ref[...] = jnp.zeros_like(dhg_ref)

        for g in range(GM_GROUPS):
            cols = slice(g * LANES, (g + 1) * LANES)
            wg = jnp.where(keep, w_ref[g], 0.0).astype(BF16)
            u_pre = u_ref[:, cols].astype(F32)
            v_pre = v_ref[:, cols].astype(F32)
            u = _gelu(u_pre)
            v = _gelu(v_pre)
            r = _rstd(v)
            xh = v * r
            gvg = gv_ref[:, cols]
            hgg = hg_ref[:, cols]
            vn = (xh * gvg).astype(BF16)
            bias = bt_ref[:, g:g + 1]
            dm = dm_ref[:, cols].astype(F32)
            du_parts, dvn_parts = [], []
            dw = jnp.zeros((CHUNK, CHUNK), F32)
            db = jnp.zeros((CHUNK, LANES), F32)
            dhg = jnp.zeros((1, LANES), F32)
            for c in range(n_c):
                rs = slice(c * CHUNK, (c + 1) * CHUNK)
                mixed = _nn(wg, vn[rs]) + bias
                a = u[rs] * mixed
                ra = _rstd(a)
                an = a * ra
                dhg = dhg + jnp.sum(dm[rs] * an, axis=0, keepdims=True)
                dan = dm[rs] * hgg
                da = ra * (dan - an * jnp.mean(dan * an, axis=-1, keepdims=True))
                du_parts.append(da * mixed)
                dmixed = da * u[rs]
                db = db + dmixed
                dmb = dmixed.astype(BF16)
                dw = dw + _nt(dmb, vn[rs])
                dvn_parts.append(_tn(wg, dmb))
            du = jnp.concatenate(du_parts, axis=0)
            dvn = jnp.concatenate(dvn_parts, axis=0)
            dw_ref[g] += dw
            db_acc[g] += db
            dhg_ref[:, cols] += dhg
            dgv_ref[:, cols] += jnp.sum(dvn * xh, axis=0, keepdims=True)
            dxh = dvn * gvg
            dv = r * (dxh - xh * jnp.mean(dxh * xh, axis=-1, keepdims=True))
            dp_ref[:, cols] = (du * _gelu_grad(u_pre)).astype(BF16)
            dp_ref[:, GM_WIDTH + g * LANES:GM_WIDTH + (g + 1) * LANES] = (dv * _gelu_grad(v_pre)).astype(BF16)

        @pl.when(step == n_steps - 1)
        def _():
            for g in range(GM_GROUPS):
                dw_ref[g] = jnp.where(keep, dw_ref[g], 0.0)
                dbt_ref[:, g:g + 1] = jnp.sum(db_acc[g], axis=-1, keepdims=True)

    full = lambda shape: pl.BlockSpec(shape, lambda i: (0,) * len(shape))
    return pl.pallas_call(
        body, name="gmlp_bwd", grid=(n_steps,),
        in_specs=[pl.BlockSpec((rows, GM_WIDTH), lambda i: (i, 0)), pl.BlockSpec((rows, GM_WIDTH), lambda i: (i, 1)),
                  pl.BlockSpec((rows, GM_WIDTH), lambda i: (i, 0)),
                  full((GM_GROUPS, CHUNK, CHUNK)), full((CHUNK, GM_GROUPS)), full((1, GM_WIDTH)), full((1, GM_WIDTH))],
        out_specs=[pl.BlockSpec((rows, 2 * GM_WIDTH), lambda i: (i, 0)), full((GM_GROUPS, CHUNK, CHUNK)),
                   full((CHUNK, GM_GROUPS)), full((1, GM_WIDTH)), full((1, GM_WIDTH))],
        out_shape=[jax.ShapeDtypeStruct((t, 2 * GM_WIDTH), BF16), jax.ShapeDtypeStruct((GM_GROUPS, CHUNK, CHUNK), F32),
                   jax.ShapeDtypeStruct((CHUNK, GM_GROUPS), F32), jax.ShapeDtypeStruct((1, GM_WIDTH), F32),
                   jax.ShapeDtypeStruct((1, GM_WIDTH), F32)],
        scratch_shapes=[pltpu.VMEM((GM_GROUPS, CHUNK, LANES), F32)],
        compiler_params=_params(1),
    )(proj, proj, dmerged, w_sp, b_sp_t, gv, hg)


def _sb_logits(qh, kj, strict):
    z = _nt(qh, kj)
    ls = jnp.minimum(z, 0.0) - jnp.log(1.0 + jnp.exp(-jnp.abs(z)))
    l1m = ls - z
    if strict is not None:
        l1m = jnp.where(strict, l1m, 0.0)
    return ls, l1m


def _sb_weights(ls, l1m, right, strict, upper):
    hi, lo = _split_bf16(l1m)
    a = jnp.exp(ls + (_nn(hi, upper) + _nn(lo, upper)) + right)
    if strict is not None:
        a = jnp.where(strict, a, 0.0)
    return a


def _sb_masks():
    row = lax.broadcasted_iota(jnp.int32, (SB_TILE, SB_TILE), 0)
    col = lax.broadcasted_iota(jnp.int32, (SB_TILE, SB_TILE), 1)
    lane = lax.broadcasted_iota(jnp.int32, (SB_TILE, LANES), 1)
    return row, col, lane < SB_HEAD_DIM


def _head_mean(x, first):
    s0 = jnp.sum(jnp.where(first, x, 0.0), axis=-1, keepdims=True)
    s1 = jnp.sum(jnp.where(first, 0.0, x), axis=-1, keepdims=True)
    return jnp.where(first, s0, s1) * (1.0 / SB_HEAD_DIM)


def _sb_fwd(proj, merged_a, hg, *, batch, seq):
    n_q = seq // SB_TILE
    q0, k0, v0 = 2 * GM_WIDTH // LANES, 2 * GM_WIDTH // LANES + SB_PAIRS, 2 * GM_WIDTH // LANES + 2 * SB_PAIRS

    def body(q_ref, k_ref, v_ref, hg_ref, _, m_ref, raw_ref, tot_ref):
        i = pl.program_id(2)
        row, col, first = _sb_masks()
        upper = (row > col).astype(BF16)
        strict = col < row
        qs = (q_ref[...].astype(F32) * SB_SCALE).astype(BF16)
        q_h = (jnp.where(first, qs, jnp.zeros_like(qs)), jnp.where(first, jnp.zeros_like(qs), qs))

        def tile(j, carry, mask):
            r0, r1, acc = carry
            at = pl.ds(pl.multiple_of(j * SB_TILE, SB_TILE), SB_TILE)
            kj, vj = k_ref[at, :], v_ref[at, :]
            outs, rs = [], []
            for h, r in ((0, r0), (1, r1)):
                ls, l1m = _sb_logits(q_h[h], kj, mask)
                a = _sb_weights(ls, l1m, r, mask, upper)
                outs.append(_nn(a.astype(BF16), vj))
                rs.append(r + jnp.sum(l1m, axis=-1, keepdims=True))
            return rs[0], rs[1], acc + jnp.where(first, outs[0], outs[1])

        zero = jnp.zeros((SB_TILE, 1), F32)
        carry = tile(i, (zero, zero, jnp.zeros((SB_TILE, LANES), F32)), strict)
        carry = lax.fori_loop(0, i, lambda s, c: tile(i - 1 - s, c, None), carry)
        acc = carry[2]
        raw_ref[...] = acc
        tot_ref[0] = jnp.where(first, carry[0], carry[1])
        m_ref[...] = (acc * lax.rsqrt(_head_mean(acc * acc, first) + EPS) * hg_ref[...]).astype(BF16)

    t = batch * seq
    blk = lambda c0: pl.BlockSpec((SB_TILE, LANES), lambda b, p, i: (b * n_q + i, c0 + p))
    kv = lambda c0: pl.BlockSpec((seq, LANES), lambda b, p, i: (b, c0 + p))
    return pl.pallas_call(
        body, name="sb_fwd", grid=(batch, SB_PAIRS, n_q),
        in_specs=[blk(q0), kv(k0), kv(v0), pl.BlockSpec((1, LANES), lambda b, p, i: (0, SB_PAIRS + p)),
                  pl.BlockSpec(memory_space=pl.ANY)],
        out_specs=[blk(SB_PAIRS), blk(0), pl.BlockSpec((1, SB_TILE, LANES), lambda b, p, i: (p, b * n_q + i, 0))],
        out_shape=[jax.ShapeDtypeStruct((t, 2 * GM_WIDTH), BF16), jax.ShapeDtypeStruct((t, GM_WIDTH), F32),
                   jax.ShapeDtypeStruct((SB_PAIRS, t, LANES), F32)],
        input_output_aliases={4: 0},
        compiler_params=_params(3),
    )(proj, proj, proj, hg, merged_a)


def _sb_bwd(proj, raw, tot, dmerged, hg, *, batch, seq):
    n_q = seq // SB_TILE
    q0, k0, v0 = 2 * GM_WIDTH // LANES, 2 * GM_WIDTH // LANES + SB_PAIRS, 2 * GM_WIDTH // LANES + 2 * SB_PAIRS

    def body(q_ref, k_ref, v_ref, raw_ref, tot_ref, dm_ref, hg_ref, dq_ref, dk_ref, dv_ref, dhg_ref, dk_acc, dv_acc):
        b, i = pl.program_id(1), pl.program_id(2)
        row, col, first = _sb_masks()
        upper = (row > col).astype(BF16)
        lower = (row < col).astype(BF16)
        strict = col < row

        @pl.when(jnp.logical_and(b == 0, i == 0))
        def _():
            dhg_ref[...] = jnp.zeros_like(dhg_ref)

        @pl.when(i == 0)
        def _():
            dk_acc[...] = jnp.zeros_like(dk_acc)
            dv_acc[...] = jnp.zeros_like(dv_acc)

        raw_v = raw_ref[...]
        dm = dm_ref[...].astype(F32)
        r = lax.rsqrt(_head_mean(raw_v * raw_v, first) + EPS)
        nrm = raw_v * r
        dhg_ref[...] += jnp.sum(dm * nrm, axis=0, keepdims=True)
        dn = dm * hg_ref[...]
        dout_b = (r * (dn - nrm * _head_mean(dn * nrm, first))).astype(BF16)
        dout_h = (jnp.where(first, dout_b, jnp.zeros_like(dout_b)), jnp.where(first, jnp.zeros_like(dout_b), dout_b))
        qv = q_ref[...]
        qs = (qv.astype(F32) * SB_SCALE).astype(BF16)
        q_h = (jnp.where(first, qv, jnp.zeros_like(qv)), jnp.where(first, jnp.zeros_like(qv), qv))
        qs_h = (jnp.where(first, qs, jnp.zeros_like(qs)), jnp.where(first, jnp.zeros_like(qs), qs))

        tot_v = tot_ref[0]
        tot_h = (tot_v[:, 0:1], tot_v[:, SB_HEAD_DIM:SB_HEAD_DIM + 1])

        def tile(j, carry, mask):
            c0, c1, g0, g1, dq = carry
            at = pl.ds(pl.multiple_of(j * SB_TILE, SB_TILE), SB_TILE)
            kj, vj = k_ref[at, :], v_ref[at, :]
            dqs, cs, gs = [], [], []
            dk_t = jnp.zeros((SB_TILE, LANES), F32)
            dv_t = jnp.zeros((SB_TILE, LANES), F32)
            for h, cc, gc in ((0, c0, g0), (1, c1, g1)):
                ls, l1m = _sb_logits(qs_h[h], kj, mask)
                cc = cc + jnp.sum(l1m, axis=-1, keepdims=True)
                a = _sb_weights(ls, l1m, tot_h[h] - cc, mask, upper)
                gmat = _nt(dout_h[h], vj) * a
                ghi, glo = _split_bf16(gmat)
                prefix = gc + (_nn(ghi, lower) + _nn(glo, lower))
                beta = jnp.exp(ls)
                dz = (gmat * (1.0 - beta) - prefix * beta) * SB_SCALE
                if mask is not None:
                    dz = jnp.where(mask, dz, 0.0)
                dzb = dz.astype(BF16)
                dqs.append(_nn(dzb, kj))
                dk_t = dk_t + _tn(dzb, q_h[h])
                dv_t = dv_t + _tn(a.astype(BF16), dout_h[h])
                cs.append(cc)
                gs.append(gc + jnp.sum(gmat, axis=-1, keepdims=True))
            dk_acc[at, :] += dk_t
            dv_acc[at, :] += dv_t
            return cs[0], cs[1], gs[0], gs[1], dq + jnp.where(first, dqs[0], dqs[1])

        zero = jnp.zeros((SB_TILE, 1), F32)
        carry = (zero, zero, zero, zero, jnp.zeros((SB_TILE, LANES), F32))
        carry = lax.fori_loop(0, i, lambda j, c: tile(j, c, None), carry)
        carry = tile(i, carry, strict)
        dq_ref[...] = carry[4].astype(BF16)

        @pl.when(i == n_q - 1)
        def _():
            dk_ref[...] = dk_acc[...].astype(BF16)
            dv_ref[...] = dv_acc[...].astype(BF16)

    t = batch * seq
    blk = lambda c0: pl.BlockSpec((SB_TILE, LANES), lambda p, b, i: (b * n_q + i, c0 + p))
    kv = lambda c0: pl.BlockSpec((seq, LANES), lambda p, b, i: (b, c0 + p))
    row_spec = pl.BlockSpec((1, LANES), lambda p, b, i: (0, SB_PAIRS + p))
    tot_spec = pl.BlockSpec((1, SB_TILE, LANES), lambda p, b, i: (p, b * n_q + i, 0))
    return pl.pallas_call(
        body, name="sb_bwd", grid=(SB_PAIRS, batch, n_q),
        in_specs=[blk(q0), kv(k0), kv(v0), blk(0), tot_spec, blk(SB_PAIRS), row_spec],
        out_specs=[blk(0), kv(0), kv(0), pl.BlockSpec((1, LANES), lambda p, b, i: (0, p))],
        out_shape=[jax.ShapeDtypeStruct((t, GM_WIDTH), BF16)] * 3 + [jax.ShapeDtypeStruct((1, GM_WIDTH), F32)],
        scratch_shapes=[pltpu.VMEM((seq, LANES), F32), pltpu.VMEM((seq, LANES), F32)],
        compiler_params=_params(3),
    )(proj, proj, proj, raw, tot, dmerged, hg)


def _x_softmax(qh, kh):
    s = _nt(qh, kh) * X_SCALE
    p = jnp.exp(s - jnp.max(s, axis=-1, keepdims=True))
    return p / jnp.sum(p, axis=-1, keepdims=True)


def _xattn_fwd(q, kv, *, batch, seq, n_mem, tq=512):
    tq = min(tq, seq)
    n_q = seq // tq
    d = X_HEADS * X_HEAD_DIM

    def body(q_ref, kv_ref, o_ref):
        for h in range(X_HEADS):
            cols = slice(h * X_HEAD_DIM, (h + 1) * X_HEAD_DIM)
            p = _x_softmax(q_ref[:, cols], kv_ref[:, cols])
            o_ref[:, cols] = _nn(p.astype(BF16), kv_ref[:, d + h * X_HEAD_DIM:d + (h + 1) * X_HEAD_DIM]).astype(BF16)

    return pl.pallas_call(
        body, name="xattn_fwd", grid=(batch, n_q),
        in_specs=[pl.BlockSpec((tq, d), lambda b, i: (b * n_q + i, 0)), pl.BlockSpec((n_mem, 2 * d), lambda b, i: (b, 0))],
        out_specs=pl.BlockSpec((tq, d), lambda b, i: (b * n_q + i, 0)),
        out_shape=jax.ShapeDtypeStruct((batch * seq, d), BF16),
        compiler_params=_params(2),
    )(q, kv)


def _xattn_bwd(q, kv, do, *, batch, seq, n_mem, tq=512):
    tq = min(tq, seq)
    n_q = seq // tq
    d = X_HEADS * X_HEAD_DIM

    def body(q_ref, kv_ref, do_ref, dq_ref, dkv_ref, acc):
        i = pl.program_id(1)

        @pl.when(i == 0)
        def _():
            acc[...] = jnp.zeros_like(acc)

        for h in range(X_HEADS):
            cols = slice(h * X_HEAD_DIM, (h + 1) * X_HEAD_DIM)
            vcols = slice(d + h * X_HEAD_DIM, d + (h + 1) * X_HEAD_DIM)
            qh, kh, vh, doh = q_ref[:, cols], kv_ref[:, cols], kv_ref[:, vcols], do_ref[:, cols]
            p = _x_softmax(qh, kh)
            dp = _nt(doh, vh)
            acc[:, vcols] += _tn(p.astype(BF16), doh)
            ds = (p * (dp - jnp.sum(dp * p, axis=-1, keepdims=True)) * X_SCALE).astype(BF16)
            dq_ref[:, cols] = _nn(ds, kh).astype(BF16)
            acc[:, cols] += _tn(ds, qh)

        @pl.when(i == n_q - 1)
        def _():
            dkv_ref[...] = acc[...].astype(BF16)

    return pl.pallas_call(
        body, name="xattn_bwd", grid=(batch, n_q),
        in_specs=[pl.BlockSpec((tq, d), lambda b, i: (b * n_q + i, 0)), pl.BlockSpec((n_mem, 2 * d), lambda b, i: (b, 0)),
                  pl.BlockSpec((tq, d), lambda b, i: (b * n_q + i, 0))],
        out_specs=[pl.BlockSpec((tq, d), lambda b, i: (b * n_q + i, 0)), pl.BlockSpec((n_mem, 2 * d), lambda b, i: (b, 0))],
        out_shape=[jax.ShapeDtypeStruct((batch * seq, d), BF16), jax.ShapeDtypeStruct((batch * n_mem, 2 * d), BF16)],
        scratch_shapes=[pltpu.VMEM((n_mem, 2 * d), F32)],
        compiler_params=_params(2),
    )(q, kv, do)


def _loss_head(h, target, g, *, tm=512):
    t, d = h.shape
    tm = min(tm, t)
    n_steps = t // tm

    def body(h_ref, t_ref, g_ref, dh_ref, dg_ref, loss_ref, sq_acc):
        step = pl.program_id(0)

        @pl.when(step == 0)
        def _():
            dg_ref[...] = jnp.zeros_like(dg_ref)
            sq_acc[...] = jnp.zeros_like(sq_acc)

        hv = h_ref[...]
        gv = g_ref[...]
        r = _rstd(hv)
        xh = hv * r
        err = xh * gv - t_ref[...]
        sq_acc[...] += jnp.sum(err * err, axis=0, keepdims=True)
        dy = err * (1.0 / d)
        dg_ref[...] += jnp.sum(dy * xh, axis=0, keepdims=True)
        dxh = dy * gv
        dh_ref[...] = r * (dxh - xh * jnp.mean(dxh * xh, axis=-1, keepdims=True))

        @pl.when(step == n_steps - 1)
        def _():
            total = jnp.sum(sq_acc[...], axis=-1, keepdims=True) * (0.5 / d)
            loss_ref[...] = jnp.broadcast_to(total, loss_ref.shape)

    return pl.pallas_call(
        body, name="loss_head", grid=(n_steps,),
        in_specs=[pl.BlockSpec((tm, d), lambda i: (i, 0)), pl.BlockSpec((tm, d), lambda i: (i, 0)), pl.BlockSpec((1, d), lambda i: (0, 0))],
        out_specs=[pl.BlockSpec((tm, d), lambda i: (i, 0)), pl.BlockSpec((1, d), lambda i: (0, 0)), pl.BlockSpec((1, LANES), lambda i: (0, 0))],
        out_shape=[jax.ShapeDtypeStruct((t, d), F32), jax.ShapeDtypeStruct((1, d), F32), jax.ShapeDtypeStruct((1, LANES), F32)],
        scratch_shapes=[pltpu.VMEM((1, d), F32)],
        compiler_params=_params(1),
    )(h, target, g)


def _my_index():
    return 4 * lax.axis_index("x") + 2 * lax.axis_index("y") + lax.axis_index("c")


def _peers():
    x, y, c = lax.axis_index("x"), lax.axis_index("y"), lax.axis_index("c")
    out = []
    for rel in range(1, N_DEV):
        dx, dy, dc = (rel >> 2) & 1, (rel >> 1) & 1, rel & 1
        px, py, pc = x ^ dx, y ^ dy, c ^ dc
        out.append(((px, py, pc), 4 * px + 2 * py + pc))
    return out


def _exchange(arrays, *, scatter, name):
    n_w = len(arrays)
    n_peer = N_DEV - 1

    def body(*refs):
        srcs, dsts = refs[:n_w], refs[n_w:2 * n_w]
        send_sems, recv_sems, local_sems = refs[2 * n_w:]
        me = _my_index()
        peers = _peers()

        def remote(w, rel):
            pos, idx = peers[rel]
            src = srcs[w].at[idx] if scatter else srcs[w]
            return pltpu.make_async_remote_copy(src_ref=src, dst_ref=dsts[w].at[me], send_sem=send_sems.at[w, rel],
                                                recv_sem=recv_sems.at[w, rel], device_id=pos, device_id_type=MESH)

        def arrival(w, rel):
            pos, idx = peers[rel]
            src = srcs[w].at[idx] if scatter else srcs[w]
            return pltpu.make_async_remote_copy(src_ref=src, dst_ref=dsts[w].at[idx], send_sem=send_sems.at[w, rel],
                                                recv_sem=recv_sems.at[w, rel], device_id=pos, device_id_type=MESH)

        local = [pltpu.make_async_copy(srcs[w].at[me] if scatter else srcs[w], dsts[w].at[me], local_sems.at[w]) for w in range(n_w)]
        sends = [remote(w, rel) for w in range(n_w) for rel in range(n_peer)]
        for cp in local + sends:
            cp.start()
        for w in range(n_w):
            for rel in range(n_peer):
                arrival(w, rel).wait_recv()
        for cp in sends:
            cp.wait_send()
        for cp in local:
            cp.wait()

    out_shape = [jax.ShapeDtypeStruct((N_DEV,) + tuple(a.shape[-2:]), a.dtype) for a in arrays]
    any_spec = pl.BlockSpec(memory_space=pl.ANY)
    return pl.pallas_call(
        body, name=name, in_specs=[any_spec] * n_w, out_specs=[any_spec] * n_w, out_shape=out_shape,
        scratch_shapes=[pltpu.SemaphoreType.DMA((n_w, n_peer)), pltpu.SemaphoreType.DMA((n_w, n_peer)), pltpu.SemaphoreType.DMA((n_w,))],
        compiler_params=pltpu.CompilerParams(has_side_effects=True),
    )(*arrays)


def _all_reduce_small(part):
    rows = part.shape[0]
    n_peer = N_DEV - 1

    def body(p_ref, o_ref, buf, send_sems, recv_sems):
        me = _my_index()
        peers = _peers()
        buf[me] = p_ref[...]
        sends = [pltpu.make_async_remote_copy(src_ref=p_ref, dst_ref=buf.at[me], send_sem=send_sems.at[rel], recv_sem=recv_sems.at[rel],
                                              device_id=peers[rel][0], device_id_type=MESH) for rel in range(n_peer)]
        for cp in sends:
            cp.start()
        for rel in range(n_peer):
            pltpu.make_async_remote_copy(src_ref=p_ref, dst_ref=buf.at[peers[rel][1]], send_sem=send_sems.at[rel], recv_sem=recv_sems.at[rel],
                                         device_id=peers[rel][0], device_id_type=MESH).wait_recv()
        for cp in sends:
            cp.wait_send()
        total = buf[0]
        for dev in range(1, N_DEV):
            total = total + buf[dev]
        o_ref[...] = total

    vmem = pl.BlockSpec(memory_space=pltpu.VMEM)
    return pl.pallas_call(
        body, name="all_reduce_small", in_specs=[vmem], out_specs=vmem, out_shape=jax.ShapeDtypeStruct(part.shape, F32),
        scratch_shapes=[pltpu.VMEM((N_DEV, rows, LANES), F32), pltpu.SemaphoreType.DMA((n_peer,)), pltpu.SemaphoreType.DMA((n_peer,))],
        compiler_params=pltpu.CompilerParams(has_side_effects=True, vmem_limit_bytes=VMEM_LIMIT),
    )(part)


def _adamw_math(w, g, m, v):
    m_new = ADAM_B1 * m + (1.0 - ADAM_B1) * g
    v_new = ADAM_B2 * v + (1.0 - ADAM_B2) * (g * g)
    m_hat = m_new / (1.0 - ADAM_B1 ** ADAM_STEP)
    v_hat = v_new / (1.0 - ADAM_B2 ** ADAM_STEP)
    delta = -ADAM_LR * (m_hat / (jnp.sqrt(v_hat) + ADAM_EPS) + ADAM_WD * w)
    return delta, m_new, v_new


def _adamw(parts, w, m, v, *, name, tr=64):
    rows, cols = w.shape
    tr = min(tr, rows)
    assert rows % tr == 0
    stacked = parts.ndim == 3

    def body(p_ref, w_ref, m_ref, v_ref, g_ref, d_ref, mo_ref, vo_ref):
        if stacked:
            g = p_ref[0].astype(F32)
            for dev in range(1, N_DEV):
                g = g + p_ref[dev].astype(F32)
        else:
            g = p_ref[...]
        delta, m_new, v_new = _adamw_math(w_ref[...], g, m_ref[...], v_ref[...])
        g_ref[...] = g
        d_ref[...] = delta
        mo_ref[...] = m_new
        vo_ref[...] = v_new

    tile = pl.BlockSpec((tr, cols), lambda i: (i, 0))
    p_spec = pl.BlockSpec((N_DEV, tr, cols), lambda i: (0, i, 0)) if stacked else tile
    return pl.pallas_call(
        body, name=name, grid=(rows // tr,), in_specs=[p_spec, tile, tile, tile], out_specs=[tile] * 4,
        out_shape=[jax.ShapeDtypeStruct((rows, cols), F32)] * 4, compiler_params=_params(1),
    )(parts, w, m, v)


def _local_step(x, mem, target, small, big):
    batch, seq, d = x.shape
    n_mem = mem.shape[1]
    t = batch * seq
    x2, mem2, tgt2 = x.reshape(t, d), mem.reshape(batch * n_mem, d), target.reshape(t, d)
    g_mix, g_cross, g_mem, g_ffn, g_final = (small[k] for k in ("norm_mix_g", "norm_cross_g", "norm_mem_g", "norm_ffn_g", "norm_final_g"))
    gv, hg, w_sp, b_sp_t = small["gm_v_norm_g"], small["head_norm_g"], small["w_spatial"], small["b_spatial_t"]
    win_t, wout, wcq, wckv_t, wco, wff1_t, wff2 = (big[k] for k in ("w_in", "w_out", "w_cq", "w_ckv", "w_co", "w_ff1", "w_ff2"))

    proj, xn = _norm_mm(x2, g_mix, win_t, mode="nt", name="proj_fwd", tn=512)
    merged_a = _gmlp_fwd(proj, w_sp, b_sp_t, gv, hg)
    merged, sb_raw, sb_tot = _sb_fwd(proj, merged_a, hg, batch=batch, seq=seq)
    h1 = _mm(merged, wout, mode="nn", out_dtype=F32, name="mix_out_fwd", epi=_epi_residual, epi_ins=(x2,))
    qx, hn1 = _norm_mm(h1, g_cross, wcq, mode="nn", name="xq_fwd")
    kvx, memn = _norm_mm(mem2, g_mem, wckv_t, mode="nt", name="xkv_fwd")
    o = _xattn_fwd(qx, kvx, batch=batch, seq=seq, n_mem=n_mem)
    h2 = _mm(o, wco, mode="nn", out_dtype=F32, name="xo_fwd", epi=_epi_residual, epi_ins=(h1,))
    fpre, hn2 = _norm_mm(h2, g_ffn, wff1_t, mode="nt", name="ff1_fwd")
    h3 = _mm(fpre, wff2, mode="nn", out_dtype=F32, name="ff2_fwd", a_fn=_relu2, epi=_epi_residual, epi_ins=(h2,))
    dh3, dg_final, loss_row = _loss_head(h3, tgt2, g_final)

    dpre = _mm(dh3, wff2, mode="nt", out_dtype=BF16, name="ff2_bwd_x", epi=_epi_relu2_grad, epi_ins=(fpre,))
    d_wff2 = _mm(fpre, dh3, mode="tn", out_dtype=BF16, name="ff2_bwd_w", a_fn=_relu2)
    d_wff1_t = _mm(dpre, hn2, mode="tn", out_dtype=BF16, name="ff1_bwd_w")
    dh2, dg_ffn = _mm(dpre, wff1_t, mode="nn", out_dtype=F32, name="ff1_bwd_x", tm=512, epi=_epi_rms_bwd,
                      epi_ins=(h2, dh3), vec_ins=(g_ffn,), aux=True)
    do = _mm(dh2, wco, mode="nt", out_dtype=BF16, name="xo_bwd_x")
    d_wco = _mm(o, dh2, mode="tn", out_dtype=BF16, name="xo_bwd_w")
    dqx, dkvx = _xattn_bwd(qx, kvx, do, batch=batch, seq=seq, n_mem=n_mem)
    d_wcq = _mm(hn1, dqx, mode="tn", out_dtype=BF16, name="xq_bwd_w")
    dh1, dg_cross = _mm(dqx, wcq, mode="nt", out_dtype=F32, name="xq_bwd_x", tm=512, epi=_epi_rms_bwd,
                        epi_ins=(h1, dh2), vec_ins=(g_cross,), aux=True)
    d_wckv_t = _mm(dkvx, memn, mode="tn", out_dtype=BF16, name="xkv_bwd_w")
    _, dg_mem = _mm(dkvx, wckv_t, mode="nn", out_dtype=BF16, name="xkv_bwd_x", tm=512, epi=_epi_rms_gain_only,
                    epi_ins=(mem2,), vec_ins=(g_mem,), aux=True)
    dmerged = _mm(dh1, wout, mode="nt", out_dtype=BF16, name="mix_out_bwd_x")
    d_wout = _mm(merged, dh1, mode="tn", out_dtype=BF16, name="mix_out_bwd_w")
    dp_a, d_wsp, d_bsp_t, d_gv, d_hg_a = _gmlp_bwd(proj, dmerged, w_sp, b_sp_t, gv, hg)
    dq, dk, dv, d_hg_b = _sb_bwd(proj, sb_raw, sb_tot, dmerged, hg, batch=batch, seq=seq)
    dproj = jnp.concatenate([dp_a, dq, dk, dv], axis=1)
    d_win_t = _mm(dproj, xn, mode="tn", out_dtype=BF16, name="proj_bwd_w", tm=512)
    dx, dg_mix = _mm(dproj, win_t, mode="nn", out_dtype=F32, name="proj_bwd_x", tm=512, tk=512, epi=_epi_rms_bwd,
                     epi_ins=(x2, dh1), vec_ins=(g_mix,), aux=True)

    d_small = {"norm_mix_g": dg_mix, "gm_v_norm_g": d_gv, "w_spatial": d_wsp, "b_spatial_t": d_bsp_t, "head_norm_g": jnp.concatenate([d_hg_a, d_hg_b], axis=1),
               "norm_cross_g": dg_cross, "norm_mem_g": dg_mem, "norm_ffn_g": dg_ffn, "norm_final_g": dg_final}
    d_big = {"w_in": d_win_t, "w_out": d_wout, "w_cq": d_wcq, "w_ckv": d_wckv_t, "w_co": d_wco, "w_ff1": d_wff1_t, "w_ff2": d_wff2}
    return loss_row, dx.reshape(batch, seq, d), d_small, d_big


_BIG = ("w_in", "w_out", "w_cq", "w_ckv", "w_co", "w_ff1", "w_ff2")
_BIG_TRANSPOSED = ("w_in", "w_ckv", "w_ff1")
_SMALL = ("norm_mix_g", "gm_v_norm_g", "w_spatial", "b_spatial", "head_norm_g", "norm_cross_g", "norm_mem_g", "norm_ffn_g", "norm_final_g")
_NAMES = ("norm_mix_g", "w_in", "gm_v_norm_g", "w_spatial", "b_spatial", "head_norm_g", "w_out", "norm_cross_g", "norm_mem_g",
          "w_cq", "w_ckv", "w_co", "norm_ffn_g", "w_ff1", "w_ff2", "norm_final_g")


def _rows_of(a):
    r = a.reshape(-1, LANES)
    pad = (-r.shape[0]) % 8
    return jnp.pad(r, ((0, pad), (0, 0))) if pad else r


def _shard2d(name, a):
    a = a[0]
    return a.T if name in _BIG_TRANSPOSED else a


def kernel(x, mem, norm_mix_g, w_in, gm_v_norm_g, w_spatial, b_spatial, head_norm_g, w_out, norm_cross_g, norm_mem_g, w_cq, w_ckv, w_co, norm_ffn_g, w_ff1, w_ff2, norm_final_g, loss_target, m_norm_mix_g, m_w_in, m_gm_v_norm_g, m_w_spatial, m_b_spatial, m_head_norm_g, m_w_out, m_norm_cross_g, m_norm_mem_g, m_w_cq, m_w_ckv, m_w_co, m_norm_ffn_g, m_w_ff1, m_w_ff2, m_norm_final_g, v_norm_mix_g, v_w_in, v_gm_v_norm_g, v_w_spatial, v_b_spatial, v_head_norm_g, v_w_out, v_norm_cross_g, v_norm_mem_g, v_w_cq, v_w_ckv, v_w_co, v_norm_ffn_g, v_w_ff1, v_w_ff2, v_norm_final_g):
    weights = dict(norm_mix_g=norm_mix_g, w_in=w_in, gm_v_norm_g=gm_v_norm_g, w_spatial=w_spatial, b_spatial=b_spatial,
                   head_norm_g=head_norm_g, w_out=w_out, norm_cross_g=norm_cross_g, norm_mem_g=norm_mem_g, w_cq=w_cq, w_ckv=w_ckv,
                   w_co=w_co, norm_ffn_g=norm_ffn_g, w_ff1=w_ff1, w_ff2=w_ff2, norm_final_g=norm_final_g)
    mom1 = dict(norm_mix_g=m_norm_mix_g, w_in=m_w_in, gm_v_norm_g=m_gm_v_norm_g, w_spatial=m_w_spatial, b_spatial=m_b_spatial,
                head_norm_g=m_head_norm_g, w_out=m_w_out, norm_cross_g=m_norm_cross_g, norm_mem_g=m_norm_mem_g, w_cq=m_w_cq,
                w_ckv=m_w_ckv, w_co=m_w_co, norm_ffn_g=m_norm_ffn_g, w_ff1=m_w_ff1, w_ff2=m_w_ff2, norm_final_g=m_norm_final_g)
    mom2 = dict(norm_mix_g=v_norm_mix_g, w_in=v_w_in, gm_v_norm_g=v_gm_v_norm_g, w_spatial=v_w_spatial, b_spatial=v_b_spatial,
                head_norm_g=v_head_norm_g, w_out=v_w_out, norm_cross_g=v_norm_cross_g, norm_mem_g=v_norm_mem_g, w_cq=v_w_cq,
                w_ckv=v_w_ckv, w_co=v_w_co, norm_ffn_g=v_norm_ffn_g, w_ff1=v_w_ff1, w_ff2=v_w_ff2, norm_final_g=v_norm_final_g)

    shards = [_shard2d(n, weights[n]).astype(BF16) for n in _BIG]
    gathered = _exchange(shards, scatter=False, name="gather_weights")
    big = {n: g.reshape(-1, g.shape[-1]) for n, g in zip(_BIG, gathered)}

    small = {n: weights[n].reshape(1, -1) for n in _SMALL if n not in ("w_spatial", "b_spatial")}
    small["w_spatial"] = w_spatial[0]
    small["b_spatial_t"] = b_spatial[0].T
    loss_row, grad_x, d_small, d_big = _local_step(x, mem, loss_target, small, big)

    parts = [d_big[n].reshape(N_DEV, -1, d_big[n].shape[-1]) for n in _BIG]
    received = _exchange(parts, scatter=True, name="scatter_grads")
    d_small["b_spatial"] = d_small.pop("b_spatial_t").T
    packed = jnp.concatenate([_rows_of(d_small[n]) for n in _SMALL] + [jnp.pad(loss_row, ((0, 7), (0, 0)))], axis=0)
    summed = _all_reduce_small(packed)

    grads, deltas, new_m, new_v = {}, {}, {}, {}
    for n, rec in zip(_BIG, received):
        outs = _adamw(rec, _shard2d(n, weights[n]), _shard2d(n, mom1[n]), _shard2d(n, mom2[n]), name="adamw_" + n)
        outs = [o.T if n in _BIG_TRANSPOSED else o for o in outs]
        grads[n], deltas[n], new_m[n], new_v[n] = (o[None] for o in outs)
    pack = lambda src: jnp.concatenate([_rows_of(src[n]) for n in _SMALL], axis=0)
    n_small_rows = sum(_rows_of(weights[n]).shape[0] for n in _SMALL)
    outs = _adamw(summed[:n_small_rows], pack(weights), pack(mom1), pack(mom2), name="adamw_small", tr=n_small_rows)
    at = 0
    for n in _SMALL:
        used = weights[n].size // LANES
        for dst, o in zip((grads, deltas, new_m, new_v), outs):
            dst[n] = o[at:at + used].reshape(weights[n].shape)
        at += _rows_of(weights[n]).shape[0]
    loss = summed[n_small_rows, 0]
    return (loss, grad_x, *[grads[n] for n in _NAMES], *[deltas[n] for n in _NAMES], *[new_m[n] for n in _NAMES],
            *[new_v[n] for n in _NAMES])
```

```python
import functools
import math

import jax
import jax.numpy as jnp
from jax import lax
from jax.experimental import pallas as pl
from jax.experimental.pallas import tpu as pltpu

F32 = jnp.float32
BF16 = jnp.bfloat16
EPS = 1e-6
N_DEV = 8
LANES = 128
CHUNK = 128
GM_GROUPS = 4
GM_WIDTH = 512
SB_PAIRS = 4
SB_HEAD_DIM = 64
SB_SCALE = 0.125
SB_TILE = 128
SB_BLOCK = 512
X_HEADS = 4
X_HEAD_DIM = 256
X_SCALE = 1.0 / 16.0
VMEM_LIMIT = 56 * 1024 * 1024
ADAM_LR, ADAM_B1, ADAM_B2, ADAM_EPS, ADAM_WD, ADAM_STEP = 0.001, 0.9, 0.999, 1e-08, 0.01, 10
MESH = pl.DeviceIdType.MESH


def _params(n_axes):
    return pltpu.CompilerParams(dimension_semantics=("arbitrary",) * n_axes, vmem_limit_bytes=VMEM_LIMIT)


def _dot(a, b, dims):
    return lax.dot_general(a, b, (dims, ((), ())), preferred_element_type=F32)


def _nn(a, b):
    return _dot(a, b, ((1,), (0,)))


def _nt(a, b):
    return _dot(a, b, ((1,), (1,)))


def _tn(a, b):
    return _dot(a, b, ((0,), (0,)))


_MODES = {"nn": _nn, "nt": _nt, "tn": _tn}


def _rstd(x):
    return lax.rsqrt(jnp.mean(x * x, axis=-1, keepdims=True) + EPS)


def _gelu(x):
    c = math.sqrt(2.0 / math.pi)
    t = jnp.tanh(c * (x + 0.044715 * x * x * x))
    return 0.5 * x * (1.0 + t)


def _gelu_grad(x):
    c = math.sqrt(2.0 / math.pi)
    t = jnp.tanh(c * (x + 0.044715 * x * x * x))
    return 0.5 * (1.0 + t) + 0.5 * x * (1.0 - t * t) * c * (1.0 + 3 * 0.044715 * x * x)


def _split_bf16(x):
    hi = x.astype(BF16)
    lo = (x - hi.astype(F32)).astype(BF16)
    return hi, lo


def _mm(a, b, *, mode, out_dtype, name, tm=1024, tn=1024, tk=1024, a_fn=None, epi=None, epi_ins=(), vec_ins=(), aux=False):
    if mode == "nn":
        (m, k), (k2, n) = a.shape, b.shape
    elif mode == "nt":
        (m, k), (n, k2) = a.shape, b.shape
    else:
        (k, m), (k2, n) = a.shape, b.shape
    assert k == k2, (a.shape, b.shape, mode)
    tm, tn, tk = min(tm, m), min(tn, n), min(tk, k)
    assert m % tm == 0 and n % tn == 0 and k % tk == 0, (m, n, k, tm, tn, tk)
    n_m, n_n, n_k = m // tm, n // tn, k // tk
    assert not aux or n_n == 1
    dot = _MODES[mode]
    n_epi, n_vec = len(epi_ins), len(vec_ins)

    def body(*refs):
        a_ref, b_ref = refs[:2]
        epi_refs = refs[2:2 + n_epi + n_vec]
        outs = refs[2 + n_epi + n_vec:]
        o_ref = outs[0]
        aux_ref = outs[1] if aux else None
        acc_ref = outs[-1] if n_k > 1 else None
        i, kk = pl.program_id(0), pl.program_id(2)
        av = a_ref[...]
        if a_fn is not None:
            av = a_fn(av)
        part = dot(av.astype(BF16), b_ref[...].astype(BF16))

        def finish(acc):
            if epi is None:
                o_ref[...] = acc.astype(out_dtype)
                return
            res = epi(acc, *[r[...] for r in epi_refs])
            if aux:
                res, row = res

                @pl.when(i == 0)
                def _():
                    aux_ref[...] = row

                @pl.when(i != 0)
                def _():
                    aux_ref[...] += row
            o_ref[...] = res.astype(out_dtype)

        if n_k == 1:
            finish(part)
        else:
            @pl.when(kk == 0)
            def _():
                acc_ref[...] = part

            @pl.when(kk != 0)
            def _():
                acc_ref[...] += part

            @pl.when(kk == n_k - 1)
            def _():
                finish(acc_ref[...])

    if mode == "tn":
        a_spec = pl.BlockSpec((tk, tm), lambda i, j, kk: (kk, i))
    else:
        a_spec = pl.BlockSpec((tm, tk), lambda i, j, kk: (i, kk))
    if mode == "nt":
        b_spec = pl.BlockSpec((tn, tk), lambda i, j, kk: (j, kk))
    else:
        b_spec = pl.BlockSpec((tk, tn), lambda i, j, kk: (kk, j))
    tile_spec = pl.BlockSpec((tm, tn), lambda i, j, kk: (i, j))
    row_spec = pl.BlockSpec((1, tn), lambda i, j, kk: (0, j))
    out_shape = [jax.ShapeDtypeStruct((m, n), out_dtype)]
    out_specs = [tile_spec]
    if aux:
        out_shape.append(jax.ShapeDtypeStruct((1, n), F32))
        out_specs.append(row_spec)
    res = pl.pallas_call(
        body, name=name, grid=(n_m, n_n, n_k),
        in_specs=[a_spec, b_spec] + [tile_spec] * n_epi + [row_spec] * n_vec,
        out_specs=out_specs, out_shape=out_shape,
        scratch_shapes=[pltpu.VMEM((tm, tn), F32)] if n_k > 1 else [],
        compiler_params=_params(3),
    )(a, b, *epi_ins, *vec_ins)
    return res if aux else res[0]


def _norm_mm(x, g, w, *, mode, name, tm=1024, tn=1024):
    m, d = x.shape
    n = w.shape[0] if mode == "nt" else w.shape[1]
    tm, tn = min(tm, m), min(tn, n)
    assert m % tm == 0 and n % tn == 0
    dot = _MODES[mode]

    def body(x_ref, g_ref, w_ref, o_ref, xn_ref, xn_s):
        @pl.when(pl.program_id(1) == 0)
        def _():
            xv = x_ref[...]
            xn = (xv * _rstd(xv) * g_ref[...]).astype(BF16)
            xn_s[...] = xn
            xn_ref[...] = xn

        o_ref[...] = dot(xn_s[...], w_ref[...]).astype(BF16)

    w_spec = pl.BlockSpec((tn, d), lambda i, j: (j, 0)) if mode == "nt" else pl.BlockSpec((d, tn), lambda i, j: (0, j))
    return pl.pallas_call(
        body, name=name, grid=(m // tm, n // tn),
        in_specs=[pl.BlockSpec((tm, d), lambda i, j: (i, 0)), pl.BlockSpec((1, d), lambda i, j: (0, 0)), w_spec],
        out_specs=[pl.BlockSpec((tm, tn), lambda i, j: (i, j)), pl.BlockSpec((tm, d), lambda i, j: (i, 0))],
        out_shape=[jax.ShapeDtypeStruct((m, n), BF16), jax.ShapeDtypeStruct((m, d), BF16)],
        scratch_shapes=[pltpu.VMEM((tm, d), BF16)],
        compiler_params=_params(2),
    )(x, g, w)


def _epi_residual(acc, res):
    return res + acc


def _epi_relu2_grad(acc, pre):
    return acc * (2.0 * jnp.maximum(pre.astype(F32), 0.0))


def _relu2(pre):
    r = jnp.maximum(pre.astype(F32), 0.0)
    return r * r


def _epi_rms_bwd(acc, h, dres, g):
    r = _rstd(h)
    xh = h * r
    dxh = acc * g
    dh = dres + r * (dxh - xh * jnp.mean(dxh * xh, axis=-1, keepdims=True))
    return dh, jnp.sum(acc * xh, axis=0, keepdims=True)


def _epi_rms_gain_only(acc, h, g):
    return acc, jnp.sum(acc * (h * _rstd(h)), axis=0, keepdims=True)


def _tril(n):
    row = lax.broadcasted_iota(jnp.int32, (n, n), 0)
    col = lax.broadcasted_iota(jnp.int32, (n, n), 1)
    return col <= row


def _gmlp_fwd(proj, w_sp, b_sp_t, gv, hg, *, rows=512):
    t = proj.shape[0]
    rows = min(rows, t)
    n_c = rows // CHUNK

    def body(u_ref, v_ref, w_ref, bt_ref, gv_ref, hg_ref, m_ref):
        keep = _tril(CHUNK)
        for g in range(GM_GROUPS):
            cols = slice(g * LANES, (g + 1) * LANES)
            wg = jnp.where(keep, w_ref[g], 0.0).astype(BF16)
            u = _gelu(u_ref[:, cols].astype(F32))
            v = _gelu(v_ref[:, cols].astype(F32))
            vn = (v * _rstd(v) * gv_ref[:, cols]).astype(BF16)
            bias = bt_ref[:, g:g + 1]
            for c in range(n_c):
                rs = slice(c * CHUNK, (c + 1) * CHUNK)
                mixed = _nn(wg, vn[rs]) + bias
                a = u[rs] * mixed
                m_ref[rs, cols] = (a * _rstd(a) * hg_ref[:, cols]).astype(BF16)

    full = lambda shape: pl.BlockSpec(shape, lambda i: (0,) * len(shape))
    return pl.pallas_call(
        body, name="gmlp_fwd", grid=(t // rows,),
        in_specs=[pl.BlockSpec((rows, GM_WIDTH), lambda i: (i, 0)), pl.BlockSpec((rows, GM_WIDTH), lambda i: (i, 1)),
                  full((GM_GROUPS, CHUNK, CHUNK)), full((CHUNK, GM_GROUPS)), full((1, GM_WIDTH)), full((1, GM_WIDTH))],
        out_specs=pl.BlockSpec((rows, GM_WIDTH), lambda i: (i, 0)),
        out_shape=jax.ShapeDtypeStruct((t, 2 * GM_WIDTH), BF16),
        compiler_params=_params(1),
    )(proj, proj, w_sp, b_sp_t, gv, hg)


def _gmlp_bwd(proj, dmerged, w_sp, b_sp_t, gv, hg, *, rows=512):
    t = proj.shape[0]
    rows = min(rows, t)
    n_c = rows // CHUNK
    n_steps = t // rows

    def body(u_ref, v_ref, dm_ref, w_ref, bt_ref, gv_ref, hg_ref, dp_ref, dw_ref, dbt_ref, dgv_ref, dhg_ref, db_acc):
        step = pl.program_id(0)
        keep = _tril(CHUNK)

        @pl.when(step == 0)
        def _():
            dw_ref[...] = jnp.zeros_like(dw_ref)
            db_acc[...] = jnp.zeros_like(db_acc)
            dgv_ref[...] = jnp.zeros_like(dgv_ref)
            dhg_ref[...] = jnp.zeros_like(dhg_ref)

        for g in range(GM_GROUPS):
            cols = slice(g * LANES, (g + 1) * LANES)
            wg = jnp.where(keep, w_ref[g], 0.0).astype(BF16)
            u_pre = u_ref[:, cols].astype(F32)
            v_pre = v_ref[:, cols].astype(F32)
            u = _gelu(u_pre)
            v = _gelu(v_pre)
            r = _rstd(v)
            xh = v * r
            gvg = gv_ref[:, cols]
            hgg = hg_ref[:, cols]
            vn = (xh * gvg).astype(BF16)
            bias = bt_ref[:, g:g + 1]
            dm = dm_ref[:, cols].astype(F32)
            du_parts, dvn_parts = [], []
            dw = jnp.zeros((CHUNK, CHUNK), F32)
            db = jnp.zeros((CHUNK, LANES), F32)
            dhg = jnp.zeros((1, LANES), F32)
            for c in range(n_c):
                rs = slice(c * CHUNK, (c + 1) * CHUNK)
                mixed = _nn(wg, vn[rs]) + bias
                a = u[rs] * mixed
                ra = _rstd(a)
                an = a * ra
                dhg = dhg + jnp.sum(dm[rs] * an, axis=0, keepdims=True)
                dan = dm[rs] * hgg
                da = ra * (dan - an * jnp.mean(dan * an, axis=-1, keepdims=True))
                du_parts.append(da * mixed)
                dmixed = da * u[rs]
                db = db + dmixed
                dmb = dmixed.astype(BF16)
                dw = dw + _nt(dmb, vn[rs])
                dvn_parts.append(_tn(wg, dmb))
            du = jnp.concatenate(du_parts, axis=0)
            dvn = jnp.concatenate(dvn_parts, axis=0)
            dw_ref[g] += dw
            db_acc[g] += db
            dhg_ref[:, cols] += dhg
            dgv_ref[:, cols] += jnp.sum(dvn * xh, axis=0, keepdims=True)
            dxh = dvn * gvg
            dv = r * (dxh - xh * jnp.mean(dxh * xh, axis=-1, keepdims=True))
            dp_ref[:, cols] = (du * _gelu_grad(u_pre)).astype(BF16)
            dp_ref[:, GM_WIDTH + g * LANES:GM_WIDTH + (g + 1) * LANES] = (dv * _gelu_grad(v_pre)).astype(BF16)

        @pl.when(step == n_steps - 1)
        def _():
            for g in range(GM_GROUPS):
                dw_ref[g] = jnp.where(keep, dw_ref[g], 0.0)
                dbt_ref[:, g:g + 1] = jnp.sum(db_acc[g], axis=-1, keepdims=True)

    full = lambda shape: pl.BlockSpec(shape, lambda i: (0,) * len(shape))
    return pl.pallas_call(
        body, name="gmlp_bwd", grid=(n_steps,),
        in_specs=[pl.BlockSpec((rows, GM_WIDTH), lambda i: (i, 0)), pl.BlockSpec((rows, GM_WIDTH), lambda i: (i, 1)),
                  pl.BlockSpec((rows, GM_WIDTH), lambda i: (i, 0)),
                  full((GM_GROUPS, CHUNK, CHUNK)), full((CHUNK, GM_GROUPS)), full((1, GM_WIDTH)), full((1, GM_WIDTH))],
        out_specs=[pl.BlockSpec((rows, 2 * GM_WIDTH), lambda i: (i, 0)), full((GM_GROUPS, CHUNK, CHUNK)),
                   full((CHUNK, GM_GROUPS)), full((1, GM_WIDTH)), full((1, GM_WIDTH))],
        out_shape=[jax.ShapeDtypeStruct((t, 2 * GM_WIDTH), BF16), jax.ShapeDtypeStruct((GM_GROUPS, CHUNK, CHUNK), F32),
                   jax.ShapeDtypeStruct((CHUNK, GM_GROUPS), F32), jax.ShapeDtypeStruct((1, GM_WIDTH), F32),
                   jax.ShapeDtypeStruct((1, GM_WIDTH), F32)],
        scratch_shapes=[pltpu.VMEM((GM_GROUPS, CHUNK, LANES), F32)],
        compiler_params=_params(1),
    )(proj, proj, dmerged, w_sp, b_sp_t, gv, hg)


def _sb_logits(qh, kj, strict):
    z = _nt(qh, kj)
    ls = jnp.minimum(z, 0.0) - jnp.log(1.0 + jnp.exp(-jnp.abs(z)))
    l1m = ls - z
    if strict is not None:
        l1m = jnp.where(strict, l1m, 0.0)
    return ls, l1m


def _tri_sums(x, tri):
    hi, lo = _split_bf16(x)
    return _nn(jnp.concatenate([hi, lo], axis=1), jnp.concatenate([tri, tri], axis=0))


def _sb_weights(ls, in_tile, right, strict):
    a = jnp.exp(ls + in_tile + right)
    if strict is not None:
        a = jnp.where(strict, a, 0.0)
    return a


def _sb_masks():
    row = lax.broadcasted_iota(jnp.int32, (SB_TILE, SB_TILE), 0)
    col = lax.broadcasted_iota(jnp.int32, (SB_TILE, SB_TILE), 1)
    lane = lax.broadcasted_iota(jnp.int32, (SB_TILE, LANES), 1)
    return row, col, lane < SB_HEAD_DIM


def _stack_heads(x, first):
    zero = jnp.zeros_like(x)
    return jnp.concatenate([jnp.where(first, x, zero), jnp.where(first, zero, x)], axis=0)


def _stack_heads_t(x_t, first_t):
    zero = jnp.zeros_like(x_t)
    return jnp.concatenate([jnp.where(first_t, x_t, zero), jnp.where(first_t, zero, x_t)], axis=1).astype(BF16)


def _stacked_col_minus_row():
    row = lax.broadcasted_iota(jnp.int32, (2 * SB_TILE, SB_TILE), 0)
    col = lax.broadcasted_iota(jnp.int32, (2 * SB_TILE, SB_TILE), 1)
    return col - (row & (SB_TILE - 1))


def _head_mean(x, first):
    s0 = jnp.sum(jnp.where(first, x, 0.0), axis=-1, keepdims=True)
    s1 = jnp.sum(jnp.where(first, 0.0, x), axis=-1, keepdims=True)
    return jnp.where(first, s0, s1) * (1.0 / SB_HEAD_DIM)


def _riding(exchange, refs, n_in, n_out, n_scratch):
    n_x = exchange.n if exchange is not None else 0
    ins, rest = refs[:n_in], refs[n_in:]
    x_src, rest = rest[:n_x], rest[n_x:]
    outs, rest = rest[:n_out], rest[n_out:]
    x_dst, rest = rest[:n_x], rest[n_x:]
    return ins, outs, rest[:n_scratch], (x_src, x_dst, rest[n_scratch:])


def _riding_specs(exchange):
    if exchange is None:
        return [], [], [], [], []
    return exchange.in_specs, exchange.out_specs, exchange.out_shape, exchange.scratch, exchange.arrays


def _ride(exchange, x_refs, first_step, last_step):
    if exchange is None:
        return lambda: None

    @pl.when(first_step)
    def _():
        exchange.start(*x_refs)

    def finish():
        @pl.when(last_step)
        def _():
            exchange.wait(*x_refs)

    return finish


def _sb_fwd(proj, merged_a, hg, *, batch, seq, exchange=None):
    n_q = seq // SB_TILE
    q0, k0, v0 = 2 * GM_WIDTH // LANES, 2 * GM_WIDTH // LANES + SB_PAIRS, 2 * GM_WIDTH // LANES + 2 * SB_PAIRS
    block_keys = min(SB_BLOCK, seq)
    n_sub = block_keys // SB_TILE

    def body(*refs):
        (q_ref, k_ref, v_ref, hg_ref, _), (m_ref, raw_ref, tot_ref), _, x_refs = _riding(exchange, refs, 5, 3, 0)
        b, p, i = pl.program_id(0), pl.program_id(1), pl.program_id(2)
        finish = _ride(exchange, x_refs, (b == 0) & (p == 0) & (i == 0), (b == batch - 1) & (p == SB_PAIRS - 1) & (i == n_q - 1))
        row, col, first = _sb_masks()
        upper = (row > col).astype(BF16)
        q2 = _stack_heads((q_ref[...].astype(F32) * SB_SCALE).astype(BF16), first)
        diff = _stacked_col_minus_row()
        last = i // n_sub
        offset = (i - last * n_sub) * SB_TILE

        def block(jb, carry, masked):
            right, acc = carry
            parts = []
            for s in range(n_sub):
                at = pl.ds(pl.multiple_of((jb * n_sub + s) * SB_TILE, SB_TILE), SB_TILE)
                keep = diff < offset - s * SB_TILE if masked else None
                ls, l1m = _sb_logits(q2, k_ref[at, :], keep)
                parts.append((at, keep, ls, l1m, jnp.sum(l1m, axis=-1, keepdims=True)))
            sums = [_tri_sums(l1m, upper) for _, _, _, l1m, _ in parts]
            weights = [None] * n_sub
            for s in reversed(range(n_sub)):
                at, keep, ls, _, total = parts[s]
                weights[s] = _sb_weights(ls, sums[s], right, keep)
                right = right + total
            a_all = jnp.concatenate([w.astype(BF16) for w in weights], axis=1)
            keys = pl.ds(pl.multiple_of(jb * block_keys, block_keys), block_keys)
            return right, acc + _nn(a_all, v_ref[keys, :])

        carry = block(last, (jnp.zeros((2 * SB_TILE, 1), F32), jnp.zeros((2 * SB_TILE, LANES), F32)), True)
        right, acc2 = lax.fori_loop(0, last, lambda s, c: block(last - 1 - s, c, False), carry)
        acc = jnp.where(first, acc2[:SB_TILE], acc2[SB_TILE:])
        raw_ref[...] = acc
        tot_ref[0] = jnp.where(first, right[:SB_TILE], right[SB_TILE:])
        m_ref[...] = (acc * lax.rsqrt(_head_mean(acc * acc, first) + EPS) * hg_ref[...]).astype(BF16)
        finish()

    t = batch * seq
    blk = lambda c0: pl.BlockSpec((SB_TILE, LANES), lambda b, p, i: (b * n_q + i, c0 + p))
    kv = lambda c0: pl.BlockSpec((seq, LANES), lambda b, p, i: (b, c0 + p))
    x_in, x_out, x_shape, x_scratch, x_arrays = _riding_specs(exchange)
    res = pl.pallas_call(
        body, name="sb_fwd", grid=(batch, SB_PAIRS, n_q),
        in_specs=[blk(q0), kv(k0), kv(v0), pl.BlockSpec((1, LANES), lambda b, p, i: (0, SB_PAIRS + p)),
                  pl.BlockSpec(memory_space=pl.ANY)] + x_in,
        out_specs=[blk(SB_PAIRS), blk(0), pl.BlockSpec((1, SB_TILE, LANES), lambda b, p, i: (p, b * n_q + i, 0))] + x_out,
        out_shape=[jax.ShapeDtypeStruct((t, 2 * GM_WIDTH), BF16), jax.ShapeDtypeStruct((t, GM_WIDTH), F32),
                   jax.ShapeDtypeStruct((SB_PAIRS, t, LANES), F32)] + x_shape,
        scratch_shapes=x_scratch,
        input_output_aliases={4: 0},
        compiler_params=_params(3),
    )(proj, proj, proj, hg, merged_a, *x_arrays)
    return res[0], res[1], res[2], res[3:]


def _sb_bwd(proj, raw, tot, dmerged, hg, *, batch, seq, exchange=None):
    n_q = seq // SB_TILE
    q0, k0, v0 = 2 * GM_WIDTH // LANES, 2 * GM_WIDTH // LANES + SB_PAIRS, 2 * GM_WIDTH // LANES + 2 * SB_PAIRS
    block_keys = min(SB_BLOCK, seq)
    n_sub, n_blocks = block_keys // SB_TILE, seq // block_keys

    def body(*refs):
        ins, outs, (dk_acc, dv_acc), x_refs = _riding(exchange, refs, 7, 4, 2)
        q_ref, k_ref, v_ref, raw_ref, tot_ref, dm_ref, hg_ref = ins
        dq_ref, dk_ref, dv_ref, dhg_ref = outs
        p, b, i = pl.program_id(0), pl.program_id(1), pl.program_id(2)
        finish = _ride(exchange, x_refs, (b == 0) & (p == 0) & (i == 0), (b == batch - 1) & (p == SB_PAIRS - 1) & (i == n_q - 1))
        row, col, first = _sb_masks()
        upper = (row > col).astype(BF16)
        lower = (row < col).astype(BF16)
        first_t = row < SB_HEAD_DIM

        @pl.when(jnp.logical_and(b == 0, i == 0))
        def _():
            dhg_ref[...] = jnp.zeros_like(dhg_ref)

        @pl.when(i == 0)
        def _():
            dk_acc[...] = jnp.zeros_like(dk_acc)
            dv_acc[...] = jnp.zeros_like(dv_acc)

        raw_v = raw_ref[...]
        dm = dm_ref[...].astype(F32)
        r = lax.rsqrt(_head_mean(raw_v * raw_v, first) + EPS)
        nrm = raw_v * r
        dhg_ref[...] += jnp.sum(dm * nrm, axis=0, keepdims=True)
        dn = dm * hg_ref[...]
        dout = r * (dn - nrm * _head_mean(dn * nrm, first))
        qv = q_ref[...]
        dout2 = _stack_heads(dout.astype(BF16), first)
        qs2 = _stack_heads((qv.astype(F32) * SB_SCALE).astype(BF16), first)
        q2_t = _stack_heads_t(qv.astype(F32).T, first_t)
        dout2_t = _stack_heads_t(dout.T, first_t)
        tot_v = tot_ref[0]
        tot2 = jnp.concatenate([tot_v[:, 0:1], tot_v[:, SB_HEAD_DIM:SB_HEAD_DIM + 1]], axis=0)
        diff = _stacked_col_minus_row()
        last = i // n_sub
        offset = (i - last * n_sub) * SB_TILE

        def block(jb, carry, masked):
            left, gleft, dq = carry
            parts = []
            for s in range(n_sub):
                at = pl.ds(pl.multiple_of((jb * n_sub + s) * SB_TILE, SB_TILE), SB_TILE)
                keep = diff < offset - s * SB_TILE if masked else None
                ls, l1m = _sb_logits(qs2, k_ref[at, :], keep)
                parts.append((s, at, keep, ls, l1m, jnp.sum(l1m, axis=-1, keepdims=True)))
            sums = [_tri_sums(l1m, upper) for _, _, _, _, l1m, _ in parts]
            d_weights = [_nt(dout2, v_ref[at, :]) for _, at, _, _, _, _ in parts]
            weights, gmats = [], []
            for s, at, keep, ls, l1m, total in parts:
                left = left + total
                weights.append(_sb_weights(ls, sums[s], tot2 - left, keep))
                gmats.append(d_weights[s] * weights[s])
            prefixes = [_tri_sums(g, lower) for g in gmats]
            dzs = []
            for s, at, keep, ls, l1m, total in parts:
                beta = jnp.exp(ls)
                dz = (gmats[s] * (1.0 - beta) - (gleft + prefixes[s]) * beta) * SB_SCALE
                gleft = gleft + jnp.sum(gmats[s], axis=-1, keepdims=True)
                if masked:
                    dz = jnp.where(keep, dz, 0.0)
                dzs.append(dz.astype(BF16))
            dz_all = jnp.concatenate(dzs, axis=1)
            a_all = jnp.concatenate([w.astype(BF16) for w in weights], axis=1)
            keys = pl.ds(pl.multiple_of(jb * block_keys, block_keys), block_keys)
            dk_acc[jb] += _nn(q2_t, dz_all)
            dv_acc[jb] += _nn(dout2_t, a_all)
            return left, gleft, dq + _nn(dz_all, k_ref[keys, :])

        zero = jnp.zeros((2 * SB_TILE, 1), F32)
        carry = lax.fori_loop(0, last, lambda jb, c: block(jb, c, False), (zero, zero, jnp.zeros((2 * SB_TILE, LANES), F32)))
        dq2 = block(last, carry, True)[2]
        dq_ref[...] = jnp.where(first, dq2[:SB_TILE], dq2[SB_TILE:]).astype(BF16)

        @pl.when(i == n_q - 1)
        def _():
            for jb in range(n_blocks):
                for s in range(n_sub):
                    rows = slice((jb * n_sub + s) * SB_TILE, (jb * n_sub + s + 1) * SB_TILE)
                    cols = slice(s * SB_TILE, (s + 1) * SB_TILE)
                    dk_ref[rows, :] = dk_acc[jb, :, cols].T.astype(BF16)
                    dv_ref[rows, :] = dv_acc[jb, :, cols].T.astype(BF16)

        finish()

    t = batch * seq
    blk = lambda c0: pl.BlockSpec((SB_TILE, LANES), lambda p, b, i: (b * n_q + i, c0 + p))
    kv = lambda c0: pl.BlockSpec((seq, LANES), lambda p, b, i: (b, c0 + p))
    row_spec = pl.BlockSpec((1, LANES), lambda p, b, i: (0, SB_PAIRS + p))
    tot_spec = pl.BlockSpec((1, SB_TILE, LANES), lambda p, b, i: (p, b * n_q + i, 0))
    x_in, x_out, x_shape, x_scratch, x_arrays = _riding_specs(exchange)
    res = pl.pallas_call(
        body, name="sb_bwd", grid=(SB_PAIRS, batch, n_q),
        in_specs=[blk(q0), kv(k0), kv(v0), blk(0), tot_spec, blk(SB_PAIRS), row_spec] + x_in,
        out_specs=[blk(0), kv(0), kv(0), pl.BlockSpec((1, LANES), lambda p, b, i: (0, p))] + x_out,
        out_shape=[jax.ShapeDtypeStruct((t, GM_WIDTH), BF16)] * 3 + [jax.ShapeDtypeStruct((1, GM_WIDTH), F32)] + x_shape,
        scratch_shapes=[pltpu.VMEM((n_blocks, LANES, block_keys), F32), pltpu.VMEM((n_blocks, LANES, block_keys), F32)] + x_scratch,
        compiler_params=_params(3),
    )(proj, proj, proj, raw, tot, dmerged, hg, *x_arrays)
    return res[0], res[1], res[2], res[3], res[4:]


def _x_softmax(qh, kh):
    s = _nt(qh, kh) * X_SCALE
    p = jnp.exp(s - jnp.max(s, axis=-1, keepdims=True))
    return p / jnp.sum(p, axis=-1, keepdims=True)


def _xattn_fwd(q, kv, *, batch, seq, n_mem, tq=512):
    tq = min(tq, seq)
    n_q = seq // tq
    d = X_HEADS * X_HEAD_DIM

    def body(q_ref, kv_ref, o_ref):
        for h in range(X_HEADS):
            cols = slice(h * X_HEAD_DIM, (h + 1) * X_HEAD_DIM)
            p = _x_softmax(q_ref[:, cols], kv_ref[:, cols])
            o_ref[:, cols] = _nn(p.astype(BF16), kv_ref[:, d + h * X_HEAD_DIM:d + (h + 1) * X_HEAD_DIM]).astype(BF16)

    return pl.pallas_call(
        body, name="xattn_fwd", grid=(batch, n_q),
        in_specs=[pl.BlockSpec((tq, d), lambda b, i: (b * n_q + i, 0)), pl.BlockSpec((n_mem, 2 * d), lambda b, i: (b, 0))],
        out_specs=pl.BlockSpec((tq, d), lambda b, i: (b * n_q + i, 0)),
        out_shape=jax.ShapeDtypeStruct((batch * seq, d), BF16),
        compiler_params=_params(2),
    )(q, kv)


def _xattn_bwd(q, kv, do, *, batch, seq, n_mem, tq=512):
    tq = min(tq, seq)
    n_q = seq // tq
    d = X_HEADS * X_HEAD_DIM

    def body(q_ref, kv_ref, do_ref, dq_ref, dkv_ref, acc):
        i = pl.program_id(1)

        @pl.when(i == 0)
        def _():
            acc[...] = jnp.zeros_like(acc)

        for h in range(X_HEADS):
            cols = slice(h * X_HEAD_DIM, (h + 1) * X_HEAD_DIM)
            vcols = slice(d + h * X_HEAD_DIM, d + (h + 1) * X_HEAD_DIM)
            qh, kh, vh, doh = q_ref[:, cols], kv_ref[:, cols], kv_ref[:, vcols], do_ref[:, cols]
            p = _x_softmax(qh, kh)
            dp = _nt(doh, vh)
            acc[:, vcols] += _tn(p.astype(BF16), doh)
            ds = (p * (dp - jnp.sum(dp * p, axis=-1, keepdims=True)) * X_SCALE).astype(BF16)
            dq_ref[:, cols] = _nn(ds, kh).astype(BF16)
            acc[:, cols] += _tn(ds, qh)

        @pl.when(i == n_q - 1)
        def _():
            dkv_ref[...] = acc[...].astype(BF16)

    return pl.pallas_call(
        body, name="xattn_bwd", grid=(batch, n_q),
        in_specs=[pl.BlockSpec((tq, d), lambda b, i: (b * n_q + i, 0)), pl.BlockSpec((n_mem, 2 * d), lambda b, i: (b, 0)),
                  pl.BlockSpec((tq, d), lambda b, i: (b * n_q + i, 0))],
        out_specs=[pl.BlockSpec((tq, d), lambda b, i: (b * n_q + i, 0)), pl.BlockSpec((n_mem, 2 * d), lambda b, i: (b, 0))],
        out_shape=[jax.ShapeDtypeStruct((batch * seq, d), BF16), jax.ShapeDtypeStruct((batch * n_mem, 2 * d), BF16)],
        scratch_shapes=[pltpu.VMEM((n_mem, 2 * d), F32)],
        compiler_params=_params(2),
    )(q, kv, do)


def _loss_head(h, target, g, *, tm=512):
    t, d = h.shape
    tm = min(tm, t)
    n_steps = t // tm

    def body(h_ref, t_ref, g_ref, dh_ref, dg_ref, loss_ref, sq_acc):
        step = pl.program_id(0)

        @pl.when(step == 0)
        def _():
            dg_ref[...] = jnp.zeros_like(dg_ref)
            sq_acc[...] = jnp.zeros_like(sq_acc)

        hv = h_ref[...]
        gv = g_ref[...]
        r = _rstd(hv)
        xh = hv * r
        err = xh * gv - t_ref[...]
        sq_acc[...] += jnp.sum(err * err, axis=0, keepdims=True)
        dy = err * (1.0 / d)
        dg_ref[...] += jnp.sum(dy * xh, axis=0, keepdims=True)
        dxh = dy * gv
        dh_ref[...] = r * (dxh - xh * jnp.mean(dxh * xh, axis=-1, keepdims=True))

        @pl.when(step == n_steps - 1)
        def _():
            total = jnp.sum(sq_acc[...], axis=-1, keepdims=True) * (0.5 / d)
            loss_ref[...] = jnp.broadcast_to(total, loss_ref.shape)

    return pl.pallas_call(
        body, name="loss_head", grid=(n_steps,),
        in_specs=[pl.BlockSpec((tm, d), lambda i: (i, 0)), pl.BlockSpec((tm, d), lambda i: (i, 0)), pl.BlockSpec((1, d), lambda i: (0, 0))],
        out_specs=[pl.BlockSpec((tm, d), lambda i: (i, 0)), pl.BlockSpec((1, d), lambda i: (0, 0)), pl.BlockSpec((1, LANES), lambda i: (0, 0))],
        out_shape=[jax.ShapeDtypeStruct((t, d), F32), jax.ShapeDtypeStruct((1, d), F32), jax.ShapeDtypeStruct((1, LANES), F32)],
        scratch_shapes=[pltpu.VMEM((1, d), F32)],
        compiler_params=_params(1),
    )(h, target, g)


def _my_index():
    return 4 * lax.axis_index("x") + 2 * lax.axis_index("y") + lax.axis_index("c")


def _peers():
    x, y, c = lax.axis_index("x"), lax.axis_index("y"), lax.axis_index("c")
    out = []
    for rel in range(1, N_DEV):
        dx, dy, dc = (rel >> 2) & 1, (rel >> 1) & 1, rel & 1
        px, py, pc = x ^ dx, y ^ dy, c ^ dc
        out.append(((px, py, pc), 4 * px + 2 * py + pc))
    return out


class _Exchange:
    def __init__(self, arrays, scatter):
        self.arrays, self.scatter, self.n = list(arrays), scatter, len(arrays)
        any_spec = pl.BlockSpec(memory_space=pl.ANY)
        self.in_specs = [any_spec] * self.n
        self.out_specs = [any_spec] * self.n
        self.out_shape = [jax.ShapeDtypeStruct((N_DEV,) + tuple(a.shape[-2:]), a.dtype) for a in self.arrays]
        n_peer = N_DEV - 1
        self.scratch = [pltpu.SemaphoreType.DMA((self.n, n_peer)), pltpu.SemaphoreType.DMA((self.n, n_peer)),
                        pltpu.SemaphoreType.DMA((self.n,))]

    def _copies(self, srcs, dsts, sems, arriving):
        send_sems, recv_sems, local_sems = sems
        me = _my_index()
        local, remote = [], []
        for w in range(self.n):
            if not arriving:
                local.append(pltpu.make_async_copy(srcs[w].at[me] if self.scatter else srcs[w], dsts[w].at[me], local_sems.at[w]))
            for rel, (pos, idx) in enumerate(_peers()):
                remote.append(pltpu.make_async_remote_copy(
                    src_ref=srcs[w].at[idx] if self.scatter else srcs[w], dst_ref=dsts[w].at[idx if arriving else me],
                    send_sem=send_sems.at[w, rel], recv_sem=recv_sems.at[w, rel], device_id=pos, device_id_type=MESH))
        return local, remote

    def start(self, srcs, dsts, sems):
        local, sends = self._copies(srcs, dsts, sems, arriving=False)
        for cp in local + sends:
            cp.start()

    def wait(self, srcs, dsts, sems):
        for cp in self._copies(srcs, dsts, sems, arriving=True)[1]:
            cp.wait_recv()
        local, sends = self._copies(srcs, dsts, sems, arriving=False)
        for cp in sends:
            cp.wait_send()
        for cp in local:
            cp.wait()

    def run(self, name):
        n = self.n

        def body(*refs):
            srcs, dsts, sems = refs[:n], refs[n:2 * n], refs[2 * n:]
            self.start(srcs, dsts, sems)
            self.wait(srcs, dsts, sems)

        return pl.pallas_call(body, name=name, in_specs=self.in_specs, out_specs=self.out_specs, out_shape=self.out_shape,
                              scratch_shapes=self.scratch)(*self.arrays)


def _all_reduce_small(part):
    rows = part.shape[0]
    n_peer = N_DEV - 1

    def body(p_ref, o_ref, buf, send_sems, recv_sems):
        me = _my_index()
        peers = _peers()
        buf[me] = p_ref[...]
        sends = [pltpu.make_async_remote_copy(src_ref=p_ref, dst_ref=buf.at[me], send_sem=send_sems.at[rel], recv_sem=recv_sems.at[rel],
                                              device_id=peers[rel][0], device_id_type=MESH) for rel in range(n_peer)]
        for cp in sends:
            cp.start()
        for rel in range(n_peer):
            pltpu.make_async_remote_copy(src_ref=p_ref, dst_ref=buf.at[peers[rel][1]], send_sem=send_sems.at[rel], recv_sem=recv_sems.at[rel],
                                         device_id=peers[rel][0], device_id_type=MESH).wait_recv()
        for cp in sends:
            cp.wait_send()
        total = buf[0]
        for dev in range(1, N_DEV):
            total = total + buf[dev]
        o_ref[...] = total

    vmem = pl.BlockSpec(memory_space=pltpu.VMEM)
    return pl.pallas_call(
        body, name="all_reduce_small", in_specs=[vmem], out_specs=vmem, out_shape=jax.ShapeDtypeStruct(part.shape, F32),
        scratch_shapes=[pltpu.VMEM((N_DEV, rows, LANES), F32), pltpu.SemaphoreType.DMA((n_peer,)), pltpu.SemaphoreType.DMA((n_peer,))],
        compiler_params=pltpu.CompilerParams(has_side_effects=True, vmem_limit_bytes=VMEM_LIMIT),
    )(part)


def _adamw_math(w, g, m, v):
    m_new = ADAM_B1 * m + (1.0 - ADAM_B1) * g
    v_new = ADAM_B2 * v + (1.0 - ADAM_B2) * (g * g)
    m_hat = m_new / (1.0 - ADAM_B1 ** ADAM_STEP)
    v_hat = v_new / (1.0 - ADAM_B2 ** ADAM_STEP)
    delta = -ADAM_LR * (m_hat / (jnp.sqrt(v_hat) + ADAM_EPS) + ADAM_WD * w)
    return delta, m_new, v_new


def _adamw(parts, w, m, v, *, name, tr=64):
    rows, cols = w.shape
    tr = min(tr, rows)
    assert rows % tr == 0
    stacked = parts.ndim == 3

    def body(p_ref, w_ref, m_ref, v_ref, g_ref, d_ref, mo_ref, vo_ref):
        if stacked:
            g = p_ref[0].astype(F32)
            for dev in range(1, N_DEV):
                g = g + p_ref[dev].astype(F32)
        else:
            g = p_ref[...]
        delta, m_new, v_new = _adamw_math(w_ref[...], g, m_ref[...], v_ref[...])
        g_ref[...] = g
        d_ref[...] = delta
        mo_ref[...] = m_new
        vo_ref[...] = v_new

    tile = pl.BlockSpec((tr, cols), lambda i: (i, 0))
    p_spec = pl.BlockSpec((N_DEV, tr, cols), lambda i: (0, i, 0)) if stacked else tile
    return pl.pallas_call(
        body, name=name, grid=(rows // tr,), in_specs=[p_spec, tile, tile, tile], out_specs=[tile] * 4,
        out_shape=[jax.ShapeDtypeStruct((rows, cols), F32)] * 4, compiler_params=_params(1),
    )(parts, w, m, v)


_LATER = ("w_out", "w_cq", "w_ckv", "w_co", "w_ff1", "w_ff2")


def _as_rows(stacked):
    return stacked.reshape(-1, stacked.shape[-1])


def _local_step(x, mem, target, small, shards):
    batch, seq, d = x.shape
    n_mem = mem.shape[1]
    t = batch * seq
    x2, mem2, tgt2 = x.reshape(t, d), mem.reshape(batch * n_mem, d), target.reshape(t, d)
    g_mix, g_cross, g_mem, g_ffn, g_final = (small[k] for k in ("norm_mix_g", "norm_cross_g", "norm_mem_g", "norm_ffn_g", "norm_final_g"))
    gv, hg, w_sp, b_sp_t = small["gm_v_norm_g"], small["head_norm_g"], small["w_spatial"], small["b_spatial_t"]

    win_t = _as_rows(_Exchange([shards["w_in"]], scatter=False).run("gather_w_in")[0])
    proj, xn = _norm_mm(x2, g_mix, win_t, mode="nt", name="proj_fwd", tn=512)
    merged_a = _gmlp_fwd(proj, w_sp, b_sp_t, gv, hg)
    merged, sb_raw, sb_tot, gathered = _sb_fwd(proj, merged_a, hg, batch=batch, seq=seq,
                                               exchange=_Exchange([shards[n] for n in _LATER], scatter=False))
    wout, wcq, wckv_t, wco, wff1_t, wff2 = (_as_rows(g) for g in gathered)
    h1 = _mm(merged, wout, mode="nn", out_dtype=F32, name="mix_out_fwd", epi=_epi_residual, epi_ins=(x2,))
    qx, hn1 = _norm_mm(h1, g_cross, wcq, mode="nn", name="xq_fwd")
    kvx, memn = _norm_mm(mem2, g_mem, wckv_t, mode="nt", name="xkv_fwd")
    o = _xattn_fwd(qx, kvx, batch=batch, seq=seq, n_mem=n_mem)
    h2 = _mm(o, wco, mode="nn", out_dtype=F32, name="xo_fwd", epi=_epi_residual, epi_ins=(h1,))
    fpre, hn2 = _norm_mm(h2, g_ffn, wff1_t, mode="nt", name="ff1_fwd")
    h3 = _mm(fpre, wff2, mode="nn", out_dtype=F32, name="ff2_fwd", a_fn=_relu2, epi=_epi_residual, epi_ins=(h2,))
    dh3, dg_final, loss_row = _loss_head(h3, tgt2, g_final)

    dpre = _mm(dh3, wff2, mode="nt", out_dtype=BF16, name="ff2_bwd_x", epi=_epi_relu2_grad, epi_ins=(fpre,))
    d_wff2 = _mm(fpre, dh3, mode="tn", out_dtype=BF16, name="ff2_bwd_w", a_fn=_relu2)
    d_wff1_t = _mm(dpre, hn2, mode="tn", out_dtype=BF16, name="ff1_bwd_w")
    dh2, dg_ffn = _mm(dpre, wff1_t, mode="nn", out_dtype=F32, name="ff1_bwd_x", tm=512, epi=_epi_rms_bwd,
                      epi_ins=(h2, dh3), vec_ins=(g_ffn,), aux=True)
    do = _mm(dh2, wco, mode="nt", out_dtype=BF16, name="xo_bwd_x")
    d_wco = _mm(o, dh2, mode="tn", out_dtype=BF16, name="xo_bwd_w")
    dqx, dkvx = _xattn_bwd(qx, kvx, do, batch=batch, seq=seq, n_mem=n_mem)
    d_wcq = _mm(hn1, dqx, mode="tn", out_dtype=BF16, name="xq_bwd_w")
    dh1, dg_cross = _mm(dqx, wcq, mode="nt", out_dtype=F32, name="xq_bwd_x", tm=512, epi=_epi_rms_bwd,
                        epi_ins=(h1, dh2), vec_ins=(g_cross,), aux=True)
    d_wckv_t = _mm(dkvx, memn, mode="tn", out_dtype=BF16, name="xkv_bwd_w")
    _, dg_mem = _mm(dkvx, wckv_t, mode="nn", out_dtype=BF16, name="xkv_bwd_x", tm=512, epi=_epi_rms_gain_only,
                    epi_ins=(mem2,), vec_ins=(g_mem,), aux=True)
    dmerged = _mm(dh1, wout, mode="nt", out_dtype=BF16, name="mix_out_bwd_x")
    d_wout = _mm(merged, dh1, mode="tn", out_dtype=BF16, name="mix_out_bwd_w")
    dp_a, d_wsp, d_bsp_t, d_gv, d_hg_a = _gmlp_bwd(proj, dmerged, w_sp, b_sp_t, gv, hg)
    d_later = {"w_out": d_wout, "w_cq": d_wcq, "w_ckv": d_wckv_t, "w_co": d_wco, "w_ff1": d_wff1_t, "w_ff2": d_wff2}
    scatter = _Exchange([d_later[n].reshape(N_DEV, -1, d) for n in _LATER], scatter=True)
    dq, dk, dv, d_hg_b, received = _sb_bwd(proj, sb_raw, sb_tot, dmerged, hg, batch=batch, seq=seq, exchange=scatter)
    dproj = jnp.concatenate([dp_a, dq, dk, dv], axis=1)
    d_win_t = _mm(dproj, xn, mode="tn", out_dtype=BF16, name="proj_bwd_w", tm=512)
    dx, dg_mix = _mm(dproj, win_t, mode="nn", out_dtype=F32, name="proj_bwd_x", tm=512, tk=512, epi=_epi_rms_bwd,
                     epi_ins=(x2, dh1), vec_ins=(g_mix,), aux=True)

    d_small = {"norm_mix_g": dg_mix, "gm_v_norm_g": d_gv, "w_spatial": d_wsp, "b_spatial_t": d_bsp_t, "head_norm_g": jnp.concatenate([d_hg_a, d_hg_b], axis=1),
               "norm_cross_g": dg_cross, "norm_mem_g": dg_mem, "norm_ffn_g": dg_ffn, "norm_final_g": dg_final}
    d_big = dict(zip(_LATER, received))
    d_big["w_in"] = _Exchange([d_win_t.reshape(N_DEV, -1, d)], scatter=True).run("scatter_w_in")[0]
    return loss_row, dx.reshape(batch, seq, d), d_small, d_big


_BIG = ("w_in", "w_out", "w_cq", "w_ckv", "w_co", "w_ff1", "w_ff2")
_BIG_TRANSPOSED = ("w_in", "w_ckv", "w_ff1")
_SMALL = ("norm_mix_g", "gm_v_norm_g", "w_spatial", "b_spatial", "head_norm_g", "norm_cross_g", "norm_mem_g", "norm_ffn_g", "norm_final_g")
_NAMES = ("norm_mix_g", "w_in", "gm_v_norm_g", "w_spatial", "b_spatial", "head_norm_g", "w_out", "norm_cross_g", "norm_mem_g",
          "w_cq", "w_ckv", "w_co", "norm_ffn_g", "w_ff1", "w_ff2", "norm_final_g")


def _rows_of(a):
    r = a.reshape(-1, LANES)
    pad = (-r.shape[0]) % 8
    return jnp.pad(r, ((0, pad), (0, 0))) if pad else r


def _shard2d(name, a):
    a = a[0]
    return a.T if name in _BIG_TRANSPOSED else a


def kernel(x, mem, norm_mix_g, w_in, gm_v_norm_g, w_spatial, b_spatial, head_norm_g, w_out, norm_cross_g, norm_mem_g, w_cq, w_ckv, w_co, norm_ffn_g, w_ff1, w_ff2, norm_final_g, loss_target, m_norm_mix_g, m_w_in, m_gm_v_norm_g, m_w_spatial, m_b_spatial, m_head_norm_g, m_w_out, m_norm_cross_g, m_norm_mem_g, m_w_cq, m_w_ckv, m_w_co, m_norm_ffn_g, m_w_ff1, m_w_ff2, m_norm_final_g, v_norm_mix_g, v_w_in, v_gm_v_norm_g, v_w_spatial, v_b_spatial, v_head_norm_g, v_w_out, v_norm_cross_g, v_norm_mem_g, v_w_cq, v_w_ckv, v_w_co, v_norm_ffn_g, v_w_ff1, v_w_ff2, v_norm_final_g):
    weights = dict(norm_mix_g=norm_mix_g, w_in=w_in, gm_v_norm_g=gm_v_norm_g, w_spatial=w_spatial, b_spatial=b_spatial,
                   head_norm_g=head_norm_g, w_out=w_out, norm_cross_g=norm_cross_g, norm_mem_g=norm_mem_g, w_cq=w_cq, w_ckv=w_ckv,
                   w_co=w_co, norm_ffn_g=norm_ffn_g, w_ff1=w_ff1, w_ff2=w_ff2, norm_final_g=norm_final_g)
    mom1 = dict(norm_mix_g=m_norm_mix_g, w_in=m_w_in, gm_v_norm_g=m_gm_v_norm_g, w_spatial=m_w_spatial, b_spatial=m_b_spatial,
                head_norm_g=m_head_norm_g, w_out=m_w_out, norm_cross_g=m_norm_cross_g, norm_mem_g=m_norm_mem_g, w_cq=m_w_cq,
                w_ckv=m_w_ckv, w_co=m_w_co, norm_ffn_g=m_norm_ffn_g, w_ff1=m_w_ff1, w_ff2=m_w_ff2, norm_final_g=m_norm_final_g)
    mom2 = dict(norm_mix_g=v_norm_mix_g, w_in=v_w_in, gm_v_norm_g=v_gm_v_norm_g, w_spatial=v_w_spatial, b_spatial=v_b_spatial,
                head_norm_g=v_head_norm_g, w_out=v_w_out, norm_cross_g=v_norm_cross_g, norm_mem_g=v_norm_mem_g, w_cq=v_w_cq,
                w_ckv=v_w_ckv, w_co=v_w_co, norm_ffn_g=v_norm_ffn_g, w_ff1=v_w_ff1, w_ff2=v_w_ff2, norm_final_g=v_norm_final_g)

    shards = {n: _shard2d(n, weights[n]).astype(BF16) for n in _BIG}
    small = {n: weights[n].reshape(1, -1) for n in _SMALL if n not in ("w_spatial", "b_spatial")}
    small["w_spatial"] = w_spatial[0]
    small["b_spatial_t"] = b_spatial[0].T
    loss_row, grad_x, d_small, d_big = _local_step(x, mem, loss_target, small, shards)

    d_small["b_spatial"] = d_small.pop("b_spatial_t").T
    packed = jnp.concatenate([_rows_of(d_small[n]) for n in _SMALL] + [jnp.pad(loss_row, ((0, 7), (0, 0)))], axis=0)
    summed = _all_reduce_small(packed)

    grads, deltas, new_m, new_v = {}, {}, {}, {}
    for n in _BIG:
        outs = _adamw(d_big[n], _shard2d(n, weights[n]), _shard2d(n, mom1[n]), _shard2d(n, mom2[n]), name="adamw_" + n)
        outs = [o.T if n in _BIG_TRANSPOSED else o for o in outs]
        grads[n], deltas[n], new_m[n], new_v[n] = (o[None] for o in outs)
    pack = lambda src: jnp.concatenate([_rows_of(src[n]) for n in _SMALL], axis=0)
    n_small_rows = sum(_rows_of(weights[n]).shape[0] for n in _SMALL)
    outs = _adamw(summed[:n_small_rows], pack(weights), pack(mom1), pack(mom2), name="adamw_small", tr=n_small_rows)
    at = 0
    for n in _SMALL:
        used = weights[n].size // LANES
        for dst, o in zip((grads, deltas, new_m, new_v), outs):
            dst[n] = o[at:at + used].reshape(weights[n].shape)
        at += _rows_of(weights[n]).shape[0]
    loss = summed[n_small_rows, 0]
    return (loss, grad_x, *[grads[n] for n in _NAMES], *[deltas[n] for n in _NAMES], *[new_m[n] for n in _NAMES],
            *[new_v[n] for n in _NAMES])
```

```python
import functools
import math

import jax
import jax.numpy as jnp
from jax import lax
from jax.experimental import pallas as pl
from jax.experimental.pallas import tpu as pltpu

F32 = jnp.float32
BF16 = jnp.bfloat16
EPS = 1e-6
N_DEV = 8
LANES = 128
CHUNK = 128
GM_GROUPS = 4
GM_WIDTH = 512
SB_PAIRS = 4
SB_HEAD_DIM = 64
SB_SCALE = 0.125
SB_TILE = 128
SB_BLOCK = 512
SB_QUERIES_FWD = 256
SB_QUERIES_BWD = 128
X_HEADS = 4
X_HEAD_DIM = 256
X_SCALE = 1.0 / 16.0
VMEM_LIMIT = 56 * 1024 * 1024
ADAM_LR, ADAM_B1, ADAM_B2, ADAM_EPS, ADAM_WD, ADAM_STEP = 0.001, 0.9, 0.999, 1e-08, 0.01, 10
MESH = pl.DeviceIdType.MESH


def _params(n_axes):
    return pltpu.CompilerParams(dimension_semantics=("arbitrary",) * n_axes, vmem_limit_bytes=VMEM_LIMIT)


def _dot(a, b, dims):
    return lax.dot_general(a, b, (dims, ((), ())), preferred_element_type=F32)


def _nn(a, b):
    return _dot(a, b, ((1,), (0,)))


def _nt(a, b):
    return _dot(a, b, ((1,), (1,)))


def _tn(a, b):
    return _dot(a, b, ((0,), (0,)))


_MODES = {"nn": _nn, "nt": _nt, "tn": _tn}


def _rstd(x):
    return lax.rsqrt(jnp.mean(x * x, axis=-1, keepdims=True) + EPS)


def _gelu(x):
    c = math.sqrt(2.0 / math.pi)
    t = jnp.tanh(c * (x + 0.044715 * x * x * x))
    return 0.5 * x * (1.0 + t)


def _gelu_grad(x):
    c = math.sqrt(2.0 / math.pi)
    t = jnp.tanh(c * (x + 0.044715 * x * x * x))
    return 0.5 * (1.0 + t) + 0.5 * x * (1.0 - t * t) * c * (1.0 + 3 * 0.044715 * x * x)


def _split_bf16(x):
    hi = x.astype(BF16)
    lo = (x - hi.astype(F32)).astype(BF16)
    return hi, lo


def _mm(a, b, *, mode, out_dtype, name, tm=1024, tn=1024, tk=1024, a_fn=None, epi=None, epi_ins=(), vec_ins=(), aux=False,
        exchange=None):
    if mode == "nn":
        (m, k), (k2, n) = a.shape, b.shape
    elif mode == "nt":
        (m, k), (n, k2) = a.shape, b.shape
    else:
        (k, m), (k2, n) = a.shape, b.shape
    assert k == k2, (a.shape, b.shape, mode)
    tm, tn, tk = min(tm, m), min(tn, n), min(tk, k)
    assert m % tm == 0 and n % tn == 0 and k % tk == 0, (m, n, k, tm, tn, tk)
    n_m, n_n, n_k = m // tm, n // tn, k // tk
    assert not aux or n_n == 1
    dot = _MODES[mode]
    n_epi, n_vec = len(epi_ins), len(vec_ins)

    def body(*refs):
        ins, outs, scratch, x_refs = _riding(exchange, refs, 2 + n_epi + n_vec, 2 if aux else 1, 1 if n_k > 1 else 0)
        a_ref, b_ref, epi_refs = ins[0], ins[1], ins[2:]
        o_ref = outs[0]
        aux_ref = outs[1] if aux else None
        acc_ref = scratch[0] if n_k > 1 else None
        i, j, kk = pl.program_id(0), pl.program_id(1), pl.program_id(2)
        ride_done = _ride(exchange, x_refs, (i == 0) & (j == 0) & (kk == 0), (i == n_m - 1) & (j == n_n - 1) & (kk == n_k - 1))
        av = a_ref[...]
        if a_fn is not None:
            av = a_fn(av)
        part = dot(av.astype(BF16), b_ref[...].astype(BF16))

        def finish(acc):
            if epi is None:
                o_ref[...] = acc.astype(out_dtype)
                return
            res = epi(acc, *[r[...] for r in epi_refs])
            if aux:
                res, row = res

                @pl.when(i == 0)
                def _():
                    aux_ref[...] = row

                @pl.when(i != 0)
                def _():
                    aux_ref[...] += row
            o_ref[...] = res.astype(out_dtype)

        if n_k == 1:
            finish(part)
        else:
            @pl.when(kk == 0)
            def _():
                acc_ref[...] = part

            @pl.when(kk != 0)
            def _():
                acc_ref[...] += part

            @pl.when(kk == n_k - 1)
            def _():
                finish(acc_ref[...])

        ride_done()

    if mode == "tn":
        a_spec = pl.BlockSpec((tk, tm), lambda i, j, kk: (kk, i))
    else:
        a_spec = pl.BlockSpec((tm, tk), lambda i, j, kk: (i, kk))
    if mode == "nt":
        b_spec = pl.BlockSpec((tn, tk), lambda i, j, kk: (j, kk))
    else:
        b_spec = pl.BlockSpec((tk, tn), lambda i, j, kk: (kk, j))
    tile_spec = pl.BlockSpec((tm, tn), lambda i, j, kk: (i, j))
    row_spec = pl.BlockSpec((1, tn), lambda i, j, kk: (0, j))
    out_shape = [jax.ShapeDtypeStruct((m, n), out_dtype)]
    out_specs = [tile_spec]
    if aux:
        out_shape.append(jax.ShapeDtypeStruct((1, n), F32))
        out_specs.append(row_spec)
    x_in, x_out, x_shape, x_scratch, x_arrays = _riding_specs(exchange)
    res = pl.pallas_call(
        body, name=name, grid=(n_m, n_n, n_k),
        in_specs=[a_spec, b_spec] + [tile_spec] * n_epi + [row_spec] * n_vec + x_in,
        out_specs=out_specs + x_out, out_shape=out_shape + x_shape,
        scratch_shapes=([pltpu.VMEM((tm, tn), F32)] if n_k > 1 else []) + x_scratch,
        compiler_params=_params(3),
    )(a, b, *epi_ins, *vec_ins, *x_arrays)
    if exchange is not None:
        return tuple(res)
    return res if aux else res[0]


def _norm_mm(x, g, w, *, mode, name, tm=1024, tn=1024):
    m, d = x.shape
    n = w.shape[0] if mode == "nt" else w.shape[1]
    tm, tn = min(tm, m), min(tn, n)
    assert m % tm == 0 and n % tn == 0
    dot = _MODES[mode]

    def body(x_ref, g_ref, w_ref, o_ref, xn_ref, xn_s):
        @pl.when(pl.program_id(1) == 0)
        def _():
            xv = x_ref[...]
            xn = (xv * _rstd(xv) * g_ref[...]).astype(BF16)
            xn_s[...] = xn
            xn_ref[...] = xn

        o_ref[...] = dot(xn_s[...], w_ref[...]).astype(BF16)

    w_spec = pl.BlockSpec((tn, d), lambda i, j: (j, 0)) if mode == "nt" else pl.BlockSpec((d, tn), lambda i, j: (0, j))
    return pl.pallas_call(
        body, name=name, grid=(m // tm, n // tn),
        in_specs=[pl.BlockSpec((tm, d), lambda i, j: (i, 0)), pl.BlockSpec((1, d), lambda i, j: (0, 0)), w_spec],
        out_specs=[pl.BlockSpec((tm, tn), lambda i, j: (i, j)), pl.BlockSpec((tm, d), lambda i, j: (i, 0))],
        out_shape=[jax.ShapeDtypeStruct((m, n), BF16), jax.ShapeDtypeStruct((m, d), BF16)],
        scratch_shapes=[pltpu.VMEM((tm, d), BF16)],
        compiler_params=_params(2),
    )(x, g, w)


def _epi_residual(acc, res):
    return res + acc


def _epi_relu2_grad(acc, pre):
    return acc * (2.0 * jnp.maximum(pre.astype(F32), 0.0))


def _relu2(pre):
    r = jnp.maximum(pre.astype(F32), 0.0)
    return r * r


def _epi_rms_bwd(acc, h, dres, g):
    r = _rstd(h)
    xh = h * r
    dxh = acc * g
    dh = dres + r * (dxh - xh * jnp.mean(dxh * xh, axis=-1, keepdims=True))
    return dh, jnp.sum(acc * xh, axis=0, keepdims=True)


def _epi_rms_gain_only(acc, h, g):
    return acc, jnp.sum(acc * (h * _rstd(h)), axis=0, keepdims=True)


def _tril(n):
    row = lax.broadcasted_iota(jnp.int32, (n, n), 0)
    col = lax.broadcasted_iota(jnp.int32, (n, n), 1)
    return col <= row


def _gmlp_fwd(proj, w_sp, b_sp_t, gv, hg, *, rows=512):
    t = proj.shape[0]
    rows = min(rows, t)
    n_c = rows // CHUNK

    def body(u_ref, v_ref, w_ref, bt_ref, gv_ref, hg_ref, m_ref):
        keep = _tril(CHUNK)
        for g in range(GM_GROUPS):
            cols = slice(g * LANES, (g + 1) * LANES)
            wg = jnp.where(keep, w_ref[g], 0.0).astype(BF16)
            u = _gelu(u_ref[:, cols].astype(F32))
            v = _gelu(v_ref[:, cols].astype(F32))
            vn = (v * _rstd(v) * gv_ref[:, cols]).astype(BF16)
            bias = bt_ref[:, g:g + 1]
            for c in range(n_c):
                rs = slice(c * CHUNK, (c + 1) * CHUNK)
                mixed = _nn(wg, vn[rs]) + bias
                a = u[rs] * mixed
                m_ref[rs, cols] = (a * _rstd(a) * hg_ref[:, cols]).astype(BF16)

    full = lambda shape: pl.BlockSpec(shape, lambda i: (0,) * len(shape))
    return pl.pallas_call(
        body, name="gmlp_fwd", grid=(t // rows,),
        in_specs=[pl.BlockSpec((rows, GM_WIDTH), lambda i: (i, 0)), pl.BlockSpec((rows, GM_WIDTH), lambda i: (i, 1)),
                  full((GM_GROUPS, CHUNK, CHUNK)), full((CHUNK, GM_GROUPS)), full((1, GM_WIDTH)), full((1, GM_WIDTH))],
        out_specs=pl.BlockSpec((rows, GM_WIDTH), lambda i: (i, 0)),
        out_shape=jax.ShapeDtypeStruct((t, 2 * GM_WIDTH), BF16),
        compiler_params=_params(1),
    )(proj, proj, w_sp, b_sp_t, gv, hg)


def _gmlp_bwd(proj, dmerged, w_sp, b_sp_t, gv, hg, *, rows=512):
    t = proj.shape[0]
    rows = min(rows, t)
    n_c = rows // CHUNK
    n_steps = t // rows

    def body(u_ref, v_ref, dm_ref, w_ref, bt_ref, gv_ref, hg_ref, dp_ref, dw_ref, dbt_ref, dgv_ref, dhg_ref, db_acc):
        step = pl.program_id(0)
        keep = _tril(CHUNK)

        @pl.when(step == 0)
        def _():
            dw_ref[...] = jnp.zeros_like(dw_ref)
            db_acc[...] = jnp.zeros_like(db_acc)
            dgv_ref[...] = jnp.zeros_like(dgv_ref)
            dhg_ref[...] = jnp.zeros_like(dhg_ref)

        for g in range(GM_GROUPS):
            cols = slice(g * LANES, (g + 1) * LANES)
            wg = jnp.where(keep, w_ref[g], 0.0).astype(BF16)
            u_pre = u_ref[:, cols].astype(F32)
            v_pre = v_ref[:, cols].astype(F32)
            u = _gelu(u_pre)
            v = _gelu(v_pre)
            r = _rstd(v)
            xh = v * r
            gvg = gv_ref[:, cols]
            hgg = hg_ref[:, cols]
            vn = (xh * gvg).astype(BF16)
            bias = bt_ref[:, g:g + 1]
            dm = dm_ref[:, cols].astype(F32)
            du_parts, dvn_parts = [], []
            dw = jnp.zeros((CHUNK, CHUNK), F32)
            db = jnp.zeros((CHUNK, LANES), F32)
            dhg = jnp.zeros((1, LANES), F32)
            for c in range(n_c):
                rs = slice(c * CHUNK, (c + 1) * CHUNK)
                mixed = _nn(wg, vn[rs]) + bias
                a = u[rs] * mixed
                ra = _rstd(a)
                an = a * ra
                dhg = dhg + jnp.sum(dm[rs] * an, axis=0, keepdims=True)
                dan = dm[rs] * hgg
                da = ra * (dan - an * jnp.mean(dan * an, axis=-1, keepdims=True))
                du_parts.append(da * mixed)
                dmixed = da * u[rs]
                db = db + dmixed
                dmb = dmixed.astype(BF16)
                dw = dw + _nt(dmb, vn[rs])
                dvn_parts.append(_tn(wg, dmb))
            du = jnp.concatenate(du_parts, axis=0)
            dvn = jnp.concatenate(dvn_parts, axis=0)
            dw_ref[g] += dw
            db_acc[g] += db
            dhg_ref[:, cols] += dhg
            dgv_ref[:, cols] += jnp.sum(dvn * xh, axis=0, keepdims=True)
            dxh = dvn * gvg
            dv = r * (dxh - xh * jnp.mean(dxh * xh, axis=-1, keepdims=True))
            dp_ref[:, cols] = (du * _gelu_grad(u_pre)).astype(BF16)
            dp_ref[:, GM_WIDTH + g * LANES:GM_WIDTH + (g + 1) * LANES] = (dv * _gelu_grad(v_pre)).astype(BF16)

        @pl.when(step == n_steps - 1)
        def _():
            for g in range(GM_GROUPS):
                dw_ref[g] = jnp.where(keep, dw_ref[g], 0.0)
                dbt_ref[:, g:g + 1] = jnp.sum(db_acc[g], axis=-1, keepdims=True)

    full = lambda shape: pl.BlockSpec(shape, lambda i: (0,) * len(shape))
    return pl.pallas_call(
        body, name="gmlp_bwd", grid=(n_steps,),
        in_specs=[pl.BlockSpec((rows, GM_WIDTH), lambda i: (i, 0)), pl.BlockSpec((rows, GM_WIDTH), lambda i: (i, 1)),
                  pl.BlockSpec((rows, GM_WIDTH), lambda i: (i, 0)),
                  full((GM_GROUPS, CHUNK, CHUNK)), full((CHUNK, GM_GROUPS)), full((1, GM_WIDTH)), full((1, GM_WIDTH))],
        out_specs=[pl.BlockSpec((rows, 2 * GM_WIDTH), lambda i: (i, 0)), full((GM_GROUPS, CHUNK, CHUNK)),
                   full((CHUNK, GM_GROUPS)), full((1, GM_WIDTH)), full((1, GM_WIDTH))],
        out_shape=[jax.ShapeDtypeStruct((t, 2 * GM_WIDTH), BF16), jax.ShapeDtypeStruct((GM_GROUPS, CHUNK, CHUNK), F32),
                   jax.ShapeDtypeStruct((CHUNK, GM_GROUPS), F32), jax.ShapeDtypeStruct((1, GM_WIDTH), F32),
                   jax.ShapeDtypeStruct((1, GM_WIDTH), F32)],
        scratch_shapes=[pltpu.VMEM((GM_GROUPS, CHUNK, LANES), F32)],
        compiler_params=_params(1),
    )(proj, proj, dmerged, w_sp, b_sp_t, gv, hg)


def _sb_logits(z, strict):
    ls = jnp.minimum(z, 0.0) - jnp.log(1.0 + jnp.exp(-jnp.abs(z)))
    l1m = ls - z
    if strict is not None:
        l1m = jnp.where(strict, l1m, 0.0)
    return ls, l1m


def _tri_sums(x, tri):
    hi, lo = _split_bf16(x)
    return _nn(jnp.concatenate([hi, lo], axis=1), jnp.concatenate([tri, tri], axis=0))


def _sb_weights(ls, in_tile, right, strict):
    a = jnp.exp(ls + in_tile + right)
    if strict is not None:
        a = jnp.where(strict, a, 0.0)
    return a


def _sb_masks(q_rows):
    row = lax.broadcasted_iota(jnp.int32, (SB_TILE, SB_TILE), 0)
    col = lax.broadcasted_iota(jnp.int32, (SB_TILE, SB_TILE), 1)
    lane = lax.broadcasted_iota(jnp.int32, (q_rows, LANES), 1)
    return row, col, lane < SB_HEAD_DIM


def _stack_heads(x, first):
    zero = jnp.zeros_like(x)
    return jnp.concatenate([jnp.where(first, x, zero), jnp.where(first, zero, x)], axis=0)


def _stack_heads_t(x_t):
    first_t = lax.broadcasted_iota(jnp.int32, x_t.shape, 0) < SB_HEAD_DIM
    zero = jnp.zeros_like(x_t)
    return jnp.concatenate([jnp.where(first_t, x_t, zero), jnp.where(first_t, zero, x_t)], axis=1).astype(BF16)


def _unstack_heads(x2, first):
    half = x2.shape[0] // 2
    return jnp.where(first, x2[:half], x2[half:])


def _stacked_col_minus_row(q_rows):
    row = lax.broadcasted_iota(jnp.int32, (2 * q_rows, SB_TILE), 0)
    col = lax.broadcasted_iota(jnp.int32, (2 * q_rows, SB_TILE), 1)
    return col - (row & (q_rows - 1))


def _head_mean(x, first):
    s0 = jnp.sum(jnp.where(first, x, 0.0), axis=-1, keepdims=True)
    s1 = jnp.sum(jnp.where(first, 0.0, x), axis=-1, keepdims=True)
    return jnp.where(first, s0, s1) * (1.0 / SB_HEAD_DIM)


def _riding(exchange, refs, n_in, n_out, n_scratch):
    n_x = exchange.n if exchange is not None else 0
    ins, rest = refs[:n_in], refs[n_in:]
    x_src, rest = rest[:n_x], rest[n_x:]
    outs, rest = rest[:n_out], rest[n_out:]
    x_dst, rest = rest[:n_x], rest[n_x:]
    return ins, outs, rest[:n_scratch], (x_src, x_dst, rest[n_scratch:])


def _riding_specs(exchange):
    if exchange is None:
        return [], [], [], [], []
    return exchange.in_specs, exchange.out_specs, exchange.out_shape, exchange.scratch, exchange.arrays


def _ride(exchange, x_refs, first_step, last_step):
    if exchange is None:
        return lambda: None

    @pl.when(first_step)
    def _():
        exchange.start(*x_refs)

    def finish():
        @pl.when(last_step)
        def _():
            exchange.wait(*x_refs)

    return finish


def _sb_fwd(proj, merged_a, hg, *, batch, seq, exchange=None):
    q0, k0, v0 = 2 * GM_WIDTH // LANES, 2 * GM_WIDTH // LANES + SB_PAIRS, 2 * GM_WIDTH // LANES + 2 * SB_PAIRS
    block_keys, q_rows = min(SB_BLOCK, seq), min(SB_QUERIES_FWD, seq)
    assert block_keys % q_rows == 0 and seq % block_keys == 0
    n_q, n_sub = seq // q_rows, block_keys // SB_TILE

    def body(*refs):
        (q_ref, k_ref, v_ref, hg_ref, _), (m_ref, raw_ref, tot_ref), _, x_refs = _riding(exchange, refs, 5, 3, 0)
        b, p, i = pl.program_id(0), pl.program_id(1), pl.program_id(2)
        finish = _ride(exchange, x_refs, (b == 0) & (p == 0) & (i == 0), (b == batch - 1) & (p == SB_PAIRS - 1) & (i == n_q - 1))
        row, col, first = _sb_masks(q_rows)
        upper = (row > col).astype(BF16)
        q2 = _stack_heads((q_ref[...].astype(F32) * SB_SCALE).astype(BF16), first)
        diff = _stacked_col_minus_row(q_rows)
        last = (i * q_rows) // block_keys
        offset = i * q_rows - last * block_keys

        def scores(jb):
            return tuple(_nt(q2, k_ref[pl.ds(pl.multiple_of((jb * n_sub + s) * SB_TILE, SB_TILE), SB_TILE), :]) for s in range(n_sub))

        def weights_of(z, right, masked):
            keeps = [diff < offset - s * SB_TILE if masked else None for s in range(n_sub)]
            logits = [_sb_logits(z[s], keeps[s]) for s in range(n_sub)]
            totals = [jnp.sum(l1m, axis=-1, keepdims=True) for _, l1m in logits]
            sums = [_tri_sums(l1m, upper) for _, l1m in logits]
            weights = [None] * n_sub
            for s in reversed(range(n_sub)):
                weights[s] = _sb_weights(logits[s][0], sums[s], right, keeps[s]).astype(BF16)
                right = right + totals[s]
            return right, jnp.concatenate(weights, axis=1)

        def values(jb):
            return v_ref[pl.ds(pl.multiple_of(jb * block_keys, block_keys), block_keys), :]

        right, a_prev = weights_of(scores(last), jnp.zeros((2 * q_rows, 1), F32), True)
        z_next = scores(jnp.maximum(last - 1, 0))

        def step(k, carry):
            right, acc, z, a_prev = carry
            jb = last - k
            acc = acc + _nn(a_prev, values(jb + 1))
            z_next = scores(jnp.maximum(jb - 1, 0))
            right, a = weights_of(z, right, False)
            return right, acc, z_next, a

        right, acc2, _, a_prev = lax.fori_loop(1, last + 1, step, (right, jnp.zeros((2 * q_rows, LANES), F32), z_next, a_prev))
        acc = _unstack_heads(acc2 + _nn(a_prev, values(0)), first)
        raw_ref[...] = acc
        tot_ref[0] = _unstack_heads(jnp.broadcast_to(right, (2 * q_rows, LANES)), first)
        m_ref[...] = (acc * lax.rsqrt(_head_mean(acc * acc, first) + EPS) * hg_ref[...]).astype(BF16)
        finish()

    t = batch * seq
    blk = lambda c0: pl.BlockSpec((q_rows, LANES), lambda b, p, i: (b * n_q + i, c0 + p))
    kv = lambda c0: pl.BlockSpec((seq, LANES), lambda b, p, i: (b, c0 + p))
    x_in, x_out, x_shape, x_scratch, x_arrays = _riding_specs(exchange)
    res = pl.pallas_call(
        body, name="sb_fwd", grid=(batch, SB_PAIRS, n_q),
        in_specs=[blk(q0), kv(k0), kv(v0), pl.BlockSpec((1, LANES), lambda b, p, i: (0, SB_PAIRS + p)),
                  pl.BlockSpec(memory_space=pl.ANY)] + x_in,
        out_specs=[blk(SB_PAIRS), blk(0), pl.BlockSpec((1, q_rows, LANES), lambda b, p, i: (p, b * n_q + i, 0))] + x_out,
        out_shape=[jax.ShapeDtypeStruct((t, 2 * GM_WIDTH), BF16), jax.ShapeDtypeStruct((t, GM_WIDTH), F32),
                   jax.ShapeDtypeStruct((SB_PAIRS, t, LANES), F32)] + x_shape,
        scratch_shapes=x_scratch,
        input_output_aliases={4: 0},
        compiler_params=_params(3),
    )(proj, proj, proj, hg, merged_a, *x_arrays)
    return res[0], res[1], res[2], res[3:]


def _sb_bwd(proj, raw, tot, dmerged, hg, *, batch, seq, exchange=None):
    q0, k0, v0 = 2 * GM_WIDTH // LANES, 2 * GM_WIDTH // LANES + SB_PAIRS, 2 * GM_WIDTH // LANES + 2 * SB_PAIRS
    block_keys, q_rows = min(SB_BLOCK, seq), min(SB_QUERIES_BWD, seq)
    assert block_keys % q_rows == 0 and seq % block_keys == 0
    n_q, n_sub, n_blocks = seq // q_rows, block_keys // SB_TILE, seq // block_keys

    def body(*refs):
        ins, outs, (dk_acc, dv_acc), x_refs = _riding(exchange, refs, 7, 4, 2)
        q_ref, k_ref, v_ref, raw_ref, tot_ref, dm_ref, hg_ref = ins
        dq_ref, dk_ref, dv_ref, dhg_ref = outs
        p, b, i = pl.program_id(0), pl.program_id(1), pl.program_id(2)
        finish = _ride(exchange, x_refs, (b == 0) & (p == 0) & (i == 0), (b == batch - 1) & (p == SB_PAIRS - 1) & (i == n_q - 1))
        row, col, first = _sb_masks(q_rows)
        upper = (row > col).astype(BF16)
        lower = (row < col).astype(BF16)

        @pl.when(jnp.logical_and(b == 0, i == 0))
        def _():
            dhg_ref[...] = jnp.zeros_like(dhg_ref)

        @pl.when(i == 0)
        def _():
            dk_acc[...] = jnp.zeros_like(dk_acc)
            dv_acc[...] = jnp.zeros_like(dv_acc)

        raw_v = raw_ref[...]
        dm = dm_ref[...].astype(F32)
        r = lax.rsqrt(_head_mean(raw_v * raw_v, first) + EPS)
        nrm = raw_v * r
        dhg_ref[...] += jnp.sum(dm * nrm, axis=0, keepdims=True)
        dn = dm * hg_ref[...]
        dout = r * (dn - nrm * _head_mean(dn * nrm, first))
        qv = q_ref[...]
        dout2 = _stack_heads(dout.astype(BF16), first)
        qs2 = _stack_heads((qv.astype(F32) * SB_SCALE).astype(BF16), first)
        q2_t = _stack_heads_t(qv.astype(F32).T)
        dout2_t = _stack_heads_t(dout.T)
        tot_v = tot_ref[0]
        tot2 = jnp.concatenate([tot_v[:, 0:1], tot_v[:, SB_HEAD_DIM:SB_HEAD_DIM + 1]], axis=0)
        diff = _stacked_col_minus_row(q_rows)
        last = (i * q_rows) // block_keys
        offset = i * q_rows - last * block_keys

        def products(jb):
            tiles = [pl.ds(pl.multiple_of((jb * n_sub + s) * SB_TILE, SB_TILE), SB_TILE) for s in range(n_sub)]
            return tuple(_nt(qs2, k_ref[at, :]) for at in tiles), tuple(_nt(dout2, v_ref[at, :]) for at in tiles)

        def gradients(z, d_weights, left, gleft, masked):
            keeps = [diff < offset - s * SB_TILE if masked else None for s in range(n_sub)]
            logits = [_sb_logits(z[s], keeps[s]) for s in range(n_sub)]
            totals = [jnp.sum(l1m, axis=-1, keepdims=True) for _, l1m in logits]
            sums = [_tri_sums(l1m, upper) for _, l1m in logits]
            weights, gmats = [], []
            for s in range(n_sub):
                left = left + totals[s]
                weights.append(_sb_weights(logits[s][0], sums[s], tot2 - left, keeps[s]))
                gmats.append(d_weights[s] * weights[s])
            prefixes = [_tri_sums(g, lower) for g in gmats]
            dzs = []
            for s in range(n_sub):
                beta = jnp.exp(logits[s][0])
                dz = (gmats[s] * (1.0 - beta) - (gleft + prefixes[s]) * beta) * SB_SCALE
                gleft = gleft + jnp.sum(gmats[s], axis=-1, keepdims=True)
                if masked:
                    dz = jnp.where(keeps[s], dz, 0.0)
                dzs.append(dz.astype(BF16))
            return left, gleft, jnp.concatenate(dzs, axis=1), jnp.concatenate([w.astype(BF16) for w in weights], axis=1)

        def apply(jb, dz_all, a_all, dq):
            dk_acc[jb] += _nn(q2_t, dz_all)
            dv_acc[jb] += _nn(dout2_t, a_all)
            return dq + _nn(dz_all, k_ref[pl.ds(pl.multiple_of(jb * block_keys, block_keys), block_keys), :])

        def step(jb, carry):
            left, gleft, dq, (z, d_weights), (dz_prev, a_prev) = carry
            dq = apply(jnp.maximum(jb - 1, 0), dz_prev, a_prev, dq)
            ahead = products(jb + 1)
            left, gleft, dz_all, a_all = gradients(z, d_weights, left, gleft, False)
            return left, gleft, dq, ahead, (dz_all, a_all)

        zero = jnp.zeros((2 * q_rows, 1), F32)
        nothing = jnp.zeros((2 * q_rows, block_keys), BF16)
        carry = (zero, zero, jnp.zeros((2 * q_rows, LANES), F32), products(0), (nothing, nothing))
        left, gleft, dq2, (z, d_weights), (dz_prev, a_prev) = lax.fori_loop(0, last, step, carry)
        dq2 = apply(jnp.maximum(last - 1, 0), dz_prev, a_prev, dq2)
        _, _, dz_all, a_all = gradients(z, d_weights, left, gleft, True)
        dq_ref[...] = _unstack_heads(apply(last, dz_all, a_all, dq2), first).astype(BF16)

        @pl.when(i == n_q - 1)
        def _():
            for jb in range(n_blocks):
                for s in range(n_sub):
                    rows = slice((jb * n_sub + s) * SB_TILE, (jb * n_sub + s + 1) * SB_TILE)
                    cols = slice(s * SB_TILE, (s + 1) * SB_TILE)
                    dk_ref[rows, :] = dk_acc[jb, :, cols].T.astype(BF16)
                    dv_ref[rows, :] = dv_acc[jb, :, cols].T.astype(BF16)

        finish()

    t = batch * seq
    blk = lambda c0: pl.BlockSpec((q_rows, LANES), lambda p, b, i: (b * n_q + i, c0 + p))
    kv = lambda c0: pl.BlockSpec((seq, LANES), lambda p, b, i: (b, c0 + p))
    row_spec = pl.BlockSpec((1, LANES), lambda p, b, i: (0, SB_PAIRS + p))
    tot_spec = pl.BlockSpec((1, q_rows, LANES), lambda p, b, i: (p, b * n_q + i, 0))
    x_in, x_out, x_shape, x_scratch, x_arrays = _riding_specs(exchange)
    res = pl.pallas_call(
        body, name="sb_bwd", grid=(SB_PAIRS, batch, n_q),
        in_specs=[blk(q0), kv(k0), kv(v0), blk(0), tot_spec, blk(SB_PAIRS), row_spec] + x_in,
        out_specs=[blk(0), kv(0), kv(0), pl.BlockSpec((1, LANES), lambda p, b, i: (0, p))] + x_out,
        out_shape=[jax.ShapeDtypeStruct((t, GM_WIDTH), BF16)] * 3 + [jax.ShapeDtypeStruct((1, GM_WIDTH), F32)] + x_shape,
        scratch_shapes=[pltpu.VMEM((n_blocks, LANES, block_keys), F32), pltpu.VMEM((n_blocks, LANES, block_keys), F32)] + x_scratch,
        compiler_params=_params(3),
    )(proj, proj, proj, raw, tot, dmerged, hg, *x_arrays)
    return res[0], res[1], res[2], res[3], res[4:]


def _x_softmax(qh, kh):
    s = _nt(qh, kh) * X_SCALE
    p = jnp.exp(s - jnp.max(s, axis=-1, keepdims=True))
    return p / jnp.sum(p, axis=-1, keepdims=True)


def _xattn_fwd(q, kv, *, batch, seq, n_mem, tq=512):
    tq = min(tq, seq)
    n_q = seq // tq
    d = X_HEADS * X_HEAD_DIM

    def body(q_ref, kv_ref, o_ref):
        for h in range(X_HEADS):
            cols = slice(h * X_HEAD_DIM, (h + 1) * X_HEAD_DIM)
            p = _x_softmax(q_ref[:, cols], kv_ref[:, cols])
            o_ref[:, cols] = _nn(p.astype(BF16), kv_ref[:, d + h * X_HEAD_DIM:d + (h + 1) * X_HEAD_DIM]).astype(BF16)

    return pl.pallas_call(
        body, name="xattn_fwd", grid=(batch, n_q),
        in_specs=[pl.BlockSpec((tq, d), lambda b, i: (b * n_q + i, 0)), pl.BlockSpec((n_mem, 2 * d), lambda b, i: (b, 0))],
        out_specs=pl.BlockSpec((tq, d), lambda b, i: (b * n_q + i, 0)),
        out_shape=jax.ShapeDtypeStruct((batch * seq, d), BF16),
        compiler_params=_params(2),
    )(q, kv)


def _xattn_bwd(q, kv, do, *, batch, seq, n_mem, tq=512):
    tq = min(tq, seq)
    n_q = seq // tq
    d = X_HEADS * X_HEAD_DIM

    def body(q_ref, kv_ref, do_ref, dq_ref, dkv_ref, acc):
        i = pl.program_id(1)

        @pl.when(i == 0)
        def _():
            acc[...] = jnp.zeros_like(acc)

        for h in range(X_HEADS):
            cols = slice(h * X_HEAD_DIM, (h + 1) * X_HEAD_DIM)
            vcols = slice(d + h * X_HEAD_DIM, d + (h + 1) * X_HEAD_DIM)
            qh, kh, vh, doh = q_ref[:, cols], kv_ref[:, cols], kv_ref[:, vcols], do_ref[:, cols]
            p = _x_softmax(qh, kh)
            dp = _nt(doh, vh)
            acc[:, vcols] += _tn(p.astype(BF16), doh)
            ds = (p * (dp - jnp.sum(dp * p, axis=-1, keepdims=True)) * X_SCALE).astype(BF16)
            dq_ref[:, cols] = _nn(ds, kh).astype(BF16)
            acc[:, cols] += _tn(ds, qh)

        @pl.when(i == n_q - 1)
        def _():
            dkv_ref[...] = acc[...].astype(BF16)

    return pl.pallas_call(
        body, name="xattn_bwd", grid=(batch, n_q),
        in_specs=[pl.BlockSpec((tq, d), lambda b, i: (b * n_q + i, 0)), pl.BlockSpec((n_mem, 2 * d), lambda b, i: (b, 0)),
                  pl.BlockSpec((tq, d), lambda b, i: (b * n_q + i, 0))],
        out_specs=[pl.BlockSpec((tq, d), lambda b, i: (b * n_q + i, 0)), pl.BlockSpec((n_mem, 2 * d), lambda b, i: (b, 0))],
        out_shape=[jax.ShapeDtypeStruct((batch * seq, d), BF16), jax.ShapeDtypeStruct((batch * n_mem, 2 * d), BF16)],
        scratch_shapes=[pltpu.VMEM((n_mem, 2 * d), F32)],
        compiler_params=_params(2),
    )(q, kv, do)


def _loss_head(h, target, g, *, tm=512):
    t, d = h.shape
    tm = min(tm, t)
    n_steps = t // tm

    def body(h_ref, t_ref, g_ref, dh_ref, dg_ref, loss_ref, sq_acc):
        step = pl.program_id(0)

        @pl.when(step == 0)
        def _():
            dg_ref[...] = jnp.zeros_like(dg_ref)
            sq_acc[...] = jnp.zeros_like(sq_acc)

        hv = h_ref[...]
        gv = g_ref[...]
        r = _rstd(hv)
        xh = hv * r
        err = xh * gv - t_ref[...]
        sq_acc[...] += jnp.sum(err * err, axis=0, keepdims=True)
        dy = err * (1.0 / d)
        dg_ref[...] += jnp.sum(dy * xh, axis=0, keepdims=True)
        dxh = dy * gv
        dh_ref[...] = r * (dxh - xh * jnp.mean(dxh * xh, axis=-1, keepdims=True))

        @pl.when(step == n_steps - 1)
        def _():
            total = jnp.sum(sq_acc[...], axis=-1, keepdims=True) * (0.5 / d)
            loss_ref[...] = jnp.broadcast_to(total, loss_ref.shape)

    return pl.pallas_call(
        body, name="loss_head", grid=(n_steps,),
        in_specs=[pl.BlockSpec((tm, d), lambda i: (i, 0)), pl.BlockSpec((tm, d), lambda i: (i, 0)), pl.BlockSpec((1, d), lambda i: (0, 0))],
        out_specs=[pl.BlockSpec((tm, d), lambda i: (i, 0)), pl.BlockSpec((1, d), lambda i: (0, 0)), pl.BlockSpec((1, LANES), lambda i: (0, 0))],
        out_shape=[jax.ShapeDtypeStruct((t, d), F32), jax.ShapeDtypeStruct((1, d), F32), jax.ShapeDtypeStruct((1, LANES), F32)],
        scratch_shapes=[pltpu.VMEM((1, d), F32)],
        compiler_params=_params(1),
    )(h, target, g)


def _my_index():
    return 4 * lax.axis_index("x") + 2 * lax.axis_index("y") + lax.axis_index("c")


def _peers():
    x, y, c = lax.axis_index("x"), lax.axis_index("y"), lax.axis_index("c")
    out = []
    for rel in range(1, N_DEV):
        dx, dy, dc = (rel >> 2) & 1, (rel >> 1) & 1, rel & 1
        px, py, pc = x ^ dx, y ^ dy, c ^ dc
        out.append(((px, py, pc), 4 * px + 2 * py + pc))
    return out


class _Exchange:
    def __init__(self, arrays, scatter):
        self.arrays, self.scatter, self.n = list(arrays), scatter, len(arrays)
        any_spec = pl.BlockSpec(memory_space=pl.ANY)
        self.in_specs = [any_spec] * self.n
        self.out_specs = [any_spec] * self.n
        self.out_shape = [jax.ShapeDtypeStruct((N_DEV,) + tuple(a.shape[-2:]), a.dtype) for a in self.arrays]
        n_peer = N_DEV - 1
        self.scratch = [pltpu.SemaphoreType.DMA((self.n, n_peer)), pltpu.SemaphoreType.DMA((self.n, n_peer)),
                        pltpu.SemaphoreType.DMA((self.n,))]

    def _copies(self, srcs, dsts, sems, arriving):
        send_sems, recv_sems, local_sems = sems
        me = _my_index()
        local, remote = [], []
        for w in range(self.n):
            if not arriving:
                local.append(pltpu.make_async_copy(srcs[w].at[me] if self.scatter else srcs[w], dsts[w].at[me], local_sems.at[w]))
            for rel, (pos, idx) in enumerate(_peers()):
                remote.append(pltpu.make_async_remote_copy(
                    src_ref=srcs[w].at[idx] if self.scatter else srcs[w], dst_ref=dsts[w].at[idx if arriving else me],
                    send_sem=send_sems.at[w, rel], recv_sem=recv_sems.at[w, rel], device_id=pos, device_id_type=MESH))
        return local, remote

    def start(self, srcs, dsts, sems):
        local, sends = self._copies(srcs, dsts, sems, arriving=False)
        for cp in local + sends:
            cp.start()

    def wait(self, srcs, dsts, sems):
        for cp in self._copies(srcs, dsts, sems, arriving=True)[1]:
            cp.wait_recv()
        local, sends = self._copies(srcs, dsts, sems, arriving=False)
        for cp in sends:
            cp.wait_send()
        for cp in local:
            cp.wait()

    def run(self, name):
        n = self.n

        def body(*refs):
            srcs, dsts, sems = refs[:n], refs[n:2 * n], refs[2 * n:]
            self.start(srcs, dsts, sems)
            self.wait(srcs, dsts, sems)

        return pl.pallas_call(body, name=name, in_specs=self.in_specs, out_specs=self.out_specs, out_shape=self.out_shape,
                              scratch_shapes=self.scratch)(*self.arrays)


def _all_reduce_small(part):
    rows = part.shape[0]
    n_peer = N_DEV - 1

    def body(p_ref, o_ref, buf, send_sems, recv_sems):
        me = _my_index()
        peers = _peers()
        buf[me] = p_ref[...]
        sends = [pltpu.make_async_remote_copy(src_ref=p_ref, dst_ref=buf.at[me], send_sem=send_sems.at[rel], recv_sem=recv_sems.at[rel],
                                              device_id=peers[rel][0], device_id_type=MESH) for rel in range(n_peer)]
        for cp in sends:
            cp.start()
        for rel in range(n_peer):
            pltpu.make_async_remote_copy(src_ref=p_ref, dst_ref=buf.at[peers[rel][1]], send_sem=send_sems.at[rel], recv_sem=recv_sems.at[rel],
                                         device_id=peers[rel][0], device_id_type=MESH).wait_recv()
        for cp in sends:
            cp.wait_send()
        total = buf[0]
        for dev in range(1, N_DEV):
            total = total + buf[dev]
        o_ref[...] = total

    vmem = pl.BlockSpec(memory_space=pltpu.VMEM)
    return pl.pallas_call(
        body, name="all_reduce_small", in_specs=[vmem], out_specs=vmem, out_shape=jax.ShapeDtypeStruct(part.shape, F32),
        scratch_shapes=[pltpu.VMEM((N_DEV, rows, LANES), F32), pltpu.SemaphoreType.DMA((n_peer,)), pltpu.SemaphoreType.DMA((n_peer,))],
        compiler_params=pltpu.CompilerParams(has_side_effects=True, vmem_limit_bytes=VMEM_LIMIT),
    )(part)


def _adamw_math(w, g, m, v):
    m_new = ADAM_B1 * m + (1.0 - ADAM_B1) * g
    v_new = ADAM_B2 * v + (1.0 - ADAM_B2) * (g * g)
    m_hat = m_new / (1.0 - ADAM_B1 ** ADAM_STEP)
    v_hat = v_new / (1.0 - ADAM_B2 ** ADAM_STEP)
    delta = -ADAM_LR * (m_hat / (jnp.sqrt(v_hat) + ADAM_EPS) + ADAM_WD * w)
    return delta, m_new, v_new


def _adamw(parts, w, m, v, *, name, tr=64):
    rows, cols = w.shape
    tr = min(tr, rows)
    assert rows % tr == 0
    stacked = parts.ndim == 3

    def body(p_ref, w_ref, m_ref, v_ref, g_ref, d_ref, mo_ref, vo_ref):
        if stacked:
            g = p_ref[0].astype(F32)
            for dev in range(1, N_DEV):
                g = g + p_ref[dev].astype(F32)
        else:
            g = p_ref[...]
        delta, m_new, v_new = _adamw_math(w_ref[...], g, m_ref[...], v_ref[...])
        g_ref[...] = g
        d_ref[...] = delta
        mo_ref[...] = m_new
        vo_ref[...] = v_new

    tile = pl.BlockSpec((tr, cols), lambda i: (i, 0))
    p_spec = pl.BlockSpec((N_DEV, tr, cols), lambda i: (0, i, 0)) if stacked else tile
    return pl.pallas_call(
        body, name=name, grid=(rows // tr,), in_specs=[p_spec, tile, tile, tile], out_specs=[tile] * 4,
        out_shape=[jax.ShapeDtypeStruct((rows, cols), F32)] * 4, compiler_params=_params(1),
    )(parts, w, m, v)


_LATER = ("w_out", "w_cq", "w_ckv", "w_co", "w_ff1", "w_ff2")


def _as_rows(stacked):
    return stacked.reshape(-1, stacked.shape[-1])


def _local_step(x, mem, target, small, shards):
    batch, seq, d = x.shape
    n_mem = mem.shape[1]
    t = batch * seq
    x2, mem2, tgt2 = x.reshape(t, d), mem.reshape(batch * n_mem, d), target.reshape(t, d)
    g_mix, g_cross, g_mem, g_ffn, g_final = (small[k] for k in ("norm_mix_g", "norm_cross_g", "norm_mem_g", "norm_ffn_g", "norm_final_g"))
    gv, hg, w_sp, b_sp_t = small["gm_v_norm_g"], small["head_norm_g"], small["w_spatial"], small["b_spatial_t"]

    win_t = _as_rows(_Exchange([shards["w_in"]], scatter=False).run("gather_w_in")[0])
    proj, xn = _norm_mm(x2, g_mix, win_t, mode="nt", name="proj_fwd", tn=512)
    merged_a = _gmlp_fwd(proj, w_sp, b_sp_t, gv, hg)
    merged, sb_raw, sb_tot, gathered = _sb_fwd(proj, merged_a, hg, batch=batch, seq=seq,
                                               exchange=_Exchange([shards[n] for n in _LATER], scatter=False))
    wout, wcq, wckv_t, wco, wff1_t, wff2 = (_as_rows(g) for g in gathered)
    h1 = _mm(merged, wout, mode="nn", out_dtype=F32, name="mix_out_fwd", epi=_epi_residual, epi_ins=(x2,))
    qx, hn1 = _norm_mm(h1, g_cross, wcq, mode="nn", name="xq_fwd")
    kvx, memn = _norm_mm(mem2, g_mem, wckv_t, mode="nt", name="xkv_fwd")
    o = _xattn_fwd(qx, kvx, batch=batch, seq=seq, n_mem=n_mem)
    h2 = _mm(o, wco, mode="nn", out_dtype=F32, name="xo_fwd", epi=_epi_residual, epi_ins=(h1,))
    fpre, hn2 = _norm_mm(h2, g_ffn, wff1_t, mode="nt", name="ff1_fwd")
    h3 = _mm(fpre, wff2, mode="nn", out_dtype=F32, name="ff2_fwd", a_fn=_relu2, epi=_epi_residual, epi_ins=(h2,))
    dh3, dg_final, loss_row = _loss_head(h3, tgt2, g_final)

    dpre = _mm(dh3, wff2, mode="nt", out_dtype=BF16, name="ff2_bwd_x", epi=_epi_relu2_grad, epi_ins=(fpre,))
    d_wff2 = _mm(fpre, dh3, mode="tn", out_dtype=BF16, name="ff2_bwd_w", a_fn=_relu2)
    d_wff1_t = _mm(dpre, hn2, mode="tn", out_dtype=BF16, name="ff1_bwd_w")
    dh2, dg_ffn = _mm(dpre, wff1_t, mode="nn", out_dtype=F32, name="ff1_bwd_x", tm=512, tk=wff1_t.shape[0], epi=_epi_rms_bwd,
                      epi_ins=(h2, dh3), vec_ins=(g_ffn,), aux=True)
    do = _mm(dh2, wco, mode="nt", out_dtype=BF16, name="xo_bwd_x")
    d_wco = _mm(o, dh2, mode="tn", out_dtype=BF16, name="xo_bwd_w")
    dqx, dkvx = _xattn_bwd(qx, kvx, do, batch=batch, seq=seq, n_mem=n_mem)
    d_wcq = _mm(hn1, dqx, mode="tn", out_dtype=BF16, name="xq_bwd_w")
    dh1, dg_cross = _mm(dqx, wcq, mode="nt", out_dtype=F32, name="xq_bwd_x", tm=512, epi=_epi_rms_bwd,
                        epi_ins=(h1, dh2), vec_ins=(g_cross,), aux=True)
    d_wckv_t = _mm(dkvx, memn, mode="tn", out_dtype=BF16, name="xkv_bwd_w")
    _, dg_mem = _mm(dkvx, wckv_t, mode="nn", out_dtype=BF16, name="xkv_bwd_x", tm=512, epi=_epi_rms_gain_only,
                    epi_ins=(mem2,), vec_ins=(g_mem,), aux=True)
    dmerged = _mm(dh1, wout, mode="nt", out_dtype=BF16, name="mix_out_bwd_x")
    d_wout = _mm(merged, dh1, mode="tn", out_dtype=BF16, name="mix_out_bwd_w")
    dp_a, d_wsp, d_bsp_t, d_gv, d_hg_a = _gmlp_bwd(proj, dmerged, w_sp, b_sp_t, gv, hg)
    d_later = {"w_out": d_wout, "w_cq": d_wcq, "w_ckv": d_wckv_t, "w_co": d_wco, "w_ff1": d_wff1_t, "w_ff2": d_wff2}
    scatter = _Exchange([d_later[n].reshape(N_DEV, -1, d) for n in _LATER], scatter=True)
    dq, dk, dv, d_hg_b, received = _sb_bwd(proj, sb_raw, sb_tot, dmerged, hg, batch=batch, seq=seq, exchange=scatter)
    dproj = jnp.concatenate([dp_a, dq, dk, dv], axis=1)
    d_win_t = _mm(dproj, xn, mode="tn", out_dtype=BF16, name="proj_bwd_w", tm=512)
    dx, dg_mix, d_win_received = _mm(dproj, win_t, mode="nn", out_dtype=F32, name="proj_bwd_x", tm=512, tk=win_t.shape[0],
                                     epi=_epi_rms_bwd, epi_ins=(x2, dh1), vec_ins=(g_mix,), aux=True,
                                     exchange=_Exchange([d_win_t.reshape(N_DEV, -1, d)], scatter=True))

    d_small = {"norm_mix_g": dg_mix, "gm_v_norm_g": d_gv, "w_spatial": d_wsp, "b_spatial_t": d_bsp_t, "head_norm_g": jnp.concatenate([d_hg_a, d_hg_b], axis=1),
               "norm_cross_g": dg_cross, "norm_mem_g": dg_mem, "norm_ffn_g": dg_ffn, "norm_final_g": dg_final}
    d_big = dict(zip(_LATER, received))
    d_big["w_in"] = d_win_received
    return loss_row, dx.reshape(batch, seq, d), d_small, d_big


_BIG = ("w_in", "w_out", "w_cq", "w_ckv", "w_co", "w_ff1", "w_ff2")
_BIG_TRANSPOSED = ("w_in", "w_ckv", "w_ff1")
_SMALL = ("norm_mix_g", "gm_v_norm_g", "w_spatial", "b_spatial", "head_norm_g", "norm_cross_g", "norm_mem_g", "norm_ffn_g", "norm_final_g")
_NAMES = ("norm_mix_g", "w_in", "gm_v_norm_g", "w_spatial", "b_spatial", "head_norm_g", "w_out", "norm_cross_g", "norm_mem_g",
          "w_cq", "w_ckv", "w_co", "norm_ffn_g", "w_ff1", "w_ff2", "norm_final_g")


def _rows_of(a):
    r = a.reshape(-1, LANES)
    pad = (-r.shape[0]) % 8
    return jnp.pad(r, ((0, pad), (0, 0))) if pad else r


def _shard2d(name, a):
    a = a[0]
    return a.T if name in _BIG_TRANSPOSED else a


def kernel(x, mem, norm_mix_g, w_in, gm_v_norm_g, w_spatial, b_spatial, head_norm_g, w_out, norm_cross_g, norm_mem_g, w_cq, w_ckv, w_co, norm_ffn_g, w_ff1, w_ff2, norm_final_g, loss_target, m_norm_mix_g, m_w_in, m_gm_v_norm_g, m_w_spatial, m_b_spatial, m_head_norm_g, m_w_out, m_norm_cross_g, m_norm_mem_g, m_w_cq, m_w_ckv, m_w_co, m_norm_ffn_g, m_w_ff1, m_w_ff2, m_norm_final_g, v_norm_mix_g, v_w_in, v_gm_v_norm_g, v_w_spatial, v_b_spatial, v_head_norm_g, v_w_out, v_norm_cross_g, v_norm_mem_g, v_w_cq, v_w_ckv, v_w_co, v_norm_ffn_g, v_w_ff1, v_w_ff2, v_norm_final_g):
    weights = dict(norm_mix_g=norm_mix_g, w_in=w_in, gm_v_norm_g=gm_v_norm_g, w_spatial=w_spatial, b_spatial=b_spatial,
                   head_norm_g=head_norm_g, w_out=w_out, norm_cross_g=norm_cross_g, norm_mem_g=norm_mem_g, w_cq=w_cq, w_ckv=w_ckv,
                   w_co=w_co, norm_ffn_g=norm_ffn_g, w_ff1=w_ff1, w_ff2=w_ff2, norm_final_g=norm_final_g)
    mom1 = dict(norm_mix_g=m_norm_mix_g, w_in=m_w_in, gm_v_norm_g=m_gm_v_norm_g, w_spatial=m_w_spatial, b_spatial=m_b_spatial,
                head_norm_g=m_head_norm_g, w_out=m_w_out, norm_cross_g=m_norm_cross_g, norm_mem_g=m_norm_mem_g, w_cq=m_w_cq,
                w_ckv=m_w_ckv, w_co=m_w_co, norm_ffn_g=m_norm_ffn_g, w_ff1=m_w_ff1, w_ff2=m_w_ff2, norm_final_g=m_norm_final_g)
    mom2 = dict(norm_mix_g=v_norm_mix_g, w_in=v_w_in, gm_v_norm_g=v_gm_v_norm_g, w_spatial=v_w_spatial, b_spatial=v_b_spatial,
                head_norm_g=v_head_norm_g, w_out=v_w_out, norm_cross_g=v_norm_cross_g, norm_mem_g=v_norm_mem_g, w_cq=v_w_cq,
                w_ckv=v_w_ckv, w_co=v_w_co, norm_ffn_g=v_norm_ffn_g, w_ff1=v_w_ff1, w_ff2=v_w_ff2, norm_final_g=v_norm_final_g)

    shards = {n: _shard2d(n, weights[n]).astype(BF16) for n in _BIG}
    small = {n: weights[n].reshape(1, -1) for n in _SMALL if n not in ("w_spatial", "b_spatial")}
    small["w_spatial"] = w_spatial[0]
    small["b_spatial_t"] = b_spatial[0].T
    loss_row, grad_x, d_small, d_big = _local_step(x, mem, loss_target, small, shards)

    d_small["b_spatial"] = d_small.pop("b_spatial_t").T
    packed = jnp.concatenate([_rows_of(d_small[n]) for n in _SMALL] + [jnp.pad(loss_row, ((0, 7), (0, 0)))], axis=0)
    summed = _all_reduce_small(packed)

    grads, deltas, new_m, new_v = {}, {}, {}, {}
    for n in _BIG:
        outs = _adamw(d_big[n], _shard2d(n, weights[n]), _shard2d(n, mom1[n]), _shard2d(n, mom2[n]), name="adamw_" + n)
        outs = [o.T if n in _BIG_TRANSPOSED else o for o in outs]
        grads[n], deltas[n], new_m[n], new_v[n] = (o[None] for o in outs)
    pack = lambda src: jnp.concatenate([_rows_of(src[n]) for n in _SMALL], axis=0)
    n_small_rows = sum(_rows_of(weights[n]).shape[0] for n in _SMALL)
    outs = _adamw(summed[:n_small_rows], pack(weights), pack(mom1), pack(mom2), name="adamw_small", tr=n_small_rows)
    at = 0
    for n in _SMALL:
        used = weights[n].size // LANES
        for dst, o in zip((grads, deltas, new_m, new_v), outs):
            dst[n] = o[at:at + used].reshape(weights[n].shape)
        at += _rows_of(weights[n]).shape[0]
    loss = summed[n_small_rows, 0]
    return (loss, grad_x, *[grads[n] for n in _NAMES], *[deltas[n] for n in _NAMES], *[new_m[n] for n in _NAMES],
            *[new_v[n] for n in _NAMES])
```

```python
import functools
import math

import jax
import jax.numpy as jnp
from jax import lax
from jax.experimental import pallas as pl
from jax.experimental.pallas import tpu as pltpu

F32 = jnp.float32
BF16 = jnp.bfloat16
EPS = 1e-6
N_DEV = 8
LANES = 128
CHUNK = 128
GM_GROUPS = 4
GM_WIDTH = 512
SB_PAIRS = 4
SB_HEAD_DIM = 64
SB_SCALE = 0.125
SB_TILE = 128
SB_BLOCK = 512
SB_QUERIES_FWD = 256
SB_QUERIES_BWD = 256
X_HEADS = 4
X_HEAD_DIM = 256
X_SCALE = 1.0 / 16.0
VMEM_LIMIT = 56 * 1024 * 1024
ADAM_LR, ADAM_B1, ADAM_B2, ADAM_EPS, ADAM_WD, ADAM_STEP = 0.001, 0.9, 0.999, 1e-08, 0.01, 10
MESH = pl.DeviceIdType.MESH


def _params(n_axes):
    return pltpu.CompilerParams(dimension_semantics=("arbitrary",) * n_axes, vmem_limit_bytes=VMEM_LIMIT)


def _dot(a, b, dims):
    return lax.dot_general(a, b, (dims, ((), ())), preferred_element_type=F32)


def _nn(a, b):
    return _dot(a, b, ((1,), (0,)))


def _nt(a, b):
    return _dot(a, b, ((1,), (1,)))


def _tn(a, b):
    return _dot(a, b, ((0,), (0,)))


_MODES = {"nn": _nn, "nt": _nt, "tn": _tn}


def _rstd(x):
    return lax.rsqrt(jnp.mean(x * x, axis=-1, keepdims=True) + EPS)


def _gelu(x):
    c = math.sqrt(2.0 / math.pi)
    t = jnp.tanh(c * (x + 0.044715 * x * x * x))
    return 0.5 * x * (1.0 + t)


def _gelu_grad(x):
    c = math.sqrt(2.0 / math.pi)
    t = jnp.tanh(c * (x + 0.044715 * x * x * x))
    return 0.5 * (1.0 + t) + 0.5 * x * (1.0 - t * t) * c * (1.0 + 3 * 0.044715 * x * x)


def _split_bf16(x):
    hi = x.astype(BF16)
    lo = (x - hi.astype(F32)).astype(BF16)
    return hi, lo


def _mm(a, b, *, mode, out_dtype, name, tm=1024, tn=1024, tk=1024, a_fn=None, epi=None, epi_ins=(), vec_ins=(), aux=False,
        exchange=None):
    if mode == "nn":
        (m, k), (k2, n) = a.shape, b.shape
    elif mode == "nt":
        (m, k), (n, k2) = a.shape, b.shape
    else:
        (k, m), (k2, n) = a.shape, b.shape
    assert k == k2, (a.shape, b.shape, mode)
    tm, tn, tk = min(tm, m), min(tn, n), min(tk, k)
    assert m % tm == 0 and n % tn == 0 and k % tk == 0, (m, n, k, tm, tn, tk)
    n_m, n_n, n_k = m // tm, n // tn, k // tk
    assert not aux or n_n == 1
    dot = _MODES[mode]
    n_epi, n_vec = len(epi_ins), len(vec_ins)

    def body(*refs):
        ins, outs, scratch, x_refs = _riding(exchange, refs, 2 + n_epi + n_vec, 2 if aux else 1, 1 if n_k > 1 else 0)
        a_ref, b_ref, epi_refs = ins[0], ins[1], ins[2:]
        o_ref = outs[0]
        aux_ref = outs[1] if aux else None
        acc_ref = scratch[0] if n_k > 1 else None
        i, j, kk = pl.program_id(0), pl.program_id(1), pl.program_id(2)
        ride_done = _ride(exchange, x_refs, (i == 0) & (j == 0) & (kk == 0), (i == n_m - 1) & (j == n_n - 1) & (kk == n_k - 1))
        av = a_ref[...]
        if a_fn is not None:
            av = a_fn(av)
        part = dot(av.astype(BF16), b_ref[...].astype(BF16))

        def finish(acc):
            if epi is None:
                o_ref[...] = acc.astype(out_dtype)
                return
            res = epi(acc, *[r[...] for r in epi_refs])
            if aux:
                res, row = res

                @pl.when(i == 0)
                def _():
                    aux_ref[...] = row

                @pl.when(i != 0)
                def _():
                    aux_ref[...] += row
            o_ref[...] = res.astype(out_dtype)

        if n_k == 1:
            finish(part)
        else:
            @pl.when(kk == 0)
            def _():
                acc_ref[...] = part

            @pl.when(kk != 0)
            def _():
                acc_ref[...] += part

            @pl.when(kk == n_k - 1)
            def _():
                finish(acc_ref[...])

        ride_done()

    if mode == "tn":
        a_spec = pl.BlockSpec((tk, tm), lambda i, j, kk: (kk, i))
    else:
        a_spec = pl.BlockSpec((tm, tk), lambda i, j, kk: (i, kk))
    if mode == "nt":
        b_spec = pl.BlockSpec((tn, tk), lambda i, j, kk: (j, kk))
    else:
        b_spec = pl.BlockSpec((tk, tn), lambda i, j, kk: (kk, j))
    tile_spec = pl.BlockSpec((tm, tn), lambda i, j, kk: (i, j))
    row_spec = pl.BlockSpec((1, tn), lambda i, j, kk: (0, j))
    out_shape = [jax.ShapeDtypeStruct((m, n), out_dtype)]
    out_specs = [tile_spec]
    if aux:
        out_shape.append(jax.ShapeDtypeStruct((1, n), F32))
        out_specs.append(row_spec)
    x_in, x_out, x_shape, x_scratch, x_arrays = _riding_specs(exchange)
    res = pl.pallas_call(
        body, name=name, grid=(n_m, n_n, n_k),
        in_specs=[a_spec, b_spec] + [tile_spec] * n_epi + [row_spec] * n_vec + x_in,
        out_specs=out_specs + x_out, out_shape=out_shape + x_shape,
        scratch_shapes=([pltpu.VMEM((tm, tn), F32)] if n_k > 1 else []) + x_scratch,
        compiler_params=_params(3),
    )(a, b, *epi_ins, *vec_ins, *x_arrays)
    if exchange is not None:
        return tuple(res)
    return res if aux else res[0]


def _norm_mm(x, g, w, *, mode, name, tm=1024, tn=1024):
    m, d = x.shape
    n = w.shape[0] if mode == "nt" else w.shape[1]
    tm, tn = min(tm, m), min(tn, n)
    assert m % tm == 0 and n % tn == 0
    dot = _MODES[mode]

    def body(x_ref, g_ref, w_ref, o_ref, xn_ref, xn_s):
        @pl.when(pl.program_id(1) == 0)
        def _():
            xv = x_ref[...]
            xn = (xv * _rstd(xv) * g_ref[...]).astype(BF16)
            xn_s[...] = xn
            xn_ref[...] = xn

        o_ref[...] = dot(xn_s[...], w_ref[...]).astype(BF16)

    w_spec = pl.BlockSpec((tn, d), lambda i, j: (j, 0)) if mode == "nt" else pl.BlockSpec((d, tn), lambda i, j: (0, j))
    return pl.pallas_call(
        body, name=name, grid=(m // tm, n // tn),
        in_specs=[pl.BlockSpec((tm, d), lambda i, j: (i, 0)), pl.BlockSpec((1, d), lambda i, j: (0, 0)), w_spec],
        out_specs=[pl.BlockSpec((tm, tn), lambda i, j: (i, j)), pl.BlockSpec((tm, d), lambda i, j: (i, 0))],
        out_shape=[jax.ShapeDtypeStruct((m, n), BF16), jax.ShapeDtypeStruct((m, d), BF16)],
        scratch_shapes=[pltpu.VMEM((tm, d), BF16)],
        compiler_params=_params(2),
    )(x, g, w)


def _epi_residual(acc, res):
    return res + acc


def _epi_relu2_grad(acc, pre):
    return acc * (2.0 * jnp.maximum(pre.astype(F32), 0.0))


def _relu2(pre):
    r = jnp.maximum(pre.astype(F32), 0.0)
    return r * r


def _epi_rms_bwd(acc, h, dres, g):
    r = _rstd(h)
    xh = h * r
    dxh = acc * g
    dh = dres + r * (dxh - xh * jnp.mean(dxh * xh, axis=-1, keepdims=True))
    return dh, jnp.sum(acc * xh, axis=0, keepdims=True)


def _epi_rms_gain_only(acc, h, g):
    return acc, jnp.sum(acc * (h * _rstd(h)), axis=0, keepdims=True)


def _tril(n):
    row = lax.broadcasted_iota(jnp.int32, (n, n), 0)
    col = lax.broadcasted_iota(jnp.int32, (n, n), 1)
    return col <= row


def _gmlp_fwd(proj, w_sp, b_sp_t, gv, hg, *, rows=512):
    t = proj.shape[0]
    rows = min(rows, t)
    n_c = rows // CHUNK

    def body(u_ref, v_ref, w_ref, bt_ref, gv_ref, hg_ref, m_ref):
        keep = _tril(CHUNK)
        for g in range(GM_GROUPS):
            cols = slice(g * LANES, (g + 1) * LANES)
            wg = jnp.where(keep, w_ref[g], 0.0).astype(BF16)
            u = _gelu(u_ref[:, cols].astype(F32))
            v = _gelu(v_ref[:, cols].astype(F32))
            vn = (v * _rstd(v) * gv_ref[:, cols]).astype(BF16)
            bias = bt_ref[:, g:g + 1]
            for c in range(n_c):
                rs = slice(c * CHUNK, (c + 1) * CHUNK)
                mixed = _nn(wg, vn[rs]) + bias
                a = u[rs] * mixed
                m_ref[rs, cols] = (a * _rstd(a) * hg_ref[:, cols]).astype(BF16)

    full = lambda shape: pl.BlockSpec(shape, lambda i: (0,) * len(shape))
    return pl.pallas_call(
        body, name="gmlp_fwd", grid=(t // rows,),
        in_specs=[pl.BlockSpec((rows, GM_WIDTH), lambda i: (i, 0)), pl.BlockSpec((rows, GM_WIDTH), lambda i: (i, 1)),
                  full((GM_GROUPS, CHUNK, CHUNK)), full((CHUNK, GM_GROUPS)), full((1, GM_WIDTH)), full((1, GM_WIDTH))],
        out_specs=pl.BlockSpec((rows, GM_WIDTH), lambda i: (i, 0)),
        out_shape=jax.ShapeDtypeStruct((t, 2 * GM_WIDTH), BF16),
        compiler_params=_params(1),
    )(proj, proj, w_sp, b_sp_t, gv, hg)


def _gmlp_bwd(proj, dmerged, w_sp, b_sp_t, gv, hg, *, rows=512):
    t = proj.shape[0]
    rows = min(rows, t)
    n_c = rows // CHUNK
    n_steps = t // rows

    def body(u_ref, v_ref, dm_ref, w_ref, bt_ref, gv_ref, hg_ref, dp_ref, dw_ref, dbt_ref, dgv_ref, dhg_ref, db_acc):
        step = pl.program_id(0)
        keep = _tril(CHUNK)

        @pl.when(step == 0)
        def _():
            dw_ref[...] = jnp.zeros_like(dw_ref)
            db_acc[...] = jnp.zeros_like(db_acc)
            dgv_ref[...] = jnp.zeros_like(dgv_ref)
            dhg_ref[...] = jnp.zeros_like(dhg_ref)

        for g in range(GM_GROUPS):
            cols = slice(g * LANES, (g + 1) * LANES)
            wg = jnp.where(keep, w_ref[g], 0.0).astype(BF16)
            u_pre = u_ref[:, cols].astype(F32)
            v_pre = v_ref[:, cols].astype(F32)
            u = _gelu(u_pre)
            v = _gelu(v_pre)
            r = _rstd(v)
            xh = v * r
            gvg = gv_ref[:, cols]
            hgg = hg_ref[:, cols]
            vn = (xh * gvg).astype(BF16)
            bias = bt_ref[:, g:g + 1]
            dm = dm_ref[:, cols].astype(F32)
            du_parts, dvn_parts = [], []
            dw = jnp.zeros((CHUNK, CHUNK), F32)
            db = jnp.zeros((CHUNK, LANES), F32)
            dhg = jnp.zeros((1, LANES), F32)
            for c in range(n_c):
                rs = slice(c * CHUNK, (c + 1) * CHUNK)
                mixed = _nn(wg, vn[rs]) + bias
                a = u[rs] * mixed
                ra = _rstd(a)
                an = a * ra
                dhg = dhg + jnp.sum(dm[rs] * an, axis=0, keepdims=True)
                dan = dm[rs] * hgg
                da = ra * (dan - an * jnp.mean(dan * an, axis=-1, keepdims=True))
                du_parts.append(da * mixed)
                dmixed = da * u[rs]
                db = db + dmixed
                dmb = dmixed.astype(BF16)
                dw = dw + _nt(dmb, vn[rs])
                dvn_parts.append(_tn(wg, dmb))
            du = jnp.concatenate(du_parts, axis=0)
            dvn = jnp.concatenate(dvn_parts, axis=0)
            dw_ref[g] += dw
            db_acc[g] += db
            dhg_ref[:, cols] += dhg
            dgv_ref[:, cols] += jnp.sum(dvn * xh, axis=0, keepdims=True)
            dxh = dvn * gvg
            dv = r * (dxh - xh * jnp.mean(dxh * xh, axis=-1, keepdims=True))
            dp_ref[:, cols] = (du * _gelu_grad(u_pre)).astype(BF16)
            dp_ref[:, GM_WIDTH + g * LANES:GM_WIDTH + (g + 1) * LANES] = (dv * _gelu_grad(v_pre)).astype(BF16)

        @pl.when(step == n_steps - 1)
        def _():
            for g in range(GM_GROUPS):
                dw_ref[g] = jnp.where(keep, dw_ref[g], 0.0)
                dbt_ref[:, g:g + 1] = jnp.sum(db_acc[g], axis=-1, keepdims=True)

    full = lambda shape: pl.BlockSpec(shape, lambda i: (0,) * len(shape))
    return pl.pallas_call(
        body, name="gmlp_bwd", grid=(n_steps,),
        in_specs=[pl.BlockSpec((rows, GM_WIDTH), lambda i: (i, 0)), pl.BlockSpec((rows, GM_WIDTH), lambda i: (i, 1)),
                  pl.BlockSpec((rows, GM_WIDTH), lambda i: (i, 0)),
                  full((GM_GROUPS, CHUNK, CHUNK)), full((CHUNK, GM_GROUPS)), full((1, GM_WIDTH)), full((1, GM_WIDTH))],
        out_specs=[pl.BlockSpec((rows, 2 * GM_WIDTH), lambda i: (i, 0)), full((GM_GROUPS, CHUNK, CHUNK)),
                   full((CHUNK, GM_GROUPS)), full((1, GM_WIDTH)), full((1, GM_WIDTH))],
        out_shape=[jax.ShapeDtypeStruct((t, 2 * GM_WIDTH), BF16), jax.ShapeDtypeStruct((GM_GROUPS, CHUNK, CHUNK), F32),
                   jax.ShapeDtypeStruct((CHUNK, GM_GROUPS), F32), jax.ShapeDtypeStruct((1, GM_WIDTH), F32),
                   jax.ShapeDtypeStruct((1, GM_WIDTH), F32)],
        scratch_shapes=[pltpu.VMEM((GM_GROUPS, CHUNK, LANES), F32)],
        compiler_params=_params(1),
    )(proj, proj, dmerged, w_sp, b_sp_t, gv, hg)


def _sb_logits(z, strict):
    e = jnp.exp(-jnp.abs(z))
    ls = jnp.minimum(z, 0.0) - jnp.log(1.0 + e)
    l1m = ls - z
    if strict is not None:
        l1m = jnp.where(strict, l1m, 0.0)
    return ls, l1m, jnp.where(z >= 0.0, e, -e)


def _tri_sums(x, tri):
    hi, lo = _split_bf16(x)
    return _nn(jnp.concatenate([hi, lo], axis=1), jnp.concatenate([tri, tri], axis=0))


def _sb_weights(ls, in_tile, right, strict):
    a = jnp.exp(ls + in_tile + right)
    if strict is not None:
        a = jnp.where(strict, a, 0.0)
    return a


def _sb_masks(q_rows):
    row = lax.broadcasted_iota(jnp.int32, (SB_TILE, SB_TILE), 0)
    col = lax.broadcasted_iota(jnp.int32, (SB_TILE, SB_TILE), 1)
    lane = lax.broadcasted_iota(jnp.int32, (q_rows, LANES), 1)
    return row, col, lane < SB_HEAD_DIM


def _stack_heads(x, first):
    zero = jnp.zeros_like(x)
    return jnp.concatenate([jnp.where(first, x, zero), jnp.where(first, zero, x)], axis=0)


def _stack_heads_t(x_t):
    first_t = lax.broadcasted_iota(jnp.int32, x_t.shape, 0) < SB_HEAD_DIM
    zero = jnp.zeros_like(x_t)
    return jnp.concatenate([jnp.where(first_t, x_t, zero), jnp.where(first_t, zero, x_t)], axis=1).astype(BF16)


def _unstack_heads(x2, first):
    half = x2.shape[0] // 2
    return jnp.where(first, x2[:half], x2[half:])


def _stacked_col_minus_row(q_rows):
    row = lax.broadcasted_iota(jnp.int32, (2 * q_rows, SB_TILE), 0)
    col = lax.broadcasted_iota(jnp.int32, (2 * q_rows, SB_TILE), 1)
    return col - (row & (q_rows - 1))


def _head_mean(x, first):
    s0 = jnp.sum(jnp.where(first, x, 0.0), axis=-1, keepdims=True)
    s1 = jnp.sum(jnp.where(first, 0.0, x), axis=-1, keepdims=True)
    return jnp.where(first, s0, s1) * (1.0 / SB_HEAD_DIM)


def _riding(exchange, refs, n_in, n_out, n_scratch):
    n_x = exchange.n if exchange is not None else 0
    ins, rest = refs[:n_in], refs[n_in:]
    x_src, rest = rest[:n_x], rest[n_x:]
    outs, rest = rest[:n_out], rest[n_out:]
    x_dst, rest = rest[:n_x], rest[n_x:]
    return ins, outs, rest[:n_scratch], (x_src, x_dst, rest[n_scratch:])


def _riding_specs(exchange):
    if exchange is None:
        return [], [], [], [], []
    return exchange.in_specs, exchange.out_specs, exchange.out_shape, exchange.scratch, exchange.arrays


def _ride(exchange, x_refs, first_step, last_step):
    if exchange is None:
        return lambda: None

    @pl.when(first_step)
    def _():
        exchange.start(*x_refs)

    def finish():
        @pl.when(last_step)
        def _():
            exchange.wait(*x_refs)

    return finish


def _sb_fwd(proj, merged_a, hg, *, batch, seq, exchange=None):
    q0, k0, v0 = 2 * GM_WIDTH // LANES, 2 * GM_WIDTH // LANES + SB_PAIRS, 2 * GM_WIDTH // LANES + 2 * SB_PAIRS
    block_keys, q_rows = min(SB_BLOCK, seq), min(SB_QUERIES_FWD, seq)
    assert block_keys % q_rows == 0 and seq % block_keys == 0
    n_q, n_sub, n_blocks = seq // q_rows, block_keys // SB_TILE, seq // block_keys

    def body(*refs):
        (q_ref, k_ref, v_ref, hg_ref, _), (m_ref, raw_ref, a_ref, e_ref), _, x_refs = _riding(exchange, refs, 5, 4, 0)
        b, p, i = pl.program_id(0), pl.program_id(1), pl.program_id(2)
        finish = _ride(exchange, x_refs, (b == 0) & (p == 0) & (i == 0), (b == batch - 1) & (p == SB_PAIRS - 1) & (i == n_q - 1))
        row, col, first = _sb_masks(q_rows)
        upper = (row > col).astype(BF16)
        q2 = _stack_heads((q_ref[...].astype(F32) * SB_SCALE).astype(BF16), first)
        diff = _stacked_col_minus_row(q_rows)
        last = (i * q_rows) // block_keys
        offset = i * q_rows - last * block_keys

        def scores(jb):
            return tuple(_nt(q2, k_ref[pl.ds(pl.multiple_of((jb * n_sub + s) * SB_TILE, SB_TILE), SB_TILE), :]) for s in range(n_sub))

        def keep_for_backward(ref, jb, s, stacked):
            cols = slice(s * SB_TILE, (s + 1) * SB_TILE)
            ref[0, 0, jb, 0, :, cols] = stacked[:q_rows]
            ref[0, 0, jb, 1, :, cols] = stacked[q_rows:]

        def weights_of(jb, z, right, masked):
            keeps = [diff < offset - s * SB_TILE if masked else None for s in range(n_sub)]
            logits = [_sb_logits(z[s], keeps[s]) for s in range(n_sub)]
            totals = [jnp.sum(l1m, axis=-1, keepdims=True) for _, l1m, _ in logits]
            sums = [_tri_sums(l1m, upper) for _, l1m, _ in logits]
            weights = [None] * n_sub
            for s in reversed(range(n_sub)):
                weights[s] = _sb_weights(logits[s][0], sums[s], right, keeps[s]).astype(BF16)
                right = right + totals[s]
                keep_for_backward(a_ref, jb, s, weights[s])
                keep_for_backward(e_ref, jb, s, logits[s][2].astype(BF16))
            return right, jnp.concatenate(weights, axis=1)

        def values(jb):
            return v_ref[pl.ds(pl.multiple_of(jb * block_keys, block_keys), block_keys), :]

        right, a_prev = weights_of(last, scores(last), jnp.zeros((2 * q_rows, 1), F32), True)
        z_next = scores(jnp.maximum(last - 1, 0))

        def step(k, carry):
            right, acc, z, a_prev = carry
            jb = last - k
            acc = acc + _nn(a_prev, values(jb + 1))
            z_next = scores(jnp.maximum(jb - 1, 0))
            right, a = weights_of(jb, z, right, False)
            return right, acc, z_next, a

        _, acc2, _, a_prev = lax.fori_loop(1, last + 1, step, (right, jnp.zeros((2 * q_rows, LANES), F32), z_next, a_prev))
        acc = _unstack_heads(acc2 + _nn(a_prev, values(0)), first)
        raw_ref[...] = acc
        m_ref[...] = (acc * lax.rsqrt(_head_mean(acc * acc, first) + EPS) * hg_ref[...]).astype(BF16)
        finish()

    t = batch * seq
    blk = lambda c0: pl.BlockSpec((q_rows, LANES), lambda b, p, i: (b * n_q + i, c0 + p))
    kv = lambda c0: pl.BlockSpec((seq, LANES), lambda b, p, i: (b, c0 + p))
    kept = pl.BlockSpec((1, 1, n_blocks, 2, q_rows, block_keys), lambda b, p, i: (p, b, 0, 0, i, 0))
    kept_shape = jax.ShapeDtypeStruct((SB_PAIRS, batch, n_blocks, 2, seq, block_keys), BF16)
    x_in, x_out, x_shape, x_scratch, x_arrays = _riding_specs(exchange)
    res = pl.pallas_call(
        body, name="sb_fwd", grid=(batch, SB_PAIRS, n_q),
        in_specs=[blk(q0), kv(k0), kv(v0), pl.BlockSpec((1, LANES), lambda b, p, i: (0, SB_PAIRS + p)),
                  pl.BlockSpec(memory_space=pl.ANY)] + x_in,
        out_specs=[blk(SB_PAIRS), blk(0), kept, kept] + x_out,
        out_shape=[jax.ShapeDtypeStruct((t, 2 * GM_WIDTH), BF16), jax.ShapeDtypeStruct((t, GM_WIDTH), F32), kept_shape, kept_shape] + x_shape,
        scratch_shapes=x_scratch,
        input_output_aliases={4: 0},
        compiler_params=_params(3),
    )(proj, proj, proj, hg, merged_a, *x_arrays)
    return res[0], res[1], res[2], res[3], res[4:]


def _sb_bwd(proj, raw, weights, signed_e, dmerged, hg, *, batch, seq, exchange=None):
    q0, k0, v0 = 2 * GM_WIDTH // LANES, 2 * GM_WIDTH // LANES + SB_PAIRS, 2 * GM_WIDTH // LANES + 2 * SB_PAIRS
    block_keys, q_rows = min(SB_BLOCK, seq), min(SB_QUERIES_BWD, seq)
    assert block_keys % q_rows == 0 and seq % block_keys == 0
    n_q, n_sub, n_blocks = seq // q_rows, block_keys // SB_TILE, seq // block_keys

    def body(*refs):
        ins, outs, (dk_acc, dv_acc), x_refs = _riding(exchange, refs, 8, 4, 2)
        q_ref, k_ref, v_ref, raw_ref, a_ref, e_ref, dm_ref, hg_ref = ins
        dq_ref, dk_ref, dv_ref, dhg_ref = outs
        p, b, i = pl.program_id(0), pl.program_id(1), pl.program_id(2)
        finish = _ride(exchange, x_refs, (b == 0) & (p == 0) & (i == 0), (b == batch - 1) & (p == SB_PAIRS - 1) & (i == n_q - 1))
        row, col, first = _sb_masks(q_rows)
        lower = (row < col).astype(BF16)

        @pl.when(jnp.logical_and(b == 0, i == 0))
        def _():
            dhg_ref[...] = jnp.zeros_like(dhg_ref)

        @pl.when(i == 0)
        def _():
            dk_acc[...] = jnp.zeros_like(dk_acc)
            dv_acc[...] = jnp.zeros_like(dv_acc)

        raw_v = raw_ref[...]
        dm = dm_ref[...].astype(F32)
        r = lax.rsqrt(_head_mean(raw_v * raw_v, first) + EPS)
        nrm = raw_v * r
        dhg_ref[...] += jnp.sum(dm * nrm, axis=0, keepdims=True)
        dn = dm * hg_ref[...]
        dout = r * (dn - nrm * _head_mean(dn * nrm, first))
        dout2 = _stack_heads(dout.astype(BF16), first)
        q2_t = _stack_heads_t(q_ref[...].astype(F32).T)
        dout2_t = _stack_heads_t(dout.T)
        diff = _stacked_col_minus_row(q_rows)
        last = (i * q_rows) // block_keys
        offset = i * q_rows - last * block_keys

        def kept(ref, jb, s=None):
            cols = slice(None) if s is None else slice(s * SB_TILE, (s + 1) * SB_TILE)
            return jnp.concatenate([ref[0, 0, jb, 0, :, cols], ref[0, 0, jb, 1, :, cols]], axis=0)

        def block(jb, carry, masked):
            gleft, dq = carry
            tiles = [pl.ds(pl.multiple_of((jb * n_sub + s) * SB_TILE, SB_TILE), SB_TILE) for s in range(n_sub)]
            gmats = [_nt(dout2, v_ref[tiles[s], :]) * kept(a_ref, jb, s).astype(F32) for s in range(n_sub)]
            prefixes = [_tri_sums(g, lower) for g in gmats]
            dzs = []
            for s in range(n_sub):
                signed = kept(e_ref, jb, s).astype(F32)
                e = jnp.abs(signed)
                inv = 1.0 / (1.0 + e)
                small = e * inv
                positive = signed > 0.0
                beta, one_minus = jnp.where(positive, inv, small), jnp.where(positive, small, inv)
                dz = (gmats[s] * one_minus - (gleft + prefixes[s]) * beta) * SB_SCALE
                gleft = gleft + jnp.sum(gmats[s], axis=-1, keepdims=True)
                if masked:
                    dz = jnp.where(diff < offset - s * SB_TILE, dz, 0.0)
                dzs.append(dz.astype(BF16))
            dz_all = jnp.concatenate(dzs, axis=1)
            dk_acc[jb] += _nn(q2_t, dz_all)
            dv_acc[jb] += _nn(dout2_t, kept(a_ref, jb))
            return gleft, dq + _nn(dz_all, k_ref[pl.ds(pl.multiple_of(jb * block_keys, block_keys), block_keys), :])

        carry = (jnp.zeros((2 * q_rows, 1), F32), jnp.zeros((2 * q_rows, LANES), F32))
        carry = lax.fori_loop(0, last, lambda jb, c: block(jb, c, False), carry)
        dq_ref[...] = _unstack_heads(block(last, carry, True)[1], first).astype(BF16)

        @pl.when(i == n_q - 1)
        def _():
            for jb in range(n_blocks):
                for s in range(n_sub):
                    rows = slice((jb * n_sub + s) * SB_TILE, (jb * n_sub + s + 1) * SB_TILE)
                    cols = slice(s * SB_TILE, (s + 1) * SB_TILE)
                    dk_ref[rows, :] = dk_acc[jb, :, cols].T.astype(BF16)
                    dv_ref[rows, :] = dv_acc[jb, :, cols].T.astype(BF16)

        finish()

    t = batch * seq
    blk = lambda c0: pl.BlockSpec((q_rows, LANES), lambda p, b, i: (b * n_q + i, c0 + p))
    kv = lambda c0: pl.BlockSpec((seq, LANES), lambda p, b, i: (b, c0 + p))
    row_spec = pl.BlockSpec((1, LANES), lambda p, b, i: (0, SB_PAIRS + p))
    kept_spec = pl.BlockSpec((1, 1, n_blocks, 2, q_rows, block_keys), lambda p, b, i: (p, b, 0, 0, i, 0))
    x_in, x_out, x_shape, x_scratch, x_arrays = _riding_specs(exchange)
    res = pl.pallas_call(
        body, name="sb_bwd", grid=(SB_PAIRS, batch, n_q),
        in_specs=[blk(q0), kv(k0), kv(v0), blk(0), kept_spec, kept_spec, blk(SB_PAIRS), row_spec] + x_in,
        out_specs=[blk(0), kv(0), kv(0), pl.BlockSpec((1, LANES), lambda p, b, i: (0, p))] + x_out,
        out_shape=[jax.ShapeDtypeStruct((t, GM_WIDTH), BF16)] * 3 + [jax.ShapeDtypeStruct((1, GM_WIDTH), F32)] + x_shape,
        scratch_shapes=[pltpu.VMEM((n_blocks, LANES, block_keys), F32), pltpu.VMEM((n_blocks, LANES, block_keys), F32)] + x_scratch,
        compiler_params=_params(3),
    )(proj, proj, proj, raw, weights, signed_e, dmerged, hg, *x_arrays)
    return res[0], res[1], res[2], res[3], res[4:]


def _x_softmax(qh, kh):
    s = _nt(qh, kh) * X_SCALE
    p = jnp.exp(s - jnp.max(s, axis=-1, keepdims=True))
    return p / jnp.sum(p, axis=-1, keepdims=True)


def _xattn_fwd(q, kv, *, batch, seq, n_mem, tq=512):
    tq = min(tq, seq)
    n_q = seq // tq
    d = X_HEADS * X_HEAD_DIM

    def body(q_ref, kv_ref, o_ref):
        for h in range(X_HEADS):
            cols = slice(h * X_HEAD_DIM, (h + 1) * X_HEAD_DIM)
            p = _x_softmax(q_ref[:, cols], kv_ref[:, cols])
            o_ref[:, cols] = _nn(p.astype(BF16), kv_ref[:, d + h * X_HEAD_DIM:d + (h + 1) * X_HEAD_DIM]).astype(BF16)

    return pl.pallas_call(
        body, name="xattn_fwd", grid=(batch, n_q),
        in_specs=[pl.BlockSpec((tq, d), lambda b, i: (b * n_q + i, 0)), pl.BlockSpec((n_mem, 2 * d), lambda b, i: (b, 0))],
        out_specs=pl.BlockSpec((tq, d), lambda b, i: (b * n_q + i, 0)),
        out_shape=jax.ShapeDtypeStruct((batch * seq, d), BF16),
        compiler_params=_params(2),
    )(q, kv)


def _xattn_bwd(q, kv, do, *, batch, seq, n_mem, tq=512):
    tq = min(tq, seq)
    n_q = seq // tq
    d = X_HEADS * X_HEAD_DIM

    def body(q_ref, kv_ref, do_ref, dq_ref, dkv_ref, acc):
        i = pl.program_id(1)

        @pl.when(i == 0)
        def _():
            acc[...] = jnp.zeros_like(acc)

        for h in range(X_HEADS):
            cols = slice(h * X_HEAD_DIM, (h + 1) * X_HEAD_DIM)
            vcols = slice(d + h * X_HEAD_DIM, d + (h + 1) * X_HEAD_DIM)
            qh, kh, vh, doh = q_ref[:, cols], kv_ref[:, cols], kv_ref[:, vcols], do_ref[:, cols]
            p = _x_softmax(qh, kh)
            dp = _nt(doh, vh)
            acc[:, vcols] += _tn(p.astype(BF16), doh)
            ds = (p * (dp - jnp.sum(dp * p, axis=-1, keepdims=True)) * X_SCALE).astype(BF16)
            dq_ref[:, cols] = _nn(ds, kh).astype(BF16)
            acc[:, cols] += _tn(ds, qh)

        @pl.when(i == n_q - 1)
        def _():
            dkv_ref[...] = acc[...].astype(BF16)

    return pl.pallas_call(
        body, name="xattn_bwd", grid=(batch, n_q),
        in_specs=[pl.BlockSpec((tq, d), lambda b, i: (b * n_q + i, 0)), pl.BlockSpec((n_mem, 2 * d), lambda b, i: (b, 0)),
                  pl.BlockSpec((tq, d), lambda b, i: (b * n_q + i, 0))],
        out_specs=[pl.BlockSpec((tq, d), lambda b, i: (b * n_q + i, 0)), pl.BlockSpec((n_mem, 2 * d), lambda b, i: (b, 0))],
        out_shape=[jax.ShapeDtypeStruct((batch * seq, d), BF16), jax.ShapeDtypeStruct((batch * n_mem, 2 * d), BF16)],
        scratch_shapes=[pltpu.VMEM((n_mem, 2 * d), F32)],
        compiler_params=_params(2),
    )(q, kv, do)


def _loss_head(h, target, g, *, tm=512):
    t, d = h.shape
    tm = min(tm, t)
    n_steps = t // tm

    def body(h_ref, t_ref, g_ref, dh_ref, dg_ref, loss_ref, sq_acc):
        step = pl.program_id(0)

        @pl.when(step == 0)
        def _():
            dg_ref[...] = jnp.zeros_like(dg_ref)
            sq_acc[...] = jnp.zeros_like(sq_acc)

        hv = h_ref[...]
        gv = g_ref[...]
        r = _rstd(hv)
        xh = hv * r
        err = xh * gv - t_ref[...]
        sq_acc[...] += jnp.sum(err * err, axis=0, keepdims=True)
        dy = err * (1.0 / d)
        dg_ref[...] += jnp.sum(dy * xh, axis=0, keepdims=True)
        dxh = dy * gv
        dh_ref[...] = r * (dxh - xh * jnp.mean(dxh * xh, axis=-1, keepdims=True))

        @pl.when(step == n_steps - 1)
        def _():
            total = jnp.sum(sq_acc[...], axis=-1, keepdims=True) * (0.5 / d)
            loss_ref[...] = jnp.broadcast_to(total, loss_ref.shape)

    return pl.pallas_call(
        body, name="loss_head", grid=(n_steps,),
        in_specs=[pl.BlockSpec((tm, d), lambda i: (i, 0)), pl.BlockSpec((tm, d), lambda i: (i, 0)), pl.BlockSpec((1, d), lambda i: (0, 0))],
        out_specs=[pl.BlockSpec((tm, d), lambda i: (i, 0)), pl.BlockSpec((1, d), lambda i: (0, 0)), pl.BlockSpec((1, LANES), lambda i: (0, 0))],
        out_shape=[jax.ShapeDtypeStruct((t, d), F32), jax.ShapeDtypeStruct((1, d), F32), jax.ShapeDtypeStruct((1, LANES), F32)],
        scratch_shapes=[pltpu.VMEM((1, d), F32)],
        compiler_params=_params(1),
    )(h, target, g)


def _my_index():
    return 4 * lax.axis_index("x") + 2 * lax.axis_index("y") + lax.axis_index("c")


def _peers():
    x, y, c = lax.axis_index("x"), lax.axis_index("y"), lax.axis_index("c")
    out = []
    for rel in range(1, N_DEV):
        dx, dy, dc = (rel >> 2) & 1, (rel >> 1) & 1, rel & 1
        px, py, pc = x ^ dx, y ^ dy, c ^ dc
        out.append(((px, py, pc), 4 * px + 2 * py + pc))
    return out


class _Exchange:
    def __init__(self, arrays, scatter):
        self.arrays, self.scatter, self.n = list(arrays), scatter, len(arrays)
        any_spec = pl.BlockSpec(memory_space=pl.ANY)
        self.in_specs = [any_spec] * self.n
        self.out_specs = [any_spec] * self.n
        self.out_shape = [jax.ShapeDtypeStruct((N_DEV,) + tuple(a.shape[-2:]), a.dtype) for a in self.arrays]
        n_peer = N_DEV - 1
        self.scratch = [pltpu.SemaphoreType.DMA((self.n, n_peer)), pltpu.SemaphoreType.DMA((self.n, n_peer)),
                        pltpu.SemaphoreType.DMA((self.n,))]

    def _copies(self, srcs, dsts, sems, arriving):
        send_sems, recv_sems, local_sems = sems
        me = _my_index()
        local, remote = [], []
        for w in range(self.n):
            if not arriving:
                local.append(pltpu.make_async_copy(srcs[w].at[me] if self.scatter else srcs[w], dsts[w].at[me], local_sems.at[w]))
            for rel, (pos, idx) in enumerate(_peers()):
                remote.append(pltpu.make_async_remote_copy(
                    src_ref=srcs[w].at[idx] if self.scatter else srcs[w], dst_ref=dsts[w].at[idx if arriving else me],
                    send_sem=send_sems.at[w, rel], recv_sem=recv_sems.at[w, rel], device_id=pos, device_id_type=MESH))
        return local, remote

    def start(self, srcs, dsts, sems):
        local, sends = self._copies(srcs, dsts, sems, arriving=False)
        for cp in local + sends:
            cp.start()

    def wait(self, srcs, dsts, sems):
        for cp in self._copies(srcs, dsts, sems, arriving=True)[1]:
            cp.wait_recv()
        local, sends = self._copies(srcs, dsts, sems, arriving=False)
        for cp in sends:
            cp.wait_send()
        for cp in local:
            cp.wait()

    def run(self, name):
        n = self.n

        def body(*refs):
            srcs, dsts, sems = refs[:n], refs[n:2 * n], refs[2 * n:]
            self.start(srcs, dsts, sems)
            self.wait(srcs, dsts, sems)

        return pl.pallas_call(body, name=name, in_specs=self.in_specs, out_specs=self.out_specs, out_shape=self.out_shape,
                              scratch_shapes=self.scratch)(*self.arrays)


def _all_reduce_small(part):
    rows = part.shape[0]
    n_peer = N_DEV - 1

    def body(p_ref, o_ref, buf, send_sems, recv_sems):
        me = _my_index()
        peers = _peers()
        buf[me] = p_ref[...]
        sends = [pltpu.make_async_remote_copy(src_ref=p_ref, dst_ref=buf.at[me], send_sem=send_sems.at[rel], recv_sem=recv_sems.at[rel],
                                              device_id=peers[rel][0], device_id_type=MESH) for rel in range(n_peer)]
        for cp in sends:
            cp.start()
        for rel in range(n_peer):
            pltpu.make_async_remote_copy(src_ref=p_ref, dst_ref=buf.at[peers[rel][1]], send_sem=send_sems.at[rel], recv_sem=recv_sems.at[rel],
                                         device_id=peers[rel][0], device_id_type=MESH).wait_recv()
        for cp in sends:
            cp.wait_send()
        total = buf[0]
        for dev in range(1, N_DEV):
            total = total + buf[dev]
        o_ref[...] = total

    vmem = pl.BlockSpec(memory_space=pltpu.VMEM)
    return pl.pallas_call(
        body, name="all_reduce_small", in_specs=[vmem], out_specs=vmem, out_shape=jax.ShapeDtypeStruct(part.shape, F32),
        scratch_shapes=[pltpu.VMEM((N_DEV, rows, LANES), F32), pltpu.SemaphoreType.DMA((n_peer,)), pltpu.SemaphoreType.DMA((n_peer,))],
        compiler_params=pltpu.CompilerParams(has_side_effects=True, vmem_limit_bytes=VMEM_LIMIT),
    )(part)


def _adamw_math(w, g, m, v):
    m_new = ADAM_B1 * m + (1.0 - ADAM_B1) * g
    v_new = ADAM_B2 * v + (1.0 - ADAM_B2) * (g * g)
    m_hat = m_new / (1.0 - ADAM_B1 ** ADAM_STEP)
    v_hat = v_new / (1.0 - ADAM_B2 ** ADAM_STEP)
    delta = -ADAM_LR * (m_hat / (jnp.sqrt(v_hat) + ADAM_EPS) + ADAM_WD * w)
    return delta, m_new, v_new


def _adamw(parts, w, m, v, *, name, tr=64):
    rows, cols = w.shape
    tr = min(tr, rows)
    assert rows % tr == 0
    stacked = parts.ndim == 3

    def body(p_ref, w_ref, m_ref, v_ref, g_ref, d_ref, mo_ref, vo_ref):
        if stacked:
            g = p_ref[0].astype(F32)
            for dev in range(1, N_DEV):
                g = g + p_ref[dev].astype(F32)
        else:
            g = p_ref[...]
        delta, m_new, v_new = _adamw_math(w_ref[...], g, m_ref[...], v_ref[...])
        g_ref[...] = g
        d_ref[...] = delta
        mo_ref[...] = m_new
        vo_ref[...] = v_new

    tile = pl.BlockSpec((tr, cols), lambda i: (i, 0))
    p_spec = pl.BlockSpec((N_DEV, tr, cols), lambda i: (0, i, 0)) if stacked else tile
    return pl.pallas_call(
        body, name=name, grid=(rows // tr,), in_specs=[p_spec, tile, tile, tile], out_specs=[tile] * 4,
        out_shape=[jax.ShapeDtypeStruct((rows, cols), F32)] * 4, compiler_params=_params(1),
    )(parts, w, m, v)


_LATER = ("w_out", "w_cq", "w_ckv", "w_co", "w_ff1", "w_ff2")


def _as_rows(stacked):
    return stacked.reshape(-1, stacked.shape[-1])


def _local_step(x, mem, target, small, shards):
    batch, seq, d = x.shape
    n_mem = mem.shape[1]
    t = batch * seq
    x2, mem2, tgt2 = x.reshape(t, d), mem.reshape(batch * n_mem, d), target.reshape(t, d)
    g_mix, g_cross, g_mem, g_ffn, g_final = (small[k] for k in ("norm_mix_g", "norm_cross_g", "norm_mem_g", "norm_ffn_g", "norm_final_g"))
    gv, hg, w_sp, b_sp_t = small["gm_v_norm_g"], small["head_norm_g"], small["w_spatial"], small["b_spatial_t"]

    win_t = _as_rows(_Exchange([shards["w_in"]], scatter=False).run("gather_w_in")[0])
    proj, xn = _norm_mm(x2, g_mix, win_t, mode="nt", name="proj_fwd", tn=512)
    merged_a = _gmlp_fwd(proj, w_sp, b_sp_t, gv, hg)
    merged, sb_raw, sb_weights, sb_signed_e, gathered = _sb_fwd(proj, merged_a, hg, batch=batch, seq=seq,
                                                                exchange=_Exchange([shards[n] for n in _LATER], scatter=False))
    wout, wcq, wckv_t, wco, wff1_t, wff2 = (_as_rows(g) for g in gathered)
    h1 = _mm(merged, wout, mode="nn", out_dtype=F32, name="mix_out_fwd", epi=_epi_residual, epi_ins=(x2,))
    qx, hn1 = _norm_mm(h1, g_cross, wcq, mode="nn", name="xq_fwd")
    kvx, memn = _norm_mm(mem2, g_mem, wckv_t, mode="nt", name="xkv_fwd")
    o = _xattn_fwd(qx, kvx, batch=batch, seq=seq, n_mem=n_mem)
    h2 = _mm(o, wco, mode="nn", out_dtype=F32, name="xo_fwd", epi=_epi_residual, epi_ins=(h1,))
    fpre, hn2 = _norm_mm(h2, g_ffn, wff1_t, mode="nt", name="ff1_fwd")
    h3 = _mm(fpre, wff2, mode="nn", out_dtype=F32, name="ff2_fwd", a_fn=_relu2, epi=_epi_residual, epi_ins=(h2,))
    dh3, dg_final, loss_row = _loss_head(h3, tgt2, g_final)

    dpre = _mm(dh3, wff2, mode="nt", out_dtype=BF16, name="ff2_bwd_x", epi=_epi_relu2_grad, epi_ins=(fpre,))
    d_wff2 = _mm(fpre, dh3, mode="tn", out_dtype=BF16, name="ff2_bwd_w", a_fn=_relu2)
    d_wff1_t = _mm(dpre, hn2, mode="tn", out_dtype=BF16, name="ff1_bwd_w")
    dh2, dg_ffn = _mm(dpre, wff1_t, mode="nn", out_dtype=F32, name="ff1_bwd_x", tm=512, tk=wff1_t.shape[0], epi=_epi_rms_bwd,
                      epi_ins=(h2, dh3), vec_ins=(g_ffn,), aux=True)
    do = _mm(dh2, wco, mode="nt", out_dtype=BF16, name="xo_bwd_x")
    d_wco = _mm(o, dh2, mode="tn", out_dtype=BF16, name="xo_bwd_w")
    dqx, dkvx = _xattn_bwd(qx, kvx, do, batch=batch, seq=seq, n_mem=n_mem)
    d_wcq = _mm(hn1, dqx, mode="tn", out_dtype=BF16, name="xq_bwd_w")
    dh1, dg_cross = _mm(dqx, wcq, mode="nt", out_dtype=F32, name="xq_bwd_x", tm=512, epi=_epi_rms_bwd,
                        epi_ins=(h1, dh2), vec_ins=(g_cross,), aux=True)
    d_wckv_t = _mm(dkvx, memn, mode="tn", out_dtype=BF16, name="xkv_bwd_w")
    _, dg_mem = _mm(dkvx, wckv_t, mode="nn", out_dtype=BF16, name="xkv_bwd_x", tm=512, epi=_epi_rms_gain_only,
                    epi_ins=(mem2,), vec_ins=(g_mem,), aux=True)
    dmerged = _mm(dh1, wout, mode="nt", out_dtype=BF16, name="mix_out_bwd_x")
    d_wout = _mm(merged, dh1, mode="tn", out_dtype=BF16, name="mix_out_bwd_w")
    dp_a, d_wsp, d_bsp_t, d_gv, d_hg_a = _gmlp_bwd(proj, dmerged, w_sp, b_sp_t, gv, hg)
    d_later = {"w_out": d_wout, "w_cq": d_wcq, "w_ckv": d_wckv_t, "w_co": d_wco, "w_ff1": d_wff1_t, "w_ff2": d_wff2}
    scatter = _Exchange([d_later[n].reshape(N_DEV, -1, d) for n in _LATER], scatter=True)
    dq, dk, dv, d_hg_b, received = _sb_bwd(proj, sb_raw, sb_weights, sb_signed_e, dmerged, hg, batch=batch, seq=seq, exchange=scatter)
    dproj = jnp.concatenate([dp_a, dq, dk, dv], axis=1)
    d_win_t = _mm(dproj, xn, mode="tn", out_dtype=BF16, name="proj_bwd_w", tm=512)
    dx, dg_mix, d_win_received = _mm(dproj, win_t, mode="nn", out_dtype=F32, name="proj_bwd_x", tm=512, tk=win_t.shape[0],
                                     epi=_epi_rms_bwd, epi_ins=(x2, dh1), vec_ins=(g_mix,), aux=True,
                                     exchange=_Exchange([d_win_t.reshape(N_DEV, -1, d)], scatter=True))

    d_small = {"norm_mix_g": dg_mix, "gm_v_norm_g": d_gv, "w_spatial": d_wsp, "b_spatial_t": d_bsp_t, "head_norm_g": jnp.concatenate([d_hg_a, d_hg_b], axis=1),
               "norm_cross_g": dg_cross, "norm_mem_g": dg_mem, "norm_ffn_g": dg_ffn, "norm_final_g": dg_final}
    d_big = dict(zip(_LATER, received))
    d_big["w_in"] = d_win_received
    return loss_row, dx.reshape(batch, seq, d), d_small, d_big


_BIG = ("w_in", "w_out", "w_cq", "w_ckv", "w_co", "w_ff1", "w_ff2")
_BIG_TRANSPOSED = ("w_in", "w_ckv", "w_ff1")
_SMALL = ("norm_mix_g", "gm_v_norm_g", "w_spatial", "b_spatial", "head_norm_g", "norm_cross_g", "norm_mem_g", "norm_ffn_g", "norm_final_g")
_NAMES = ("norm_mix_g", "w_in", "gm_v_norm_g", "w_spatial", "b_spatial", "head_norm_g", "w_out", "norm_cross_g", "norm_mem_g",
          "w_cq", "w_ckv", "w_co", "norm_ffn_g", "w_ff1", "w_ff2", "norm_final_g")


def _rows_of(a):
    r = a.reshape(-1, LANES)
    pad = (-r.shape[0]) % 8
    return jnp.pad(r, ((0, pad), (0, 0))) if pad else r


def _shard2d(name, a):
    a = a[0]
    return a.T if name in _BIG_TRANSPOSED else a


def kernel(x, mem, norm_mix_g, w_in, gm_v_norm_g, w_spatial, b_spatial, head_norm_g, w_out, norm_cross_g, norm_mem_g, w_cq, w_ckv, w_co, norm_ffn_g, w_ff1, w_ff2, norm_final_g, loss_target, m_norm_mix_g, m_w_in, m_gm_v_norm_g, m_w_spatial, m_b_spatial, m_head_norm_g, m_w_out, m_norm_cross_g, m_norm_mem_g, m_w_cq, m_w_ckv, m_w_co, m_norm_ffn_g, m_w_ff1, m_w_ff2, m_norm_final_g, v_norm_mix_g, v_w_in, v_gm_v_norm_g, v_w_spatial, v_b_spatial, v_head_norm_g, v_w_out, v_norm_cross_g, v_norm_mem_g, v_w_cq, v_w_ckv, v_w_co, v_norm_ffn_g, v_w_ff1, v_w_ff2, v_norm_final_g):
    weights = dict(norm_mix_g=norm_mix_g, w_in=w_in, gm_v_norm_g=gm_v_norm_g, w_spatial=w_spatial, b_spatial=b_spatial,
                   head_norm_g=head_norm_g, w_out=w_out, norm_cross_g=norm_cross_g, norm_mem_g=norm_mem_g, w_cq=w_cq, w_ckv=w_ckv,
                   w_co=w_co, norm_ffn_g=norm_ffn_g, w_ff1=w_ff1, w_ff2=w_ff2, norm_final_g=norm_final_g)
    mom1 = dict(norm_mix_g=m_norm_mix_g, w_in=m_w_in, gm_v_norm_g=m_gm_v_norm_g, w_spatial=m_w_spatial, b_spatial=m_b_spatial,
                head_norm_g=m_head_norm_g, w_out=m_w_out, norm_cross_g=m_norm_cross_g, norm_mem_g=m_norm_mem_g, w_cq=m_w_cq,
                w_ckv=m_w_ckv, w_co=m_w_co, norm_ffn_g=m_norm_ffn_g, w_ff1=m_w_ff1, w_ff2=m_w_ff2, norm_final_g=m_norm_final_g)
    mom2 = dict(norm_mix_g=v_norm_mix_g, w_in=v_w_in, gm_v_norm_g=v_gm_v_norm_g, w_spatial=v_w_spatial, b_spatial=v_b_spatial,
                head_norm_g=v_head_norm_g, w_out=v_w_out, norm_cross_g=v_norm_cross_g, norm_mem_g=v_norm_mem_g, w_cq=v_w_cq,
                w_ckv=v_w_ckv, w_co=v_w_co, norm_ffn_g=v_norm_ffn_g, w_ff1=v_w_ff1, w_ff2=v_w_ff2, norm_final_g=v_norm_final_g)

    shards = {n: _shard2d(n, weights[n]).astype(BF16) for n in _BIG}
    small = {n: weights[n].reshape(1, -1) for n in _SMALL if n not in ("w_spatial", "b_spatial")}
    small["w_spatial"] = w_spatial[0]
    small["b_spatial_t"] = b_spatial[0].T
    loss_row, grad_x, d_small, d_big = _local_step(x, mem, loss_target, small, shards)

    d_small["b_spatial"] = d_small.pop("b_spatial_t").T
    packed = jnp.concatenate([_rows_of(d_small[n]) for n in _SMALL] + [jnp.pad(loss_row, ((0, 7), (0, 0)))], axis=0)
    summed = _all_reduce_small(packed)

    grads, deltas, new_m, new_v = {}, {}, {}, {}
    for n in _BIG:
        outs = _adamw(d_big[n], _shard2d(n, weights[n]), _shard2d(n, mom1[n]), _shard2d(n, mom2[n]), name="adamw_" + n)
        outs = [o.T if n in _BIG_TRANSPOSED else o for o in outs]
        grads[n], deltas[n], new_m[n], new_v[n] = (o[None] for o in outs)
    pack = lambda src: jnp.concatenate([_rows_of(src[n]) for n in _SMALL], axis=0)
    n_small_rows = sum(_rows_of(weights[n]).shape[0] for n in _SMALL)
    outs = _adamw(summed[:n_small_rows], pack(weights), pack(mom1), pack(mom2), name="adamw_small", tr=n_small_rows)
    at = 0
    for n in _SMALL:
        used = weights[n].size // LANES
        for dst, o in zip((grads, deltas, new_m, new_v), outs):
            dst[n] = o[at:at + used].reshape(weights[n].shape)
        at += _rows_of(weights[n]).shape[0]
    loss = summed[n_small_rows, 0]
    return (loss, grad_x, *[grads[n] for n in _NAMES], *[deltas[n] for n in _NAMES], *[new_m[n] for n in _NAMES],
            *[new_v[n] for n in _NAMES])
```

```python
import functools
import math

import jax
import jax.numpy as jnp
from jax import lax
from jax.experimental import pallas as pl
from jax.experimental.pallas import tpu as pltpu

F32 = jnp.float32
BF16 = jnp.bfloat16
EPS = 1e-6
N_DEV = 8
LANES = 128
CHUNK = 128
GM_GROUPS = 4
GM_WIDTH = 512
SB_PAIRS = 4
SB_HEAD_DIM = 64
SB_SCALE = 0.125
SB_TILE = 128
SB_BLOCK = 512
SB_QUERIES_FWD = 256
SB_QUERIES_BWD = 256
X_HEADS = 4
X_HEAD_DIM = 256
X_SCALE = 1.0 / 16.0
VMEM_LIMIT = 56 * 1024 * 1024
ADAM_LR, ADAM_B1, ADAM_B2, ADAM_EPS, ADAM_WD, ADAM_STEP = 0.001, 0.9, 0.999, 1e-08, 0.01, 10
MESH = pl.DeviceIdType.MESH


def _params(n_axes):
    return pltpu.CompilerParams(dimension_semantics=("arbitrary",) * n_axes, vmem_limit_bytes=VMEM_LIMIT)


def _dot(a, b, dims):
    return lax.dot_general(a, b, (dims, ((), ())), preferred_element_type=F32)


def _nn(a, b):
    return _dot(a, b, ((1,), (0,)))


def _nt(a, b):
    return _dot(a, b, ((1,), (1,)))


def _tn(a, b):
    return _dot(a, b, ((0,), (0,)))


_MODES = {"nn": _nn, "nt": _nt, "tn": _tn}


def _rstd(x):
    return lax.rsqrt(jnp.mean(x * x, axis=-1, keepdims=True) + EPS)


def _gelu(x):
    c = math.sqrt(2.0 / math.pi)
    t = jnp.tanh(c * (x + 0.044715 * x * x * x))
    return 0.5 * x * (1.0 + t)


def _gelu_grad(x):
    c = math.sqrt(2.0 / math.pi)
    t = jnp.tanh(c * (x + 0.044715 * x * x * x))
    return 0.5 * (1.0 + t) + 0.5 * x * (1.0 - t * t) * c * (1.0 + 3 * 0.044715 * x * x)


def _split_bf16(x):
    hi = x.astype(BF16)
    lo = (x - hi.astype(F32)).astype(BF16)
    return hi, lo


def _mm(a, b, *, mode, out_dtype, name, tm=1024, tn=1024, tk=1024, a_fn=None, epi=None, epi_ins=(), vec_ins=(), aux=False,
        exchange=None):
    if mode == "nn":
        (m, k), (k2, n) = a.shape, b.shape
    elif mode == "nt":
        (m, k), (n, k2) = a.shape, b.shape
    else:
        (k, m), (k2, n) = a.shape, b.shape
    assert k == k2, (a.shape, b.shape, mode)
    tm, tn, tk = min(tm, m), min(tn, n), min(tk, k)
    assert m % tm == 0 and n % tn == 0 and k % tk == 0, (m, n, k, tm, tn, tk)
    n_m, n_n, n_k = m // tm, n // tn, k // tk
    assert not aux or n_n == 1
    dot = _MODES[mode]
    n_epi, n_vec = len(epi_ins), len(vec_ins)

    def body(*refs):
        ins, outs, scratch, x_refs = _riding(exchange, refs, 2 + n_epi + n_vec, 2 if aux else 1, 1 if n_k > 1 else 0)
        a_ref, b_ref, epi_refs = ins[0], ins[1], ins[2:]
        o_ref = outs[0]
        aux_ref = outs[1] if aux else None
        acc_ref = scratch[0] if n_k > 1 else None
        i, j, kk = pl.program_id(0), pl.program_id(1), pl.program_id(2)
        ride_done = _ride(exchange, x_refs, (i == 0) & (j == 0) & (kk == 0), (i == n_m - 1) & (j == n_n - 1) & (kk == n_k - 1))
        av = a_ref[...]
        if a_fn is not None:
            av = a_fn(av)
        part = dot(av.astype(BF16), b_ref[...].astype(BF16))

        def finish(acc):
            if epi is None:
                o_ref[...] = acc.astype(out_dtype)
                return
            res = epi(acc, *[r[...] for r in epi_refs])
            if aux:
                res, row = res

                @pl.when(i == 0)
                def _():
                    aux_ref[...] = row

                @pl.when(i != 0)
                def _():
                    aux_ref[...] += row
            o_ref[...] = res.astype(out_dtype)

        if n_k == 1:
            finish(part)
        else:
            @pl.when(kk == 0)
            def _():
                acc_ref[...] = part

            @pl.when(kk != 0)
            def _():
                acc_ref[...] += part

            @pl.when(kk == n_k - 1)
            def _():
                finish(acc_ref[...])

        ride_done()

    if mode == "tn":
        a_spec = pl.BlockSpec((tk, tm), lambda i, j, kk: (kk, i))
    else:
        a_spec = pl.BlockSpec((tm, tk), lambda i, j, kk: (i, kk))
    if mode == "nt":
        b_spec = pl.BlockSpec((tn, tk), lambda i, j, kk: (j, kk))
    else:
        b_spec = pl.BlockSpec((tk, tn), lambda i, j, kk: (kk, j))
    tile_spec = pl.BlockSpec((tm, tn), lambda i, j, kk: (i, j))
    row_spec = pl.BlockSpec((1, tn), lambda i, j, kk: (0, j))
    out_shape = [jax.ShapeDtypeStruct((m, n), out_dtype)]
    out_specs = [tile_spec]
    if aux:
        out_shape.append(jax.ShapeDtypeStruct((1, n), F32))
        out_specs.append(row_spec)
    x_in, x_out, x_shape, x_scratch, x_arrays = _riding_specs(exchange)
    res = pl.pallas_call(
        body, name=name, grid=(n_m, n_n, n_k),
        in_specs=[a_spec, b_spec] + [tile_spec] * n_epi + [row_spec] * n_vec + x_in,
        out_specs=out_specs + x_out, out_shape=out_shape + x_shape,
        scratch_shapes=([pltpu.VMEM((tm, tn), F32)] if n_k > 1 else []) + x_scratch,
        compiler_params=_params(3),
    )(a, b, *epi_ins, *vec_ins, *x_arrays)
    if exchange is not None:
        return tuple(res)
    return res if aux else res[0]


def _norm_mm(x, g, w, *, mode, name, tm=1024, tn=1024):
    m, d = x.shape
    n = w.shape[0] if mode == "nt" else w.shape[1]
    tm, tn = min(tm, m), min(tn, n)
    assert m % tm == 0 and n % tn == 0
    dot = _MODES[mode]

    def body(x_ref, g_ref, w_ref, o_ref, xn_ref, xn_s):
        @pl.when(pl.program_id(1) == 0)
        def _():
            xv = x_ref[...]
            xn = (xv * _rstd(xv) * g_ref[...]).astype(BF16)
            xn_s[...] = xn
            xn_ref[...] = xn

        o_ref[...] = dot(xn_s[...], w_ref[...]).astype(BF16)

    w_spec = pl.BlockSpec((tn, d), lambda i, j: (j, 0)) if mode == "nt" else pl.BlockSpec((d, tn), lambda i, j: (0, j))
    return pl.pallas_call(
        body, name=name, grid=(m // tm, n // tn),
        in_specs=[pl.BlockSpec((tm, d), lambda i, j: (i, 0)), pl.BlockSpec((1, d), lambda i, j: (0, 0)), w_spec],
        out_specs=[pl.BlockSpec((tm, tn), lambda i, j: (i, j)), pl.BlockSpec((tm, d), lambda i, j: (i, 0))],
        out_shape=[jax.ShapeDtypeStruct((m, n), BF16), jax.ShapeDtypeStruct((m, d), BF16)],
        scratch_shapes=[pltpu.VMEM((tm, d), BF16)],
        compiler_params=_params(2),
    )(x, g, w)


def _epi_residual(acc, res):
    return res + acc


def _epi_relu2_grad(acc, pre):
    return acc * (2.0 * jnp.maximum(pre.astype(F32), 0.0))


def _relu2(pre):
    r = jnp.maximum(pre.astype(F32), 0.0)
    return r * r


def _epi_rms_bwd(acc, h, dres, g):
    r = _rstd(h)
    xh = h * r
    dxh = acc * g
    dh = dres + r * (dxh - xh * jnp.mean(dxh * xh, axis=-1, keepdims=True))
    return dh, jnp.sum(acc * xh, axis=0, keepdims=True)


def _epi_rms_gain_only(acc, h, g):
    return acc, jnp.sum(acc * (h * _rstd(h)), axis=0, keepdims=True)


def _tril(n):
    row = lax.broadcasted_iota(jnp.int32, (n, n), 0)
    col = lax.broadcasted_iota(jnp.int32, (n, n), 1)
    return col <= row


def _gmlp_fwd(proj, w_sp, b_sp_t, gv, hg, *, rows=512):
    t = proj.shape[0]
    rows = min(rows, t)
    n_c = rows // CHUNK

    def body(u_ref, v_ref, w_ref, bt_ref, gv_ref, hg_ref, m_ref):
        keep = _tril(CHUNK)
        for g in range(GM_GROUPS):
            cols = slice(g * LANES, (g + 1) * LANES)
            wg = jnp.where(keep, w_ref[g], 0.0).astype(BF16)
            u = _gelu(u_ref[:, cols].astype(F32))
            v = _gelu(v_ref[:, cols].astype(F32))
            vn = (v * _rstd(v) * gv_ref[:, cols]).astype(BF16)
            bias = bt_ref[:, g:g + 1]
            for c in range(n_c):
                rs = slice(c * CHUNK, (c + 1) * CHUNK)
                mixed = _nn(wg, vn[rs]) + bias
                a = u[rs] * mixed
                m_ref[rs, cols] = (a * _rstd(a) * hg_ref[:, cols]).astype(BF16)

    full = lambda shape: pl.BlockSpec(shape, lambda i: (0,) * len(shape))
    return pl.pallas_call(
        body, name="gmlp_fwd", grid=(t // rows,),
        in_specs=[pl.BlockSpec((rows, GM_WIDTH), lambda i: (i, 0)), pl.BlockSpec((rows, GM_WIDTH), lambda i: (i, 1)),
                  full((GM_GROUPS, CHUNK, CHUNK)), full((CHUNK, GM_GROUPS)), full((1, GM_WIDTH)), full((1, GM_WIDTH))],
        out_specs=pl.BlockSpec((rows, GM_WIDTH), lambda i: (i, 0)),
        out_shape=jax.ShapeDtypeStruct((t, 2 * GM_WIDTH), BF16),
        compiler_params=_params(1),
    )(proj, proj, w_sp, b_sp_t, gv, hg)


def _gmlp_bwd(proj, dmerged, w_sp, b_sp_t, gv, hg, *, rows=512):
    t = proj.shape[0]
    rows = min(rows, t)
    n_c = rows // CHUNK
    n_steps = t // rows

    def body(u_ref, v_ref, dm_ref, w_ref, bt_ref, gv_ref, hg_ref, dp_ref, dw_ref, dbt_ref, dgv_ref, dhg_ref, db_acc):
        step = pl.program_id(0)
        keep = _tril(CHUNK)

        @pl.when(step == 0)
        def _():
            dw_ref[...] = jnp.zeros_like(dw_ref)
            db_acc[...] = jnp.zeros_like(db_acc)
            dgv_ref[...] = jnp.zeros_like(dgv_ref)
            dhg_ref[...] = jnp.zeros_like(dhg_ref)

        for g in range(GM_GROUPS):
            cols = slice(g * LANES, (g + 1) * LANES)
            wg = jnp.where(keep, w_ref[g], 0.0).astype(BF16)
            u_pre = u_ref[:, cols].astype(F32)
            v_pre = v_ref[:, cols].astype(F32)
            u = _gelu(u_pre)
            v = _gelu(v_pre)
            r = _rstd(v)
            xh = v * r
            gvg = gv_ref[:, cols]
            hgg = hg_ref[:, cols]
            vn = (xh * gvg).astype(BF16)
            bias = bt_ref[:, g:g + 1]
            dm = dm_ref[:, cols].astype(F32)
            du_parts, dvn_parts = [], []
            dw = jnp.zeros((CHUNK, CHUNK), F32)
            db = jnp.zeros((CHUNK, LANES), F32)
            dhg = jnp.zeros((1, LANES), F32)
            for c in range(n_c):
                rs = slice(c * CHUNK, (c + 1) * CHUNK)
                mixed = _nn(wg, vn[rs]) + bias
                a = u[rs] * mixed
                ra = _rstd(a)
                an = a * ra
                dhg = dhg + jnp.sum(dm[rs] * an, axis=0, keepdims=True)
                dan = dm[rs] * hgg
                da = ra * (dan - an * jnp.mean(dan * an, axis=-1, keepdims=True))
                du_parts.append(da * mixed)
                dmixed = da * u[rs]
                db = db + dmixed
                dmb = dmixed.astype(BF16)
                dw = dw + _nt(dmb, vn[rs])
                dvn_parts.append(_tn(wg, dmb))
            du = jnp.concatenate(du_parts, axis=0)
            dvn = jnp.concatenate(dvn_parts, axis=0)
            dw_ref[g] += dw
            db_acc[g] += db
            dhg_ref[:, cols] += dhg
            dgv_ref[:, cols] += jnp.sum(dvn * xh, axis=0, keepdims=True)
            dxh = dvn * gvg
            dv = r * (dxh - xh * jnp.mean(dxh * xh, axis=-1, keepdims=True))
            dp_ref[:, cols] = (du * _gelu_grad(u_pre)).astype(BF16)
            dp_ref[:, GM_WIDTH + g * LANES:GM_WIDTH + (g + 1) * LANES] = (dv * _gelu_grad(v_pre)).astype(BF16)

        @pl.when(step == n_steps - 1)
        def _():
            for g in range(GM_GROUPS):
                dw_ref[g] = jnp.where(keep, dw_ref[g], 0.0)
                dbt_ref[:, g:g + 1] = jnp.sum(db_acc[g], axis=-1, keepdims=True)

    full = lambda shape: pl.BlockSpec(shape, lambda i: (0,) * len(shape))
    return pl.pallas_call(
        body, name="gmlp_bwd", grid=(n_steps,),
        in_specs=[pl.BlockSpec((rows, GM_WIDTH), lambda i: (i, 0)), pl.BlockSpec((rows, GM_WIDTH), lambda i: (i, 1)),
                  pl.BlockSpec((rows, GM_WIDTH), lambda i: (i, 0)),
                  full((GM_GROUPS, CHUNK, CHUNK)), full((CHUNK, GM_GROUPS)), full((1, GM_WIDTH)), full((1, GM_WIDTH))],
        out_specs=[pl.BlockSpec((rows, 2 * GM_WIDTH), lambda i: (i, 0)), full((GM_GROUPS, CHUNK, CHUNK)),
                   full((CHUNK, GM_GROUPS)), full((1, GM_WIDTH)), full((1, GM_WIDTH))],
        out_shape=[jax.ShapeDtypeStruct((t, 2 * GM_WIDTH), BF16), jax.ShapeDtypeStruct((GM_GROUPS, CHUNK, CHUNK), F32),
                   jax.ShapeDtypeStruct((CHUNK, GM_GROUPS), F32), jax.ShapeDtypeStruct((1, GM_WIDTH), F32),
                   jax.ShapeDtypeStruct((1, GM_WIDTH), F32)],
        scratch_shapes=[pltpu.VMEM((GM_GROUPS, CHUNK, LANES), F32)],
        compiler_params=_params(1),
    )(proj, proj, dmerged, w_sp, b_sp_t, gv, hg)


def _sb_logits(z, strict):
    e = jnp.exp(-jnp.abs(z))
    ls = jnp.minimum(z, 0.0) - jnp.log(1.0 + e)
    l1m = ls - z
    if strict is not None:
        l1m = jnp.where(strict, l1m, 0.0)
    return ls, l1m, jnp.where(z >= 0.0, e, -e)


def _tri_sums(x, tri):
    hi, lo = _split_bf16(x)
    return _nn(jnp.concatenate([hi, lo], axis=1), jnp.concatenate([tri, tri], axis=0))


def _sb_weights(ls, in_tile, right, strict):
    a = jnp.exp(ls + in_tile + right)
    if strict is not None:
        a = jnp.where(strict, a, 0.0)
    return a


def _sb_masks(q_rows):
    row = lax.broadcasted_iota(jnp.int32, (SB_TILE, SB_TILE), 0)
    col = lax.broadcasted_iota(jnp.int32, (SB_TILE, SB_TILE), 1)
    lane = lax.broadcasted_iota(jnp.int32, (q_rows, LANES), 1)
    return row, col, lane < SB_HEAD_DIM


def _stack_heads(x, first):
    zero = jnp.zeros_like(x)
    return jnp.concatenate([jnp.where(first, x, zero), jnp.where(first, zero, x)], axis=0)


def _stack_heads_t(x_t):
    first_t = lax.broadcasted_iota(jnp.int32, x_t.shape, 0) < SB_HEAD_DIM
    zero = jnp.zeros_like(x_t)
    return jnp.concatenate([jnp.where(first_t, x_t, zero), jnp.where(first_t, zero, x_t)], axis=1).astype(BF16)


def _unstack_heads(x2, first):
    half = x2.shape[0] // 2
    return jnp.where(first, x2[:half], x2[half:])


def _stacked_col_minus_row(q_rows):
    row = lax.broadcasted_iota(jnp.int32, (2 * q_rows, SB_TILE), 0)
    col = lax.broadcasted_iota(jnp.int32, (2 * q_rows, SB_TILE), 1)
    return col - (row & (q_rows - 1))


def _head_mean(x, first):
    s0 = jnp.sum(jnp.where(first, x, 0.0), axis=-1, keepdims=True)
    s1 = jnp.sum(jnp.where(first, 0.0, x), axis=-1, keepdims=True)
    return jnp.where(first, s0, s1) * (1.0 / SB_HEAD_DIM)


def _riding(exchange, refs, n_in, n_out, n_scratch):
    n_x = exchange.n if exchange is not None else 0
    ins, rest = refs[:n_in], refs[n_in:]
    x_src, rest = rest[:n_x], rest[n_x:]
    outs, rest = rest[:n_out], rest[n_out:]
    x_dst, rest = rest[:n_x], rest[n_x:]
    return ins, outs, rest[:n_scratch], (x_src, x_dst, rest[n_scratch:])


def _riding_specs(exchange):
    if exchange is None:
        return [], [], [], [], []
    return exchange.in_specs, exchange.out_specs, exchange.out_shape, exchange.scratch, exchange.arrays


def _ride(exchange, x_refs, first_step, last_step):
    if exchange is None:
        return lambda: None

    @pl.when(first_step)
    def _():
        exchange.start(*x_refs)

    def finish():
        @pl.when(last_step)
        def _():
            exchange.wait(*x_refs)

    return finish


def _sb_fwd(proj, merged_a, hg, *, batch, seq, exchange=None):
    q0, k0, v0 = 2 * GM_WIDTH // LANES, 2 * GM_WIDTH // LANES + SB_PAIRS, 2 * GM_WIDTH // LANES + 2 * SB_PAIRS
    block_keys, q_rows = min(SB_BLOCK, seq), min(SB_QUERIES_FWD, seq)
    assert block_keys % q_rows == 0 and seq % block_keys == 0
    n_q, n_sub, n_blocks = seq // q_rows, block_keys // SB_TILE, seq // block_keys

    def body(*refs):
        (q_ref, k_ref, v_ref, hg_ref, _), (m_ref, raw_ref, a_ref, e_ref), _, x_refs = _riding(exchange, refs, 5, 4, 0)
        b, p, i = pl.program_id(0), pl.program_id(1), pl.program_id(2)
        finish = _ride(exchange, x_refs, (b == 0) & (p == 0) & (i == 0), (b == batch - 1) & (p == SB_PAIRS - 1) & (i == n_q - 1))
        row, col, first = _sb_masks(q_rows)
        upper = (row > col).astype(BF16)
        q2 = _stack_heads((q_ref[...].astype(F32) * SB_SCALE).astype(BF16), first)
        diff = _stacked_col_minus_row(q_rows)
        last = (i * q_rows) // block_keys
        offset = i * q_rows - last * block_keys

        def scores(jb):
            return tuple(_nt(q2, k_ref[pl.ds(pl.multiple_of((jb * n_sub + s) * SB_TILE, SB_TILE), SB_TILE), :]) for s in range(n_sub))

        def keep_for_backward(ref, jb, s, stacked):
            cols = slice(s * SB_TILE, (s + 1) * SB_TILE)
            ref[0, 0, jb, 0, :, cols] = stacked[:q_rows]
            ref[0, 0, jb, 1, :, cols] = stacked[q_rows:]

        def weights_of(jb, z, right, masked):
            keeps = [diff < offset - s * SB_TILE if masked else None for s in range(n_sub)]
            logits = [_sb_logits(z[s], keeps[s]) for s in range(n_sub)]
            totals = [jnp.sum(l1m, axis=-1, keepdims=True) for _, l1m, _ in logits]
            sums = [_tri_sums(l1m, upper) for _, l1m, _ in logits]
            weights = [None] * n_sub
            for s in reversed(range(n_sub)):
                weights[s] = _sb_weights(logits[s][0], sums[s], right, keeps[s]).astype(BF16)
                right = right + totals[s]
                keep_for_backward(a_ref, jb, s, weights[s])
                keep_for_backward(e_ref, jb, s, logits[s][2].astype(BF16))
            return right, jnp.concatenate(weights, axis=1)

        def values(jb):
            return v_ref[pl.ds(pl.multiple_of(jb * block_keys, block_keys), block_keys), :]

        right, a_prev = weights_of(last, scores(last), jnp.zeros((2 * q_rows, 1), F32), True)
        z_next = scores(jnp.maximum(last - 1, 0))

        def step(k, carry):
            right, acc, z, a_prev = carry
            jb = last - k
            acc = acc + _nn(a_prev, values(jb + 1))
            z_next = scores(jnp.maximum(jb - 1, 0))
            right, a = weights_of(jb, z, right, False)
            return right, acc, z_next, a

        _, acc2, _, a_prev = lax.fori_loop(1, last + 1, step, (right, jnp.zeros((2 * q_rows, LANES), F32), z_next, a_prev))
        acc = _unstack_heads(acc2 + _nn(a_prev, values(0)), first)
        raw_ref[...] = acc
        m_ref[...] = (acc * lax.rsqrt(_head_mean(acc * acc, first) + EPS) * hg_ref[...]).astype(BF16)
        finish()

    t = batch * seq
    blk = lambda c0: pl.BlockSpec((q_rows, LANES), lambda b, p, i: (b * n_q + i, c0 + p))
    kv = lambda c0: pl.BlockSpec((seq, LANES), lambda b, p, i: (b, c0 + p))
    kept = pl.BlockSpec((1, 1, n_blocks, 2, q_rows, block_keys), lambda b, p, i: (p, b, 0, 0, i, 0))
    kept_shape = jax.ShapeDtypeStruct((SB_PAIRS, batch, n_blocks, 2, seq, block_keys), BF16)
    x_in, x_out, x_shape, x_scratch, x_arrays = _riding_specs(exchange)
    res = pl.pallas_call(
        body, name="sb_fwd", grid=(batch, SB_PAIRS, n_q),
        in_specs=[blk(q0), kv(k0), kv(v0), pl.BlockSpec((1, LANES), lambda b, p, i: (0, SB_PAIRS + p)),
                  pl.BlockSpec(memory_space=pl.ANY)] + x_in,
        out_specs=[blk(SB_PAIRS), blk(0), kept, kept] + x_out,
        out_shape=[jax.ShapeDtypeStruct((t, 2 * GM_WIDTH), BF16), jax.ShapeDtypeStruct((t, GM_WIDTH), F32), kept_shape, kept_shape] + x_shape,
        scratch_shapes=x_scratch,
        input_output_aliases={4: 0},
        compiler_params=_params(3),
    )(proj, proj, proj, hg, merged_a, *x_arrays)
    return res[0], res[1], res[2], res[3], res[4:]


def _sb_bwd(proj, raw, weights, signed_e, dmerged, hg, *, batch, seq, exchange=None):
    q0, k0, v0 = 2 * GM_WIDTH // LANES, 2 * GM_WIDTH // LANES + SB_PAIRS, 2 * GM_WIDTH // LANES + 2 * SB_PAIRS
    block_keys, q_rows = min(SB_BLOCK, seq), min(SB_QUERIES_BWD, seq)
    assert block_keys % q_rows == 0 and seq % block_keys == 0
    n_q, n_sub, n_blocks = seq // q_rows, block_keys // SB_TILE, seq // block_keys

    def body(*refs):
        ins, outs, (dk_acc, dv_acc), x_refs = _riding(exchange, refs, 8, 4, 2)
        q_ref, k_ref, v_ref, raw_ref, a_ref, e_ref, dm_ref, hg_ref = ins
        dq_ref, dk_ref, dv_ref, dhg_ref = outs
        p, b, i = pl.program_id(0), pl.program_id(1), pl.program_id(2)
        finish = _ride(exchange, x_refs, (b == 0) & (p == 0) & (i == 0), (b == batch - 1) & (p == SB_PAIRS - 1) & (i == n_q - 1))
        row, col, first = _sb_masks(q_rows)
        lower = (row < col).astype(BF16)

        @pl.when(jnp.logical_and(b == 0, i == 0))
        def _():
            dhg_ref[...] = jnp.zeros_like(dhg_ref)

        @pl.when(i == 0)
        def _():
            dk_acc[...] = jnp.zeros_like(dk_acc)
            dv_acc[...] = jnp.zeros_like(dv_acc)

        raw_v = raw_ref[...]
        dm = dm_ref[...].astype(F32)
        r = lax.rsqrt(_head_mean(raw_v * raw_v, first) + EPS)
        nrm = raw_v * r
        dhg_ref[...] += jnp.sum(dm * nrm, axis=0, keepdims=True)
        dn = dm * hg_ref[...]
        dout = r * (dn - nrm * _head_mean(dn * nrm, first))
        dout2 = _stack_heads(dout.astype(BF16), first)
        q2_t = _stack_heads_t(q_ref[...].astype(F32).T)
        dout2_t = _stack_heads_t(dout.T)
        diff = _stacked_col_minus_row(q_rows)
        last = (i * q_rows) // block_keys
        offset = i * q_rows - last * block_keys

        def kept(ref, jb, s=None):
            cols = slice(None) if s is None else slice(s * SB_TILE, (s + 1) * SB_TILE)
            return jnp.concatenate([ref[0, 0, jb, 0, :, cols], ref[0, 0, jb, 1, :, cols]], axis=0)

        def block(jb, carry, masked):
            gleft, dq = carry
            tiles = [pl.ds(pl.multiple_of((jb * n_sub + s) * SB_TILE, SB_TILE), SB_TILE) for s in range(n_sub)]
            gmats = [_nt(dout2, v_ref[tiles[s], :]) * kept(a_ref, jb, s).astype(F32) for s in range(n_sub)]
            prefixes = [_tri_sums(g, lower) for g in gmats]
            dzs = []
            for s in range(n_sub):
                signed = kept(e_ref, jb, s).astype(F32)
                e = jnp.abs(signed)
                inv = 1.0 / (1.0 + e)
                small = e * inv
                positive = signed > 0.0
                beta, one_minus = jnp.where(positive, inv, small), jnp.where(positive, small, inv)
                dz = (gmats[s] * one_minus - (gleft + prefixes[s]) * beta) * SB_SCALE
                gleft = gleft + jnp.sum(gmats[s], axis=-1, keepdims=True)
                if masked:
                    dz = jnp.where(diff < offset - s * SB_TILE, dz, 0.0)
                dzs.append(dz.astype(BF16))
            dz_all = jnp.concatenate(dzs, axis=1)
            dk_acc[jb] += _nn(q2_t, dz_all)
            dv_acc[jb] += _nn(dout2_t, kept(a_ref, jb))
            return gleft, dq + _nn(dz_all, k_ref[pl.ds(pl.multiple_of(jb * block_keys, block_keys), block_keys), :])

        carry = (jnp.zeros((2 * q_rows, 1), F32), jnp.zeros((2 * q_rows, LANES), F32))
        carry = lax.fori_loop(0, last, lambda jb, c: block(jb, c, False), carry)
        dq_ref[...] = _unstack_heads(block(last, carry, True)[1], first).astype(BF16)

        @pl.when(i == n_q - 1)
        def _():
            for jb in range(n_blocks):
                for s in range(n_sub):
                    rows = slice((jb * n_sub + s) * SB_TILE, (jb * n_sub + s + 1) * SB_TILE)
                    cols = slice(s * SB_TILE, (s + 1) * SB_TILE)
                    dk_ref[rows, :] = dk_acc[jb, :, cols].T.astype(BF16)
                    dv_ref[rows, :] = dv_acc[jb, :, cols].T.astype(BF16)

        finish()

    t = batch * seq
    blk = lambda c0: pl.BlockSpec((q_rows, LANES), lambda p, b, i: (b * n_q + i, c0 + p))
    kv = lambda c0: pl.BlockSpec((seq, LANES), lambda p, b, i: (b, c0 + p))
    row_spec = pl.BlockSpec((1, LANES), lambda p, b, i: (0, SB_PAIRS + p))
    kept_spec = pl.BlockSpec((1, 1, n_blocks, 2, q_rows, block_keys), lambda p, b, i: (p, b, 0, 0, i, 0))
    x_in, x_out, x_shape, x_scratch, x_arrays = _riding_specs(exchange)
    res = pl.pallas_call(
        body, name="sb_bwd", grid=(SB_PAIRS, batch, n_q),
        in_specs=[blk(q0), kv(k0), kv(v0), blk(0), kept_spec, kept_spec, blk(SB_PAIRS), row_spec] + x_in,
        out_specs=[blk(0), kv(0), kv(0), pl.BlockSpec((1, LANES), lambda p, b, i: (0, p))] + x_out,
        out_shape=[jax.ShapeDtypeStruct((t, GM_WIDTH), BF16)] * 3 + [jax.ShapeDtypeStruct((1, GM_WIDTH), F32)] + x_shape,
        scratch_shapes=[pltpu.VMEM((n_blocks, LANES, block_keys), F32), pltpu.VMEM((n_blocks, LANES, block_keys), F32)] + x_scratch,
        compiler_params=_params(3),
    )(proj, proj, proj, raw, weights, signed_e, dmerged, hg, *x_arrays)
    return res[0], res[1], res[2], res[3], res[4:]


def _x_softmax(qh, kh):
    s = _nt(qh, kh) * X_SCALE
    p = jnp.exp(s - jnp.max(s, axis=-1, keepdims=True))
    return p / jnp.sum(p, axis=-1, keepdims=True)


def _xattn_fwd(q, kv, *, batch, seq, n_mem, tq=512):
    tq = min(tq, seq)
    n_q = seq // tq
    d = X_HEADS * X_HEAD_DIM

    def body(q_ref, kv_ref, o_ref):
        for h in range(X_HEADS):
            cols = slice(h * X_HEAD_DIM, (h + 1) * X_HEAD_DIM)
            p = _x_softmax(q_ref[:, cols], kv_ref[:, cols])
            o_ref[:, cols] = _nn(p.astype(BF16), kv_ref[:, d + h * X_HEAD_DIM:d + (h + 1) * X_HEAD_DIM]).astype(BF16)

    return pl.pallas_call(
        body, name="xattn_fwd", grid=(batch, n_q),
        in_specs=[pl.BlockSpec((tq, d), lambda b, i: (b * n_q + i, 0)), pl.BlockSpec((n_mem, 2 * d), lambda b, i: (b, 0))],
        out_specs=pl.BlockSpec((tq, d), lambda b, i: (b * n_q + i, 0)),
        out_shape=jax.ShapeDtypeStruct((batch * seq, d), BF16),
        compiler_params=_params(2),
    )(q, kv)


def _xattn_bwd(q, kv, do, *, batch, seq, n_mem, tq=512):
    tq = min(tq, seq)
    n_q = seq // tq
    d = X_HEADS * X_HEAD_DIM

    def body(q_ref, kv_ref, do_ref, dq_ref, dkv_ref, acc):
        i = pl.program_id(1)

        @pl.when(i == 0)
        def _():
            acc[...] = jnp.zeros_like(acc)

        for h in range(X_HEADS):
            cols = slice(h * X_HEAD_DIM, (h + 1) * X_HEAD_DIM)
            vcols = slice(d + h * X_HEAD_DIM, d + (h + 1) * X_HEAD_DIM)
            qh, kh, vh, doh = q_ref[:, cols], kv_ref[:, cols], kv_ref[:, vcols], do_ref[:, cols]
            p = _x_softmax(qh, kh)
            dp = _nt(doh, vh)
            acc[:, vcols] += _tn(p.astype(BF16), doh)
            ds = (p * (dp - jnp.sum(dp * p, axis=-1, keepdims=True)) * X_SCALE).astype(BF16)
            dq_ref[:, cols] = _nn(ds, kh).astype(BF16)
            acc[:, cols] += _tn(ds, qh)

        @pl.when(i == n_q - 1)
        def _():
            dkv_ref[...] = acc[...].astype(BF16)

    return pl.pallas_call(
        body, name="xattn_bwd", grid=(batch, n_q),
        in_specs=[pl.BlockSpec((tq, d), lambda b, i: (b * n_q + i, 0)), pl.BlockSpec((n_mem, 2 * d), lambda b, i: (b, 0)),
                  pl.BlockSpec((tq, d), lambda b, i: (b * n_q + i, 0))],
        out_specs=[pl.BlockSpec((tq, d), lambda b, i: (b * n_q + i, 0)), pl.BlockSpec((n_mem, 2 * d), lambda b, i: (b, 0))],
        out_shape=[jax.ShapeDtypeStruct((batch * seq, d), BF16), jax.ShapeDtypeStruct((batch * n_mem, 2 * d), BF16)],
        scratch_shapes=[pltpu.VMEM((n_mem, 2 * d), F32)],
        compiler_params=_params(2),
    )(q, kv, do)


def _loss_head(h, target, g, *, tm=512):
    t, d = h.shape
    tm = min(tm, t)
    n_steps = t // tm

    def body(h_ref, t_ref, g_ref, dh_ref, dg_ref, loss_ref, sq_acc):
        step = pl.program_id(0)

        @pl.when(step == 0)
        def _():
            dg_ref[...] = jnp.zeros_like(dg_ref)
            sq_acc[...] = jnp.zeros_like(sq_acc)

        hv = h_ref[...]
        gv = g_ref[...]
        r = _rstd(hv)
        xh = hv * r
        err = xh * gv - t_ref[...]
        sq_acc[...] += jnp.sum(err * err, axis=0, keepdims=True)
        dy = err * (1.0 / d)
        dg_ref[...] += jnp.sum(dy * xh, axis=0, keepdims=True)
        dxh = dy * gv
        dh_ref[...] = r * (dxh - xh * jnp.mean(dxh * xh, axis=-1, keepdims=True))

        @pl.when(step == n_steps - 1)
        def _():
            total = jnp.sum(sq_acc[...], axis=-1, keepdims=True) * (0.5 / d)
            loss_ref[...] = jnp.broadcast_to(total, loss_ref.shape)

    return pl.pallas_call(
        body, name="loss_head", grid=(n_steps,),
        in_specs=[pl.BlockSpec((tm, d), lambda i: (i, 0)), pl.BlockSpec((tm, d), lambda i: (i, 0)), pl.BlockSpec((1, d), lambda i: (0, 0))],
        out_specs=[pl.BlockSpec((tm, d), lambda i: (i, 0)), pl.BlockSpec((1, d), lambda i: (0, 0)), pl.BlockSpec((1, LANES), lambda i: (0, 0))],
        out_shape=[jax.ShapeDtypeStruct((t, d), F32), jax.ShapeDtypeStruct((1, d), F32), jax.ShapeDtypeStruct((1, LANES), F32)],
        scratch_shapes=[pltpu.VMEM((1, d), F32)],
        compiler_params=_params(1),
    )(h, target, g)


def _my_index():
    return 4 * lax.axis_index("x") + 2 * lax.axis_index("y") + lax.axis_index("c")


def _peers():
    x, y, c = lax.axis_index("x"), lax.axis_index("y"), lax.axis_index("c")
    out = []
    for rel in range(1, N_DEV):
        dx, dy, dc = (rel >> 2) & 1, (rel >> 1) & 1, rel & 1
        px, py, pc = x ^ dx, y ^ dy, c ^ dc
        out.append(((px, py, pc), 4 * px + 2 * py + pc))
    return out


class _Exchange:
    def __init__(self, arrays, scatter):
        self.arrays, self.scatter, self.n = list(arrays), scatter, len(arrays)
        any_spec = pl.BlockSpec(memory_space=pl.ANY)
        self.in_specs = [any_spec] * self.n
        self.out_specs = [any_spec] * self.n
        self.out_shape = [jax.ShapeDtypeStruct((N_DEV,) + tuple(a.shape[-2:]), a.dtype) for a in self.arrays]
        n_peer = N_DEV - 1
        self.scratch = [pltpu.SemaphoreType.DMA((self.n, n_peer)), pltpu.SemaphoreType.DMA((self.n, n_peer)),
                        pltpu.SemaphoreType.DMA((self.n,))]

    def _copies(self, srcs, dsts, sems, arriving):
        send_sems, recv_sems, local_sems = sems
        me = _my_index()
        local, remote = [], []
        for w in range(self.n):
            if not arriving:
                local.append(pltpu.make_async_copy(srcs[w].at[me] if self.scatter else srcs[w], dsts[w].at[me], local_sems.at[w]))
            for rel, (pos, idx) in enumerate(_peers()):
                remote.append(pltpu.make_async_remote_copy(
                    src_ref=srcs[w].at[idx] if self.scatter else srcs[w], dst_ref=dsts[w].at[idx if arriving else me],
                    send_sem=send_sems.at[w, rel], recv_sem=recv_sems.at[w, rel], device_id=pos, device_id_type=MESH))
        return local, remote

    def start(self, srcs, dsts, sems):
        local, sends = self._copies(srcs, dsts, sems, arriving=False)
        for cp in local + sends:
            cp.start()

    def wait(self, srcs, dsts, sems):
        for cp in self._copies(srcs, dsts, sems, arriving=True)[1]:
            cp.wait_recv()
        local, sends = self._copies(srcs, dsts, sems, arriving=False)
        for cp in sends:
            cp.wait_send()
        for cp in local:
            cp.wait()

    def run(self, name):
        n = self.n

        def body(*refs):
            srcs, dsts, sems = refs[:n], refs[n:2 * n], refs[2 * n:]
            self.start(srcs, dsts, sems)
            self.wait(srcs, dsts, sems)

        return pl.pallas_call(body, name=name, in_specs=self.in_specs, out_specs=self.out_specs, out_shape=self.out_shape,
                              scratch_shapes=self.scratch)(*self.arrays)


def _all_reduce_small(part):
    rows = part.shape[0]
    n_peer = N_DEV - 1

    def body(p_ref, o_ref, buf, send_sems, recv_sems):
        me = _my_index()
        peers = _peers()
        buf[me] = p_ref[...]
        sends = [pltpu.make_async_remote_copy(src_ref=p_ref, dst_ref=buf.at[me], send_sem=send_sems.at[rel], recv_sem=recv_sems.at[rel],
                                              device_id=peers[rel][0], device_id_type=MESH) for rel in range(n_peer)]
        for cp in sends:
            cp.start()
        for rel in range(n_peer):
            pltpu.make_async_remote_copy(src_ref=p_ref, dst_ref=buf.at[peers[rel][1]], send_sem=send_sems.at[rel], recv_sem=recv_sems.at[rel],
                                         device_id=peers[rel][0], device_id_type=MESH).wait_recv()
        for cp in sends:
            cp.wait_send()
        total = buf[0]
        for dev in range(1, N_DEV):
            total = total + buf[dev]
        o_ref[...] = total

    vmem = pl.BlockSpec(memory_space=pltpu.VMEM)
    return pl.pallas_call(
        body, name="all_reduce_small", in_specs=[vmem], out_specs=vmem, out_shape=jax.ShapeDtypeStruct(part.shape, F32),
        scratch_shapes=[pltpu.VMEM((N_DEV, rows, LANES), F32), pltpu.SemaphoreType.DMA((n_peer,)), pltpu.SemaphoreType.DMA((n_peer,))],
        compiler_params=pltpu.CompilerParams(has_side_effects=True, vmem_limit_bytes=VMEM_LIMIT),
    )(part)


def _adamw_math(w, g, m, v):
    m_new = ADAM_B1 * m + (1.0 - ADAM_B1) * g
    v_new = ADAM_B2 * v + (1.0 - ADAM_B2) * (g * g)
    m_hat = m_new / (1.0 - ADAM_B1 ** ADAM_STEP)
    v_hat = v_new / (1.0 - ADAM_B2 ** ADAM_STEP)
    delta = -ADAM_LR * (m_hat / (jnp.sqrt(v_hat) + ADAM_EPS) + ADAM_WD * w)
    return delta, m_new, v_new


def _adamw(parts, w, m, v, *, name, tr=64):
    rows, cols = w.shape
    tr = min(tr, rows)
    assert rows % tr == 0
    stacked = parts.ndim == 3

    def body(p_ref, w_ref, m_ref, v_ref, g_ref, d_ref, mo_ref, vo_ref):
        if stacked:
            g = p_ref[0].astype(F32)
            for dev in range(1, N_DEV):
                g = g + p_ref[dev].astype(F32)
        else:
            g = p_ref[...]
        delta, m_new, v_new = _adamw_math(w_ref[...], g, m_ref[...], v_ref[...])
        g_ref[...] = g
        d_ref[...] = delta
        mo_ref[...] = m_new
        vo_ref[...] = v_new

    tile = pl.BlockSpec((tr, cols), lambda i: (i, 0))
    p_spec = pl.BlockSpec((N_DEV, tr, cols), lambda i: (0, i, 0)) if stacked else tile
    return pl.pallas_call(
        body, name=name, grid=(rows // tr,), in_specs=[p_spec, tile, tile, tile], out_specs=[tile] * 4,
        out_shape=[jax.ShapeDtypeStruct((rows, cols), F32)] * 4, compiler_params=_params(1),
    )(parts, w, m, v)


_LATER = ("w_out", "w_cq", "w_ckv", "w_co", "w_ff1", "w_ff2")


def _as_rows(stacked):
    return stacked.reshape(-1, stacked.shape[-1])


def _local_step(x, mem, target, small, shards):
    batch, seq, d = x.shape
    n_mem = mem.shape[1]
    t = batch * seq
    x2, mem2, tgt2 = x.reshape(t, d), mem.reshape(batch * n_mem, d), target.reshape(t, d)
    g_mix, g_cross, g_mem, g_ffn, g_final = (small[k] for k in ("norm_mix_g", "norm_cross_g", "norm_mem_g", "norm_ffn_g", "norm_final_g"))
    gv, hg, w_sp, b_sp_t = small["gm_v_norm_g"], small["head_norm_g"], small["w_spatial"], small["b_spatial_t"]

    win_t = _as_rows(_Exchange([shards["w_in"]], scatter=False).run("gather_w_in")[0])
    proj, xn = _norm_mm(x2, g_mix, win_t, mode="nt", name="proj_fwd", tm=512, tn=win_t.shape[0])
    merged_a = _gmlp_fwd(proj, w_sp, b_sp_t, gv, hg)
    merged, sb_raw, sb_weights, sb_signed_e, gathered = _sb_fwd(proj, merged_a, hg, batch=batch, seq=seq,
                                                                exchange=_Exchange([shards[n] for n in _LATER], scatter=False))
    wout, wcq, wckv_t, wco, wff1_t, wff2 = (_as_rows(g) for g in gathered)
    h1 = _mm(merged, wout, mode="nn", out_dtype=F32, name="mix_out_fwd", epi=_epi_residual, epi_ins=(x2,))
    qx, hn1 = _norm_mm(h1, g_cross, wcq, mode="nn", name="xq_fwd")
    kvx, memn = _norm_mm(mem2, g_mem, wckv_t, mode="nt", name="xkv_fwd", tm=512, tn=wckv_t.shape[0])
    o = _xattn_fwd(qx, kvx, batch=batch, seq=seq, n_mem=n_mem)
    h2 = _mm(o, wco, mode="nn", out_dtype=F32, name="xo_fwd", epi=_epi_residual, epi_ins=(h1,))
    fpre, hn2 = _norm_mm(h2, g_ffn, wff1_t, mode="nt", name="ff1_fwd", tm=512, tn=wff1_t.shape[0])
    h3 = _mm(fpre, wff2, mode="nn", out_dtype=F32, name="ff2_fwd", tm=512, tk=wff2.shape[0], a_fn=_relu2, epi=_epi_residual,
             epi_ins=(h2,))
    dh3, dg_final, loss_row = _loss_head(h3, tgt2, g_final)

    dpre = _mm(dh3, wff2, mode="nt", out_dtype=BF16, name="ff2_bwd_x", tm=512, tn=wff2.shape[0], epi=_epi_relu2_grad,
               epi_ins=(fpre,))
    d_wff2 = _mm(fpre, dh3, mode="tn", out_dtype=BF16, name="ff2_bwd_w", tm=2048, a_fn=_relu2)
    d_wff1_t = _mm(dpre, hn2, mode="tn", out_dtype=BF16, name="ff1_bwd_w", tm=2048)
    dh2, dg_ffn = _mm(dpre, wff1_t, mode="nn", out_dtype=F32, name="ff1_bwd_x", tm=512, tk=wff1_t.shape[0], epi=_epi_rms_bwd,
                      epi_ins=(h2, dh3), vec_ins=(g_ffn,), aux=True)
    do = _mm(dh2, wco, mode="nt", out_dtype=BF16, name="xo_bwd_x")
    d_wco = _mm(o, dh2, mode="tn", out_dtype=BF16, name="xo_bwd_w")
    dqx, dkvx = _xattn_bwd(qx, kvx, do, batch=batch, seq=seq, n_mem=n_mem)
    d_wcq = _mm(hn1, dqx, mode="tn", out_dtype=BF16, name="xq_bwd_w")
    dh1, dg_cross = _mm(dqx, wcq, mode="nt", out_dtype=F32, name="xq_bwd_x", tm=512, epi=_epi_rms_bwd,
                        epi_ins=(h1, dh2), vec_ins=(g_cross,), aux=True)
    d_wckv_t = _mm(dkvx, memn, mode="tn", out_dtype=BF16, name="xkv_bwd_w")
    _, dg_mem = _mm(dkvx, wckv_t, mode="nn", out_dtype=BF16, name="xkv_bwd_x", tm=512, epi=_epi_rms_gain_only,
                    epi_ins=(mem2,), vec_ins=(g_mem,), aux=True)
    dmerged = _mm(dh1, wout, mode="nt", out_dtype=BF16, name="mix_out_bwd_x")
    d_wout = _mm(merged, dh1, mode="tn", out_dtype=BF16, name="mix_out_bwd_w")
    dp_a, d_wsp, d_bsp_t, d_gv, d_hg_a = _gmlp_bwd(proj, dmerged, w_sp, b_sp_t, gv, hg)
    d_later = {"w_out": d_wout, "w_cq": d_wcq, "w_ckv": d_wckv_t, "w_co": d_wco, "w_ff1": d_wff1_t, "w_ff2": d_wff2}
    scatter = _Exchange([d_later[n].reshape(N_DEV, -1, d) for n in _LATER], scatter=True)
    dq, dk, dv, d_hg_b, received = _sb_bwd(proj, sb_raw, sb_weights, sb_signed_e, dmerged, hg, batch=batch, seq=seq, exchange=scatter)
    dproj = jnp.concatenate([dp_a, dq, dk, dv], axis=1)
    d_win_t = _mm(dproj, xn, mode="tn", out_dtype=BF16, name="proj_bwd_w", tm=512)
    dx, dg_mix, d_win_received = _mm(dproj, win_t, mode="nn", out_dtype=F32, name="proj_bwd_x", tm=512, tk=win_t.shape[0],
                                     epi=_epi_rms_bwd, epi_ins=(x2, dh1), vec_ins=(g_mix,), aux=True,
                                     exchange=_Exchange([d_win_t.reshape(N_DEV, -1, d)], scatter=True))

    d_small = {"norm_mix_g": dg_mix, "gm_v_norm_g": d_gv, "w_spatial": d_wsp, "b_spatial_t": d_bsp_t, "head_norm_g": jnp.concatenate([d_hg_a, d_hg_b], axis=1),
               "norm_cross_g": dg_cross, "norm_mem_g": dg_mem, "norm_ffn_g": dg_ffn, "norm_final_g": dg_final}
    d_big = dict(zip(_LATER, received))
    d_big["w_in"] = d_win_received
    return loss_row, dx.reshape(batch, seq, d), d_small, d_big


_BIG = ("w_in", "w_out", "w_cq", "w_ckv", "w_co", "w_ff1", "w_ff2")
_BIG_TRANSPOSED = ("w_in", "w_ckv", "w_ff1")
_SMALL = ("norm_mix_g", "gm_v_norm_g", "w_spatial", "b_spatial", "head_norm_g", "norm_cross_g", "norm_mem_g", "norm_ffn_g", "norm_final_g")
_NAMES = ("norm_mix_g", "w_in", "gm_v_norm_g", "w_spatial", "b_spatial", "head_norm_g", "w_out", "norm_cross_g", "norm_mem_g",
          "w_cq", "w_ckv", "w_co", "norm_ffn_g", "w_ff1", "w_ff2", "norm_final_g")


def _rows_of(a):
    r = a.reshape(-1, LANES)
    pad = (-r.shape[0]) % 8
    return jnp.pad(r, ((0, pad), (0, 0))) if pad else r


def _shard2d(name, a):
    a = a[0]
    return a.T if name in _BIG_TRANSPOSED else a


def kernel(x, mem, norm_mix_g, w_in, gm_v_norm_g, w_spatial, b_spatial, head_norm_g, w_out, norm_cross_g, norm_mem_g, w_cq, w_ckv, w_co, norm_ffn_g, w_ff1, w_ff2, norm_final_g, loss_target, m_norm_mix_g, m_w_in, m_gm_v_norm_g, m_w_spatial, m_b_spatial, m_head_norm_g, m_w_out, m_norm_cross_g, m_norm_mem_g, m_w_cq, m_w_ckv, m_w_co, m_norm_ffn_g, m_w_ff1, m_w_ff2, m_norm_final_g, v_norm_mix_g, v_w_in, v_gm_v_norm_g, v_w_spatial, v_b_spatial, v_head_norm_g, v_w_out, v_norm_cross_g, v_norm_mem_g, v_w_cq, v_w_ckv, v_w_co, v_norm_ffn_g, v_w_ff1, v_w_ff2, v_norm_final_g):
    weights = dict(norm_mix_g=norm_mix_g, w_in=w_in, gm_v_norm_g=gm_v_norm_g, w_spatial=w_spatial, b_spatial=b_spatial,
                   head_norm_g=head_norm_g, w_out=w_out, norm_cross_g=norm_cross_g, norm_mem_g=norm_mem_g, w_cq=w_cq, w_ckv=w_ckv,
                   w_co=w_co, norm_ffn_g=norm_ffn_g, w_ff1=w_ff1, w_ff2=w_ff2, norm_final_g=norm_final_g)
    mom1 = dict(norm_mix_g=m_norm_mix_g, w_in=m_w_in, gm_v_norm_g=m_gm_v_norm_g, w_spatial=m_w_spatial, b_spatial=m_b_spatial,
                head_norm_g=m_head_norm_g, w_out=m_w_out, norm_cross_g=m_norm_cross_g, norm_mem_g=m_norm_mem_g, w_cq=m_w_cq,
                w_ckv=m_w_ckv, w_co=m_w_co, norm_ffn_g=m_norm_ffn_g, w_ff1=m_w_ff1, w_ff2=m_w_ff2, norm_final_g=m_norm_final_g)
    mom2 = dict(norm_mix_g=v_norm_mix_g, w_in=v_w_in, gm_v_norm_g=v_gm_v_norm_g, w_spatial=v_w_spatial, b_spatial=v_b_spatial,
                head_norm_g=v_head_norm_g, w_out=v_w_out, norm_cross_g=v_norm_cross_g, norm_mem_g=v_norm_mem_g, w_cq=v_w_cq,
                w_ckv=v_w_ckv, w_co=v_w_co, norm_ffn_g=v_norm_ffn_g, w_ff1=v_w_ff1, w_ff2=v_w_ff2, norm_final_g=v_norm_final_g)

    shards = {n: _shard2d(n, weights[n]).astype(BF16) for n in _BIG}
    small = {n: weights[n].reshape(1, -1) for n in _SMALL if n not in ("w_spatial", "b_spatial")}
    small["w_spatial"] = w_spatial[0]
    small["b_spatial_t"] = b_spatial[0].T
    loss_row, grad_x, d_small, d_big = _local_step(x, mem, loss_target, small, shards)

    d_small["b_spatial"] = d_small.pop("b_spatial_t").T
    packed = jnp.concatenate([_rows_of(d_small[n]) for n in _SMALL] + [jnp.pad(loss_row, ((0, 7), (0, 0)))], axis=0)
    summed = _all_reduce_small(packed)

    grads, deltas, new_m, new_v = {}, {}, {}, {}
    for n in _BIG:
        outs = _adamw(d_big[n], _shard2d(n, weights[n]), _shard2d(n, mom1[n]), _shard2d(n, mom2[n]), name="adamw_" + n)
        outs = [o.T if n in _BIG_TRANSPOSED else o for o in outs]
        grads[n], deltas[n], new_m[n], new_v[n] = (o[None] for o in outs)
    pack = lambda src: jnp.concatenate([_rows_of(src[n]) for n in _SMALL], axis=0)
    n_small_rows = sum(_rows_of(weights[n]).shape[0] for n in _SMALL)
    outs = _adamw(summed[:n_small_rows], pack(weights), pack(mom1), pack(mom2), name="adamw_small", tr=n_small_rows)
    at = 0
    for n in _SMALL:
        used = weights[n].size // LANES
        for dst, o in zip((grads, deltas, new_m, new_v), outs):
            dst[n] = o[at:at + used].reshape(weights[n].shape)
        at += _rows_of(weights[n]).shape[0]
    loss = summed[n_small_rows, 0]
    return (loss, grad_x, *[grads[n] for n in _NAMES], *[deltas[n] for n in _NAMES], *[new_m[n] for n in _NAMES],
            *[new_v[n] for n in _NAMES])
```

```python
import functools
import math

import jax
import jax.numpy as jnp
from jax import lax
from jax.experimental import pallas as pl
from jax.experimental.pallas import tpu as pltpu

F32 = jnp.float32
BF16 = jnp.bfloat16
EPS = 1e-6
N_DEV = 8
LANES = 128
CHUNK = 128
GM_GROUPS = 4
GM_WIDTH = 512
SB_PAIRS = 4
SB_HEAD_DIM = 64
SB_SCALE = 0.125
SB_TILE = 128
SB_BLOCK = 512
SB_QUERIES_FWD = 256
SB_QUERIES_BWD = 256
X_HEADS = 4
X_HEAD_DIM = 256
X_SCALE = 1.0 / 16.0
VMEM_LIMIT = 56 * 1024 * 1024
ADAM_LR, ADAM_B1, ADAM_B2, ADAM_EPS, ADAM_WD, ADAM_STEP = 0.001, 0.9, 0.999, 1e-08, 0.01, 10
MESH = pl.DeviceIdType.MESH


def _params(n_axes):
    return pltpu.CompilerParams(dimension_semantics=("arbitrary",) * n_axes, vmem_limit_bytes=VMEM_LIMIT)


def _dot(a, b, dims):
    return lax.dot_general(a, b, (dims, ((), ())), preferred_element_type=F32)


def _nn(a, b):
    return _dot(a, b, ((1,), (0,)))


def _nt(a, b):
    return _dot(a, b, ((1,), (1,)))


def _tn(a, b):
    return _dot(a, b, ((0,), (0,)))


_MODES = {"nn": _nn, "nt": _nt, "tn": _tn}


def _rstd(x):
    return lax.rsqrt(jnp.mean(x * x, axis=-1, keepdims=True) + EPS)


def _gelu(x):
    c = math.sqrt(2.0 / math.pi)
    t = jnp.tanh(c * (x + 0.044715 * x * x * x))
    return 0.5 * x * (1.0 + t)


def _gelu_and_grad(x):
    c = math.sqrt(2.0 / math.pi)
    t = jnp.tanh(c * (x + 0.044715 * x * x * x))
    half = 0.5 * (1.0 + t)
    return x * half, half + 0.5 * x * (1.0 - t * t) * c * (1.0 + 3 * 0.044715 * x * x)


def _split_bf16(x):
    hi = x.astype(BF16)
    lo = (x - hi.astype(F32)).astype(BF16)
    return hi, lo


def _mm(a, b, *, mode, out_dtype, name, tm=1024, tn=1024, tk=1024, a_fn=None, epi=None, epi_ins=(), vec_ins=(), aux=False,
        exchange=None):
    if mode == "nn":
        (m, k), (k2, n) = a.shape, b.shape
    elif mode == "nt":
        (m, k), (n, k2) = a.shape, b.shape
    else:
        (k, m), (k2, n) = a.shape, b.shape
    assert k == k2, (a.shape, b.shape, mode)
    tm, tn, tk = min(tm, m), min(tn, n), min(tk, k)
    assert m % tm == 0 and n % tn == 0 and k % tk == 0, (m, n, k, tm, tn, tk)
    n_m, n_n, n_k = m // tm, n // tn, k // tk
    assert not aux or n_n == 1
    dot = _MODES[mode]
    n_epi, n_vec = len(epi_ins), len(vec_ins)

    def body(*refs):
        ins, outs, scratch, x_refs = _riding(exchange, refs, 2 + n_epi + n_vec, 2 if aux else 1, 1 if n_k > 1 else 0)
        a_ref, b_ref, epi_refs = ins[0], ins[1], ins[2:]
        o_ref = outs[0]
        aux_ref = outs[1] if aux else None
        acc_ref = scratch[0] if n_k > 1 else None
        i, j, kk = pl.program_id(0), pl.program_id(1), pl.program_id(2)
        ride_done = _ride(exchange, x_refs, (i == 0) & (j == 0) & (kk == 0), (i == n_m - 1) & (j == n_n - 1) & (kk == n_k - 1))
        av = a_ref[...]
        if a_fn is not None:
            av = a_fn(av)
        part = dot(av.astype(BF16), b_ref[...].astype(BF16))

        def finish(acc):
            if epi is None:
                o_ref[...] = acc.astype(out_dtype)
                return
            res = epi(acc, *[r[...] for r in epi_refs])
            if aux:
                res, rows = res[0], res[1:]
                rows = rows[0] if len(rows) == 1 else jnp.concatenate(rows, axis=0)

                @pl.when(i == 0)
                def _():
                    aux_ref[...] = rows

                @pl.when(i != 0)
                def _():
                    aux_ref[...] += rows
            o_ref[...] = res.astype(out_dtype)

        if n_k == 1:
            finish(part)
        else:
            @pl.when(kk == 0)
            def _():
                acc_ref[...] = part

            @pl.when(kk != 0)
            def _():
                acc_ref[...] += part

            @pl.when(kk == n_k - 1)
            def _():
                finish(acc_ref[...])

        ride_done()

    if mode == "tn":
        a_spec = pl.BlockSpec((tk, tm), lambda i, j, kk: (kk, i))
    else:
        a_spec = pl.BlockSpec((tm, tk), lambda i, j, kk: (i, kk))
    if mode == "nt":
        b_spec = pl.BlockSpec((tn, tk), lambda i, j, kk: (j, kk))
    else:
        b_spec = pl.BlockSpec((tk, tn), lambda i, j, kk: (kk, j))
    tile_spec = pl.BlockSpec((tm, tn), lambda i, j, kk: (i, j))
    row_spec = pl.BlockSpec((1, tn), lambda i, j, kk: (0, j))
    out_shape = [jax.ShapeDtypeStruct((m, n), out_dtype)]
    out_specs = [tile_spec]
    if aux:
        out_shape.append(jax.ShapeDtypeStruct((int(aux), n), F32))
        out_specs.append(pl.BlockSpec((int(aux), tn), lambda i, j, kk: (0, j)))
    x_in, x_out, x_shape, x_scratch, x_arrays = _riding_specs(exchange)
    res = pl.pallas_call(
        body, name=name, grid=(n_m, n_n, n_k),
        in_specs=[a_spec, b_spec] + [tile_spec] * n_epi + [row_spec] * n_vec + x_in,
        out_specs=out_specs + x_out, out_shape=out_shape + x_shape,
        scratch_shapes=([pltpu.VMEM((tm, tn), F32)] if n_k > 1 else []) + x_scratch,
        compiler_params=_params(3),
    )(a, b, *epi_ins, *vec_ins, *x_arrays)
    if exchange is not None:
        return tuple(res)
    return res if aux else res[0]


def _norm_mm(x, g, w, *, mode, name, tm=1024, tn=1024):
    m, d = x.shape
    n = w.shape[0] if mode == "nt" else w.shape[1]
    tm, tn = min(tm, m), min(tn, n)
    assert m % tm == 0 and n % tn == 0
    dot = _MODES[mode]

    def body(x_ref, g_ref, w_ref, o_ref, xn_ref, xn_s):
        @pl.when(pl.program_id(1) == 0)
        def _():
            xv = x_ref[...]
            xn = (xv * _rstd(xv) * g_ref[...]).astype(BF16)
            xn_s[...] = xn
            xn_ref[...] = xn

        o_ref[...] = dot(xn_s[...], w_ref[...]).astype(BF16)

    w_spec = pl.BlockSpec((tn, d), lambda i, j: (j, 0)) if mode == "nt" else pl.BlockSpec((d, tn), lambda i, j: (0, j))
    return pl.pallas_call(
        body, name=name, grid=(m // tm, n // tn),
        in_specs=[pl.BlockSpec((tm, d), lambda i, j: (i, 0)), pl.BlockSpec((1, d), lambda i, j: (0, 0)), w_spec],
        out_specs=[pl.BlockSpec((tm, tn), lambda i, j: (i, j)), pl.BlockSpec((tm, d), lambda i, j: (i, 0))],
        out_shape=[jax.ShapeDtypeStruct((m, n), BF16), jax.ShapeDtypeStruct((m, d), BF16)],
        scratch_shapes=[pltpu.VMEM((tm, d), BF16)],
        compiler_params=_params(2),
    )(x, g, w)


def _epi_residual(acc, res):
    return res + acc


def _epi_relu2_grad(acc, pre):
    return acc * (2.0 * jnp.maximum(pre.astype(F32), 0.0))


def _relu2(pre):
    r = jnp.maximum(pre.astype(F32), 0.0)
    return r * r


def _epi_rms_bwd(acc, h, dres, g):
    r = _rstd(h)
    xh = h * r
    dxh = acc * g
    dh = dres + r * (dxh - xh * jnp.mean(dxh * xh, axis=-1, keepdims=True))
    return dh, jnp.sum(acc * xh, axis=0, keepdims=True)


def _epi_loss(acc, h_in, target, g):
    h = h_in + acc
    r = _rstd(h)
    xh = h * r
    err = xh * g - target
    dy = err * (1.0 / h.shape[-1])
    dxh = dy * g
    dh = r * (dxh - xh * jnp.mean(dxh * xh, axis=-1, keepdims=True))
    return dh, jnp.sum(dy * xh, axis=0, keepdims=True), jnp.sum(err * err, axis=0, keepdims=True)


def _epi_rms_gain_only(acc, h, g):
    return acc, jnp.sum(acc * (h * _rstd(h)), axis=0, keepdims=True)


def _tril(n):
    row = lax.broadcasted_iota(jnp.int32, (n, n), 0)
    col = lax.broadcasted_iota(jnp.int32, (n, n), 1)
    return col <= row


def _gmlp_fwd(proj, w_sp, b_sp_t, gv, hg, *, rows=512):
    t = proj.shape[0]
    rows = min(rows, t)
    n_c = rows // CHUNK

    def body(u_ref, v_ref, w_ref, bt_ref, gv_ref, hg_ref, m_ref):
        keep = _tril(CHUNK)
        for g in range(GM_GROUPS):
            cols = slice(g * LANES, (g + 1) * LANES)
            wg = jnp.where(keep, w_ref[g], 0.0).astype(BF16)
            u = _gelu(u_ref[:, cols].astype(F32))
            v = _gelu(v_ref[:, cols].astype(F32))
            vn = (v * _rstd(v) * gv_ref[:, cols]).astype(BF16)
            bias = bt_ref[:, g:g + 1]
            for c in range(n_c):
                rs = slice(c * CHUNK, (c + 1) * CHUNK)
                mixed = _nn(wg, vn[rs]) + bias
                a = u[rs] * mixed
                m_ref[rs, cols] = (a * _rstd(a) * hg_ref[:, cols]).astype(BF16)

    full = lambda shape: pl.BlockSpec(shape, lambda i: (0,) * len(shape))
    return pl.pallas_call(
        body, name="gmlp_fwd", grid=(t // rows,),
        in_specs=[pl.BlockSpec((rows, GM_WIDTH), lambda i: (i, 0)), pl.BlockSpec((rows, GM_WIDTH), lambda i: (i, 1)),
                  full((GM_GROUPS, CHUNK, CHUNK)), full((CHUNK, GM_GROUPS)), full((1, GM_WIDTH)), full((1, GM_WIDTH))],
        out_specs=pl.BlockSpec((rows, GM_WIDTH), lambda i: (i, 0)),
        out_shape=jax.ShapeDtypeStruct((t, 2 * GM_WIDTH), BF16),
        compiler_params=_params(1),
    )(proj, proj, w_sp, b_sp_t, gv, hg)


def _gmlp_bwd(proj, dmerged, w_sp, b_sp_t, gv, hg, *, rows=512):
    t = proj.shape[0]
    rows = min(rows, t)
    n_c = rows // CHUNK
    n_steps = t // rows

    def body(u_ref, v_ref, dm_ref, w_ref, bt_ref, gv_ref, hg_ref, dp_ref, dw_ref, dbt_ref, dgv_ref, dhg_ref, db_acc):
        step = pl.program_id(0)
        keep = _tril(CHUNK)

        @pl.when(step == 0)
        def _():
            dw_ref[...] = jnp.zeros_like(dw_ref)
            db_acc[...] = jnp.zeros_like(db_acc)
            dgv_ref[...] = jnp.zeros_like(dgv_ref)
            dhg_ref[...] = jnp.zeros_like(dhg_ref)

        for g in range(GM_GROUPS):
            cols = slice(g * LANES, (g + 1) * LANES)
            wg = jnp.where(keep, w_ref[g], 0.0).astype(BF16)
            u, u_slope = _gelu_and_grad(u_ref[:, cols].astype(F32))
            v, v_slope = _gelu_and_grad(v_ref[:, cols].astype(F32))
            r = _rstd(v)
            xh = v * r
            gvg = gv_ref[:, cols]
            hgg = hg_ref[:, cols]
            vn = (xh * gvg).astype(BF16)
            bias = bt_ref[:, g:g + 1]
            dm = dm_ref[:, cols].astype(F32)
            du_parts, dvn_parts = [], []
            dw = jnp.zeros((CHUNK, CHUNK), F32)
            db = jnp.zeros((CHUNK, LANES), F32)
            dhg = jnp.zeros((1, LANES), F32)
            for c in range(n_c):
                rs = slice(c * CHUNK, (c + 1) * CHUNK)
                mixed = _nn(wg, vn[rs]) + bias
                a = u[rs] * mixed
                ra = _rstd(a)
                an = a * ra
                dhg = dhg + jnp.sum(dm[rs] * an, axis=0, keepdims=True)
                dan = dm[rs] * hgg
                da = ra * (dan - an * jnp.mean(dan * an, axis=-1, keepdims=True))
                du_parts.append(da * mixed)
                dmixed = da * u[rs]
                db = db + dmixed
                dmb = dmixed.astype(BF16)
                dw = dw + _nt(dmb, vn[rs])
                dvn_parts.append(_tn(wg, dmb))
            du = jnp.concatenate(du_parts, axis=0)
            dvn = jnp.concatenate(dvn_parts, axis=0)
            dw_ref[g] += dw
            db_acc[g] += db
            dhg_ref[:, cols] += dhg
            dgv_ref[:, cols] += jnp.sum(dvn * xh, axis=0, keepdims=True)
            dxh = dvn * gvg
            dv = r * (dxh - xh * jnp.mean(dxh * xh, axis=-1, keepdims=True))
            dp_ref[:, cols] = (du * u_slope).astype(BF16)
            dp_ref[:, GM_WIDTH + g * LANES:GM_WIDTH + (g + 1) * LANES] = (dv * v_slope).astype(BF16)

        @pl.when(step == n_steps - 1)
        def _():
            for g in range(GM_GROUPS):
                dw_ref[g] = jnp.where(keep, dw_ref[g], 0.0)
                dbt_ref[:, g:g + 1] = jnp.sum(db_acc[g], axis=-1, keepdims=True)

    full = lambda shape: pl.BlockSpec(shape, lambda i: (0,) * len(shape))
    return pl.pallas_call(
        body, name="gmlp_bwd", grid=(n_steps,),
        in_specs=[pl.BlockSpec((rows, GM_WIDTH), lambda i: (i, 0)), pl.BlockSpec((rows, GM_WIDTH), lambda i: (i, 1)),
                  pl.BlockSpec((rows, GM_WIDTH), lambda i: (i, 0)),
                  full((GM_GROUPS, CHUNK, CHUNK)), full((CHUNK, GM_GROUPS)), full((1, GM_WIDTH)), full((1, GM_WIDTH))],
        out_specs=[pl.BlockSpec((rows, 2 * GM_WIDTH), lambda i: (i, 0)), full((GM_GROUPS, CHUNK, CHUNK)),
                   full((CHUNK, GM_GROUPS)), full((1, GM_WIDTH)), full((1, GM_WIDTH))],
        out_shape=[jax.ShapeDtypeStruct((t, 2 * GM_WIDTH), BF16), jax.ShapeDtypeStruct((GM_GROUPS, CHUNK, CHUNK), F32),
                   jax.ShapeDtypeStruct((CHUNK, GM_GROUPS), F32), jax.ShapeDtypeStruct((1, GM_WIDTH), F32),
                   jax.ShapeDtypeStruct((1, GM_WIDTH), F32)],
        scratch_shapes=[pltpu.VMEM((GM_GROUPS, CHUNK, LANES), F32)],
        compiler_params=_params(1),
    )(proj, proj, dmerged, w_sp, b_sp_t, gv, hg)


def _sb_logits(z, strict):
    e = jnp.exp(-jnp.abs(z))
    ls = jnp.minimum(z, 0.0) - jnp.log(1.0 + e)
    l1m = ls - z
    if strict is not None:
        l1m = jnp.where(strict, l1m, 0.0)
    return ls, l1m, jnp.where(z >= 0.0, e, -e)


def _tri_sums(x, tri):
    hi, lo = _split_bf16(x)
    return _nn(jnp.concatenate([hi, lo], axis=1), jnp.concatenate([tri, tri], axis=0))


def _sb_weights(ls, in_tile, right, strict):
    a = jnp.exp(ls + in_tile + right)
    if strict is not None:
        a = jnp.where(strict, a, 0.0)
    return a


def _sb_masks(q_rows):
    row = lax.broadcasted_iota(jnp.int32, (SB_TILE, SB_TILE), 0)
    col = lax.broadcasted_iota(jnp.int32, (SB_TILE, SB_TILE), 1)
    lane = lax.broadcasted_iota(jnp.int32, (q_rows, LANES), 1)
    return row, col, lane < SB_HEAD_DIM


def _stack_heads(x, first):
    zero = jnp.zeros_like(x)
    return jnp.concatenate([jnp.where(first, x, zero), jnp.where(first, zero, x)], axis=0)


def _stack_heads_t(x_t):
    first_t = lax.broadcasted_iota(jnp.int32, x_t.shape, 0) < SB_HEAD_DIM
    zero = jnp.zeros_like(x_t)
    return jnp.concatenate([jnp.where(first_t, x_t, zero), jnp.where(first_t, zero, x_t)], axis=1).astype(BF16)


def _unstack_heads(x2, first):
    half = x2.shape[0] // 2
    return jnp.where(first, x2[:half], x2[half:])


def _stacked_col_minus_row(q_rows):
    row = lax.broadcasted_iota(jnp.int32, (2 * q_rows, SB_TILE), 0)
    col = lax.broadcasted_iota(jnp.int32, (2 * q_rows, SB_TILE), 1)
    return col - (row & (q_rows - 1))


def _head_mean(x, first):
    s0 = jnp.sum(jnp.where(first, x, 0.0), axis=-1, keepdims=True)
    s1 = jnp.sum(jnp.where(first, 0.0, x), axis=-1, keepdims=True)
    return jnp.where(first, s0, s1) * (1.0 / SB_HEAD_DIM)


def _riding(exchange, refs, n_in, n_out, n_scratch):
    n_x = exchange.n if exchange is not None else 0
    ins, rest = refs[:n_in], refs[n_in:]
    x_src, rest = rest[:n_x], rest[n_x:]
    outs, rest = rest[:n_out], rest[n_out:]
    x_dst, rest = rest[:n_x], rest[n_x:]
    return ins, outs, rest[:n_scratch], (x_src, x_dst, rest[n_scratch:])


def _riding_specs(exchange):
    if exchange is None:
        return [], [], [], [], []
    return exchange.in_specs, exchange.out_specs, exchange.out_shape, exchange.scratch, exchange.arrays


def _ride(exchange, x_refs, first_step, last_step):
    if exchange is None:
        return lambda: None

    @pl.when(first_step)
    def _():
        exchange.start(*x_refs)

    def finish():
        @pl.when(last_step)
        def _():
            exchange.wait(*x_refs)

    return finish


def _sb_fwd(proj, merged_a, hg, *, batch, seq, exchange=None):
    q0, k0, v0 = 2 * GM_WIDTH // LANES, 2 * GM_WIDTH // LANES + SB_PAIRS, 2 * GM_WIDTH // LANES + 2 * SB_PAIRS
    block_keys, q_rows = min(SB_BLOCK, seq), min(SB_QUERIES_FWD, seq)
    assert block_keys % q_rows == 0 and seq % block_keys == 0
    n_q, n_sub, n_blocks = seq // q_rows, block_keys // SB_TILE, seq // block_keys

    def body(*refs):
        (q_ref, k_ref, v_ref, hg_ref, _), (m_ref, raw_ref, a_ref, e_ref), _, x_refs = _riding(exchange, refs, 5, 4, 0)
        b, p, i = pl.program_id(0), pl.program_id(1), pl.program_id(2)
        finish = _ride(exchange, x_refs, (b == 0) & (p == 0) & (i == 0), (b == batch - 1) & (p == SB_PAIRS - 1) & (i == n_q - 1))
        row, col, first = _sb_masks(q_rows)
        upper = (row > col).astype(BF16)
        q2 = _stack_heads((q_ref[...].astype(F32) * SB_SCALE).astype(BF16), first)
        diff = _stacked_col_minus_row(q_rows)
        last = (i * q_rows) // block_keys
        offset = i * q_rows - last * block_keys

        def scores(jb, count=n_sub):
            return tuple(_nt(q2, k_ref[pl.ds(pl.multiple_of((jb * n_sub + s) * SB_TILE, SB_TILE), SB_TILE), :]) for s in range(count))

        def keep_for_backward(ref, jb, s, stacked):
            cols = slice(s * SB_TILE, (s + 1) * SB_TILE)
            ref[0, 0, jb, 0, :, cols] = stacked[:q_rows]
            ref[0, 0, jb, 1, :, cols] = stacked[q_rows:]

        def weights_of(jb, z, right, masked, count=n_sub):
            keeps = [diff < offset - s * SB_TILE if masked else None for s in range(count)]
            logits = [_sb_logits(z[s], keeps[s]) for s in range(count)]
            totals = [jnp.sum(l1m, axis=-1, keepdims=True) for _, l1m, _ in logits]
            sums = [_tri_sums(l1m, upper) for _, l1m, _ in logits]
            weights = [jnp.zeros((2 * q_rows, SB_TILE), BF16)] * n_sub
            for s in reversed(range(count)):
                weights[s] = _sb_weights(logits[s][0], sums[s], right, keeps[s]).astype(BF16)
                right = right + totals[s]
                keep_for_backward(a_ref, jb, s, weights[s])
                keep_for_backward(e_ref, jb, s, logits[s][2].astype(BF16))
            return right, jnp.concatenate(weights, axis=1)

        def values(jb):
            return v_ref[pl.ds(pl.multiple_of(jb * block_keys, block_keys), block_keys), :]

        def diagonal_block(count):
            return lambda: weights_of(last, scores(last, count), jnp.zeros((2 * q_rows, 1), F32), True, count)

        right, a_prev = lax.switch(offset // q_rows, [diagonal_block((c + 1) * q_rows // SB_TILE) for c in range(block_keys // q_rows)])
        z_next = scores(jnp.maximum(last - 1, 0))

        def step(k, carry):
            right, acc, z, a_prev = carry
            jb = last - k
            acc = acc + _nn(a_prev, values(jb + 1))
            z_next = scores(jnp.maximum(jb - 1, 0))
            right, a = weights_of(jb, z, right, False)
            return right, acc, z_next, a

        _, acc2, _, a_prev = lax.fori_loop(1, last + 1, step, (right, jnp.zeros((2 * q_rows, LANES), F32), z_next, a_prev))
        acc = _unstack_heads(acc2 + _nn(a_prev, values(0)), first)
        raw_ref[...] = acc
        m_ref[...] = (acc * lax.rsqrt(_head_mean(acc * acc, first) + EPS) * hg_ref[...]).astype(BF16)
        finish()

    t = batch * seq
    blk = lambda c0: pl.BlockSpec((q_rows, LANES), lambda b, p, i: (b * n_q + i, c0 + p))
    kv = lambda c0: pl.BlockSpec((seq, LANES), lambda b, p, i: (b, c0 + p))
    kept = pl.BlockSpec((1, 1, n_blocks, 2, q_rows, block_keys), lambda b, p, i: (p, b, 0, 0, i, 0))
    kept_shape = jax.ShapeDtypeStruct((SB_PAIRS, batch, n_blocks, 2, seq, block_keys), BF16)
    x_in, x_out, x_shape, x_scratch, x_arrays = _riding_specs(exchange)
    res = pl.pallas_call(
        body, name="sb_fwd", grid=(batch, SB_PAIRS, n_q),
        in_specs=[blk(q0), kv(k0), kv(v0), pl.BlockSpec((1, LANES), lambda b, p, i: (0, SB_PAIRS + p)),
                  pl.BlockSpec(memory_space=pl.ANY)] + x_in,
        out_specs=[blk(SB_PAIRS), blk(0), kept, kept] + x_out,
        out_shape=[jax.ShapeDtypeStruct((t, 2 * GM_WIDTH), BF16), jax.ShapeDtypeStruct((t, GM_WIDTH), F32), kept_shape, kept_shape] + x_shape,
        scratch_shapes=x_scratch,
        input_output_aliases={4: 0},
        compiler_params=_params(3),
    )(proj, proj, proj, hg, merged_a, *x_arrays)
    return res[0], res[1], res[2], res[3], res[4:]


def _sb_bwd(proj, raw, weights, signed_e, dmerged, hg, *, batch, seq, exchange=None):
    q0, k0, v0 = 2 * GM_WIDTH // LANES, 2 * GM_WIDTH // LANES + SB_PAIRS, 2 * GM_WIDTH // LANES + 2 * SB_PAIRS
    block_keys, q_rows = min(SB_BLOCK, seq), min(SB_QUERIES_BWD, seq)
    assert block_keys % q_rows == 0 and seq % block_keys == 0
    n_q, n_sub, n_blocks = seq // q_rows, block_keys // SB_TILE, seq // block_keys

    def body(*refs):
        ins, outs, (dk_acc, dv_acc), x_refs = _riding(exchange, refs, 8, 4, 2)
        q_ref, k_ref, v_ref, raw_ref, a_ref, e_ref, dm_ref, hg_ref = ins
        dq_ref, dk_ref, dv_ref, dhg_ref = outs
        p, b, i = pl.program_id(0), pl.program_id(1), pl.program_id(2)
        finish = _ride(exchange, x_refs, (b == 0) & (p == 0) & (i == 0), (b == batch - 1) & (p == SB_PAIRS - 1) & (i == n_q - 1))
        row, col, first = _sb_masks(q_rows)
        lower = (row < col).astype(BF16)

        @pl.when(jnp.logical_and(b == 0, i == 0))
        def _():
            dhg_ref[...] = jnp.zeros_like(dhg_ref)

        @pl.when(i == 0)
        def _():
            dk_acc[...] = jnp.zeros_like(dk_acc)
            dv_acc[...] = jnp.zeros_like(dv_acc)

        raw_v = raw_ref[...]
        dm = dm_ref[...].astype(F32)
        r = lax.rsqrt(_head_mean(raw_v * raw_v, first) + EPS)
        nrm = raw_v * r
        dhg_ref[...] += jnp.sum(dm * nrm, axis=0, keepdims=True)
        dn = dm * hg_ref[...]
        dout = r * (dn - nrm * _head_mean(dn * nrm, first))
        dout2 = _stack_heads(dout.astype(BF16), first)
        q2_t = _stack_heads_t(q_ref[...].astype(F32).T)
        dout2_t = _stack_heads_t(dout.T)
        diff = _stacked_col_minus_row(q_rows)
        last = (i * q_rows) // block_keys
        offset = i * q_rows - last * block_keys

        def kept(ref, jb, cols):
            return jnp.concatenate([ref[0, 0, jb, 0, :, cols], ref[0, 0, jb, 1, :, cols]], axis=0)

        def block(jb, carry, masked, count=n_sub):
            gleft, dq = carry
            tiles = [pl.ds(pl.multiple_of((jb * n_sub + s) * SB_TILE, SB_TILE), SB_TILE) for s in range(count)]
            cols = [slice(s * SB_TILE, (s + 1) * SB_TILE) for s in range(count)]
            gmats = [_nt(dout2, v_ref[tiles[s], :]) * kept(a_ref, jb, cols[s]).astype(F32) for s in range(count)]
            prefixes = [_tri_sums(g, lower) for g in gmats]
            dzs = []
            for s in range(count):
                signed = kept(e_ref, jb, cols[s]).astype(F32)
                e = jnp.abs(signed)
                inv = 1.0 / (1.0 + e)
                small = e * inv
                positive = signed > 0.0
                beta, one_minus = jnp.where(positive, inv, small), jnp.where(positive, small, inv)
                dz = (gmats[s] * one_minus - (gleft + prefixes[s]) * beta) * SB_SCALE
                gleft = gleft + jnp.sum(gmats[s], axis=-1, keepdims=True)
                if masked:
                    dz = jnp.where(diff < offset - s * SB_TILE, dz, 0.0)
                dzs.append(dz.astype(BF16))
            dz_all = jnp.concatenate(dzs, axis=1)
            width = count * SB_TILE
            dk_acc[jb, :, :width] += _nn(q2_t, dz_all)
            dv_acc[jb, :, :width] += _nn(dout2_t, kept(a_ref, jb, slice(0, width)))
            return gleft, dq + _nn(dz_all, k_ref[pl.ds(pl.multiple_of(jb * block_keys, block_keys), width), :])

        carry = (jnp.zeros((2 * q_rows, 1), F32), jnp.zeros((2 * q_rows, LANES), F32))
        carry = lax.fori_loop(0, last, lambda jb, c: block(jb, c, False), carry)
        diagonal = [functools.partial(block, last, carry, True, (c + 1) * q_rows // SB_TILE) for c in range(block_keys // q_rows)]
        dq_ref[...] = _unstack_heads(lax.switch(offset // q_rows, diagonal)[1], first).astype(BF16)

        @pl.when(i == n_q - 1)
        def _():
            for jb in range(n_blocks):
                for s in range(n_sub):
                    rows = slice((jb * n_sub + s) * SB_TILE, (jb * n_sub + s + 1) * SB_TILE)
                    cols = slice(s * SB_TILE, (s + 1) * SB_TILE)
                    dk_ref[rows, :] = dk_acc[jb, :, cols].T.astype(BF16)
                    dv_ref[rows, :] = dv_acc[jb, :, cols].T.astype(BF16)

        finish()

    t = batch * seq
    blk = lambda c0: pl.BlockSpec((q_rows, LANES), lambda p, b, i: (b * n_q + i, c0 + p))
    kv = lambda c0: pl.BlockSpec((seq, LANES), lambda p, b, i: (b, c0 + p))
    row_spec = pl.BlockSpec((1, LANES), lambda p, b, i: (0, SB_PAIRS + p))
    kept_spec = pl.BlockSpec((1, 1, n_blocks, 2, q_rows, block_keys), lambda p, b, i: (p, b, 0, 0, i, 0))
    x_in, x_out, x_shape, x_scratch, x_arrays = _riding_specs(exchange)
    res = pl.pallas_call(
        body, name="sb_bwd", grid=(SB_PAIRS, batch, n_q),
        in_specs=[blk(q0), kv(k0), kv(v0), blk(0), kept_spec, kept_spec, blk(SB_PAIRS), row_spec] + x_in,
        out_specs=[blk(0), kv(0), kv(0), pl.BlockSpec((1, LANES), lambda p, b, i: (0, p))] + x_out,
        out_shape=[jax.ShapeDtypeStruct((t, GM_WIDTH), BF16)] * 3 + [jax.ShapeDtypeStruct((1, GM_WIDTH), F32)] + x_shape,
        scratch_shapes=[pltpu.VMEM((n_blocks, LANES, block_keys), F32), pltpu.VMEM((n_blocks, LANES, block_keys), F32)] + x_scratch,
        compiler_params=_params(3),
    )(proj, proj, proj, raw, weights, signed_e, dmerged, hg, *x_arrays)
    return res[0], res[1], res[2], res[3], res[4:]


def _x_softmax(qh, kh):
    s = _nt(qh, kh) * X_SCALE
    p = jnp.exp(s - jnp.max(s, axis=-1, keepdims=True))
    return p / jnp.sum(p, axis=-1, keepdims=True)


def _xattn_fwd(q, kv, *, batch, seq, n_mem, tq=512):
    tq = min(tq, seq)
    n_q = seq // tq
    d = X_HEADS * X_HEAD_DIM

    def body(q_ref, kv_ref, o_ref):
        for h in range(X_HEADS):
            cols = slice(h * X_HEAD_DIM, (h + 1) * X_HEAD_DIM)
            p = _x_softmax(q_ref[:, cols], kv_ref[:, cols])
            o_ref[:, cols] = _nn(p.astype(BF16), kv_ref[:, d + h * X_HEAD_DIM:d + (h + 1) * X_HEAD_DIM]).astype(BF16)

    return pl.pallas_call(
        body, name="xattn_fwd", grid=(batch, n_q),
        in_specs=[pl.BlockSpec((tq, d), lambda b, i: (b * n_q + i, 0)), pl.BlockSpec((n_mem, 2 * d), lambda b, i: (b, 0))],
        out_specs=pl.BlockSpec((tq, d), lambda b, i: (b * n_q + i, 0)),
        out_shape=jax.ShapeDtypeStruct((batch * seq, d), BF16),
        compiler_params=_params(2),
    )(q, kv)


def _xattn_bwd(q, kv, do, *, batch, seq, n_mem, tq=512):
    tq = min(tq, seq)
    n_q = seq // tq
    d = X_HEADS * X_HEAD_DIM

    def body(q_ref, kv_ref, do_ref, dq_ref, dkv_ref, acc):
        i = pl.program_id(1)

        @pl.when(i == 0)
        def _():
            acc[...] = jnp.zeros_like(acc)

        for h in range(X_HEADS):
            cols = slice(h * X_HEAD_DIM, (h + 1) * X_HEAD_DIM)
            vcols = slice(d + h * X_HEAD_DIM, d + (h + 1) * X_HEAD_DIM)
            qh, kh, vh, doh = q_ref[:, cols], kv_ref[:, cols], kv_ref[:, vcols], do_ref[:, cols]
            p = _x_softmax(qh, kh)
            dp = _nt(doh, vh)
            acc[:, vcols] += _tn(p.astype(BF16), doh)
            ds = (p * (dp - jnp.sum(dp * p, axis=-1, keepdims=True)) * X_SCALE).astype(BF16)
            dq_ref[:, cols] = _nn(ds, kh).astype(BF16)
            acc[:, cols] += _tn(ds, qh)

        @pl.when(i == n_q - 1)
        def _():
            dkv_ref[...] = acc[...].astype(BF16)

    return pl.pallas_call(
        body, name="xattn_bwd", grid=(batch, n_q),
        in_specs=[pl.BlockSpec((tq, d), lambda b, i: (b * n_q + i, 0)), pl.BlockSpec((n_mem, 2 * d), lambda b, i: (b, 0)),
                  pl.BlockSpec((tq, d), lambda b, i: (b * n_q + i, 0))],
        out_specs=[pl.BlockSpec((tq, d), lambda b, i: (b * n_q + i, 0)), pl.BlockSpec((n_mem, 2 * d), lambda b, i: (b, 0))],
        out_shape=[jax.ShapeDtypeStruct((batch * seq, d), BF16), jax.ShapeDtypeStruct((batch * n_mem, 2 * d), BF16)],
        scratch_shapes=[pltpu.VMEM((n_mem, 2 * d), F32)],
        compiler_params=_params(2),
    )(q, kv, do)


def _my_index():
    return 4 * lax.axis_index("x") + 2 * lax.axis_index("y") + lax.axis_index("c")


def _peers():
    x, y, c = lax.axis_index("x"), lax.axis_index("y"), lax.axis_index("c")
    out = []
    for rel in range(1, N_DEV):
        dx, dy, dc = (rel >> 2) & 1, (rel >> 1) & 1, rel & 1
        px, py, pc = x ^ dx, y ^ dy, c ^ dc
        out.append(((px, py, pc), 4 * px + 2 * py + pc))
    return out


class _Exchange:
    def __init__(self, arrays, scatter):
        self.arrays, self.scatter, self.n = list(arrays), scatter, len(arrays)
        any_spec = pl.BlockSpec(memory_space=pl.ANY)
        self.in_specs = [any_spec] * self.n
        self.out_specs = [any_spec] * self.n
        self.out_shape = [jax.ShapeDtypeStruct((N_DEV,) + tuple(a.shape[-2:]), a.dtype) for a in self.arrays]
        n_peer = N_DEV - 1
        self.scratch = [pltpu.SemaphoreType.DMA((self.n, n_peer)), pltpu.SemaphoreType.DMA((self.n, n_peer)),
                        pltpu.SemaphoreType.DMA((self.n,))]

    def _copies(self, srcs, dsts, sems, arriving):
        send_sems, recv_sems, local_sems = sems
        me = _my_index()
        local, remote = [], []
        for w in range(self.n):
            if not arriving:
                local.append(pltpu.make_async_copy(srcs[w].at[me] if self.scatter else srcs[w], dsts[w].at[me], local_sems.at[w]))
            for rel, (pos, idx) in enumerate(_peers()):
                remote.append(pltpu.make_async_remote_copy(
                    src_ref=srcs[w].at[idx] if self.scatter else srcs[w], dst_ref=dsts[w].at[idx if arriving else me],
                    send_sem=send_sems.at[w, rel], recv_sem=recv_sems.at[w, rel], device_id=pos, device_id_type=MESH))
        return local, remote

    def start(self, srcs, dsts, sems):
        local, sends = self._copies(srcs, dsts, sems, arriving=False)
        for cp in local + sends:
            cp.start()

    def wait(self, srcs, dsts, sems):
        for cp in self._copies(srcs, dsts, sems, arriving=True)[1]:
            cp.wait_recv()
        local, sends = self._copies(srcs, dsts, sems, arriving=False)
        for cp in sends:
            cp.wait_send()
        for cp in local:
            cp.wait()

    def run(self, name):
        n = self.n

        def body(*refs):
            srcs, dsts, sems = refs[:n], refs[n:2 * n], refs[2 * n:]
            self.start(srcs, dsts, sems)
            self.wait(srcs, dsts, sems)

        return pl.pallas_call(body, name=name, in_specs=self.in_specs, out_specs=self.out_specs, out_shape=self.out_shape,
                              scratch_shapes=self.scratch)(*self.arrays)


def _all_reduce_small(part, *, loss_rows, loss_scale):
    rows = part.shape[0]
    n_peer = N_DEV - 1

    def body(p_ref, o_ref, buf, send_sems, recv_sems):
        me = _my_index()
        peers = _peers()
        buf[me] = p_ref[...]
        sends = [pltpu.make_async_remote_copy(src_ref=p_ref, dst_ref=buf.at[me], send_sem=send_sems.at[rel], recv_sem=recv_sems.at[rel],
                                              device_id=peers[rel][0], device_id_type=MESH) for rel in range(n_peer)]
        for cp in sends:
            cp.start()
        for rel in range(n_peer):
            pltpu.make_async_remote_copy(src_ref=p_ref, dst_ref=buf.at[peers[rel][1]], send_sem=send_sems.at[rel], recv_sem=recv_sems.at[rel],
                                         device_id=peers[rel][0], device_id_type=MESH).wait_recv()
        for cp in sends:
            cp.wait_send()
        total = buf[0]
        for dev in range(1, N_DEV):
            total = total + buf[dev]
        o_ref[...] = total
        squares = total[rows - loss_rows:]
        loss = jnp.sum(jnp.sum(squares, axis=0, keepdims=True), axis=-1, keepdims=True) * loss_scale
        o_ref[rows - loss_rows:, :] = jnp.broadcast_to(loss, (loss_rows, LANES))

    vmem = pl.BlockSpec(memory_space=pltpu.VMEM)
    return pl.pallas_call(
        body, name="all_reduce_small", in_specs=[vmem], out_specs=vmem, out_shape=jax.ShapeDtypeStruct(part.shape, F32),
        scratch_shapes=[pltpu.VMEM((N_DEV, rows, LANES), F32), pltpu.SemaphoreType.DMA((n_peer,)), pltpu.SemaphoreType.DMA((n_peer,))],
        compiler_params=pltpu.CompilerParams(has_side_effects=True, vmem_limit_bytes=VMEM_LIMIT),
    )(part)


def _adamw_math(w, g, m, v):
    m_new = ADAM_B1 * m + (1.0 - ADAM_B1) * g
    v_new = ADAM_B2 * v + (1.0 - ADAM_B2) * (g * g)
    m_hat = m_new / (1.0 - ADAM_B1 ** ADAM_STEP)
    v_hat = v_new / (1.0 - ADAM_B2 ** ADAM_STEP)
    delta = -ADAM_LR * (m_hat / (jnp.sqrt(v_hat) + ADAM_EPS) + ADAM_WD * w)
    return delta, m_new, v_new


def _adamw(parts, w, m, v, *, name, tr=64):
    rows, cols = w.shape
    tr = min(tr, rows)
    assert rows % tr == 0
    stacked = parts.ndim == 3

    def body(p_ref, w_ref, m_ref, v_ref, g_ref, d_ref, mo_ref, vo_ref):
        if stacked:
            g = p_ref[0].astype(F32)
            for dev in range(1, N_DEV):
                g = g + p_ref[dev].astype(F32)
        else:
            g = p_ref[...]
        delta, m_new, v_new = _adamw_math(w_ref[...], g, m_ref[...], v_ref[...])
        g_ref[...] = g
        d_ref[...] = delta
        mo_ref[...] = m_new
        vo_ref[...] = v_new

    tile = pl.BlockSpec((tr, cols), lambda i: (i, 0))
    p_spec = pl.BlockSpec((N_DEV, tr, cols), lambda i: (0, i, 0)) if stacked else tile
    return pl.pallas_call(
        body, name=name, grid=(rows // tr,), in_specs=[p_spec, tile, tile, tile], out_specs=[tile] * 4,
        out_shape=[jax.ShapeDtypeStruct((rows, cols), F32)] * 4, compiler_params=_params(1),
    )(parts, w, m, v)


_LATER = ("w_out", "w_cq", "w_ckv", "w_co", "w_ff1", "w_ff2")


def _as_rows(stacked):
    return stacked.reshape(-1, stacked.shape[-1])


def _local_step(x, mem, target, small, shards):
    batch, seq, d = x.shape
    n_mem = mem.shape[1]
    t = batch * seq
    x2, mem2, tgt2 = x.reshape(t, d), mem.reshape(batch * n_mem, d), target.reshape(t, d)
    g_mix, g_cross, g_mem, g_ffn, g_final = (small[k] for k in ("norm_mix_g", "norm_cross_g", "norm_mem_g", "norm_ffn_g", "norm_final_g"))
    gv, hg, w_sp, b_sp_t = small["gm_v_norm_g"], small["head_norm_g"], small["w_spatial"], small["b_spatial_t"]

    win_t = _as_rows(_Exchange([shards["w_in"]], scatter=False).run("gather_w_in")[0])
    proj, xn = _norm_mm(x2, g_mix, win_t, mode="nt", name="proj_fwd", tm=512, tn=win_t.shape[0])
    merged_a = _gmlp_fwd(proj, w_sp, b_sp_t, gv, hg)
    merged, sb_raw, sb_weights, sb_signed_e, gathered = _sb_fwd(proj, merged_a, hg, batch=batch, seq=seq,
                                                                exchange=_Exchange([shards[n] for n in _LATER], scatter=False))
    wout, wcq, wckv_t, wco, wff1_t, wff2 = (_as_rows(g) for g in gathered)
    h1 = _mm(merged, wout, mode="nn", out_dtype=F32, name="mix_out_fwd", epi=_epi_residual, epi_ins=(x2,))
    qx, hn1 = _norm_mm(h1, g_cross, wcq, mode="nn", name="xq_fwd")
    kvx, memn = _norm_mm(mem2, g_mem, wckv_t, mode="nt", name="xkv_fwd", tm=512, tn=wckv_t.shape[0])
    o = _xattn_fwd(qx, kvx, batch=batch, seq=seq, n_mem=n_mem)
    h2 = _mm(o, wco, mode="nn", out_dtype=F32, name="xo_fwd", epi=_epi_residual, epi_ins=(h1,))
    fpre, hn2 = _norm_mm(h2, g_ffn, wff1_t, mode="nt", name="ff1_fwd", tm=512, tn=wff1_t.shape[0])
    dh3, final_rows = _mm(fpre, wff2, mode="nn", out_dtype=F32, name="ff2_fwd_loss", tm=512, tk=wff2.shape[0], a_fn=_relu2,
                          epi=_epi_loss, epi_ins=(h2, tgt2), vec_ins=(g_final,), aux=2)
    dg_final, sq_err = final_rows[0:1], final_rows[1:2]

    dpre = _mm(dh3, wff2, mode="nt", out_dtype=BF16, name="ff2_bwd_x", tm=512, tn=wff2.shape[0], epi=_epi_relu2_grad,
               epi_ins=(fpre,))
    d_wff2 = _mm(fpre, dh3, mode="tn", out_dtype=BF16, name="ff2_bwd_w", tm=2048, a_fn=_relu2)
    d_wff1_t = _mm(dpre, hn2, mode="tn", out_dtype=BF16, name="ff1_bwd_w", tm=2048)
    dh2, dg_ffn = _mm(dpre, wff1_t, mode="nn", out_dtype=F32, name="ff1_bwd_x", tm=512, tk=wff1_t.shape[0], epi=_epi_rms_bwd,
                      epi_ins=(h2, dh3), vec_ins=(g_ffn,), aux=True)
    do = _mm(dh2, wco, mode="nt", out_dtype=BF16, name="xo_bwd_x")
    d_wco = _mm(o, dh2, mode="tn", out_dtype=BF16, name="xo_bwd_w")
    dqx, dkvx = _xattn_bwd(qx, kvx, do, batch=batch, seq=seq, n_mem=n_mem)
    d_wcq = _mm(hn1, dqx, mode="tn", out_dtype=BF16, name="xq_bwd_w")
    dh1, dg_cross = _mm(dqx, wcq, mode="nt", out_dtype=F32, name="xq_bwd_x", tm=512, epi=_epi_rms_bwd,
                        epi_ins=(h1, dh2), vec_ins=(g_cross,), aux=True)
    d_wckv_t = _mm(dkvx, memn, mode="tn", out_dtype=BF16, name="xkv_bwd_w")
    _, dg_mem = _mm(dkvx, wckv_t, mode="nn", out_dtype=BF16, name="xkv_bwd_x", tm=512, epi=_epi_rms_gain_only,
                    epi_ins=(mem2,), vec_ins=(g_mem,), aux=True)
    dmerged = _mm(dh1, wout, mode="nt", out_dtype=BF16, name="mix_out_bwd_x")
    d_wout = _mm(merged, dh1, mode="tn", out_dtype=BF16, name="mix_out_bwd_w")
    dp_a, d_wsp, d_bsp_t, d_gv, d_hg_a = _gmlp_bwd(proj, dmerged, w_sp, b_sp_t, gv, hg)
    d_later = {"w_out": d_wout, "w_cq": d_wcq, "w_ckv": d_wckv_t, "w_co": d_wco, "w_ff1": d_wff1_t, "w_ff2": d_wff2}
    scatter = _Exchange([d_later[n].reshape(N_DEV, -1, d) for n in _LATER], scatter=True)
    dq, dk, dv, d_hg_b, received = _sb_bwd(proj, sb_raw, sb_weights, sb_signed_e, dmerged, hg, batch=batch, seq=seq, exchange=scatter)
    dproj = jnp.concatenate([dp_a, dq, dk, dv], axis=1)
    d_win_t = _mm(dproj, xn, mode="tn", out_dtype=BF16, name="proj_bwd_w", tm=512)
    dx, dg_mix, d_win_received = _mm(dproj, win_t, mode="nn", out_dtype=F32, name="proj_bwd_x", tm=512, tk=win_t.shape[0],
                                     epi=_epi_rms_bwd, epi_ins=(x2, dh1), vec_ins=(g_mix,), aux=True,
                                     exchange=_Exchange([d_win_t.reshape(N_DEV, -1, d)], scatter=True))

    d_small = {"norm_mix_g": dg_mix, "gm_v_norm_g": d_gv, "w_spatial": d_wsp, "b_spatial_t": d_bsp_t, "head_norm_g": jnp.concatenate([d_hg_a, d_hg_b], axis=1),
               "norm_cross_g": dg_cross, "norm_mem_g": dg_mem, "norm_ffn_g": dg_ffn, "norm_final_g": dg_final}
    d_big = dict(zip(_LATER, received))
    d_big["w_in"] = d_win_received
    return sq_err, dx.reshape(batch, seq, d), d_small, d_big


_BIG = ("w_in", "w_out", "w_cq", "w_ckv", "w_co", "w_ff1", "w_ff2")
_BIG_TRANSPOSED = ("w_in", "w_ckv", "w_ff1")
_SMALL = ("norm_mix_g", "gm_v_norm_g", "w_spatial", "b_spatial", "head_norm_g", "norm_cross_g", "norm_mem_g", "norm_ffn_g", "norm_final_g")
_NAMES = ("norm_mix_g", "w_in", "gm_v_norm_g", "w_spatial", "b_spatial", "head_norm_g", "w_out", "norm_cross_g", "norm_mem_g",
          "w_cq", "w_ckv", "w_co", "norm_ffn_g", "w_ff1", "w_ff2", "norm_final_g")


def _rows_of(a):
    r = a.reshape(-1, LANES)
    pad = (-r.shape[0]) % 8
    return jnp.pad(r, ((0, pad), (0, 0))) if pad else r


def _shard2d(name, a):
    a = a[0]
    return a.T if name in _BIG_TRANSPOSED else a


def kernel(x, mem, norm_mix_g, w_in, gm_v_norm_g, w_spatial, b_spatial, head_norm_g, w_out, norm_cross_g, norm_mem_g, w_cq, w_ckv, w_co, norm_ffn_g, w_ff1, w_ff2, norm_final_g, loss_target, m_norm_mix_g, m_w_in, m_gm_v_norm_g, m_w_spatial, m_b_spatial, m_head_norm_g, m_w_out, m_norm_cross_g, m_norm_mem_g, m_w_cq, m_w_ckv, m_w_co, m_norm_ffn_g, m_w_ff1, m_w_ff2, m_norm_final_g, v_norm_mix_g, v_w_in, v_gm_v_norm_g, v_w_spatial, v_b_spatial, v_head_norm_g, v_w_out, v_norm_cross_g, v_norm_mem_g, v_w_cq, v_w_ckv, v_w_co, v_norm_ffn_g, v_w_ff1, v_w_ff2, v_norm_final_g):
    weights = dict(norm_mix_g=norm_mix_g, w_in=w_in, gm_v_norm_g=gm_v_norm_g, w_spatial=w_spatial, b_spatial=b_spatial,
                   head_norm_g=head_norm_g, w_out=w_out, norm_cross_g=norm_cross_g, norm_mem_g=norm_mem_g, w_cq=w_cq, w_ckv=w_ckv,
                   w_co=w_co, norm_ffn_g=norm_ffn_g, w_ff1=w_ff1, w_ff2=w_ff2, norm_final_g=norm_final_g)
    mom1 = dict(norm_mix_g=m_norm_mix_g, w_in=m_w_in, gm_v_norm_g=m_gm_v_norm_g, w_spatial=m_w_spatial, b_spatial=m_b_spatial,
                head_norm_g=m_head_norm_g, w_out=m_w_out, norm_cross_g=m_norm_cross_g, norm_mem_g=m_norm_mem_g, w_cq=m_w_cq,
                w_ckv=m_w_ckv, w_co=m_w_co, norm_ffn_g=m_norm_ffn_g, w_ff1=m_w_ff1, w_ff2=m_w_ff2, norm_final_g=m_norm_final_g)
    mom2 = dict(norm_mix_g=v_norm_mix_g, w_in=v_w_in, gm_v_norm_g=v_gm_v_norm_g, w_spatial=v_w_spatial, b_spatial=v_b_spatial,
                head_norm_g=v_head_norm_g, w_out=v_w_out, norm_cross_g=v_norm_cross_g, norm_mem_g=v_norm_mem_g, w_cq=v_w_cq,
                w_ckv=v_w_ckv, w_co=v_w_co, norm_ffn_g=v_norm_ffn_g, w_ff1=v_w_ff1, w_ff2=v_w_ff2, norm_final_g=v_norm_final_g)

    shards = {n: _shard2d(n, weights[n]).astype(BF16) for n in _BIG}
    small = {n: weights[n].reshape(1, -1) for n in _SMALL if n not in ("w_spatial", "b_spatial")}
    small["w_spatial"] = w_spatial[0]
    small["b_spatial_t"] = b_spatial[0].T
    sq_err, grad_x, d_small, d_big = _local_step(x, mem, loss_target, small, shards)

    d_small["b_spatial"] = d_small.pop("b_spatial_t").T
    sq_rows = _rows_of(sq_err)
    packed = jnp.concatenate([_rows_of(d_small[n]) for n in _SMALL] + [sq_rows], axis=0)
    summed = _all_reduce_small(packed, loss_rows=sq_rows.shape[0], loss_scale=0.5 / x.shape[-1])

    grads, deltas, new_m, new_v = {}, {}, {}, {}
    for n in _BIG:
        outs = _adamw(d_big[n], _shard2d(n, weights[n]), _shard2d(n, mom1[n]), _shard2d(n, mom2[n]), name="adamw_" + n)
        outs = [o.T if n in _BIG_TRANSPOSED else o for o in outs]
        grads[n], deltas[n], new_m[n], new_v[n] = (o[None] for o in outs)
    pack = lambda src: jnp.concatenate([_rows_of(src[n]) for n in _SMALL], axis=0)
    n_small_rows = sum(_rows_of(weights[n]).shape[0] for n in _SMALL)
    outs = _adamw(summed[:n_small_rows], pack(weights), pack(mom1), pack(mom2), name="adamw_small", tr=n_small_rows)
    at = 0
    for n in _SMALL:
        used = weights[n].size // LANES
        for dst, o in zip((grads, deltas, new_m, new_v), outs):
            dst[n] = o[at:at + used].reshape(weights[n].shape)
        at += _rows_of(weights[n]).shape[0]
    loss = summed[n_small_rows, 0]
    return (loss, grad_x, *[grads[n] for n in _NAMES], *[deltas[n] for n in _NAMES], *[new_m[n] for n in _NAMES],
            *[new_v[n] for n in _NAMES])
```

```python
import functools
import math

import jax
import jax.numpy as jnp
from jax import lax
from jax.experimental import pallas as pl
from jax.experimental.pallas import tpu as pltpu

F32 = jnp.float32
BF16 = jnp.bfloat16
EPS = 1e-6
N_DEV = 8
LANES = 128
CHUNK = 128
GM_GROUPS = 4
GM_WIDTH = 512
SB_PAIRS = 4
SB_HEAD_DIM = 64
SB_SCALE = 0.125
SB_TILE = 128
SB_BLOCK = 512
SB_QUERIES_FWD = 512
SB_QUERIES_BWD = 512
X_HEADS = 4
X_HEAD_DIM = 256
X_SCALE = 1.0 / 16.0
VMEM_LIMIT = 56 * 1024 * 1024
ADAM_LR, ADAM_B1, ADAM_B2, ADAM_EPS, ADAM_WD, ADAM_STEP = 0.001, 0.9, 0.999, 1e-08, 0.01, 10
MESH = pl.DeviceIdType.MESH


def _params(n_axes):
    return pltpu.CompilerParams(dimension_semantics=("arbitrary",) * n_axes, vmem_limit_bytes=VMEM_LIMIT)


def _dot(a, b, dims):
    return lax.dot_general(a, b, (dims, ((), ())), preferred_element_type=F32)


def _nn(a, b):
    return _dot(a, b, ((1,), (0,)))


def _nt(a, b):
    return _dot(a, b, ((1,), (1,)))


def _tn(a, b):
    return _dot(a, b, ((0,), (0,)))


_MODES = {"nn": _nn, "nt": _nt, "tn": _tn}


def _rstd(x):
    return lax.rsqrt(jnp.mean(x * x, axis=-1, keepdims=True) + EPS)


def _gelu(x):
    c = math.sqrt(2.0 / math.pi)
    t = jnp.tanh(c * (x + 0.044715 * x * x * x))
    return 0.5 * x * (1.0 + t)


def _gelu_and_grad(x):
    c = math.sqrt(2.0 / math.pi)
    t = jnp.tanh(c * (x + 0.044715 * x * x * x))
    half = 0.5 * (1.0 + t)
    return x * half, half + 0.5 * x * (1.0 - t * t) * c * (1.0 + 3 * 0.044715 * x * x)


def _split_bf16(x):
    hi = x.astype(BF16)
    lo = (x - hi.astype(F32)).astype(BF16)
    return hi, lo


def _mm(a, b, *, mode, out_dtype, name, tm=1024, tn=1024, tk=1024, a_fn=None, epi=None, epi_ins=(), vec_ins=(), aux=False,
        exchange=None):
    if mode == "nn":
        (m, k), (k2, n) = a.shape, b.shape
    elif mode == "nt":
        (m, k), (n, k2) = a.shape, b.shape
    else:
        (k, m), (k2, n) = a.shape, b.shape
    assert k == k2, (a.shape, b.shape, mode)
    tm, tn, tk = min(tm, m), min(tn, n), min(tk, k)
    assert m % tm == 0 and n % tn == 0 and k % tk == 0, (m, n, k, tm, tn, tk)
    n_m, n_n, n_k = m // tm, n // tn, k // tk
    assert not aux or n_n == 1
    dot = _MODES[mode]
    n_epi, n_vec = len(epi_ins), len(vec_ins)

    def body(*refs):
        ins, outs, scratch, x_refs = _riding(exchange, refs, 2 + n_epi + n_vec, 2 if aux else 1, 1 if n_k > 1 else 0)
        a_ref, b_ref, epi_refs = ins[0], ins[1], ins[2:]
        o_ref = outs[0]
        aux_ref = outs[1] if aux else None
        acc_ref = scratch[0] if n_k > 1 else None
        i, j, kk = pl.program_id(0), pl.program_id(1), pl.program_id(2)
        ride_done = _ride(exchange, x_refs, (i == 0) & (j == 0) & (kk == 0), (i == n_m - 1) & (j == n_n - 1) & (kk == n_k - 1))
        av = a_ref[...]
        if a_fn is not None:
            av = a_fn(av)
        part = dot(av.astype(BF16), b_ref[...].astype(BF16))

        def finish(acc):
            if epi is None:
                o_ref[...] = acc.astype(out_dtype)
                return
            res = epi(acc, *[r[...] for r in epi_refs])
            if aux:
                res, rows = res[0], res[1:]
                rows = rows[0] if len(rows) == 1 else jnp.concatenate(rows, axis=0)

                @pl.when(i == 0)
                def _():
                    aux_ref[...] = rows

                @pl.when(i != 0)
                def _():
                    aux_ref[...] += rows
            o_ref[...] = res.astype(out_dtype)

        if n_k == 1:
            finish(part)
        else:
            @pl.when(kk == 0)
            def _():
                acc_ref[...] = part

            @pl.when(kk != 0)
            def _():
                acc_ref[...] += part

            @pl.when(kk == n_k - 1)
            def _():
                finish(acc_ref[...])

        ride_done()

    if mode == "tn":
        a_spec = pl.BlockSpec((tk, tm), lambda i, j, kk: (kk, i))
    else:
        a_spec = pl.BlockSpec((tm, tk), lambda i, j, kk: (i, kk))
    if mode == "nt":
        b_spec = pl.BlockSpec((tn, tk), lambda i, j, kk: (j, kk))
    else:
        b_spec = pl.BlockSpec((tk, tn), lambda i, j, kk: (kk, j))
    tile_spec = pl.BlockSpec((tm, tn), lambda i, j, kk: (i, j))
    row_spec = pl.BlockSpec((1, tn), lambda i, j, kk: (0, j))
    out_shape = [jax.ShapeDtypeStruct((m, n), out_dtype)]
    out_specs = [tile_spec]
    if aux:
        out_shape.append(jax.ShapeDtypeStruct((int(aux), n), F32))
        out_specs.append(pl.BlockSpec((int(aux), tn), lambda i, j, kk: (0, j)))
    x_in, x_out, x_shape, x_scratch, x_arrays = _riding_specs(exchange)
    res = pl.pallas_call(
        body, name=name, grid=(n_m, n_n, n_k),
        in_specs=[a_spec, b_spec] + [tile_spec] * n_epi + [row_spec] * n_vec + x_in,
        out_specs=out_specs + x_out, out_shape=out_shape + x_shape,
        scratch_shapes=([pltpu.VMEM((tm, tn), F32)] if n_k > 1 else []) + x_scratch,
        compiler_params=_params(3),
    )(a, b, *epi_ins, *vec_ins, *x_arrays)
    if exchange is not None:
        return tuple(res)
    return res if aux else res[0]


def _norm_mm(x, g, w, *, mode, name, tm=1024, tn=1024):
    m, d = x.shape
    n = w.shape[0] if mode == "nt" else w.shape[1]
    tm, tn = min(tm, m), min(tn, n)
    assert m % tm == 0 and n % tn == 0
    dot = _MODES[mode]

    def body(x_ref, g_ref, w_ref, o_ref, xn_ref, xn_s):
        @pl.when(pl.program_id(1) == 0)
        def _():
            xv = x_ref[...]
            xn = (xv * _rstd(xv) * g_ref[...]).astype(BF16)
            xn_s[...] = xn
            xn_ref[...] = xn

        o_ref[...] = dot(xn_s[...], w_ref[...]).astype(BF16)

    w_spec = pl.BlockSpec((tn, d), lambda i, j: (j, 0)) if mode == "nt" else pl.BlockSpec((d, tn), lambda i, j: (0, j))
    return pl.pallas_call(
        body, name=name, grid=(m // tm, n // tn),
        in_specs=[pl.BlockSpec((tm, d), lambda i, j: (i, 0)), pl.BlockSpec((1, d), lambda i, j: (0, 0)), w_spec],
        out_specs=[pl.BlockSpec((tm, tn), lambda i, j: (i, j)), pl.BlockSpec((tm, d), lambda i, j: (i, 0))],
        out_shape=[jax.ShapeDtypeStruct((m, n), BF16), jax.ShapeDtypeStruct((m, d), BF16)],
        scratch_shapes=[pltpu.VMEM((tm, d), BF16)],
        compiler_params=_params(2),
    )(x, g, w)


def _epi_residual(acc, res):
    return res + acc


def _epi_relu2_grad(acc, pre):
    return acc * (2.0 * jnp.maximum(pre.astype(F32), 0.0))


def _relu2(pre):
    r = jnp.maximum(pre.astype(F32), 0.0)
    return r * r


def _epi_rms_bwd(acc, h, dres, g):
    r = _rstd(h)
    xh = h * r
    dxh = acc * g
    dh = dres + r * (dxh - xh * jnp.mean(dxh * xh, axis=-1, keepdims=True))
    return dh, jnp.sum(acc * xh, axis=0, keepdims=True)


def _epi_loss(acc, h_in, target, g):
    h = h_in + acc
    r = _rstd(h)
    xh = h * r
    err = xh * g - target
    dy = err * (1.0 / h.shape[-1])
    dxh = dy * g
    dh = r * (dxh - xh * jnp.mean(dxh * xh, axis=-1, keepdims=True))
    return dh, jnp.sum(dy * xh, axis=0, keepdims=True), jnp.sum(err * err, axis=0, keepdims=True)


def _epi_rms_gain_only(acc, h, g):
    return acc, jnp.sum(acc * (h * _rstd(h)), axis=0, keepdims=True)


def _tril(n):
    row = lax.broadcasted_iota(jnp.int32, (n, n), 0)
    col = lax.broadcasted_iota(jnp.int32, (n, n), 1)
    return col <= row


def _gmlp_fwd(proj, w_sp, b_sp_t, gv, hg, *, rows=512):
    t = proj.shape[0]
    rows = min(rows, t)
    n_c = rows // CHUNK

    def body(u_ref, v_ref, w_ref, bt_ref, gv_ref, hg_ref, m_ref):
        keep = _tril(CHUNK)
        for g in range(GM_GROUPS):
            cols = slice(g * LANES, (g + 1) * LANES)
            wg = jnp.where(keep, w_ref[g], 0.0).astype(BF16)
            u = _gelu(u_ref[:, cols].astype(F32))
            v = _gelu(v_ref[:, cols].astype(F32))
            vn = (v * _rstd(v) * gv_ref[:, cols]).astype(BF16)
            bias = bt_ref[:, g:g + 1]
            for c in range(n_c):
                rs = slice(c * CHUNK, (c + 1) * CHUNK)
                mixed = _nn(wg, vn[rs]) + bias
                a = u[rs] * mixed
                m_ref[rs, cols] = (a * _rstd(a) * hg_ref[:, cols]).astype(BF16)

    full = lambda shape: pl.BlockSpec(shape, lambda i: (0,) * len(shape))
    return pl.pallas_call(
        body, name="gmlp_fwd", grid=(t // rows,),
        in_specs=[pl.BlockSpec((rows, GM_WIDTH), lambda i: (i, 0)), pl.BlockSpec((rows, GM_WIDTH), lambda i: (i, 1)),
                  full((GM_GROUPS, CHUNK, CHUNK)), full((CHUNK, GM_GROUPS)), full((1, GM_WIDTH)), full((1, GM_WIDTH))],
        out_specs=pl.BlockSpec((rows, GM_WIDTH), lambda i: (i, 0)),
        out_shape=jax.ShapeDtypeStruct((t, 2 * GM_WIDTH), BF16),
        compiler_params=_params(1),
    )(proj, proj, w_sp, b_sp_t, gv, hg)


def _gmlp_bwd(proj, dmerged, w_sp, b_sp_t, gv, hg, *, rows=512):
    t = proj.shape[0]
    rows = min(rows, t)
    n_c = rows // CHUNK
    n_steps = t // rows

    def body(u_ref, v_ref, dm_ref, w_ref, bt_ref, gv_ref, hg_ref, dp_ref, dw_ref, dbt_ref, dgv_ref, dhg_ref, db_acc):
        step = pl.program_id(0)
        keep = _tril(CHUNK)

        @pl.when(step == 0)
        def _():
            dw_ref[...] = jnp.zeros_like(dw_ref)
            db_acc[...] = jnp.zeros_like(db_acc)
            dgv_ref[...] = jnp.zeros_like(dgv_ref)
            dhg_ref[...] = jnp.zeros_like(dhg_ref)

        for g in range(GM_GROUPS):
            cols = slice(g * LANES, (g + 1) * LANES)
            wg = jnp.where(keep, w_ref[g], 0.0).astype(BF16)
            u, u_slope = _gelu_and_grad(u_ref[:, cols].astype(F32))
            v, v_slope = _gelu_and_grad(v_ref[:, cols].astype(F32))
            r = _rstd(v)
            xh = v * r
            gvg = gv_ref[:, cols]
            hgg = hg_ref[:, cols]
            vn = (xh * gvg).astype(BF16)
            bias = bt_ref[:, g:g + 1]
            dm = dm_ref[:, cols].astype(F32)
            du_parts, dvn_parts = [], []
            dw = jnp.zeros((CHUNK, CHUNK), F32)
            db = jnp.zeros((CHUNK, LANES), F32)
            dhg = jnp.zeros((1, LANES), F32)
            for c in range(n_c):
                rs = slice(c * CHUNK, (c + 1) * CHUNK)
                mixed = _nn(wg, vn[rs]) + bias
                a = u[rs] * mixed
                ra = _rstd(a)
                an = a * ra
                dhg = dhg + jnp.sum(dm[rs] * an, axis=0, keepdims=True)
                dan = dm[rs] * hgg
                da = ra * (dan - an * jnp.mean(dan * an, axis=-1, keepdims=True))
                du_parts.append(da * mixed)
                dmixed = da * u[rs]
                db = db + dmixed
                dmb = dmixed.astype(BF16)
                dw = dw + _nt(dmb, vn[rs])
                dvn_parts.append(_tn(wg, dmb))
            du = jnp.concatenate(du_parts, axis=0)
            dvn = jnp.concatenate(dvn_parts, axis=0)
            dw_ref[g] += dw
            db_acc[g] += db
            dhg_ref[:, cols] += dhg
            dgv_ref[:, cols] += jnp.sum(dvn * xh, axis=0, keepdims=True)
            dxh = dvn * gvg
            dv = r * (dxh - xh * jnp.mean(dxh * xh, axis=-1, keepdims=True))
            dp_ref[:, cols] = (du * u_slope).astype(BF16)
            dp_ref[:, GM_WIDTH + g * LANES:GM_WIDTH + (g + 1) * LANES] = (dv * v_slope).astype(BF16)

        @pl.when(step == n_steps - 1)
        def _():
            for g in range(GM_GROUPS):
                dw_ref[g] = jnp.where(keep, dw_ref[g], 0.0)
                dbt_ref[:, g:g + 1] = jnp.sum(db_acc[g], axis=-1, keepdims=True)

    full = lambda shape: pl.BlockSpec(shape, lambda i: (0,) * len(shape))
    return pl.pallas_call(
        body, name="gmlp_bwd", grid=(n_steps,),
        in_specs=[pl.BlockSpec((rows, GM_WIDTH), lambda i: (i, 0)), pl.BlockSpec((rows, GM_WIDTH), lambda i: (i, 1)),
                  pl.BlockSpec((rows, GM_WIDTH), lambda i: (i, 0)),
                  full((GM_GROUPS, CHUNK, CHUNK)), full((CHUNK, GM_GROUPS)), full((1, GM_WIDTH)), full((1, GM_WIDTH))],
        out_specs=[pl.BlockSpec((rows, 2 * GM_WIDTH), lambda i: (i, 0)), full((GM_GROUPS, CHUNK, CHUNK)),
                   full((CHUNK, GM_GROUPS)), full((1, GM_WIDTH)), full((1, GM_WIDTH))],
        out_shape=[jax.ShapeDtypeStruct((t, 2 * GM_WIDTH), BF16), jax.ShapeDtypeStruct((GM_GROUPS, CHUNK, CHUNK), F32),
                   jax.ShapeDtypeStruct((CHUNK, GM_GROUPS), F32), jax.ShapeDtypeStruct((1, GM_WIDTH), F32),
                   jax.ShapeDtypeStruct((1, GM_WIDTH), F32)],
        scratch_shapes=[pltpu.VMEM((GM_GROUPS, CHUNK, LANES), F32)],
        compiler_params=_params(1),
    )(proj, proj, dmerged, w_sp, b_sp_t, gv, hg)


def _sb_logits(z, strict):
    e = jnp.exp(-jnp.abs(z))
    ls = jnp.minimum(z, 0.0) - jnp.log(1.0 + e)
    l1m = ls - z
    if strict is not None:
        l1m = jnp.where(strict, l1m, 0.0)
    return ls, l1m, jnp.where(z >= 0.0, e, -e)


def _tri_sums(x, tri):
    hi, lo = _split_bf16(x)
    return _nn(jnp.concatenate([hi, lo], axis=1), jnp.concatenate([tri, tri], axis=0))


def _sb_weights(ls, in_tile, right, strict):
    a = jnp.exp(ls + in_tile + right)
    if strict is not None:
        a = jnp.where(strict, a, 0.0)
    return a


def _sb_masks(q_rows):
    row = lax.broadcasted_iota(jnp.int32, (SB_TILE, SB_TILE), 0)
    col = lax.broadcasted_iota(jnp.int32, (SB_TILE, SB_TILE), 1)
    lane = lax.broadcasted_iota(jnp.int32, (q_rows, LANES), 1)
    return row, col, lane < SB_HEAD_DIM


def _stack_heads(x, first):
    zero = jnp.zeros_like(x)
    return jnp.concatenate([jnp.where(first, x, zero), jnp.where(first, zero, x)], axis=0)


def _stack_heads_t(x_t):
    first_t = lax.broadcasted_iota(jnp.int32, x_t.shape, 0) < SB_HEAD_DIM
    zero = jnp.zeros_like(x_t)
    return jnp.concatenate([jnp.where(first_t, x_t, zero), jnp.where(first_t, zero, x_t)], axis=1).astype(BF16)


def _unstack_heads(x2, first):
    half = x2.shape[0] // 2
    return jnp.where(first, x2[:half], x2[half:])


def _stacked_col_minus_row(q_rows):
    row = lax.broadcasted_iota(jnp.int32, (2 * q_rows, SB_TILE), 0)
    col = lax.broadcasted_iota(jnp.int32, (2 * q_rows, SB_TILE), 1)
    return col - (row & (q_rows - 1))


def _head_mean(x, first):
    s0 = jnp.sum(jnp.where(first, x, 0.0), axis=-1, keepdims=True)
    s1 = jnp.sum(jnp.where(first, 0.0, x), axis=-1, keepdims=True)
    return jnp.where(first, s0, s1) * (1.0 / SB_HEAD_DIM)


def _riding(exchange, refs, n_in, n_out, n_scratch):
    n_x = exchange.n if exchange is not None else 0
    ins, rest = refs[:n_in], refs[n_in:]
    x_src, rest = rest[:n_x], rest[n_x:]
    outs, rest = rest[:n_out], rest[n_out:]
    x_dst, rest = rest[:n_x], rest[n_x:]
    return ins, outs, rest[:n_scratch], (x_src, x_dst, rest[n_scratch:])


def _riding_specs(exchange):
    if exchange is None:
        return [], [], [], [], []
    return exchange.in_specs, exchange.out_specs, exchange.out_shape, exchange.scratch, exchange.arrays


def _ride(exchange, x_refs, first_step, last_step):
    if exchange is None:
        return lambda: None

    @pl.when(first_step)
    def _():
        exchange.start(*x_refs)

    def finish():
        @pl.when(last_step)
        def _():
            exchange.wait(*x_refs)

    return finish


def _sb_fwd(proj, merged_a, hg, *, batch, seq, exchange=None):
    q0, k0, v0 = 2 * GM_WIDTH // LANES, 2 * GM_WIDTH // LANES + SB_PAIRS, 2 * GM_WIDTH // LANES + 2 * SB_PAIRS
    block_keys, q_rows = min(SB_BLOCK, seq), min(SB_QUERIES_FWD, seq)
    assert block_keys % q_rows == 0 and seq % block_keys == 0
    n_q, n_sub, n_blocks = seq // q_rows, block_keys // SB_TILE, seq // block_keys

    def body(*refs):
        (q_ref, k_ref, v_ref, hg_ref, _), (m_ref, raw_ref, a_ref, e_ref), _, x_refs = _riding(exchange, refs, 5, 4, 0)
        b, p, i = pl.program_id(0), pl.program_id(1), pl.program_id(2)
        finish = _ride(exchange, x_refs, (b == 0) & (p == 0) & (i == 0), (b == batch - 1) & (p == SB_PAIRS - 1) & (i == n_q - 1))
        row, col, first = _sb_masks(q_rows)
        upper = (row > col).astype(BF16)
        q2 = _stack_heads((q_ref[...].astype(F32) * SB_SCALE).astype(BF16), first)
        diff = _stacked_col_minus_row(q_rows)
        last = (i * q_rows) // block_keys
        offset = i * q_rows - last * block_keys

        def scores(jb):
            return tuple(_nt(q2, k_ref[pl.ds(pl.multiple_of((jb * n_sub + s) * SB_TILE, SB_TILE), SB_TILE), :]) for s in range(n_sub))

        def keep_for_backward(ref, jb, s, stacked):
            cols = slice(s * SB_TILE, (s + 1) * SB_TILE)
            ref[0, 0, jb, 0, :, cols] = stacked[:q_rows]
            ref[0, 0, jb, 1, :, cols] = stacked[q_rows:]

        def weights_of(jb, z, right, masked):
            keeps = [diff < offset - s * SB_TILE if masked else None for s in range(n_sub)]
            logits = [_sb_logits(z[s], keeps[s]) for s in range(n_sub)]
            totals = [jnp.sum(l1m, axis=-1, keepdims=True) for _, l1m, _ in logits]
            sums = [_tri_sums(l1m, upper) for _, l1m, _ in logits]
            weights = [None] * n_sub
            for s in reversed(range(n_sub)):
                weights[s] = _sb_weights(logits[s][0], sums[s], right, keeps[s]).astype(BF16)
                right = right + totals[s]
                keep_for_backward(a_ref, jb, s, weights[s])
                keep_for_backward(e_ref, jb, s, logits[s][2].astype(BF16))
            return right, jnp.concatenate(weights, axis=1)

        def values(jb):
            return v_ref[pl.ds(pl.multiple_of(jb * block_keys, block_keys), block_keys), :]

        right, a_prev = weights_of(last, scores(last), jnp.zeros((2 * q_rows, 1), F32), True)
        z_next = scores(jnp.maximum(last - 1, 0))

        def step(k, carry):
            right, acc, z, a_prev = carry
            jb = last - k
            acc = acc + _nn(a_prev, values(jb + 1))
            z_next = scores(jnp.maximum(jb - 1, 0))
            right, a = weights_of(jb, z, right, False)
            return right, acc, z_next, a

        _, acc2, _, a_prev = lax.fori_loop(1, last + 1, step, (right, jnp.zeros((2 * q_rows, LANES), F32), z_next, a_prev))
        acc = _unstack_heads(acc2 + _nn(a_prev, values(0)), first)
        raw_ref[...] = acc
        m_ref[...] = (acc * lax.rsqrt(_head_mean(acc * acc, first) + EPS) * hg_ref[...]).astype(BF16)
        finish()

    t = batch * seq
    blk = lambda c0: pl.BlockSpec((q_rows, LANES), lambda b, p, i: (b * n_q + i, c0 + p))
    kv = lambda c0: pl.BlockSpec((seq, LANES), lambda b, p, i: (b, c0 + p))
    kept = pl.BlockSpec((1, 1, n_blocks, 2, q_rows, block_keys), lambda b, p, i: (p, b, 0, 0, i, 0))
    kept_shape = jax.ShapeDtypeStruct((SB_PAIRS, batch, n_blocks, 2, seq, block_keys), BF16)
    x_in, x_out, x_shape, x_scratch, x_arrays = _riding_specs(exchange)
    res = pl.pallas_call(
        body, name="sb_fwd", grid=(batch, SB_PAIRS, n_q),
        in_specs=[blk(q0), kv(k0), kv(v0), pl.BlockSpec((1, LANES), lambda b, p, i: (0, SB_PAIRS + p)),
                  pl.BlockSpec(memory_space=pl.ANY)] + x_in,
        out_specs=[blk(SB_PAIRS), blk(0), kept, kept] + x_out,
        out_shape=[jax.ShapeDtypeStruct((t, 2 * GM_WIDTH), BF16), jax.ShapeDtypeStruct((t, GM_WIDTH), F32), kept_shape, kept_shape] + x_shape,
        scratch_shapes=x_scratch,
        input_output_aliases={4: 0},
        compiler_params=_params(3),
    )(proj, proj, proj, hg, merged_a, *x_arrays)
    return res[0], res[1], res[2], res[3], res[4:]


def _sb_bwd(proj, raw, weights, signed_e, dmerged, hg, *, batch, seq, exchange=None):
    q0, k0, v0 = 2 * GM_WIDTH // LANES, 2 * GM_WIDTH // LANES + SB_PAIRS, 2 * GM_WIDTH // LANES + 2 * SB_PAIRS
    block_keys, q_rows = min(SB_BLOCK, seq), min(SB_QUERIES_BWD, seq)
    assert block_keys % q_rows == 0 and seq % block_keys == 0
    n_q, n_sub, n_blocks = seq // q_rows, block_keys // SB_TILE, seq // block_keys

    def body(*refs):
        ins, outs, (dk_acc, dv_acc), x_refs = _riding(exchange, refs, 8, 4, 2)
        q_ref, k_ref, v_ref, raw_ref, a_ref, e_ref, dm_ref, hg_ref = ins
        dq_ref, dk_ref, dv_ref, dhg_ref = outs
        p, b, i = pl.program_id(0), pl.program_id(1), pl.program_id(2)
        finish = _ride(exchange, x_refs, (b == 0) & (p == 0) & (i == 0), (b == batch - 1) & (p == SB_PAIRS - 1) & (i == n_q - 1))
        row, col, first = _sb_masks(q_rows)
        lower = (row < col).astype(BF16)

        @pl.when(jnp.logical_and(b == 0, i == 0))
        def _():
            dhg_ref[...] = jnp.zeros_like(dhg_ref)

        @pl.when(i == 0)
        def _():
            dk_acc[...] = jnp.zeros_like(dk_acc)
            dv_acc[...] = jnp.zeros_like(dv_acc)

        raw_v = raw_ref[...]
        dm = dm_ref[...].astype(F32)
        r = lax.rsqrt(_head_mean(raw_v * raw_v, first) + EPS)
        nrm = raw_v * r
        dhg_ref[...] += jnp.sum(dm * nrm, axis=0, keepdims=True)
        dn = dm * hg_ref[...]
        dout = r * (dn - nrm * _head_mean(dn * nrm, first))
        dout2 = _stack_heads(dout.astype(BF16), first)
        q2_t = _stack_heads_t(q_ref[...].astype(F32).T)
        dout2_t = _stack_heads_t(dout.T)
        diff = _stacked_col_minus_row(q_rows)
        last = (i * q_rows) // block_keys
        offset = i * q_rows - last * block_keys

        def kept(ref, jb, cols):
            return jnp.concatenate([ref[0, 0, jb, 0, :, cols], ref[0, 0, jb, 1, :, cols]], axis=0)

        def block(jb, carry, masked, count=n_sub):
            gleft, dq = carry
            tiles = [pl.ds(pl.multiple_of((jb * n_sub + s) * SB_TILE, SB_TILE), SB_TILE) for s in range(count)]
            cols = [slice(s * SB_TILE, (s + 1) * SB_TILE) for s in range(count)]
            gmats = [_nt(dout2, v_ref[tiles[s], :]) * kept(a_ref, jb, cols[s]).astype(F32) for s in range(count)]
            prefixes = [_tri_sums(g, lower) for g in gmats]
            dzs = []
            for s in range(count):
                signed = kept(e_ref, jb, cols[s]).astype(F32)
                e = jnp.abs(signed)
                inv = 1.0 / (1.0 + e)
                small = e * inv
                positive = signed > 0.0
                beta, one_minus = jnp.where(positive, inv, small), jnp.where(positive, small, inv)
                dz = (gmats[s] * one_minus - (gleft + prefixes[s]) * beta) * SB_SCALE
                gleft = gleft + jnp.sum(gmats[s], axis=-1, keepdims=True)
                if masked:
                    dz = jnp.where(diff < offset - s * SB_TILE, dz, 0.0)
                dzs.append(dz.astype(BF16))
            dz_all = jnp.concatenate(dzs, axis=1)
            width = count * SB_TILE
            dk_acc[jb, :, :width] += _nn(q2_t, dz_all)
            dv_acc[jb, :, :width] += _nn(dout2_t, kept(a_ref, jb, slice(0, width)))
            return gleft, dq + _nn(dz_all, k_ref[pl.ds(pl.multiple_of(jb * block_keys, block_keys), width), :])

        carry = (jnp.zeros((2 * q_rows, 1), F32), jnp.zeros((2 * q_rows, LANES), F32))
        carry = lax.fori_loop(0, last, lambda jb, c: block(jb, c, False), carry)
        diagonal = [functools.partial(block, last, carry, True, (c + 1) * q_rows // SB_TILE) for c in range(block_keys // q_rows)]
        dq_ref[...] = _unstack_heads(lax.switch(offset // q_rows, diagonal)[1], first).astype(BF16)

        @pl.when(i == n_q - 1)
        def _():
            for jb in range(n_blocks):
                for s in range(n_sub):
                    rows = slice((jb * n_sub + s) * SB_TILE, (jb * n_sub + s + 1) * SB_TILE)
                    cols = slice(s * SB_TILE, (s + 1) * SB_TILE)
                    dk_ref[rows, :] = dk_acc[jb, :, cols].T.astype(BF16)
                    dv_ref[rows, :] = dv_acc[jb, :, cols].T.astype(BF16)

        finish()

    t = batch * seq
    blk = lambda c0: pl.BlockSpec((q_rows, LANES), lambda p, b, i: (b * n_q + i, c0 + p))
    kv = lambda c0: pl.BlockSpec((seq, LANES), lambda p, b, i: (b, c0 + p))
    row_spec = pl.BlockSpec((1, LANES), lambda p, b, i: (0, SB_PAIRS + p))
    kept_spec = pl.BlockSpec((1, 1, n_blocks, 2, q_rows, block_keys), lambda p, b, i: (p, b, 0, 0, i, 0))
    x_in, x_out, x_shape, x_scratch, x_arrays = _riding_specs(exchange)
    res = pl.pallas_call(
        body, name="sb_bwd", grid=(SB_PAIRS, batch, n_q),
        in_specs=[blk(q0), kv(k0), kv(v0), blk(0), kept_spec, kept_spec, blk(SB_PAIRS), row_spec] + x_in,
        out_specs=[blk(0), kv(0), kv(0), pl.BlockSpec((1, LANES), lambda p, b, i: (0, p))] + x_out,
        out_shape=[jax.ShapeDtypeStruct((t, GM_WIDTH), BF16)] * 3 + [jax.ShapeDtypeStruct((1, GM_WIDTH), F32)] + x_shape,
        scratch_shapes=[pltpu.VMEM((n_blocks, LANES, block_keys), F32), pltpu.VMEM((n_blocks, LANES, block_keys), F32)] + x_scratch,
        compiler_params=_params(3),
    )(proj, proj, proj, raw, weights, signed_e, dmerged, hg, *x_arrays)
    return res[0], res[1], res[2], res[3], res[4:]


def _x_softmax(qh, kh):
    s = _nt(qh, kh) * X_SCALE
    p = jnp.exp(s - jnp.max(s, axis=-1, keepdims=True))
    return p / jnp.sum(p, axis=-1, keepdims=True)


def _xattn_fwd(q, kv, *, batch, seq, n_mem, tq=512):
    tq = min(tq, seq)
    n_q = seq // tq
    d = X_HEADS * X_HEAD_DIM

    def body(q_ref, kv_ref, o_ref):
        for h in range(X_HEADS):
            cols = slice(h * X_HEAD_DIM, (h + 1) * X_HEAD_DIM)
            p = _x_softmax(q_ref[:, cols], kv_ref[:, cols])
            o_ref[:, cols] = _nn(p.astype(BF16), kv_ref[:, d + h * X_HEAD_DIM:d + (h + 1) * X_HEAD_DIM]).astype(BF16)

    return pl.pallas_call(
        body, name="xattn_fwd", grid=(batch, n_q),
        in_specs=[pl.BlockSpec((tq, d), lambda b, i: (b * n_q + i, 0)), pl.BlockSpec((n_mem, 2 * d), lambda b, i: (b, 0))],
        out_specs=pl.BlockSpec((tq, d), lambda b, i: (b * n_q + i, 0)),
        out_shape=jax.ShapeDtypeStruct((batch * seq, d), BF16),
        compiler_params=_params(2),
    )(q, kv)


def _xattn_bwd(q, kv, do, *, batch, seq, n_mem, tq=512):
    tq = min(tq, seq)
    n_q = seq // tq
    d = X_HEADS * X_HEAD_DIM

    def body(q_ref, kv_ref, do_ref, dq_ref, dkv_ref, acc):
        i = pl.program_id(1)

        @pl.when(i == 0)
        def _():
            acc[...] = jnp.zeros_like(acc)

        for h in range(X_HEADS):
            cols = slice(h * X_HEAD_DIM, (h + 1) * X_HEAD_DIM)
            vcols = slice(d + h * X_HEAD_DIM, d + (h + 1) * X_HEAD_DIM)
            qh, kh, vh, doh = q_ref[:, cols], kv_ref[:, cols], kv_ref[:, vcols], do_ref[:, cols]
            p = _x_softmax(qh, kh)
            dp = _nt(doh, vh)
            acc[:, vcols] += _tn(p.astype(BF16), doh)
            ds = (p * (dp - jnp.sum(dp * p, axis=-1, keepdims=True)) * X_SCALE).astype(BF16)
            dq_ref[:, cols] = _nn(ds, kh).astype(BF16)
            acc[:, cols] += _tn(ds, qh)

        @pl.when(i == n_q - 1)
        def _():
            dkv_ref[...] = acc[...].astype(BF16)

    return pl.pallas_call(
        body, name="xattn_bwd", grid=(batch, n_q),
        in_specs=[pl.BlockSpec((tq, d), lambda b, i: (b * n_q + i, 0)), pl.BlockSpec((n_mem, 2 * d), lambda b, i: (b, 0)),
                  pl.BlockSpec((tq, d), lambda b, i: (b * n_q + i, 0))],
        out_specs=[pl.BlockSpec((tq, d), lambda b, i: (b * n_q + i, 0)), pl.BlockSpec((n_mem, 2 * d), lambda b, i: (b, 0))],
        out_shape=[jax.ShapeDtypeStruct((batch * seq, d), BF16), jax.ShapeDtypeStruct((batch * n_mem, 2 * d), BF16)],
        scratch_shapes=[pltpu.VMEM((n_mem, 2 * d), F32)],
        compiler_params=_params(2),
    )(q, kv, do)


def _my_index():
    return 4 * lax.axis_index("x") + 2 * lax.axis_index("y") + lax.axis_index("c")


def _peers():
    x, y, c = lax.axis_index("x"), lax.axis_index("y"), lax.axis_index("c")
    out = []
    for rel in range(1, N_DEV):
        dx, dy, dc = (rel >> 2) & 1, (rel >> 1) & 1, rel & 1
        px, py, pc = x ^ dx, y ^ dy, c ^ dc
        out.append(((px, py, pc), 4 * px + 2 * py + pc))
    return out


class _Exchange:
    def __init__(self, arrays, scatter):
        self.arrays, self.scatter, self.n = list(arrays), scatter, len(arrays)
        any_spec = pl.BlockSpec(memory_space=pl.ANY)
        self.in_specs = [any_spec] * self.n
        self.out_specs = [any_spec] * self.n
        self.out_shape = [jax.ShapeDtypeStruct((N_DEV,) + tuple(a.shape[-2:]), a.dtype) for a in self.arrays]
        n_peer = N_DEV - 1
        self.scratch = [pltpu.SemaphoreType.DMA((self.n, n_peer)), pltpu.SemaphoreType.DMA((self.n, n_peer)),
                        pltpu.SemaphoreType.DMA((self.n,))]

    def _copies(self, srcs, dsts, sems, arriving):
        send_sems, recv_sems, local_sems = sems
        me = _my_index()
        local, remote = [], []
        for w in range(self.n):
            if not arriving:
                local.append(pltpu.make_async_copy(srcs[w].at[me] if self.scatter else srcs[w], dsts[w].at[me], local_sems.at[w]))
            for rel, (pos, idx) in enumerate(_peers()):
                remote.append(pltpu.make_async_remote_copy(
                    src_ref=srcs[w].at[idx] if self.scatter else srcs[w], dst_ref=dsts[w].at[idx if arriving else me],
                    send_sem=send_sems.at[w, rel], recv_sem=recv_sems.at[w, rel], device_id=pos, device_id_type=MESH))
        return local, remote

    def start(self, srcs, dsts, sems):
        local, sends = self._copies(srcs, dsts, sems, arriving=False)
        for cp in local + sends:
            cp.start()

    def wait(self, srcs, dsts, sems):
        for cp in self._copies(srcs, dsts, sems, arriving=True)[1]:
            cp.wait_recv()
        local, sends = self._copies(srcs, dsts, sems, arriving=False)
        for cp in sends:
            cp.wait_send()
        for cp in local:
            cp.wait()

    def run(self, name):
        n = self.n

        def body(*refs):
            srcs, dsts, sems = refs[:n], refs[n:2 * n], refs[2 * n:]
            self.start(srcs, dsts, sems)
            self.wait(srcs, dsts, sems)

        return pl.pallas_call(body, name=name, in_specs=self.in_specs, out_specs=self.out_specs, out_shape=self.out_shape,
                              scratch_shapes=self.scratch)(*self.arrays)


def _all_reduce_small(part, *, loss_rows, loss_scale):
    rows = part.shape[0]
    n_peer = N_DEV - 1

    def body(p_ref, o_ref, buf, send_sems, recv_sems):
        me = _my_index()
        peers = _peers()
        buf[me] = p_ref[...]
        sends = [pltpu.make_async_remote_copy(src_ref=p_ref, dst_ref=buf.at[me], send_sem=send_sems.at[rel], recv_sem=recv_sems.at[rel],
                                              device_id=peers[rel][0], device_id_type=MESH) for rel in range(n_peer)]
        for cp in sends:
            cp.start()
        for rel in range(n_peer):
            pltpu.make_async_remote_copy(src_ref=p_ref, dst_ref=buf.at[peers[rel][1]], send_sem=send_sems.at[rel], recv_sem=recv_sems.at[rel],
                                         device_id=peers[rel][0], device_id_type=MESH).wait_recv()
        for cp in sends:
            cp.wait_send()
        total = buf[0]
        for dev in range(1, N_DEV):
            total = total + buf[dev]
        o_ref[...] = total
        squares = total[rows - loss_rows:]
        loss = jnp.sum(jnp.sum(squares, axis=0, keepdims=True), axis=-1, keepdims=True) * loss_scale
        o_ref[rows - loss_rows:, :] = jnp.broadcast_to(loss, (loss_rows, LANES))

    vmem = pl.BlockSpec(memory_space=pltpu.VMEM)
    return pl.pallas_call(
        body, name="all_reduce_small", in_specs=[vmem], out_specs=vmem, out_shape=jax.ShapeDtypeStruct(part.shape, F32),
        scratch_shapes=[pltpu.VMEM((N_DEV, rows, LANES), F32), pltpu.SemaphoreType.DMA((n_peer,)), pltpu.SemaphoreType.DMA((n_peer,))],
        compiler_params=pltpu.CompilerParams(has_side_effects=True, vmem_limit_bytes=VMEM_LIMIT),
    )(part)


def _adamw_math(w, g, m, v):
    m_new = ADAM_B1 * m + (1.0 - ADAM_B1) * g
    v_new = ADAM_B2 * v + (1.0 - ADAM_B2) * (g * g)
    m_hat = m_new / (1.0 - ADAM_B1 ** ADAM_STEP)
    v_hat = v_new / (1.0 - ADAM_B2 ** ADAM_STEP)
    delta = -ADAM_LR * (m_hat / (jnp.sqrt(v_hat) + ADAM_EPS) + ADAM_WD * w)
    return delta, m_new, v_new


def _adamw(parts, w, m, v, *, name, tr=64):
    rows, cols = w.shape
    tr = min(tr, rows)
    assert rows % tr == 0
    stacked = parts.ndim == 3

    def body(p_ref, w_ref, m_ref, v_ref, g_ref, d_ref, mo_ref, vo_ref):
        if stacked:
            g = p_ref[0].astype(F32)
            for dev in range(1, N_DEV):
                g = g + p_ref[dev].astype(F32)
        else:
            g = p_ref[...]
        delta, m_new, v_new = _adamw_math(w_ref[...], g, m_ref[...], v_ref[...])
        g_ref[...] = g
        d_ref[...] = delta
        mo_ref[...] = m_new
        vo_ref[...] = v_new

    tile = pl.BlockSpec((tr, cols), lambda i: (i, 0))
    p_spec = pl.BlockSpec((N_DEV, tr, cols), lambda i: (0, i, 0)) if stacked else tile
    return pl.pallas_call(
        body, name=name, grid=(rows // tr,), in_specs=[p_spec, tile, tile, tile], out_specs=[tile] * 4,
        out_shape=[jax.ShapeDtypeStruct((rows, cols), F32)] * 4, compiler_params=_params(1),
    )(parts, w, m, v)


_LATER = ("w_out", "w_cq", "w_ckv", "w_co", "w_ff1", "w_ff2")


def _as_rows(stacked):
    return stacked.reshape(-1, stacked.shape[-1])


def _local_step(x, mem, target, small, shards):
    batch, seq, d = x.shape
    n_mem = mem.shape[1]
    t = batch * seq
    x2, mem2, tgt2 = x.reshape(t, d), mem.reshape(batch * n_mem, d), target.reshape(t, d)
    g_mix, g_cross, g_mem, g_ffn, g_final = (small[k] for k in ("norm_mix_g", "norm_cross_g", "norm_mem_g", "norm_ffn_g", "norm_final_g"))
    gv, hg, w_sp, b_sp_t = small["gm_v_norm_g"], small["head_norm_g"], small["w_spatial"], small["b_spatial_t"]

    win_t = _as_rows(_Exchange([shards["w_in"]], scatter=False).run("gather_w_in")[0])
    proj, xn = _norm_mm(x2, g_mix, win_t, mode="nt", name="proj_fwd", tm=512, tn=win_t.shape[0])
    merged_a = _gmlp_fwd(proj, w_sp, b_sp_t, gv, hg)
    merged, sb_raw, sb_weights, sb_signed_e, gathered = _sb_fwd(proj, merged_a, hg, batch=batch, seq=seq,
                                                                exchange=_Exchange([shards[n] for n in _LATER], scatter=False))
    wout, wcq, wckv_t, wco, wff1_t, wff2 = (_as_rows(g) for g in gathered)
    h1 = _mm(merged, wout, mode="nn", out_dtype=F32, name="mix_out_fwd", epi=_epi_residual, epi_ins=(x2,))
    qx, hn1 = _norm_mm(h1, g_cross, wcq, mode="nn", name="xq_fwd")
    kvx, memn = _norm_mm(mem2, g_mem, wckv_t, mode="nt", name="xkv_fwd", tm=512, tn=wckv_t.shape[0])
    o = _xattn_fwd(qx, kvx, batch=batch, seq=seq, n_mem=n_mem)
    h2 = _mm(o, wco, mode="nn", out_dtype=F32, name="xo_fwd", epi=_epi_residual, epi_ins=(h1,))
    fpre, hn2 = _norm_mm(h2, g_ffn, wff1_t, mode="nt", name="ff1_fwd", tm=512, tn=wff1_t.shape[0])
    dh3, final_rows = _mm(fpre, wff2, mode="nn", out_dtype=F32, name="ff2_fwd_loss", tm=512, tk=wff2.shape[0], a_fn=_relu2,
                          epi=_epi_loss, epi_ins=(h2, tgt2), vec_ins=(g_final,), aux=2)
    dg_final, sq_err = final_rows[0:1], final_rows[1:2]

    dpre = _mm(dh3, wff2, mode="nt", out_dtype=BF16, name="ff2_bwd_x", tm=512, tn=wff2.shape[0], epi=_epi_relu2_grad,
               epi_ins=(fpre,))
    d_wff2 = _mm(fpre, dh3, mode="tn", out_dtype=BF16, name="ff2_bwd_w", tm=2048, a_fn=_relu2)
    d_wff1_t = _mm(dpre, hn2, mode="tn", out_dtype=BF16, name="ff1_bwd_w", tm=2048)
    dh2, dg_ffn = _mm(dpre, wff1_t, mode="nn", out_dtype=F32, name="ff1_bwd_x", tm=512, tk=wff1_t.shape[0], epi=_epi_rms_bwd,
                      epi_ins=(h2, dh3), vec_ins=(g_ffn,), aux=True)
    do = _mm(dh2, wco, mode="nt", out_dtype=BF16, name="xo_bwd_x")
    d_wco = _mm(o, dh2, mode="tn", out_dtype=BF16, name="xo_bwd_w")
    dqx, dkvx = _xattn_bwd(qx, kvx, do, batch=batch, seq=seq, n_mem=n_mem)
    d_wcq = _mm(hn1, dqx, mode="tn", out_dtype=BF16, name="xq_bwd_w")
    dh1, dg_cross = _mm(dqx, wcq, mode="nt", out_dtype=F32, name="xq_bwd_x", tm=512, epi=_epi_rms_bwd,
                        epi_ins=(h1, dh2), vec_ins=(g_cross,), aux=True)
    d_wckv_t = _mm(dkvx, memn, mode="tn", out_dtype=BF16, name="xkv_bwd_w")
    _, dg_mem = _mm(dkvx, wckv_t, mode="nn", out_dtype=BF16, name="xkv_bwd_x", tm=512, epi=_epi_rms_gain_only,
                    epi_ins=(mem2,), vec_ins=(g_mem,), aux=True)
    dmerged = _mm(dh1, wout, mode="nt", out_dtype=BF16, name="mix_out_bwd_x")
    d_wout = _mm(merged, dh1, mode="tn", out_dtype=BF16, name="mix_out_bwd_w")
    dp_a, d_wsp, d_bsp_t, d_gv, d_hg_a = _gmlp_bwd(proj, dmerged, w_sp, b_sp_t, gv, hg)
    d_later = {"w_out": d_wout, "w_cq": d_wcq, "w_ckv": d_wckv_t, "w_co": d_wco, "w_ff1": d_wff1_t, "w_ff2": d_wff2}
    scatter = _Exchange([d_later[n].reshape(N_DEV, -1, d) for n in _LATER], scatter=True)
    dq, dk, dv, d_hg_b, received = _sb_bwd(proj, sb_raw, sb_weights, sb_signed_e, dmerged, hg, batch=batch, seq=seq, exchange=scatter)
    dproj = jnp.concatenate([dp_a, dq, dk, dv], axis=1)
    d_win_t = _mm(dproj, xn, mode="tn", out_dtype=BF16, name="proj_bwd_w", tm=512)
    dx, dg_mix, d_win_received = _mm(dproj, win_t, mode="nn", out_dtype=F32, name="proj_bwd_x", tm=512, tk=win_t.shape[0],
                                     epi=_epi_rms_bwd, epi_ins=(x2, dh1), vec_ins=(g_mix,), aux=True,
                                     exchange=_Exchange([d_win_t.reshape(N_DEV, -1, d)], scatter=True))

    d_small = {"norm_mix_g": dg_mix, "gm_v_norm_g": d_gv, "w_spatial": d_wsp, "b_spatial_t": d_bsp_t, "head_norm_g": jnp.concatenate([d_hg_a, d_hg_b], axis=1),
               "norm_cross_g": dg_cross, "norm_mem_g": dg_mem, "norm_ffn_g": dg_ffn, "norm_final_g": dg_final}
    d_big = dict(zip(_LATER, received))
    d_big["w_in"] = d_win_received
    return sq_err, dx.reshape(batch, seq, d), d_small, d_big


_BIG = ("w_in", "w_out", "w_cq", "w_ckv", "w_co", "w_ff1", "w_ff2")
_BIG_TRANSPOSED = ("w_in", "w_ckv", "w_ff1")
_SMALL = ("norm_mix_g", "gm_v_norm_g", "w_spatial", "b_spatial", "head_norm_g", "norm_cross_g", "norm_mem_g", "norm_ffn_g", "norm_final_g")
_NAMES = ("norm_mix_g", "w_in", "gm_v_norm_g", "w_spatial", "b_spatial", "head_norm_g", "w_out", "norm_cross_g", "norm_mem_g",
          "w_cq", "w_ckv", "w_co", "norm_ffn_g", "w_ff1", "w_ff2", "norm_final_g")


def _rows_of(a):
    r = a.reshape(-1, LANES)
    pad = (-r.shape[0]) % 8
    return jnp.pad(r, ((0, pad), (0, 0))) if pad else r


def _shard2d(name, a):
    a = a[0]
    return a.T if name in _BIG_TRANSPOSED else a


def kernel(x, mem, norm_mix_g, w_in, gm_v_norm_g, w_spatial, b_spatial, head_norm_g, w_out, norm_cross_g, norm_mem_g, w_cq, w_ckv, w_co, norm_ffn_g, w_ff1, w_ff2, norm_final_g, loss_target, m_norm_mix_g, m_w_in, m_gm_v_norm_g, m_w_spatial, m_b_spatial, m_head_norm_g, m_w_out, m_norm_cross_g, m_norm_mem_g, m_w_cq, m_w_ckv, m_w_co, m_norm_ffn_g, m_w_ff1, m_w_ff2, m_norm_final_g, v_norm_mix_g, v_w_in, v_gm_v_norm_g, v_w_spatial, v_b_spatial, v_head_norm_g, v_w_out, v_norm_cross_g, v_norm_mem_g, v_w_cq, v_w_ckv, v_w_co, v_norm_ffn_g, v_w_ff1, v_w_ff2, v_norm_final_g):
    weights = dict(norm_mix_g=norm_mix_g, w_in=w_in, gm_v_norm_g=gm_v_norm_g, w_spatial=w_spatial, b_spatial=b_spatial,
                   head_norm_g=head_norm_g, w_out=w_out, norm_cross_g=norm_cross_g, norm_mem_g=norm_mem_g, w_cq=w_cq, w_ckv=w_ckv,
                   w_co=w_co, norm_ffn_g=norm_ffn_g, w_ff1=w_ff1, w_ff2=w_ff2, norm_final_g=norm_final_g)
    mom1 = dict(norm_mix_g=m_norm_mix_g, w_in=m_w_in, gm_v_norm_g=m_gm_v_norm_g, w_spatial=m_w_spatial, b_spatial=m_b_spatial,
                head_norm_g=m_head_norm_g, w_out=m_w_out, norm_cross_g=m_norm_cross_g, norm_mem_g=m_norm_mem_g, w_cq=m_w_cq,
                w_ckv=m_w_ckv, w_co=m_w_co, norm_ffn_g=m_norm_ffn_g, w_ff1=m_w_ff1, w_ff2=m_w_ff2, norm_final_g=m_norm_final_g)
    mom2 = dict(norm_mix_g=v_norm_mix_g, w_in=v_w_in, gm_v_norm_g=v_gm_v_norm_g, w_spatial=v_w_spatial, b_spatial=v_b_spatial,
                head_norm_g=v_head_norm_g, w_out=v_w_out, norm_cross_g=v_norm_cross_g, norm_mem_g=v_norm_mem_g, w_cq=v_w_cq,
                w_ckv=v_w_ckv, w_co=v_w_co, norm_ffn_g=v_norm_ffn_g, w_ff1=v_w_ff1, w_ff2=v_w_ff2, norm_final_g=v_norm_final_g)

    shards = {n: _shard2d(n, weights[n]).astype(BF16) for n in _BIG}
    small = {n: weights[n].reshape(1, -1) for n in _SMALL if n not in ("w_spatial", "b_spatial")}
    small["w_spatial"] = w_spatial[0]
    small["b_spatial_t"] = b_spatial[0].T
    sq_err, grad_x, d_small, d_big = _local_step(x, mem, loss_target, small, shards)

    d_small["b_spatial"] = d_small.pop("b_spatial_t").T
    sq_rows = _rows_of(sq_err)
    packed = jnp.concatenate([_rows_of(d_small[n]) for n in _SMALL] + [sq_rows], axis=0)
    summed = _all_reduce_small(packed, loss_rows=sq_rows.shape[0], loss_scale=0.5 / x.shape[-1])

    grads, deltas, new_m, new_v = {}, {}, {}, {}
    for n in _BIG:
        outs = _adamw(d_big[n], _shard2d(n, weights[n]), _shard2d(n, mom1[n]), _shard2d(n, mom2[n]), name="adamw_" + n)
        outs = [o.T if n in _BIG_TRANSPOSED else o for o in outs]
        grads[n], deltas[n], new_m[n], new_v[n] = (o[None] for o in outs)
    pack = lambda src: jnp.concatenate([_rows_of(src[n]) for n in _SMALL], axis=0)
    n_small_rows = sum(_rows_of(weights[n]).shape[0] for n in _SMALL)
    outs = _adamw(summed[:n_small_rows], pack(weights), pack(mom1), pack(mom2), name="adamw_small", tr=n_small_rows)
    at = 0
    for n in _SMALL:
        used = weights[n].size // LANES
        for dst, o in zip((grads, deltas, new_m, new_v), outs):
            dst[n] = o[at:at + used].reshape(weights[n].shape)
        at += _rows_of(weights[n]).shape[0]
    loss = summed[n_small_rows, 0]
    return (loss, grad_x, *[grads[n] for n in _NAMES], *[deltas[n] for n in _NAMES], *[new_m[n] for n in _NAMES],
            *[new_v[n] for n in _NAMES])
```

```python
import math

import jax
import jax.numpy as jnp
from jax import lax
from jax.experimental import pallas as pl
from jax.experimental.pallas import tpu as pltpu

F32 = jnp.float32
BF16 = jnp.bfloat16
EPS = 1e-6
N_DEV = 8
LANES = 128
CHUNK = 128
GM_GROUPS = 4
GM_WIDTH = 512
SB_PAIRS = 4
SB_HEAD_DIM = 64
SB_SCALE = 0.125
SB_TILE = 128
SB_BLOCK = 512
X_HEADS = 4
X_HEAD_DIM = 256
X_SCALE = 1.0 / 16.0
VMEM_LIMIT = 56 * 1024 * 1024
ADAM_LR, ADAM_B1, ADAM_B2, ADAM_EPS, ADAM_WD, ADAM_STEP = 0.001, 0.9, 0.999, 1e-08, 0.01, 10
MESH = pl.DeviceIdType.MESH


def _params(n_axes):
    return pltpu.CompilerParams(dimension_semantics=("arbitrary",) * n_axes, vmem_limit_bytes=VMEM_LIMIT)


def _dot(a, b, dims):
    return lax.dot_general(a, b, (dims, ((), ())), preferred_element_type=F32)


def _nn(a, b):
    return _dot(a, b, ((1,), (0,)))


def _nt(a, b):
    return _dot(a, b, ((1,), (1,)))


def _tn(a, b):
    return _dot(a, b, ((0,), (0,)))


_MODES = {"nn": _nn, "nt": _nt, "tn": _tn}


def _rstd(x):
    return lax.rsqrt(jnp.mean(x * x, axis=-1, keepdims=True) + EPS)


def _gelu(x):
    c = math.sqrt(2.0 / math.pi)
    t = jnp.tanh(c * (x + 0.044715 * x * x * x))
    return 0.5 * x * (1.0 + t)


def _gelu_and_grad(x):
    c = math.sqrt(2.0 / math.pi)
    t = jnp.tanh(c * (x + 0.044715 * x * x * x))
    half = 0.5 * (1.0 + t)
    return x * half, half + 0.5 * x * (1.0 - t * t) * c * (1.0 + 3 * 0.044715 * x * x)


def _split_bf16(x):
    hi = x.astype(BF16)
    lo = (x - hi.astype(F32)).astype(BF16)
    return hi, lo


def _mm(a, b, *, mode, out_dtype, name, tm=1024, tn=1024, tk=1024, a_fn=None, epi=None, epi_ins=(), vec_ins=(), aux=False,
        exchange=None):
    if mode == "nn":
        (m, k), (k2, n) = a.shape, b.shape
    elif mode == "nt":
        (m, k), (n, k2) = a.shape, b.shape
    else:
        (k, m), (k2, n) = a.shape, b.shape
    assert k == k2, (a.shape, b.shape, mode)
    tm, tn, tk = min(tm, m), min(tn, n), min(tk, k)
    assert m % tm == 0 and n % tn == 0 and k % tk == 0, (m, n, k, tm, tn, tk)
    n_m, n_n, n_k = m // tm, n // tn, k // tk
    assert not aux or n_n == 1
    dot = _MODES[mode]
    n_epi, n_vec = len(epi_ins), len(vec_ins)

    def body(*refs):
        ins, outs, scratch, x_refs = _riding(exchange, refs, 2 + n_epi + n_vec, 2 if aux else 1, 1 if n_k > 1 else 0)
        a_ref, b_ref, epi_refs = ins[0], ins[1], ins[2:]
        o_ref = outs[0]
        aux_ref = outs[1] if aux else None
        acc_ref = scratch[0] if n_k > 1 else None
        i, j, kk = pl.program_id(0), pl.program_id(1), pl.program_id(2)
        ride_done = _ride(exchange, x_refs, (i == 0) & (j == 0) & (kk == 0), (i == n_m - 1) & (j == n_n - 1) & (kk == n_k - 1))
        av = a_ref[...]
        if a_fn is not None:
            av = a_fn(av)
        part = dot(av.astype(BF16), b_ref[...].astype(BF16))

        def finish(acc):
            if epi is None:
                o_ref[...] = acc.astype(out_dtype)
                return
            res = epi(acc, *[r[...] for r in epi_refs])
            if aux:
                res, rows = res[0], res[1:]
                rows = rows[0] if len(rows) == 1 else jnp.concatenate(rows, axis=0)

                @pl.when(i == 0)
                def _():
                    aux_ref[...] = rows

                @pl.when(i != 0)
                def _():
                    aux_ref[...] += rows
            o_ref[...] = res.astype(out_dtype)

        if n_k == 1:
            finish(part)
        else:
            @pl.when(kk == 0)
            def _():
                acc_ref[...] = part

            @pl.when(kk != 0)
            def _():
                acc_ref[...] += part

            @pl.when(kk == n_k - 1)
            def _():
                finish(acc_ref[...])

        ride_done()

    if mode == "tn":
        a_spec = pl.BlockSpec((tk, tm), lambda i, j, kk: (kk, i))
    else:
        a_spec = pl.BlockSpec((tm, tk), lambda i, j, kk: (i, kk))
    if mode == "nt":
        b_spec = pl.BlockSpec((tn, tk), lambda i, j, kk: (j, kk))
    else:
        b_spec = pl.BlockSpec((tk, tn), lambda i, j, kk: (kk, j))
    tile_spec = pl.BlockSpec((tm, tn), lambda i, j, kk: (i, j))
    row_spec = pl.BlockSpec((1, tn), lambda i, j, kk: (0, j))
    out_shape = [jax.ShapeDtypeStruct((m, n), out_dtype)]
    out_specs = [tile_spec]
    if aux:
        out_shape.append(jax.ShapeDtypeStruct((int(aux), n), F32))
        out_specs.append(pl.BlockSpec((int(aux), tn), lambda i, j, kk: (0, j)))
    x_in, x_out, x_shape, x_scratch, x_arrays = _riding_specs(exchange)
    res = pl.pallas_call(
        body, name=name, grid=(n_m, n_n, n_k),
        in_specs=[a_spec, b_spec] + [tile_spec] * n_epi + [row_spec] * n_vec + x_in,
        out_specs=out_specs + x_out, out_shape=out_shape + x_shape,
        scratch_shapes=([pltpu.VMEM((tm, tn), F32)] if n_k > 1 else []) + x_scratch,
        compiler_params=_params(3),
    )(a, b, *epi_ins, *vec_ins, *x_arrays)
    if exchange is not None:
        return tuple(res)
    return res if aux else res[0]


def _norm_mm(x, g, w, *, mode, name, tm=1024, tn=1024):
    m, d = x.shape
    n = w.shape[0] if mode == "nt" else w.shape[1]
    tm, tn = min(tm, m), min(tn, n)
    assert m % tm == 0 and n % tn == 0
    dot = _MODES[mode]

    def body(x_ref, g_ref, w_ref, o_ref, xn_ref, xn_s):
        @pl.when(pl.program_id(1) == 0)
        def _():
            xv = x_ref[...]
            xn = (xv * _rstd(xv) * g_ref[...]).astype(BF16)
            xn_s[...] = xn
            xn_ref[...] = xn

        o_ref[...] = dot(xn_s[...], w_ref[...]).astype(BF16)

    w_spec = pl.BlockSpec((tn, d), lambda i, j: (j, 0)) if mode == "nt" else pl.BlockSpec((d, tn), lambda i, j: (0, j))
    return pl.pallas_call(
        body, name=name, grid=(m // tm, n // tn),
        in_specs=[pl.BlockSpec((tm, d), lambda i, j: (i, 0)), pl.BlockSpec((1, d), lambda i, j: (0, 0)), w_spec],
        out_specs=[pl.BlockSpec((tm, tn), lambda i, j: (i, j)), pl.BlockSpec((tm, d), lambda i, j: (i, 0))],
        out_shape=[jax.ShapeDtypeStruct((m, n), BF16), jax.ShapeDtypeStruct((m, d), BF16)],
        scratch_shapes=[pltpu.VMEM((tm, d), BF16)],
        compiler_params=_params(2),
    )(x, g, w)


def _epi_residual(acc, res):
    return res + acc


def _epi_relu2_grad(acc, pre):
    return acc * (2.0 * jnp.maximum(pre.astype(F32), 0.0))


def _relu2(pre):
    r = jnp.maximum(pre.astype(F32), 0.0)
    return r * r


def _epi_rms_bwd(acc, h, dres, g):
    r = _rstd(h)
    xh = h * r
    dxh = acc * g
    dh = dres + r * (dxh - xh * jnp.mean(dxh * xh, axis=-1, keepdims=True))
    return dh, jnp.sum(acc * xh, axis=0, keepdims=True)


def _epi_loss(acc, h_in, target, g):
    h = h_in + acc
    r = _rstd(h)
    xh = h * r
    err = xh * g - target
    dy = err * (1.0 / h.shape[-1])
    dxh = dy * g
    dh = r * (dxh - xh * jnp.mean(dxh * xh, axis=-1, keepdims=True))
    return dh, jnp.sum(dy * xh, axis=0, keepdims=True), jnp.sum(err * err, axis=0, keepdims=True)


def _epi_rms_gain_only(acc, h, g):
    return acc, jnp.sum(acc * (h * _rstd(h)), axis=0, keepdims=True)


def _tril(n):
    row = lax.broadcasted_iota(jnp.int32, (n, n), 0)
    col = lax.broadcasted_iota(jnp.int32, (n, n), 1)
    return col <= row


def _gmlp_fwd(proj, w_sp, b_sp_t, gv, hg, *, rows=512):
    t = proj.shape[0]
    rows = min(rows, t)
    n_c = rows // CHUNK

    def body(u_ref, v_ref, w_ref, bt_ref, gv_ref, hg_ref, m_ref):
        keep = _tril(CHUNK)
        for g in range(GM_GROUPS):
            cols = slice(g * LANES, (g + 1) * LANES)
            wg = jnp.where(keep, w_ref[g], 0.0).astype(BF16)
            u = _gelu(u_ref[:, cols].astype(F32))
            v = _gelu(v_ref[:, cols].astype(F32))
            vn = (v * _rstd(v) * gv_ref[:, cols]).astype(BF16)
            bias = bt_ref[:, g:g + 1]
            for c in range(n_c):
                rs = slice(c * CHUNK, (c + 1) * CHUNK)
                mixed = _nn(wg, vn[rs]) + bias
                a = u[rs] * mixed
                m_ref[rs, cols] = (a * _rstd(a) * hg_ref[:, cols]).astype(BF16)

    full = lambda shape: pl.BlockSpec(shape, lambda i: (0,) * len(shape))
    return pl.pallas_call(
        body, name="gmlp_fwd", grid=(t // rows,),
        in_specs=[pl.BlockSpec((rows, GM_WIDTH), lambda i: (i, 0)), pl.BlockSpec((rows, GM_WIDTH), lambda i: (i, 1)),
                  full((GM_GROUPS, CHUNK, CHUNK)), full((CHUNK, GM_GROUPS)), full((1, GM_WIDTH)), full((1, GM_WIDTH))],
        out_specs=pl.BlockSpec((rows, GM_WIDTH), lambda i: (i, 0)),
        out_shape=jax.ShapeDtypeStruct((t, 2 * GM_WIDTH), BF16),
        compiler_params=_params(1),
    )(proj, proj, w_sp, b_sp_t, gv, hg)


def _gmlp_bwd(proj, dmerged, w_sp, b_sp_t, gv, hg, *, rows=512):
    t = proj.shape[0]
    rows = min(rows, t)
    n_c = rows // CHUNK
    n_steps = t // rows

    def body(u_ref, v_ref, dm_ref, w_ref, bt_ref, gv_ref, hg_ref, dp_ref, dw_ref, dbt_ref, dgv_ref, dhg_ref, db_acc):
        step = pl.program_id(0)
        keep = _tril(CHUNK)

        @pl.when(step == 0)
        def _():
            dw_ref[...] = jnp.zeros_like(dw_ref)
            db_acc[...] = jnp.zeros_like(db_acc)
            dgv_ref[...] = jnp.zeros_like(dgv_ref)
            dhg_ref[...] = jnp.zeros_like(dhg_ref)

        for g in range(GM_GROUPS):
            cols = slice(g * LANES, (g + 1) * LANES)
            wg = jnp.where(keep, w_ref[g], 0.0).astype(BF16)
            u, u_slope = _gelu_and_grad(u_ref[:, cols].astype(F32))
            v, v_slope = _gelu_and_grad(v_ref[:, cols].astype(F32))
            r = _rstd(v)
            xh = v * r
            gvg = gv_ref[:, cols]
            hgg = hg_ref[:, cols]
            vn = (xh * gvg).astype(BF16)
            bias = bt_ref[:, g:g + 1]
            dm = dm_ref[:, cols].astype(F32)
            du_parts, dvn_parts = [], []
            dw = jnp.zeros((CHUNK, CHUNK), F32)
            db = jnp.zeros((CHUNK, LANES), F32)
            dhg = jnp.zeros((1, LANES), F32)
            for c in range(n_c):
                rs = slice(c * CHUNK, (c + 1) * CHUNK)
                mixed = _nn(wg, vn[rs]) + bias
                a = u[rs] * mixed
                ra = _rstd(a)
                an = a * ra
                dhg = dhg + jnp.sum(dm[rs] * an, axis=0, keepdims=True)
                dan = dm[rs] * hgg
                da = ra * (dan - an * jnp.mean(dan * an, axis=-1, keepdims=True))
                du_parts.append(da * mixed)
                dmixed = da * u[rs]
                db = db + dmixed
                dmb = dmixed.astype(BF16)
                dw = dw + _nt(dmb, vn[rs])
                dvn_parts.append(_tn(wg, dmb))
            du = jnp.concatenate(du_parts, axis=0)
            dvn = jnp.concatenate(dvn_parts, axis=0)
            dw_ref[g] += dw
            db_acc[g] += db
            dhg_ref[:, cols] += dhg
            dgv_ref[:, cols] += jnp.sum(dvn * xh, axis=0, keepdims=True)
            dxh = dvn * gvg
            dv = r * (dxh - xh * jnp.mean(dxh * xh, axis=-1, keepdims=True))
            dp_ref[:, cols] = (du * u_slope).astype(BF16)
            dp_ref[:, GM_WIDTH + g * LANES:GM_WIDTH + (g + 1) * LANES] = (dv * v_slope).astype(BF16)

        @pl.when(step == n_steps - 1)
        def _():
            for g in range(GM_GROUPS):
                dw_ref[g] = jnp.where(keep, dw_ref[g], 0.0)
                dbt_ref[:, g:g + 1] = jnp.sum(db_acc[g], axis=-1, keepdims=True)

    full = lambda shape: pl.BlockSpec(shape, lambda i: (0,) * len(shape))
    return pl.pallas_call(
        body, name="gmlp_bwd", grid=(n_steps,),
        in_specs=[pl.BlockSpec((rows, GM_WIDTH), lambda i: (i, 0)), pl.BlockSpec((rows, GM_WIDTH), lambda i: (i, 1)),
                  pl.BlockSpec((rows, GM_WIDTH), lambda i: (i, 0)),
                  full((GM_GROUPS, CHUNK, CHUNK)), full((CHUNK, GM_GROUPS)), full((1, GM_WIDTH)), full((1, GM_WIDTH))],
        out_specs=[pl.BlockSpec((rows, 2 * GM_WIDTH), lambda i: (i, 0)), full((GM_GROUPS, CHUNK, CHUNK)),
                   full((CHUNK, GM_GROUPS)), full((1, GM_WIDTH)), full((1, GM_WIDTH))],
        out_shape=[jax.ShapeDtypeStruct((t, 2 * GM_WIDTH), BF16), jax.ShapeDtypeStruct((GM_GROUPS, CHUNK, CHUNK), F32),
                   jax.ShapeDtypeStruct((CHUNK, GM_GROUPS), F32), jax.ShapeDtypeStruct((1, GM_WIDTH), F32),
                   jax.ShapeDtypeStruct((1, GM_WIDTH), F32)],
        scratch_shapes=[pltpu.VMEM((GM_GROUPS, CHUNK, LANES), F32)],
        compiler_params=_params(1),
    )(proj, proj, dmerged, w_sp, b_sp_t, gv, hg)


def _sb_logits(z, strict):
    e = jnp.exp(-jnp.abs(z))
    ls = jnp.minimum(z, 0.0) - jnp.log(1.0 + e)
    l1m = ls - z
    if strict is not None:
        l1m = jnp.where(strict, l1m, 0.0)
    return ls, l1m, jnp.where(z >= 0.0, e, -e)


def _tri_sums(x, tri):
    hi, lo = _split_bf16(x)
    return _nn(jnp.concatenate([hi, lo], axis=1), jnp.concatenate([tri, tri], axis=0))


def _sb_weights(ls, in_tile, right, strict):
    a = jnp.exp(ls + in_tile + right)
    if strict is not None:
        a = jnp.where(strict, a, 0.0)
    return a


def _sb_masks(q_rows):
    row = lax.broadcasted_iota(jnp.int32, (SB_TILE, SB_TILE), 0)
    col = lax.broadcasted_iota(jnp.int32, (SB_TILE, SB_TILE), 1)
    lane = lax.broadcasted_iota(jnp.int32, (q_rows, LANES), 1)
    return row, col, lane < SB_HEAD_DIM


def _stack_heads(x, first):
    zero = jnp.zeros_like(x)
    return jnp.concatenate([jnp.where(first, x, zero), jnp.where(first, zero, x)], axis=0)


def _stack_heads_t(x_t):
    first_t = lax.broadcasted_iota(jnp.int32, x_t.shape, 0) < SB_HEAD_DIM
    zero = jnp.zeros_like(x_t)
    return jnp.concatenate([jnp.where(first_t, x_t, zero), jnp.where(first_t, zero, x_t)], axis=1).astype(BF16)


def _unstack_heads(x2, first):
    half = x2.shape[0] // 2
    return jnp.where(first, x2[:half], x2[half:])


def _live_rows(x, s, q_rows):
    if s == 0:
        return x
    return jnp.concatenate([x[s * SB_TILE:q_rows], x[q_rows + s * SB_TILE:]], axis=0)


def _spread_rows(x, s, q_rows):
    if s == 0:
        return x
    half = q_rows - s * SB_TILE
    zero = jnp.zeros((s * SB_TILE,) + x.shape[1:], x.dtype)
    return jnp.concatenate([zero, x[:half], zero, x[half:]], axis=0)


def _stacked_col_minus_row(q_rows):
    row = lax.broadcasted_iota(jnp.int32, (2 * q_rows, SB_TILE), 0)
    col = lax.broadcasted_iota(jnp.int32, (2 * q_rows, SB_TILE), 1)
    return col - (row & (q_rows - 1))


def _head_mean(x, first):
    s0 = jnp.sum(jnp.where(first, x, 0.0), axis=-1, keepdims=True)
    s1 = jnp.sum(jnp.where(first, 0.0, x), axis=-1, keepdims=True)
    return jnp.where(first, s0, s1) * (1.0 / SB_HEAD_DIM)


def _riding(exchange, refs, n_in, n_out, n_scratch):
    n_x = exchange.n if exchange is not None else 0
    ins, rest = refs[:n_in], refs[n_in:]
    x_src, rest = rest[:n_x], rest[n_x:]
    outs, rest = rest[:n_out], rest[n_out:]
    x_dst, rest = rest[:n_x], rest[n_x:]
    return ins, outs, rest[:n_scratch], (x_src, x_dst, rest[n_scratch:])


def _riding_specs(exchange):
    if exchange is None:
        return [], [], [], [], []
    return exchange.in_specs, exchange.out_specs, exchange.out_shape, exchange.scratch, exchange.arrays


def _ride(exchange, x_refs, first_step, last_step):
    if exchange is None:
        return lambda: None

    @pl.when(first_step)
    def _():
        exchange.start(*x_refs)

    def finish():
        @pl.when(last_step)
        def _():
            exchange.wait(*x_refs)

    return finish


def _sb_fwd(proj, merged_a, hg, *, batch, seq, exchange=None):
    q0, k0, v0 = 2 * GM_WIDTH // LANES, 2 * GM_WIDTH // LANES + SB_PAIRS, 2 * GM_WIDTH // LANES + 2 * SB_PAIRS
    q_rows = block_keys = min(SB_BLOCK, seq)
    assert seq % block_keys == 0
    n_q, n_sub, n_blocks = seq // q_rows, block_keys // SB_TILE, seq // block_keys

    def body(*refs):
        (q_ref, k_ref, v_ref, hg_ref, _), (m_ref, raw_ref, a_ref, e_ref), _, x_refs = _riding(exchange, refs, 5, 4, 0)
        b, p, i = pl.program_id(0), pl.program_id(1), pl.program_id(2)
        finish = _ride(exchange, x_refs, (b == 0) & (p == 0) & (i == 0), (b == batch - 1) & (p == SB_PAIRS - 1) & (i == n_q - 1))
        row, col, first = _sb_masks(q_rows)
        upper = (row > col).astype(BF16)
        q2 = _stack_heads((q_ref[...].astype(F32) * SB_SCALE).astype(BF16), first)
        diff = _stacked_col_minus_row(q_rows)
        last = i

        def key_tile(jb, s):
            return k_ref[pl.ds(pl.multiple_of((jb * n_sub + s) * SB_TILE, SB_TILE), SB_TILE), :]

        def scores(jb):
            return tuple(_nt(q2, key_tile(jb, s)) for s in range(n_sub))

        def keep_for_backward(ref, jb, s, stacked):
            cols = slice(s * SB_TILE, (s + 1) * SB_TILE)
            ref[0, 0, jb, 0, :, cols] = stacked[:q_rows]
            ref[0, 0, jb, 1, :, cols] = stacked[q_rows:]

        def weights_of(jb, z, right):
            logits = [_sb_logits(z[s], None) for s in range(n_sub)]
            totals = [jnp.sum(l1m, axis=-1, keepdims=True) for _, l1m, _ in logits]
            sums = [_tri_sums(l1m, upper) for _, l1m, _ in logits]
            weights = [None] * n_sub
            for s in reversed(range(n_sub)):
                weights[s] = _sb_weights(logits[s][0], sums[s], right, None).astype(BF16)
                right = right + totals[s]
                keep_for_backward(a_ref, jb, s, weights[s])
                keep_for_backward(e_ref, jb, s, logits[s][2].astype(BF16))
            return right, jnp.concatenate(weights, axis=1)

        def diagonal_weights():
            keeps = [_live_rows(diff, s, q_rows) < -s * SB_TILE for s in range(n_sub)]
            logits = [_sb_logits(_nt(_live_rows(q2, s, q_rows), key_tile(last, s)), keeps[s]) for s in range(n_sub)]
            totals = [jnp.sum(l1m, axis=-1, keepdims=True) for _, l1m, _ in logits]
            sums = [_tri_sums(l1m, upper) for _, l1m, _ in logits]
            right = jnp.zeros((2 * q_rows, 1), F32)
            weights = [None] * n_sub
            for s in reversed(range(n_sub)):
                live = _sb_weights(logits[s][0], sums[s], _live_rows(right, s, q_rows), keeps[s]).astype(BF16)
                weights[s] = _spread_rows(live, s, q_rows)
                right = right + _spread_rows(totals[s], s, q_rows)
                keep_for_backward(a_ref, last, s, weights[s])
                keep_for_backward(e_ref, last, s, _spread_rows(logits[s][2].astype(BF16), s, q_rows))
            return right, jnp.concatenate(weights, axis=1)

        def values(jb):
            return v_ref[pl.ds(pl.multiple_of(jb * block_keys, block_keys), block_keys), :]

        right, a_prev = diagonal_weights()
        z_next = scores(jnp.maximum(last - 1, 0))

        def step(k, carry):
            right, acc, z, a_prev = carry
            jb = last - k
            acc = acc + _nn(a_prev, values(jb + 1))
            z_next = scores(jnp.maximum(jb - 1, 0))
            right, a = weights_of(jb, z, right)
            return right, acc, z_next, a

        _, acc2, _, a_prev = lax.fori_loop(1, last + 1, step, (right, jnp.zeros((2 * q_rows, LANES), F32), z_next, a_prev))
        acc = _unstack_heads(acc2 + _nn(a_prev, values(0)), first)
        raw_ref[...] = acc
        m_ref[...] = (acc * lax.rsqrt(_head_mean(acc * acc, first) + EPS) * hg_ref[...]).astype(BF16)
        finish()

    t = batch * seq
    blk = lambda c0: pl.BlockSpec((q_rows, LANES), lambda b, p, i: (b * n_q + i, c0 + p))
    kv = lambda c0: pl.BlockSpec((seq, LANES), lambda b, p, i: (b, c0 + p))
    kept = pl.BlockSpec((1, 1, n_blocks, 2, q_rows, block_keys), lambda b, p, i: (p, b, 0, 0, i, 0))
    kept_shape = jax.ShapeDtypeStruct((SB_PAIRS, batch, n_blocks, 2, seq, block_keys), BF16)
    x_in, x_out, x_shape, x_scratch, x_arrays = _riding_specs(exchange)
    res = pl.pallas_call(
        body, name="sb_fwd", grid=(batch, SB_PAIRS, n_q),
        in_specs=[blk(q0), kv(k0), kv(v0), pl.BlockSpec((1, LANES), lambda b, p, i: (0, SB_PAIRS + p)),
                  pl.BlockSpec(memory_space=pl.ANY)] + x_in,
        out_specs=[blk(SB_PAIRS), blk(0), kept, kept] + x_out,
        out_shape=[jax.ShapeDtypeStruct((t, 2 * GM_WIDTH), BF16), jax.ShapeDtypeStruct((t, GM_WIDTH), F32), kept_shape, kept_shape] + x_shape,
        scratch_shapes=x_scratch,
        input_output_aliases={4: 0},
        compiler_params=_params(3),
    )(proj, proj, proj, hg, merged_a, *x_arrays)
    return res[0], res[1], res[2], res[3], res[4:]


def _sb_bwd(proj, raw, weights, signed_e, dmerged, hg, *, batch, seq, exchange=None):
    q0, k0, v0 = 2 * GM_WIDTH // LANES, 2 * GM_WIDTH // LANES + SB_PAIRS, 2 * GM_WIDTH // LANES + 2 * SB_PAIRS
    q_rows = block_keys = min(SB_BLOCK, seq)
    assert seq % block_keys == 0
    n_q, n_sub, n_blocks = seq // q_rows, block_keys // SB_TILE, seq // block_keys

    def body(*refs):
        ins, outs, (dk_acc, dv_acc), x_refs = _riding(exchange, refs, 8, 4, 2)
        q_ref, k_ref, v_ref, raw_ref, a_ref, e_ref, dm_ref, hg_ref = ins
        dq_ref, dk_ref, dv_ref, dhg_ref = outs
        p, b, i = pl.program_id(0), pl.program_id(1), pl.program_id(2)
        finish = _ride(exchange, x_refs, (b == 0) & (p == 0) & (i == 0), (b == batch - 1) & (p == SB_PAIRS - 1) & (i == n_q - 1))
        row, col, first = _sb_masks(q_rows)
        lower = (row < col).astype(BF16)

        @pl.when(jnp.logical_and(b == 0, i == 0))
        def _():
            dhg_ref[...] = jnp.zeros_like(dhg_ref)

        @pl.when(i == 0)
        def _():
            dk_acc[...] = jnp.zeros_like(dk_acc)
            dv_acc[...] = jnp.zeros_like(dv_acc)

        raw_v = raw_ref[...]
        dm = dm_ref[...].astype(F32)
        r = lax.rsqrt(_head_mean(raw_v * raw_v, first) + EPS)
        nrm = raw_v * r
        dhg_ref[...] += jnp.sum(dm * nrm, axis=0, keepdims=True)
        dn = dm * hg_ref[...]
        dout = r * (dn - nrm * _head_mean(dn * nrm, first))
        dout2 = _stack_heads(dout.astype(BF16), first)
        q2_t = _stack_heads_t(q_ref[...].astype(F32).T)
        dout2_t = _stack_heads_t(dout.T)
        diff = _stacked_col_minus_row(q_rows)
        last = i

        def kept(ref, jb, cols):
            return jnp.concatenate([ref[0, 0, jb, 0, :, cols], ref[0, 0, jb, 1, :, cols]], axis=0)

        def block(jb, carry, diagonal):
            gleft, dq = carry
            live = (lambda x, s: _live_rows(x, s, q_rows)) if diagonal else (lambda x, s: x)
            spread = (lambda x, s: _spread_rows(x, s, q_rows)) if diagonal else (lambda x, s: x)
            tiles = [pl.ds(pl.multiple_of((jb * n_sub + s) * SB_TILE, SB_TILE), SB_TILE) for s in range(n_sub)]
            cols = [slice(s * SB_TILE, (s + 1) * SB_TILE) for s in range(n_sub)]
            gmats = [_nt(live(dout2, s), v_ref[tiles[s], :]) * live(kept(a_ref, jb, cols[s]), s).astype(F32) for s in range(n_sub)]
            prefixes = [_tri_sums(g, lower) for g in gmats]
            dzs = []
            for s in range(n_sub):
                signed = live(kept(e_ref, jb, cols[s]), s).astype(F32)
                e = jnp.abs(signed)
                inv = 1.0 / (1.0 + e)
                small = e * inv
                positive = signed > 0.0
                beta, one_minus = jnp.where(positive, inv, small), jnp.where(positive, small, inv)
                dz = (gmats[s] * one_minus - (live(gleft, s) + prefixes[s]) * beta) * SB_SCALE
                gleft = gleft + spread(jnp.sum(gmats[s], axis=-1, keepdims=True), s)
                if diagonal:
                    dz = jnp.where(live(diff, s) < -s * SB_TILE, dz, 0.0)
                dzs.append(spread(dz.astype(BF16), s))
            dz_all = jnp.concatenate(dzs, axis=1)
            dk_acc[jb] += _nn(q2_t, dz_all)
            dv_acc[jb] += _nn(dout2_t, kept(a_ref, jb, slice(None)))
            return gleft, dq + _nn(dz_all, k_ref[pl.ds(pl.multiple_of(jb * block_keys, block_keys), block_keys), :])

        carry = (jnp.zeros((2 * q_rows, 1), F32), jnp.zeros((2 * q_rows, LANES), F32))
        carry = lax.fori_loop(0, last, lambda jb, c: block(jb, c, False), carry)
        dq_ref[...] = _unstack_heads(block(last, carry, True)[1], first).astype(BF16)

        @pl.when(i == n_q - 1)
        def _():
            for jb in range(n_blocks):
                for s in range(n_sub):
                    rows = slice((jb * n_sub + s) * SB_TILE, (jb * n_sub + s + 1) * SB_TILE)
                    cols = slice(s * SB_TILE, (s + 1) * SB_TILE)
                    dk_ref[rows, :] = dk_acc[jb, :, cols].T.astype(BF16)
                    dv_ref[rows, :] = dv_acc[jb, :, cols].T.astype(BF16)

        finish()

    t = batch * seq
    blk = lambda c0: pl.BlockSpec((q_rows, LANES), lambda p, b, i: (b * n_q + i, c0 + p))
    kv = lambda c0: pl.BlockSpec((seq, LANES), lambda p, b, i: (b, c0 + p))
    row_spec = pl.BlockSpec((1, LANES), lambda p, b, i: (0, SB_PAIRS + p))
    kept_spec = pl.BlockSpec((1, 1, n_blocks, 2, q_rows, block_keys), lambda p, b, i: (p, b, 0, 0, i, 0))
    x_in, x_out, x_shape, x_scratch, x_arrays = _riding_specs(exchange)
    res = pl.pallas_call(
        body, name="sb_bwd", grid=(SB_PAIRS, batch, n_q),
        in_specs=[blk(q0), kv(k0), kv(v0), blk(0), kept_spec, kept_spec, blk(SB_PAIRS), row_spec] + x_in,
        out_specs=[blk(0), kv(0), kv(0), pl.BlockSpec((1, LANES), lambda p, b, i: (0, p))] + x_out,
        out_shape=[jax.ShapeDtypeStruct((t, GM_WIDTH), BF16)] * 3 + [jax.ShapeDtypeStruct((1, GM_WIDTH), F32)] + x_shape,
        scratch_shapes=[pltpu.VMEM((n_blocks, LANES, block_keys), F32), pltpu.VMEM((n_blocks, LANES, block_keys), F32)] + x_scratch,
        compiler_params=_params(3),
    )(proj, proj, proj, raw, weights, signed_e, dmerged, hg, *x_arrays)
    return res[0], res[1], res[2], res[3], res[4:]


def _x_softmax(qh, kh):
    s = _nt(qh, kh) * X_SCALE
    p = jnp.exp(s - jnp.max(s, axis=-1, keepdims=True))
    return p / jnp.sum(p, axis=-1, keepdims=True)


def _xattn_fwd(q, kv, *, batch, seq, n_mem, tq=512):
    tq = min(tq, seq)
    n_q = seq // tq
    d = X_HEADS * X_HEAD_DIM

    def body(q_ref, kv_ref, o_ref):
        for h in range(X_HEADS):
            cols = slice(h * X_HEAD_DIM, (h + 1) * X_HEAD_DIM)
            p = _x_softmax(q_ref[:, cols], kv_ref[:, cols])
            o_ref[:, cols] = _nn(p.astype(BF16), kv_ref[:, d + h * X_HEAD_DIM:d + (h + 1) * X_HEAD_DIM]).astype(BF16)

    return pl.pallas_call(
        body, name="xattn_fwd", grid=(batch, n_q),
        in_specs=[pl.BlockSpec((tq, d), lambda b, i: (b * n_q + i, 0)), pl.BlockSpec((n_mem, 2 * d), lambda b, i: (b, 0))],
        out_specs=pl.BlockSpec((tq, d), lambda b, i: (b * n_q + i, 0)),
        out_shape=jax.ShapeDtypeStruct((batch * seq, d), BF16),
        compiler_params=_params(2),
    )(q, kv)


def _xattn_bwd(q, kv, do, *, batch, seq, n_mem, tq=512):
    tq = min(tq, seq)
    n_q = seq // tq
    d = X_HEADS * X_HEAD_DIM

    def body(q_ref, kv_ref, do_ref, dq_ref, dkv_ref, acc):
        i = pl.program_id(1)

        @pl.when(i == 0)
        def _():
            acc[...] = jnp.zeros_like(acc)

        for h in range(X_HEADS):
            cols = slice(h * X_HEAD_DIM, (h + 1) * X_HEAD_DIM)
            vcols = slice(d + h * X_HEAD_DIM, d + (h + 1) * X_HEAD_DIM)
            qh, kh, vh, doh = q_ref[:, cols], kv_ref[:, cols], kv_ref[:, vcols], do_ref[:, cols]
            p = _x_softmax(qh, kh)
            dp = _nt(doh, vh)
            acc[:, vcols] += _tn(p.astype(BF16), doh)
            ds = (p * (dp - jnp.sum(dp * p, axis=-1, keepdims=True)) * X_SCALE).astype(BF16)
            dq_ref[:, cols] = _nn(ds, kh).astype(BF16)
            acc[:, cols] += _tn(ds, qh)

        @pl.when(i == n_q - 1)
        def _():
            dkv_ref[...] = acc[...].astype(BF16)

    return pl.pallas_call(
        body, name="xattn_bwd", grid=(batch, n_q),
        in_specs=[pl.BlockSpec((tq, d), lambda b, i: (b * n_q + i, 0)), pl.BlockSpec((n_mem, 2 * d), lambda b, i: (b, 0)),
                  pl.BlockSpec((tq, d), lambda b, i: (b * n_q + i, 0))],
        out_specs=[pl.BlockSpec((tq, d), lambda b, i: (b * n_q + i, 0)), pl.BlockSpec((n_mem, 2 * d), lambda b, i: (b, 0))],
        out_shape=[jax.ShapeDtypeStruct((batch * seq, d), BF16), jax.ShapeDtypeStruct((batch * n_mem, 2 * d), BF16)],
        scratch_shapes=[pltpu.VMEM((n_mem, 2 * d), F32)],
        compiler_params=_params(2),
    )(q, kv, do)


def _my_index():
    return 4 * lax.axis_index("x") + 2 * lax.axis_index("y") + lax.axis_index("c")


def _peers():
    x, y, c = lax.axis_index("x"), lax.axis_index("y"), lax.axis_index("c")
    out = []
    for rel in range(1, N_DEV):
        dx, dy, dc = (rel >> 2) & 1, (rel >> 1) & 1, rel & 1
        px, py, pc = x ^ dx, y ^ dy, c ^ dc
        out.append(((px, py, pc), 4 * px + 2 * py + pc))
    return out


class _Exchange:
    def __init__(self, arrays, scatter):
        self.arrays, self.scatter, self.n = list(arrays), scatter, len(arrays)
        any_spec = pl.BlockSpec(memory_space=pl.ANY)
        self.in_specs = [any_spec] * self.n
        self.out_specs = [any_spec] * self.n
        self.out_shape = [jax.ShapeDtypeStruct((N_DEV,) + tuple(a.shape[-2:]), a.dtype) for a in self.arrays]
        n_peer = N_DEV - 1
        self.scratch = [pltpu.SemaphoreType.DMA((self.n, n_peer)), pltpu.SemaphoreType.DMA((self.n, n_peer)),
                        pltpu.SemaphoreType.DMA((self.n,))]

    def _copies(self, srcs, dsts, sems, arriving):
        send_sems, recv_sems, local_sems = sems
        me = _my_index()
        local, remote = [], []
        for w in range(self.n):
            if not arriving:
                local.append(pltpu.make_async_copy(srcs[w].at[me] if self.scatter else srcs[w], dsts[w].at[me], local_sems.at[w]))
            for rel, (pos, idx) in enumerate(_peers()):
                remote.append(pltpu.make_async_remote_copy(
                    src_ref=srcs[w].at[idx] if self.scatter else srcs[w], dst_ref=dsts[w].at[idx if arriving else me],
                    send_sem=send_sems.at[w, rel], recv_sem=recv_sems.at[w, rel], device_id=pos, device_id_type=MESH))
        return local, remote

    def start(self, srcs, dsts, sems):
        local, sends = self._copies(srcs, dsts, sems, arriving=False)
        for cp in local + sends:
            cp.start()

    def wait(self, srcs, dsts, sems):
        for cp in self._copies(srcs, dsts, sems, arriving=True)[1]:
            cp.wait_recv()
        local, sends = self._copies(srcs, dsts, sems, arriving=False)
        for cp in sends:
            cp.wait_send()
        for cp in local:
            cp.wait()


def _gather_two_level(shard, *, name):
    def body(x_ref, out_ref, send_sems, recv_sems, local_sem):
        x, y, c = lax.axis_index("x"), lax.axis_index("y"), lax.axis_index("c")
        me, sibling = (x, y, c), (x, y, 1 - c)
        chips = [(1 - x, y), (x, 1 - y), (1 - x, 1 - y)]

        def slot(px, py, pc):
            return out_ref.at[4 * px + 2 * py + pc]

        def copy(k, block, to, src=None):
            return pltpu.make_async_remote_copy(src_ref=slot(*block) if src is None else src, dst_ref=slot(*block),
                                                send_sem=send_sems.at[k], recv_sem=recv_sems.at[k], device_id=to, device_id_type=MESH)

        mine = pltpu.make_async_copy(x_ref, slot(*me), local_sem)
        mine.start()
        first = [copy(0, me, sibling, src=x_ref)] + [copy(1 + j, me, (*chip, c), src=x_ref) for j, chip in enumerate(chips)]
        for cp in first:
            cp.start()
        passed = [copy(4 + j, (*chip, c), sibling) for j, chip in enumerate(chips)]
        for j, chip in enumerate(chips):
            copy(1 + j, (*chip, c), me).wait_recv()
            passed[j].start()
        copy(0, sibling, me).wait_recv()
        for j, chip in enumerate(chips):
            copy(4 + j, (*chip, 1 - c), me).wait_recv()
        for cp in first + passed:
            cp.wait_send()
        mine.wait()

    any_spec = pl.BlockSpec(memory_space=pl.ANY)
    return pl.pallas_call(
        body, name=name, in_specs=[any_spec], out_specs=any_spec,
        out_shape=jax.ShapeDtypeStruct((N_DEV,) + shard.shape, shard.dtype),
        scratch_shapes=[pltpu.SemaphoreType.DMA((N_DEV - 1,)), pltpu.SemaphoreType.DMA((N_DEV - 1,)), pltpu.SemaphoreType.DMA(())],
    )(shard)


def _all_reduce_small(part, *, loss_rows, loss_scale):
    rows = part.shape[0]
    n_peer = N_DEV - 1

    def body(p_ref, o_ref, buf, send_sems, recv_sems):
        me = _my_index()
        peers = _peers()
        buf[me] = p_ref[...]
        sends = [pltpu.make_async_remote_copy(src_ref=p_ref, dst_ref=buf.at[me], send_sem=send_sems.at[rel], recv_sem=recv_sems.at[rel],
                                              device_id=peers[rel][0], device_id_type=MESH) for rel in range(n_peer)]
        for cp in sends:
            cp.start()
        for rel in range(n_peer):
            pltpu.make_async_remote_copy(src_ref=p_ref, dst_ref=buf.at[peers[rel][1]], send_sem=send_sems.at[rel], recv_sem=recv_sems.at[rel],
                                         device_id=peers[rel][0], device_id_type=MESH).wait_recv()
        for cp in sends:
            cp.wait_send()
        total = buf[0]
        for dev in range(1, N_DEV):
            total = total + buf[dev]
        o_ref[...] = total
        squares = total[rows - loss_rows:]
        loss = jnp.sum(jnp.sum(squares, axis=0, keepdims=True), axis=-1, keepdims=True) * loss_scale
        o_ref[rows - loss_rows:, :] = jnp.broadcast_to(loss, (loss_rows, LANES))

    vmem = pl.BlockSpec(memory_space=pltpu.VMEM)
    return pl.pallas_call(
        body, name="all_reduce_small", in_specs=[vmem], out_specs=vmem, out_shape=jax.ShapeDtypeStruct(part.shape, F32),
        scratch_shapes=[pltpu.VMEM((N_DEV, rows, LANES), F32), pltpu.SemaphoreType.DMA((n_peer,)), pltpu.SemaphoreType.DMA((n_peer,))],
        compiler_params=pltpu.CompilerParams(has_side_effects=True, vmem_limit_bytes=VMEM_LIMIT),
    )(part)


def _adamw_math(w, g, m, v):
    m_new = ADAM_B1 * m + (1.0 - ADAM_B1) * g
    v_new = ADAM_B2 * v + (1.0 - ADAM_B2) * (g * g)
    m_hat = m_new / (1.0 - ADAM_B1 ** ADAM_STEP)
    v_hat = v_new / (1.0 - ADAM_B2 ** ADAM_STEP)
    delta = -ADAM_LR * (m_hat / (jnp.sqrt(v_hat) + ADAM_EPS) + ADAM_WD * w)
    return delta, m_new, v_new


def _adamw(parts, w, m, v, *, name, tr=64):
    rows, cols = w.shape
    tr = min(tr, rows)
    assert rows % tr == 0
    stacked = parts.ndim == 3

    def body(p_ref, w_ref, m_ref, v_ref, g_ref, d_ref, mo_ref, vo_ref):
        if stacked:
            g = p_ref[0].astype(F32)
            for dev in range(1, N_DEV):
                g = g + p_ref[dev].astype(F32)
        else:
            g = p_ref[...]
        delta, m_new, v_new = _adamw_math(w_ref[...], g, m_ref[...], v_ref[...])
        g_ref[...] = g
        d_ref[...] = delta
        mo_ref[...] = m_new
        vo_ref[...] = v_new

    tile = pl.BlockSpec((tr, cols), lambda i: (i, 0))
    p_spec = pl.BlockSpec((N_DEV, tr, cols), lambda i: (0, i, 0)) if stacked else tile
    return pl.pallas_call(
        body, name=name, grid=(rows // tr,), in_specs=[p_spec, tile, tile, tile], out_specs=[tile] * 4,
        out_shape=[jax.ShapeDtypeStruct((rows, cols), F32)] * 4, compiler_params=_params(1),
    )(parts, w, m, v)


_LATER = ("w_out", "w_cq", "w_ckv", "w_co", "w_ff1", "w_ff2")


def _as_rows(stacked):
    return stacked.reshape(-1, stacked.shape[-1])


def _local_step(x, mem, target, small, shards):
    batch, seq, d = x.shape
    n_mem = mem.shape[1]
    t = batch * seq
    x2, mem2, tgt2 = x.reshape(t, d), mem.reshape(batch * n_mem, d), target.reshape(t, d)
    g_mix, g_cross, g_mem, g_ffn, g_final = (small[k] for k in ("norm_mix_g", "norm_cross_g", "norm_mem_g", "norm_ffn_g", "norm_final_g"))
    gv, hg, w_sp, b_sp_t = small["gm_v_norm_g"], small["head_norm_g"], small["w_spatial"], small["b_spatial_t"]

    win_t = _as_rows(_gather_two_level(shards["w_in"], name="gather_w_in"))
    proj, xn = _norm_mm(x2, g_mix, win_t, mode="nt", name="proj_fwd", tm=512, tn=win_t.shape[0])
    merged_a = _gmlp_fwd(proj, w_sp, b_sp_t, gv, hg)
    merged, sb_raw, sb_weights, sb_signed_e, gathered = _sb_fwd(proj, merged_a, hg, batch=batch, seq=seq,
                                                                exchange=_Exchange([shards[n] for n in _LATER], scatter=False))
    wout, wcq, wckv_t, wco, wff1_t, wff2 = (_as_rows(g) for g in gathered)
    h1 = _mm(merged, wout, mode="nn", out_dtype=F32, name="mix_out_fwd", epi=_epi_residual, epi_ins=(x2,))
    qx, hn1 = _norm_mm(h1, g_cross, wcq, mode="nn", name="xq_fwd")
    kvx, memn = _norm_mm(mem2, g_mem, wckv_t, mode="nt", name="xkv_fwd", tm=512, tn=wckv_t.shape[0])
    o = _xattn_fwd(qx, kvx, batch=batch, seq=seq, n_mem=n_mem)
    h2 = _mm(o, wco, mode="nn", out_dtype=F32, name="xo_fwd", epi=_epi_residual, epi_ins=(h1,))
    fpre, hn2 = _norm_mm(h2, g_ffn, wff1_t, mode="nt", name="ff1_fwd", tm=512, tn=wff1_t.shape[0])
    dh3, final_rows = _mm(fpre, wff2, mode="nn", out_dtype=F32, name="ff2_fwd_loss", tm=512, tk=wff2.shape[0], a_fn=_relu2,
                          epi=_epi_loss, epi_ins=(h2, tgt2), vec_ins=(g_final,), aux=2)
    dg_final, sq_err = final_rows[0:1], final_rows[1:2]

    dpre = _mm(dh3, wff2, mode="nt", out_dtype=BF16, name="ff2_bwd_x", tm=512, tn=wff2.shape[0], epi=_epi_relu2_grad,
               epi_ins=(fpre,))
    d_wff2 = _mm(fpre, dh3, mode="tn", out_dtype=BF16, name="ff2_bwd_w", tm=2048, a_fn=_relu2)
    d_wff1_t = _mm(dpre, hn2, mode="tn", out_dtype=BF16, name="ff1_bwd_w", tm=2048)
    dh2, dg_ffn = _mm(dpre, wff1_t, mode="nn", out_dtype=F32, name="ff1_bwd_x", tm=512, tk=wff1_t.shape[0], epi=_epi_rms_bwd,
                      epi_ins=(h2, dh3), vec_ins=(g_ffn,), aux=True)
    do = _mm(dh2, wco, mode="nt", out_dtype=BF16, name="xo_bwd_x")
    d_wco = _mm(o, dh2, mode="tn", out_dtype=BF16, name="xo_bwd_w")
    dqx, dkvx = _xattn_bwd(qx, kvx, do, batch=batch, seq=seq, n_mem=n_mem)
    d_wcq = _mm(hn1, dqx, mode="tn", out_dtype=BF16, name="xq_bwd_w")
    dh1, dg_cross = _mm(dqx, wcq, mode="nt", out_dtype=F32, name="xq_bwd_x", tm=512, epi=_epi_rms_bwd,
                        epi_ins=(h1, dh2), vec_ins=(g_cross,), aux=True)
    d_wckv_t = _mm(dkvx, memn, mode="tn", out_dtype=BF16, name="xkv_bwd_w")
    _, dg_mem = _mm(dkvx, wckv_t, mode="nn", out_dtype=BF16, name="xkv_bwd_x", tm=512, epi=_epi_rms_gain_only,
                    epi_ins=(mem2,), vec_ins=(g_mem,), aux=True)
    dmerged = _mm(dh1, wout, mode="nt", out_dtype=BF16, name="mix_out_bwd_x")
    d_wout = _mm(merged, dh1, mode="tn", out_dtype=BF16, name="mix_out_bwd_w")
    dp_a, d_wsp, d_bsp_t, d_gv, d_hg_a = _gmlp_bwd(proj, dmerged, w_sp, b_sp_t, gv, hg)
    d_later = {"w_out": d_wout, "w_cq": d_wcq, "w_ckv": d_wckv_t, "w_co": d_wco, "w_ff1": d_wff1_t, "w_ff2": d_wff2}
    scatter = _Exchange([d_later[n].reshape(N_DEV, -1, d) for n in _LATER], scatter=True)
    dq, dk, dv, d_hg_b, received = _sb_bwd(proj, sb_raw, sb_weights, sb_signed_e, dmerged, hg, batch=batch, seq=seq, exchange=scatter)
    dproj = jnp.concatenate([dp_a, dq, dk, dv], axis=1)
    d_win_t = _mm(dproj, xn, mode="tn", out_dtype=BF16, name="proj_bwd_w", tm=512)
    dx, dg_mix, d_win_received = _mm(dproj, win_t, mode="nn", out_dtype=F32, name="proj_bwd_x", tm=512, tk=win_t.shape[0],
                                     epi=_epi_rms_bwd, epi_ins=(x2, dh1), vec_ins=(g_mix,), aux=True,
                                     exchange=_Exchange([d_win_t.reshape(N_DEV, -1, d)], scatter=True))

    d_small = {"norm_mix_g": dg_mix, "gm_v_norm_g": d_gv, "w_spatial": d_wsp, "b_spatial_t": d_bsp_t, "head_norm_g": jnp.concatenate([d_hg_a, d_hg_b], axis=1),
               "norm_cross_g": dg_cross, "norm_mem_g": dg_mem, "norm_ffn_g": dg_ffn, "norm_final_g": dg_final}
    d_big = dict(zip(_LATER, received))
    d_big["w_in"] = d_win_received
    return sq_err, dx.reshape(batch, seq, d), d_small, d_big


_BIG = ("w_in", "w_out", "w_cq", "w_ckv", "w_co", "w_ff1", "w_ff2")
_BIG_TRANSPOSED = ("w_in", "w_ckv", "w_ff1")
_SMALL = ("norm_mix_g", "gm_v_norm_g", "w_spatial", "b_spatial", "head_norm_g", "norm_cross_g", "norm_mem_g", "norm_ffn_g", "norm_final_g")
_NAMES = ("norm_mix_g", "w_in", "gm_v_norm_g", "w_spatial", "b_spatial", "head_norm_g", "w_out", "norm_cross_g", "norm_mem_g",
          "w_cq", "w_ckv", "w_co", "norm_ffn_g", "w_ff1", "w_ff2", "norm_final_g")


def _rows_of(a):
    r = a.reshape(-1, LANES)
    pad = (-r.shape[0]) % 8
    return jnp.pad(r, ((0, pad), (0, 0))) if pad else r


def _shard2d(name, a):
    a = a[0]
    return a.T if name in _BIG_TRANSPOSED else a


def kernel(x, mem, norm_mix_g, w_in, gm_v_norm_g, w_spatial, b_spatial, head_norm_g, w_out, norm_cross_g, norm_mem_g, w_cq, w_ckv, w_co, norm_ffn_g, w_ff1, w_ff2, norm_final_g, loss_target, m_norm_mix_g, m_w_in, m_gm_v_norm_g, m_w_spatial, m_b_spatial, m_head_norm_g, m_w_out, m_norm_cross_g, m_norm_mem_g, m_w_cq, m_w_ckv, m_w_co, m_norm_ffn_g, m_w_ff1, m_w_ff2, m_norm_final_g, v_norm_mix_g, v_w_in, v_gm_v_norm_g, v_w_spatial, v_b_spatial, v_head_norm_g, v_w_out, v_norm_cross_g, v_norm_mem_g, v_w_cq, v_w_ckv, v_w_co, v_norm_ffn_g, v_w_ff1, v_w_ff2, v_norm_final_g):
    weights = dict(norm_mix_g=norm_mix_g, w_in=w_in, gm_v_norm_g=gm_v_norm_g, w_spatial=w_spatial, b_spatial=b_spatial,
                   head_norm_g=head_norm_g, w_out=w_out, norm_cross_g=norm_cross_g, norm_mem_g=norm_mem_g, w_cq=w_cq, w_ckv=w_ckv,
                   w_co=w_co, norm_ffn_g=norm_ffn_g, w_ff1=w_ff1, w_ff2=w_ff2, norm_final_g=norm_final_g)
    mom1 = dict(norm_mix_g=m_norm_mix_g, w_in=m_w_in, gm_v_norm_g=m_gm_v_norm_g, w_spatial=m_w_spatial, b_spatial=m_b_spatial,
                head_norm_g=m_head_norm_g, w_out=m_w_out, norm_cross_g=m_norm_cross_g, norm_mem_g=m_norm_mem_g, w_cq=m_w_cq,
                w_ckv=m_w_ckv, w_co=m_w_co, norm_ffn_g=m_norm_ffn_g, w_ff1=m_w_ff1, w_ff2=m_w_ff2, norm_final_g=m_norm_final_g)
    mom2 = dict(norm_mix_g=v_norm_mix_g, w_in=v_w_in, gm_v_norm_g=v_gm_v_norm_g, w_spatial=v_w_spatial, b_spatial=v_b_spatial,
                head_norm_g=v_head_norm_g, w_out=v_w_out, norm_cross_g=v_norm_cross_g, norm_mem_g=v_norm_mem_g, w_cq=v_w_cq,
                w_ckv=v_w_ckv, w_co=v_w_co, norm_ffn_g=v_norm_ffn_g, w_ff1=v_w_ff1, w_ff2=v_w_ff2, norm_final_g=v_norm_final_g)

    shards = {n: _shard2d(n, weights[n]).astype(BF16) for n in _BIG}
    small = {n: weights[n].reshape(1, -1) for n in _SMALL if n not in ("w_spatial", "b_spatial")}
    small["w_spatial"] = w_spatial[0]
    small["b_spatial_t"] = b_spatial[0].T
    sq_err, grad_x, d_small, d_big = _local_step(x, mem, loss_target, small, shards)

    d_small["b_spatial"] = d_small.pop("b_spatial_t").T
    sq_rows = _rows_of(sq_err)
    packed = jnp.concatenate([_rows_of(d_small[n]) for n in _SMALL] + [sq_rows], axis=0)
    summed = _all_reduce_small(packed, loss_rows=sq_rows.shape[0], loss_scale=0.5 / x.shape[-1])

    grads, deltas, new_m, new_v = {}, {}, {}, {}
    for n in _BIG:
        outs = _adamw(d_big[n], _shard2d(n, weights[n]), _shard2d(n, mom1[n]), _shard2d(n, mom2[n]), name="adamw_" + n)
        outs = [o.T if n in _BIG_TRANSPOSED else o for o in outs]
        grads[n], deltas[n], new_m[n], new_v[n] = (o[None] for o in outs)
    pack = lambda src: jnp.concatenate([_rows_of(src[n]) for n in _SMALL], axis=0)
    n_small_rows = sum(_rows_of(weights[n]).shape[0] for n in _SMALL)
    outs = _adamw(summed[:n_small_rows], pack(weights), pack(mom1), pack(mom2), name="adamw_small", tr=n_small_rows)
    at = 0
    for n in _SMALL:
        used = weights[n].size // LANES
        for dst, o in zip((grads, deltas, new_m, new_v), outs):
            dst[n] = o[at:at + used].reshape(weights[n].shape)
        at += _rows_of(weights[n]).shape[0]
    loss = summed[n_small_rows, 0]
    return (loss, grad_x, *[grads[n] for n in _NAMES], *[deltas[n] for n in _NAMES], *[new_m[n] for n in _NAMES],
            *[new_v[n] for n in _NAMES])
```

```python
import math

import jax
import jax.numpy as jnp
from jax import lax
from jax.experimental import pallas as pl
from jax.experimental.pallas import tpu as pltpu

F32 = jnp.float32
BF16 = jnp.bfloat16
EPS = 1e-6
N_DEV = 8
LANES = 128
CHUNK = 128
GM_GROUPS = 4
GM_WIDTH = 512
SB_PAIRS = 4
SB_HEAD_DIM = 64
SB_SCALE = 0.125
SB_TILE = 128
SB_BLOCK = 512
X_HEADS = 4
X_HEAD_DIM = 256
X_SCALE = 1.0 / 16.0
VMEM_LIMIT = 56 * 1024 * 1024
ADAM_LR, ADAM_B1, ADAM_B2, ADAM_EPS, ADAM_WD, ADAM_STEP = 0.001, 0.9, 0.999, 1e-08, 0.01, 10
MESH = pl.DeviceIdType.MESH


def _params(n_axes):
    return pltpu.CompilerParams(dimension_semantics=("arbitrary",) * n_axes, vmem_limit_bytes=VMEM_LIMIT)


def _dot(a, b, dims):
    return lax.dot_general(a, b, (dims, ((), ())), preferred_element_type=F32)


def _nn(a, b):
    return _dot(a, b, ((1,), (0,)))


def _nt(a, b):
    return _dot(a, b, ((1,), (1,)))


def _tn(a, b):
    return _dot(a, b, ((0,), (0,)))


_MODES = {"nn": _nn, "nt": _nt, "tn": _tn}


def _rstd(x):
    return lax.rsqrt(jnp.mean(x * x, axis=-1, keepdims=True) + EPS)


def _gelu(x):
    c = math.sqrt(2.0 / math.pi)
    t = jnp.tanh(c * (x + 0.044715 * x * x * x))
    return 0.5 * x * (1.0 + t)


def _gelu_and_grad(x):
    c = math.sqrt(2.0 / math.pi)
    t = jnp.tanh(c * (x + 0.044715 * x * x * x))
    half = 0.5 * (1.0 + t)
    return x * half, half + 0.5 * x * (1.0 - t * t) * c * (1.0 + 3 * 0.044715 * x * x)


def _split_bf16(x):
    hi = x.astype(BF16)
    lo = (x - hi.astype(F32)).astype(BF16)
    return hi, lo


def _mm(a, b, *, mode, out_dtype, name, tm=1024, tn=1024, tk=1024, a_fn=None, b_fn=None, epi=None, epi_ins=(), vec_ins=(),
        aux=False, col_chunk=None, exchange=None):
    if mode == "nn":
        (m, k), (k2, n) = a.shape, b.shape
    elif mode == "nt":
        (m, k), (n, k2) = a.shape, b.shape
    else:
        (k, m), (k2, n) = a.shape, b.shape
    assert k == k2, (a.shape, b.shape, mode)
    tm, tn, tk = min(tm, m), min(tn, n), min(tk, k)
    assert m % tm == 0 and n % tn == 0 and k % tk == 0, (m, n, k, tm, tn, tk)
    n_m, n_n, n_k = m // tm, n // tn, k // tk
    assert not aux or n_n == 1
    dot = _MODES[mode]
    n_epi, n_vec = len(epi_ins), len(vec_ins)

    def body(*refs):
        ins, outs, scratch, x_refs = _riding(exchange, refs, 2 + n_epi + n_vec, 2 if aux else 1, 1 if n_k > 1 else 0)
        a_ref, b_ref, epi_refs = ins[0], ins[1], ins[2:]
        o_ref = outs[0]
        aux_ref = outs[1] if aux else None
        acc_ref = scratch[0] if n_k > 1 else None
        i, j, kk = pl.program_id(0), pl.program_id(1), pl.program_id(2)
        ride_done = _ride(exchange, x_refs, (i == 0) & (j == 0) & (kk == 0), (i == n_m - 1) & (j == n_n - 1) & (kk == n_k - 1))
        def product():
            av, bv = a_ref[...], b_ref[...]
            if a_fn is not None:
                av = a_fn(av)
            if b_fn is not None:
                bv = b_fn(bv)
            return dot(av.astype(BF16), bv.astype(BF16))

        def finish(acc):
            if col_chunk is not None:
                for ch in range(tn // col_chunk):
                    o_ref[ch] = acc[:, ch * col_chunk:(ch + 1) * col_chunk].astype(out_dtype)
                return
            if epi is None:
                o_ref[...] = acc.astype(out_dtype)
                return
            res = epi(acc, *[r[...] for r in epi_refs])
            if aux:
                res, rows = res[0], res[1:]
                rows = rows[0] if len(rows) == 1 else jnp.concatenate(rows, axis=0)

                @pl.when(i == 0)
                def _():
                    aux_ref[...] = rows

                @pl.when(i != 0)
                def _():
                    aux_ref[...] += rows
            o_ref[...] = res.astype(out_dtype)

        if n_k == 1:
            finish(product())
        else:
            @pl.when(kk == 0)
            def _():
                acc_ref[...] = product()

            @pl.when(kk != 0)
            def _():
                acc_ref[...] += product()

            @pl.when(kk == n_k - 1)
            def _():
                finish(acc_ref[...])

        ride_done()

    if mode == "tn":
        a_spec = pl.BlockSpec((tk, tm), lambda i, j, kk: (kk, i))
    else:
        a_spec = pl.BlockSpec((tm, tk), lambda i, j, kk: (i, kk))
    if mode == "nt":
        b_spec = pl.BlockSpec((tn, tk), lambda i, j, kk: (j, kk))
    else:
        b_spec = pl.BlockSpec((tk, tn), lambda i, j, kk: (kk, j))
    tile_spec = pl.BlockSpec((tm, tn), lambda i, j, kk: (i, j))
    row_spec = pl.BlockSpec((1, tn), lambda i, j, kk: (0, j))
    out_shape = [jax.ShapeDtypeStruct((m, n), out_dtype)]
    out_specs = [tile_spec]
    if col_chunk is not None:
        assert epi is None and not aux and tn % col_chunk == 0
        out_shape = [jax.ShapeDtypeStruct((n // col_chunk, m, col_chunk), out_dtype)]
        out_specs = [pl.BlockSpec((tn // col_chunk, tm, col_chunk), lambda i, j, kk: (j, i, 0))]
    if aux:
        out_shape.append(jax.ShapeDtypeStruct((int(aux), n), F32))
        out_specs.append(pl.BlockSpec((int(aux), tn), lambda i, j, kk: (0, j)))
    x_in, x_out, x_shape, x_scratch, x_arrays = _riding_specs(exchange)
    res = pl.pallas_call(
        body, name=name, grid=(n_m, n_n, n_k),
        in_specs=[a_spec, b_spec] + [tile_spec] * n_epi + [row_spec] * n_vec + x_in,
        out_specs=out_specs + x_out, out_shape=out_shape + x_shape,
        scratch_shapes=([pltpu.VMEM((tm, tn), F32)] if n_k > 1 else []) + x_scratch,
        compiler_params=_params(3),
    )(a, b, *epi_ins, *vec_ins, *x_arrays)
    if exchange is not None:
        return tuple(res)
    return res if aux else res[0]


def _norm_mm(x, g, w, *, mode, name, tm=1024, tn=1024):
    m, d = x.shape
    n = w.shape[0] if mode == "nt" else w.shape[1]
    tm, tn = min(tm, m), min(tn, n)
    assert m % tm == 0 and n % tn == 0
    dot = _MODES[mode]

    def body(x_ref, g_ref, w_ref, o_ref, xn_ref, xn_s):
        @pl.when(pl.program_id(1) == 0)
        def _():
            xv = x_ref[...]
            xn = (xv * _rstd(xv) * g_ref[...]).astype(BF16)
            xn_s[...] = xn
            xn_ref[...] = xn

        o_ref[...] = dot(xn_s[...], w_ref[...]).astype(BF16)

    w_spec = pl.BlockSpec((tn, d), lambda i, j: (j, 0)) if mode == "nt" else pl.BlockSpec((d, tn), lambda i, j: (0, j))
    return pl.pallas_call(
        body, name=name, grid=(m // tm, n // tn),
        in_specs=[pl.BlockSpec((tm, d), lambda i, j: (i, 0)), pl.BlockSpec((1, d), lambda i, j: (0, 0)), w_spec],
        out_specs=[pl.BlockSpec((tm, tn), lambda i, j: (i, j)), pl.BlockSpec((tm, d), lambda i, j: (i, 0))],
        out_shape=[jax.ShapeDtypeStruct((m, n), BF16), jax.ShapeDtypeStruct((m, d), BF16)],
        scratch_shapes=[pltpu.VMEM((tm, d), BF16)],
        compiler_params=_params(2),
    )(x, g, w)


def _epi_residual(acc, res):
    return res + acc


def _epi_relu2_grad(acc, pre):
    return acc * (2.0 * jnp.maximum(pre.astype(F32), 0.0))


def _relu2(pre):
    r = jnp.maximum(pre.astype(F32), 0.0)
    return r * r


def _epi_rms_bwd(acc, h, dres, g):
    r = _rstd(h)
    xh = h * r
    dxh = acc * g
    dh = dres + r * (dxh - xh * jnp.mean(dxh * xh, axis=-1, keepdims=True))
    return dh, jnp.sum(acc * xh, axis=0, keepdims=True)


def _epi_loss(acc, h_in, target, g):
    h = h_in + acc
    r = _rstd(h)
    xh = h * r
    err = xh * g - target
    dy = err * (1.0 / h.shape[-1])
    dxh = dy * g
    dh = r * (dxh - xh * jnp.mean(dxh * xh, axis=-1, keepdims=True))
    return dh, jnp.sum(dy * xh, axis=0, keepdims=True), jnp.sum(err * err, axis=0, keepdims=True)


def _epi_rms_gain_only(acc, h, g):
    return acc, jnp.sum(acc * (h * _rstd(h)), axis=0, keepdims=True)


def _tril(n):
    row = lax.broadcasted_iota(jnp.int32, (n, n), 0)
    col = lax.broadcasted_iota(jnp.int32, (n, n), 1)
    return col <= row


def _gmlp_fwd(proj, w_sp, b_sp_t, gv, hg, *, rows=512):
    t = proj.shape[0]
    rows = min(rows, t)
    n_c = rows // CHUNK

    def body(u_ref, v_ref, w_ref, bt_ref, gv_ref, hg_ref, m_ref):
        keep = _tril(CHUNK)
        for g in range(GM_GROUPS):
            cols = slice(g * LANES, (g + 1) * LANES)
            wg = jnp.where(keep, w_ref[g], 0.0).astype(BF16)
            u = _gelu(u_ref[:, cols].astype(F32))
            v = _gelu(v_ref[:, cols].astype(F32))
            vn = (v * _rstd(v) * gv_ref[:, cols]).astype(BF16)
            bias = bt_ref[:, g:g + 1]
            for c in range(n_c):
                rs = slice(c * CHUNK, (c + 1) * CHUNK)
                mixed = _nn(wg, vn[rs]) + bias
                a = u[rs] * mixed
                m_ref[rs, cols] = (a * _rstd(a) * hg_ref[:, cols]).astype(BF16)

    full = lambda shape: pl.BlockSpec(shape, lambda i: (0,) * len(shape))
    return pl.pallas_call(
        body, name="gmlp_fwd", grid=(t // rows,),
        in_specs=[pl.BlockSpec((rows, GM_WIDTH), lambda i: (i, 0)), pl.BlockSpec((rows, GM_WIDTH), lambda i: (i, 1)),
                  full((GM_GROUPS, CHUNK, CHUNK)), full((CHUNK, GM_GROUPS)), full((1, GM_WIDTH)), full((1, GM_WIDTH))],
        out_specs=pl.BlockSpec((rows, GM_WIDTH), lambda i: (i, 0)),
        out_shape=jax.ShapeDtypeStruct((t, 2 * GM_WIDTH), BF16),
        compiler_params=_params(1),
    )(proj, proj, w_sp, b_sp_t, gv, hg)


def _gmlp_bwd(proj, dmerged, w_sp, b_sp_t, gv, hg, *, rows=512):
    t = proj.shape[0]
    rows = min(rows, t)
    n_c = rows // CHUNK
    n_steps = t // rows

    def body(u_ref, v_ref, dm_ref, w_ref, bt_ref, gv_ref, hg_ref, dp_ref, dw_ref, dbt_ref, dgv_ref, dhg_ref, db_acc):
        step = pl.program_id(0)
        keep = _tril(CHUNK)

        @pl.when(step == 0)
        def _():
            dw_ref[...] = jnp.zeros_like(dw_ref)
            db_acc[...] = jnp.zeros_like(db_acc)
            dgv_ref[...] = jnp.zeros_like(dgv_ref)
            dhg_ref[...] = jnp.zeros_like(dhg_ref)

        for g in range(GM_GROUPS):
            cols = slice(g * LANES, (g + 1) * LANES)
            wg = jnp.where(keep, w_ref[g], 0.0).astype(BF16)
            u, u_slope = _gelu_and_grad(u_ref[:, cols].astype(F32))
            v, v_slope = _gelu_and_grad(v_ref[:, cols].astype(F32))
            r = _rstd(v)
            xh = v * r
            gvg = gv_ref[:, cols]
            hgg = hg_ref[:, cols]
            vn = (xh * gvg).astype(BF16)
            bias = bt_ref[:, g:g + 1]
            dm = dm_ref[:, cols].astype(F32)
            du_parts, dvn_parts = [], []
            dw = jnp.zeros((CHUNK, CHUNK), F32)
            db = jnp.zeros((CHUNK, LANES), F32)
            dhg = jnp.zeros((1, LANES), F32)
            for c in range(n_c):
                rs = slice(c * CHUNK, (c + 1) * CHUNK)
                mixed = _nn(wg, vn[rs]) + bias
                a = u[rs] * mixed
                ra = _rstd(a)
                an = a * ra
                dhg = dhg + jnp.sum(dm[rs] * an, axis=0, keepdims=True)
                dan = dm[rs] * hgg
                da = ra * (dan - an * jnp.mean(dan * an, axis=-1, keepdims=True))
                du_parts.append(da * mixed)
                dmixed = da * u[rs]
                db = db + dmixed
                dmb = dmixed.astype(BF16)
                dw = dw + _nt(dmb, vn[rs])
                dvn_parts.append(_tn(wg, dmb))
            du = jnp.concatenate(du_parts, axis=0)
            dvn = jnp.concatenate(dvn_parts, axis=0)
            dw_ref[g] += dw
            db_acc[g] += db
            dhg_ref[:, cols] += dhg
            dgv_ref[:, cols] += jnp.sum(dvn * xh, axis=0, keepdims=True)
            dxh = dvn * gvg
            dv = r * (dxh - xh * jnp.mean(dxh * xh, axis=-1, keepdims=True))
            dp_ref[:, cols] = (du * u_slope).astype(BF16)
            dp_ref[:, GM_WIDTH + g * LANES:GM_WIDTH + (g + 1) * LANES] = (dv * v_slope).astype(BF16)

        @pl.when(step == n_steps - 1)
        def _():
            for g in range(GM_GROUPS):
                dw_ref[g] = jnp.where(keep, dw_ref[g], 0.0)
                dbt_ref[:, g:g + 1] = jnp.sum(db_acc[g], axis=-1, keepdims=True)

    full = lambda shape: pl.BlockSpec(shape, lambda i: (0,) * len(shape))
    return pl.pallas_call(
        body, name="gmlp_bwd", grid=(n_steps,),
        in_specs=[pl.BlockSpec((rows, GM_WIDTH), lambda i: (i, 0)), pl.BlockSpec((rows, GM_WIDTH), lambda i: (i, 1)),
                  pl.BlockSpec((rows, GM_WIDTH), lambda i: (i, 0)),
                  full((GM_GROUPS, CHUNK, CHUNK)), full((CHUNK, GM_GROUPS)), full((1, GM_WIDTH)), full((1, GM_WIDTH))],
        out_specs=[pl.BlockSpec((rows, 2 * GM_WIDTH), lambda i: (i, 0)), full((GM_GROUPS, CHUNK, CHUNK)),
                   full((CHUNK, GM_GROUPS)), full((1, GM_WIDTH)), full((1, GM_WIDTH))],
        out_shape=[jax.ShapeDtypeStruct((t, 2 * GM_WIDTH), BF16), jax.ShapeDtypeStruct((GM_GROUPS, CHUNK, CHUNK), F32),
                   jax.ShapeDtypeStruct((CHUNK, GM_GROUPS), F32), jax.ShapeDtypeStruct((1, GM_WIDTH), F32),
                   jax.ShapeDtypeStruct((1, GM_WIDTH), F32)],
        scratch_shapes=[pltpu.VMEM((GM_GROUPS, CHUNK, LANES), F32)],
        compiler_params=_params(1),
    )(proj, proj, dmerged, w_sp, b_sp_t, gv, hg)


def _sb_logits(z, strict):
    e = jnp.exp(-jnp.abs(z))
    ls = jnp.minimum(z, 0.0) - jnp.log(1.0 + e)
    l1m = ls - z
    if strict is not None:
        l1m = jnp.where(strict, l1m, 0.0)
    return ls, l1m, jnp.where(z >= 0.0, e, -e)


def _tri_sums(x, tri):
    hi, lo = _split_bf16(x)
    return _nn(jnp.concatenate([hi, lo], axis=1), jnp.concatenate([tri, tri], axis=0))


def _sb_weights(ls, in_tile, right, strict):
    a = jnp.exp(ls + in_tile + right)
    if strict is not None:
        a = jnp.where(strict, a, 0.0)
    return a


def _sb_masks(q_rows):
    row = lax.broadcasted_iota(jnp.int32, (SB_TILE, SB_TILE), 0)
    col = lax.broadcasted_iota(jnp.int32, (SB_TILE, SB_TILE), 1)
    lane = lax.broadcasted_iota(jnp.int32, (q_rows, LANES), 1)
    return row, col, lane < SB_HEAD_DIM


def _stack_heads(x, first):
    zero = jnp.zeros_like(x)
    return jnp.concatenate([jnp.where(first, x, zero), jnp.where(first, zero, x)], axis=0)


def _stack_heads_t(x_t):
    first_t = lax.broadcasted_iota(jnp.int32, x_t.shape, 0) < SB_HEAD_DIM
    zero = jnp.zeros_like(x_t)
    return jnp.concatenate([jnp.where(first_t, x_t, zero), jnp.where(first_t, zero, x_t)], axis=1).astype(BF16)


def _unstack_heads(x2, first):
    half = x2.shape[0] // 2
    return jnp.where(first, x2[:half], x2[half:])


def _live_rows(x, s, q_rows):
    if s == 0:
        return x
    return jnp.concatenate([x[s * SB_TILE:q_rows], x[q_rows + s * SB_TILE:]], axis=0)


def _spread_rows(x, s, q_rows):
    if s == 0:
        return x
    half = q_rows - s * SB_TILE
    zero = jnp.zeros((s * SB_TILE,) + x.shape[1:], x.dtype)
    return jnp.concatenate([zero, x[:half], zero, x[half:]], axis=0)


def _stacked_col_minus_row(q_rows):
    row = lax.broadcasted_iota(jnp.int32, (2 * q_rows, SB_TILE), 0)
    col = lax.broadcasted_iota(jnp.int32, (2 * q_rows, SB_TILE), 1)
    return col - (row & (q_rows - 1))


def _head_mean(x, first):
    s0 = jnp.sum(jnp.where(first, x, 0.0), axis=-1, keepdims=True)
    s1 = jnp.sum(jnp.where(first, 0.0, x), axis=-1, keepdims=True)
    return jnp.where(first, s0, s1) * (1.0 / SB_HEAD_DIM)


def _riding(exchange, refs, n_in, n_out, n_scratch):
    n_x = exchange.n if exchange is not None else 0
    ins, rest = refs[:n_in], refs[n_in:]
    x_src, rest = rest[:n_x], rest[n_x:]
    outs, rest = rest[:n_out], rest[n_out:]
    x_dst, rest = rest[:n_x], rest[n_x:]
    return ins, outs, rest[:n_scratch], (x_src, x_dst, rest[n_scratch:])


def _riding_specs(exchange):
    if exchange is None:
        return [], [], [], [], []
    return exchange.in_specs, exchange.out_specs, exchange.out_shape, exchange.scratch, exchange.arrays


def _ride(exchange, x_refs, first_step, last_step):
    if exchange is None:
        return lambda: None

    @pl.when(first_step)
    def _():
        exchange.start(*x_refs)

    def finish():
        @pl.when(last_step)
        def _():
            exchange.wait(*x_refs)

    return finish


def _sb_fwd(proj, merged_a, hg, *, batch, seq, exchange=None):
    q0, k0, v0 = 2 * GM_WIDTH // LANES, 2 * GM_WIDTH // LANES + SB_PAIRS, 2 * GM_WIDTH // LANES + 2 * SB_PAIRS
    q_rows = block_keys = min(SB_BLOCK, seq)
    assert seq % block_keys == 0
    n_q, n_sub, n_blocks = seq // q_rows, block_keys // SB_TILE, seq // block_keys

    def body(*refs):
        (q_ref, k_ref, v_ref, hg_ref, _), (m_ref, raw_ref, a_ref, e_ref), _, x_refs = _riding(exchange, refs, 5, 4, 0)
        b, p, i = pl.program_id(0), pl.program_id(1), pl.program_id(2)
        finish = _ride(exchange, x_refs, (b == 0) & (p == 0) & (i == 0), (b == batch - 1) & (p == SB_PAIRS - 1) & (i == n_q - 1))
        row, col, first = _sb_masks(q_rows)
        upper = (row > col).astype(BF16)
        q2 = _stack_heads((q_ref[...].astype(F32) * SB_SCALE).astype(BF16), first)
        diff = _stacked_col_minus_row(q_rows)
        last = i

        def key_tile(jb, s):
            return k_ref[pl.ds(pl.multiple_of((jb * n_sub + s) * SB_TILE, SB_TILE), SB_TILE), :]

        def scores(jb):
            return tuple(_nt(q2, key_tile(jb, s)) for s in range(n_sub))

        def keep_for_backward(ref, jb, s, stacked):
            cols = slice(s * SB_TILE, (s + 1) * SB_TILE)
            ref[0, 0, jb, 0, :, cols] = stacked[:q_rows]
            ref[0, 0, jb, 1, :, cols] = stacked[q_rows:]

        def weights_of(jb, z, right):
            logits = [_sb_logits(z[s], None) for s in range(n_sub)]
            totals = [jnp.sum(l1m, axis=-1, keepdims=True) for _, l1m, _ in logits]
            sums = [_tri_sums(l1m, upper) for _, l1m, _ in logits]
            weights = [None] * n_sub
            for s in reversed(range(n_sub)):
                weights[s] = _sb_weights(logits[s][0], sums[s], right, None).astype(BF16)
                right = right + totals[s]
                keep_for_backward(a_ref, jb, s, weights[s])
                keep_for_backward(e_ref, jb, s, logits[s][2].astype(BF16))
            return right

        def diagonal_weights():
            keeps = [_live_rows(diff, s, q_rows) < -s * SB_TILE for s in range(n_sub)]
            logits = [_sb_logits(_nt(_live_rows(q2, s, q_rows), key_tile(last, s)), keeps[s]) for s in range(n_sub)]
            totals = [jnp.sum(l1m, axis=-1, keepdims=True) for _, l1m, _ in logits]
            sums = [_tri_sums(l1m, upper) for _, l1m, _ in logits]
            right = jnp.zeros((2 * q_rows, 1), F32)
            weights = [None] * n_sub
            for s in reversed(range(n_sub)):
                live = _sb_weights(logits[s][0], sums[s], _live_rows(right, s, q_rows), keeps[s]).astype(BF16)
                weights[s] = _spread_rows(live, s, q_rows)
                right = right + _spread_rows(totals[s], s, q_rows)
                keep_for_backward(a_ref, last, s, weights[s])
                keep_for_backward(e_ref, last, s, _spread_rows(logits[s][2].astype(BF16), s, q_rows))
            return right

        def weighted_values(jb):
            weights = jnp.concatenate([a_ref[0, 0, jb, 0], a_ref[0, 0, jb, 1]], axis=0)
            return _nn(weights, v_ref[pl.ds(pl.multiple_of(jb * block_keys, block_keys), block_keys), :])

        right = diagonal_weights()
        z_next = scores(jnp.maximum(last - 1, 0))

        def step(k, carry):
            right, acc, z = carry
            jb = last - k
            acc = acc + weighted_values(jb + 1)
            z_next = scores(jnp.maximum(jb - 1, 0))
            return weights_of(jb, z, right), acc, z_next

        _, acc2, _ = lax.fori_loop(1, last + 1, step, (right, jnp.zeros((2 * q_rows, LANES), F32), z_next))
        acc = _unstack_heads(acc2 + weighted_values(0), first)
        raw_ref[...] = acc
        m_ref[...] = (acc * lax.rsqrt(_head_mean(acc * acc, first) + EPS) * hg_ref[...]).astype(BF16)
        finish()

    t = batch * seq
    blk = lambda c0: pl.BlockSpec((q_rows, LANES), lambda b, p, i: (b * n_q + i, c0 + p))
    kv = lambda c0: pl.BlockSpec((seq, LANES), lambda b, p, i: (b, c0 + p))
    kept = pl.BlockSpec((1, 1, n_blocks, 2, q_rows, block_keys), lambda b, p, i: (p, b, 0, 0, i, 0))
    kept_shape = jax.ShapeDtypeStruct((SB_PAIRS, batch, n_blocks, 2, seq, block_keys), BF16)
    x_in, x_out, x_shape, x_scratch, x_arrays = _riding_specs(exchange)
    res = pl.pallas_call(
        body, name="sb_fwd", grid=(batch, SB_PAIRS, n_q),
        in_specs=[blk(q0), kv(k0), kv(v0), pl.BlockSpec((1, LANES), lambda b, p, i: (0, SB_PAIRS + p)),
                  pl.BlockSpec(memory_space=pl.ANY)] + x_in,
        out_specs=[blk(SB_PAIRS), blk(0), kept, kept] + x_out,
        out_shape=[jax.ShapeDtypeStruct((t, 2 * GM_WIDTH), BF16), jax.ShapeDtypeStruct((t, GM_WIDTH), F32), kept_shape, kept_shape] + x_shape,
        scratch_shapes=x_scratch,
        input_output_aliases={4: 0},
        compiler_params=_params(3),
    )(proj, proj, proj, hg, merged_a, *x_arrays)
    return res[0], res[1], res[2], res[3], res[4:]


def _sb_bwd(proj, raw, weights, signed_e, dmerged, hg, *, batch, seq, exchange=None):
    q0, k0, v0 = 2 * GM_WIDTH // LANES, 2 * GM_WIDTH // LANES + SB_PAIRS, 2 * GM_WIDTH // LANES + 2 * SB_PAIRS
    q_rows = block_keys = min(SB_BLOCK, seq)
    assert seq % block_keys == 0
    n_q, n_sub, n_blocks = seq // q_rows, block_keys // SB_TILE, seq // block_keys

    def body(*refs):
        ins, outs, (dk_acc, dv_acc), x_refs = _riding(exchange, refs, 8, 4, 2)
        q_ref, k_ref, v_ref, raw_ref, a_ref, e_ref, dm_ref, hg_ref = ins
        dq_ref, dk_ref, dv_ref, dhg_ref = outs
        p, b, i = pl.program_id(0), pl.program_id(1), pl.program_id(2)
        finish = _ride(exchange, x_refs, (b == 0) & (p == 0) & (i == 0), (b == batch - 1) & (p == SB_PAIRS - 1) & (i == n_q - 1))
        row, col, first = _sb_masks(q_rows)
        lower = (row < col).astype(BF16)

        @pl.when(jnp.logical_and(b == 0, i == 0))
        def _():
            dhg_ref[...] = jnp.zeros_like(dhg_ref)

        @pl.when(i == 0)
        def _():
            dk_acc[...] = jnp.zeros_like(dk_acc)
            dv_acc[...] = jnp.zeros_like(dv_acc)

        raw_v = raw_ref[...]
        dm = dm_ref[...].astype(F32)
        r = lax.rsqrt(_head_mean(raw_v * raw_v, first) + EPS)
        nrm = raw_v * r
        dhg_ref[...] += jnp.sum(dm * nrm, axis=0, keepdims=True)
        dn = dm * hg_ref[...]
        dout = r * (dn - nrm * _head_mean(dn * nrm, first))
        dout2 = _stack_heads(dout.astype(BF16), first)
        q2_t = _stack_heads_t(q_ref[...].astype(F32).T)
        dout2_t = _stack_heads_t(dout.T)
        diff = _stacked_col_minus_row(q_rows)
        last = i

        def kept(ref, jb, cols):
            return jnp.concatenate([ref[0, 0, jb, 0, :, cols], ref[0, 0, jb, 1, :, cols]], axis=0)

        def block(jb, carry, diagonal):
            gleft, dq = carry
            live = (lambda x, s: _live_rows(x, s, q_rows)) if diagonal else (lambda x, s: x)
            spread = (lambda x, s: _spread_rows(x, s, q_rows)) if diagonal else (lambda x, s: x)
            tiles = [pl.ds(pl.multiple_of((jb * n_sub + s) * SB_TILE, SB_TILE), SB_TILE) for s in range(n_sub)]
            cols = [slice(s * SB_TILE, (s + 1) * SB_TILE) for s in range(n_sub)]
            gmats = [_nt(live(dout2, s), v_ref[tiles[s], :]) * live(kept(a_ref, jb, cols[s]), s).astype(F32) for s in range(n_sub)]
            prefixes = [_tri_sums(g, lower) for g in gmats]
            dzs = []
            for s in range(n_sub):
                signed = live(kept(e_ref, jb, cols[s]), s).astype(F32)
                e = jnp.abs(signed)
                inv = 1.0 / (1.0 + e)
                small = e * inv
                positive = signed > 0.0
                beta, one_minus = jnp.where(positive, inv, small), jnp.where(positive, small, inv)
                dz = (gmats[s] * one_minus - (live(gleft, s) + prefixes[s]) * beta) * SB_SCALE
                gleft = gleft + spread(jnp.sum(gmats[s], axis=-1, keepdims=True), s)
                if diagonal:
                    dz = jnp.where(live(diff, s) < -s * SB_TILE, dz, 0.0)
                dzs.append(spread(dz.astype(BF16), s))
            dz_all = jnp.concatenate(dzs, axis=1)
            dk_acc[jb] += _nn(q2_t, dz_all)
            dv_acc[jb] += _nn(dout2_t, kept(a_ref, jb, slice(None)))
            return gleft, dq + _nn(dz_all, k_ref[pl.ds(pl.multiple_of(jb * block_keys, block_keys), block_keys), :])

        carry = (jnp.zeros((2 * q_rows, 1), F32), jnp.zeros((2 * q_rows, LANES), F32))
        carry = lax.fori_loop(0, last, lambda jb, c: block(jb, c, False), carry)
        dq_ref[...] = _unstack_heads(block(last, carry, True)[1], first).astype(BF16)

        @pl.when(i == n_q - 1)
        def _():
            for jb in range(n_blocks):
                for s in range(n_sub):
                    rows = slice((jb * n_sub + s) * SB_TILE, (jb * n_sub + s + 1) * SB_TILE)
                    cols = slice(s * SB_TILE, (s + 1) * SB_TILE)
                    dk_ref[rows, :] = dk_acc[jb, :, cols].T.astype(BF16)
                    dv_ref[rows, :] = dv_acc[jb, :, cols].T.astype(BF16)

        finish()

    t = batch * seq
    blk = lambda c0: pl.BlockSpec((q_rows, LANES), lambda p, b, i: (b * n_q + i, c0 + p))
    kv = lambda c0: pl.BlockSpec((seq, LANES), lambda p, b, i: (b, c0 + p))
    row_spec = pl.BlockSpec((1, LANES), lambda p, b, i: (0, SB_PAIRS + p))
    kept_spec = pl.BlockSpec((1, 1, n_blocks, 2, q_rows, block_keys), lambda p, b, i: (p, b, 0, 0, i, 0))
    x_in, x_out, x_shape, x_scratch, x_arrays = _riding_specs(exchange)
    res = pl.pallas_call(
        body, name="sb_bwd", grid=(SB_PAIRS, batch, n_q),
        in_specs=[blk(q0), kv(k0), kv(v0), blk(0), kept_spec, kept_spec, blk(SB_PAIRS), row_spec] + x_in,
        out_specs=[blk(0), kv(0), kv(0), pl.BlockSpec((1, LANES), lambda p, b, i: (0, p))] + x_out,
        out_shape=[jax.ShapeDtypeStruct((t, GM_WIDTH), BF16)] * 3 + [jax.ShapeDtypeStruct((1, GM_WIDTH), F32)] + x_shape,
        scratch_shapes=[pltpu.VMEM((n_blocks, LANES, block_keys), F32), pltpu.VMEM((n_blocks, LANES, block_keys), F32)] + x_scratch,
        compiler_params=_params(3),
    )(proj, proj, proj, raw, weights, signed_e, dmerged, hg, *x_arrays)
    return res[0], res[1], res[2], res[3], res[4:]


def _x_softmax(qh, kh):
    s = _nt(qh, kh) * X_SCALE
    p = jnp.exp(s - jnp.max(s, axis=-1, keepdims=True))
    return p * (1.0 / jnp.sum(p, axis=-1, keepdims=True))


def _xattn_fwd(q, kv, *, batch, seq, n_mem, tq=512):
    tq = min(tq, seq)
    n_q = seq // tq
    d = X_HEADS * X_HEAD_DIM

    def body(q_ref, kv_ref, o_ref):
        for h in range(X_HEADS):
            cols = slice(h * X_HEAD_DIM, (h + 1) * X_HEAD_DIM)
            p = _x_softmax(q_ref[:, cols], kv_ref[:, cols])
            o_ref[:, cols] = _nn(p.astype(BF16), kv_ref[:, d + h * X_HEAD_DIM:d + (h + 1) * X_HEAD_DIM]).astype(BF16)

    return pl.pallas_call(
        body, name="xattn_fwd", grid=(batch, n_q),
        in_specs=[pl.BlockSpec((tq, d), lambda b, i: (b * n_q + i, 0)), pl.BlockSpec((n_mem, 2 * d), lambda b, i: (b, 0))],
        out_specs=pl.BlockSpec((tq, d), lambda b, i: (b * n_q + i, 0)),
        out_shape=jax.ShapeDtypeStruct((batch * seq, d), BF16),
        compiler_params=_params(2),
    )(q, kv)


def _xattn_bwd(q, kv, do, *, batch, seq, n_mem, tq=512):
    tq = min(tq, seq)
    n_q = seq // tq
    d = X_HEADS * X_HEAD_DIM

    def body(q_ref, kv_ref, do_ref, dq_ref, dkv_ref, acc):
        i = pl.program_id(1)

        @pl.when(i == 0)
        def _():
            acc[...] = jnp.zeros_like(acc)

        for h in range(X_HEADS):
            cols = slice(h * X_HEAD_DIM, (h + 1) * X_HEAD_DIM)
            vcols = slice(d + h * X_HEAD_DIM, d + (h + 1) * X_HEAD_DIM)
            qh, kh, vh, doh = q_ref[:, cols], kv_ref[:, cols], kv_ref[:, vcols], do_ref[:, cols]
            p = _x_softmax(qh, kh)
            dp = _nt(doh, vh)
            acc[:, vcols] += _tn(p.astype(BF16), doh)
            ds = (p * (dp - jnp.sum(dp * p, axis=-1, keepdims=True)) * X_SCALE).astype(BF16)
            dq_ref[:, cols] = _nn(ds, kh).astype(BF16)
            acc[:, cols] += _tn(ds, qh)

        @pl.when(i == n_q - 1)
        def _():
            dkv_ref[...] = acc[...].astype(BF16)

    return pl.pallas_call(
        body, name="xattn_bwd", grid=(batch, n_q),
        in_specs=[pl.BlockSpec((tq, d), lambda b, i: (b * n_q + i, 0)), pl.BlockSpec((n_mem, 2 * d), lambda b, i: (b, 0)),
                  pl.BlockSpec((tq, d), lambda b, i: (b * n_q + i, 0))],
        out_specs=[pl.BlockSpec((tq, d), lambda b, i: (b * n_q + i, 0)), pl.BlockSpec((n_mem, 2 * d), lambda b, i: (b, 0))],
        out_shape=[jax.ShapeDtypeStruct((batch * seq, d), BF16), jax.ShapeDtypeStruct((batch * n_mem, 2 * d), BF16)],
        scratch_shapes=[pltpu.VMEM((n_mem, 2 * d), F32)],
        compiler_params=_params(2),
    )(q, kv, do)


def _my_index():
    return 4 * lax.axis_index("x") + 2 * lax.axis_index("y") + lax.axis_index("c")


def _peers():
    x, y, c = lax.axis_index("x"), lax.axis_index("y"), lax.axis_index("c")
    out = []
    for rel in range(1, N_DEV):
        dx, dy, dc = (rel >> 2) & 1, (rel >> 1) & 1, rel & 1
        px, py, pc = x ^ dx, y ^ dy, c ^ dc
        out.append(((px, py, pc), 4 * px + 2 * py + pc))
    return out


class _Exchange:
    def __init__(self, arrays, scatter):
        self.arrays, self.scatter, self.n = list(arrays), scatter, len(arrays)
        any_spec = pl.BlockSpec(memory_space=pl.ANY)
        self.in_specs = [any_spec] * self.n
        self.out_specs = [any_spec] * self.n
        self.out_shape = [jax.ShapeDtypeStruct((N_DEV,) + tuple(a.shape[-2:]), a.dtype) for a in self.arrays]
        n_peer = N_DEV - 1
        self.scratch = [pltpu.SemaphoreType.DMA((self.n, n_peer)), pltpu.SemaphoreType.DMA((self.n, n_peer)),
                        pltpu.SemaphoreType.DMA((self.n,))]

    def _copies(self, srcs, dsts, sems, arriving):
        send_sems, recv_sems, local_sems = sems
        me = _my_index()
        local, remote = [], []
        for w in range(self.n):
            if not arriving:
                local.append(pltpu.make_async_copy(srcs[w].at[me] if self.scatter else srcs[w], dsts[w].at[me], local_sems.at[w]))
            for rel, (pos, idx) in enumerate(_peers()):
                remote.append(pltpu.make_async_remote_copy(
                    src_ref=srcs[w].at[idx] if self.scatter else srcs[w], dst_ref=dsts[w].at[idx if arriving else me],
                    send_sem=send_sems.at[w, rel], recv_sem=recv_sems.at[w, rel], device_id=pos, device_id_type=MESH))
        return local, remote

    def start(self, srcs, dsts, sems):
        local, sends = self._copies(srcs, dsts, sems, arriving=False)
        for cp in local + sends:
            cp.start()

    def wait(self, srcs, dsts, sems):
        for cp in self._copies(srcs, dsts, sems, arriving=True)[1]:
            cp.wait_recv()
        local, sends = self._copies(srcs, dsts, sems, arriving=False)
        for cp in sends:
            cp.wait_send()
        for cp in local:
            cp.wait()


def _gather_two_level(shard, *, name):
    def body(x_ref, out_ref, send_sems, recv_sems, local_sem):
        x, y, c = lax.axis_index("x"), lax.axis_index("y"), lax.axis_index("c")
        me, sibling = (x, y, c), (x, y, 1 - c)
        chips = [(1 - x, y), (x, 1 - y), (1 - x, 1 - y)]

        def slot(px, py, pc):
            return out_ref.at[4 * px + 2 * py + pc]

        def copy(k, block, to, src=None):
            return pltpu.make_async_remote_copy(src_ref=slot(*block) if src is None else src, dst_ref=slot(*block),
                                                send_sem=send_sems.at[k], recv_sem=recv_sems.at[k], device_id=to, device_id_type=MESH)

        mine = pltpu.make_async_copy(x_ref, slot(*me), local_sem)
        mine.start()
        first = [copy(0, me, sibling, src=x_ref)] + [copy(1 + j, me, (*chip, c), src=x_ref) for j, chip in enumerate(chips)]
        for cp in first:
            cp.start()
        passed = [copy(4 + j, (*chip, c), sibling) for j, chip in enumerate(chips)]
        for j, chip in enumerate(chips):
            copy(1 + j, (*chip, c), me).wait_recv()
            passed[j].start()
        copy(0, sibling, me).wait_recv()
        for j, chip in enumerate(chips):
            copy(4 + j, (*chip, 1 - c), me).wait_recv()
        for cp in first + passed:
            cp.wait_send()
        mine.wait()

    any_spec = pl.BlockSpec(memory_space=pl.ANY)
    return pl.pallas_call(
        body, name=name, in_specs=[any_spec], out_specs=any_spec,
        out_shape=jax.ShapeDtypeStruct((N_DEV,) + shard.shape, shard.dtype),
        scratch_shapes=[pltpu.SemaphoreType.DMA((N_DEV - 1,)), pltpu.SemaphoreType.DMA((N_DEV - 1,)), pltpu.SemaphoreType.DMA(())],
    )(shard)


def _all_reduce_small(part, *, loss_rows, loss_scale):
    rows = part.shape[0]
    n_peer = N_DEV - 1

    def body(p_ref, o_ref, buf, send_sems, recv_sems):
        me = _my_index()
        peers = _peers()
        buf[me] = p_ref[...]
        sends = [pltpu.make_async_remote_copy(src_ref=p_ref, dst_ref=buf.at[me], send_sem=send_sems.at[rel], recv_sem=recv_sems.at[rel],
                                              device_id=peers[rel][0], device_id_type=MESH) for rel in range(n_peer)]
        for cp in sends:
            cp.start()
        for rel in range(n_peer):
            pltpu.make_async_remote_copy(src_ref=p_ref, dst_ref=buf.at[peers[rel][1]], send_sem=send_sems.at[rel], recv_sem=recv_sems.at[rel],
                                         device_id=peers[rel][0], device_id_type=MESH).wait_recv()
        for cp in sends:
            cp.wait_send()
        total = buf[0]
        for dev in range(1, N_DEV):
            total = total + buf[dev]
        o_ref[...] = total
        squares = total[rows - loss_rows:]
        loss = jnp.sum(jnp.sum(squares, axis=0, keepdims=True), axis=-1, keepdims=True) * loss_scale
        o_ref[rows - loss_rows:, :] = jnp.broadcast_to(loss, (loss_rows, LANES))

    vmem = pl.BlockSpec(memory_space=pltpu.VMEM)
    return pl.pallas_call(
        body, name="all_reduce_small", in_specs=[vmem], out_specs=vmem, out_shape=jax.ShapeDtypeStruct(part.shape, F32),
        scratch_shapes=[pltpu.VMEM((N_DEV, rows, LANES), F32), pltpu.SemaphoreType.DMA((n_peer,)), pltpu.SemaphoreType.DMA((n_peer,))],
        compiler_params=pltpu.CompilerParams(has_side_effects=True, vmem_limit_bytes=VMEM_LIMIT),
    )(part)


def _adamw_math(w, g, m, v):
    m_new = ADAM_B1 * m + (1.0 - ADAM_B1) * g
    v_new = ADAM_B2 * v + (1.0 - ADAM_B2) * (g * g)
    m_hat = m_new / (1.0 - ADAM_B1 ** ADAM_STEP)
    v_hat = v_new / (1.0 - ADAM_B2 ** ADAM_STEP)
    delta = -ADAM_LR * (m_hat / (jnp.sqrt(v_hat) + ADAM_EPS) + ADAM_WD * w)
    return delta, m_new, v_new


def _adamw(parts, w, m, v, *, name, tr=64):
    rows, cols = w.shape
    tr = min(tr, rows)
    assert rows % tr == 0
    stacked = parts.ndim == 3

    def body(p_ref, w_ref, m_ref, v_ref, g_ref, d_ref, mo_ref, vo_ref):
        if stacked:
            g = p_ref[0].astype(F32)
            for dev in range(1, N_DEV):
                g = g + p_ref[dev].astype(F32)
        else:
            g = p_ref[...]
        delta, m_new, v_new = _adamw_math(w_ref[...], g, m_ref[...], v_ref[...])
        g_ref[...] = g
        d_ref[...] = delta
        mo_ref[...] = m_new
        vo_ref[...] = v_new

    tile = pl.BlockSpec((tr, cols), lambda i: (i, 0))
    p_spec = pl.BlockSpec((N_DEV, tr, cols), lambda i: (0, i, 0)) if stacked else tile
    return pl.pallas_call(
        body, name=name, grid=(rows // tr,), in_specs=[p_spec, tile, tile, tile], out_specs=[tile] * 4,
        out_shape=[jax.ShapeDtypeStruct((rows, cols), F32)] * 4, compiler_params=_params(1),
    )(parts, w, m, v)


_LATER = ("w_out", "w_cq", "w_ckv", "w_co", "w_ff1", "w_ff2")


def _as_rows(stacked):
    return stacked.reshape(-1, stacked.shape[-1])


def _local_step(x, mem, target, small, shards):
    batch, seq, d = x.shape
    n_mem = mem.shape[1]
    t = batch * seq
    x2, mem2, tgt2 = x.reshape(t, d), mem.reshape(batch * n_mem, d), target.reshape(t, d)
    g_mix, g_cross, g_mem, g_ffn, g_final = (small[k] for k in ("norm_mix_g", "norm_cross_g", "norm_mem_g", "norm_ffn_g", "norm_final_g"))
    gv, hg, w_sp, b_sp_t = small["gm_v_norm_g"], small["head_norm_g"], small["w_spatial"], small["b_spatial_t"]

    win_t = _as_rows(_gather_two_level(shards["w_in"], name="gather_w_in"))
    proj, xn = _norm_mm(x2, g_mix, win_t, mode="nt", name="proj_fwd", tm=512, tn=win_t.shape[0])
    merged_a = _gmlp_fwd(proj, w_sp, b_sp_t, gv, hg)
    merged, sb_raw, sb_weights, sb_signed_e, gathered = _sb_fwd(proj, merged_a, hg, batch=batch, seq=seq,
                                                                exchange=_Exchange([shards[n] for n in _LATER], scatter=False))
    wout, wcq, wckv_t, wco, wff1_t, wff2 = (_as_rows(g) for g in gathered)
    h1 = _mm(merged, wout, mode="nn", out_dtype=F32, name="mix_out_fwd", epi=_epi_residual, epi_ins=(x2,))
    qx, hn1 = _norm_mm(h1, g_cross, wcq, mode="nn", name="xq_fwd")
    kvx, memn = _norm_mm(mem2, g_mem, wckv_t, mode="nt", name="xkv_fwd", tm=512, tn=wckv_t.shape[0])
    o = _xattn_fwd(qx, kvx, batch=batch, seq=seq, n_mem=n_mem)
    h2 = _mm(o, wco, mode="nn", out_dtype=F32, name="xo_fwd", epi=_epi_residual, epi_ins=(h1,))
    fpre, hn2 = _norm_mm(h2, g_ffn, wff1_t, mode="nt", name="ff1_fwd", tm=512, tn=wff1_t.shape[0])
    dh3, final_rows = _mm(fpre, wff2, mode="nn", out_dtype=F32, name="ff2_fwd_loss", tm=512, tk=wff2.shape[0], a_fn=_relu2,
                          epi=_epi_loss, epi_ins=(h2, tgt2), vec_ins=(g_final,), aux=2)
    dg_final, sq_err = final_rows[0:1], final_rows[1:2]

    dpre = _mm(dh3, wff2, mode="nt", out_dtype=BF16, name="ff2_bwd_x", tm=512, tn=wff2.shape[0], epi=_epi_relu2_grad,
               epi_ins=(fpre,))
    chunk = wff2.shape[0] // N_DEV
    d_wff2_t = _mm(dh3, fpre, mode="tn", out_dtype=BF16, name="ff2_bwd_w", tn=2048, b_fn=_relu2, col_chunk=chunk)
    d_wff1 = _mm(hn2, dpre, mode="tn", out_dtype=BF16, name="ff1_bwd_w", tn=2048, col_chunk=chunk)
    dh2, dg_ffn = _mm(dpre, wff1_t, mode="nn", out_dtype=F32, name="ff1_bwd_x", tm=512, tk=wff1_t.shape[0], epi=_epi_rms_bwd,
                      epi_ins=(h2, dh3), vec_ins=(g_ffn,), aux=True)
    do = _mm(dh2, wco, mode="nt", out_dtype=BF16, name="xo_bwd_x")
    d_wco = _mm(o, dh2, mode="tn", out_dtype=BF16, name="xo_bwd_w")
    dqx, dkvx = _xattn_bwd(qx, kvx, do, batch=batch, seq=seq, n_mem=n_mem)
    d_wcq = _mm(hn1, dqx, mode="tn", out_dtype=BF16, name="xq_bwd_w")
    dh1, dg_cross = _mm(dqx, wcq, mode="nt", out_dtype=F32, name="xq_bwd_x", tm=512, epi=_epi_rms_bwd,
                        epi_ins=(h1, dh2), vec_ins=(g_cross,), aux=True)
    d_wckv_t = _mm(dkvx, memn, mode="tn", out_dtype=BF16, name="xkv_bwd_w")
    _, dg_mem = _mm(dkvx, wckv_t, mode="nn", out_dtype=BF16, name="xkv_bwd_x", tm=512, epi=_epi_rms_gain_only,
                    epi_ins=(mem2,), vec_ins=(g_mem,), aux=True)
    dmerged = _mm(dh1, wout, mode="nt", out_dtype=BF16, name="mix_out_bwd_x")
    d_wout = _mm(merged, dh1, mode="tn", out_dtype=BF16, name="mix_out_bwd_w")
    dp_a, d_wsp, d_bsp_t, d_gv, d_hg_a = _gmlp_bwd(proj, dmerged, w_sp, b_sp_t, gv, hg)
    d_later = {"w_out": d_wout, "w_cq": d_wcq, "w_ckv": d_wckv_t, "w_co": d_wco, "w_ff1": d_wff1, "w_ff2": d_wff2_t}
    scatter = _Exchange([g if g.ndim == 3 else g.reshape(N_DEV, -1, d) for g in (d_later[n] for n in _LATER)], scatter=True)
    dq, dk, dv, d_hg_b, received = _sb_bwd(proj, sb_raw, sb_weights, sb_signed_e, dmerged, hg, batch=batch, seq=seq, exchange=scatter)
    dproj = jnp.concatenate([dp_a, dq, dk, dv], axis=1)
    d_win_t = _mm(xn, dproj, mode="tn", out_dtype=BF16, name="proj_bwd_w", tn=dproj.shape[1] // 2).T
    dx, dg_mix, d_win_received = _mm(dproj, win_t, mode="nn", out_dtype=F32, name="proj_bwd_x", tm=512, tk=win_t.shape[0],
                                     epi=_epi_rms_bwd, epi_ins=(x2, dh1), vec_ins=(g_mix,), aux=True,
                                     exchange=_Exchange([d_win_t.reshape(N_DEV, -1, d)], scatter=True))

    d_small = {"norm_mix_g": dg_mix, "gm_v_norm_g": d_gv, "w_spatial": d_wsp, "b_spatial_t": d_bsp_t, "head_norm_g": jnp.concatenate([d_hg_a, d_hg_b], axis=1),
               "norm_cross_g": dg_cross, "norm_mem_g": dg_mem, "norm_ffn_g": dg_ffn, "norm_final_g": dg_final}
    d_big = dict(zip(_LATER, received))
    d_big["w_in"] = d_win_received
    return sq_err, dx.reshape(batch, seq, d), d_small, d_big


_BIG = ("w_in", "w_out", "w_cq", "w_ckv", "w_co", "w_ff1", "w_ff2")
_GATHERED_TRANSPOSED = ("w_in", "w_ckv", "w_ff1")
_UPDATED_TRANSPOSED = ("w_in", "w_ckv", "w_ff2")
_SMALL = ("norm_mix_g", "gm_v_norm_g", "w_spatial", "b_spatial", "head_norm_g", "norm_cross_g", "norm_mem_g", "norm_ffn_g", "norm_final_g")
_NAMES = ("norm_mix_g", "w_in", "gm_v_norm_g", "w_spatial", "b_spatial", "head_norm_g", "w_out", "norm_cross_g", "norm_mem_g",
          "w_cq", "w_ckv", "w_co", "norm_ffn_g", "w_ff1", "w_ff2", "norm_final_g")


def _rows_of(a):
    r = a.reshape(-1, LANES)
    pad = (-r.shape[0]) % 8
    return jnp.pad(r, ((0, pad), (0, 0))) if pad else r


def _shard2d(a, transposed):
    return a[0].T if transposed else a[0]


def kernel(x, mem, norm_mix_g, w_in, gm_v_norm_g, w_spatial, b_spatial, head_norm_g, w_out, norm_cross_g, norm_mem_g, w_cq, w_ckv, w_co, norm_ffn_g, w_ff1, w_ff2, norm_final_g, loss_target, m_norm_mix_g, m_w_in, m_gm_v_norm_g, m_w_spatial, m_b_spatial, m_head_norm_g, m_w_out, m_norm_cross_g, m_norm_mem_g, m_w_cq, m_w_ckv, m_w_co, m_norm_ffn_g, m_w_ff1, m_w_ff2, m_norm_final_g, v_norm_mix_g, v_w_in, v_gm_v_norm_g, v_w_spatial, v_b_spatial, v_head_norm_g, v_w_out, v_norm_cross_g, v_norm_mem_g, v_w_cq, v_w_ckv, v_w_co, v_norm_ffn_g, v_w_ff1, v_w_ff2, v_norm_final_g):
    weights = dict(norm_mix_g=norm_mix_g, w_in=w_in, gm_v_norm_g=gm_v_norm_g, w_spatial=w_spatial, b_spatial=b_spatial,
                   head_norm_g=head_norm_g, w_out=w_out, norm_cross_g=norm_cross_g, norm_mem_g=norm_mem_g, w_cq=w_cq, w_ckv=w_ckv,
                   w_co=w_co, norm_ffn_g=norm_ffn_g, w_ff1=w_ff1, w_ff2=w_ff2, norm_final_g=norm_final_g)
    mom1 = dict(norm_mix_g=m_norm_mix_g, w_in=m_w_in, gm_v_norm_g=m_gm_v_norm_g, w_spatial=m_w_spatial, b_spatial=m_b_spatial,
                head_norm_g=m_head_norm_g, w_out=m_w_out, norm_cross_g=m_norm_cross_g, norm_mem_g=m_norm_mem_g, w_cq=m_w_cq,
                w_ckv=m_w_ckv, w_co=m_w_co, norm_ffn_g=m_norm_ffn_g, w_ff1=m_w_ff1, w_ff2=m_w_ff2, norm_final_g=m_norm_final_g)
    mom2 = dict(norm_mix_g=v_norm_mix_g, w_in=v_w_in, gm_v_norm_g=v_gm_v_norm_g, w_spatial=v_w_spatial, b_spatial=v_b_spatial,
                head_norm_g=v_head_norm_g, w_out=v_w_out, norm_cross_g=v_norm_cross_g, norm_mem_g=v_norm_mem_g, w_cq=v_w_cq,
                w_ckv=v_w_ckv, w_co=v_w_co, norm_ffn_g=v_norm_ffn_g, w_ff1=v_w_ff1, w_ff2=v_w_ff2, norm_final_g=v_norm_final_g)

    shards = {n: _shard2d(weights[n], n in _GATHERED_TRANSPOSED).astype(BF16) for n in _BIG}
    small = {n: weights[n].reshape(1, -1) for n in _SMALL if n not in ("w_spatial", "b_spatial")}
    small["w_spatial"] = w_spatial[0]
    small["b_spatial_t"] = b_spatial[0].T
    sq_err, grad_x, d_small, d_big = _local_step(x, mem, loss_target, small, shards)

    d_small["b_spatial"] = d_small.pop("b_spatial_t").T
    sq_rows = _rows_of(sq_err)
    packed = jnp.concatenate([_rows_of(d_small[n]) for n in _SMALL] + [sq_rows], axis=0)
    summed = _all_reduce_small(packed, loss_rows=sq_rows.shape[0], loss_scale=0.5 / x.shape[-1])

    grads, deltas, new_m, new_v = {}, {}, {}, {}
    for n in _BIG:
        flip = n in _UPDATED_TRANSPOSED
        outs = _adamw(d_big[n], _shard2d(weights[n], flip), _shard2d(mom1[n], flip), _shard2d(mom2[n], flip), name="adamw_" + n)
        outs = [o.T if flip else o for o in outs]
        grads[n], deltas[n], new_m[n], new_v[n] = (o[None] for o in outs)
    pack = lambda src: jnp.concatenate([_rows_of(src[n]) for n in _SMALL], axis=0)
    n_small_rows = sum(_rows_of(weights[n]).shape[0] for n in _SMALL)
    outs = _adamw(summed[:n_small_rows], pack(weights), pack(mom1), pack(mom2), name="adamw_small", tr=n_small_rows)
    at = 0
    for n in _SMALL:
        used = weights[n].size // LANES
        for dst, o in zip((grads, deltas, new_m, new_v), outs):
            dst[n] = o[at:at + used].reshape(weights[n].shape)
        at += _rows_of(weights[n]).shape[0]
    loss = summed[n_small_rows, 0]
    return (loss, grad_x, *[grads[n] for n in _NAMES], *[deltas[n] for n in _NAMES], *[new_m[n] for n in _NAMES],
            *[new_v[n] for n in _NAMES])
```

```python
import math

import jax
import jax.numpy as jnp
from jax import lax
from jax.experimental import pallas as pl
from jax.experimental.pallas import tpu as pltpu

F32 = jnp.float32
BF16 = jnp.bfloat16
EPS = 1e-6
N_DEV = 8
LANES = 128
CHUNK = 128
GM_GROUPS = 4
GM_WIDTH = 512
SB_PAIRS = 4
SB_HEAD_DIM = 64
SB_SCALE = 0.125
SB_TILE = 128
SB_BLOCK = 512
X_HEADS = 4
X_HEAD_DIM = 256
X_SCALE = 1.0 / 16.0
VMEM_LIMIT = 56 * 1024 * 1024
ADAM_LR, ADAM_B1, ADAM_B2, ADAM_EPS, ADAM_WD, ADAM_STEP = 0.001, 0.9, 0.999, 1e-08, 0.01, 10
MESH = pl.DeviceIdType.MESH


def _params(n_axes):
    return pltpu.CompilerParams(dimension_semantics=("arbitrary",) * n_axes, vmem_limit_bytes=VMEM_LIMIT)


def _dot(a, b, dims):
    return lax.dot_general(a, b, (dims, ((), ())), preferred_element_type=F32)


def _nn(a, b):
    return _dot(a, b, ((1,), (0,)))


def _nt(a, b):
    return _dot(a, b, ((1,), (1,)))


def _tn(a, b):
    return _dot(a, b, ((0,), (0,)))


_MODES = {"nn": _nn, "nt": _nt, "tn": _tn}


def _rstd(x):
    return lax.rsqrt(jnp.mean(x * x, axis=-1, keepdims=True) + EPS)


def _gelu(x):
    c = math.sqrt(2.0 / math.pi)
    t = jnp.tanh(c * (x + 0.044715 * x * x * x))
    return 0.5 * x * (1.0 + t)


def _gelu_and_grad(x):
    c = math.sqrt(2.0 / math.pi)
    t = jnp.tanh(c * (x + 0.044715 * x * x * x))
    half = 0.5 * (1.0 + t)
    return x * half, half + 0.5 * x * (1.0 - t * t) * c * (1.0 + 3 * 0.044715 * x * x)


def _split_bf16(x):
    hi = x.astype(BF16)
    lo = (x - hi.astype(F32)).astype(BF16)
    return hi, lo


def _mm(a, b, *, mode, out_dtype, name, tm=1024, tn=1024, tk=1024, a_fn=None, b_fn=None, epi=None, epi_ins=(), vec_ins=(),
        aux=False, col_chunk=None, exchange=None):
    if mode == "nn":
        (m, k), (k2, n) = a.shape, b.shape
    elif mode == "nt":
        (m, k), (n, k2) = a.shape, b.shape
    else:
        (k, m), (k2, n) = a.shape, b.shape
    assert k == k2, (a.shape, b.shape, mode)
    tm, tn, tk = min(tm, m), min(tn, n), min(tk, k)
    assert m % tm == 0 and n % tn == 0 and k % tk == 0, (m, n, k, tm, tn, tk)
    n_m, n_n, n_k = m // tm, n // tn, k // tk
    assert not aux or n_n == 1
    dot = _MODES[mode]
    n_epi, n_vec = len(epi_ins), len(vec_ins)

    def body(*refs):
        ins, outs, scratch, x_refs = _riding(exchange, refs, 2 + n_epi + n_vec, 2 if aux else 1, 1 if n_k > 1 else 0)
        a_ref, b_ref, epi_refs = ins[0], ins[1], ins[2:]
        o_ref = outs[0]
        aux_ref = outs[1] if aux else None
        acc_ref = scratch[0] if n_k > 1 else None
        i, j, kk = pl.program_id(0), pl.program_id(1), pl.program_id(2)
        ride_done = _ride(exchange, x_refs, (i == 0) & (j == 0) & (kk == 0), (i == n_m - 1) & (j == n_n - 1) & (kk == n_k - 1))
        def product():
            av, bv = a_ref[...], b_ref[...]
            if a_fn is not None:
                av = a_fn(av)
            if b_fn is not None:
                bv = b_fn(bv)
            return dot(av.astype(BF16), bv.astype(BF16))

        def finish(acc):
            if col_chunk is not None:
                for ch in range(tn // col_chunk):
                    o_ref[ch] = acc[:, ch * col_chunk:(ch + 1) * col_chunk].astype(out_dtype)
                return
            if epi is None:
                o_ref[...] = acc.astype(out_dtype)
                return
            res = epi(acc, *[r[...] for r in epi_refs])
            if aux:
                res, rows = res[0], res[1:]
                rows = rows[0] if len(rows) == 1 else jnp.concatenate(rows, axis=0)

                @pl.when(i == 0)
                def _():
                    aux_ref[...] = rows

                @pl.when(i != 0)
                def _():
                    aux_ref[...] += rows
            o_ref[...] = res.astype(out_dtype)

        if n_k == 1:
            finish(product())
        else:
            @pl.when(kk == 0)
            def _():
                acc_ref[...] = product()

            @pl.when(kk != 0)
            def _():
                acc_ref[...] += product()

            @pl.when(kk == n_k - 1)
            def _():
                finish(acc_ref[...])

        ride_done()

    if mode == "tn":
        a_spec = pl.BlockSpec((tk, tm), lambda i, j, kk: (kk, i))
    else:
        a_spec = pl.BlockSpec((tm, tk), lambda i, j, kk: (i, kk))
    if mode == "nt":
        b_spec = pl.BlockSpec((tn, tk), lambda i, j, kk: (j, kk))
    else:
        b_spec = pl.BlockSpec((tk, tn), lambda i, j, kk: (kk, j))
    tile_spec = pl.BlockSpec((tm, tn), lambda i, j, kk: (i, j))
    row_spec = pl.BlockSpec((1, tn), lambda i, j, kk: (0, j))
    out_shape = [jax.ShapeDtypeStruct((m, n), out_dtype)]
    out_specs = [tile_spec]
    if col_chunk is not None:
        assert epi is None and not aux and tn % col_chunk == 0
        out_shape = [jax.ShapeDtypeStruct((n // col_chunk, m, col_chunk), out_dtype)]
        out_specs = [pl.BlockSpec((tn // col_chunk, tm, col_chunk), lambda i, j, kk: (j, i, 0))]
    if aux:
        out_shape.append(jax.ShapeDtypeStruct((int(aux), n), F32))
        out_specs.append(pl.BlockSpec((int(aux), tn), lambda i, j, kk: (0, j)))
    x_in, x_out, x_shape, x_scratch, x_arrays = _riding_specs(exchange)
    res = pl.pallas_call(
        body, name=name, grid=(n_m, n_n, n_k),
        in_specs=[a_spec, b_spec] + [tile_spec] * n_epi + [row_spec] * n_vec + x_in,
        out_specs=out_specs + x_out, out_shape=out_shape + x_shape,
        scratch_shapes=([pltpu.VMEM((tm, tn), F32)] if n_k > 1 else []) + x_scratch,
        compiler_params=_params(3),
    )(a, b, *epi_ins, *vec_ins, *x_arrays)
    if exchange is not None:
        return tuple(res)
    return res if aux else res[0]


def _norm_mm(x, g, w, *, mode, name, tm=1024, tn=1024):
    m, d = x.shape
    n = w.shape[0] if mode == "nt" else w.shape[1]
    tm, tn = min(tm, m), min(tn, n)
    assert m % tm == 0 and n % tn == 0
    dot = _MODES[mode]

    def body(x_ref, g_ref, w_ref, o_ref, xn_ref, xn_s):
        @pl.when(pl.program_id(1) == 0)
        def _():
            xv = x_ref[...]
            xn = (xv * _rstd(xv) * g_ref[...]).astype(BF16)
            xn_s[...] = xn
            xn_ref[...] = xn

        o_ref[...] = dot(xn_s[...], w_ref[...]).astype(BF16)

    w_spec = pl.BlockSpec((tn, d), lambda i, j: (j, 0)) if mode == "nt" else pl.BlockSpec((d, tn), lambda i, j: (0, j))
    return pl.pallas_call(
        body, name=name, grid=(m // tm, n // tn),
        in_specs=[pl.BlockSpec((tm, d), lambda i, j: (i, 0)), pl.BlockSpec((1, d), lambda i, j: (0, 0)), w_spec],
        out_specs=[pl.BlockSpec((tm, tn), lambda i, j: (i, j)), pl.BlockSpec((tm, d), lambda i, j: (i, 0))],
        out_shape=[jax.ShapeDtypeStruct((m, n), BF16), jax.ShapeDtypeStruct((m, d), BF16)],
        scratch_shapes=[pltpu.VMEM((tm, d), BF16)],
        compiler_params=_params(2),
    )(x, g, w)


def _epi_residual(acc, res):
    return res + acc


def _epi_relu2_grad(acc, pre):
    return acc * (2.0 * jnp.maximum(pre.astype(F32), 0.0))


def _relu2(pre):
    r = jnp.maximum(pre.astype(F32), 0.0)
    return r * r


def _epi_rms_bwd(acc, h, dres, g):
    r = _rstd(h)
    xh = h * r
    dxh = acc * g
    dh = dres + r * (dxh - xh * jnp.mean(dxh * xh, axis=-1, keepdims=True))
    return dh, jnp.sum(acc * xh, axis=0, keepdims=True)


def _epi_loss(acc, h_in, target, g):
    h = h_in + acc
    r = _rstd(h)
    xh = h * r
    err = xh * g - target
    dy = err * (1.0 / h.shape[-1])
    dxh = dy * g
    dh = r * (dxh - xh * jnp.mean(dxh * xh, axis=-1, keepdims=True))
    return dh, jnp.sum(dy * xh, axis=0, keepdims=True), jnp.sum(err * err, axis=0, keepdims=True)


def _epi_rms_gain_only(acc, h, g):
    return acc, jnp.sum(acc * (h * _rstd(h)), axis=0, keepdims=True)


def _tril(n):
    row = lax.broadcasted_iota(jnp.int32, (n, n), 0)
    col = lax.broadcasted_iota(jnp.int32, (n, n), 1)
    return col <= row


def _gmlp_fwd(proj, w_sp, b_sp_t, gv, hg, *, rows=512):
    t = proj.shape[0]
    rows = min(rows, t)
    n_c = rows // CHUNK

    def body(u_ref, v_ref, w_ref, bt_ref, gv_ref, hg_ref, m_ref):
        keep = _tril(CHUNK)
        for g in range(GM_GROUPS):
            cols = slice(g * LANES, (g + 1) * LANES)
            wg = jnp.where(keep, w_ref[g], 0.0).astype(BF16)
            u = _gelu(u_ref[:, cols].astype(F32))
            v = _gelu(v_ref[:, cols].astype(F32))
            vn = (v * _rstd(v) * gv_ref[:, cols]).astype(BF16)
            bias = bt_ref[:, g:g + 1]
            for c in range(n_c):
                rs = slice(c * CHUNK, (c + 1) * CHUNK)
                mixed = _nn(wg, vn[rs]) + bias
                a = u[rs] * mixed
                m_ref[rs, cols] = (a * _rstd(a) * hg_ref[:, cols]).astype(BF16)

    full = lambda shape: pl.BlockSpec(shape, lambda i: (0,) * len(shape))
    return pl.pallas_call(
        body, name="gmlp_fwd", grid=(t // rows,),
        in_specs=[pl.BlockSpec((rows, GM_WIDTH), lambda i: (i, 0)), pl.BlockSpec((rows, GM_WIDTH), lambda i: (i, 1)),
                  full((GM_GROUPS, CHUNK, CHUNK)), full((CHUNK, GM_GROUPS)), full((1, GM_WIDTH)), full((1, GM_WIDTH))],
        out_specs=pl.BlockSpec((rows, GM_WIDTH), lambda i: (i, 0)),
        out_shape=jax.ShapeDtypeStruct((t, 2 * GM_WIDTH), BF16),
        compiler_params=_params(1),
    )(proj, proj, w_sp, b_sp_t, gv, hg)


def _gmlp_bwd(proj, dmerged, w_sp, b_sp_t, gv, hg, *, rows=512):
    t = proj.shape[0]
    rows = min(rows, t)
    n_c = rows // CHUNK
    n_steps = t // rows

    def body(u_ref, v_ref, dm_ref, w_ref, bt_ref, gv_ref, hg_ref, dp_ref, dw_ref, dbt_ref, dgv_ref, dhg_ref, db_acc):
        step = pl.program_id(0)
        keep = _tril(CHUNK)

        @pl.when(step == 0)
        def _():
            dw_ref[...] = jnp.zeros_like(dw_ref)
            db_acc[...] = jnp.zeros_like(db_acc)
            dgv_ref[...] = jnp.zeros_like(dgv_ref)
            dhg_ref[...] = jnp.zeros_like(dhg_ref)

        for g in range(GM_GROUPS):
            cols = slice(g * LANES, (g + 1) * LANES)
            wg = jnp.where(keep, w_ref[g], 0.0).astype(BF16)
            u, u_slope = _gelu_and_grad(u_ref[:, cols].astype(F32))
            v, v_slope = _gelu_and_grad(v_ref[:, cols].astype(F32))
            r = _rstd(v)
            xh = v * r
            gvg = gv_ref[:, cols]
            hgg = hg_ref[:, cols]
            vn = (xh * gvg).astype(BF16)
            bias = bt_ref[:, g:g + 1]
            dm = dm_ref[:, cols].astype(F32)
            du_parts, dvn_parts = [], []
            dw = jnp.zeros((CHUNK, CHUNK), F32)
            db = jnp.zeros((CHUNK, LANES), F32)
            dhg = jnp.zeros((1, LANES), F32)
            for c in range(n_c):
                rs = slice(c * CHUNK, (c + 1) * CHUNK)
                mixed = _nn(wg, vn[rs]) + bias
                a = u[rs] * mixed
                ra = _rstd(a)
                an = a * ra
                dhg = dhg + jnp.sum(dm[rs] * an, axis=0, keepdims=True)
                dan = dm[rs] * hgg
                da = ra * (dan - an * jnp.mean(dan * an, axis=-1, keepdims=True))
                du_parts.append(da * mixed)
                dmixed = da * u[rs]
                db = db + dmixed
                dmb = dmixed.astype(BF16)
                dw = dw + _nt(dmb, vn[rs])
                dvn_parts.append(_tn(wg, dmb))
            du = jnp.concatenate(du_parts, axis=0)
            dvn = jnp.concatenate(dvn_parts, axis=0)
            dw_ref[g] += dw
            db_acc[g] += db
            dhg_ref[:, cols] += dhg
            dgv_ref[:, cols] += jnp.sum(dvn * xh, axis=0, keepdims=True)
            dxh = dvn * gvg
            dv = r * (dxh - xh * jnp.mean(dxh * xh, axis=-1, keepdims=True))
            dp_ref[:, cols] = (du * u_slope).astype(BF16)
            dp_ref[:, GM_WIDTH + g * LANES:GM_WIDTH + (g + 1) * LANES] = (dv * v_slope).astype(BF16)

        @pl.when(step == n_steps - 1)
        def _():
            for g in range(GM_GROUPS):
                dw_ref[g] = jnp.where(keep, dw_ref[g], 0.0)
                dbt_ref[:, g:g + 1] = jnp.sum(db_acc[g], axis=-1, keepdims=True)

    full = lambda shape: pl.BlockSpec(shape, lambda i: (0,) * len(shape))
    return pl.pallas_call(
        body, name="gmlp_bwd", grid=(n_steps,),
        in_specs=[pl.BlockSpec((rows, GM_WIDTH), lambda i: (i, 0)), pl.BlockSpec((rows, GM_WIDTH), lambda i: (i, 1)),
                  pl.BlockSpec((rows, GM_WIDTH), lambda i: (i, 0)),
                  full((GM_GROUPS, CHUNK, CHUNK)), full((CHUNK, GM_GROUPS)), full((1, GM_WIDTH)), full((1, GM_WIDTH))],
        out_specs=[pl.BlockSpec((rows, 2 * GM_WIDTH), lambda i: (i, 0)), full((GM_GROUPS, CHUNK, CHUNK)),
                   full((CHUNK, GM_GROUPS)), full((1, GM_WIDTH)), full((1, GM_WIDTH))],
        out_shape=[jax.ShapeDtypeStruct((t, 2 * GM_WIDTH), BF16), jax.ShapeDtypeStruct((GM_GROUPS, CHUNK, CHUNK), F32),
                   jax.ShapeDtypeStruct((CHUNK, GM_GROUPS), F32), jax.ShapeDtypeStruct((1, GM_WIDTH), F32),
                   jax.ShapeDtypeStruct((1, GM_WIDTH), F32)],
        scratch_shapes=[pltpu.VMEM((GM_GROUPS, CHUNK, LANES), F32)],
        compiler_params=_params(1),
    )(proj, proj, dmerged, w_sp, b_sp_t, gv, hg)


def _sb_logits(z, strict):
    ls = jnp.minimum(z, 0.0) - jnp.log(1.0 + jnp.exp(-jnp.abs(z)))
    l1m = ls - z
    if strict is not None:
        l1m = jnp.where(strict, l1m, 0.0)
    return ls, l1m


def _tri_sums(x, tri):
    hi, lo = _split_bf16(x)
    return _nn(jnp.concatenate([hi, lo], axis=1), jnp.concatenate([tri, tri], axis=0))


def _sb_weights(ls, in_tile, right, strict):
    a = jnp.exp(ls + in_tile + right)
    if strict is not None:
        a = jnp.where(strict, a, 0.0)
    return a


def _sb_masks(q_rows):
    row = lax.broadcasted_iota(jnp.int32, (SB_TILE, SB_TILE), 0)
    col = lax.broadcasted_iota(jnp.int32, (SB_TILE, SB_TILE), 1)
    lane = lax.broadcasted_iota(jnp.int32, (q_rows, LANES), 1)
    return row, col, lane < SB_HEAD_DIM


def _stack_heads(x, first):
    zero = jnp.zeros_like(x)
    return jnp.concatenate([jnp.where(first, x, zero), jnp.where(first, zero, x)], axis=0)


def _stack_heads_t(x_t):
    first_t = lax.broadcasted_iota(jnp.int32, x_t.shape, 0) < SB_HEAD_DIM
    zero = jnp.zeros_like(x_t)
    return jnp.concatenate([jnp.where(first_t, x_t, zero), jnp.where(first_t, zero, x_t)], axis=1).astype(BF16)


def _unstack_heads(x2, first):
    half = x2.shape[0] // 2
    return jnp.where(first, x2[:half], x2[half:])


def _live_rows(x, s, q_rows):
    if s == 0:
        return x
    return jnp.concatenate([x[s * SB_TILE:q_rows], x[q_rows + s * SB_TILE:]], axis=0)


def _spread_rows(x, s, q_rows):
    if s == 0:
        return x
    half = q_rows - s * SB_TILE
    zero = jnp.zeros((s * SB_TILE,) + x.shape[1:], x.dtype)
    return jnp.concatenate([zero, x[:half], zero, x[half:]], axis=0)


def _stacked_col_minus_row(q_rows):
    row = lax.broadcasted_iota(jnp.int32, (2 * q_rows, SB_TILE), 0)
    col = lax.broadcasted_iota(jnp.int32, (2 * q_rows, SB_TILE), 1)
    return col - (row & (q_rows - 1))


def _head_mean(x, first):
    s0 = jnp.sum(jnp.where(first, x, 0.0), axis=-1, keepdims=True)
    s1 = jnp.sum(jnp.where(first, 0.0, x), axis=-1, keepdims=True)
    return jnp.where(first, s0, s1) * (1.0 / SB_HEAD_DIM)


def _riding(exchange, refs, n_in, n_out, n_scratch):
    n_x = exchange.n if exchange is not None else 0
    ins, rest = refs[:n_in], refs[n_in:]
    x_src, rest = rest[:n_x], rest[n_x:]
    outs, rest = rest[:n_out], rest[n_out:]
    x_dst, rest = rest[:n_x], rest[n_x:]
    return ins, outs, rest[:n_scratch], (x_src, x_dst, rest[n_scratch:])


def _riding_specs(exchange):
    if exchange is None:
        return [], [], [], [], []
    return exchange.in_specs, exchange.out_specs, exchange.out_shape, exchange.scratch, exchange.arrays


def _ride(exchange, x_refs, first_step, last_step):
    if exchange is None:
        return lambda: None

    @pl.when(first_step)
    def _():
        exchange.start(*x_refs)

    def finish():
        @pl.when(last_step)
        def _():
            exchange.wait(*x_refs)

    return finish


def _sb_fwd(proj, merged_a, hg, *, batch, seq, exchange=None):
    q0, k0, v0 = 2 * GM_WIDTH // LANES, 2 * GM_WIDTH // LANES + SB_PAIRS, 2 * GM_WIDTH // LANES + 2 * SB_PAIRS
    q_rows = block_keys = min(SB_BLOCK, seq)
    assert seq % block_keys == 0
    n_q, n_sub, n_blocks = seq // q_rows, block_keys // SB_TILE, seq // block_keys

    def body(*refs):
        (q_ref, k_ref, v_ref, hg_ref, _), (m_ref, raw_ref, a_ref, l_ref), _, x_refs = _riding(exchange, refs, 5, 4, 0)
        b, p, i = pl.program_id(0), pl.program_id(1), pl.program_id(2)
        finish = _ride(exchange, x_refs, (b == 0) & (p == 0) & (i == 0), (b == batch - 1) & (p == SB_PAIRS - 1) & (i == n_q - 1))
        row, col, first = _sb_masks(q_rows)
        upper = (row > col).astype(BF16)
        q2 = _stack_heads((q_ref[...].astype(F32) * SB_SCALE).astype(BF16), first)
        diff = _stacked_col_minus_row(q_rows)
        last = i

        def key_tile(jb, s):
            return k_ref[pl.ds(pl.multiple_of((jb * n_sub + s) * SB_TILE, SB_TILE), SB_TILE), :]

        def scores(jb):
            return tuple(_nt(q2, key_tile(jb, s)) for s in range(n_sub))

        def keep_for_backward(ref, jb, s, stacked):
            cols = slice(s * SB_TILE, (s + 1) * SB_TILE)
            ref[0, 0, jb, 0, :, cols] = stacked[:q_rows]
            ref[0, 0, jb, 1, :, cols] = stacked[q_rows:]

        def weights_of(jb, z, right):
            logits = [_sb_logits(z[s], None) for s in range(n_sub)]
            totals = [jnp.sum(l1m, axis=-1, keepdims=True) for _, l1m in logits]
            sums = [_tri_sums(l1m, upper) for _, l1m in logits]
            for s in reversed(range(n_sub)):
                keep_for_backward(a_ref, jb, s, _sb_weights(logits[s][0], sums[s], right, None).astype(BF16))
                keep_for_backward(l_ref, jb, s, logits[s][1].astype(BF16))
                right = right + totals[s]
            return right

        def diagonal_weights():
            keeps = [_live_rows(diff, s, q_rows) < -s * SB_TILE for s in range(n_sub)]
            logits = [_sb_logits(_nt(_live_rows(q2, s, q_rows), key_tile(last, s)), keeps[s]) for s in range(n_sub)]
            totals = [jnp.sum(l1m, axis=-1, keepdims=True) for _, l1m in logits]
            sums = [_tri_sums(l1m, upper) for _, l1m in logits]
            right = jnp.zeros((2 * q_rows, 1), F32)
            for s in reversed(range(n_sub)):
                live = _sb_weights(logits[s][0], sums[s], _live_rows(right, s, q_rows), keeps[s]).astype(BF16)
                keep_for_backward(a_ref, last, s, _spread_rows(live, s, q_rows))
                keep_for_backward(l_ref, last, s, _spread_rows(logits[s][1].astype(BF16), s, q_rows))
                right = right + _spread_rows(totals[s], s, q_rows)
            return right

        def weighted_values(jb):
            weights = jnp.concatenate([a_ref[0, 0, jb, 0], a_ref[0, 0, jb, 1]], axis=0)
            return _nn(weights, v_ref[pl.ds(pl.multiple_of(jb * block_keys, block_keys), block_keys), :])

        right = diagonal_weights()
        z_next = scores(jnp.maximum(last - 1, 0))

        def step(k, carry):
            right, acc, z = carry
            jb = last - k
            acc = acc + weighted_values(jb + 1)
            z_next = scores(jnp.maximum(jb - 1, 0))
            return weights_of(jb, z, right), acc, z_next

        _, acc2, _ = lax.fori_loop(1, last + 1, step, (right, jnp.zeros((2 * q_rows, LANES), F32), z_next))
        acc = _unstack_heads(acc2 + weighted_values(0), first)
        raw_ref[...] = acc
        m_ref[...] = (acc * lax.rsqrt(_head_mean(acc * acc, first) + EPS) * hg_ref[...]).astype(BF16)
        finish()

    t = batch * seq
    blk = lambda c0: pl.BlockSpec((q_rows, LANES), lambda b, p, i: (b * n_q + i, c0 + p))
    kv = lambda c0: pl.BlockSpec((seq, LANES), lambda b, p, i: (b, c0 + p))
    kept = pl.BlockSpec((1, 1, n_blocks, 2, q_rows, block_keys), lambda b, p, i: (p, b, 0, 0, i, 0))
    kept_shape = jax.ShapeDtypeStruct((SB_PAIRS, batch, n_blocks, 2, seq, block_keys), BF16)
    x_in, x_out, x_shape, x_scratch, x_arrays = _riding_specs(exchange)
    res = pl.pallas_call(
        body, name="sb_fwd", grid=(batch, SB_PAIRS, n_q),
        in_specs=[blk(q0), kv(k0), kv(v0), pl.BlockSpec((1, LANES), lambda b, p, i: (0, SB_PAIRS + p)),
                  pl.BlockSpec(memory_space=pl.ANY)] + x_in,
        out_specs=[blk(SB_PAIRS), blk(0), kept, kept] + x_out,
        out_shape=[jax.ShapeDtypeStruct((t, 2 * GM_WIDTH), BF16), jax.ShapeDtypeStruct((t, GM_WIDTH), F32), kept_shape, kept_shape] + x_shape,
        scratch_shapes=x_scratch,
        input_output_aliases={4: 0},
        compiler_params=_params(3),
    )(proj, proj, proj, hg, merged_a, *x_arrays)
    return res[0], res[1], res[2], res[3], res[4:]


def _sb_bwd(proj, raw, weights, log_rest, dmerged, hg, *, batch, seq, exchange=None):
    q0, k0, v0 = 2 * GM_WIDTH // LANES, 2 * GM_WIDTH // LANES + SB_PAIRS, 2 * GM_WIDTH // LANES + 2 * SB_PAIRS
    q_rows = block_keys = min(SB_BLOCK, seq)
    assert seq % block_keys == 0
    n_q, n_sub, n_blocks = seq // q_rows, block_keys // SB_TILE, seq // block_keys

    def body(*refs):
        ins, outs, (dk_acc, dv_acc), x_refs = _riding(exchange, refs, 8, 4, 2)
        q_ref, k_ref, v_ref, raw_ref, a_ref, l_ref, dm_ref, hg_ref = ins
        dq_ref, dk_ref, dv_ref, dhg_ref = outs
        p, b, i = pl.program_id(0), pl.program_id(1), pl.program_id(2)
        finish = _ride(exchange, x_refs, (b == 0) & (p == 0) & (i == 0), (b == batch - 1) & (p == SB_PAIRS - 1) & (i == n_q - 1))
        row, col, first = _sb_masks(q_rows)
        lower = (row < col).astype(BF16)

        @pl.when(jnp.logical_and(b == 0, i == 0))
        def _():
            dhg_ref[...] = jnp.zeros_like(dhg_ref)

        @pl.when(i == 0)
        def _():
            dk_acc[...] = jnp.zeros_like(dk_acc)
            dv_acc[...] = jnp.zeros_like(dv_acc)

        raw_v = raw_ref[...]
        dm = dm_ref[...].astype(F32)
        r = lax.rsqrt(_head_mean(raw_v * raw_v, first) + EPS)
        nrm = raw_v * r
        dhg_ref[...] += jnp.sum(dm * nrm, axis=0, keepdims=True)
        dn = dm * hg_ref[...]
        dout = r * (dn - nrm * _head_mean(dn * nrm, first))
        dout2 = _stack_heads(dout.astype(BF16), first)
        q2_t = _stack_heads_t(q_ref[...].astype(F32).T)
        dout2_t = _stack_heads_t(dout.T)
        diff = _stacked_col_minus_row(q_rows)
        last = i

        def kept(ref, jb, cols):
            return jnp.concatenate([ref[0, 0, jb, 0, :, cols], ref[0, 0, jb, 1, :, cols]], axis=0)

        def block(jb, carry, diagonal):
            gleft, dq = carry
            live = (lambda x, s: _live_rows(x, s, q_rows)) if diagonal else (lambda x, s: x)
            spread = (lambda x, s: _spread_rows(x, s, q_rows)) if diagonal else (lambda x, s: x)
            tiles = [pl.ds(pl.multiple_of((jb * n_sub + s) * SB_TILE, SB_TILE), SB_TILE) for s in range(n_sub)]
            cols = [slice(s * SB_TILE, (s + 1) * SB_TILE) for s in range(n_sub)]
            gmats = [_nt(live(dout2, s), v_ref[tiles[s], :]) * live(kept(a_ref, jb, cols[s]), s).astype(F32) for s in range(n_sub)]
            prefixes = [_tri_sums(g, lower) for g in gmats]
            dzs = []
            for s in range(n_sub):
                one_minus = jnp.exp(live(kept(l_ref, jb, cols[s]), s).astype(F32))
                dz = (gmats[s] * one_minus - (live(gleft, s) + prefixes[s]) * (1.0 - one_minus)) * SB_SCALE
                gleft = gleft + spread(jnp.sum(gmats[s], axis=-1, keepdims=True), s)
                if diagonal:
                    dz = jnp.where(live(diff, s) < -s * SB_TILE, dz, 0.0)
                dzs.append(spread(dz.astype(BF16), s))
            dz_all = jnp.concatenate(dzs, axis=1)
            dk_acc[jb] += _nn(q2_t, dz_all)
            dv_acc[jb] += _nn(dout2_t, kept(a_ref, jb, slice(None)))
            return gleft, dq + _nn(dz_all, k_ref[pl.ds(pl.multiple_of(jb * block_keys, block_keys), block_keys), :])

        carry = (jnp.zeros((2 * q_rows, 1), F32), jnp.zeros((2 * q_rows, LANES), F32))
        carry = lax.fori_loop(0, last, lambda jb, c: block(jb, c, False), carry)
        dq_ref[...] = _unstack_heads(block(last, carry, True)[1], first).astype(BF16)

        @pl.when(i == n_q - 1)
        def _():
            for jb in range(n_blocks):
                for s in range(n_sub):
                    rows = slice((jb * n_sub + s) * SB_TILE, (jb * n_sub + s + 1) * SB_TILE)
                    cols = slice(s * SB_TILE, (s + 1) * SB_TILE)
                    dk_ref[rows, :] = dk_acc[jb, :, cols].T.astype(BF16)
                    dv_ref[rows, :] = dv_acc[jb, :, cols].T.astype(BF16)

        finish()

    t = batch * seq
    blk = lambda c0: pl.BlockSpec((q_rows, LANES), lambda p, b, i: (b * n_q + i, c0 + p))
    kv = lambda c0: pl.BlockSpec((seq, LANES), lambda p, b, i: (b, c0 + p))
    row_spec = pl.BlockSpec((1, LANES), lambda p, b, i: (0, SB_PAIRS + p))
    kept_spec = pl.BlockSpec((1, 1, n_blocks, 2, q_rows, block_keys), lambda p, b, i: (p, b, 0, 0, i, 0))
    x_in, x_out, x_shape, x_scratch, x_arrays = _riding_specs(exchange)
    res = pl.pallas_call(
        body, name="sb_bwd", grid=(SB_PAIRS, batch, n_q),
        in_specs=[blk(q0), kv(k0), kv(v0), blk(0), kept_spec, kept_spec, blk(SB_PAIRS), row_spec] + x_in,
        out_specs=[blk(0), kv(0), kv(0), pl.BlockSpec((1, LANES), lambda p, b, i: (0, p))] + x_out,
        out_shape=[jax.ShapeDtypeStruct((t, GM_WIDTH), BF16)] * 3 + [jax.ShapeDtypeStruct((1, GM_WIDTH), F32)] + x_shape,
        scratch_shapes=[pltpu.VMEM((n_blocks, LANES, block_keys), F32), pltpu.VMEM((n_blocks, LANES, block_keys), F32)] + x_scratch,
        compiler_params=_params(3),
    )(proj, proj, proj, raw, weights, log_rest, dmerged, hg, *x_arrays)
    return res[0], res[1], res[2], res[3], res[4:]


def _x_softmax(qh, kh):
    s = _nt(qh, kh) * X_SCALE
    p = jnp.exp(s - jnp.max(s, axis=-1, keepdims=True))
    return p * (1.0 / jnp.sum(p, axis=-1, keepdims=True))


def _xattn_fwd(q, kv, *, batch, seq, n_mem, tq=512):
    tq = min(tq, seq)
    n_q = seq // tq
    d = X_HEADS * X_HEAD_DIM

    def body(q_ref, kv_ref, o_ref):
        for h in range(X_HEADS):
            cols = slice(h * X_HEAD_DIM, (h + 1) * X_HEAD_DIM)
            p = _x_softmax(q_ref[:, cols], kv_ref[:, cols])
            o_ref[:, cols] = _nn(p.astype(BF16), kv_ref[:, d + h * X_HEAD_DIM:d + (h + 1) * X_HEAD_DIM]).astype(BF16)

    return pl.pallas_call(
        body, name="xattn_fwd", grid=(batch, n_q),
        in_specs=[pl.BlockSpec((tq, d), lambda b, i: (b * n_q + i, 0)), pl.BlockSpec((n_mem, 2 * d), lambda b, i: (b, 0))],
        out_specs=pl.BlockSpec((tq, d), lambda b, i: (b * n_q + i, 0)),
        out_shape=jax.ShapeDtypeStruct((batch * seq, d), BF16),
        compiler_params=_params(2),
    )(q, kv)


def _xattn_bwd(q, kv, do, *, batch, seq, n_mem, tq=512):
    tq = min(tq, seq)
    n_q = seq // tq
    d = X_HEADS * X_HEAD_DIM

    def body(q_ref, kv_ref, do_ref, dq_ref, dkv_ref, acc):
        i = pl.program_id(1)

        @pl.when(i == 0)
        def _():
            acc[...] = jnp.zeros_like(acc)

        for h in range(X_HEADS):
            cols = slice(h * X_HEAD_DIM, (h + 1) * X_HEAD_DIM)
            vcols = slice(d + h * X_HEAD_DIM, d + (h + 1) * X_HEAD_DIM)
            qh, kh, vh, doh = q_ref[:, cols], kv_ref[:, cols], kv_ref[:, vcols], do_ref[:, cols]
            p = _x_softmax(qh, kh)
            dp = _nt(doh, vh)
            acc[:, vcols] += _tn(p.astype(BF16), doh)
            ds = (p * (dp - jnp.sum(dp * p, axis=-1, keepdims=True)) * X_SCALE).astype(BF16)
            dq_ref[:, cols] = _nn(ds, kh).astype(BF16)
            acc[:, cols] += _tn(ds, qh)

        @pl.when(i == n_q - 1)
        def _():
            dkv_ref[...] = acc[...].astype(BF16)

    return pl.pallas_call(
        body, name="xattn_bwd", grid=(batch, n_q),
        in_specs=[pl.BlockSpec((tq, d), lambda b, i: (b * n_q + i, 0)), pl.BlockSpec((n_mem, 2 * d), lambda b, i: (b, 0)),
                  pl.BlockSpec((tq, d), lambda b, i: (b * n_q + i, 0))],
        out_specs=[pl.BlockSpec((tq, d), lambda b, i: (b * n_q + i, 0)), pl.BlockSpec((n_mem, 2 * d), lambda b, i: (b, 0))],
        out_shape=[jax.ShapeDtypeStruct((batch * seq, d), BF16), jax.ShapeDtypeStruct((batch * n_mem, 2 * d), BF16)],
        scratch_shapes=[pltpu.VMEM((n_mem, 2 * d), F32)],
        compiler_params=_params(2),
    )(q, kv, do)


def _my_index():
    return 4 * lax.axis_index("x") + 2 * lax.axis_index("y") + lax.axis_index("c")


def _peers():
    x, y, c = lax.axis_index("x"), lax.axis_index("y"), lax.axis_index("c")
    out = []
    for rel in range(1, N_DEV):
        dx, dy, dc = (rel >> 2) & 1, (rel >> 1) & 1, rel & 1
        px, py, pc = x ^ dx, y ^ dy, c ^ dc
        out.append(((px, py, pc), 4 * px + 2 * py + pc))
    return out


class _Exchange:
    def __init__(self, arrays, scatter):
        self.arrays, self.scatter, self.n = list(arrays), scatter, len(arrays)
        any_spec = pl.BlockSpec(memory_space=pl.ANY)
        self.in_specs = [any_spec] * self.n
        self.out_specs = [any_spec] * self.n
        self.out_shape = [jax.ShapeDtypeStruct((N_DEV,) + tuple(a.shape[-2:]), a.dtype) for a in self.arrays]
        n_peer = N_DEV - 1
        self.scratch = [pltpu.SemaphoreType.DMA((self.n, n_peer)), pltpu.SemaphoreType.DMA((self.n, n_peer)),
                        pltpu.SemaphoreType.DMA((self.n,))]

    def _copies(self, srcs, dsts, sems, arriving):
        send_sems, recv_sems, local_sems = sems
        me = _my_index()
        local, remote = [], []
        for w in range(self.n):
            if not arriving:
                local.append(pltpu.make_async_copy(srcs[w].at[me] if self.scatter else srcs[w], dsts[w].at[me], local_sems.at[w]))
            for rel, (pos, idx) in enumerate(_peers()):
                remote.append(pltpu.make_async_remote_copy(
                    src_ref=srcs[w].at[idx] if self.scatter else srcs[w], dst_ref=dsts[w].at[idx if arriving else me],
                    send_sem=send_sems.at[w, rel], recv_sem=recv_sems.at[w, rel], device_id=pos, device_id_type=MESH))
        return local, remote

    def start(self, srcs, dsts, sems):
        local, sends = self._copies(srcs, dsts, sems, arriving=False)
        for cp in local + sends:
            cp.start()

    def wait(self, srcs, dsts, sems):
        for cp in self._copies(srcs, dsts, sems, arriving=True)[1]:
            cp.wait_recv()
        local, sends = self._copies(srcs, dsts, sems, arriving=False)
        for cp in sends:
            cp.wait_send()
        for cp in local:
            cp.wait()


def _gather_two_level(shard, *, name):
    def body(x_ref, out_ref, send_sems, recv_sems, local_sem):
        x, y, c = lax.axis_index("x"), lax.axis_index("y"), lax.axis_index("c")
        me, sibling = (x, y, c), (x, y, 1 - c)
        chips = [(1 - x, y), (x, 1 - y), (1 - x, 1 - y)]

        def slot(px, py, pc):
            return out_ref.at[4 * px + 2 * py + pc]

        def copy(k, block, to, src=None):
            return pltpu.make_async_remote_copy(src_ref=slot(*block) if src is None else src, dst_ref=slot(*block),
                                                send_sem=send_sems.at[k], recv_sem=recv_sems.at[k], device_id=to, device_id_type=MESH)

        mine = pltpu.make_async_copy(x_ref, slot(*me), local_sem)
        mine.start()
        first = [copy(0, me, sibling, src=x_ref)] + [copy(1 + j, me, (*chip, c), src=x_ref) for j, chip in enumerate(chips)]
        for cp in first:
            cp.start()
        passed = [copy(4 + j, (*chip, c), sibling) for j, chip in enumerate(chips)]
        for j, chip in enumerate(chips):
            copy(1 + j, (*chip, c), me).wait_recv()
            passed[j].start()
        copy(0, sibling, me).wait_recv()
        for j, chip in enumerate(chips):
            copy(4 + j, (*chip, 1 - c), me).wait_recv()
        for cp in first + passed:
            cp.wait_send()
        mine.wait()

    any_spec = pl.BlockSpec(memory_space=pl.ANY)
    return pl.pallas_call(
        body, name=name, in_specs=[any_spec], out_specs=any_spec,
        out_shape=jax.ShapeDtypeStruct((N_DEV,) + shard.shape, shard.dtype),
        scratch_shapes=[pltpu.SemaphoreType.DMA((N_DEV - 1,)), pltpu.SemaphoreType.DMA((N_DEV - 1,)), pltpu.SemaphoreType.DMA(())],
    )(shard)


def _all_reduce_small(part, *, loss_rows, loss_scale):
    rows = part.shape[0]
    n_peer = N_DEV - 1

    def body(p_ref, o_ref, buf, send_sems, recv_sems):
        me = _my_index()
        peers = _peers()
        buf[me] = p_ref[...]
        sends = [pltpu.make_async_remote_copy(src_ref=p_ref, dst_ref=buf.at[me], send_sem=send_sems.at[rel], recv_sem=recv_sems.at[rel],
                                              device_id=peers[rel][0], device_id_type=MESH) for rel in range(n_peer)]
        for cp in sends:
            cp.start()
        for rel in range(n_peer):
            pltpu.make_async_remote_copy(src_ref=p_ref, dst_ref=buf.at[peers[rel][1]], send_sem=send_sems.at[rel], recv_sem=recv_sems.at[rel],
                                         device_id=peers[rel][0], device_id_type=MESH).wait_recv()
        for cp in sends:
            cp.wait_send()
        total = buf[0]
        for dev in range(1, N_DEV):
            total = total + buf[dev]
        o_ref[...] = total
        squares = total[rows - loss_rows:]
        loss = jnp.sum(jnp.sum(squares, axis=0, keepdims=True), axis=-1, keepdims=True) * loss_scale
        o_ref[rows - loss_rows:, :] = jnp.broadcast_to(loss, (loss_rows, LANES))

    vmem = pl.BlockSpec(memory_space=pltpu.VMEM)
    return pl.pallas_call(
        body, name="all_reduce_small", in_specs=[vmem], out_specs=vmem, out_shape=jax.ShapeDtypeStruct(part.shape, F32),
        scratch_shapes=[pltpu.VMEM((N_DEV, rows, LANES), F32), pltpu.SemaphoreType.DMA((n_peer,)), pltpu.SemaphoreType.DMA((n_peer,))],
        compiler_params=pltpu.CompilerParams(has_side_effects=True, vmem_limit_bytes=VMEM_LIMIT),
    )(part)


def _adamw_math(w, g, m, v):
    m_new = ADAM_B1 * m + (1.0 - ADAM_B1) * g
    v_new = ADAM_B2 * v + (1.0 - ADAM_B2) * (g * g)
    m_hat = m_new / (1.0 - ADAM_B1 ** ADAM_STEP)
    v_hat = v_new / (1.0 - ADAM_B2 ** ADAM_STEP)
    delta = -ADAM_LR * (m_hat / (jnp.sqrt(v_hat) + ADAM_EPS) + ADAM_WD * w)
    return delta, m_new, v_new


def _adamw(parts, w, m, v, *, name, tr=64):
    rows, cols = w.shape
    tr = min(tr, rows)
    while rows % (2 * tr) == 0 and 2 * tr * cols <= 256 * 1024:
        tr *= 2
    assert rows % tr == 0
    stacked = parts.ndim == 3

    def body(p_ref, w_ref, m_ref, v_ref, g_ref, d_ref, mo_ref, vo_ref):
        if stacked:
            g = p_ref[0].astype(F32)
            for dev in range(1, N_DEV):
                g = g + p_ref[dev].astype(F32)
        else:
            g = p_ref[...]
        delta, m_new, v_new = _adamw_math(w_ref[...], g, m_ref[...], v_ref[...])
        g_ref[...] = g
        d_ref[...] = delta
        mo_ref[...] = m_new
        vo_ref[...] = v_new

    tile = pl.BlockSpec((tr, cols), lambda i: (i, 0))
    p_spec = pl.BlockSpec((N_DEV, tr, cols), lambda i: (0, i, 0)) if stacked else tile
    return pl.pallas_call(
        body, name=name, grid=(rows // tr,), in_specs=[p_spec, tile, tile, tile], out_specs=[tile] * 4,
        out_shape=[jax.ShapeDtypeStruct((rows, cols), F32)] * 4, compiler_params=_params(1),
    )(parts, w, m, v)


_LATER = ("w_out", "w_cq", "w_ckv", "w_co", "w_ff1", "w_ff2")


def _as_rows(stacked):
    return stacked.reshape(-1, stacked.shape[-1])


def _local_step(x, mem, target, small, shards):
    batch, seq, d = x.shape
    n_mem = mem.shape[1]
    t = batch * seq
    x2, mem2, tgt2 = x.reshape(t, d), mem.reshape(batch * n_mem, d), target.reshape(t, d)
    g_mix, g_cross, g_mem, g_ffn, g_final = (small[k] for k in ("norm_mix_g", "norm_cross_g", "norm_mem_g", "norm_ffn_g", "norm_final_g"))
    gv, hg, w_sp, b_sp_t = small["gm_v_norm_g"], small["head_norm_g"], small["w_spatial"], small["b_spatial_t"]

    win_t = _as_rows(_gather_two_level(shards["w_in"], name="gather_w_in"))
    proj, xn = _norm_mm(x2, g_mix, win_t, mode="nt", name="proj_fwd", tm=512, tn=win_t.shape[0])
    merged_a = _gmlp_fwd(proj, w_sp, b_sp_t, gv, hg)
    merged, sb_raw, sb_weights, sb_log_rest, gathered = _sb_fwd(proj, merged_a, hg, batch=batch, seq=seq,
                                                                exchange=_Exchange([shards[n] for n in _LATER], scatter=False))
    wout, wcq, wckv_t, wco, wff1_t, wff2 = (_as_rows(g) for g in gathered)
    h1 = _mm(merged, wout, mode="nn", out_dtype=F32, name="mix_out_fwd", epi=_epi_residual, epi_ins=(x2,))
    qx, hn1 = _norm_mm(h1, g_cross, wcq, mode="nn", name="xq_fwd")
    kvx, memn = _norm_mm(mem2, g_mem, wckv_t, mode="nt", name="xkv_fwd", tm=512, tn=wckv_t.shape[0])
    o = _xattn_fwd(qx, kvx, batch=batch, seq=seq, n_mem=n_mem)
    h2 = _mm(o, wco, mode="nn", out_dtype=F32, name="xo_fwd", epi=_epi_residual, epi_ins=(h1,))
    fpre, hn2 = _norm_mm(h2, g_ffn, wff1_t, mode="nt", name="ff1_fwd", tm=512, tn=wff1_t.shape[0])
    dh3, final_rows = _mm(fpre, wff2, mode="nn", out_dtype=F32, name="ff2_fwd_loss", tm=512, tk=wff2.shape[0], a_fn=_relu2,
                          epi=_epi_loss, epi_ins=(h2, tgt2), vec_ins=(g_final,), aux=2)
    dg_final, sq_err = final_rows[0:1], final_rows[1:2]

    dpre = _mm(dh3, wff2, mode="nt", out_dtype=BF16, name="ff2_bwd_x", tm=512, tn=wff2.shape[0], epi=_epi_relu2_grad,
               epi_ins=(fpre,))
    chunk = wff2.shape[0] // N_DEV
    d_wff2_t = _mm(dh3, fpre, mode="tn", out_dtype=BF16, name="ff2_bwd_w", tn=2048, b_fn=_relu2, col_chunk=chunk)
    d_wff1 = _mm(hn2, dpre, mode="tn", out_dtype=BF16, name="ff1_bwd_w", tn=2048, col_chunk=chunk)
    dh2, dg_ffn = _mm(dpre, wff1_t, mode="nn", out_dtype=F32, name="ff1_bwd_x", tm=512, tk=wff1_t.shape[0], epi=_epi_rms_bwd,
                      epi_ins=(h2, dh3), vec_ins=(g_ffn,), aux=True)
    do = _mm(dh2, wco, mode="nt", out_dtype=BF16, name="xo_bwd_x")
    d_wco = _mm(o, dh2, mode="tn", out_dtype=BF16, name="xo_bwd_w")
    dqx, dkvx = _xattn_bwd(qx, kvx, do, batch=batch, seq=seq, n_mem=n_mem)
    d_wcq = _mm(hn1, dqx, mode="tn", out_dtype=BF16, name="xq_bwd_w")
    dh1, dg_cross = _mm(dqx, wcq, mode="nt", out_dtype=F32, name="xq_bwd_x", tm=512, epi=_epi_rms_bwd,
                        epi_ins=(h1, dh2), vec_ins=(g_cross,), aux=True)
    d_wckv_t = _mm(dkvx, memn, mode="tn", out_dtype=BF16, name="xkv_bwd_w")
    _, dg_mem = _mm(dkvx, wckv_t, mode="nn", out_dtype=BF16, name="xkv_bwd_x", tm=512, epi=_epi_rms_gain_only,
                    epi_ins=(mem2,), vec_ins=(g_mem,), aux=True)
    dmerged = _mm(dh1, wout, mode="nt", out_dtype=BF16, name="mix_out_bwd_x")
    d_wout = _mm(merged, dh1, mode="tn", out_dtype=BF16, name="mix_out_bwd_w")
    dp_a, d_wsp, d_bsp_t, d_gv, d_hg_a = _gmlp_bwd(proj, dmerged, w_sp, b_sp_t, gv, hg)
    d_later = {"w_out": d_wout, "w_cq": d_wcq, "w_ckv": d_wckv_t, "w_co": d_wco, "w_ff1": d_wff1, "w_ff2": d_wff2_t}
    scatter = _Exchange([g if g.ndim == 3 else g.reshape(N_DEV, -1, d) for g in (d_later[n] for n in _LATER)], scatter=True)
    dq, dk, dv, d_hg_b, received = _sb_bwd(proj, sb_raw, sb_weights, sb_log_rest, dmerged, hg, batch=batch, seq=seq, exchange=scatter)
    dproj = jnp.concatenate([dp_a, dq, dk, dv], axis=1)
    d_win_t = _mm(xn, dproj, mode="tn", out_dtype=BF16, name="proj_bwd_w", tn=dproj.shape[1] // 2).T
    dx, dg_mix, d_win_received = _mm(dproj, win_t, mode="nn", out_dtype=F32, name="proj_bwd_x", tm=512, tk=win_t.shape[0],
                                     epi=_epi_rms_bwd, epi_ins=(x2, dh1), vec_ins=(g_mix,), aux=True,
                                     exchange=_Exchange([d_win_t.reshape(N_DEV, -1, d)], scatter=True))

    d_small = {"norm_mix_g": dg_mix, "gm_v_norm_g": d_gv, "w_spatial": d_wsp, "b_spatial_t": d_bsp_t, "head_norm_g": jnp.concatenate([d_hg_a, d_hg_b], axis=1),
               "norm_cross_g": dg_cross, "norm_mem_g": dg_mem, "norm_ffn_g": dg_ffn, "norm_final_g": dg_final}
    d_big = dict(zip(_LATER, received))
    d_big["w_in"] = d_win_received
    return sq_err, dx.reshape(batch, seq, d), d_small, d_big


_BIG = ("w_in", "w_out", "w_cq", "w_ckv", "w_co", "w_ff1", "w_ff2")
_GATHERED_TRANSPOSED = ("w_in", "w_ckv", "w_ff1")
_UPDATED_TRANSPOSED = ("w_in", "w_ckv", "w_ff2")
_SMALL = ("norm_mix_g", "gm_v_norm_g", "w_spatial", "b_spatial", "head_norm_g", "norm_cross_g", "norm_mem_g", "norm_ffn_g", "norm_final_g")
_NAMES = ("norm_mix_g", "w_in", "gm_v_norm_g", "w_spatial", "b_spatial", "head_norm_g", "w_out", "norm_cross_g", "norm_mem_g",
          "w_cq", "w_ckv", "w_co", "norm_ffn_g", "w_ff1", "w_ff2", "norm_final_g")


def _rows_of(a):
    r = a.reshape(-1, LANES)
    pad = (-r.shape[0]) % 8
    return jnp.pad(r, ((0, pad), (0, 0))) if pad else r


def _shard2d(a, transposed):
    return a[0].T if transposed else a[0]


def kernel(x, mem, norm_mix_g, w_in, gm_v_norm_g, w_spatial, b_spatial, head_norm_g, w_out, norm_cross_g, norm_mem_g, w_cq, w_ckv, w_co, norm_ffn_g, w_ff1, w_ff2, norm_final_g, loss_target, m_norm_mix_g, m_w_in, m_gm_v_norm_g, m_w_spatial, m_b_spatial, m_head_norm_g, m_w_out, m_norm_cross_g, m_norm_mem_g, m_w_cq, m_w_ckv, m_w_co, m_norm_ffn_g, m_w_ff1, m_w_ff2, m_norm_final_g, v_norm_mix_g, v_w_in, v_gm_v_norm_g, v_w_spatial, v_b_spatial, v_head_norm_g, v_w_out, v_norm_cross_g, v_norm_mem_g, v_w_cq, v_w_ckv, v_w_co, v_norm_ffn_g, v_w_ff1, v_w_ff2, v_norm_final_g):
    weights = dict(norm_mix_g=norm_mix_g, w_in=w_in, gm_v_norm_g=gm_v_norm_g, w_spatial=w_spatial, b_spatial=b_spatial,
                   head_norm_g=head_norm_g, w_out=w_out, norm_cross_g=norm_cross_g, norm_mem_g=norm_mem_g, w_cq=w_cq, w_ckv=w_ckv,
                   w_co=w_co, norm_ffn_g=norm_ffn_g, w_ff1=w_ff1, w_ff2=w_ff2, norm_final_g=norm_final_g)
    mom1 = dict(norm_mix_g=m_norm_mix_g, w_in=m_w_in, gm_v_norm_g=m_gm_v_norm_g, w_spatial=m_w_spatial, b_spatial=m_b_spatial,
                head_norm_g=m_head_norm_g, w_out=m_w_out, norm_cross_g=m_norm_cross_g, norm_mem_g=m_norm_mem_g, w_cq=m_w_cq,
                w_ckv=m_w_ckv, w_co=m_w_co, norm_ffn_g=m_norm_ffn_g, w_ff1=m_w_ff1, w_ff2=m_w_ff2, norm_final_g=m_norm_final_g)
    mom2 = dict(norm_mix_g=v_norm_mix_g, w_in=v_w_in, gm_v_norm_g=v_gm_v_norm_g, w_spatial=v_w_spatial, b_spatial=v_b_spatial,
                head_norm_g=v_head_norm_g, w_out=v_w_out, norm_cross_g=v_norm_cross_g, norm_mem_g=v_norm_mem_g, w_cq=v_w_cq,
                w_ckv=v_w_ckv, w_co=v_w_co, norm_ffn_g=v_norm_ffn_g, w_ff1=v_w_ff1, w_ff2=v_w_ff2, norm_final_g=v_norm_final_g)

    shards = {n: _shard2d(weights[n], n in _GATHERED_TRANSPOSED).astype(BF16) for n in _BIG}
    small = {n: weights[n].reshape(1, -1) for n in _SMALL if n not in ("w_spatial", "b_spatial")}
    small["w_spatial"] = w_spatial[0]
    small["b_spatial_t"] = b_spatial[0].T
    sq_err, grad_x, d_small, d_big = _local_step(x, mem, loss_target, small, shards)

    d_small["b_spatial"] = d_small.pop("b_spatial_t").T
    sq_rows = _rows_of(sq_err)
    packed = jnp.concatenate([_rows_of(d_small[n]) for n in _SMALL] + [sq_rows], axis=0)
    summed = _all_reduce_small(packed, loss_rows=sq_rows.shape[0], loss_scale=0.5 / x.shape[-1])

    grads, deltas, new_m, new_v = {}, {}, {}, {}
    for n in _BIG:
        flip = n in _UPDATED_TRANSPOSED
        outs = _adamw(d_big[n], _shard2d(weights[n], flip), _shard2d(mom1[n], flip), _shard2d(mom2[n], flip), name="adamw_" + n)
        outs = [o.T if flip else o for o in outs]
        grads[n], deltas[n], new_m[n], new_v[n] = (o[None] for o in outs)
    pack = lambda src: jnp.concatenate([_rows_of(src[n]) for n in _SMALL], axis=0)
    n_small_rows = sum(_rows_of(weights[n]).shape[0] for n in _SMALL)
    outs = _adamw(summed[:n_small_rows], pack(weights), pack(mom1), pack(mom2), name="adamw_small", tr=n_small_rows)
    at = 0
    for n in _SMALL:
        used = weights[n].size // LANES
        for dst, o in zip((grads, deltas, new_m, new_v), outs):
            dst[n] = o[at:at + used].reshape(weights[n].shape)
        at += _rows_of(weights[n]).shape[0]
    loss = summed[n_small_rows, 0]
    return (loss, grad_x, *[grads[n] for n in _NAMES], *[deltas[n] for n in _NAMES], *[new_m[n] for n in _NAMES],
            *[new_v[n] for n in _NAMES])
```

```python
import math

import jax
import jax.numpy as jnp
from jax import lax
from jax.experimental import pallas as pl
from jax.experimental.pallas import tpu as pltpu

F32 = jnp.float32
BF16 = jnp.bfloat16
EPS = 1e-6
N_DEV = 8
LANES = 128
CHUNK = 128
GM_GROUPS = 4
GM_WIDTH = 512
SB_PAIRS = 4
SB_HEAD_DIM = 64
SB_SCALE = 0.125
SB_TILE = 128
SB_BLOCK = 512
X_HEADS = 4
X_HEAD_DIM = 256
X_SCALE = 1.0 / 16.0
VMEM_LIMIT = 56 * 1024 * 1024
ADAM_LR, ADAM_B1, ADAM_B2, ADAM_EPS, ADAM_WD, ADAM_STEP = 0.001, 0.9, 0.999, 1e-08, 0.01, 10
MESH = pl.DeviceIdType.MESH


def _params(n_axes):
    return pltpu.CompilerParams(dimension_semantics=("arbitrary",) * n_axes, vmem_limit_bytes=VMEM_LIMIT)


def _dot(a, b, dims):
    return lax.dot_general(a, b, (dims, ((), ())), preferred_element_type=F32)


def _nn(a, b):
    return _dot(a, b, ((1,), (0,)))


def _nt(a, b):
    return _dot(a, b, ((1,), (1,)))


def _tn(a, b):
    return _dot(a, b, ((0,), (0,)))


_MODES = {"nn": _nn, "nt": _nt, "tn": _tn}


def _rstd(x):
    return lax.rsqrt(jnp.mean(x * x, axis=-1, keepdims=True) + EPS)


def _gelu(x):
    c = math.sqrt(2.0 / math.pi)
    t = jnp.tanh(c * (x + 0.044715 * x * x * x))
    return 0.5 * x * (1.0 + t)


def _gelu_and_grad(x):
    c = math.sqrt(2.0 / math.pi)
    t = jnp.tanh(c * (x + 0.044715 * x * x * x))
    half = 0.5 * (1.0 + t)
    return x * half, half + 0.5 * x * (1.0 - t * t) * c * (1.0 + 3 * 0.044715 * x * x)


def _split_bf16(x):
    hi = x.astype(BF16)
    lo = (x - hi.astype(F32)).astype(BF16)
    return hi, lo


def _mm(a, b, *, mode, out_dtype, name, tm=1024, tn=1024, tk=1024, a_fn=None, b_fn=None, epi=None, epi_ins=(), vec_ins=(),
        aux=False, col_chunk=None, exchange=None):
    if mode == "nn":
        (m, k), (k2, n) = a.shape, b.shape
    elif mode == "nt":
        (m, k), (n, k2) = a.shape, b.shape
    else:
        (k, m), (k2, n) = a.shape, b.shape
    assert k == k2, (a.shape, b.shape, mode)
    tm, tn, tk = min(tm, m), min(tn, n), min(tk, k)
    assert m % tm == 0 and n % tn == 0 and k % tk == 0, (m, n, k, tm, tn, tk)
    n_m, n_n, n_k = m // tm, n // tn, k // tk
    assert not aux or n_n == 1
    dot = _MODES[mode]
    n_epi, n_vec = len(epi_ins), len(vec_ins)

    def body(*refs):
        ins, outs, scratch, x_refs = _riding(exchange, refs, 2 + n_epi + n_vec, 2 if aux else 1, 1 if n_k > 1 else 0)
        a_ref, b_ref, epi_refs = ins[0], ins[1], ins[2:]
        o_ref = outs[0]
        aux_ref = outs[1] if aux else None
        acc_ref = scratch[0] if n_k > 1 else None
        i, j, kk = pl.program_id(0), pl.program_id(1), pl.program_id(2)
        ride_done = _ride(exchange, x_refs, (i == 0) & (j == 0) & (kk == 0), (i == n_m - 1) & (j == n_n - 1) & (kk == n_k - 1))
        def product():
            av, bv = a_ref[...], b_ref[...]
            if a_fn is not None:
                av = a_fn(av)
            if b_fn is not None:
                bv = b_fn(bv)
            return dot(av.astype(BF16), bv.astype(BF16))

        def finish(acc):
            if col_chunk is not None:
                for ch in range(tn // col_chunk):
                    o_ref[ch] = acc[:, ch * col_chunk:(ch + 1) * col_chunk].astype(out_dtype)
                return
            if epi is None:
                o_ref[...] = acc.astype(out_dtype)
                return
            res = epi(acc, *[r[...] for r in epi_refs])
            if aux:
                res, rows = res[0], res[1:]
                rows = rows[0] if len(rows) == 1 else jnp.concatenate(rows, axis=0)

                @pl.when(i == 0)
                def _():
                    aux_ref[...] = rows

                @pl.when(i != 0)
                def _():
                    aux_ref[...] += rows
            o_ref[...] = res.astype(out_dtype)

        if n_k == 1:
            finish(product())
        else:
            @pl.when(kk == 0)
            def _():
                acc_ref[...] = product()

            @pl.when(kk != 0)
            def _():
                acc_ref[...] += product()

            @pl.when(kk == n_k - 1)
            def _():
                finish(acc_ref[...])

        ride_done()

    if mode == "tn":
        a_spec = pl.BlockSpec((tk, tm), lambda i, j, kk: (kk, i))
    else:
        a_spec = pl.BlockSpec((tm, tk), lambda i, j, kk: (i, kk))
    if mode == "nt":
        b_spec = pl.BlockSpec((tn, tk), lambda i, j, kk: (j, kk))
    else:
        b_spec = pl.BlockSpec((tk, tn), lambda i, j, kk: (kk, j))
    tile_spec = pl.BlockSpec((tm, tn), lambda i, j, kk: (i, j))
    row_spec = pl.BlockSpec((1, tn), lambda i, j, kk: (0, j))
    out_shape = [jax.ShapeDtypeStruct((m, n), out_dtype)]
    out_specs = [tile_spec]
    if col_chunk is not None:
        assert epi is None and not aux and tn % col_chunk == 0
        out_shape = [jax.ShapeDtypeStruct((n // col_chunk, m, col_chunk), out_dtype)]
        out_specs = [pl.BlockSpec((tn // col_chunk, tm, col_chunk), lambda i, j, kk: (j, i, 0))]
    if aux:
        out_shape.append(jax.ShapeDtypeStruct((int(aux), n), F32))
        out_specs.append(pl.BlockSpec((int(aux), tn), lambda i, j, kk: (0, j)))
    x_in, x_out, x_shape, x_scratch, x_arrays = _riding_specs(exchange)
    res = pl.pallas_call(
        body, name=name, grid=(n_m, n_n, n_k),
        in_specs=[a_spec, b_spec] + [tile_spec] * n_epi + [row_spec] * n_vec + x_in,
        out_specs=out_specs + x_out, out_shape=out_shape + x_shape,
        scratch_shapes=([pltpu.VMEM((tm, tn), F32)] if n_k > 1 else []) + x_scratch,
        compiler_params=_params(3),
    )(a, b, *epi_ins, *vec_ins, *x_arrays)
    if exchange is not None:
        return tuple(res)
    return res if aux else res[0]


def _norm_mm(x, g, w, *, mode, name, tm=1024, tn=1024):
    m, d = x.shape
    n = w.shape[0] if mode == "nt" else w.shape[1]
    tm, tn = min(tm, m), min(tn, n)
    assert m % tm == 0 and n % tn == 0
    dot = _MODES[mode]

    def body(x_ref, g_ref, w_ref, o_ref, xn_ref, xn_s):
        @pl.when(pl.program_id(1) == 0)
        def _():
            xv = x_ref[...]
            xn = (xv * _rstd(xv) * g_ref[...]).astype(BF16)
            xn_s[...] = xn
            xn_ref[...] = xn

        o_ref[...] = dot(xn_s[...], w_ref[...]).astype(BF16)

    w_spec = pl.BlockSpec((tn, d), lambda i, j: (j, 0)) if mode == "nt" else pl.BlockSpec((d, tn), lambda i, j: (0, j))
    return pl.pallas_call(
        body, name=name, grid=(m // tm, n // tn),
        in_specs=[pl.BlockSpec((tm, d), lambda i, j: (i, 0)), pl.BlockSpec((1, d), lambda i, j: (0, 0)), w_spec],
        out_specs=[pl.BlockSpec((tm, tn), lambda i, j: (i, j)), pl.BlockSpec((tm, d), lambda i, j: (i, 0))],
        out_shape=[jax.ShapeDtypeStruct((m, n), BF16), jax.ShapeDtypeStruct((m, d), BF16)],
        scratch_shapes=[pltpu.VMEM((tm, d), BF16)],
        compiler_params=_params(2),
    )(x, g, w)


def _epi_residual(acc, res):
    return res + acc


def _epi_relu2_grad(acc, pre):
    return acc * (2.0 * jnp.maximum(pre.astype(F32), 0.0))


def _relu2(pre):
    r = jnp.maximum(pre.astype(F32), 0.0)
    return r * r


def _epi_rms_bwd(acc, h, dres, g):
    r = _rstd(h)
    xh = h * r
    dxh = acc * g
    dh = dres + r * (dxh - xh * jnp.mean(dxh * xh, axis=-1, keepdims=True))
    return dh, jnp.sum(acc * xh, axis=0, keepdims=True)


def _epi_loss(acc, h_in, target, g):
    h = h_in + acc
    r = _rstd(h)
    xh = h * r
    err = xh * g - target
    dy = err * (1.0 / h.shape[-1])
    dxh = dy * g
    dh = r * (dxh - xh * jnp.mean(dxh * xh, axis=-1, keepdims=True))
    return dh, jnp.sum(dy * xh, axis=0, keepdims=True), jnp.sum(err * err, axis=0, keepdims=True)


def _epi_rms_gain_only(acc, h, g):
    return acc, jnp.sum(acc * (h * _rstd(h)), axis=0, keepdims=True)


def _tril(n):
    row = lax.broadcasted_iota(jnp.int32, (n, n), 0)
    col = lax.broadcasted_iota(jnp.int32, (n, n), 1)
    return col <= row


def _gmlp_fwd(proj, w_sp, b_sp_t, gv, hg, *, rows=512):
    t = proj.shape[0]
    rows = min(rows, t)
    n_c = rows // CHUNK

    def body(u_ref, v_ref, w_ref, bt_ref, gv_ref, hg_ref, m_ref):
        keep = _tril(CHUNK)
        for g in range(GM_GROUPS):
            cols = slice(g * LANES, (g + 1) * LANES)
            wg = jnp.where(keep, w_ref[g], 0.0).astype(BF16)
            u = _gelu(u_ref[:, cols].astype(F32))
            v = _gelu(v_ref[:, cols].astype(F32))
            vn = (v * _rstd(v) * gv_ref[:, cols]).astype(BF16)
            bias = bt_ref[:, g:g + 1]
            for c in range(n_c):
                rs = slice(c * CHUNK, (c + 1) * CHUNK)
                mixed = _nn(wg, vn[rs]) + bias
                a = u[rs] * mixed
                m_ref[rs, cols] = (a * _rstd(a) * hg_ref[:, cols]).astype(BF16)

    full = lambda shape: pl.BlockSpec(shape, lambda i: (0,) * len(shape))
    return pl.pallas_call(
        body, name="gmlp_fwd", grid=(t // rows,),
        in_specs=[pl.BlockSpec((rows, GM_WIDTH), lambda i: (i, 0)), pl.BlockSpec((rows, GM_WIDTH), lambda i: (i, 1)),
                  full((GM_GROUPS, CHUNK, CHUNK)), full((CHUNK, GM_GROUPS)), full((1, GM_WIDTH)), full((1, GM_WIDTH))],
        out_specs=pl.BlockSpec((rows, GM_WIDTH), lambda i: (i, 0)),
        out_shape=jax.ShapeDtypeStruct((t, 2 * GM_WIDTH), BF16),
        compiler_params=_params(1),
    )(proj, proj, w_sp, b_sp_t, gv, hg)


def _gmlp_bwd(proj, dmerged, w_sp, b_sp_t, gv, hg, *, rows=512):
    t = proj.shape[0]
    rows = min(rows, t)
    n_c = rows // CHUNK
    n_steps = t // rows

    def body(u_ref, v_ref, dm_ref, w_ref, bt_ref, gv_ref, hg_ref, dp_ref, dw_ref, dbt_ref, dgv_ref, dhg_ref, db_acc):
        step = pl.program_id(0)
        keep = _tril(CHUNK)

        @pl.when(step == 0)
        def _():
            dw_ref[...] = jnp.zeros_like(dw_ref)
            db_acc[...] = jnp.zeros_like(db_acc)
            dgv_ref[...] = jnp.zeros_like(dgv_ref)
            dhg_ref[...] = jnp.zeros_like(dhg_ref)

        for g in range(GM_GROUPS):
            cols = slice(g * LANES, (g + 1) * LANES)
            wg = jnp.where(keep, w_ref[g], 0.0).astype(BF16)
            u, u_slope = _gelu_and_grad(u_ref[:, cols].astype(F32))
            v, v_slope = _gelu_and_grad(v_ref[:, cols].astype(F32))
            r = _rstd(v)
            xh = v * r
            gvg = gv_ref[:, cols]
            hgg = hg_ref[:, cols]
            vn = (xh * gvg).astype(BF16)
            bias = bt_ref[:, g:g + 1]
            dm = dm_ref[:, cols].astype(F32)
            du_parts, dvn_parts = [], []
            dw = jnp.zeros((CHUNK, CHUNK), F32)
            db = jnp.zeros((CHUNK, LANES), F32)
            dhg = jnp.zeros((1, LANES), F32)
            for c in range(n_c):
                rs = slice(c * CHUNK, (c + 1) * CHUNK)
                mixed = _nn(wg, vn[rs]) + bias
                a = u[rs] * mixed
                ra = _rstd(a)
                an = a * ra
                dhg = dhg + jnp.sum(dm[rs] * an, axis=0, keepdims=True)
                dan = dm[rs] * hgg
                da = ra * (dan - an * jnp.mean(dan * an, axis=-1, keepdims=True))
                du_parts.append(da * mixed)
                dmixed = da * u[rs]
                db = db + dmixed
                dmb = dmixed.astype(BF16)
                dw = dw + _nt(dmb, vn[rs])
                dvn_parts.append(_tn(wg, dmb))
            du = jnp.concatenate(du_parts, axis=0)
            dvn = jnp.concatenate(dvn_parts, axis=0)
            dw_ref[g] += dw
            db_acc[g] += db
            dhg_ref[:, cols] += dhg
            dgv_ref[:, cols] += jnp.sum(dvn * xh, axis=0, keepdims=True)
            dxh = dvn * gvg
            dv = r * (dxh - xh * jnp.mean(dxh * xh, axis=-1, keepdims=True))
            dp_ref[:, cols] = (du * u_slope).astype(BF16)
            dp_ref[:, GM_WIDTH + g * LANES:GM_WIDTH + (g + 1) * LANES] = (dv * v_slope).astype(BF16)

        @pl.when(step == n_steps - 1)
        def _():
            for g in range(GM_GROUPS):
                dw_ref[g] = jnp.where(keep, dw_ref[g], 0.0)
                dbt_ref[:, g:g + 1] = jnp.sum(db_acc[g], axis=-1, keepdims=True)

    full = lambda shape: pl.BlockSpec(shape, lambda i: (0,) * len(shape))
    return pl.pallas_call(
        body, name="gmlp_bwd", grid=(n_steps,),
        in_specs=[pl.BlockSpec((rows, GM_WIDTH), lambda i: (i, 0)), pl.BlockSpec((rows, GM_WIDTH), lambda i: (i, 1)),
                  pl.BlockSpec((rows, GM_WIDTH), lambda i: (i, 0)),
                  full((GM_GROUPS, CHUNK, CHUNK)), full((CHUNK, GM_GROUPS)), full((1, GM_WIDTH)), full((1, GM_WIDTH))],
        out_specs=[pl.BlockSpec((rows, 2 * GM_WIDTH), lambda i: (i, 0)), full((GM_GROUPS, CHUNK, CHUNK)),
                   full((CHUNK, GM_GROUPS)), full((1, GM_WIDTH)), full((1, GM_WIDTH))],
        out_shape=[jax.ShapeDtypeStruct((t, proj.shape[1]), BF16), jax.ShapeDtypeStruct((GM_GROUPS, CHUNK, CHUNK), F32),
                   jax.ShapeDtypeStruct((CHUNK, GM_GROUPS), F32), jax.ShapeDtypeStruct((1, GM_WIDTH), F32),
                   jax.ShapeDtypeStruct((1, GM_WIDTH), F32)],
        scratch_shapes=[pltpu.VMEM((GM_GROUPS, CHUNK, LANES), F32)],
        compiler_params=_params(1),
    )(proj, proj, dmerged, w_sp, b_sp_t, gv, hg)


def _sb_logits(z, strict):
    ls = jnp.minimum(z, 0.0) - jnp.log(1.0 + jnp.exp(-jnp.abs(z)))
    l1m = ls - z
    if strict is not None:
        l1m = jnp.where(strict, l1m, 0.0)
    return ls, l1m


def _tri_sums(x, tri):
    hi, lo = _split_bf16(x)
    return _nn(jnp.concatenate([hi, lo], axis=1), jnp.concatenate([tri, tri], axis=0))


def _sb_weights(ls, in_tile, right, strict):
    a = jnp.exp(ls + in_tile + right)
    if strict is not None:
        a = jnp.where(strict, a, 0.0)
    return a


def _sb_masks(q_rows):
    row = lax.broadcasted_iota(jnp.int32, (SB_TILE, SB_TILE), 0)
    col = lax.broadcasted_iota(jnp.int32, (SB_TILE, SB_TILE), 1)
    lane = lax.broadcasted_iota(jnp.int32, (q_rows, LANES), 1)
    return row, col, lane < SB_HEAD_DIM


def _stack_heads(x, first):
    zero = jnp.zeros_like(x)
    return jnp.concatenate([jnp.where(first, x, zero), jnp.where(first, zero, x)], axis=0)


def _stack_heads_t(x_t):
    first_t = lax.broadcasted_iota(jnp.int32, x_t.shape, 0) < SB_HEAD_DIM
    zero = jnp.zeros_like(x_t)
    return jnp.concatenate([jnp.where(first_t, x_t, zero), jnp.where(first_t, zero, x_t)], axis=1).astype(BF16)


def _unstack_heads(x2, first):
    half = x2.shape[0] // 2
    return jnp.where(first, x2[:half], x2[half:])


def _live_rows(x, s, q_rows):
    if s == 0:
        return x
    return jnp.concatenate([x[s * SB_TILE:q_rows], x[q_rows + s * SB_TILE:]], axis=0)


def _spread_rows(x, s, q_rows):
    if s == 0:
        return x
    half = q_rows - s * SB_TILE
    zero = jnp.zeros((s * SB_TILE,) + x.shape[1:], x.dtype)
    return jnp.concatenate([zero, x[:half], zero, x[half:]], axis=0)


def _stacked_col_minus_row(q_rows):
    row = lax.broadcasted_iota(jnp.int32, (2 * q_rows, SB_TILE), 0)
    col = lax.broadcasted_iota(jnp.int32, (2 * q_rows, SB_TILE), 1)
    return col - (row & (q_rows - 1))


def _head_mean(x, first):
    s0 = jnp.sum(jnp.where(first, x, 0.0), axis=-1, keepdims=True)
    s1 = jnp.sum(jnp.where(first, 0.0, x), axis=-1, keepdims=True)
    return jnp.where(first, s0, s1) * (1.0 / SB_HEAD_DIM)


def _riding(exchange, refs, n_in, n_out, n_scratch):
    n_x = exchange.n if exchange is not None else 0
    ins, rest = refs[:n_in], refs[n_in:]
    x_src, rest = rest[:n_x], rest[n_x:]
    outs, rest = rest[:n_out], rest[n_out:]
    x_dst, rest = rest[:n_x], rest[n_x:]
    return ins, outs, rest[:n_scratch], (x_src, x_dst, rest[n_scratch:])


def _riding_specs(exchange):
    if exchange is None:
        return [], [], [], [], []
    return exchange.in_specs, exchange.out_specs, exchange.out_shape, exchange.scratch, exchange.arrays


def _ride(exchange, x_refs, first_step, last_step):
    if exchange is None:
        return lambda: None

    @pl.when(first_step)
    def _():
        exchange.start(*x_refs)

    def finish():
        @pl.when(last_step)
        def _():
            exchange.wait(*x_refs)

    return finish


def _sb_fwd(proj, merged_a, hg, *, batch, seq, exchange=None):
    q0, k0, v0 = 2 * GM_WIDTH // LANES, 2 * GM_WIDTH // LANES + SB_PAIRS, 2 * GM_WIDTH // LANES + 2 * SB_PAIRS
    q_rows = block_keys = min(SB_BLOCK, seq)
    assert seq % block_keys == 0
    n_q, n_sub, n_blocks = seq // q_rows, block_keys // SB_TILE, seq // block_keys

    def body(*refs):
        (q_ref, k_ref, v_ref, hg_ref, _), (m_ref, raw_ref, a_ref, l_ref), _, x_refs = _riding(exchange, refs, 5, 4, 0)
        b, p, i = pl.program_id(0), pl.program_id(1), pl.program_id(2)
        finish = _ride(exchange, x_refs, (b == 0) & (p == 0) & (i == 0), (b == batch - 1) & (p == SB_PAIRS - 1) & (i == n_q - 1))
        row, col, first = _sb_masks(q_rows)
        upper = (row > col).astype(BF16)
        q2 = _stack_heads((q_ref[...].astype(F32) * SB_SCALE).astype(BF16), first)
        diff = _stacked_col_minus_row(q_rows)
        last = i

        def key_tile(jb, s):
            return k_ref[pl.ds(pl.multiple_of((jb * n_sub + s) * SB_TILE, SB_TILE), SB_TILE), :]

        def scores(jb):
            return tuple(_nt(q2, key_tile(jb, s)) for s in range(n_sub))

        def keep_for_backward(ref, jb, s, stacked):
            cols = slice(s * SB_TILE, (s + 1) * SB_TILE)
            ref[0, 0, jb, 0, :, cols] = stacked[:q_rows]
            ref[0, 0, jb, 1, :, cols] = stacked[q_rows:]

        def weights_of(jb, z, right):
            logits = [_sb_logits(z[s], None) for s in range(n_sub)]
            totals = [jnp.sum(l1m, axis=-1, keepdims=True) for _, l1m in logits]
            sums = [_tri_sums(l1m, upper) for _, l1m in logits]
            for s in reversed(range(n_sub)):
                keep_for_backward(a_ref, jb, s, _sb_weights(logits[s][0], sums[s], right, None).astype(BF16))
                keep_for_backward(l_ref, jb, s, logits[s][1].astype(BF16))
                right = right + totals[s]
            return right

        def diagonal_weights():
            keeps = [_live_rows(diff, s, q_rows) < -s * SB_TILE for s in range(n_sub)]
            logits = [_sb_logits(_nt(_live_rows(q2, s, q_rows), key_tile(last, s)), keeps[s]) for s in range(n_sub)]
            totals = [jnp.sum(l1m, axis=-1, keepdims=True) for _, l1m in logits]
            sums = [_tri_sums(l1m, upper) for _, l1m in logits]
            right = jnp.zeros((2 * q_rows, 1), F32)
            for s in reversed(range(n_sub)):
                live = _sb_weights(logits[s][0], sums[s], _live_rows(right, s, q_rows), keeps[s]).astype(BF16)
                keep_for_backward(a_ref, last, s, _spread_rows(live, s, q_rows))
                keep_for_backward(l_ref, last, s, _spread_rows(logits[s][1].astype(BF16), s, q_rows))
                right = right + _spread_rows(totals[s], s, q_rows)
            return right

        def weighted_values(jb):
            weights = jnp.concatenate([a_ref[0, 0, jb, 0], a_ref[0, 0, jb, 1]], axis=0)
            return _nn(weights, v_ref[pl.ds(pl.multiple_of(jb * block_keys, block_keys), block_keys), :])

        right = diagonal_weights()
        z_next = scores(jnp.maximum(last - 1, 0))

        def step(k, carry):
            right, acc, z = carry
            jb = last - k
            acc = acc + weighted_values(jb + 1)
            z_next = scores(jnp.maximum(jb - 1, 0))
            return weights_of(jb, z, right), acc, z_next

        _, acc2, _ = lax.fori_loop(1, last + 1, step, (right, jnp.zeros((2 * q_rows, LANES), F32), z_next))
        acc = _unstack_heads(acc2 + weighted_values(0), first)
        raw_ref[...] = acc
        m_ref[...] = (acc * lax.rsqrt(_head_mean(acc * acc, first) + EPS) * hg_ref[...]).astype(BF16)
        finish()

    t = batch * seq
    blk = lambda c0: pl.BlockSpec((q_rows, LANES), lambda b, p, i: (b * n_q + i, c0 + p))
    kv = lambda c0: pl.BlockSpec((seq, LANES), lambda b, p, i: (b, c0 + p))
    kept = pl.BlockSpec((1, 1, n_blocks, 2, q_rows, block_keys), lambda b, p, i: (p, b, 0, 0, i, 0))
    kept_shape = jax.ShapeDtypeStruct((SB_PAIRS, batch, n_blocks, 2, seq, block_keys), BF16)
    x_in, x_out, x_shape, x_scratch, x_arrays = _riding_specs(exchange)
    res = pl.pallas_call(
        body, name="sb_fwd", grid=(batch, SB_PAIRS, n_q),
        in_specs=[blk(q0), kv(k0), kv(v0), pl.BlockSpec((1, LANES), lambda b, p, i: (0, SB_PAIRS + p)),
                  pl.BlockSpec(memory_space=pl.ANY)] + x_in,
        out_specs=[blk(SB_PAIRS), blk(0), kept, kept] + x_out,
        out_shape=[jax.ShapeDtypeStruct((t, 2 * GM_WIDTH), BF16), jax.ShapeDtypeStruct((t, GM_WIDTH), F32), kept_shape, kept_shape] + x_shape,
        scratch_shapes=x_scratch,
        input_output_aliases={4: 0},
        compiler_params=_params(3),
    )(proj, proj, proj, hg, merged_a, *x_arrays)
    return res[0], res[1], res[2], res[3], res[4:]


def _sb_bwd(proj, raw, weights, log_rest, dmerged, hg, *, batch, seq, exchange=None):
    q0, k0, v0 = 2 * GM_WIDTH // LANES, 2 * GM_WIDTH // LANES + SB_PAIRS, 2 * GM_WIDTH // LANES + 2 * SB_PAIRS
    q_rows = block_keys = min(SB_BLOCK, seq)
    assert seq % block_keys == 0
    n_q, n_sub, n_blocks = seq // q_rows, block_keys // SB_TILE, seq // block_keys

    def body(*refs):
        ins, outs, (dk_acc, dv_acc), x_refs = _riding(exchange, refs, 8, 4, 2)
        q_ref, k_ref, v_ref, raw_ref, a_ref, l_ref, dm_ref, hg_ref = ins
        dq_ref, dk_ref, dv_ref, dhg_ref = outs
        p, b, i = pl.program_id(0), pl.program_id(1), pl.program_id(2)
        finish = _ride(exchange, x_refs, (b == 0) & (p == 0) & (i == 0), (b == batch - 1) & (p == SB_PAIRS - 1) & (i == n_q - 1))
        row, col, first = _sb_masks(q_rows)
        lower = (row < col).astype(BF16)

        @pl.when(jnp.logical_and(b == 0, i == 0))
        def _():
            dhg_ref[...] = jnp.zeros_like(dhg_ref)

        @pl.when(i == 0)
        def _():
            dk_acc[...] = jnp.zeros_like(dk_acc)
            dv_acc[...] = jnp.zeros_like(dv_acc)

        raw_v = raw_ref[...]
        dm = dm_ref[...].astype(F32)
        r = lax.rsqrt(_head_mean(raw_v * raw_v, first) + EPS)
        nrm = raw_v * r
        dhg_ref[...] += jnp.sum(dm * nrm, axis=0, keepdims=True)
        dn = dm * hg_ref[...]
        dout = r * (dn - nrm * _head_mean(dn * nrm, first))
        dout2 = _stack_heads(dout.astype(BF16), first)
        q2_t = _stack_heads_t(q_ref[...].astype(F32).T)
        dout2_t = _stack_heads_t(dout.T)
        diff = _stacked_col_minus_row(q_rows)
        last = i

        def kept(ref, jb, cols):
            return jnp.concatenate([ref[0, 0, jb, 0, :, cols], ref[0, 0, jb, 1, :, cols]], axis=0)

        def block(jb, carry, diagonal):
            gleft, dq = carry
            live = (lambda x, s: _live_rows(x, s, q_rows)) if diagonal else (lambda x, s: x)
            spread = (lambda x, s: _spread_rows(x, s, q_rows)) if diagonal else (lambda x, s: x)
            tiles = [pl.ds(pl.multiple_of((jb * n_sub + s) * SB_TILE, SB_TILE), SB_TILE) for s in range(n_sub)]
            cols = [slice(s * SB_TILE, (s + 1) * SB_TILE) for s in range(n_sub)]
            gmats = [_nt(live(dout2, s), v_ref[tiles[s], :]) * live(kept(a_ref, jb, cols[s]), s).astype(F32) for s in range(n_sub)]
            prefixes = [_tri_sums(g, lower) for g in gmats]
            dzs = []
            for s in range(n_sub):
                one_minus = jnp.exp(live(kept(l_ref, jb, cols[s]), s).astype(F32))
                dz = (gmats[s] * one_minus - (live(gleft, s) + prefixes[s]) * (1.0 - one_minus)) * SB_SCALE
                gleft = gleft + spread(jnp.sum(gmats[s], axis=-1, keepdims=True), s)
                if diagonal:
                    dz = jnp.where(live(diff, s) < -s * SB_TILE, dz, 0.0)
                dzs.append(spread(dz.astype(BF16), s))
            dz_all = jnp.concatenate(dzs, axis=1)
            dk_acc[jb] += _nn(q2_t, dz_all)
            dv_acc[jb] += _nn(dout2_t, kept(a_ref, jb, slice(None)))
            return gleft, dq + _nn(dz_all, k_ref[pl.ds(pl.multiple_of(jb * block_keys, block_keys), block_keys), :])

        carry = (jnp.zeros((2 * q_rows, 1), F32), jnp.zeros((2 * q_rows, LANES), F32))
        carry = lax.fori_loop(0, last, lambda jb, c: block(jb, c, False), carry)
        dq_ref[...] = _unstack_heads(block(last, carry, True)[1], first).astype(BF16)

        @pl.when(i == n_q - 1)
        def _():
            for jb in range(n_blocks):
                for s in range(n_sub):
                    rows = slice((jb * n_sub + s) * SB_TILE, (jb * n_sub + s + 1) * SB_TILE)
                    cols = slice(s * SB_TILE, (s + 1) * SB_TILE)
                    dk_ref[rows, :] = dk_acc[jb, :, cols].T.astype(BF16)
                    dv_ref[rows, :] = dv_acc[jb, :, cols].T.astype(BF16)

        finish()

    t = batch * seq
    blk = lambda c0: pl.BlockSpec((q_rows, LANES), lambda p, b, i: (b * n_q + i, c0 + p))
    kv = lambda c0: pl.BlockSpec((seq, LANES), lambda p, b, i: (b, c0 + p))
    row_spec = pl.BlockSpec((1, LANES), lambda p, b, i: (0, SB_PAIRS + p))
    kept_spec = pl.BlockSpec((1, 1, n_blocks, 2, q_rows, block_keys), lambda p, b, i: (p, b, 0, 0, i, 0))
    x_in, x_out, x_shape, x_scratch, x_arrays = _riding_specs(exchange)
    res = pl.pallas_call(
        body, name="sb_bwd", grid=(SB_PAIRS, batch, n_q),
        in_specs=[blk(q0), kv(k0), kv(v0), blk(0), kept_spec, kept_spec, blk(SB_PAIRS), row_spec] + x_in,
        out_specs=[blk(0), kv(0), kv(0), pl.BlockSpec((1, LANES), lambda p, b, i: (0, p))] + x_out,
        out_shape=[jax.ShapeDtypeStruct((t, GM_WIDTH), BF16)] * 3 + [jax.ShapeDtypeStruct((1, GM_WIDTH), F32)] + x_shape,
        scratch_shapes=[pltpu.VMEM((n_blocks, LANES, block_keys), F32), pltpu.VMEM((n_blocks, LANES, block_keys), F32)] + x_scratch,
        compiler_params=_params(3),
    )(proj, proj, proj, raw, weights, log_rest, dmerged, hg, *x_arrays)
    return res[0], res[1], res[2], res[3], res[4:]


def _x_softmax(s):
    s = s * X_SCALE
    p = jnp.exp(s - jnp.max(s, axis=-1, keepdims=True))
    return p * (1.0 / jnp.sum(p, axis=-1, keepdims=True))


def _x_heads(width):
    return [slice(h * X_HEAD_DIM, (h + 1) * X_HEAD_DIM) for h in range(X_HEADS)], \
           [slice(width + h * X_HEAD_DIM, width + (h + 1) * X_HEAD_DIM) for h in range(X_HEADS)]


def _xattn_fwd(q, kv, *, batch, seq, n_mem, tq=512):
    tq = min(tq, seq)
    n_q = seq // tq
    d = X_HEADS * X_HEAD_DIM

    def body(q_ref, kv_ref, o_ref):
        kcols, vcols = _x_heads(d)
        scores = [_nt(q_ref[:, c], kv_ref[:, c]) for c in kcols]
        probs = [_x_softmax(s).astype(BF16) for s in scores]
        for p, c, vc in zip(probs, kcols, vcols):
            o_ref[:, c] = _nn(p, kv_ref[:, vc]).astype(BF16)

    return pl.pallas_call(
        body, name="xattn_fwd", grid=(batch, n_q),
        in_specs=[pl.BlockSpec((tq, d), lambda b, i: (b * n_q + i, 0)), pl.BlockSpec((n_mem, 2 * d), lambda b, i: (b, 0))],
        out_specs=pl.BlockSpec((tq, d), lambda b, i: (b * n_q + i, 0)),
        out_shape=jax.ShapeDtypeStruct((batch * seq, d), BF16),
        compiler_params=_params(2),
    )(q, kv)


def _xattn_bwd(q, kv, do, *, batch, seq, n_mem, tq=512):
    tq = min(tq, seq)
    n_q = seq // tq
    d = X_HEADS * X_HEAD_DIM

    def body(q_ref, kv_ref, do_ref, dq_ref, dkv_ref, acc):
        i = pl.program_id(1)

        @pl.when(i == 0)
        def _():
            acc[...] = jnp.zeros_like(acc)

        kcols, vcols = _x_heads(d)
        scores = [_nt(q_ref[:, c], kv_ref[:, c]) for c in kcols]
        d_probs = [_nt(do_ref[:, c], kv_ref[:, vc]) for c, vc in zip(kcols, vcols)]
        probs = [_x_softmax(s) for s in scores]
        d_scores = [(p * (dp - jnp.sum(dp * p, axis=-1, keepdims=True)) * X_SCALE).astype(BF16) for p, dp in zip(probs, d_probs)]
        for p, ds, c, vc in zip(probs, d_scores, kcols, vcols):
            acc[:, vc] += _tn(p.astype(BF16), do_ref[:, c])
            dq_ref[:, c] = _nn(ds, kv_ref[:, c]).astype(BF16)
            acc[:, c] += _tn(ds, q_ref[:, c])

        @pl.when(i == n_q - 1)
        def _():
            dkv_ref[...] = acc[...].astype(BF16)

    return pl.pallas_call(
        body, name="xattn_bwd", grid=(batch, n_q),
        in_specs=[pl.BlockSpec((tq, d), lambda b, i: (b * n_q + i, 0)), pl.BlockSpec((n_mem, 2 * d), lambda b, i: (b, 0)),
                  pl.BlockSpec((tq, d), lambda b, i: (b * n_q + i, 0))],
        out_specs=[pl.BlockSpec((tq, d), lambda b, i: (b * n_q + i, 0)), pl.BlockSpec((n_mem, 2 * d), lambda b, i: (b, 0))],
        out_shape=[jax.ShapeDtypeStruct((batch * seq, d), BF16), jax.ShapeDtypeStruct((batch * n_mem, 2 * d), BF16)],
        scratch_shapes=[pltpu.VMEM((n_mem, 2 * d), F32)],
        compiler_params=_params(2),
    )(q, kv, do)


def _my_index():
    return 4 * lax.axis_index("x") + 2 * lax.axis_index("y") + lax.axis_index("c")


def _peers():
    x, y, c = lax.axis_index("x"), lax.axis_index("y"), lax.axis_index("c")
    out = []
    for rel in range(1, N_DEV):
        dx, dy, dc = (rel >> 2) & 1, (rel >> 1) & 1, rel & 1
        px, py, pc = x ^ dx, y ^ dy, c ^ dc
        out.append(((px, py, pc), 4 * px + 2 * py + pc))
    return out


class _Exchange:
    def __init__(self, arrays, scatter):
        self.arrays, self.scatter, self.n = list(arrays), scatter, len(arrays)
        any_spec = pl.BlockSpec(memory_space=pl.ANY)
        self.in_specs = [any_spec] * self.n
        self.out_specs = [any_spec] * self.n
        self.out_shape = [jax.ShapeDtypeStruct((N_DEV,) + tuple(a.shape[-2:]), a.dtype) for a in self.arrays]
        n_peer = N_DEV - 1
        self.scratch = [pltpu.SemaphoreType.DMA((self.n, n_peer)), pltpu.SemaphoreType.DMA((self.n, n_peer)),
                        pltpu.SemaphoreType.DMA((self.n,))]

    def _copies(self, srcs, dsts, sems, arriving):
        send_sems, recv_sems, local_sems = sems
        me = _my_index()
        local, remote = [], []
        for w in range(self.n):
            if not arriving:
                local.append(pltpu.make_async_copy(srcs[w].at[me] if self.scatter else srcs[w], dsts[w].at[me], local_sems.at[w]))
            for rel, (pos, idx) in enumerate(_peers()):
                remote.append(pltpu.make_async_remote_copy(
                    src_ref=srcs[w].at[idx] if self.scatter else srcs[w], dst_ref=dsts[w].at[idx if arriving else me],
                    send_sem=send_sems.at[w, rel], recv_sem=recv_sems.at[w, rel], device_id=pos, device_id_type=MESH))
        return local, remote

    def start(self, srcs, dsts, sems):
        local, sends = self._copies(srcs, dsts, sems, arriving=False)
        for cp in local + sends:
            cp.start()

    def wait(self, srcs, dsts, sems):
        for cp in self._copies(srcs, dsts, sems, arriving=True)[1]:
            cp.wait_recv()
        local, sends = self._copies(srcs, dsts, sems, arriving=False)
        for cp in sends:
            cp.wait_send()
        for cp in local:
            cp.wait()


def _two_level_gather(x_ref, slots_ref, send_sems, recv_sems, local_sem):
    x, y, c = lax.axis_index("x"), lax.axis_index("y"), lax.axis_index("c")
    me, sibling = (x, y, c), (x, y, 1 - c)
    chips = [(1 - x, y), (x, 1 - y), (1 - x, 1 - y)]

    def slot(px, py, pc):
        return slots_ref.at[4 * px + 2 * py + pc]

    def copy(k, block, to, src=None):
        return pltpu.make_async_remote_copy(src_ref=slot(*block) if src is None else src, dst_ref=slot(*block),
                                            send_sem=send_sems.at[k], recv_sem=recv_sems.at[k], device_id=to, device_id_type=MESH)

    mine = pltpu.make_async_copy(x_ref, slot(*me), local_sem)
    mine.start()
    first = [copy(0, me, sibling, src=x_ref)] + [copy(1 + j, me, (*chip, c), src=x_ref) for j, chip in enumerate(chips)]
    for cp in first:
        cp.start()
    passed = [copy(4 + j, (*chip, c), sibling) for j, chip in enumerate(chips)]
    for j, chip in enumerate(chips):
        copy(1 + j, (*chip, c), me).wait_recv()
        passed[j].start()
    copy(0, sibling, me).wait_recv()
    for j, chip in enumerate(chips):
        copy(4 + j, (*chip, 1 - c), me).wait_recv()
    for cp in first + passed:
        cp.wait_send()
    mine.wait()


_TWO_LEVEL_SEMS = [pltpu.SemaphoreType.DMA((N_DEV - 1,)), pltpu.SemaphoreType.DMA((N_DEV - 1,)), pltpu.SemaphoreType.DMA(())]


def _gather_two_level(shard, *, name):
    any_spec = pl.BlockSpec(memory_space=pl.ANY)
    return pl.pallas_call(
        _two_level_gather_body(), name=name, in_specs=[any_spec], out_specs=any_spec,
        out_shape=jax.ShapeDtypeStruct((N_DEV,) + shard.shape, shard.dtype), scratch_shapes=_TWO_LEVEL_SEMS,
    )(shard)


def _two_level_gather_body():
    def body(x_ref, out_ref, send_sems, recv_sems, local_sem):
        _two_level_gather(x_ref, out_ref, send_sems, recv_sems, local_sem)
    return body


def _all_reduce_small(part, *, loss_rows, loss_scale):
    rows = part.shape[0]

    def body(p_ref, o_ref, buf, send_sems, recv_sems, local_sem):
        _two_level_gather(p_ref, buf, send_sems, recv_sems, local_sem)
        total = buf[0]
        for dev in range(1, N_DEV):
            total = total + buf[dev]
        o_ref[...] = total
        squares = total[rows - loss_rows:]
        loss = jnp.sum(jnp.sum(squares, axis=0, keepdims=True), axis=-1, keepdims=True) * loss_scale
        o_ref[rows - loss_rows:, :] = jnp.broadcast_to(loss, (loss_rows, LANES))

    vmem = pl.BlockSpec(memory_space=pltpu.VMEM)
    return pl.pallas_call(
        body, name="all_reduce_small", in_specs=[vmem], out_specs=vmem, out_shape=jax.ShapeDtypeStruct(part.shape, F32),
        scratch_shapes=[pltpu.VMEM((N_DEV, rows, LANES), F32)] + _TWO_LEVEL_SEMS,
        compiler_params=pltpu.CompilerParams(has_side_effects=True, vmem_limit_bytes=VMEM_LIMIT),
    )(part)


def _adamw_math(w, g, m, v):
    m_new = ADAM_B1 * m + (1.0 - ADAM_B1) * g
    v_new = ADAM_B2 * v + (1.0 - ADAM_B2) * (g * g)
    m_hat = m_new / (1.0 - ADAM_B1 ** ADAM_STEP)
    v_hat = v_new / (1.0 - ADAM_B2 ** ADAM_STEP)
    delta = -ADAM_LR * (m_hat / (jnp.sqrt(v_hat) + ADAM_EPS) + ADAM_WD * w)
    return delta, m_new, v_new


def _adamw(parts, w, m, v, *, name, tr=64):
    rows, cols = w.shape
    tr = min(tr, rows)
    while rows % (2 * tr) == 0 and 2 * tr * cols <= 256 * 1024:
        tr *= 2
    assert rows % tr == 0
    stacked = parts.ndim == 3

    def body(p_ref, w_ref, m_ref, v_ref, g_ref, d_ref, mo_ref, vo_ref):
        if stacked:
            g = p_ref[0].astype(F32)
            for dev in range(1, N_DEV):
                g = g + p_ref[dev].astype(F32)
        else:
            g = p_ref[...]
        delta, m_new, v_new = _adamw_math(w_ref[...], g, m_ref[...], v_ref[...])
        g_ref[...] = g
        d_ref[...] = delta
        mo_ref[...] = m_new
        vo_ref[...] = v_new

    tile = pl.BlockSpec((tr, cols), lambda i: (i, 0))
    p_spec = pl.BlockSpec((N_DEV, tr, cols), lambda i: (0, i, 0)) if stacked else tile
    return pl.pallas_call(
        body, name=name, grid=(rows // tr,), in_specs=[p_spec, tile, tile, tile], out_specs=[tile] * 4,
        out_shape=[jax.ShapeDtypeStruct((rows, cols), F32)] * 4, compiler_params=_params(1),
    )(parts, w, m, v)


_LATER = ("w_out", "w_cq", "w_ckv", "w_co", "w_ff1", "w_ff2")


def _as_rows(stacked):
    return stacked.reshape(-1, stacked.shape[-1])


def _local_step(x, mem, target, small, shards):
    batch, seq, d = x.shape
    n_mem = mem.shape[1]
    t = batch * seq
    x2, mem2, tgt2 = x.reshape(t, d), mem.reshape(batch * n_mem, d), target.reshape(t, d)
    g_mix, g_cross, g_mem, g_ffn, g_final = (small[k] for k in ("norm_mix_g", "norm_cross_g", "norm_mem_g", "norm_ffn_g", "norm_final_g"))
    gv, hg, w_sp, b_sp_t = small["gm_v_norm_g"], small["head_norm_g"], small["w_spatial"], small["b_spatial_t"]

    win_t = _as_rows(_gather_two_level(shards["w_in"], name="gather_w_in"))
    proj, xn = _norm_mm(x2, g_mix, win_t, mode="nt", name="proj_fwd", tm=512, tn=win_t.shape[0])
    merged_a = _gmlp_fwd(proj, w_sp, b_sp_t, gv, hg)
    merged, sb_raw, sb_weights, sb_log_rest, gathered = _sb_fwd(proj, merged_a, hg, batch=batch, seq=seq,
                                                                exchange=_Exchange([shards[n] for n in _LATER], scatter=False))
    wout, wcq, wckv_t, wco, wff1_t, wff2 = (_as_rows(g) for g in gathered)
    h1 = _mm(merged, wout, mode="nn", out_dtype=F32, name="mix_out_fwd", epi=_epi_residual, epi_ins=(x2,))
    qx, hn1 = _norm_mm(h1, g_cross, wcq, mode="nn", name="xq_fwd")
    kvx, memn = _norm_mm(mem2, g_mem, wckv_t, mode="nt", name="xkv_fwd", tm=512, tn=wckv_t.shape[0])
    o = _xattn_fwd(qx, kvx, batch=batch, seq=seq, n_mem=n_mem)
    h2 = _mm(o, wco, mode="nn", out_dtype=F32, name="xo_fwd", epi=_epi_residual, epi_ins=(h1,))
    fpre, hn2 = _norm_mm(h2, g_ffn, wff1_t, mode="nt", name="ff1_fwd", tm=512, tn=wff1_t.shape[0])
    dh3, final_rows = _mm(fpre, wff2, mode="nn", out_dtype=F32, name="ff2_fwd_loss", tm=512, tk=wff2.shape[0], a_fn=_relu2,
                          epi=_epi_loss, epi_ins=(h2, tgt2), vec_ins=(g_final,), aux=2)
    dg_final, sq_err = final_rows[0:1], final_rows[1:2]

    dpre = _mm(dh3, wff2, mode="nt", out_dtype=BF16, name="ff2_bwd_x", tm=512, tn=wff2.shape[0], epi=_epi_relu2_grad,
               epi_ins=(fpre,))
    chunk = wff2.shape[0] // N_DEV
    d_wff2_t = _mm(dh3, fpre, mode="tn", out_dtype=BF16, name="ff2_bwd_w", tn=2048, b_fn=_relu2, col_chunk=chunk)
    d_wff1 = _mm(hn2, dpre, mode="tn", out_dtype=BF16, name="ff1_bwd_w", tn=2048, col_chunk=chunk)
    dh2, dg_ffn = _mm(dpre, wff1_t, mode="nn", out_dtype=F32, name="ff1_bwd_x", tm=512, tk=wff1_t.shape[0], epi=_epi_rms_bwd,
                      epi_ins=(h2, dh3), vec_ins=(g_ffn,), aux=True)
    do = _mm(dh2, wco, mode="nt", out_dtype=BF16, name="xo_bwd_x")
    d_wco = _mm(o, dh2, mode="tn", out_dtype=BF16, name="xo_bwd_w")
    dqx, dkvx = _xattn_bwd(qx, kvx, do, batch=batch, seq=seq, n_mem=n_mem)
    d_wcq = _mm(hn1, dqx, mode="tn", out_dtype=BF16, name="xq_bwd_w")
    dh1, dg_cross = _mm(dqx, wcq, mode="nt", out_dtype=F32, name="xq_bwd_x", tm=512, epi=_epi_rms_bwd,
                        epi_ins=(h1, dh2), vec_ins=(g_cross,), aux=True)
    d_wckv_t = _mm(dkvx, memn, mode="tn", out_dtype=BF16, name="xkv_bwd_w")
    _, dg_mem = _mm(dkvx, wckv_t, mode="nn", out_dtype=BF16, name="xkv_bwd_x", tm=512, epi=_epi_rms_gain_only,
                    epi_ins=(mem2,), vec_ins=(g_mem,), aux=True)
    dmerged = _mm(dh1, wout, mode="nt", out_dtype=BF16, name="mix_out_bwd_x")
    d_wout = _mm(merged, dh1, mode="tn", out_dtype=BF16, name="mix_out_bwd_w")
    dp_a, d_wsp, d_bsp_t, d_gv, d_hg_a = _gmlp_bwd(proj, dmerged, w_sp, b_sp_t, gv, hg)
    d_later = {"w_out": d_wout, "w_cq": d_wcq, "w_ckv": d_wckv_t, "w_co": d_wco, "w_ff1": d_wff1, "w_ff2": d_wff2_t}
    scatter = _Exchange([g if g.ndim == 3 else g.reshape(N_DEV, -1, d) for g in (d_later[n] for n in _LATER)], scatter=True)
    dq, dk, dv, d_hg_b, received = _sb_bwd(proj, sb_raw, sb_weights, sb_log_rest, dmerged, hg, batch=batch, seq=seq, exchange=scatter)
    dproj = dp_a
    for part, at in ((dq, 2 * GM_WIDTH), (dk, 3 * GM_WIDTH), (dv, 4 * GM_WIDTH)):
        dproj = lax.dynamic_update_slice(dproj, part, (0, at))
    d_win_t = _mm(xn, dproj, mode="tn", out_dtype=BF16, name="proj_bwd_w", tn=dproj.shape[1] // 2).T
    dx, dg_mix, d_win_received = _mm(dproj, win_t, mode="nn", out_dtype=F32, name="proj_bwd_x", tm=512, tk=win_t.shape[0],
                                     epi=_epi_rms_bwd, epi_ins=(x2, dh1), vec_ins=(g_mix,), aux=True,
                                     exchange=_Exchange([d_win_t.reshape(N_DEV, -1, d)], scatter=True))

    d_small = {"norm_mix_g": dg_mix, "gm_v_norm_g": d_gv, "w_spatial": d_wsp, "b_spatial_t": d_bsp_t, "head_norm_g": jnp.concatenate([d_hg_a, d_hg_b], axis=1),
               "norm_cross_g": dg_cross, "norm_mem_g": dg_mem, "norm_ffn_g": dg_ffn, "norm_final_g": dg_final}
    d_big = dict(zip(_LATER, received))
    d_big["w_in"] = d_win_received
    return sq_err, dx.reshape(batch, seq, d), d_small, d_big


_BIG = ("w_in", "w_out", "w_cq", "w_ckv", "w_co", "w_ff1", "w_ff2")
_GATHERED_TRANSPOSED = ("w_in", "w_ckv", "w_ff1")
_UPDATED_TRANSPOSED = ("w_in", "w_ckv", "w_ff2")
_SMALL = ("norm_mix_g", "gm_v_norm_g", "w_spatial", "b_spatial", "head_norm_g", "norm_cross_g", "norm_mem_g", "norm_ffn_g", "norm_final_g")
_NAMES = ("norm_mix_g", "w_in", "gm_v_norm_g", "w_spatial", "b_spatial", "head_norm_g", "w_out", "norm_cross_g", "norm_mem_g",
          "w_cq", "w_ckv", "w_co", "norm_ffn_g", "w_ff1", "w_ff2", "norm_final_g")


def _rows_of(a):
    r = a.reshape(-1, LANES)
    pad = (-r.shape[0]) % 8
    return jnp.pad(r, ((0, pad), (0, 0))) if pad else r


def _shard2d(a, transposed):
    return a[0].T if transposed else a[0]


def kernel(x, mem, norm_mix_g, w_in, gm_v_norm_g, w_spatial, b_spatial, head_norm_g, w_out, norm_cross_g, norm_mem_g, w_cq, w_ckv, w_co, norm_ffn_g, w_ff1, w_ff2, norm_final_g, loss_target, m_norm_mix_g, m_w_in, m_gm_v_norm_g, m_w_spatial, m_b_spatial, m_head_norm_g, m_w_out, m_norm_cross_g, m_norm_mem_g, m_w_cq, m_w_ckv, m_w_co, m_norm_ffn_g, m_w_ff1, m_w_ff2, m_norm_final_g, v_norm_mix_g, v_w_in, v_gm_v_norm_g, v_w_spatial, v_b_spatial, v_head_norm_g, v_w_out, v_norm_cross_g, v_norm_mem_g, v_w_cq, v_w_ckv, v_w_co, v_norm_ffn_g, v_w_ff1, v_w_ff2, v_norm_final_g):
    weights = dict(norm_mix_g=norm_mix_g, w_in=w_in, gm_v_norm_g=gm_v_norm_g, w_spatial=w_spatial, b_spatial=b_spatial,
                   head_norm_g=head_norm_g, w_out=w_out, norm_cross_g=norm_cross_g, norm_mem_g=norm_mem_g, w_cq=w_cq, w_ckv=w_ckv,
                   w_co=w_co, norm_ffn_g=norm_ffn_g, w_ff1=w_ff1, w_ff2=w_ff2, norm_final_g=norm_final_g)
    mom1 = dict(norm_mix_g=m_norm_mix_g, w_in=m_w_in, gm_v_norm_g=m_gm_v_norm_g, w_spatial=m_w_spatial, b_spatial=m_b_spatial,
                head_norm_g=m_head_norm_g, w_out=m_w_out, norm_cross_g=m_norm_cross_g, norm_mem_g=m_norm_mem_g, w_cq=m_w_cq,
                w_ckv=m_w_ckv, w_co=m_w_co, norm_ffn_g=m_norm_ffn_g, w_ff1=m_w_ff1, w_ff2=m_w_ff2, norm_final_g=m_norm_final_g)
    mom2 = dict(norm_mix_g=v_norm_mix_g, w_in=v_w_in, gm_v_norm_g=v_gm_v_norm_g, w_spatial=v_w_spatial, b_spatial=v_b_spatial,
                head_norm_g=v_head_norm_g, w_out=v_w_out, norm_cross_g=v_norm_cross_g, norm_mem_g=v_norm_mem_g, w_cq=v_w_cq,
                w_ckv=v_w_ckv, w_co=v_w_co, norm_ffn_g=v_norm_ffn_g, w_ff1=v_w_ff1, w_ff2=v_w_ff2, norm_final_g=v_norm_final_g)

    shards = {n: _shard2d(weights[n], n in _GATHERED_TRANSPOSED).astype(BF16) for n in _BIG}
    small = {n: weights[n].reshape(1, -1) for n in _SMALL if n not in ("w_spatial", "b_spatial")}
    small["w_spatial"] = w_spatial[0]
    small["b_spatial_t"] = b_spatial[0].T
    sq_err, grad_x, d_small, d_big = _local_step(x, mem, loss_target, small, shards)

    d_small["b_spatial"] = d_small.pop("b_spatial_t").T
    sq_rows = _rows_of(sq_err)
    packed = jnp.concatenate([_rows_of(d_small[n]) for n in _SMALL] + [sq_rows], axis=0)
    summed = _all_reduce_small(packed, loss_rows=sq_rows.shape[0], loss_scale=0.5 / x.shape[-1])

    grads, deltas, new_m, new_v = {}, {}, {}, {}
    for n in _BIG:
        flip = n in _UPDATED_TRANSPOSED
        outs = _adamw(d_big[n], _shard2d(weights[n], flip), _shard2d(mom1[n], flip), _shard2d(mom2[n], flip), name="adamw_" + n)
        outs = [o.T if flip else o for o in outs]
        grads[n], deltas[n], new_m[n], new_v[n] = (o[None] for o in outs)
    pack = lambda src: jnp.concatenate([_rows_of(src[n]) for n in _SMALL], axis=0)
    n_small_rows = sum(_rows_of(weights[n]).shape[0] for n in _SMALL)
    outs = _adamw(summed[:n_small_rows], pack(weights), pack(mom1), pack(mom2), name="adamw_small", tr=n_small_rows)
    at = 0
    for n in _SMALL:
        used = weights[n].size // LANES
        for dst, o in zip((grads, deltas, new_m, new_v), outs):
            dst[n] = o[at:at + used].reshape(weights[n].shape)
        at += _rows_of(weights[n]).shape[0]
    loss = summed[n_small_rows, 0]
    return (loss, grad_x, *[grads[n] for n in _NAMES], *[deltas[n] for n in _NAMES], *[new_m[n] for n in _NAMES],
            *[new_v[n] for n in _NAMES])
```

```python
import math

import jax
import jax.numpy as jnp
from jax import lax
from jax.experimental import pallas as pl
from jax.experimental.pallas import tpu as pltpu

F32 = jnp.float32
BF16 = jnp.bfloat16
EPS = 1e-6
N_DEV = 8
LANES = 128
CHUNK = 128
GM_GROUPS = 4
GM_WIDTH = 512
SB_PAIRS = 4
SB_HEAD_DIM = 64
SB_SCALE = 0.125
SB_TILE = 128
SB_BLOCK = 512
X_HEADS = 4
X_HEAD_DIM = 256
X_SCALE = 1.0 / 16.0
VMEM_LIMIT = 56 * 1024 * 1024
ADAM_LR, ADAM_B1, ADAM_B2, ADAM_EPS, ADAM_WD, ADAM_STEP = 0.001, 0.9, 0.999, 1e-08, 0.01, 10
MESH = pl.DeviceIdType.MESH


def _params(n_axes):
    return pltpu.CompilerParams(dimension_semantics=("arbitrary",) * n_axes, vmem_limit_bytes=VMEM_LIMIT)


def _dot(a, b, dims):
    return lax.dot_general(a, b, (dims, ((), ())), preferred_element_type=F32)


def _nn(a, b):
    return _dot(a, b, ((1,), (0,)))


def _nt(a, b):
    return _dot(a, b, ((1,), (1,)))


def _tn(a, b):
    return _dot(a, b, ((0,), (0,)))


_MODES = {"nn": _nn, "nt": _nt, "tn": _tn}


def _rstd(x):
    return lax.rsqrt(jnp.mean(x * x, axis=-1, keepdims=True) + EPS)


def _gelu(x):
    c = math.sqrt(2.0 / math.pi)
    t = jnp.tanh(c * (x + 0.044715 * x * x * x))
    return 0.5 * x * (1.0 + t)


def _gelu_and_grad(x):
    c = math.sqrt(2.0 / math.pi)
    t = jnp.tanh(c * (x + 0.044715 * x * x * x))
    half = 0.5 * (1.0 + t)
    return x * half, half + 0.5 * x * (1.0 - t * t) * c * (1.0 + 3 * 0.044715 * x * x)


def _split_bf16(x):
    hi = x.astype(BF16)
    lo = (x - hi.astype(F32)).astype(BF16)
    return hi, lo


def _mm(a, b, *, mode, out_dtype, name, tm=1024, tn=1024, tk=1024, a_fn=None, b_fn=None, epi=None, epi_ins=(), vec_ins=(),
        whole_ins=(), more_outs=(), aux=False, col_chunk=None, exchange=None):
    if mode == "nn":
        (m, k), (k2, n) = a.shape, b.shape
    elif mode == "nt":
        (m, k), (n, k2) = a.shape, b.shape
    else:
        (k, m), (k2, n) = a.shape, b.shape
    assert k == k2, (a.shape, b.shape, mode)
    tm, tn, tk = min(tm, m), min(tn, n), min(tk, k)
    assert m % tm == 0 and n % tn == 0 and k % tk == 0, (m, n, k, tm, tn, tk)
    n_m, n_n, n_k = m // tm, n // tn, k // tk
    assert not (aux or more_outs) or n_n == 1
    dot = _MODES[mode]
    n_epi, n_vec, n_whole, n_more = len(epi_ins), len(vec_ins), len(whole_ins), len(more_outs)

    def body(*refs):
        ins, outs, scratch, x_refs = _riding(exchange, refs, 2 + n_epi + n_vec + n_whole, 1 + n_more + (1 if aux else 0),
                                             1 if n_k > 1 else 0)
        a_ref, b_ref, epi_refs = ins[0], ins[1], ins[2:]
        o_ref, more_refs = outs[0], outs[1:1 + n_more]
        aux_ref = outs[1 + n_more] if aux else None
        acc_ref = scratch[0] if n_k > 1 else None
        i, j, kk = pl.program_id(0), pl.program_id(1), pl.program_id(2)
        ride_done = _ride(exchange, x_refs, (i == 0) & (j == 0) & (kk == 0), (i == n_m - 1) & (j == n_n - 1) & (kk == n_k - 1))
        def product():
            av, bv = a_ref[...], b_ref[...]
            if a_fn is not None:
                av = a_fn(av)
            if b_fn is not None:
                bv = b_fn(bv)
            return dot(av.astype(BF16), bv.astype(BF16))

        def finish(acc):
            if col_chunk is not None:
                for ch in range(tn // col_chunk):
                    o_ref[ch] = acc[:, ch * col_chunk:(ch + 1) * col_chunk].astype(out_dtype)
                return
            if epi is None:
                o_ref[...] = acc.astype(out_dtype)
                return
            res = epi(acc, *[r[...] for r in epi_refs])
            if n_more:
                for more_ref, value in zip(more_refs, res[1:1 + n_more]):
                    more_ref[...] = value.astype(more_ref.dtype)
                res = (res[0],) + tuple(res[1 + n_more:]) if aux else res[0]
            if aux:
                res, rows = res[0], res[1:]
                rows = rows[0] if len(rows) == 1 else jnp.concatenate(rows, axis=0)

                @pl.when(i == 0)
                def _():
                    aux_ref[...] = rows

                @pl.when(i != 0)
                def _():
                    aux_ref[...] += rows
            o_ref[...] = res.astype(out_dtype)

        if n_k == 1:
            finish(product())
        else:
            @pl.when(kk == 0)
            def _():
                acc_ref[...] = product()

            @pl.when(kk != 0)
            def _():
                acc_ref[...] += product()

            @pl.when(kk == n_k - 1)
            def _():
                finish(acc_ref[...])

        ride_done()

    if mode == "tn":
        a_spec = pl.BlockSpec((tk, tm), lambda i, j, kk: (kk, i))
    else:
        a_spec = pl.BlockSpec((tm, tk), lambda i, j, kk: (i, kk))
    if mode == "nt":
        b_spec = pl.BlockSpec((tn, tk), lambda i, j, kk: (j, kk))
    else:
        b_spec = pl.BlockSpec((tk, tn), lambda i, j, kk: (kk, j))
    tile_spec = pl.BlockSpec((tm, tn), lambda i, j, kk: (i, j))
    row_spec = pl.BlockSpec((1, tn), lambda i, j, kk: (0, j))
    out_shape = [jax.ShapeDtypeStruct((m, n), out_dtype)]
    out_specs = [tile_spec]
    if col_chunk is not None:
        assert epi is None and not aux and tn % col_chunk == 0
        out_shape = [jax.ShapeDtypeStruct((n // col_chunk, m, col_chunk), out_dtype)]
        out_specs = [pl.BlockSpec((tn // col_chunk, tm, col_chunk), lambda i, j, kk: (j, i, 0))]
    for columns, dtype in more_outs:
        out_shape.append(jax.ShapeDtypeStruct((m, columns), dtype))
        out_specs.append(pl.BlockSpec((tm, columns), lambda i, j, kk: (i, 0)))
    if aux:
        out_shape.append(jax.ShapeDtypeStruct((int(aux), n), F32))
        out_specs.append(pl.BlockSpec((int(aux), tn), lambda i, j, kk: (0, j)))
    whole_specs = [pl.BlockSpec(w.shape, lambda i, j, kk, nd=w.ndim: (0,) * nd) for w in whole_ins]
    x_in, x_out, x_shape, x_scratch, x_arrays = _riding_specs(exchange)
    res = pl.pallas_call(
        body, name=name, grid=(n_m, n_n, n_k),
        in_specs=[a_spec, b_spec] + [tile_spec] * n_epi + [row_spec] * n_vec + whole_specs + x_in,
        out_specs=out_specs + x_out, out_shape=out_shape + x_shape,
        scratch_shapes=([pltpu.VMEM((tm, tn), F32)] if n_k > 1 else []) + x_scratch,
        compiler_params=_params(3),
    )(a, b, *epi_ins, *vec_ins, *whole_ins, *x_arrays)
    if exchange is not None or more_outs:
        return tuple(res)
    return res if aux else res[0]


def _norm_mm(x, g, w, *, mode, name, tm=1024, tn=1024):
    m, d = x.shape
    n = w.shape[0] if mode == "nt" else w.shape[1]
    tm, tn = min(tm, m), min(tn, n)
    assert m % tm == 0 and n % tn == 0
    dot = _MODES[mode]

    def body(x_ref, g_ref, w_ref, o_ref, xn_ref, xn_s):
        @pl.when(pl.program_id(1) == 0)
        def _():
            xv = x_ref[...]
            xn = (xv * _rstd(xv) * g_ref[...]).astype(BF16)
            xn_s[...] = xn
            xn_ref[...] = xn

        o_ref[...] = dot(xn_s[...], w_ref[...]).astype(BF16)

    w_spec = pl.BlockSpec((tn, d), lambda i, j: (j, 0)) if mode == "nt" else pl.BlockSpec((d, tn), lambda i, j: (0, j))
    return pl.pallas_call(
        body, name=name, grid=(m // tm, n // tn),
        in_specs=[pl.BlockSpec((tm, d), lambda i, j: (i, 0)), pl.BlockSpec((1, d), lambda i, j: (0, 0)), w_spec],
        out_specs=[pl.BlockSpec((tm, tn), lambda i, j: (i, j)), pl.BlockSpec((tm, d), lambda i, j: (i, 0))],
        out_shape=[jax.ShapeDtypeStruct((m, n), BF16), jax.ShapeDtypeStruct((m, d), BF16)],
        scratch_shapes=[pltpu.VMEM((tm, d), BF16)],
        compiler_params=_params(2),
    )(x, g, w)


def _epi_residual(acc, res):
    return res + acc


def _epi_relu2_grad(acc, pre):
    return acc * (2.0 * jnp.maximum(pre.astype(F32), 0.0))


def _relu2(pre):
    r = jnp.maximum(pre.astype(F32), 0.0)
    return r * r


def _epi_rms_bwd(acc, h, dres, g):
    r = _rstd(h)
    xh = h * r
    dxh = acc * g
    dh = dres + r * (dxh - xh * jnp.mean(dxh * xh, axis=-1, keepdims=True))
    return dh, jnp.sum(acc * xh, axis=0, keepdims=True)


def _epi_residual_then_norm_mm(mode):
    dot = _MODES[mode]

    def epi(acc, res, g, w):
        h = res + acc
        hn = (h * _rstd(h) * g).astype(BF16)
        return h, dot(hn, w), hn

    return epi


def _epi_rms_bwd_then_mm(acc, h, dres, g, w):
    dh, row = _epi_rms_bwd(acc, h, dres, g)
    return dh, _nt(dh.astype(BF16), w), row


def _epi_loss(acc, h_in, target, g):
    h = h_in + acc
    r = _rstd(h)
    xh = h * r
    err = xh * g - target
    dy = err * (1.0 / h.shape[-1])
    dxh = dy * g
    dh = r * (dxh - xh * jnp.mean(dxh * xh, axis=-1, keepdims=True))
    return dh, jnp.sum(dy * xh, axis=0, keepdims=True), jnp.sum(err * err, axis=0, keepdims=True)


def _epi_rms_gain_only(acc, h, g):
    return acc, jnp.sum(acc * (h * _rstd(h)), axis=0, keepdims=True)


def _tril(n):
    row = lax.broadcasted_iota(jnp.int32, (n, n), 0)
    col = lax.broadcasted_iota(jnp.int32, (n, n), 1)
    return col <= row


def _gmlp_fwd(proj, w_sp, b_sp_t, gv, hg, *, rows=512):
    t = proj.shape[0]
    rows = min(rows, t)
    n_c = rows // CHUNK

    def body(u_ref, v_ref, w_ref, bt_ref, gv_ref, hg_ref, m_ref):
        keep = _tril(CHUNK)
        for g in range(GM_GROUPS):
            cols = slice(g * LANES, (g + 1) * LANES)
            wg = jnp.where(keep, w_ref[g], 0.0).astype(BF16)
            u = _gelu(u_ref[:, cols].astype(F32))
            v = _gelu(v_ref[:, cols].astype(F32))
            vn = (v * _rstd(v) * gv_ref[:, cols]).astype(BF16)
            bias = bt_ref[:, g:g + 1]
            for c in range(n_c):
                rs = slice(c * CHUNK, (c + 1) * CHUNK)
                mixed = _nn(wg, vn[rs]) + bias
                a = u[rs] * mixed
                m_ref[rs, cols] = (a * _rstd(a) * hg_ref[:, cols]).astype(BF16)

    full = lambda shape: pl.BlockSpec(shape, lambda i: (0,) * len(shape))
    return pl.pallas_call(
        body, name="gmlp_fwd", grid=(t // rows,),
        in_specs=[pl.BlockSpec((rows, GM_WIDTH), lambda i: (i, 0)), pl.BlockSpec((rows, GM_WIDTH), lambda i: (i, 1)),
                  full((GM_GROUPS, CHUNK, CHUNK)), full((CHUNK, GM_GROUPS)), full((1, GM_WIDTH)), full((1, GM_WIDTH))],
        out_specs=pl.BlockSpec((rows, GM_WIDTH), lambda i: (i, 0)),
        out_shape=jax.ShapeDtypeStruct((t, 2 * GM_WIDTH), BF16),
        compiler_params=_params(1),
    )(proj, proj, w_sp, b_sp_t, gv, hg)


def _gmlp_bwd(proj, dmerged, w_sp, b_sp_t, gv, hg, *, rows=512):
    t = proj.shape[0]
    rows = min(rows, t)
    n_c = rows // CHUNK
    n_steps = t // rows

    def body(u_ref, v_ref, dm_ref, w_ref, bt_ref, gv_ref, hg_ref, dp_ref, dw_ref, dbt_ref, dgv_ref, dhg_ref, db_acc):
        step = pl.program_id(0)
        keep = _tril(CHUNK)

        @pl.when(step == 0)
        def _():
            dw_ref[...] = jnp.zeros_like(dw_ref)
            db_acc[...] = jnp.zeros_like(db_acc)
            dgv_ref[...] = jnp.zeros_like(dgv_ref)
            dhg_ref[...] = jnp.zeros_like(dhg_ref)

        for g in range(GM_GROUPS):
            cols = slice(g * LANES, (g + 1) * LANES)
            wg = jnp.where(keep, w_ref[g], 0.0).astype(BF16)
            u, u_slope = _gelu_and_grad(u_ref[:, cols].astype(F32))
            v, v_slope = _gelu_and_grad(v_ref[:, cols].astype(F32))
            r = _rstd(v)
            xh = v * r
            gvg = gv_ref[:, cols]
            hgg = hg_ref[:, cols]
            vn = (xh * gvg).astype(BF16)
            bias = bt_ref[:, g:g + 1]
            dm = dm_ref[:, cols].astype(F32)
            du_parts, dvn_parts = [], []
            dw = jnp.zeros((CHUNK, CHUNK), F32)
            db = jnp.zeros((CHUNK, LANES), F32)
            dhg = jnp.zeros((1, LANES), F32)
            for c in range(n_c):
                rs = slice(c * CHUNK, (c + 1) * CHUNK)
                mixed = _nn(wg, vn[rs]) + bias
                a = u[rs] * mixed
                ra = _rstd(a)
                an = a * ra
                dhg = dhg + jnp.sum(dm[rs] * an, axis=0, keepdims=True)
                dan = dm[rs] * hgg
                da = ra * (dan - an * jnp.mean(dan * an, axis=-1, keepdims=True))
                du_parts.append(da * mixed)
                dmixed = da * u[rs]
                db = db + dmixed
                dmb = dmixed.astype(BF16)
                dw = dw + _nt(dmb, vn[rs])
                dvn_parts.append(_tn(wg, dmb))
            du = jnp.concatenate(du_parts, axis=0)
            dvn = jnp.concatenate(dvn_parts, axis=0)
            dw_ref[g] += dw
            db_acc[g] += db
            dhg_ref[:, cols] += dhg
            dgv_ref[:, cols] += jnp.sum(dvn * xh, axis=0, keepdims=True)
            dxh = dvn * gvg
            dv = r * (dxh - xh * jnp.mean(dxh * xh, axis=-1, keepdims=True))
            dp_ref[:, cols] = (du * u_slope).astype(BF16)
            dp_ref[:, GM_WIDTH + g * LANES:GM_WIDTH + (g + 1) * LANES] = (dv * v_slope).astype(BF16)

        @pl.when(step == n_steps - 1)
        def _():
            for g in range(GM_GROUPS):
                dw_ref[g] = jnp.where(keep, dw_ref[g], 0.0)
                dbt_ref[:, g:g + 1] = jnp.sum(db_acc[g], axis=-1, keepdims=True)

    full = lambda shape: pl.BlockSpec(shape, lambda i: (0,) * len(shape))
    return pl.pallas_call(
        body, name="gmlp_bwd", grid=(n_steps,),
        in_specs=[pl.BlockSpec((rows, GM_WIDTH), lambda i: (i, 0)), pl.BlockSpec((rows, GM_WIDTH), lambda i: (i, 1)),
                  pl.BlockSpec((rows, GM_WIDTH), lambda i: (i, 0)),
                  full((GM_GROUPS, CHUNK, CHUNK)), full((CHUNK, GM_GROUPS)), full((1, GM_WIDTH)), full((1, GM_WIDTH))],
        out_specs=[pl.BlockSpec((rows, 2 * GM_WIDTH), lambda i: (i, 0)), full((GM_GROUPS, CHUNK, CHUNK)),
                   full((CHUNK, GM_GROUPS)), full((1, GM_WIDTH)), full((1, GM_WIDTH))],
        out_shape=[jax.ShapeDtypeStruct((t, proj.shape[1]), BF16), jax.ShapeDtypeStruct((GM_GROUPS, CHUNK, CHUNK), F32),
                   jax.ShapeDtypeStruct((CHUNK, GM_GROUPS), F32), jax.ShapeDtypeStruct((1, GM_WIDTH), F32),
                   jax.ShapeDtypeStruct((1, GM_WIDTH), F32)],
        scratch_shapes=[pltpu.VMEM((GM_GROUPS, CHUNK, LANES), F32)],
        compiler_params=_params(1),
    )(proj, proj, dmerged, w_sp, b_sp_t, gv, hg)


def _sb_logits(z, strict):
    ls = jnp.minimum(z, 0.0) - jnp.log(1.0 + jnp.exp(-jnp.abs(z)))
    l1m = ls - z
    if strict is not None:
        l1m = jnp.where(strict, l1m, 0.0)
    return ls, l1m


def _tri_sums(x, tri):
    hi, lo = _split_bf16(x)
    return _nn(jnp.concatenate([hi, lo], axis=1), jnp.concatenate([tri, tri], axis=0))


def _sb_weights(ls, in_tile, right, strict):
    a = jnp.exp(ls + in_tile + right)
    if strict is not None:
        a = jnp.where(strict, a, 0.0)
    return a


def _sb_masks(q_rows):
    row = lax.broadcasted_iota(jnp.int32, (SB_TILE, SB_TILE), 0)
    col = lax.broadcasted_iota(jnp.int32, (SB_TILE, SB_TILE), 1)
    lane = lax.broadcasted_iota(jnp.int32, (q_rows, LANES), 1)
    return row, col, lane < SB_HEAD_DIM


def _stack_heads(x, first):
    zero = jnp.zeros_like(x)
    return jnp.concatenate([jnp.where(first, x, zero), jnp.where(first, zero, x)], axis=0)


def _stack_heads_t(x_t):
    first_t = lax.broadcasted_iota(jnp.int32, x_t.shape, 0) < SB_HEAD_DIM
    zero = jnp.zeros_like(x_t)
    return jnp.concatenate([jnp.where(first_t, x_t, zero), jnp.where(first_t, zero, x_t)], axis=1).astype(BF16)


def _unstack_heads(x2, first):
    half = x2.shape[0] // 2
    return jnp.where(first, x2[:half], x2[half:])


def _live_rows(x, s, q_rows):
    if s == 0:
        return x
    return jnp.concatenate([x[s * SB_TILE:q_rows], x[q_rows + s * SB_TILE:]], axis=0)


def _spread_rows(x, s, q_rows):
    if s == 0:
        return x
    half = q_rows - s * SB_TILE
    zero = jnp.zeros((s * SB_TILE,) + x.shape[1:], x.dtype)
    return jnp.concatenate([zero, x[:half], zero, x[half:]], axis=0)


def _stacked_col_minus_row(q_rows):
    row = lax.broadcasted_iota(jnp.int32, (2 * q_rows, SB_TILE), 0)
    col = lax.broadcasted_iota(jnp.int32, (2 * q_rows, SB_TILE), 1)
    return col - (row & (q_rows - 1))


def _head_mean(x, first):
    s0 = jnp.sum(jnp.where(first, x, 0.0), axis=-1, keepdims=True)
    s1 = jnp.sum(jnp.where(first, 0.0, x), axis=-1, keepdims=True)
    return jnp.where(first, s0, s1) * (1.0 / SB_HEAD_DIM)


def _riding(exchange, refs, n_in, n_out, n_scratch):
    n_x = exchange.n if exchange is not None else 0
    ins, rest = refs[:n_in], refs[n_in:]
    x_src, rest = rest[:n_x], rest[n_x:]
    outs, rest = rest[:n_out], rest[n_out:]
    x_dst, rest = rest[:n_x], rest[n_x:]
    return ins, outs, rest[:n_scratch], (x_src, x_dst, rest[n_scratch:])


def _riding_specs(exchange):
    if exchange is None:
        return [], [], [], [], []
    return exchange.in_specs, exchange.out_specs, exchange.out_shape, exchange.scratch, exchange.arrays


def _ride(exchange, x_refs, first_step, last_step):
    if exchange is None:
        return lambda: None

    @pl.when(first_step)
    def _():
        exchange.start(*x_refs)

    def finish():
        @pl.when(last_step)
        def _():
            exchange.wait(*x_refs)

    return finish


def _sb_fwd(proj, merged_a, hg, *, batch, seq, exchange=None):
    q0, k0, v0 = 2 * GM_WIDTH // LANES, 2 * GM_WIDTH // LANES + SB_PAIRS, 2 * GM_WIDTH // LANES + 2 * SB_PAIRS
    q_rows = block_keys = min(SB_BLOCK, seq)
    assert seq % block_keys == 0
    n_q, n_sub, n_blocks = seq // q_rows, block_keys // SB_TILE, seq // block_keys

    def body(*refs):
        (q_ref, k_ref, v_ref, hg_ref, _), (m_ref, raw_ref, a_ref, l_ref), _, x_refs = _riding(exchange, refs, 5, 4, 0)
        b, p, i = pl.program_id(0), pl.program_id(1), pl.program_id(2)
        finish = _ride(exchange, x_refs, (b == 0) & (p == 0) & (i == 0), (b == batch - 1) & (p == SB_PAIRS - 1) & (i == n_q - 1))
        row, col, first = _sb_masks(q_rows)
        upper = (row > col).astype(BF16)
        q2 = _stack_heads((q_ref[...].astype(F32) * SB_SCALE).astype(BF16), first)
        diff = _stacked_col_minus_row(q_rows)
        last = i

        def key_tile(jb, s):
            return k_ref[pl.ds(pl.multiple_of((jb * n_sub + s) * SB_TILE, SB_TILE), SB_TILE), :]

        def scores(jb):
            return tuple(_nt(q2, key_tile(jb, s)) for s in range(n_sub))

        def keep_for_backward(ref, jb, s, stacked):
            cols = slice(s * SB_TILE, (s + 1) * SB_TILE)
            ref[0, 0, jb, 0, :, cols] = stacked[:q_rows]
            ref[0, 0, jb, 1, :, cols] = stacked[q_rows:]

        def weights_of(jb, z, right):
            logits = [_sb_logits(z[s], None) for s in range(n_sub)]
            totals = [jnp.sum(l1m, axis=-1, keepdims=True) for _, l1m in logits]
            sums = [_tri_sums(l1m, upper) for _, l1m in logits]
            for s in reversed(range(n_sub)):
                keep_for_backward(a_ref, jb, s, _sb_weights(logits[s][0], sums[s], right, None).astype(BF16))
                keep_for_backward(l_ref, jb, s, logits[s][1].astype(BF16))
                right = right + totals[s]
            return right

        def diagonal_weights():
            keeps = [_live_rows(diff, s, q_rows) < -s * SB_TILE for s in range(n_sub)]
            logits = [_sb_logits(_nt(_live_rows(q2, s, q_rows), key_tile(last, s)), keeps[s]) for s in range(n_sub)]
            totals = [jnp.sum(l1m, axis=-1, keepdims=True) for _, l1m in logits]
            sums = [_tri_sums(l1m, upper) for _, l1m in logits]
            right = jnp.zeros((2 * q_rows, 1), F32)
            for s in reversed(range(n_sub)):
                live = _sb_weights(logits[s][0], sums[s], _live_rows(right, s, q_rows), keeps[s]).astype(BF16)
                keep_for_backward(a_ref, last, s, _spread_rows(live, s, q_rows))
                keep_for_backward(l_ref, last, s, _spread_rows(logits[s][1].astype(BF16), s, q_rows))
                right = right + _spread_rows(totals[s], s, q_rows)
            return right

        def weighted_values(jb):
            weights = jnp.concatenate([a_ref[0, 0, jb, 0], a_ref[0, 0, jb, 1]], axis=0)
            return _nn(weights, v_ref[pl.ds(pl.multiple_of(jb * block_keys, block_keys), block_keys), :])

        right = diagonal_weights()
        z_next = scores(jnp.maximum(last - 1, 0))

        def step(k, carry):
            right, acc, z = carry
            jb = last - k
            acc = acc + weighted_values(jb + 1)
            z_next = scores(jnp.maximum(jb - 1, 0))
            return weights_of(jb, z, right), acc, z_next

        _, acc2, _ = lax.fori_loop(1, last + 1, step, (right, jnp.zeros((2 * q_rows, LANES), F32), z_next))
        acc = _unstack_heads(acc2 + weighted_values(0), first)
        raw_ref[...] = acc
        m_ref[...] = (acc * lax.rsqrt(_head_mean(acc * acc, first) + EPS) * hg_ref[...]).astype(BF16)
        finish()

    t = batch * seq
    blk = lambda c0: pl.BlockSpec((q_rows, LANES), lambda b, p, i: (b * n_q + i, c0 + p))
    kv = lambda c0: pl.BlockSpec((seq, LANES), lambda b, p, i: (b, c0 + p))
    kept = pl.BlockSpec((1, 1, n_blocks, 2, q_rows, block_keys), lambda b, p, i: (p, b, 0, 0, i, 0))
    kept_shape = jax.ShapeDtypeStruct((SB_PAIRS, batch, n_blocks, 2, seq, block_keys), BF16)
    x_in, x_out, x_shape, x_scratch, x_arrays = _riding_specs(exchange)
    res = pl.pallas_call(
        body, name="sb_fwd", grid=(batch, SB_PAIRS, n_q),
        in_specs=[blk(q0), kv(k0), kv(v0), pl.BlockSpec((1, LANES), lambda b, p, i: (0, SB_PAIRS + p)),
                  pl.BlockSpec(memory_space=pl.ANY)] + x_in,
        out_specs=[blk(SB_PAIRS), blk(0), kept, kept] + x_out,
        out_shape=[jax.ShapeDtypeStruct((t, 2 * GM_WIDTH), BF16), jax.ShapeDtypeStruct((t, GM_WIDTH), F32), kept_shape, kept_shape] + x_shape,
        scratch_shapes=x_scratch,
        input_output_aliases={4: 0},
        compiler_params=_params(3),
    )(proj, proj, proj, hg, merged_a, *x_arrays)
    return res[0], res[1], res[2], res[3], res[4:]


def _sb_bwd(proj, raw, weights, log_rest, dmerged, hg, *, batch, seq, exchange=None):
    q0, k0, v0 = 2 * GM_WIDTH // LANES, 2 * GM_WIDTH // LANES + SB_PAIRS, 2 * GM_WIDTH // LANES + 2 * SB_PAIRS
    q_rows = block_keys = min(SB_BLOCK, seq)
    assert seq % block_keys == 0
    n_q, n_sub, n_blocks = seq // q_rows, block_keys // SB_TILE, seq // block_keys

    def body(*refs):
        ins, outs, (dk_acc, dv_acc), x_refs = _riding(exchange, refs, 8, 4, 2)
        q_ref, k_ref, v_ref, raw_ref, a_ref, l_ref, dm_ref, hg_ref = ins
        dq_ref, dk_ref, dv_ref, dhg_ref = outs
        p, b, i = pl.program_id(0), pl.program_id(1), pl.program_id(2)
        finish = _ride(exchange, x_refs, (b == 0) & (p == 0) & (i == 0), (b == batch - 1) & (p == SB_PAIRS - 1) & (i == n_q - 1))
        row, col, first = _sb_masks(q_rows)
        lower = (row < col).astype(BF16)

        @pl.when(jnp.logical_and(b == 0, i == 0))
        def _():
            dhg_ref[...] = jnp.zeros_like(dhg_ref)

        @pl.when(i == 0)
        def _():
            dk_acc[...] = jnp.zeros_like(dk_acc)
            dv_acc[...] = jnp.zeros_like(dv_acc)

        raw_v = raw_ref[...]
        dm = dm_ref[...].astype(F32)
        r = lax.rsqrt(_head_mean(raw_v * raw_v, first) + EPS)
        nrm = raw_v * r
        dhg_ref[...] += jnp.sum(dm * nrm, axis=0, keepdims=True)
        dn = dm * hg_ref[...]
        dout = r * (dn - nrm * _head_mean(dn * nrm, first))
        dout2 = _stack_heads(dout.astype(BF16), first)
        q2_t = _stack_heads_t(q_ref[...].astype(F32).T)
        dout2_t = _stack_heads_t(dout.T)
        diff = _stacked_col_minus_row(q_rows)
        last = i

        def kept(ref, jb, cols):
            return jnp.concatenate([ref[0, 0, jb, 0, :, cols], ref[0, 0, jb, 1, :, cols]], axis=0)

        def block(jb, carry, diagonal):
            gleft, dq = carry
            live = (lambda x, s: _live_rows(x, s, q_rows)) if diagonal else (lambda x, s: x)
            spread = (lambda x, s: _spread_rows(x, s, q_rows)) if diagonal else (lambda x, s: x)
            tiles = [pl.ds(pl.multiple_of((jb * n_sub + s) * SB_TILE, SB_TILE), SB_TILE) for s in range(n_sub)]
            cols = [slice(s * SB_TILE, (s + 1) * SB_TILE) for s in range(n_sub)]
            gmats = [_nt(live(dout2, s), v_ref[tiles[s], :]) * live(kept(a_ref, jb, cols[s]), s).astype(F32) for s in range(n_sub)]
            prefixes = [_tri_sums(g, lower) for g in gmats]
            dzs = []
            for s in range(n_sub):
                one_minus = jnp.exp(live(kept(l_ref, jb, cols[s]), s).astype(F32))
                dz = (gmats[s] * one_minus - (live(gleft, s) + prefixes[s]) * (1.0 - one_minus)) * SB_SCALE
                gleft = gleft + spread(jnp.sum(gmats[s], axis=-1, keepdims=True), s)
                if diagonal:
                    dz = jnp.where(live(diff, s) < -s * SB_TILE, dz, 0.0)
                dzs.append(spread(dz.astype(BF16), s))
            dz_all = jnp.concatenate(dzs, axis=1)
            dk_acc[jb] += _nn(q2_t, dz_all)
            dv_acc[jb] += _nn(dout2_t, kept(a_ref, jb, slice(None)))
            return gleft, dq + _nn(dz_all, k_ref[pl.ds(pl.multiple_of(jb * block_keys, block_keys), block_keys), :])

        carry = (jnp.zeros((2 * q_rows, 1), F32), jnp.zeros((2 * q_rows, LANES), F32))
        carry = lax.fori_loop(0, last, lambda jb, c: block(jb, c, False), carry)
        dq_ref[...] = _unstack_heads(block(last, carry, True)[1], first).astype(BF16)

        @pl.when(i == n_q - 1)
        def _():
            for jb in range(n_blocks):
                for s in range(n_sub):
                    rows = slice((jb * n_sub + s) * SB_TILE, (jb * n_sub + s + 1) * SB_TILE)
                    cols = slice(s * SB_TILE, (s + 1) * SB_TILE)
                    dk_ref[rows, :] = dk_acc[jb, :, cols].T.astype(BF16)
                    dv_ref[rows, :] = dv_acc[jb, :, cols].T.astype(BF16)

        finish()

    t = batch * seq
    blk = lambda c0: pl.BlockSpec((q_rows, LANES), lambda p, b, i: (b * n_q + i, c0 + p))
    kv = lambda c0: pl.BlockSpec((seq, LANES), lambda p, b, i: (b, c0 + p))
    row_spec = pl.BlockSpec((1, LANES), lambda p, b, i: (0, SB_PAIRS + p))
    kept_spec = pl.BlockSpec((1, 1, n_blocks, 2, q_rows, block_keys), lambda p, b, i: (p, b, 0, 0, i, 0))
    x_in, x_out, x_shape, x_scratch, x_arrays = _riding_specs(exchange)
    res = pl.pallas_call(
        body, name="sb_bwd", grid=(SB_PAIRS, batch, n_q),
        in_specs=[blk(q0), kv(k0), kv(v0), blk(0), kept_spec, kept_spec, blk(SB_PAIRS), row_spec] + x_in,
        out_specs=[blk(0), kv(0), kv(0), pl.BlockSpec((1, LANES), lambda p, b, i: (0, p))] + x_out,
        out_shape=[jax.ShapeDtypeStruct((t, GM_WIDTH), BF16)] * 3 + [jax.ShapeDtypeStruct((1, GM_WIDTH), F32)] + x_shape,
        scratch_shapes=[pltpu.VMEM((n_blocks, LANES, block_keys), F32), pltpu.VMEM((n_blocks, LANES, block_keys), F32)] + x_scratch,
        compiler_params=_params(3),
    )(proj, proj, proj, raw, weights, log_rest, dmerged, hg, *x_arrays)
    return res[0], res[1], res[2], res[3], res[4:]


def _x_softmax(s):
    s = s * X_SCALE
    p = jnp.exp(s - jnp.max(s, axis=-1, keepdims=True))
    return p * (1.0 / jnp.sum(p, axis=-1, keepdims=True))


def _x_heads(width):
    return [slice(h * X_HEAD_DIM, (h + 1) * X_HEAD_DIM) for h in range(X_HEADS)], \
           [slice(width + h * X_HEAD_DIM, width + (h + 1) * X_HEAD_DIM) for h in range(X_HEADS)]


def _xattn_fwd(q, kv, *, batch, seq, n_mem, tq=512):
    tq = min(tq, seq)
    n_q = seq // tq
    d = X_HEADS * X_HEAD_DIM

    def body(q_ref, kv_ref, o_ref):
        kcols, vcols = _x_heads(d)
        scores = [_nt(q_ref[:, c], kv_ref[:, c]) for c in kcols]
        probs = [_x_softmax(s).astype(BF16) for s in scores]
        for p, c, vc in zip(probs, kcols, vcols):
            o_ref[:, c] = _nn(p, kv_ref[:, vc]).astype(BF16)

    return pl.pallas_call(
        body, name="xattn_fwd", grid=(batch, n_q),
        in_specs=[pl.BlockSpec((tq, d), lambda b, i: (b * n_q + i, 0)), pl.BlockSpec((n_mem, 2 * d), lambda b, i: (b, 0))],
        out_specs=pl.BlockSpec((tq, d), lambda b, i: (b * n_q + i, 0)),
        out_shape=jax.ShapeDtypeStruct((batch * seq, d), BF16),
        compiler_params=_params(2),
    )(q, kv)


def _xattn_bwd(q, kv, do, *, batch, seq, n_mem, tq=512):
    tq = min(tq, seq)
    n_q = seq // tq
    d = X_HEADS * X_HEAD_DIM

    def body(q_ref, kv_ref, do_ref, dq_ref, dkv_ref, acc):
        i = pl.program_id(1)

        @pl.when(i == 0)
        def _():
            acc[...] = jnp.zeros_like(acc)

        kcols, vcols = _x_heads(d)
        scores = [_nt(q_ref[:, c], kv_ref[:, c]) for c in kcols]
        d_probs = [_nt(do_ref[:, c], kv_ref[:, vc]) for c, vc in zip(kcols, vcols)]
        probs = [_x_softmax(s) for s in scores]
        d_scores = [(p * (dp - jnp.sum(dp * p, axis=-1, keepdims=True)) * X_SCALE).astype(BF16) for p, dp in zip(probs, d_probs)]
        for p, ds, c, vc in zip(probs, d_scores, kcols, vcols):
            acc[:, vc] += _tn(p.astype(BF16), do_ref[:, c])
            dq_ref[:, c] = _nn(ds, kv_ref[:, c]).astype(BF16)
            acc[:, c] += _tn(ds, q_ref[:, c])

        @pl.when(i == n_q - 1)
        def _():
            dkv_ref[...] = acc[...].astype(BF16)

    return pl.pallas_call(
        body, name="xattn_bwd", grid=(batch, n_q),
        in_specs=[pl.BlockSpec((tq, d), lambda b, i: (b * n_q + i, 0)), pl.BlockSpec((n_mem, 2 * d), lambda b, i: (b, 0)),
                  pl.BlockSpec((tq, d), lambda b, i: (b * n_q + i, 0))],
        out_specs=[pl.BlockSpec((tq, d), lambda b, i: (b * n_q + i, 0)), pl.BlockSpec((n_mem, 2 * d), lambda b, i: (b, 0))],
        out_shape=[jax.ShapeDtypeStruct((batch * seq, d), BF16), jax.ShapeDtypeStruct((batch * n_mem, 2 * d), BF16)],
        scratch_shapes=[pltpu.VMEM((n_mem, 2 * d), F32)],
        compiler_params=_params(2),
    )(q, kv, do)


def _my_index():
    return 4 * lax.axis_index("x") + 2 * lax.axis_index("y") + lax.axis_index("c")


def _peers():
    x, y, c = lax.axis_index("x"), lax.axis_index("y"), lax.axis_index("c")
    out = []
    for rel in range(1, N_DEV):
        dx, dy, dc = (rel >> 2) & 1, (rel >> 1) & 1, rel & 1
        px, py, pc = x ^ dx, y ^ dy, c ^ dc
        out.append(((px, py, pc), 4 * px + 2 * py + pc))
    return out


class _Exchange:
    def __init__(self, arrays, scatter):
        self.arrays, self.scatter, self.n = list(arrays), scatter, len(arrays)
        any_spec = pl.BlockSpec(memory_space=pl.ANY)
        self.in_specs = [any_spec] * self.n
        self.out_specs = [any_spec] * self.n
        self.out_shape = [jax.ShapeDtypeStruct((N_DEV,) + tuple(a.shape[-2:]), a.dtype) for a in self.arrays]
        n_peer = N_DEV - 1
        self.scratch = [pltpu.SemaphoreType.DMA((self.n, n_peer)), pltpu.SemaphoreType.DMA((self.n, n_peer)),
                        pltpu.SemaphoreType.DMA((self.n,))]

    def _copies(self, srcs, dsts, sems, arriving):
        send_sems, recv_sems, local_sems = sems
        me = _my_index()
        local, remote = [], []
        for w in range(self.n):
            if not arriving:
                local.append(pltpu.make_async_copy(srcs[w].at[me] if self.scatter else srcs[w], dsts[w].at[me], local_sems.at[w]))
            for rel, (pos, idx) in enumerate(_peers()):
                remote.append(pltpu.make_async_remote_copy(
                    src_ref=srcs[w].at[idx] if self.scatter else srcs[w], dst_ref=dsts[w].at[idx if arriving else me],
                    send_sem=send_sems.at[w, rel], recv_sem=recv_sems.at[w, rel], device_id=pos, device_id_type=MESH))
        return local, remote

    def start(self, srcs, dsts, sems):
        local, sends = self._copies(srcs, dsts, sems, arriving=False)
        for cp in local + sends:
            cp.start()

    def wait(self, srcs, dsts, sems):
        for cp in self._copies(srcs, dsts, sems, arriving=True)[1]:
            cp.wait_recv()
        local, sends = self._copies(srcs, dsts, sems, arriving=False)
        for cp in sends:
            cp.wait_send()
        for cp in local:
            cp.wait()


def _two_level_gather(x_ref, slots_ref, send_sems, recv_sems, local_sem):
    x, y, c = lax.axis_index("x"), lax.axis_index("y"), lax.axis_index("c")
    me, sibling = (x, y, c), (x, y, 1 - c)
    chips = [(1 - x, y), (x, 1 - y), (1 - x, 1 - y)]

    def slot(px, py, pc):
        return slots_ref.at[4 * px + 2 * py + pc]

    def copy(k, block, to, src=None):
        return pltpu.make_async_remote_copy(src_ref=slot(*block) if src is None else src, dst_ref=slot(*block),
                                            send_sem=send_sems.at[k], recv_sem=recv_sems.at[k], device_id=to, device_id_type=MESH)

    mine = pltpu.make_async_copy(x_ref, slot(*me), local_sem)
    mine.start()
    first = [copy(0, me, sibling, src=x_ref)] + [copy(1 + j, me, (*chip, c), src=x_ref) for j, chip in enumerate(chips)]
    for cp in first:
        cp.start()
    passed = [copy(4 + j, (*chip, c), sibling) for j, chip in enumerate(chips)]
    for j, chip in enumerate(chips):
        copy(1 + j, (*chip, c), me).wait_recv()
        passed[j].start()
    copy(0, sibling, me).wait_recv()
    for j, chip in enumerate(chips):
        copy(4 + j, (*chip, 1 - c), me).wait_recv()
    for cp in first + passed:
        cp.wait_send()
    mine.wait()


_TWO_LEVEL_SEMS = [pltpu.SemaphoreType.DMA((N_DEV - 1,)), pltpu.SemaphoreType.DMA((N_DEV - 1,)), pltpu.SemaphoreType.DMA(())]


def _gather_two_level(shard, *, name):
    any_spec = pl.BlockSpec(memory_space=pl.ANY)
    return pl.pallas_call(
        _two_level_gather_body(), name=name, in_specs=[any_spec], out_specs=any_spec,
        out_shape=jax.ShapeDtypeStruct((N_DEV,) + shard.shape, shard.dtype), scratch_shapes=_TWO_LEVEL_SEMS,
    )(shard)


def _two_level_gather_body():
    def body(x_ref, out_ref, send_sems, recv_sems, local_sem):
        _two_level_gather(x_ref, out_ref, send_sems, recv_sems, local_sem)
    return body


def _all_reduce_small(part, *, loss_rows, loss_scale):
    rows = part.shape[0]

    def body(p_ref, o_ref, buf, send_sems, recv_sems, local_sem):
        _two_level_gather(p_ref, buf, send_sems, recv_sems, local_sem)
        total = buf[0]
        for dev in range(1, N_DEV):
            total = total + buf[dev]
        o_ref[...] = total
        squares = total[rows - loss_rows:]
        loss = jnp.sum(jnp.sum(squares, axis=0, keepdims=True), axis=-1, keepdims=True) * loss_scale
        o_ref[rows - loss_rows:, :] = jnp.broadcast_to(loss, (loss_rows, LANES))

    vmem = pl.BlockSpec(memory_space=pltpu.VMEM)
    return pl.pallas_call(
        body, name="all_reduce_small", in_specs=[vmem], out_specs=vmem, out_shape=jax.ShapeDtypeStruct(part.shape, F32),
        scratch_shapes=[pltpu.VMEM((N_DEV, rows, LANES), F32)] + _TWO_LEVEL_SEMS,
        compiler_params=pltpu.CompilerParams(has_side_effects=True, vmem_limit_bytes=VMEM_LIMIT),
    )(part)


def _adamw_math(w, g, m, v):
    m_new = ADAM_B1 * m + (1.0 - ADAM_B1) * g
    v_new = ADAM_B2 * v + (1.0 - ADAM_B2) * (g * g)
    m_hat = m_new / (1.0 - ADAM_B1 ** ADAM_STEP)
    v_hat = v_new / (1.0 - ADAM_B2 ** ADAM_STEP)
    delta = -ADAM_LR * (m_hat / (jnp.sqrt(v_hat) + ADAM_EPS) + ADAM_WD * w)
    return delta, m_new, v_new


def _adamw(parts, w, m, v, *, name, tr=64):
    rows, cols = w.shape
    tr = min(tr, rows)
    while rows % (2 * tr) == 0 and 2 * tr * cols <= 256 * 1024:
        tr *= 2
    assert rows % tr == 0
    stacked = parts.ndim == 3

    def body(p_ref, w_ref, m_ref, v_ref, g_ref, d_ref, mo_ref, vo_ref):
        if stacked:
            g = p_ref[0].astype(F32)
            for dev in range(1, N_DEV):
                g = g + p_ref[dev].astype(F32)
        else:
            g = p_ref[...]
        delta, m_new, v_new = _adamw_math(w_ref[...], g, m_ref[...], v_ref[...])
        g_ref[...] = g
        d_ref[...] = delta
        mo_ref[...] = m_new
        vo_ref[...] = v_new

    tile = pl.BlockSpec((tr, cols), lambda i: (i, 0))
    p_spec = pl.BlockSpec((N_DEV, tr, cols), lambda i: (0, i, 0)) if stacked else tile
    return pl.pallas_call(
        body, name=name, grid=(rows // tr,), in_specs=[p_spec, tile, tile, tile], out_specs=[tile] * 4,
        out_shape=[jax.ShapeDtypeStruct((rows, cols), F32)] * 4, compiler_params=_params(1),
    )(parts, w, m, v)


_LATER = ("w_out", "w_cq", "w_ckv", "w_co", "w_ff1", "w_ff2")


def _as_rows(stacked):
    return stacked.reshape(-1, stacked.shape[-1])


def _local_step(x, mem, target, small, shards):
    batch, seq, d = x.shape
    n_mem = mem.shape[1]
    t = batch * seq
    x2, mem2, tgt2 = x.reshape(t, d), mem.reshape(batch * n_mem, d), target.reshape(t, d)
    g_mix, g_cross, g_mem, g_ffn, g_final = (small[k] for k in ("norm_mix_g", "norm_cross_g", "norm_mem_g", "norm_ffn_g", "norm_final_g"))
    gv, hg, w_sp, b_sp_t = small["gm_v_norm_g"], small["head_norm_g"], small["w_spatial"], small["b_spatial_t"]

    win_t = _as_rows(_gather_two_level(shards["w_in"], name="gather_w_in"))
    proj, xn = _norm_mm(x2, g_mix, win_t, mode="nt", name="proj_fwd", tm=512, tn=win_t.shape[0])
    merged_a = _gmlp_fwd(proj, w_sp, b_sp_t, gv, hg)
    merged, sb_raw, sb_weights, sb_log_rest, gathered = _sb_fwd(proj, merged_a, hg, batch=batch, seq=seq,
                                                                exchange=_Exchange([shards[n] for n in _LATER], scatter=False))
    wout, wcq, wckv_t, wco, wff1_t, wff2 = (_as_rows(g) for g in gathered)
    h1, qx, hn1 = _mm(merged, wout, mode="nn", out_dtype=F32, name="mix_out_fwd_xq_fwd", tm=512, epi=_epi_residual_then_norm_mm("nn"),
                      epi_ins=(x2,), vec_ins=(g_cross,), whole_ins=(wcq,), more_outs=((wcq.shape[1], BF16), (d, BF16)))
    kvx, memn = _norm_mm(mem2, g_mem, wckv_t, mode="nt", name="xkv_fwd", tm=512, tn=wckv_t.shape[0])
    o = _xattn_fwd(qx, kvx, batch=batch, seq=seq, n_mem=n_mem)
    h2, fpre, hn2 = _mm(o, wco, mode="nn", out_dtype=F32, name="xo_fwd_ff1_fwd", tm=512, epi=_epi_residual_then_norm_mm("nt"),
                        epi_ins=(h1,), vec_ins=(g_ffn,), whole_ins=(wff1_t,), more_outs=((wff1_t.shape[0], BF16), (d, BF16)))
    dh3, final_rows = _mm(fpre, wff2, mode="nn", out_dtype=F32, name="ff2_fwd_loss", tm=512, tk=wff2.shape[0], a_fn=_relu2,
                          epi=_epi_loss, epi_ins=(h2, tgt2), vec_ins=(g_final,), aux=2)
    dg_final, sq_err = final_rows[0:1], final_rows[1:2]

    dpre = _mm(dh3, wff2, mode="nt", out_dtype=BF16, name="ff2_bwd_x", tm=512, tn=wff2.shape[0], epi=_epi_relu2_grad,
               epi_ins=(fpre,))
    chunk = wff2.shape[0] // N_DEV
    d_wff2_t = _mm(dh3, fpre, mode="tn", out_dtype=BF16, name="ff2_bwd_w", tn=2048, b_fn=_relu2, col_chunk=chunk)
    d_wff1 = _mm(hn2, dpre, mode="tn", out_dtype=BF16, name="ff1_bwd_w", tn=2048, col_chunk=chunk)
    dh2, do, dg_ffn = _mm(dpre, wff1_t, mode="nn", out_dtype=F32, name="ff1_bwd_x_xo_bwd_x", tm=512, tk=wff1_t.shape[0],
                          epi=_epi_rms_bwd_then_mm, epi_ins=(h2, dh3), vec_ins=(g_ffn,), whole_ins=(wco,),
                          more_outs=((wco.shape[0], BF16),), aux=True)
    d_wco = _mm(o, dh2, mode="tn", out_dtype=BF16, name="xo_bwd_w")
    dqx, dkvx = _xattn_bwd(qx, kvx, do, batch=batch, seq=seq, n_mem=n_mem)
    d_wcq = _mm(hn1, dqx, mode="tn", out_dtype=BF16, name="xq_bwd_w")
    dh1, dmerged, dg_cross = _mm(dqx, wcq, mode="nt", out_dtype=F32, name="xq_bwd_x_mix_out_bwd_x", tm=512,
                                 epi=_epi_rms_bwd_then_mm, epi_ins=(h1, dh2), vec_ins=(g_cross,), whole_ins=(wout,),
                                 more_outs=((wout.shape[0], BF16),), aux=True)
    d_wckv_t = _mm(dkvx, memn, mode="tn", out_dtype=BF16, name="xkv_bwd_w")
    _, dg_mem = _mm(dkvx, wckv_t, mode="nn", out_dtype=BF16, name="xkv_bwd_x", tm=512, epi=_epi_rms_gain_only,
                    epi_ins=(mem2,), vec_ins=(g_mem,), aux=True)
    d_wout = _mm(merged, dh1, mode="tn", out_dtype=BF16, name="mix_out_bwd_w")
    dp_a, d_wsp, d_bsp_t, d_gv, d_hg_a = _gmlp_bwd(proj, dmerged, w_sp, b_sp_t, gv, hg)
    d_later = {"w_out": d_wout, "w_cq": d_wcq, "w_ckv": d_wckv_t, "w_co": d_wco, "w_ff1": d_wff1, "w_ff2": d_wff2_t}
    scatter = _Exchange([g if g.ndim == 3 else g.reshape(N_DEV, -1, d) for g in (d_later[n] for n in _LATER)], scatter=True)
    dq, dk, dv, d_hg_b, received = _sb_bwd(proj, sb_raw, sb_weights, sb_log_rest, dmerged, hg, batch=batch, seq=seq, exchange=scatter)
    dproj = dp_a
    for part, at in ((dq, 2 * GM_WIDTH), (dk, 3 * GM_WIDTH), (dv, 4 * GM_WIDTH)):
        dproj = lax.dynamic_update_slice(dproj, part, (0, at))
    d_win_t = _mm(xn, dproj, mode="tn", out_dtype=BF16, name="proj_bwd_w", tn=dproj.shape[1] // 2).T
    dx, dg_mix, d_win_received = _mm(dproj, win_t, mode="nn", out_dtype=F32, name="proj_bwd_x", tm=512, tk=win_t.shape[0],
                                     epi=_epi_rms_bwd, epi_ins=(x2, dh1), vec_ins=(g_mix,), aux=True,
                                     exchange=_Exchange([d_win_t.reshape(N_DEV, -1, d)], scatter=True))

    d_small = {"norm_mix_g": dg_mix, "gm_v_norm_g": d_gv, "w_spatial": d_wsp, "b_spatial_t": d_bsp_t, "head_norm_g": jnp.concatenate([d_hg_a, d_hg_b], axis=1),
               "norm_cross_g": dg_cross, "norm_mem_g": dg_mem, "norm_ffn_g": dg_ffn, "norm_final_g": dg_final}
    d_big = dict(zip(_LATER, received))
    d_big["w_in"] = d_win_received
    return sq_err, dx.reshape(batch, seq, d), d_small, d_big


_BIG = ("w_in", "w_out", "w_cq", "w_ckv", "w_co", "w_ff1", "w_ff2")
_GATHERED_TRANSPOSED = ("w_in", "w_ckv", "w_ff1")
_UPDATED_TRANSPOSED = ("w_in", "w_ckv", "w_ff2")
_SMALL = ("norm_mix_g", "gm_v_norm_g", "w_spatial", "b_spatial", "head_norm_g", "norm_cross_g", "norm_mem_g", "norm_ffn_g", "norm_final_g")
_NAMES = ("norm_mix_g", "w_in", "gm_v_norm_g", "w_spatial", "b_spatial", "head_norm_g", "w_out", "norm_cross_g", "norm_mem_g",
          "w_cq", "w_ckv", "w_co", "norm_ffn_g", "w_ff1", "w_ff2", "norm_final_g")


def _rows_of(a):
    r = a.reshape(-1, LANES)
    pad = (-r.shape[0]) % 8
    return jnp.pad(r, ((0, pad), (0, 0))) if pad else r


def _shard2d(a, transposed):
    return a[0].T if transposed else a[0]


def kernel(x, mem, norm_mix_g, w_in, gm_v_norm_g, w_spatial, b_spatial, head_norm_g, w_out, norm_cross_g, norm_mem_g, w_cq, w_ckv, w_co, norm_ffn_g, w_ff1, w_ff2, norm_final_g, loss_target, m_norm_mix_g, m_w_in, m_gm_v_norm_g, m_w_spatial, m_b_spatial, m_head_norm_g, m_w_out, m_norm_cross_g, m_norm_mem_g, m_w_cq, m_w_ckv, m_w_co, m_norm_ffn_g, m_w_ff1, m_w_ff2, m_norm_final_g, v_norm_mix_g, v_w_in, v_gm_v_norm_g, v_w_spatial, v_b_spatial, v_head_norm_g, v_w_out, v_norm_cross_g, v_norm_mem_g, v_w_cq, v_w_ckv, v_w_co, v_norm_ffn_g, v_w_ff1, v_w_ff2, v_norm_final_g):
    weights = dict(norm_mix_g=norm_mix_g, w_in=w_in, gm_v_norm_g=gm_v_norm_g, w_spatial=w_spatial, b_spatial=b_spatial,
                   head_norm_g=head_norm_g, w_out=w_out, norm_cross_g=norm_cross_g, norm_mem_g=norm_mem_g, w_cq=w_cq, w_ckv=w_ckv,
                   w_co=w_co, norm_ffn_g=norm_ffn_g, w_ff1=w_ff1, w_ff2=w_ff2, norm_final_g=norm_final_g)
    mom1 = dict(norm_mix_g=m_norm_mix_g, w_in=m_w_in, gm_v_norm_g=m_gm_v_norm_g, w_spatial=m_w_spatial, b_spatial=m_b_spatial,
                head_norm_g=m_head_norm_g, w_out=m_w_out, norm_cross_g=m_norm_cross_g, norm_mem_g=m_norm_mem_g, w_cq=m_w_cq,
                w_ckv=m_w_ckv, w_co=m_w_co, norm_ffn_g=m_norm_ffn_g, w_ff1=m_w_ff1, w_ff2=m_w_ff2, norm_final_g=m_norm_final_g)
    mom2 = dict(norm_mix_g=v_norm_mix_g, w_in=v_w_in, gm_v_norm_g=v_gm_v_norm_g, w_spatial=v_w_spatial, b_spatial=v_b_spatial,
                head_norm_g=v_head_norm_g, w_out=v_w_out, norm_cross_g=v_norm_cross_g, norm_mem_g=v_norm_mem_g, w_cq=v_w_cq,
                w_ckv=v_w_ckv, w_co=v_w_co, norm_ffn_g=v_norm_ffn_g, w_ff1=v_w_ff1, w_ff2=v_w_ff2, norm_final_g=v_norm_final_g)

    shards = {n: _shard2d(weights[n], n in _GATHERED_TRANSPOSED).astype(BF16) for n in _BIG}
    small = {n: weights[n].reshape(1, -1) for n in _SMALL if n not in ("w_spatial", "b_spatial")}
    small["w_spatial"] = w_spatial[0]
    small["b_spatial_t"] = b_spatial[0].T
    sq_err, grad_x, d_small, d_big = _local_step(x, mem, loss_target, small, shards)

    d_small["b_spatial"] = d_small.pop("b_spatial_t").T
    sq_rows = _rows_of(sq_err)
    packed = jnp.concatenate([_rows_of(d_small[n]) for n in _SMALL] + [sq_rows], axis=0)
    summed = _all_reduce_small(packed, loss_rows=sq_rows.shape[0], loss_scale=0.5 / x.shape[-1])

    grads, deltas, new_m, new_v = {}, {}, {}, {}
    for n in _BIG:
        flip = n in _UPDATED_TRANSPOSED
        outs = _adamw(d_big[n], _shard2d(weights[n], flip), _shard2d(mom1[n], flip), _shard2d(mom2[n], flip), name="adamw_" + n)
        outs = [o.T if flip else o for o in outs]
        grads[n], deltas[n], new_m[n], new_v[n] = (o[None] for o in outs)
    pack = lambda src: jnp.concatenate([_rows_of(src[n]) for n in _SMALL], axis=0)
    n_small_rows = sum(_rows_of(weights[n]).shape[0] for n in _SMALL)
    outs = _adamw(summed[:n_small_rows], pack(weights), pack(mom1), pack(mom2), name="adamw_small", tr=n_small_rows)
    at = 0
    for n in _SMALL:
        used = weights[n].size // LANES
        for dst, o in zip((grads, deltas, new_m, new_v), outs):
            dst[n] = o[at:at + used].reshape(weights[n].shape)
        at += _rows_of(weights[n]).shape[0]
    loss = summed[n_small_rows, 0]
    return (loss, grad_x, *[grads[n] for n in _NAMES], *[deltas[n] for n in _NAMES], *[new_m[n] for n in _NAMES],
            *[new_v[n] for n in _NAMES])
```

```python
import math

import jax
import jax.numpy as jnp
from jax import lax
from jax.experimental import pallas as pl
from jax.experimental.pallas import tpu as pltpu

F32 = jnp.float32
BF16 = jnp.bfloat16
EPS = 1e-6
N_DEV = 8
LANES = 128
CHUNK = 128
GM_GROUPS = 4
GM_WIDTH = 512
SB_PAIRS = 4
SB_HEAD_DIM = 64
SB_SCALE = 0.125
SB_TILE = 128
SB_BLOCK = 512
X_HEADS = 4
X_HEAD_DIM = 256
X_SCALE = 1.0 / 16.0
VMEM_LIMIT = 56 * 1024 * 1024
ADAM_LR, ADAM_B1, ADAM_B2, ADAM_EPS, ADAM_WD, ADAM_STEP = 0.001, 0.9, 0.999, 1e-08, 0.01, 10
MESH = pl.DeviceIdType.MESH


def _params(n_axes):
    return pltpu.CompilerParams(dimension_semantics=("arbitrary",) * n_axes, vmem_limit_bytes=VMEM_LIMIT)


def _dot(a, b, dims):
    return lax.dot_general(a, b, (dims, ((), ())), preferred_element_type=F32)


def _nn(a, b):
    return _dot(a, b, ((1,), (0,)))


def _nt(a, b):
    return _dot(a, b, ((1,), (1,)))


def _tn(a, b):
    return _dot(a, b, ((0,), (0,)))


_MODES = {"nn": _nn, "nt": _nt, "tn": _tn}


def _rstd(x):
    return lax.rsqrt(jnp.mean(x * x, axis=-1, keepdims=True) + EPS)


def _gelu(x):
    c = math.sqrt(2.0 / math.pi)
    t = jnp.tanh(c * (x + 0.044715 * x * x * x))
    return 0.5 * x * (1.0 + t)


def _gelu_and_grad(x):
    c = math.sqrt(2.0 / math.pi)
    t = jnp.tanh(c * (x + 0.044715 * x * x * x))
    half = 0.5 * (1.0 + t)
    return x * half, half + 0.5 * x * (1.0 - t * t) * c * (1.0 + 3 * 0.044715 * x * x)


def _split_bf16(x):
    hi = x.astype(BF16)
    lo = (x - hi.astype(F32)).astype(BF16)
    return hi, lo


def _mm(a, b, *, mode, out_dtype, name, tm=1024, tn=1024, tk=1024, a_fn=None, b_fn=None, epi=None, epi_ins=(), vec_ins=(),
        whole_ins=(), more_outs=(), aux=False, col_chunk=None, exchange=None):
    if mode == "nn":
        (m, k), (k2, n) = a.shape, b.shape
    elif mode == "nt":
        (m, k), (n, k2) = a.shape, b.shape
    else:
        (k, m), (k2, n) = a.shape, b.shape
    assert k == k2, (a.shape, b.shape, mode)
    tm, tn, tk = min(tm, m), min(tn, n), min(tk, k)
    assert m % tm == 0 and n % tn == 0 and k % tk == 0, (m, n, k, tm, tn, tk)
    n_m, n_n, n_k = m // tm, n // tn, k // tk
    assert not (aux or more_outs) or n_n == 1
    dot = _MODES[mode]
    n_epi, n_vec, n_whole, n_more = len(epi_ins), len(vec_ins), len(whole_ins), len(more_outs)

    def body(*refs):
        ins, outs, scratch, x_refs = _riding(exchange, refs, 2 + n_epi + n_vec + n_whole, 1 + n_more + (1 if aux else 0),
                                             1 if n_k > 1 else 0)
        a_ref, b_ref, epi_refs = ins[0], ins[1], ins[2:]
        o_ref, more_refs = outs[0], outs[1:1 + n_more]
        aux_ref = outs[1 + n_more] if aux else None
        acc_ref = scratch[0] if n_k > 1 else None
        i, j, kk = pl.program_id(0), pl.program_id(1), pl.program_id(2)
        ride_done = _ride(exchange, x_refs, (i == 0) & (j == 0) & (kk == 0), (i == n_m - 1) & (j == n_n - 1) & (kk == n_k - 1))
        def product():
            av, bv = a_ref[...], b_ref[...]
            if a_fn is not None:
                av = a_fn(av)
            if b_fn is not None:
                bv = b_fn(bv)
            return dot(av.astype(BF16), bv.astype(BF16))

        def finish(acc):
            if col_chunk is not None:
                for ch in range(tn // col_chunk):
                    o_ref[ch] = acc[:, ch * col_chunk:(ch + 1) * col_chunk].astype(out_dtype)
                return
            if epi is None:
                o_ref[...] = acc.astype(out_dtype)
                return
            res = epi(acc, *[r[...] for r in epi_refs])
            if n_more:
                for more_ref, value in zip(more_refs, res[1:1 + n_more]):
                    more_ref[...] = value.astype(more_ref.dtype)
                res = (res[0],) + tuple(res[1 + n_more:]) if aux else res[0]
            if aux:
                res, rows = res[0], res[1:]
                rows = rows[0] if len(rows) == 1 else jnp.concatenate(rows, axis=0)

                @pl.when(i == 0)
                def _():
                    aux_ref[...] = rows

                @pl.when(i != 0)
                def _():
                    aux_ref[...] += rows
            o_ref[...] = res.astype(out_dtype)

        if n_k == 1:
            finish(product())
        else:
            @pl.when(kk == 0)
            def _():
                acc_ref[...] = product()

            @pl.when(kk != 0)
            def _():
                acc_ref[...] += product()

            @pl.when(kk == n_k - 1)
            def _():
                finish(acc_ref[...])

        ride_done()

    if mode == "tn":
        a_spec = pl.BlockSpec((tk, tm), lambda i, j, kk: (kk, i))
    else:
        a_spec = pl.BlockSpec((tm, tk), lambda i, j, kk: (i, kk))
    if mode == "nt":
        b_spec = pl.BlockSpec((tn, tk), lambda i, j, kk: (j, kk))
    else:
        b_spec = pl.BlockSpec((tk, tn), lambda i, j, kk: (kk, j))
    tile_spec = pl.BlockSpec((tm, tn), lambda i, j, kk: (i, j))
    row_spec = pl.BlockSpec((1, tn), lambda i, j, kk: (0, j))
    out_shape = [jax.ShapeDtypeStruct((m, n), out_dtype)]
    out_specs = [tile_spec]
    if col_chunk is not None:
        assert epi is None and not aux and tn % col_chunk == 0
        out_shape = [jax.ShapeDtypeStruct((n // col_chunk, m, col_chunk), out_dtype)]
        out_specs = [pl.BlockSpec((tn // col_chunk, tm, col_chunk), lambda i, j, kk: (j, i, 0))]
    for columns, dtype in more_outs:
        out_shape.append(jax.ShapeDtypeStruct((m, columns), dtype))
        out_specs.append(pl.BlockSpec((tm, columns), lambda i, j, kk: (i, 0)))
    if aux:
        out_shape.append(jax.ShapeDtypeStruct((int(aux), n), F32))
        out_specs.append(pl.BlockSpec((int(aux), tn), lambda i, j, kk: (0, j)))
    whole_specs = [pl.BlockSpec(w.shape, lambda i, j, kk, nd=w.ndim: (0,) * nd) for w in whole_ins]
    x_in, x_out, x_shape, x_scratch, x_arrays = _riding_specs(exchange)
    res = pl.pallas_call(
        body, name=name, grid=(n_m, n_n, n_k),
        in_specs=[a_spec, b_spec] + [tile_spec] * n_epi + [row_spec] * n_vec + whole_specs + x_in,
        out_specs=out_specs + x_out, out_shape=out_shape + x_shape,
        scratch_shapes=([pltpu.VMEM((tm, tn), F32)] if n_k > 1 else []) + x_scratch,
        compiler_params=_params(3),
    )(a, b, *epi_ins, *vec_ins, *whole_ins, *x_arrays)
    if exchange is not None or more_outs:
        return tuple(res)
    return res if aux else res[0]


def _norm_mm(x, g, w, *, mode, name, tm=1024, tn=1024):
    m, d = x.shape
    n = w.shape[0] if mode == "nt" else w.shape[1]
    tm, tn = min(tm, m), min(tn, n)
    assert m % tm == 0 and n % tn == 0
    dot = _MODES[mode]

    def body(x_ref, g_ref, w_ref, o_ref, xn_ref, xn_s):
        @pl.when(pl.program_id(1) == 0)
        def _():
            xv = x_ref[...]
            xn = (xv * _rstd(xv) * g_ref[...]).astype(BF16)
            xn_s[...] = xn
            xn_ref[...] = xn

        o_ref[...] = dot(xn_s[...], w_ref[...]).astype(BF16)

    w_spec = pl.BlockSpec((tn, d), lambda i, j: (j, 0)) if mode == "nt" else pl.BlockSpec((d, tn), lambda i, j: (0, j))
    return pl.pallas_call(
        body, name=name, grid=(m // tm, n // tn),
        in_specs=[pl.BlockSpec((tm, d), lambda i, j: (i, 0)), pl.BlockSpec((1, d), lambda i, j: (0, 0)), w_spec],
        out_specs=[pl.BlockSpec((tm, tn), lambda i, j: (i, j)), pl.BlockSpec((tm, d), lambda i, j: (i, 0))],
        out_shape=[jax.ShapeDtypeStruct((m, n), BF16), jax.ShapeDtypeStruct((m, d), BF16)],
        scratch_shapes=[pltpu.VMEM((tm, d), BF16)],
        compiler_params=_params(2),
    )(x, g, w)


def _epi_residual(acc, res):
    return res + acc


def _epi_relu2_grad(acc, pre):
    return acc * (2.0 * jnp.maximum(pre.astype(F32), 0.0))


def _relu2(pre):
    r = jnp.maximum(pre.astype(F32), 0.0)
    return r * r


def _epi_rms_bwd(acc, h, dres, g):
    r = _rstd(h)
    xh = h * r
    dxh = acc * g
    dh = dres + r * (dxh - xh * jnp.mean(dxh * xh, axis=-1, keepdims=True))
    return dh, jnp.sum(acc * xh, axis=0, keepdims=True)


def _epi_residual_then_norm_mm(mode):
    dot = _MODES[mode]

    def epi(acc, res, g, w):
        h = res + acc
        hn = (h * _rstd(h) * g).astype(BF16)
        return h, dot(hn, w), hn

    return epi


def _epi_rms_bwd_then_mm(acc, h, dres, g, w):
    dh, row = _epi_rms_bwd(acc, h, dres, g)
    return dh, _nt(dh.astype(BF16), w), row


def _epi_loss(acc, h_in, target, g):
    h = h_in + acc
    r = _rstd(h)
    xh = h * r
    err = xh * g - target
    dy = err * (1.0 / h.shape[-1])
    dxh = dy * g
    dh = r * (dxh - xh * jnp.mean(dxh * xh, axis=-1, keepdims=True))
    return dh, jnp.sum(dy * xh, axis=0, keepdims=True), jnp.sum(err * err, axis=0, keepdims=True)


def _epi_rms_gain_only(acc, h, g):
    return acc, jnp.sum(acc * (h * _rstd(h)), axis=0, keepdims=True)


def _tril(n):
    row = lax.broadcasted_iota(jnp.int32, (n, n), 0)
    col = lax.broadcasted_iota(jnp.int32, (n, n), 1)
    return col <= row


def _gmlp_fwd(proj, w_sp, b_sp_t, gv, hg, *, rows=512):
    t = proj.shape[0]
    rows = min(rows, t)
    n_c = rows // CHUNK

    def body(u_ref, v_ref, w_ref, bt_ref, gv_ref, hg_ref, m_ref):
        keep = _tril(CHUNK)
        for g in range(GM_GROUPS):
            cols = slice(g * LANES, (g + 1) * LANES)
            wg = jnp.where(keep, w_ref[g], 0.0).astype(BF16)
            u = _gelu(u_ref[:, cols].astype(F32))
            v = _gelu(v_ref[:, cols].astype(F32))
            vn = (v * _rstd(v) * gv_ref[:, cols]).astype(BF16)
            bias = bt_ref[:, g:g + 1]
            chunks = [slice(c * CHUNK, (c + 1) * CHUNK) for c in range(n_c)]
            gated = [u[rs] * (_nn(wg, vn[rs]) + bias) for rs in chunks]
            for rs, a in zip(chunks, gated):
                m_ref[rs, cols] = (a * _rstd(a) * hg_ref[:, cols]).astype(BF16)

    full = lambda shape: pl.BlockSpec(shape, lambda i: (0,) * len(shape))
    return pl.pallas_call(
        body, name="gmlp_fwd", grid=(t // rows,),
        in_specs=[pl.BlockSpec((rows, GM_WIDTH), lambda i: (i, 0)), pl.BlockSpec((rows, GM_WIDTH), lambda i: (i, 1)),
                  full((GM_GROUPS, CHUNK, CHUNK)), full((CHUNK, GM_GROUPS)), full((1, GM_WIDTH)), full((1, GM_WIDTH))],
        out_specs=pl.BlockSpec((rows, GM_WIDTH), lambda i: (i, 0)),
        out_shape=jax.ShapeDtypeStruct((t, 2 * GM_WIDTH), BF16),
        compiler_params=_params(1),
    )(proj, proj, w_sp, b_sp_t, gv, hg)


def _gmlp_bwd(proj, dmerged, w_sp, b_sp_t, gv, hg, *, rows=512):
    t = proj.shape[0]
    rows = min(rows, t)
    n_c = rows // CHUNK
    n_steps = t // rows

    def body(u_ref, v_ref, dm_ref, w_ref, bt_ref, gv_ref, hg_ref, dp_ref, dw_ref, dbt_ref, dgv_ref, dhg_ref, db_acc):
        step = pl.program_id(0)
        keep = _tril(CHUNK)

        @pl.when(step == 0)
        def _():
            dw_ref[...] = jnp.zeros_like(dw_ref)
            db_acc[...] = jnp.zeros_like(db_acc)
            dgv_ref[...] = jnp.zeros_like(dgv_ref)
            dhg_ref[...] = jnp.zeros_like(dhg_ref)

        for g in range(GM_GROUPS):
            cols = slice(g * LANES, (g + 1) * LANES)
            wg = jnp.where(keep, w_ref[g], 0.0).astype(BF16)
            u, u_slope = _gelu_and_grad(u_ref[:, cols].astype(F32))
            v, v_slope = _gelu_and_grad(v_ref[:, cols].astype(F32))
            r = _rstd(v)
            xh = v * r
            gvg = gv_ref[:, cols]
            hgg = hg_ref[:, cols]
            vn = (xh * gvg).astype(BF16)
            bias = bt_ref[:, g:g + 1]
            dm = dm_ref[:, cols].astype(F32)
            chunks = [slice(c * CHUNK, (c + 1) * CHUNK) for c in range(n_c)]
            mixed = [_nn(wg, vn[rs]) + bias for rs in chunks]
            gated = [u[rs] * mx for rs, mx in zip(chunks, mixed)]
            scale = [_rstd(a) for a in gated]
            normed = [a * ra for a, ra in zip(gated, scale)]
            d_normed = [dm[rs] * hgg for rs in chunks]
            d_gated = [ra * (dan - an * jnp.mean(dan * an, axis=-1, keepdims=True)) for ra, dan, an in zip(scale, d_normed, normed)]
            d_mixed = [da * u[rs] for da, rs in zip(d_gated, chunks)]
            d_mixed_b = [dmx.astype(BF16) for dmx in d_mixed]
            du = jnp.concatenate([da * mx for da, mx in zip(d_gated, mixed)], axis=0)
            dvn = jnp.concatenate([_tn(wg, dmb) for dmb in d_mixed_b], axis=0)
            dw_ref[g] += sum(_nt(dmb, vn[rs]) for dmb, rs in zip(d_mixed_b, chunks))
            db_acc[g] += sum(d_mixed)
            dhg_ref[:, cols] += sum(jnp.sum(dm[rs] * an, axis=0, keepdims=True) for rs, an in zip(chunks, normed))
            dgv_ref[:, cols] += jnp.sum(dvn * xh, axis=0, keepdims=True)
            dxh = dvn * gvg
            dv = r * (dxh - xh * jnp.mean(dxh * xh, axis=-1, keepdims=True))
            dp_ref[:, cols] = (du * u_slope).astype(BF16)
            dp_ref[:, GM_WIDTH + g * LANES:GM_WIDTH + (g + 1) * LANES] = (dv * v_slope).astype(BF16)

        @pl.when(step == n_steps - 1)
        def _():
            for g in range(GM_GROUPS):
                dw_ref[g] = jnp.where(keep, dw_ref[g], 0.0)
                dbt_ref[:, g:g + 1] = jnp.sum(db_acc[g], axis=-1, keepdims=True)

    full = lambda shape: pl.BlockSpec(shape, lambda i: (0,) * len(shape))
    return pl.pallas_call(
        body, name="gmlp_bwd", grid=(n_steps,),
        in_specs=[pl.BlockSpec((rows, GM_WIDTH), lambda i: (i, 0)), pl.BlockSpec((rows, GM_WIDTH), lambda i: (i, 1)),
                  pl.BlockSpec((rows, GM_WIDTH), lambda i: (i, 0)),
                  full((GM_GROUPS, CHUNK, CHUNK)), full((CHUNK, GM_GROUPS)), full((1, GM_WIDTH)), full((1, GM_WIDTH))],
        out_specs=[pl.BlockSpec((rows, 2 * GM_WIDTH), lambda i: (i, 0)), full((GM_GROUPS, CHUNK, CHUNK)),
                   full((CHUNK, GM_GROUPS)), full((1, GM_WIDTH)), full((1, GM_WIDTH))],
        out_shape=[jax.ShapeDtypeStruct((t, proj.shape[1]), BF16), jax.ShapeDtypeStruct((GM_GROUPS, CHUNK, CHUNK), F32),
                   jax.ShapeDtypeStruct((CHUNK, GM_GROUPS), F32), jax.ShapeDtypeStruct((1, GM_WIDTH), F32),
                   jax.ShapeDtypeStruct((1, GM_WIDTH), F32)],
        scratch_shapes=[pltpu.VMEM((GM_GROUPS, CHUNK, LANES), F32)],
        compiler_params=_params(1),
    )(proj, proj, dmerged, w_sp, b_sp_t, gv, hg)


def _sb_logits(z, strict):
    ls = jnp.minimum(z, 0.0) - jnp.log(1.0 + jnp.exp(-jnp.abs(z)))
    l1m = ls - z
    if strict is not None:
        l1m = jnp.where(strict, l1m, 0.0)
    return ls, l1m


def _tri_sums(x, tri):
    hi, lo = _split_bf16(x)
    return _nn(jnp.concatenate([hi, lo], axis=1), jnp.concatenate([tri, tri], axis=0))


def _sb_weights(ls, in_tile, right, strict):
    a = jnp.exp(ls + in_tile + right)
    if strict is not None:
        a = jnp.where(strict, a, 0.0)
    return a


def _sb_masks(q_rows):
    row = lax.broadcasted_iota(jnp.int32, (SB_TILE, SB_TILE), 0)
    col = lax.broadcasted_iota(jnp.int32, (SB_TILE, SB_TILE), 1)
    lane = lax.broadcasted_iota(jnp.int32, (q_rows, LANES), 1)
    return row, col, lane < SB_HEAD_DIM


def _stack_heads(x, first):
    zero = jnp.zeros_like(x)
    return jnp.concatenate([jnp.where(first, x, zero), jnp.where(first, zero, x)], axis=0)


def _stack_heads_t(x_t):
    first_t = lax.broadcasted_iota(jnp.int32, x_t.shape, 0) < SB_HEAD_DIM
    zero = jnp.zeros_like(x_t)
    return jnp.concatenate([jnp.where(first_t, x_t, zero), jnp.where(first_t, zero, x_t)], axis=1).astype(BF16)


def _unstack_heads(x2, first):
    half = x2.shape[0] // 2
    return jnp.where(first, x2[:half], x2[half:])


def _live_rows(x, s, q_rows):
    if s == 0:
        return x
    return jnp.concatenate([x[s * SB_TILE:q_rows], x[q_rows + s * SB_TILE:]], axis=0)


def _spread_rows(x, s, q_rows):
    if s == 0:
        return x
    half = q_rows - s * SB_TILE
    zero = jnp.zeros((s * SB_TILE,) + x.shape[1:], x.dtype)
    return jnp.concatenate([zero, x[:half], zero, x[half:]], axis=0)


def _stacked_col_minus_row(q_rows):
    row = lax.broadcasted_iota(jnp.int32, (2 * q_rows, SB_TILE), 0)
    col = lax.broadcasted_iota(jnp.int32, (2 * q_rows, SB_TILE), 1)
    return col - (row & (q_rows - 1))


def _head_mean(x, first):
    s0 = jnp.sum(jnp.where(first, x, 0.0), axis=-1, keepdims=True)
    s1 = jnp.sum(jnp.where(first, 0.0, x), axis=-1, keepdims=True)
    return jnp.where(first, s0, s1) * (1.0 / SB_HEAD_DIM)


def _riding(exchange, refs, n_in, n_out, n_scratch):
    n_x = exchange.n if exchange is not None else 0
    ins, rest = refs[:n_in], refs[n_in:]
    x_src, rest = rest[:n_x], rest[n_x:]
    outs, rest = rest[:n_out], rest[n_out:]
    x_dst, rest = rest[:n_x], rest[n_x:]
    return ins, outs, rest[:n_scratch], (x_src, x_dst, rest[n_scratch:])


def _riding_specs(exchange):
    if exchange is None:
        return [], [], [], [], []
    return exchange.in_specs, exchange.out_specs, exchange.out_shape, exchange.scratch, exchange.arrays


def _ride(exchange, x_refs, first_step, last_step):
    if exchange is None:
        return lambda: None

    @pl.when(first_step)
    def _():
        exchange.start(*x_refs)

    def finish():
        @pl.when(last_step)
        def _():
            exchange.wait(*x_refs)

    return finish


def _sb_fwd(proj, merged_a, hg, *, batch, seq, exchange=None):
    q0, k0, v0 = 2 * GM_WIDTH // LANES, 2 * GM_WIDTH // LANES + SB_PAIRS, 2 * GM_WIDTH // LANES + 2 * SB_PAIRS
    q_rows = block_keys = min(SB_BLOCK, seq)
    assert seq % block_keys == 0
    n_q, n_sub, n_blocks = seq // q_rows, block_keys // SB_TILE, seq // block_keys

    def body(*refs):
        (q_ref, k_ref, v_ref, hg_ref, _), (m_ref, raw_ref, a_ref, l_ref), _, x_refs = _riding(exchange, refs, 5, 4, 0)
        b, p, i = pl.program_id(0), pl.program_id(1), pl.program_id(2)
        finish = _ride(exchange, x_refs, (b == 0) & (p == 0) & (i == 0), (b == batch - 1) & (p == SB_PAIRS - 1) & (i == n_q - 1))
        row, col, first = _sb_masks(q_rows)
        upper = (row > col).astype(BF16)
        q2 = _stack_heads((q_ref[...].astype(F32) * SB_SCALE).astype(BF16), first)
        diff = _stacked_col_minus_row(q_rows)
        last = i

        def key_tile(jb, s):
            return k_ref[pl.ds(pl.multiple_of((jb * n_sub + s) * SB_TILE, SB_TILE), SB_TILE), :]

        def scores(jb):
            return tuple(_nt(q2, key_tile(jb, s)) for s in range(n_sub))

        def keep_for_backward(ref, jb, s, stacked):
            ref[0, 0, jb, 0, :, s * SB_TILE:(s + 1) * SB_TILE] = stacked

        def weights_of(jb, z, right):
            logits = [_sb_logits(z[s], None) for s in range(n_sub)]
            totals = [jnp.sum(l1m, axis=-1, keepdims=True) for _, l1m in logits]
            sums = [_tri_sums(l1m, upper) for _, l1m in logits]
            for s in reversed(range(n_sub)):
                keep_for_backward(a_ref, jb, s, _sb_weights(logits[s][0], sums[s], right, None).astype(BF16))
                keep_for_backward(l_ref, jb, s, logits[s][1].astype(BF16))
                right = right + totals[s]
            return right

        def diagonal_weights():
            keeps = [_live_rows(diff, s, q_rows) < -s * SB_TILE for s in range(n_sub)]
            logits = [_sb_logits(_nt(_live_rows(q2, s, q_rows), key_tile(last, s)), keeps[s]) for s in range(n_sub)]
            totals = [jnp.sum(l1m, axis=-1, keepdims=True) for _, l1m in logits]
            sums = [_tri_sums(l1m, upper) for _, l1m in logits]
            right = jnp.zeros((2 * q_rows, 1), F32)
            for s in reversed(range(n_sub)):
                live = _sb_weights(logits[s][0], sums[s], _live_rows(right, s, q_rows), keeps[s]).astype(BF16)
                keep_for_backward(a_ref, last, s, _spread_rows(live, s, q_rows))
                keep_for_backward(l_ref, last, s, _spread_rows(logits[s][1].astype(BF16), s, q_rows))
                right = right + _spread_rows(totals[s], s, q_rows)
            return right

        def weighted_values(jb):
            return _nn(a_ref[0, 0, jb, 0], v_ref[pl.ds(pl.multiple_of(jb * block_keys, block_keys), block_keys), :])

        right = diagonal_weights()
        z_next = scores(jnp.maximum(last - 1, 0))

        def step(k, carry):
            right, acc, z = carry
            jb = last - k
            acc = acc + weighted_values(jb + 1)
            z_next = scores(jnp.maximum(jb - 1, 0))
            return weights_of(jb, z, right), acc, z_next

        _, acc2, _ = lax.fori_loop(1, last + 1, step, (right, jnp.zeros((2 * q_rows, LANES), F32), z_next))
        acc = _unstack_heads(acc2 + weighted_values(0), first)
        raw_ref[...] = acc
        m_ref[...] = (acc * lax.rsqrt(_head_mean(acc * acc, first) + EPS) * hg_ref[...]).astype(BF16)
        finish()

    t = batch * seq
    blk = lambda c0: pl.BlockSpec((q_rows, LANES), lambda b, p, i: (b * n_q + i, c0 + p))
    kv = lambda c0: pl.BlockSpec((seq, LANES), lambda b, p, i: (b, c0 + p))
    kept = pl.BlockSpec((1, 1, n_blocks, 1, 2 * q_rows, block_keys), lambda b, p, i: (p, b, 0, i, 0, 0))
    kept_shape = jax.ShapeDtypeStruct((SB_PAIRS, batch, n_blocks, n_q, 2 * q_rows, block_keys), BF16)
    x_in, x_out, x_shape, x_scratch, x_arrays = _riding_specs(exchange)
    res = pl.pallas_call(
        body, name="sb_fwd", grid=(batch, SB_PAIRS, n_q),
        in_specs=[blk(q0), kv(k0), kv(v0), pl.BlockSpec((1, LANES), lambda b, p, i: (0, SB_PAIRS + p)),
                  pl.BlockSpec(memory_space=pl.ANY)] + x_in,
        out_specs=[blk(SB_PAIRS), blk(0), kept, kept] + x_out,
        out_shape=[jax.ShapeDtypeStruct((t, 2 * GM_WIDTH), BF16), jax.ShapeDtypeStruct((t, GM_WIDTH), F32), kept_shape, kept_shape] + x_shape,
        scratch_shapes=x_scratch,
        input_output_aliases={4: 0},
        compiler_params=_params(3),
    )(proj, proj, proj, hg, merged_a, *x_arrays)
    return res[0], res[1], res[2], res[3], res[4:]


def _sb_bwd(proj, raw, weights, log_rest, dmerged, hg, *, batch, seq, exchange=None):
    q0, k0, v0 = 2 * GM_WIDTH // LANES, 2 * GM_WIDTH // LANES + SB_PAIRS, 2 * GM_WIDTH // LANES + 2 * SB_PAIRS
    q_rows = block_keys = min(SB_BLOCK, seq)
    assert seq % block_keys == 0
    n_q, n_sub, n_blocks = seq // q_rows, block_keys // SB_TILE, seq // block_keys

    def body(*refs):
        ins, outs, (dk_acc, dv_acc), x_refs = _riding(exchange, refs, 8, 4, 2)
        q_ref, k_ref, v_ref, raw_ref, a_ref, l_ref, dm_ref, hg_ref = ins
        dq_ref, dk_ref, dv_ref, dhg_ref = outs
        p, b, i = pl.program_id(0), pl.program_id(1), pl.program_id(2)
        finish = _ride(exchange, x_refs, (b == 0) & (p == 0) & (i == 0), (b == batch - 1) & (p == SB_PAIRS - 1) & (i == n_q - 1))
        row, col, first = _sb_masks(q_rows)
        lower = (row < col).astype(BF16)

        @pl.when(jnp.logical_and(b == 0, i == 0))
        def _():
            dhg_ref[...] = jnp.zeros_like(dhg_ref)

        @pl.when(i == 0)
        def _():
            dk_acc[...] = jnp.zeros_like(dk_acc)
            dv_acc[...] = jnp.zeros_like(dv_acc)

        raw_v = raw_ref[...]
        dm = dm_ref[...].astype(F32)
        r = lax.rsqrt(_head_mean(raw_v * raw_v, first) + EPS)
        nrm = raw_v * r
        dhg_ref[...] += jnp.sum(dm * nrm, axis=0, keepdims=True)
        dn = dm * hg_ref[...]
        dout = r * (dn - nrm * _head_mean(dn * nrm, first))
        dout2 = _stack_heads(dout.astype(BF16), first)
        q2_t = _stack_heads_t(q_ref[...].astype(F32).T)
        dout2_t = _stack_heads_t(dout.T)
        diff = _stacked_col_minus_row(q_rows)
        last = i

        def kept(ref, jb, cols):
            return ref[0, 0, jb, 0, :, cols]

        def block(jb, carry, diagonal):
            gleft, dq = carry
            live = (lambda x, s: _live_rows(x, s, q_rows)) if diagonal else (lambda x, s: x)
            spread = (lambda x, s: _spread_rows(x, s, q_rows)) if diagonal else (lambda x, s: x)
            tiles = [pl.ds(pl.multiple_of((jb * n_sub + s) * SB_TILE, SB_TILE), SB_TILE) for s in range(n_sub)]
            cols = [slice(s * SB_TILE, (s + 1) * SB_TILE) for s in range(n_sub)]
            gmats = [_nt(live(dout2, s), v_ref[tiles[s], :]) * live(kept(a_ref, jb, cols[s]), s).astype(F32) for s in range(n_sub)]
            prefixes = [_tri_sums(g, lower) for g in gmats]
            dzs = []
            for s in range(n_sub):
                one_minus = jnp.exp(live(kept(l_ref, jb, cols[s]), s).astype(F32))
                dz = (gmats[s] * one_minus - (live(gleft, s) + prefixes[s]) * (1.0 - one_minus)) * SB_SCALE
                gleft = gleft + spread(jnp.sum(gmats[s], axis=-1, keepdims=True), s)
                if diagonal:
                    dz = jnp.where(live(diff, s) < -s * SB_TILE, dz, 0.0)
                dzs.append(spread(dz.astype(BF16), s))
            dz_all = jnp.concatenate(dzs, axis=1)
            dk_acc[jb] += _nn(q2_t, dz_all)
            dv_acc[jb] += _nn(dout2_t, kept(a_ref, jb, slice(None)))
            return gleft, dq + _nn(dz_all, k_ref[pl.ds(pl.multiple_of(jb * block_keys, block_keys), block_keys), :])

        carry = (jnp.zeros((2 * q_rows, 1), F32), jnp.zeros((2 * q_rows, LANES), F32))
        carry = lax.fori_loop(0, last, lambda jb, c: block(jb, c, False), carry)
        dq_ref[...] = _unstack_heads(block(last, carry, True)[1], first).astype(BF16)

        @pl.when(i == n_q - 1)
        def _():
            for jb in range(n_blocks):
                for s in range(n_sub):
                    rows = slice((jb * n_sub + s) * SB_TILE, (jb * n_sub + s + 1) * SB_TILE)
                    cols = slice(s * SB_TILE, (s + 1) * SB_TILE)
                    dk_ref[rows, :] = dk_acc[jb, :, cols].T.astype(BF16)
                    dv_ref[rows, :] = dv_acc[jb, :, cols].T.astype(BF16)

        finish()

    t = batch * seq
    blk = lambda c0: pl.BlockSpec((q_rows, LANES), lambda p, b, i: (b * n_q + i, c0 + p))
    kv = lambda c0: pl.BlockSpec((seq, LANES), lambda p, b, i: (b, c0 + p))
    row_spec = pl.BlockSpec((1, LANES), lambda p, b, i: (0, SB_PAIRS + p))
    kept_spec = pl.BlockSpec((1, 1, n_blocks, 1, 2 * q_rows, block_keys), lambda p, b, i: (p, b, 0, i, 0, 0))
    x_in, x_out, x_shape, x_scratch, x_arrays = _riding_specs(exchange)
    res = pl.pallas_call(
        body, name="sb_bwd", grid=(SB_PAIRS, batch, n_q),
        in_specs=[blk(q0), kv(k0), kv(v0), blk(0), kept_spec, kept_spec, blk(SB_PAIRS), row_spec] + x_in,
        out_specs=[blk(0), kv(0), kv(0), pl.BlockSpec((1, LANES), lambda p, b, i: (0, p))] + x_out,
        out_shape=[jax.ShapeDtypeStruct((t, GM_WIDTH), BF16)] * 3 + [jax.ShapeDtypeStruct((1, GM_WIDTH), F32)] + x_shape,
        scratch_shapes=[pltpu.VMEM((n_blocks, LANES, block_keys), F32), pltpu.VMEM((n_blocks, LANES, block_keys), F32)] + x_scratch,
        compiler_params=_params(3),
    )(proj, proj, proj, raw, weights, log_rest, dmerged, hg, *x_arrays)
    return res[0], res[1], res[2], res[3], res[4:]


def _x_softmax(s):
    s = s * X_SCALE
    p = jnp.exp(s - jnp.max(s, axis=-1, keepdims=True))
    return p * (1.0 / jnp.sum(p, axis=-1, keepdims=True))


def _x_heads(width):
    return [slice(h * X_HEAD_DIM, (h + 1) * X_HEAD_DIM) for h in range(X_HEADS)], \
           [slice(width + h * X_HEAD_DIM, width + (h + 1) * X_HEAD_DIM) for h in range(X_HEADS)]


def _xattn_fwd(q, kv, *, batch, seq, n_mem, tq=512):
    tq = min(tq, seq)
    n_q = seq // tq
    d = X_HEADS * X_HEAD_DIM

    def body(q_ref, kv_ref, o_ref):
        kcols, vcols = _x_heads(d)
        scores = [_nt(q_ref[:, c], kv_ref[:, c]) for c in kcols]
        probs = [_x_softmax(s).astype(BF16) for s in scores]
        for p, c, vc in zip(probs, kcols, vcols):
            o_ref[:, c] = _nn(p, kv_ref[:, vc]).astype(BF16)

    return pl.pallas_call(
        body, name="xattn_fwd", grid=(batch, n_q),
        in_specs=[pl.BlockSpec((tq, d), lambda b, i: (b * n_q + i, 0)), pl.BlockSpec((n_mem, 2 * d), lambda b, i: (b, 0))],
        out_specs=pl.BlockSpec((tq, d), lambda b, i: (b * n_q + i, 0)),
        out_shape=jax.ShapeDtypeStruct((batch * seq, d), BF16),
        compiler_params=_params(2),
    )(q, kv)


def _xattn_bwd(q, kv, do, *, batch, seq, n_mem, tq=512):
    tq = min(tq, seq)
    n_q = seq // tq
    d = X_HEADS * X_HEAD_DIM

    def body(q_ref, kv_ref, do_ref, dq_ref, dkv_ref, acc):
        i = pl.program_id(1)

        @pl.when(i == 0)
        def _():
            acc[...] = jnp.zeros_like(acc)

        kcols, vcols = _x_heads(d)
        scores = [_nt(q_ref[:, c], kv_ref[:, c]) for c in kcols]
        d_probs = [_nt(do_ref[:, c], kv_ref[:, vc]) for c, vc in zip(kcols, vcols)]
        probs = [_x_softmax(s) for s in scores]
        d_scores = [(p * (dp - jnp.sum(dp * p, axis=-1, keepdims=True)) * X_SCALE).astype(BF16) for p, dp in zip(probs, d_probs)]
        for p, ds, c, vc in zip(probs, d_scores, kcols, vcols):
            acc[:, vc] += _tn(p.astype(BF16), do_ref[:, c])
            dq_ref[:, c] = _nn(ds, kv_ref[:, c]).astype(BF16)
            acc[:, c] += _tn(ds, q_ref[:, c])

        @pl.when(i == n_q - 1)
        def _():
            dkv_ref[...] = acc[...].astype(BF16)

    return pl.pallas_call(
        body, name="xattn_bwd", grid=(batch, n_q),
        in_specs=[pl.BlockSpec((tq, d), lambda b, i: (b * n_q + i, 0)), pl.BlockSpec((n_mem, 2 * d), lambda b, i: (b, 0)),
                  pl.BlockSpec((tq, d), lambda b, i: (b * n_q + i, 0))],
        out_specs=[pl.BlockSpec((tq, d), lambda b, i: (b * n_q + i, 0)), pl.BlockSpec((n_mem, 2 * d), lambda b, i: (b, 0))],
        out_shape=[jax.ShapeDtypeStruct((batch * seq, d), BF16), jax.ShapeDtypeStruct((batch * n_mem, 2 * d), BF16)],
        scratch_shapes=[pltpu.VMEM((n_mem, 2 * d), F32)],
        compiler_params=_params(2),
    )(q, kv, do)


def _my_index():
    return 4 * lax.axis_index("x") + 2 * lax.axis_index("y") + lax.axis_index("c")


def _peers():
    x, y, c = lax.axis_index("x"), lax.axis_index("y"), lax.axis_index("c")
    out = []
    for rel in range(1, N_DEV):
        dx, dy, dc = (rel >> 2) & 1, (rel >> 1) & 1, rel & 1
        px, py, pc = x ^ dx, y ^ dy, c ^ dc
        out.append(((px, py, pc), 4 * px + 2 * py + pc))
    return out


class _Exchange:
    def __init__(self, arrays, scatter):
        self.arrays, self.scatter, self.n = list(arrays), scatter, len(arrays)
        any_spec = pl.BlockSpec(memory_space=pl.ANY)
        self.in_specs = [any_spec] * self.n
        self.out_specs = [any_spec] * self.n
        self.out_shape = [jax.ShapeDtypeStruct((N_DEV,) + tuple(a.shape[-2:]), a.dtype) for a in self.arrays]
        n_peer = N_DEV - 1
        self.scratch = [pltpu.SemaphoreType.DMA((self.n, n_peer)), pltpu.SemaphoreType.DMA((self.n, n_peer)),
                        pltpu.SemaphoreType.DMA((self.n,))]

    def _copies(self, srcs, dsts, sems, arriving):
        send_sems, recv_sems, local_sems = sems
        me = _my_index()
        local, remote = [], []
        for w in range(self.n):
            if not arriving:
                local.append(pltpu.make_async_copy(srcs[w].at[me] if self.scatter else srcs[w], dsts[w].at[me], local_sems.at[w]))
            for rel, (pos, idx) in enumerate(_peers()):
                remote.append(pltpu.make_async_remote_copy(
                    src_ref=srcs[w].at[idx] if self.scatter else srcs[w], dst_ref=dsts[w].at[idx if arriving else me],
                    send_sem=send_sems.at[w, rel], recv_sem=recv_sems.at[w, rel], device_id=pos, device_id_type=MESH))
        return local, remote

    def start(self, srcs, dsts, sems):
        local, sends = self._copies(srcs, dsts, sems, arriving=False)
        for cp in local + sends:
            cp.start()

    def wait(self, srcs, dsts, sems):
        for cp in self._copies(srcs, dsts, sems, arriving=True)[1]:
            cp.wait_recv()
        local, sends = self._copies(srcs, dsts, sems, arriving=False)
        for cp in sends:
            cp.wait_send()
        for cp in local:
            cp.wait()


def _two_level_gather(x_ref, slots_ref, send_sems, recv_sems, local_sem):
    x, y, c = lax.axis_index("x"), lax.axis_index("y"), lax.axis_index("c")
    me, sibling = (x, y, c), (x, y, 1 - c)
    chips = [(1 - x, y), (x, 1 - y), (1 - x, 1 - y)]

    def slot(px, py, pc):
        return slots_ref.at[4 * px + 2 * py + pc]

    def copy(k, block, to, src=None):
        return pltpu.make_async_remote_copy(src_ref=slot(*block) if src is None else src, dst_ref=slot(*block),
                                            send_sem=send_sems.at[k], recv_sem=recv_sems.at[k], device_id=to, device_id_type=MESH)

    mine = pltpu.make_async_copy(x_ref, slot(*me), local_sem)
    mine.start()
    first = [copy(0, me, sibling, src=x_ref)] + [copy(1 + j, me, (*chip, c), src=x_ref) for j, chip in enumerate(chips)]
    for cp in first:
        cp.start()
    passed = [copy(4 + j, (*chip, c), sibling) for j, chip in enumerate(chips)]
    for j, chip in enumerate(chips):
        copy(1 + j, (*chip, c), me).wait_recv()
        passed[j].start()
    copy(0, sibling, me).wait_recv()
    for j, chip in enumerate(chips):
        copy(4 + j, (*chip, 1 - c), me).wait_recv()
    for cp in first + passed:
        cp.wait_send()
    mine.wait()


_TWO_LEVEL_SEMS = [pltpu.SemaphoreType.DMA((N_DEV - 1,)), pltpu.SemaphoreType.DMA((N_DEV - 1,)), pltpu.SemaphoreType.DMA(())]


def _gather_two_level(shard, *, name):
    any_spec = pl.BlockSpec(memory_space=pl.ANY)
    return pl.pallas_call(
        _two_level_gather_body(), name=name, in_specs=[any_spec], out_specs=any_spec,
        out_shape=jax.ShapeDtypeStruct((N_DEV,) + shard.shape, shard.dtype), scratch_shapes=_TWO_LEVEL_SEMS,
    )(shard)


def _two_level_gather_body():
    def body(x_ref, out_ref, send_sems, recv_sems, local_sem):
        _two_level_gather(x_ref, out_ref, send_sems, recv_sems, local_sem)
    return body


def _all_reduce_small(part, *, loss_rows, loss_scale):
    rows = part.shape[0]

    def body(p_ref, o_ref, buf, send_sems, recv_sems, local_sem):
        _two_level_gather(p_ref, buf, send_sems, recv_sems, local_sem)
        total = buf[0]
        for dev in range(1, N_DEV):
            total = total + buf[dev]
        o_ref[...] = total
        squares = total[rows - loss_rows:]
        loss = jnp.sum(jnp.sum(squares, axis=0, keepdims=True), axis=-1, keepdims=True) * loss_scale
        o_ref[rows - loss_rows:, :] = jnp.broadcast_to(loss, (loss_rows, LANES))

    vmem = pl.BlockSpec(memory_space=pltpu.VMEM)
    return pl.pallas_call(
        body, name="all_reduce_small", in_specs=[vmem], out_specs=vmem, out_shape=jax.ShapeDtypeStruct(part.shape, F32),
        scratch_shapes=[pltpu.VMEM((N_DEV, rows, LANES), F32)] + _TWO_LEVEL_SEMS,
        compiler_params=pltpu.CompilerParams(has_side_effects=True, vmem_limit_bytes=VMEM_LIMIT),
    )(part)


def _adamw_math(w, g, m, v):
    m_new = ADAM_B1 * m + (1.0 - ADAM_B1) * g
    v_new = ADAM_B2 * v + (1.0 - ADAM_B2) * (g * g)
    m_hat = m_new / (1.0 - ADAM_B1 ** ADAM_STEP)
    v_hat = v_new / (1.0 - ADAM_B2 ** ADAM_STEP)
    delta = -ADAM_LR * (m_hat / (jnp.sqrt(v_hat) + ADAM_EPS) + ADAM_WD * w)
    return delta, m_new, v_new


def _adamw(parts, w, m, v, *, name, tr=64):
    rows, cols = w.shape
    tr = min(tr, rows)
    while rows % (2 * tr) == 0 and 2 * tr * cols <= 256 * 1024:
        tr *= 2
    assert rows % tr == 0
    stacked = parts.ndim == 3

    def body(p_ref, w_ref, m_ref, v_ref, g_ref, d_ref, mo_ref, vo_ref):
        if stacked:
            g = p_ref[0].astype(F32)
            for dev in range(1, N_DEV):
                g = g + p_ref[dev].astype(F32)
        else:
            g = p_ref[...]
        delta, m_new, v_new = _adamw_math(w_ref[...], g, m_ref[...], v_ref[...])
        g_ref[...] = g
        d_ref[...] = delta
        mo_ref[...] = m_new
        vo_ref[...] = v_new

    tile = pl.BlockSpec((tr, cols), lambda i: (i, 0))
    p_spec = pl.BlockSpec((N_DEV, tr, cols), lambda i: (0, i, 0)) if stacked else tile
    return pl.pallas_call(
        body, name=name, grid=(rows // tr,), in_specs=[p_spec, tile, tile, tile], out_specs=[tile] * 4,
        out_shape=[jax.ShapeDtypeStruct((rows, cols), F32)] * 4, compiler_params=_params(1),
    )(parts, w, m, v)


_LATER = ("w_out", "w_cq", "w_ckv", "w_co", "w_ff1", "w_ff2")


def _as_rows(stacked):
    return stacked.reshape(-1, stacked.shape[-1])


def _local_step(x, mem, target, small, shards):
    batch, seq, d = x.shape
    n_mem = mem.shape[1]
    t = batch * seq
    x2, mem2, tgt2 = x.reshape(t, d), mem.reshape(batch * n_mem, d), target.reshape(t, d)
    g_mix, g_cross, g_mem, g_ffn, g_final = (small[k] for k in ("norm_mix_g", "norm_cross_g", "norm_mem_g", "norm_ffn_g", "norm_final_g"))
    gv, hg, w_sp, b_sp_t = small["gm_v_norm_g"], small["head_norm_g"], small["w_spatial"], small["b_spatial_t"]

    win_t = _as_rows(_gather_two_level(shards["w_in"], name="gather_w_in"))
    proj, xn = _norm_mm(x2, g_mix, win_t, mode="nt", name="proj_fwd", tm=512, tn=win_t.shape[0])
    merged_a = _gmlp_fwd(proj, w_sp, b_sp_t, gv, hg)
    merged, sb_raw, sb_weights, sb_log_rest, gathered = _sb_fwd(proj, merged_a, hg, batch=batch, seq=seq,
                                                                exchange=_Exchange([shards[n] for n in _LATER], scatter=False))
    wout, wcq, wckv_t, wco, wff1_t, wff2 = (_as_rows(g) for g in gathered)
    h1, qx, hn1 = _mm(merged, wout, mode="nn", out_dtype=F32, name="mix_out_fwd_xq_fwd", tm=512, epi=_epi_residual_then_norm_mm("nn"),
                      epi_ins=(x2,), vec_ins=(g_cross,), whole_ins=(wcq,), more_outs=((wcq.shape[1], BF16), (d, BF16)))
    kvx, memn = _norm_mm(mem2, g_mem, wckv_t, mode="nt", name="xkv_fwd", tm=512, tn=wckv_t.shape[0])
    o = _xattn_fwd(qx, kvx, batch=batch, seq=seq, n_mem=n_mem)
    h2, fpre, hn2 = _mm(o, wco, mode="nn", out_dtype=F32, name="xo_fwd_ff1_fwd", tm=512, epi=_epi_residual_then_norm_mm("nt"),
                        epi_ins=(h1,), vec_ins=(g_ffn,), whole_ins=(wff1_t,), more_outs=((wff1_t.shape[0], BF16), (d, BF16)))
    dh3, final_rows = _mm(fpre, wff2, mode="nn", out_dtype=F32, name="ff2_fwd_loss", tm=512, tk=wff2.shape[0], a_fn=_relu2,
                          epi=_epi_loss, epi_ins=(h2, tgt2), vec_ins=(g_final,), aux=2)
    dg_final, sq_err = final_rows[0:1], final_rows[1:2]

    dpre = _mm(dh3, wff2, mode="nt", out_dtype=BF16, name="ff2_bwd_x", tm=512, tn=wff2.shape[0], epi=_epi_relu2_grad,
               epi_ins=(fpre,))
    chunk = wff2.shape[0] // N_DEV
    d_wff2_t = _mm(dh3, fpre, mode="tn", out_dtype=BF16, name="ff2_bwd_w", tn=2048, b_fn=_relu2, col_chunk=chunk)
    d_wff1 = _mm(hn2, dpre, mode="tn", out_dtype=BF16, name="ff1_bwd_w", tn=2048, col_chunk=chunk)
    dh2, do, dg_ffn = _mm(dpre, wff1_t, mode="nn", out_dtype=F32, name="ff1_bwd_x_xo_bwd_x", tm=512, tk=wff1_t.shape[0],
                          epi=_epi_rms_bwd_then_mm, epi_ins=(h2, dh3), vec_ins=(g_ffn,), whole_ins=(wco,),
                          more_outs=((wco.shape[0], BF16),), aux=True)
    d_wco = _mm(o, dh2, mode="tn", out_dtype=BF16, name="xo_bwd_w")
    dqx, dkvx = _xattn_bwd(qx, kvx, do, batch=batch, seq=seq, n_mem=n_mem)
    d_wcq = _mm(hn1, dqx, mode="tn", out_dtype=BF16, name="xq_bwd_w")
    dh1, dmerged, dg_cross = _mm(dqx, wcq, mode="nt", out_dtype=F32, name="xq_bwd_x_mix_out_bwd_x", tm=512,
                                 epi=_epi_rms_bwd_then_mm, epi_ins=(h1, dh2), vec_ins=(g_cross,), whole_ins=(wout,),
                                 more_outs=((wout.shape[0], BF16),), aux=True)
    d_wckv_t = _mm(dkvx, memn, mode="tn", out_dtype=BF16, name="xkv_bwd_w")
    _, dg_mem = _mm(dkvx, wckv_t, mode="nn", out_dtype=BF16, name="xkv_bwd_x", tm=512, epi=_epi_rms_gain_only,
                    epi_ins=(mem2,), vec_ins=(g_mem,), aux=True)
    d_wout = _mm(merged, dh1, mode="tn", out_dtype=BF16, name="mix_out_bwd_w")
    dp_a, d_wsp, d_bsp_t, d_gv, d_hg_a = _gmlp_bwd(proj, dmerged, w_sp, b_sp_t, gv, hg)
    d_later = {"w_out": d_wout, "w_cq": d_wcq, "w_ckv": d_wckv_t, "w_co": d_wco, "w_ff1": d_wff1, "w_ff2": d_wff2_t}
    scatter = _Exchange([g if g.ndim == 3 else g.reshape(N_DEV, -1, d) for g in (d_later[n] for n in _LATER)], scatter=True)
    dq, dk, dv, d_hg_b, received = _sb_bwd(proj, sb_raw, sb_weights, sb_log_rest, dmerged, hg, batch=batch, seq=seq, exchange=scatter)
    dproj = dp_a
    for part, at in ((dq, 2 * GM_WIDTH), (dk, 3 * GM_WIDTH), (dv, 4 * GM_WIDTH)):
        dproj = lax.dynamic_update_slice(dproj, part, (0, at))
    d_win_t = _mm(xn, dproj, mode="tn", out_dtype=BF16, name="proj_bwd_w", tn=dproj.shape[1] // 2).T
    dx, dg_mix, d_win_received = _mm(dproj, win_t, mode="nn", out_dtype=F32, name="proj_bwd_x", tm=512, tk=win_t.shape[0],
                                     epi=_epi_rms_bwd, epi_ins=(x2, dh1), vec_ins=(g_mix,), aux=True,
                                     exchange=_Exchange([d_win_t.reshape(N_DEV, -1, d)], scatter=True))

    d_small = {"norm_mix_g": dg_mix, "gm_v_norm_g": d_gv, "w_spatial": d_wsp, "b_spatial_t": d_bsp_t, "head_norm_g": jnp.concatenate([d_hg_a, d_hg_b], axis=1),
               "norm_cross_g": dg_cross, "norm_mem_g": dg_mem, "norm_ffn_g": dg_ffn, "norm_final_g": dg_final}
    d_big = dict(zip(_LATER, received))
    d_big["w_in"] = d_win_received
    return sq_err, dx.reshape(batch, seq, d), d_small, d_big


_BIG = ("w_in", "w_out", "w_cq", "w_ckv", "w_co", "w_ff1", "w_ff2")
_GATHERED_TRANSPOSED = ("w_in", "w_ckv", "w_ff1")
_UPDATED_TRANSPOSED = ("w_in", "w_ckv", "w_ff2")
_SMALL = ("norm_mix_g", "gm_v_norm_g", "w_spatial", "b_spatial", "head_norm_g", "norm_cross_g", "norm_mem_g", "norm_ffn_g", "norm_final_g")
_NAMES = ("norm_mix_g", "w_in", "gm_v_norm_g", "w_spatial", "b_spatial", "head_norm_g", "w_out", "norm_cross_g", "norm_mem_g",
          "w_cq", "w_ckv", "w_co", "norm_ffn_g", "w_ff1", "w_ff2", "norm_final_g")


def _rows_of(a):
    r = a.reshape(-1, LANES)
    pad = (-r.shape[0]) % 8
    return jnp.pad(r, ((0, pad), (0, 0))) if pad else r


def _shard2d(a, transposed):
    return a[0].T if transposed else a[0]


def kernel(x, mem, norm_mix_g, w_in, gm_v_norm_g, w_spatial, b_spatial, head_norm_g, w_out, norm_cross_g, norm_mem_g, w_cq, w_ckv, w_co, norm_ffn_g, w_ff1, w_ff2, norm_final_g, loss_target, m_norm_mix_g, m_w_in, m_gm_v_norm_g, m_w_spatial, m_b_spatial, m_head_norm_g, m_w_out, m_norm_cross_g, m_norm_mem_g, m_w_cq, m_w_ckv, m_w_co, m_norm_ffn_g, m_w_ff1, m_w_ff2, m_norm_final_g, v_norm_mix_g, v_w_in, v_gm_v_norm_g, v_w_spatial, v_b_spatial, v_head_norm_g, v_w_out, v_norm_cross_g, v_norm_mem_g, v_w_cq, v_w_ckv, v_w_co, v_norm_ffn_g, v_w_ff1, v_w_ff2, v_norm_final_g):
    weights = dict(norm_mix_g=norm_mix_g, w_in=w_in, gm_v_norm_g=gm_v_norm_g, w_spatial=w_spatial, b_spatial=b_spatial,
                   head_norm_g=head_norm_g, w_out=w_out, norm_cross_g=norm_cross_g, norm_mem_g=norm_mem_g, w_cq=w_cq, w_ckv=w_ckv,
                   w_co=w_co, norm_ffn_g=norm_ffn_g, w_ff1=w_ff1, w_ff2=w_ff2, norm_final_g=norm_final_g)
    mom1 = dict(norm_mix_g=m_norm_mix_g, w_in=m_w_in, gm_v_norm_g=m_gm_v_norm_g, w_spatial=m_w_spatial, b_spatial=m_b_spatial,
                head_norm_g=m_head_norm_g, w_out=m_w_out, norm_cross_g=m_norm_cross_g, norm_mem_g=m_norm_mem_g, w_cq=m_w_cq,
                w_ckv=m_w_ckv, w_co=m_w_co, norm_ffn_g=m_norm_ffn_g, w_ff1=m_w_ff1, w_ff2=m_w_ff2, norm_final_g=m_norm_final_g)
    mom2 = dict(norm_mix_g=v_norm_mix_g, w_in=v_w_in, gm_v_norm_g=v_gm_v_norm_g, w_spatial=v_w_spatial, b_spatial=v_b_spatial,
                head_norm_g=v_head_norm_g, w_out=v_w_out, norm_cross_g=v_norm_cross_g, norm_mem_g=v_norm_mem_g, w_cq=v_w_cq,
                w_ckv=v_w_ckv, w_co=v_w_co, norm_ffn_g=v_norm_ffn_g, w_ff1=v_w_ff1, w_ff2=v_w_ff2, norm_final_g=v_norm_final_g)

    shards = {n: _shard2d(weights[n], n in _GATHERED_TRANSPOSED).astype(BF16) for n in _BIG}
    small = {n: weights[n].reshape(1, -1) for n in _SMALL if n not in ("w_spatial", "b_spatial")}
    small["w_spatial"] = w_spatial[0]
    small["b_spatial_t"] = b_spatial[0].T
    sq_err, grad_x, d_small, d_big = _local_step(x, mem, loss_target, small, shards)

    d_small["b_spatial"] = d_small.pop("b_spatial_t").T
    sq_rows = _rows_of(sq_err)
    packed = jnp.concatenate([_rows_of(d_small[n]) for n in _SMALL] + [sq_rows], axis=0)
    summed = _all_reduce_small(packed, loss_rows=sq_rows.shape[0], loss_scale=0.5 / x.shape[-1])

    grads, deltas, new_m, new_v = {}, {}, {}, {}
    for n in _BIG:
        flip = n in _UPDATED_TRANSPOSED
        outs = _adamw(d_big[n], _shard2d(weights[n], flip), _shard2d(mom1[n], flip), _shard2d(mom2[n], flip), name="adamw_" + n)
        outs = [o.T if flip else o for o in outs]
        grads[n], deltas[n], new_m[n], new_v[n] = (o[None] for o in outs)
    pack = lambda src: jnp.concatenate([_rows_of(src[n]) for n in _SMALL], axis=0)
    n_small_rows = sum(_rows_of(weights[n]).shape[0] for n in _SMALL)
    outs = _adamw(summed[:n_small_rows], pack(weights), pack(mom1), pack(mom2), name="adamw_small", tr=n_small_rows)
    at = 0
    for n in _SMALL:
        used = weights[n].size // LANES
        for dst, o in zip((grads, deltas, new_m, new_v), outs):
            dst[n] = o[at:at + used].reshape(weights[n].shape)
        at += _rows_of(weights[n]).shape[0]
    loss = summed[n_small_rows, 0]
    return (loss, grad_x, *[grads[n] for n in _NAMES], *[deltas[n] for n in _NAMES], *[new_m[n] for n in _NAMES],
            *[new_v[n] for n in _NAMES])
```

```python
import math

import jax
import jax.numpy as jnp
from jax import lax
from jax.experimental import pallas as pl
from jax.experimental.pallas import tpu as pltpu

F32 = jnp.float32
BF16 = jnp.bfloat16
EPS = 1e-6
N_DEV = 8
LANES = 128
CHUNK = 128
GM_GROUPS = 4
GM_WIDTH = 512
SB_PAIRS = 4
SB_HEAD_DIM = 64
SB_SCALE = 0.125
SB_TILE = 128
SB_BLOCK = 512
X_HEADS = 4
X_HEAD_DIM = 256
X_SCALE = 1.0 / 16.0
VMEM_LIMIT = 56 * 1024 * 1024
ADAM_LR, ADAM_B1, ADAM_B2, ADAM_EPS, ADAM_WD, ADAM_STEP = 0.001, 0.9, 0.999, 1e-08, 0.01, 10
MESH = pl.DeviceIdType.MESH


def _params(n_axes):
    return pltpu.CompilerParams(dimension_semantics=("arbitrary",) * n_axes, vmem_limit_bytes=VMEM_LIMIT)


def _dot(a, b, dims):
    return lax.dot_general(a, b, (dims, ((), ())), preferred_element_type=F32)


def _nn(a, b):
    return _dot(a, b, ((1,), (0,)))


def _nt(a, b):
    return _dot(a, b, ((1,), (1,)))


def _tn(a, b):
    return _dot(a, b, ((0,), (0,)))


_MODES = {"nn": _nn, "nt": _nt, "tn": _tn}


def _rstd(x):
    return lax.rsqrt(jnp.mean(x * x, axis=-1, keepdims=True) + EPS)


def _gelu(x):
    c = math.sqrt(2.0 / math.pi)
    t = jnp.tanh(c * (x + 0.044715 * x * x * x))
    return 0.5 * x * (1.0 + t)


def _gelu_and_grad(x):
    c = math.sqrt(2.0 / math.pi)
    t = jnp.tanh(c * (x + 0.044715 * x * x * x))
    half = 0.5 * (1.0 + t)
    return x * half, half + 0.5 * x * (1.0 - t * t) * c * (1.0 + 3 * 0.044715 * x * x)


def _split_bf16(x):
    hi = x.astype(BF16)
    lo = (x - hi.astype(F32)).astype(BF16)
    return hi, lo


def _mm(a, b, *, mode, out_dtype, name, tm=1024, tn=1024, tk=1024, a_fn=None, b_fn=None, epi=None, epi_ins=(), vec_ins=(),
        whole_ins=(), more_outs=(), aux=False, col_chunk=None, row_parts=1, exchange=None):
    if mode == "nn":
        (m, k), (k2, n) = a.shape, b.shape
    elif mode == "nt":
        (m, k), (n, k2) = a.shape, b.shape
    else:
        (k, m), (k2, n) = a.shape, b.shape
    assert k == k2, (a.shape, b.shape, mode)
    tm, tn, tk = min(tm, m), min(tn, n), min(tk, k)
    assert m % tm == 0 and n % tn == 0 and k % tk == 0, (m, n, k, tm, tn, tk)
    n_m, n_n, n_k = m // tm, n // tn, k // tk
    assert not (aux or more_outs) or n_n == 1
    assert row_parts == 1 or (n_k == 1 and mode != "tn" and epi is not None and tm % (8 * row_parts) == 0)
    dot = _MODES[mode]
    n_epi, n_vec, n_whole, n_more = len(epi_ins), len(vec_ins), len(whole_ins), len(more_outs)

    def body(*refs):
        ins, outs, scratch, x_refs = _riding(exchange, refs, 2 + n_epi + n_vec + n_whole, 1 + n_more + (1 if aux else 0),
                                             1 if n_k > 1 else 0)
        a_ref, b_ref, epi_refs = ins[0], ins[1], ins[2:]
        o_ref, more_refs = outs[0], outs[1:1 + n_more]
        aux_ref = outs[1 + n_more] if aux else None
        acc_ref = scratch[0] if n_k > 1 else None
        i, j, kk = pl.program_id(0), pl.program_id(1), pl.program_id(2)
        ride_done = _ride(exchange, x_refs, (i == 0) & (j == 0) & (kk == 0), (i == n_m - 1) & (j == n_n - 1) & (kk == n_k - 1))
        def product(rows=slice(None)):
            av, bv = a_ref[...] if mode == "tn" else a_ref[rows, :], b_ref[...]
            if a_fn is not None:
                av = a_fn(av)
            if b_fn is not None:
                bv = b_fn(bv)
            return dot(av.astype(BF16), bv.astype(BF16))

        def finish(accs, parts=(slice(None),)):
            if col_chunk is not None:
                for ch in range(tn // col_chunk):
                    o_ref[ch] = accs[0][:, ch * col_chunk:(ch + 1) * col_chunk].astype(out_dtype)
                return
            if epi is None:
                o_ref[...] = accs[0].astype(out_dtype)
                return
            row_sums = []
            for acc, rows in zip(accs, parts):
                res = epi(acc, *[r[rows, :] for r in epi_refs[:n_epi]], *[r[...] for r in epi_refs[n_epi:]])
                if n_more:
                    for more_ref, value in zip(more_refs, res[1:1 + n_more]):
                        more_ref[rows, :] = value.astype(more_ref.dtype)
                    res = (res[0],) + tuple(res[1 + n_more:]) if aux else res[0]
                if aux:
                    res, sums = res[0], res[1:]
                    row_sums.append(sums[0] if len(sums) == 1 else jnp.concatenate(sums, axis=0))
                o_ref[rows, :] = res.astype(out_dtype)
            if aux:
                total = sum(row_sums)

                @pl.when(i == 0)
                def _():
                    aux_ref[...] = total

                @pl.when(i != 0)
                def _():
                    aux_ref[...] += total

        if n_k == 1:
            parts = [slice(h * (tm // row_parts), (h + 1) * (tm // row_parts)) for h in range(row_parts)]
            finish([product(rows) for rows in parts], parts)
        else:
            @pl.when(kk == 0)
            def _():
                acc_ref[...] = product()

            @pl.when(kk != 0)
            def _():
                acc_ref[...] += product()

            @pl.when(kk == n_k - 1)
            def _():
                finish([acc_ref[...]])

        ride_done()

    if mode == "tn":
        a_spec = pl.BlockSpec((tk, tm), lambda i, j, kk: (kk, i))
    else:
        a_spec = pl.BlockSpec((tm, tk), lambda i, j, kk: (i, kk))
    if mode == "nt":
        b_spec = pl.BlockSpec((tn, tk), lambda i, j, kk: (j, kk))
    else:
        b_spec = pl.BlockSpec((tk, tn), lambda i, j, kk: (kk, j))
    tile_spec = pl.BlockSpec((tm, tn), lambda i, j, kk: (i, j))
    row_spec = pl.BlockSpec((1, tn), lambda i, j, kk: (0, j))
    out_shape = [jax.ShapeDtypeStruct((m, n), out_dtype)]
    out_specs = [tile_spec]
    if col_chunk is not None:
        assert epi is None and not aux and tn % col_chunk == 0
        out_shape = [jax.ShapeDtypeStruct((n // col_chunk, m, col_chunk), out_dtype)]
        out_specs = [pl.BlockSpec((tn // col_chunk, tm, col_chunk), lambda i, j, kk: (j, i, 0))]
    for columns, dtype in more_outs:
        out_shape.append(jax.ShapeDtypeStruct((m, columns), dtype))
        out_specs.append(pl.BlockSpec((tm, columns), lambda i, j, kk: (i, 0)))
    if aux:
        out_shape.append(jax.ShapeDtypeStruct((int(aux), n), F32))
        out_specs.append(pl.BlockSpec((int(aux), tn), lambda i, j, kk: (0, j)))
    whole_specs = [pl.BlockSpec(w.shape, lambda i, j, kk, nd=w.ndim: (0,) * nd) for w in whole_ins]
    x_in, x_out, x_shape, x_scratch, x_arrays = _riding_specs(exchange)
    res = pl.pallas_call(
        body, name=name, grid=(n_m, n_n, n_k),
        in_specs=[a_spec, b_spec] + [tile_spec] * n_epi + [row_spec] * n_vec + whole_specs + x_in,
        out_specs=out_specs + x_out, out_shape=out_shape + x_shape,
        scratch_shapes=([pltpu.VMEM((tm, tn), F32)] if n_k > 1 else []) + x_scratch,
        compiler_params=_params(3),
    )(a, b, *epi_ins, *vec_ins, *whole_ins, *x_arrays)
    if exchange is not None or more_outs:
        return tuple(res)
    return res if aux else res[0]


def _norm_mm(x, g, w, *, mode, name, tm=1024, tn=1024):
    m, d = x.shape
    n = w.shape[0] if mode == "nt" else w.shape[1]
    tm, tn = min(tm, m), min(tn, n)
    assert m % tm == 0 and n % tn == 0
    dot = _MODES[mode]

    def body(x_ref, g_ref, w_ref, o_ref, xn_ref, xn_s):
        @pl.when(pl.program_id(1) == 0)
        def _():
            xv = x_ref[...]
            xn = (xv * _rstd(xv) * g_ref[...]).astype(BF16)
            xn_s[...] = xn
            xn_ref[...] = xn

        o_ref[...] = dot(xn_s[...], w_ref[...]).astype(BF16)

    w_spec = pl.BlockSpec((tn, d), lambda i, j: (j, 0)) if mode == "nt" else pl.BlockSpec((d, tn), lambda i, j: (0, j))
    return pl.pallas_call(
        body, name=name, grid=(m // tm, n // tn),
        in_specs=[pl.BlockSpec((tm, d), lambda i, j: (i, 0)), pl.BlockSpec((1, d), lambda i, j: (0, 0)), w_spec],
        out_specs=[pl.BlockSpec((tm, tn), lambda i, j: (i, j)), pl.BlockSpec((tm, d), lambda i, j: (i, 0))],
        out_shape=[jax.ShapeDtypeStruct((m, n), BF16), jax.ShapeDtypeStruct((m, d), BF16)],
        scratch_shapes=[pltpu.VMEM((tm, d), BF16)],
        compiler_params=_params(2),
    )(x, g, w)


def _epi_residual(acc, res):
    return res + acc


def _epi_relu2_grad(acc, pre):
    return acc * (2.0 * jnp.maximum(pre.astype(F32), 0.0))


def _relu2(pre):
    r = jnp.maximum(pre.astype(F32), 0.0)
    return r * r


def _epi_rms_bwd(acc, h, dres, g):
    r = _rstd(h)
    xh = h * r
    dxh = acc * g
    dh = dres + r * (dxh - xh * jnp.mean(dxh * xh, axis=-1, keepdims=True))
    return dh, jnp.sum(acc * xh, axis=0, keepdims=True)


def _epi_residual_then_norm_mm(mode):
    dot = _MODES[mode]

    def epi(acc, res, g, w):
        h = res + acc
        hn = (h * _rstd(h) * g).astype(BF16)
        return h, dot(hn, w), hn

    return epi


def _epi_rms_bwd_then_mm(acc, h, dres, g, w):
    dh, row = _epi_rms_bwd(acc, h, dres, g)
    return dh, _nt(dh.astype(BF16), w), row


def _epi_loss(acc, h_in, target, g):
    h = h_in + acc
    r = _rstd(h)
    xh = h * r
    err = xh * g - target
    dy = err * (1.0 / h.shape[-1])
    dxh = dy * g
    dh = r * (dxh - xh * jnp.mean(dxh * xh, axis=-1, keepdims=True))
    return dh, jnp.sum(dy * xh, axis=0, keepdims=True), jnp.sum(err * err, axis=0, keepdims=True)


def _epi_rms_gain_only(acc, h, g):
    return acc, jnp.sum(acc * (h * _rstd(h)), axis=0, keepdims=True)


def _tril(n):
    row = lax.broadcasted_iota(jnp.int32, (n, n), 0)
    col = lax.broadcasted_iota(jnp.int32, (n, n), 1)
    return col <= row


def _gmlp_fwd(proj, w_sp, b_sp_t, gv, hg, *, rows=512):
    t = proj.shape[0]
    rows = min(rows, t)
    n_c = rows // CHUNK

    def body(u_ref, v_ref, w_ref, bt_ref, gv_ref, hg_ref, m_ref):
        keep = _tril(CHUNK)
        for g in range(GM_GROUPS):
            cols = slice(g * LANES, (g + 1) * LANES)
            wg = jnp.where(keep, w_ref[g], 0.0).astype(BF16)
            u = _gelu(u_ref[:, cols].astype(F32))
            v = _gelu(v_ref[:, cols].astype(F32))
            vn = (v * _rstd(v) * gv_ref[:, cols]).astype(BF16)
            bias = bt_ref[:, g:g + 1]
            chunks = [slice(c * CHUNK, (c + 1) * CHUNK) for c in range(n_c)]
            gated = [u[rs] * (_nn(wg, vn[rs]) + bias) for rs in chunks]
            for rs, a in zip(chunks, gated):
                m_ref[rs, cols] = (a * _rstd(a) * hg_ref[:, cols]).astype(BF16)

    full = lambda shape: pl.BlockSpec(shape, lambda i: (0,) * len(shape))
    return pl.pallas_call(
        body, name="gmlp_fwd", grid=(t // rows,),
        in_specs=[pl.BlockSpec((rows, GM_WIDTH), lambda i: (i, 0)), pl.BlockSpec((rows, GM_WIDTH), lambda i: (i, 1)),
                  full((GM_GROUPS, CHUNK, CHUNK)), full((CHUNK, GM_GROUPS)), full((1, GM_WIDTH)), full((1, GM_WIDTH))],
        out_specs=pl.BlockSpec((rows, GM_WIDTH), lambda i: (i, 0)),
        out_shape=jax.ShapeDtypeStruct((t, 2 * GM_WIDTH), BF16),
        compiler_params=_params(1),
    )(proj, proj, w_sp, b_sp_t, gv, hg)


def _gmlp_bwd(proj, dmerged, w_sp, b_sp_t, gv, hg, *, rows=512):
    t = proj.shape[0]
    rows = min(rows, t)
    n_c = rows // CHUNK
    n_steps = t // rows

    def body(u_ref, v_ref, dm_ref, w_ref, bt_ref, gv_ref, hg_ref, dp_ref, dw_ref, dbt_ref, dgv_ref, dhg_ref, db_acc):
        step = pl.program_id(0)
        keep = _tril(CHUNK)

        @pl.when(step == 0)
        def _():
            dw_ref[...] = jnp.zeros_like(dw_ref)
            db_acc[...] = jnp.zeros_like(db_acc)
            dgv_ref[...] = jnp.zeros_like(dgv_ref)
            dhg_ref[...] = jnp.zeros_like(dhg_ref)

        for g in range(GM_GROUPS):
            cols = slice(g * LANES, (g + 1) * LANES)
            wg = jnp.where(keep, w_ref[g], 0.0).astype(BF16)
            u, u_slope = _gelu_and_grad(u_ref[:, cols].astype(F32))
            v, v_slope = _gelu_and_grad(v_ref[:, cols].astype(F32))
            r = _rstd(v)
            xh = v * r
            gvg = gv_ref[:, cols]
            hgg = hg_ref[:, cols]
            vn = (xh * gvg).astype(BF16)
            bias = bt_ref[:, g:g + 1]
            dm = dm_ref[:, cols].astype(F32)
            chunks = [slice(c * CHUNK, (c + 1) * CHUNK) for c in range(n_c)]
            mixed = [_nn(wg, vn[rs]) + bias for rs in chunks]
            gated = [u[rs] * mx for rs, mx in zip(chunks, mixed)]
            scale = [_rstd(a) for a in gated]
            normed = [a * ra for a, ra in zip(gated, scale)]
            d_normed = [dm[rs] * hgg for rs in chunks]
            d_gated = [ra * (dan - an * jnp.mean(dan * an, axis=-1, keepdims=True)) for ra, dan, an in zip(scale, d_normed, normed)]
            d_mixed = [da * u[rs] for da, rs in zip(d_gated, chunks)]
            d_mixed_b = [dmx.astype(BF16) for dmx in d_mixed]
            du = jnp.concatenate([da * mx for da, mx in zip(d_gated, mixed)], axis=0)
            dvn = jnp.concatenate([_tn(wg, dmb) for dmb in d_mixed_b], axis=0)
            dw_ref[g] += sum(_nt(dmb, vn[rs]) for dmb, rs in zip(d_mixed_b, chunks))
            db_acc[g] += sum(d_mixed)
            dhg_ref[:, cols] += sum(jnp.sum(dm[rs] * an, axis=0, keepdims=True) for rs, an in zip(chunks, normed))
            dgv_ref[:, cols] += jnp.sum(dvn * xh, axis=0, keepdims=True)
            dxh = dvn * gvg
            dv = r * (dxh - xh * jnp.mean(dxh * xh, axis=-1, keepdims=True))
            dp_ref[:, cols] = (du * u_slope).astype(BF16)
            dp_ref[:, GM_WIDTH + g * LANES:GM_WIDTH + (g + 1) * LANES] = (dv * v_slope).astype(BF16)

        @pl.when(step == n_steps - 1)
        def _():
            for g in range(GM_GROUPS):
                dw_ref[g] = jnp.where(keep, dw_ref[g], 0.0)
                dbt_ref[:, g:g + 1] = jnp.sum(db_acc[g], axis=-1, keepdims=True)

    full = lambda shape: pl.BlockSpec(shape, lambda i: (0,) * len(shape))
    return pl.pallas_call(
        body, name="gmlp_bwd", grid=(n_steps,),
        in_specs=[pl.BlockSpec((rows, GM_WIDTH), lambda i: (i, 0)), pl.BlockSpec((rows, GM_WIDTH), lambda i: (i, 1)),
                  pl.BlockSpec((rows, GM_WIDTH), lambda i: (i, 0)),
                  full((GM_GROUPS, CHUNK, CHUNK)), full((CHUNK, GM_GROUPS)), full((1, GM_WIDTH)), full((1, GM_WIDTH))],
        out_specs=[pl.BlockSpec((rows, 2 * GM_WIDTH), lambda i: (i, 0)), full((GM_GROUPS, CHUNK, CHUNK)),
                   full((CHUNK, GM_GROUPS)), full((1, GM_WIDTH)), full((1, GM_WIDTH))],
        out_shape=[jax.ShapeDtypeStruct((t, proj.shape[1]), BF16), jax.ShapeDtypeStruct((GM_GROUPS, CHUNK, CHUNK), F32),
                   jax.ShapeDtypeStruct((CHUNK, GM_GROUPS), F32), jax.ShapeDtypeStruct((1, GM_WIDTH), F32),
                   jax.ShapeDtypeStruct((1, GM_WIDTH), F32)],
        scratch_shapes=[pltpu.VMEM((GM_GROUPS, CHUNK, LANES), F32)],
        compiler_params=_params(1),
    )(proj, proj, dmerged, w_sp, b_sp_t, gv, hg)


def _sb_logits(z, strict):
    ls = jnp.minimum(z, 0.0) - jnp.log(1.0 + jnp.exp(-jnp.abs(z)))
    l1m = ls - z
    if strict is not None:
        l1m = jnp.where(strict, l1m, 0.0)
    return ls, l1m


def _tri_sums(x, tri):
    hi, lo = _split_bf16(x)
    return _nn(jnp.concatenate([hi, lo], axis=1), jnp.concatenate([tri, tri], axis=0))


def _sb_weights(ls, in_tile, right, strict):
    a = jnp.exp(ls + in_tile + right)
    if strict is not None:
        a = jnp.where(strict, a, 0.0)
    return a


def _sb_masks(q_rows):
    row = lax.broadcasted_iota(jnp.int32, (SB_TILE, SB_TILE), 0)
    col = lax.broadcasted_iota(jnp.int32, (SB_TILE, SB_TILE), 1)
    lane = lax.broadcasted_iota(jnp.int32, (q_rows, LANES), 1)
    return row, col, lane < SB_HEAD_DIM


def _stack_heads(x, first):
    zero = jnp.zeros_like(x)
    return jnp.concatenate([jnp.where(first, x, zero), jnp.where(first, zero, x)], axis=0)


def _stack_heads_t(x_t):
    first_t = lax.broadcasted_iota(jnp.int32, x_t.shape, 0) < SB_HEAD_DIM
    zero = jnp.zeros_like(x_t)
    return jnp.concatenate([jnp.where(first_t, x_t, zero), jnp.where(first_t, zero, x_t)], axis=1).astype(BF16)


def _unstack_heads(x2, first):
    half = x2.shape[0] // 2
    return jnp.where(first, x2[:half], x2[half:])


def _live_rows(x, s, q_rows):
    if s == 0:
        return x
    return jnp.concatenate([x[s * SB_TILE:q_rows], x[q_rows + s * SB_TILE:]], axis=0)


def _spread_rows(x, s, q_rows):
    if s == 0:
        return x
    half = q_rows - s * SB_TILE
    zero = jnp.zeros((s * SB_TILE,) + x.shape[1:], x.dtype)
    return jnp.concatenate([zero, x[:half], zero, x[half:]], axis=0)


def _stacked_col_minus_row(q_rows):
    row = lax.broadcasted_iota(jnp.int32, (2 * q_rows, SB_TILE), 0)
    col = lax.broadcasted_iota(jnp.int32, (2 * q_rows, SB_TILE), 1)
    return col - (row & (q_rows - 1))


def _head_mean(x, first):
    s0 = jnp.sum(jnp.where(first, x, 0.0), axis=-1, keepdims=True)
    s1 = jnp.sum(jnp.where(first, 0.0, x), axis=-1, keepdims=True)
    return jnp.where(first, s0, s1) * (1.0 / SB_HEAD_DIM)


def _riding(exchange, refs, n_in, n_out, n_scratch):
    n_x = exchange.n if exchange is not None else 0
    ins, rest = refs[:n_in], refs[n_in:]
    x_src, rest = rest[:n_x], rest[n_x:]
    outs, rest = rest[:n_out], rest[n_out:]
    x_dst, rest = rest[:n_x], rest[n_x:]
    return ins, outs, rest[:n_scratch], (x_src, x_dst, rest[n_scratch:])


def _riding_specs(exchange):
    if exchange is None:
        return [], [], [], [], []
    return exchange.in_specs, exchange.out_specs, exchange.out_shape, exchange.scratch, exchange.arrays


def _ride(exchange, x_refs, first_step, last_step):
    if exchange is None:
        return lambda: None

    @pl.when(first_step)
    def _():
        exchange.start(*x_refs)

    def finish():
        @pl.when(last_step)
        def _():
            exchange.wait(*x_refs)

    return finish


def _sb_fwd(proj, merged_a, hg, *, batch, seq, exchange=None):
    q0, k0, v0 = 2 * GM_WIDTH // LANES, 2 * GM_WIDTH // LANES + SB_PAIRS, 2 * GM_WIDTH // LANES + 2 * SB_PAIRS
    q_rows = block_keys = min(SB_BLOCK, seq)
    assert seq % block_keys == 0
    n_q, n_sub, n_blocks = seq // q_rows, block_keys // SB_TILE, seq // block_keys

    def body(*refs):
        (q_ref, k_ref, v_ref, hg_ref, _), (m_ref, raw_ref, a_ref, l_ref), _, x_refs = _riding(exchange, refs, 5, 4, 0)
        b, p, i = pl.program_id(0), pl.program_id(1), pl.program_id(2)
        finish = _ride(exchange, x_refs, (b == 0) & (p == 0) & (i == 0), (b == batch - 1) & (p == SB_PAIRS - 1) & (i == n_q - 1))
        row, col, first = _sb_masks(q_rows)
        upper = (row > col).astype(BF16)
        q2 = _stack_heads((q_ref[...].astype(F32) * SB_SCALE).astype(BF16), first)
        diff = _stacked_col_minus_row(q_rows)
        last = i

        def key_tile(jb, s):
            return k_ref[pl.ds(pl.multiple_of((jb * n_sub + s) * SB_TILE, SB_TILE), SB_TILE), :]

        def scores(jb):
            return tuple(_nt(q2, key_tile(jb, s)) for s in range(n_sub))

        def keep_for_backward(ref, jb, s, stacked):
            ref[0, 0, jb, 0, :, s * SB_TILE:(s + 1) * SB_TILE] = stacked

        def weights_of(jb, z, right):
            logits = [_sb_logits(z[s], None) for s in reversed(range(n_sub))][::-1]
            totals = [jnp.sum(l1m, axis=-1, keepdims=True) for _, l1m in logits]
            sums = [_tri_sums(l1m, upper) for _, l1m in reversed(logits)][::-1]
            for s in reversed(range(n_sub)):
                keep_for_backward(a_ref, jb, s, _sb_weights(logits[s][0], sums[s], right, None).astype(BF16))
                keep_for_backward(l_ref, jb, s, logits[s][1].astype(BF16))
                right = right + totals[s]
            return right

        def diagonal_weights():
            keeps = [_live_rows(diff, s, q_rows) < -s * SB_TILE for s in range(n_sub)]
            logits = [_sb_logits(_nt(_live_rows(q2, s, q_rows), key_tile(last, s)), keeps[s]) for s in range(n_sub)]
            totals = [jnp.sum(l1m, axis=-1, keepdims=True) for _, l1m in logits]
            sums = [_tri_sums(l1m, upper) for _, l1m in logits]
            right = jnp.zeros((2 * q_rows, 1), F32)
            for s in reversed(range(n_sub)):
                live = _sb_weights(logits[s][0], sums[s], _live_rows(right, s, q_rows), keeps[s]).astype(BF16)
                keep_for_backward(a_ref, last, s, _spread_rows(live, s, q_rows))
                keep_for_backward(l_ref, last, s, _spread_rows(logits[s][1].astype(BF16), s, q_rows))
                right = right + _spread_rows(totals[s], s, q_rows)
            return right

        def weighted_values(jb):
            return _nn(a_ref[0, 0, jb, 0], v_ref[pl.ds(pl.multiple_of(jb * block_keys, block_keys), block_keys), :])

        right = diagonal_weights()
        z_next = scores(jnp.maximum(last - 1, 0))

        def step(k, carry):
            right, acc, z = carry
            jb = last - k
            acc = acc + weighted_values(jb + 1)
            z_next = scores(jnp.maximum(jb - 1, 0))
            return weights_of(jb, z, right), acc, z_next

        _, acc2, _ = lax.fori_loop(1, last + 1, step, (right, jnp.zeros((2 * q_rows, LANES), F32), z_next))
        acc = _unstack_heads(acc2 + weighted_values(0), first)
        raw_ref[...] = acc
        m_ref[...] = (acc * lax.rsqrt(_head_mean(acc * acc, first) + EPS) * hg_ref[...]).astype(BF16)
        finish()

    t = batch * seq
    blk = lambda c0: pl.BlockSpec((q_rows, LANES), lambda b, p, i: (b * n_q + i, c0 + p))
    kv = lambda c0: pl.BlockSpec((seq, LANES), lambda b, p, i: (b, c0 + p))
    kept = pl.BlockSpec((1, 1, n_blocks, 1, 2 * q_rows, block_keys), lambda b, p, i: (p, b, 0, i, 0, 0))
    kept_shape = jax.ShapeDtypeStruct((SB_PAIRS, batch, n_blocks, n_q, 2 * q_rows, block_keys), BF16)
    x_in, x_out, x_shape, x_scratch, x_arrays = _riding_specs(exchange)
    res = pl.pallas_call(
        body, name="sb_fwd", grid=(batch, SB_PAIRS, n_q),
        in_specs=[blk(q0), kv(k0), kv(v0), pl.BlockSpec((1, LANES), lambda b, p, i: (0, SB_PAIRS + p)),
                  pl.BlockSpec(memory_space=pl.ANY)] + x_in,
        out_specs=[blk(SB_PAIRS), blk(0), kept, kept] + x_out,
        out_shape=[jax.ShapeDtypeStruct((t, 2 * GM_WIDTH), BF16), jax.ShapeDtypeStruct((t, GM_WIDTH), F32), kept_shape, kept_shape] + x_shape,
        scratch_shapes=x_scratch,
        input_output_aliases={4: 0},
        compiler_params=_params(3),
    )(proj, proj, proj, hg, merged_a, *x_arrays)
    return res[0], res[1], res[2], res[3], res[4:]


def _sb_bwd(proj, raw, weights, log_rest, dmerged, hg, *, batch, seq, exchange=None):
    q0, k0, v0 = 2 * GM_WIDTH // LANES, 2 * GM_WIDTH // LANES + SB_PAIRS, 2 * GM_WIDTH // LANES + 2 * SB_PAIRS
    q_rows = block_keys = min(SB_BLOCK, seq)
    assert seq % block_keys == 0
    n_q, n_sub, n_blocks = seq // q_rows, block_keys // SB_TILE, seq // block_keys

    def body(*refs):
        ins, outs, (dk_acc, dv_acc), x_refs = _riding(exchange, refs, 8, 4, 2)
        q_ref, k_ref, v_ref, raw_ref, a_ref, l_ref, dm_ref, hg_ref = ins
        dq_ref, dk_ref, dv_ref, dhg_ref = outs
        p, b, i = pl.program_id(0), pl.program_id(1), pl.program_id(2)
        finish = _ride(exchange, x_refs, (b == 0) & (p == 0) & (i == 0), (b == batch - 1) & (p == SB_PAIRS - 1) & (i == n_q - 1))
        row, col, first = _sb_masks(q_rows)
        lower = (row < col).astype(BF16)

        @pl.when(jnp.logical_and(b == 0, i == 0))
        def _():
            dhg_ref[...] = jnp.zeros_like(dhg_ref)

        @pl.when(i == 0)
        def _():
            dk_acc[...] = jnp.zeros_like(dk_acc)
            dv_acc[...] = jnp.zeros_like(dv_acc)

        raw_v = raw_ref[...]
        dm = dm_ref[...].astype(F32)
        r = lax.rsqrt(_head_mean(raw_v * raw_v, first) + EPS)
        nrm = raw_v * r
        dhg_ref[...] += jnp.sum(dm * nrm, axis=0, keepdims=True)
        dn = dm * hg_ref[...]
        dout = r * (dn - nrm * _head_mean(dn * nrm, first))
        dout2 = _stack_heads(dout.astype(BF16), first)
        q2_t = _stack_heads_t(q_ref[...].astype(F32).T)
        dout2_t = _stack_heads_t(dout.T)
        diff = _stacked_col_minus_row(q_rows)
        last = i

        def kept(ref, jb, cols):
            return ref[0, 0, jb, 0, :, cols]

        def block(jb, carry, diagonal):
            gleft, dq = carry
            live = (lambda x, s: _live_rows(x, s, q_rows)) if diagonal else (lambda x, s: x)
            spread = (lambda x, s: _spread_rows(x, s, q_rows)) if diagonal else (lambda x, s: x)
            tiles = [pl.ds(pl.multiple_of((jb * n_sub + s) * SB_TILE, SB_TILE), SB_TILE) for s in range(n_sub)]
            cols = [slice(s * SB_TILE, (s + 1) * SB_TILE) for s in range(n_sub)]
            gmats = [_nt(live(dout2, s), v_ref[tiles[s], :]) * live(kept(a_ref, jb, cols[s]), s).astype(F32) for s in range(n_sub)]
            prefixes = [_tri_sums(g, lower) for g in gmats]
            dzs = []
            for s in range(n_sub):
                one_minus = jnp.exp(live(kept(l_ref, jb, cols[s]), s).astype(F32))
                dz = (gmats[s] * one_minus - (live(gleft, s) + prefixes[s]) * (1.0 - one_minus)) * SB_SCALE
                gleft = gleft + spread(jnp.sum(gmats[s], axis=-1, keepdims=True), s)
                if diagonal:
                    dz = jnp.where(live(diff, s) < -s * SB_TILE, dz, 0.0)
                dzs.append(spread(dz.astype(BF16), s))
            dz_all = jnp.concatenate(dzs, axis=1)
            dk_acc[jb] += _nn(q2_t, dz_all)
            dv_acc[jb] += _nn(dout2_t, kept(a_ref, jb, slice(None)))
            return gleft, dq + _nn(dz_all, k_ref[pl.ds(pl.multiple_of(jb * block_keys, block_keys), block_keys), :])

        carry = (jnp.zeros((2 * q_rows, 1), F32), jnp.zeros((2 * q_rows, LANES), F32))
        carry = lax.fori_loop(0, last, lambda jb, c: block(jb, c, False), carry)
        dq_ref[...] = _unstack_heads(block(last, carry, True)[1], first).astype(BF16)

        @pl.when(i == n_q - 1)
        def _():
            for jb in range(n_blocks):
                for s in range(n_sub):
                    rows = slice((jb * n_sub + s) * SB_TILE, (jb * n_sub + s + 1) * SB_TILE)
                    cols = slice(s * SB_TILE, (s + 1) * SB_TILE)
                    dk_ref[rows, :] = dk_acc[jb, :, cols].T.astype(BF16)
                    dv_ref[rows, :] = dv_acc[jb, :, cols].T.astype(BF16)

        finish()

    t = batch * seq
    blk = lambda c0: pl.BlockSpec((q_rows, LANES), lambda p, b, i: (b * n_q + i, c0 + p))
    kv = lambda c0: pl.BlockSpec((seq, LANES), lambda p, b, i: (b, c0 + p))
    row_spec = pl.BlockSpec((1, LANES), lambda p, b, i: (0, SB_PAIRS + p))
    kept_spec = pl.BlockSpec((1, 1, n_blocks, 1, 2 * q_rows, block_keys), lambda p, b, i: (p, b, 0, i, 0, 0))
    x_in, x_out, x_shape, x_scratch, x_arrays = _riding_specs(exchange)
    res = pl.pallas_call(
        body, name="sb_bwd", grid=(SB_PAIRS, batch, n_q),
        in_specs=[blk(q0), kv(k0), kv(v0), blk(0), kept_spec, kept_spec, blk(SB_PAIRS), row_spec] + x_in,
        out_specs=[blk(0), kv(0), kv(0), pl.BlockSpec((1, LANES), lambda p, b, i: (0, p))] + x_out,
        out_shape=[jax.ShapeDtypeStruct((t, GM_WIDTH), BF16)] * 3 + [jax.ShapeDtypeStruct((1, GM_WIDTH), F32)] + x_shape,
        scratch_shapes=[pltpu.VMEM((n_blocks, LANES, block_keys), F32), pltpu.VMEM((n_blocks, LANES, block_keys), F32)] + x_scratch,
        compiler_params=_params(3),
    )(proj, proj, proj, raw, weights, log_rest, dmerged, hg, *x_arrays)
    return res[0], res[1], res[2], res[3], res[4:]


def _x_softmax(s):
    s = s * X_SCALE
    p = jnp.exp(s - jnp.max(s, axis=-1, keepdims=True))
    return p * (1.0 / jnp.sum(p, axis=-1, keepdims=True))


def _x_heads(width):
    return [slice(h * X_HEAD_DIM, (h + 1) * X_HEAD_DIM) for h in range(X_HEADS)], \
           [slice(width + h * X_HEAD_DIM, width + (h + 1) * X_HEAD_DIM) for h in range(X_HEADS)]


def _xattn_fwd(q, kv, *, batch, seq, n_mem, tq=512):
    tq = min(tq, seq)
    n_q = seq // tq
    d = X_HEADS * X_HEAD_DIM

    def body(q_ref, kv_ref, o_ref):
        kcols, vcols = _x_heads(d)
        scores = [_nt(q_ref[:, c], kv_ref[:, c]) for c in kcols]
        probs = [_x_softmax(s).astype(BF16) for s in scores]
        for p, c, vc in zip(probs, kcols, vcols):
            o_ref[:, c] = _nn(p, kv_ref[:, vc]).astype(BF16)

    return pl.pallas_call(
        body, name="xattn_fwd", grid=(batch, n_q),
        in_specs=[pl.BlockSpec((tq, d), lambda b, i: (b * n_q + i, 0)), pl.BlockSpec((n_mem, 2 * d), lambda b, i: (b, 0))],
        out_specs=pl.BlockSpec((tq, d), lambda b, i: (b * n_q + i, 0)),
        out_shape=jax.ShapeDtypeStruct((batch * seq, d), BF16),
        compiler_params=_params(2),
    )(q, kv)


def _xattn_bwd(q, kv, do, *, batch, seq, n_mem, tq=512):
    tq = min(tq, seq)
    n_q = seq // tq
    d = X_HEADS * X_HEAD_DIM

    def body(q_ref, kv_ref, do_ref, dq_ref, dkv_ref, acc):
        i = pl.program_id(1)

        @pl.when(i == 0)
        def _():
            acc[...] = jnp.zeros_like(acc)

        kcols, vcols = _x_heads(d)
        scores = [_nt(q_ref[:, c], kv_ref[:, c]) for c in kcols]
        d_probs = [_nt(do_ref[:, c], kv_ref[:, vc]) for c, vc in zip(kcols, vcols)]
        probs = [_x_softmax(s) for s in scores]
        d_scores = [(p * (dp - jnp.sum(dp * p, axis=-1, keepdims=True)) * X_SCALE).astype(BF16) for p, dp in zip(probs, d_probs)]
        for p, ds, c, vc in zip(probs, d_scores, kcols, vcols):
            acc[:, vc] += _tn(p.astype(BF16), do_ref[:, c])
            dq_ref[:, c] = _nn(ds, kv_ref[:, c]).astype(BF16)
            acc[:, c] += _tn(ds, q_ref[:, c])

        @pl.when(i == n_q - 1)
        def _():
            dkv_ref[...] = acc[...].astype(BF16)

    return pl.pallas_call(
        body, name="xattn_bwd", grid=(batch, n_q),
        in_specs=[pl.BlockSpec((tq, d), lambda b, i: (b * n_q + i, 0)), pl.BlockSpec((n_mem, 2 * d), lambda b, i: (b, 0)),
                  pl.BlockSpec((tq, d), lambda b, i: (b * n_q + i, 0))],
        out_specs=[pl.BlockSpec((tq, d), lambda b, i: (b * n_q + i, 0)), pl.BlockSpec((n_mem, 2 * d), lambda b, i: (b, 0))],
        out_shape=[jax.ShapeDtypeStruct((batch * seq, d), BF16), jax.ShapeDtypeStruct((batch * n_mem, 2 * d), BF16)],
        scratch_shapes=[pltpu.VMEM((n_mem, 2 * d), F32)],
        compiler_params=_params(2),
    )(q, kv, do)


def _my_index():
    return 4 * lax.axis_index("x") + 2 * lax.axis_index("y") + lax.axis_index("c")


def _peers():
    x, y, c = lax.axis_index("x"), lax.axis_index("y"), lax.axis_index("c")
    out = []
    for rel in range(1, N_DEV):
        dx, dy, dc = (rel >> 2) & 1, (rel >> 1) & 1, rel & 1
        px, py, pc = x ^ dx, y ^ dy, c ^ dc
        out.append(((px, py, pc), 4 * px + 2 * py + pc))
    return out


class _Exchange:
    def __init__(self, arrays, scatter):
        self.arrays, self.scatter, self.n = list(arrays), scatter, len(arrays)
        any_spec = pl.BlockSpec(memory_space=pl.ANY)
        self.in_specs = [any_spec] * self.n
        self.out_specs = [any_spec] * self.n
        self.out_shape = [jax.ShapeDtypeStruct((N_DEV,) + tuple(a.shape[-2:]), a.dtype) for a in self.arrays]
        n_peer = N_DEV - 1
        self.scratch = [pltpu.SemaphoreType.DMA((self.n, n_peer)), pltpu.SemaphoreType.DMA((self.n, n_peer)),
                        pltpu.SemaphoreType.DMA((self.n,))]

    def _copies(self, srcs, dsts, sems, arriving):
        send_sems, recv_sems, local_sems = sems
        me = _my_index()
        local, remote = [], []
        for w in range(self.n):
            if not arriving:
                local.append(pltpu.make_async_copy(srcs[w].at[me] if self.scatter else srcs[w], dsts[w].at[me], local_sems.at[w]))
            for rel, (pos, idx) in enumerate(_peers()):
                remote.append(pltpu.make_async_remote_copy(
                    src_ref=srcs[w].at[idx] if self.scatter else srcs[w], dst_ref=dsts[w].at[idx if arriving else me],
                    send_sem=send_sems.at[w, rel], recv_sem=recv_sems.at[w, rel], device_id=pos, device_id_type=MESH))
        return local, remote

    def start(self, srcs, dsts, sems):
        local, sends = self._copies(srcs, dsts, sems, arriving=False)
        for cp in local + sends:
            cp.start()

    def wait(self, srcs, dsts, sems):
        for cp in self._copies(srcs, dsts, sems, arriving=True)[1]:
            cp.wait_recv()
        local, sends = self._copies(srcs, dsts, sems, arriving=False)
        for cp in sends:
            cp.wait_send()
        for cp in local:
            cp.wait()


def _two_level_gather(x_ref, slots_ref, send_sems, recv_sems, local_sem):
    x, y, c = lax.axis_index("x"), lax.axis_index("y"), lax.axis_index("c")
    me, sibling = (x, y, c), (x, y, 1 - c)
    chips = [(1 - x, y), (x, 1 - y), (1 - x, 1 - y)]

    def slot(px, py, pc):
        return slots_ref.at[4 * px + 2 * py + pc]

    def copy(k, block, to, src=None):
        return pltpu.make_async_remote_copy(src_ref=slot(*block) if src is None else src, dst_ref=slot(*block),
                                            send_sem=send_sems.at[k], recv_sem=recv_sems.at[k], device_id=to, device_id_type=MESH)

    mine = pltpu.make_async_copy(x_ref, slot(*me), local_sem)
    mine.start()
    first = [copy(0, me, sibling, src=x_ref)] + [copy(1 + j, me, (*chip, c), src=x_ref) for j, chip in enumerate(chips)]
    for cp in first:
        cp.start()
    passed = [copy(4 + j, (*chip, c), sibling) for j, chip in enumerate(chips)]
    for j, chip in enumerate(chips):
        copy(1 + j, (*chip, c), me).wait_recv()
        passed[j].start()
    copy(0, sibling, me).wait_recv()
    for j, chip in enumerate(chips):
        copy(4 + j, (*chip, 1 - c), me).wait_recv()
    for cp in first + passed:
        cp.wait_send()
    mine.wait()


_TWO_LEVEL_SEMS = [pltpu.SemaphoreType.DMA((N_DEV - 1,)), pltpu.SemaphoreType.DMA((N_DEV - 1,)), pltpu.SemaphoreType.DMA(())]


def _gather_two_level(shard, *, name):
    any_spec = pl.BlockSpec(memory_space=pl.ANY)
    return pl.pallas_call(
        _two_level_gather_body(), name=name, in_specs=[any_spec], out_specs=any_spec,
        out_shape=jax.ShapeDtypeStruct((N_DEV,) + shard.shape, shard.dtype), scratch_shapes=_TWO_LEVEL_SEMS,
    )(shard)


def _two_level_gather_body():
    def body(x_ref, out_ref, send_sems, recv_sems, local_sem):
        _two_level_gather(x_ref, out_ref, send_sems, recv_sems, local_sem)
    return body


def _all_reduce_small(part, *, loss_rows, loss_scale):
    rows = part.shape[0]

    def body(p_ref, o_ref, buf, send_sems, recv_sems, local_sem):
        _two_level_gather(p_ref, buf, send_sems, recv_sems, local_sem)
        total = buf[0]
        for dev in range(1, N_DEV):
            total = total + buf[dev]
        o_ref[...] = total
        squares = total[rows - loss_rows:]
        loss = jnp.sum(jnp.sum(squares, axis=0, keepdims=True), axis=-1, keepdims=True) * loss_scale
        o_ref[rows - loss_rows:, :] = jnp.broadcast_to(loss, (loss_rows, LANES))

    vmem = pl.BlockSpec(memory_space=pltpu.VMEM)
    return pl.pallas_call(
        body, name="all_reduce_small", in_specs=[vmem], out_specs=vmem, out_shape=jax.ShapeDtypeStruct(part.shape, F32),
        scratch_shapes=[pltpu.VMEM((N_DEV, rows, LANES), F32)] + _TWO_LEVEL_SEMS,
        compiler_params=pltpu.CompilerParams(has_side_effects=True, vmem_limit_bytes=VMEM_LIMIT),
    )(part)


def _adamw_math(w, g, m, v):
    m_new = ADAM_B1 * m + (1.0 - ADAM_B1) * g
    v_new = ADAM_B2 * v + (1.0 - ADAM_B2) * (g * g)
    m_hat = m_new / (1.0 - ADAM_B1 ** ADAM_STEP)
    v_hat = v_new / (1.0 - ADAM_B2 ** ADAM_STEP)
    delta = -ADAM_LR * (m_hat / (jnp.sqrt(v_hat) + ADAM_EPS) + ADAM_WD * w)
    return delta, m_new, v_new


def _adamw(parts, w, m, v, *, name, tr=64):
    rows, cols = w.shape
    tr = min(tr, rows)
    while rows % (2 * tr) == 0 and 2 * tr * cols <= 256 * 1024:
        tr *= 2
    assert rows % tr == 0
    stacked = parts.ndim == 3

    def body(p_ref, w_ref, m_ref, v_ref, g_ref, d_ref, mo_ref, vo_ref):
        if stacked:
            g = p_ref[0].astype(F32)
            for dev in range(1, N_DEV):
                g = g + p_ref[dev].astype(F32)
        else:
            g = p_ref[...]
        delta, m_new, v_new = _adamw_math(w_ref[...], g, m_ref[...], v_ref[...])
        g_ref[...] = g
        d_ref[...] = delta
        mo_ref[...] = m_new
        vo_ref[...] = v_new

    tile = pl.BlockSpec((tr, cols), lambda i: (i, 0))
    p_spec = pl.BlockSpec((N_DEV, tr, cols), lambda i: (0, i, 0)) if stacked else tile
    return pl.pallas_call(
        body, name=name, grid=(rows // tr,), in_specs=[p_spec, tile, tile, tile], out_specs=[tile] * 4,
        out_shape=[jax.ShapeDtypeStruct((rows, cols), F32)] * 4, compiler_params=_params(1),
    )(parts, w, m, v)


_LATER = ("w_out", "w_cq", "w_ckv", "w_co", "w_ff1", "w_ff2")


def _as_rows(stacked):
    return stacked.reshape(-1, stacked.shape[-1])


def _local_step(x, mem, target, small, shards):
    batch, seq, d = x.shape
    n_mem = mem.shape[1]
    t = batch * seq
    x2, mem2, tgt2 = x.reshape(t, d), mem.reshape(batch * n_mem, d), target.reshape(t, d)
    g_mix, g_cross, g_mem, g_ffn, g_final = (small[k] for k in ("norm_mix_g", "norm_cross_g", "norm_mem_g", "norm_ffn_g", "norm_final_g"))
    gv, hg, w_sp, b_sp_t = small["gm_v_norm_g"], small["head_norm_g"], small["w_spatial"], small["b_spatial_t"]

    win_t = _as_rows(_gather_two_level(shards["w_in"], name="gather_w_in"))
    proj, xn = _norm_mm(x2, g_mix, win_t, mode="nt", name="proj_fwd", tm=512, tn=win_t.shape[0])
    merged_a = _gmlp_fwd(proj, w_sp, b_sp_t, gv, hg)
    merged, sb_raw, sb_weights, sb_log_rest, gathered = _sb_fwd(proj, merged_a, hg, batch=batch, seq=seq,
                                                                exchange=_Exchange([shards[n] for n in _LATER], scatter=False))
    wout, wcq, wckv_t, wco, wff1_t, wff2 = (_as_rows(g) for g in gathered)
    h1, qx, hn1 = _mm(merged, wout, mode="nn", out_dtype=F32, name="mix_out_fwd_xq_fwd", tm=512, row_parts=2, epi=_epi_residual_then_norm_mm("nn"),
                      epi_ins=(x2,), vec_ins=(g_cross,), whole_ins=(wcq,), more_outs=((wcq.shape[1], BF16), (d, BF16)))
    kvx, memn = _norm_mm(mem2, g_mem, wckv_t, mode="nt", name="xkv_fwd", tm=512, tn=wckv_t.shape[0])
    o = _xattn_fwd(qx, kvx, batch=batch, seq=seq, n_mem=n_mem)
    h2, fpre, hn2 = _mm(o, wco, mode="nn", out_dtype=F32, name="xo_fwd_ff1_fwd", tm=512, row_parts=2, epi=_epi_residual_then_norm_mm("nt"),
                        epi_ins=(h1,), vec_ins=(g_ffn,), whole_ins=(wff1_t,), more_outs=((wff1_t.shape[0], BF16), (d, BF16)))
    dh3, final_rows = _mm(fpre, wff2, mode="nn", out_dtype=F32, name="ff2_fwd_loss", tm=512, row_parts=2, tk=wff2.shape[0], a_fn=_relu2,
                          epi=_epi_loss, epi_ins=(h2, tgt2), vec_ins=(g_final,), aux=2)
    dg_final, sq_err = final_rows[0:1], final_rows[1:2]

    dpre = _mm(dh3, wff2, mode="nt", out_dtype=BF16, name="ff2_bwd_x", tm=512, tn=wff2.shape[0], epi=_epi_relu2_grad,
               epi_ins=(fpre,))
    chunk = wff2.shape[0] // N_DEV
    d_wff2_t = _mm(dh3, fpre, mode="tn", out_dtype=BF16, name="ff2_bwd_w", tn=2048, b_fn=_relu2, col_chunk=chunk)
    d_wff1 = _mm(hn2, dpre, mode="tn", out_dtype=BF16, name="ff1_bwd_w", tn=2048, col_chunk=chunk)
    dh2, do, dg_ffn = _mm(dpre, wff1_t, mode="nn", out_dtype=F32, name="ff1_bwd_x_xo_bwd_x", tm=512, row_parts=2, tk=wff1_t.shape[0],
                          epi=_epi_rms_bwd_then_mm, epi_ins=(h2, dh3), vec_ins=(g_ffn,), whole_ins=(wco,),
                          more_outs=((wco.shape[0], BF16),), aux=True)
    d_wco = _mm(o, dh2, mode="tn", out_dtype=BF16, name="xo_bwd_w")
    dqx, dkvx = _xattn_bwd(qx, kvx, do, batch=batch, seq=seq, n_mem=n_mem)
    d_wcq = _mm(hn1, dqx, mode="tn", out_dtype=BF16, name="xq_bwd_w")
    dh1, dmerged, dg_cross = _mm(dqx, wcq, mode="nt", out_dtype=F32, name="xq_bwd_x_mix_out_bwd_x", tm=512, row_parts=2,
                                 epi=_epi_rms_bwd_then_mm, epi_ins=(h1, dh2), vec_ins=(g_cross,), whole_ins=(wout,),
                                 more_outs=((wout.shape[0], BF16),), aux=True)
    d_wckv_t = _mm(dkvx, memn, mode="tn", out_dtype=BF16, name="xkv_bwd_w")
    _, dg_mem = _mm(dkvx, wckv_t, mode="nn", out_dtype=BF16, name="xkv_bwd_x", tm=512, epi=_epi_rms_gain_only,
                    epi_ins=(mem2,), vec_ins=(g_mem,), aux=True)
    d_wout = _mm(merged, dh1, mode="tn", out_dtype=BF16, name="mix_out_bwd_w")
    dp_a, d_wsp, d_bsp_t, d_gv, d_hg_a = _gmlp_bwd(proj, dmerged, w_sp, b_sp_t, gv, hg)
    d_later = {"w_out": d_wout, "w_cq": d_wcq, "w_ckv": d_wckv_t, "w_co": d_wco, "w_ff1": d_wff1, "w_ff2": d_wff2_t}
    scatter = _Exchange([g if g.ndim == 3 else g.reshape(N_DEV, -1, d) for g in (d_later[n] for n in _LATER)], scatter=True)
    dq, dk, dv, d_hg_b, received = _sb_bwd(proj, sb_raw, sb_weights, sb_log_rest, dmerged, hg, batch=batch, seq=seq, exchange=scatter)
    dproj = dp_a
    for part, at in ((dq, 2 * GM_WIDTH), (dk, 3 * GM_WIDTH), (dv, 4 * GM_WIDTH)):
        dproj = lax.dynamic_update_slice(dproj, part, (0, at))
    d_win_t = _mm(xn, dproj, mode="tn", out_dtype=BF16, name="proj_bwd_w", tn=dproj.shape[1] // 2).T
    dx, dg_mix, d_win_received = _mm(dproj, win_t, mode="nn", out_dtype=F32, name="proj_bwd_x", tm=512, row_parts=2, tk=win_t.shape[0],
                                     epi=_epi_rms_bwd, epi_ins=(x2, dh1), vec_ins=(g_mix,), aux=True,
                                     exchange=_Exchange([d_win_t.reshape(N_DEV, -1, d)], scatter=True))

    d_small = {"norm_mix_g": dg_mix, "gm_v_norm_g": d_gv, "w_spatial": d_wsp, "b_spatial_t": d_bsp_t, "head_norm_g": jnp.concatenate([d_hg_a, d_hg_b], axis=1),
               "norm_cross_g": dg_cross, "norm_mem_g": dg_mem, "norm_ffn_g": dg_ffn, "norm_final_g": dg_final}
    d_big = dict(zip(_LATER, received))
    d_big["w_in"] = d_win_received
    return sq_err, dx.reshape(batch, seq, d), d_small, d_big


_BIG = ("w_in", "w_out", "w_cq", "w_ckv", "w_co", "w_ff1", "w_ff2")
_GATHERED_TRANSPOSED = ("w_in", "w_ckv", "w_ff1")
_UPDATED_TRANSPOSED = ("w_in", "w_ckv", "w_ff2")
_SMALL = ("norm_mix_g", "gm_v_norm_g", "w_spatial", "b_spatial", "head_norm_g", "norm_cross_g", "norm_mem_g", "norm_ffn_g", "norm_final_g")
_NAMES = ("norm_mix_g", "w_in", "gm_v_norm_g", "w_spatial", "b_spatial", "head_norm_g", "w_out", "norm_cross_g", "norm_mem_g",
          "w_cq", "w_ckv", "w_co", "norm_ffn_g", "w_ff1", "w_ff2", "norm_final_g")


def _rows_of(a):
    r = a.reshape(-1, LANES)
    pad = (-r.shape[0]) % 8
    return jnp.pad(r, ((0, pad), (0, 0))) if pad else r


def _shard2d(a, transposed):
    return a[0].T if transposed else a[0]


def kernel(x, mem, norm_mix_g, w_in, gm_v_norm_g, w_spatial, b_spatial, head_norm_g, w_out, norm_cross_g, norm_mem_g, w_cq, w_ckv, w_co, norm_ffn_g, w_ff1, w_ff2, norm_final_g, loss_target, m_norm_mix_g, m_w_in, m_gm_v_norm_g, m_w_spatial, m_b_spatial, m_head_norm_g, m_w_out, m_norm_cross_g, m_norm_mem_g, m_w_cq, m_w_ckv, m_w_co, m_norm_ffn_g, m_w_ff1, m_w_ff2, m_norm_final_g, v_norm_mix_g, v_w_in, v_gm_v_norm_g, v_w_spatial, v_b_spatial, v_head_norm_g, v_w_out, v_norm_cross_g, v_norm_mem_g, v_w_cq, v_w_ckv, v_w_co, v_norm_ffn_g, v_w_ff1, v_w_ff2, v_norm_final_g):
    weights = dict(norm_mix_g=norm_mix_g, w_in=w_in, gm_v_norm_g=gm_v_norm_g, w_spatial=w_spatial, b_spatial=b_spatial,
                   head_norm_g=head_norm_g, w_out=w_out, norm_cross_g=norm_cross_g, norm_mem_g=norm_mem_g, w_cq=w_cq, w_ckv=w_ckv,
                   w_co=w_co, norm_ffn_g=norm_ffn_g, w_ff1=w_ff1, w_ff2=w_ff2, norm_final_g=norm_final_g)
    mom1 = dict(norm_mix_g=m_norm_mix_g, w_in=m_w_in, gm_v_norm_g=m_gm_v_norm_g, w_spatial=m_w_spatial, b_spatial=m_b_spatial,
                head_norm_g=m_head_norm_g, w_out=m_w_out, norm_cross_g=m_norm_cross_g, norm_mem_g=m_norm_mem_g, w_cq=m_w_cq,
                w_ckv=m_w_ckv, w_co=m_w_co, norm_ffn_g=m_norm_ffn_g, w_ff1=m_w_ff1, w_ff2=m_w_ff2, norm_final_g=m_norm_final_g)
    mom2 = dict(norm_mix_g=v_norm_mix_g, w_in=v_w_in, gm_v_norm_g=v_gm_v_norm_g, w_spatial=v_w_spatial, b_spatial=v_b_spatial,
                head_norm_g=v_head_norm_g, w_out=v_w_out, norm_cross_g=v_norm_cross_g, norm_mem_g=v_norm_mem_g, w_cq=v_w_cq,
                w_ckv=v_w_ckv, w_co=v_w_co, norm_ffn_g=v_norm_ffn_g, w_ff1=v_w_ff1, w_ff2=v_w_ff2, norm_final_g=v_norm_final_g)

    shards = {n: _shard2d(weights[n], n in _GATHERED_TRANSPOSED).astype(BF16) for n in _BIG}
    small = {n: weights[n].reshape(1, -1) for n in _SMALL if n not in ("w_spatial", "b_spatial")}
    small["w_spatial"] = w_spatial[0]
    small["b_spatial_t"] = b_spatial[0].T
    sq_err, grad_x, d_small, d_big = _local_step(x, mem, loss_target, small, shards)

    d_small["b_spatial"] = d_small.pop("b_spatial_t").T
    sq_rows = _rows_of(sq_err)
    packed = jnp.concatenate([_rows_of(d_small[n]) for n in _SMALL] + [sq_rows], axis=0)
    summed = _all_reduce_small(packed, loss_rows=sq_rows.shape[0], loss_scale=0.5 / x.shape[-1])

    grads, deltas, new_m, new_v = {}, {}, {}, {}
    for n in _BIG:
        flip = n in _UPDATED_TRANSPOSED
        outs = _adamw(d_big[n], _shard2d(weights[n], flip), _shard2d(mom1[n], flip), _shard2d(mom2[n], flip), name="adamw_" + n)
        outs = [o.T if flip else o for o in outs]
        grads[n], deltas[n], new_m[n], new_v[n] = (o[None] for o in outs)
    pack = lambda src: jnp.concatenate([_rows_of(src[n]) for n in _SMALL], axis=0)
    n_small_rows = sum(_rows_of(weights[n]).shape[0] for n in _SMALL)
    outs = _adamw(summed[:n_small_rows], pack(weights), pack(mom1), pack(mom2), name="adamw_small", tr=n_small_rows)
    at = 0
    for n in _SMALL:
        used = weights[n].size // LANES
        for dst, o in zip((grads, deltas, new_m, new_v), outs):
            dst[n] = o[at:at + used].reshape(weights[n].shape)
        at += _rows_of(weights[n]).shape[0]
    loss = summed[n_small_rows, 0]
    return (loss, grad_x, *[grads[n] for n in _NAMES], *[deltas[n] for n in _NAMES], *[new_m[n] for n in _NAMES],
            *[new_v[n] for n in _NAMES])
```

```python
import math

import jax
import jax.numpy as jnp
from jax import lax
from jax.experimental import pallas as pl
from jax.experimental.pallas import tpu as pltpu

F32 = jnp.float32
BF16 = jnp.bfloat16
EPS = 1e-6
N_DEV = 8
LANES = 128
CHUNK = 128
GM_GROUPS = 4
GM_WIDTH = 512
SB_PAIRS = 4
SB_HEAD_DIM = 64
SB_SCALE = 0.125
SB_TILE = 128
SB_BLOCK = 512
X_HEADS = 4
X_HEAD_DIM = 256
X_SCALE = 1.0 / 16.0
VMEM_LIMIT = 56 * 1024 * 1024
ADAM_LR, ADAM_B1, ADAM_B2, ADAM_EPS, ADAM_WD, ADAM_STEP = 0.001, 0.9, 0.999, 1e-08, 0.01, 10
MESH = pl.DeviceIdType.MESH


def _params(n_axes):
    return pltpu.CompilerParams(dimension_semantics=("arbitrary",) * n_axes, vmem_limit_bytes=VMEM_LIMIT)


def _dot(a, b, dims):
    return lax.dot_general(a, b, (dims, ((), ())), preferred_element_type=F32)


def _nn(a, b):
    return _dot(a, b, ((1,), (0,)))


def _nt(a, b):
    return _dot(a, b, ((1,), (1,)))


def _tn(a, b):
    return _dot(a, b, ((0,), (0,)))


_MODES = {"nn": _nn, "nt": _nt, "tn": _tn}


def _rstd(x):
    return lax.rsqrt(jnp.mean(x * x, axis=-1, keepdims=True) + EPS)


def _gelu(x):
    c = math.sqrt(2.0 / math.pi)
    t = jnp.tanh(c * (x + 0.044715 * x * x * x))
    return 0.5 * x * (1.0 + t)


def _gelu_and_grad(x):
    c = math.sqrt(2.0 / math.pi)
    t = jnp.tanh(c * (x + 0.044715 * x * x * x))
    half = 0.5 * (1.0 + t)
    return x * half, half + 0.5 * x * (1.0 - t * t) * c * (1.0 + 3 * 0.044715 * x * x)


def _split_bf16(x):
    hi = x.astype(BF16)
    lo = (x - hi.astype(F32)).astype(BF16)
    return hi, lo


def _mm(a, b, *, mode, out_dtype, name, tm=1024, tn=1024, tk=1024, a_fn=None, b_fn=None, epi=None, epi_ins=(), vec_ins=(),
        whole_ins=(), more_outs=(), aux=False, col_chunk=None, row_parts=1, exchange=None):
    if mode == "nn":
        (m, k), (k2, n) = a.shape, b.shape
    elif mode == "nt":
        (m, k), (n, k2) = a.shape, b.shape
    else:
        (k, m), (k2, n) = a.shape, b.shape
    assert k == k2, (a.shape, b.shape, mode)
    tm, tn, tk = min(tm, m), min(tn, n), min(tk, k)
    assert m % tm == 0 and n % tn == 0 and k % tk == 0, (m, n, k, tm, tn, tk)
    n_m, n_n, n_k = m // tm, n // tn, k // tk
    assert not (aux or more_outs) or n_n == 1
    assert row_parts == 1 or (n_k == 1 and mode != "tn" and epi is not None and tm % (8 * row_parts) == 0)
    dot = _MODES[mode]
    n_epi, n_vec, n_whole, n_more = len(epi_ins), len(vec_ins), len(whole_ins), len(more_outs)

    def body(*refs):
        ins, outs, scratch, x_refs = _riding(exchange, refs, 2 + n_epi + n_vec + n_whole, 1 + n_more + (1 if aux else 0),
                                             1 if n_k > 1 else 0)
        a_ref, b_ref, epi_refs = ins[0], ins[1], ins[2:]
        o_ref, more_refs = outs[0], outs[1:1 + n_more]
        aux_ref = outs[1 + n_more] if aux else None
        acc_ref = scratch[0] if n_k > 1 else None
        i, j, kk = pl.program_id(0), pl.program_id(1), pl.program_id(2)
        ride_done = _ride(exchange, x_refs, (i == 0) & (j == 0) & (kk == 0), (i == n_m - 1) & (j == n_n - 1) & (kk == n_k - 1))
        def product(rows=slice(None)):
            av, bv = a_ref[...] if mode == "tn" else a_ref[rows, :], b_ref[...]
            if a_fn is not None:
                av = a_fn(av)
            if b_fn is not None:
                bv = b_fn(bv)
            return dot(av.astype(BF16), bv.astype(BF16))

        def finish(accs, parts=(slice(None),)):
            if col_chunk is not None:
                for ch in range(tn // col_chunk):
                    o_ref[ch] = accs[0][:, ch * col_chunk:(ch + 1) * col_chunk].astype(out_dtype)
                return
            if epi is None:
                o_ref[...] = accs[0].astype(out_dtype)
                return
            row_sums = []
            for acc, rows in zip(accs, parts):
                res = epi(acc, *[r[rows, :] for r in epi_refs[:n_epi]], *[r[...] for r in epi_refs[n_epi:]])
                if n_more:
                    for more_ref, value in zip(more_refs, res[1:1 + n_more]):
                        more_ref[rows, :] = value.astype(more_ref.dtype)
                    res = (res[0],) + tuple(res[1 + n_more:]) if aux else res[0]
                if aux:
                    res, sums = res[0], res[1:]
                    row_sums.append(sums[0] if len(sums) == 1 else jnp.concatenate(sums, axis=0))
                o_ref[rows, :] = res.astype(out_dtype)
            if aux:
                total = sum(row_sums)

                @pl.when(i == 0)
                def _():
                    aux_ref[...] = total

                @pl.when(i != 0)
                def _():
                    aux_ref[...] += total

        if n_k == 1:
            parts = [slice(h * (tm // row_parts), (h + 1) * (tm // row_parts)) for h in range(row_parts)]
            finish([product(rows) for rows in parts], parts)
        else:
            @pl.when(kk == 0)
            def _():
                acc_ref[...] = product()

            @pl.when(kk != 0)
            def _():
                acc_ref[...] += product()

            @pl.when(kk == n_k - 1)
            def _():
                finish([acc_ref[...]])

        ride_done()

    if mode == "tn":
        a_spec = pl.BlockSpec((tk, tm), lambda i, j, kk: (kk, i))
    else:
        a_spec = pl.BlockSpec((tm, tk), lambda i, j, kk: (i, kk))
    if mode == "nt":
        b_spec = pl.BlockSpec((tn, tk), lambda i, j, kk: (j, kk))
    else:
        b_spec = pl.BlockSpec((tk, tn), lambda i, j, kk: (kk, j))
    tile_spec = pl.BlockSpec((tm, tn), lambda i, j, kk: (i, j))
    row_spec = pl.BlockSpec((1, tn), lambda i, j, kk: (0, j))
    out_shape = [jax.ShapeDtypeStruct((m, n), out_dtype)]
    out_specs = [tile_spec]
    if col_chunk is not None:
        assert epi is None and not aux and tn % col_chunk == 0
        out_shape = [jax.ShapeDtypeStruct((n // col_chunk, m, col_chunk), out_dtype)]
        out_specs = [pl.BlockSpec((tn // col_chunk, tm, col_chunk), lambda i, j, kk: (j, i, 0))]
    for columns, dtype in more_outs:
        out_shape.append(jax.ShapeDtypeStruct((m, columns), dtype))
        out_specs.append(pl.BlockSpec((tm, columns), lambda i, j, kk: (i, 0)))
    if aux:
        out_shape.append(jax.ShapeDtypeStruct((int(aux), n), F32))
        out_specs.append(pl.BlockSpec((int(aux), tn), lambda i, j, kk: (0, j)))
    whole_specs = [pl.BlockSpec(w.shape, lambda i, j, kk, nd=w.ndim: (0,) * nd) for w in whole_ins]
    x_in, x_out, x_shape, x_scratch, x_arrays = _riding_specs(exchange)
    res = pl.pallas_call(
        body, name=name, grid=(n_m, n_n, n_k),
        in_specs=[a_spec, b_spec] + [tile_spec] * n_epi + [row_spec] * n_vec + whole_specs + x_in,
        out_specs=out_specs + x_out, out_shape=out_shape + x_shape,
        scratch_shapes=([pltpu.VMEM((tm, tn), F32)] if n_k > 1 else []) + x_scratch,
        compiler_params=_params(3),
    )(a, b, *epi_ins, *vec_ins, *whole_ins, *x_arrays)
    if exchange is not None or more_outs:
        return tuple(res)
    return res if aux else res[0]


def _norm_mm(x, g, w, *, mode, name, tm=1024, tn=1024):
    m, d = x.shape
    n = w.shape[0] if mode == "nt" else w.shape[1]
    tm, tn = min(tm, m), min(tn, n)
    assert m % tm == 0 and n % tn == 0
    dot = _MODES[mode]

    def body(x_ref, g_ref, w_ref, o_ref, xn_ref, xn_s):
        @pl.when(pl.program_id(1) == 0)
        def _():
            xv = x_ref[...]
            xn = (xv * _rstd(xv) * g_ref[...]).astype(BF16)
            xn_s[...] = xn
            xn_ref[...] = xn

        o_ref[...] = dot(xn_s[...], w_ref[...]).astype(BF16)

    w_spec = pl.BlockSpec((tn, d), lambda i, j: (j, 0)) if mode == "nt" else pl.BlockSpec((d, tn), lambda i, j: (0, j))
    return pl.pallas_call(
        body, name=name, grid=(m // tm, n // tn),
        in_specs=[pl.BlockSpec((tm, d), lambda i, j: (i, 0)), pl.BlockSpec((1, d), lambda i, j: (0, 0)), w_spec],
        out_specs=[pl.BlockSpec((tm, tn), lambda i, j: (i, j)), pl.BlockSpec((tm, d), lambda i, j: (i, 0))],
        out_shape=[jax.ShapeDtypeStruct((m, n), BF16), jax.ShapeDtypeStruct((m, d), BF16)],
        scratch_shapes=[pltpu.VMEM((tm, d), BF16)],
        compiler_params=_params(2),
    )(x, g, w)


def _epi_residual(acc, res):
    return res + acc


def _epi_relu2_grad(acc, pre):
    return acc * (2.0 * jnp.maximum(pre.astype(F32), 0.0))


def _relu2(pre):
    r = jnp.maximum(pre.astype(F32), 0.0)
    return r * r


def _epi_rms_bwd(acc, h, dres, g):
    r = _rstd(h)
    xh = h * r
    dxh = acc * g
    dh = dres + r * (dxh - xh * jnp.mean(dxh * xh, axis=-1, keepdims=True))
    return dh, jnp.sum(acc * xh, axis=0, keepdims=True)


def _epi_residual_then_norm_mm(mode):
    dot = _MODES[mode]

    def epi(acc, res, g, w):
        h = res + acc
        hn = (h * _rstd(h) * g).astype(BF16)
        return h, dot(hn, w), hn

    return epi


def _epi_rms_bwd_then_mm(acc, h, dres, g, w):
    dh, row = _epi_rms_bwd(acc, h, dres, g)
    return dh, _nt(dh.astype(BF16), w), row


def _epi_loss(acc, h_in, target, g):
    h = h_in + acc
    r = _rstd(h)
    xh = h * r
    err = xh * g - target
    dy = err * (1.0 / h.shape[-1])
    dxh = dy * g
    dh = r * (dxh - xh * jnp.mean(dxh * xh, axis=-1, keepdims=True))
    return dh, jnp.sum(dy * xh, axis=0, keepdims=True), jnp.sum(err * err, axis=0, keepdims=True)


def _epi_rms_gain_only(acc, h, g):
    return acc, jnp.sum(acc * (h * _rstd(h)), axis=0, keepdims=True)


def _tril(n):
    row = lax.broadcasted_iota(jnp.int32, (n, n), 0)
    col = lax.broadcasted_iota(jnp.int32, (n, n), 1)
    return col <= row


def _gmlp_fwd(proj, w_sp, b_sp_t, gv, hg, *, rows=512):
    t = proj.shape[0]
    rows = min(rows, t)
    n_c = rows // CHUNK

    def body(u_ref, v_ref, w_ref, bt_ref, gv_ref, hg_ref, m_ref):
        keep = _tril(CHUNK)
        for g in range(GM_GROUPS):
            cols = slice(g * LANES, (g + 1) * LANES)
            wg = jnp.where(keep, w_ref[g], 0.0).astype(BF16)
            u = _gelu(u_ref[:, cols].astype(F32))
            v = _gelu(v_ref[:, cols].astype(F32))
            vn = (v * _rstd(v) * gv_ref[:, cols]).astype(BF16)
            bias = bt_ref[:, g:g + 1]
            chunks = [slice(c * CHUNK, (c + 1) * CHUNK) for c in range(n_c)]
            gated = [u[rs] * (_nn(wg, vn[rs]) + bias) for rs in chunks]
            for rs, a in zip(chunks, gated):
                m_ref[rs, cols] = (a * _rstd(a) * hg_ref[:, cols]).astype(BF16)

    full = lambda shape: pl.BlockSpec(shape, lambda i: (0,) * len(shape))
    return pl.pallas_call(
        body, name="gmlp_fwd", grid=(t // rows,),
        in_specs=[pl.BlockSpec((rows, GM_WIDTH), lambda i: (i, 0)), pl.BlockSpec((rows, GM_WIDTH), lambda i: (i, 1)),
                  full((GM_GROUPS, CHUNK, CHUNK)), full((CHUNK, GM_GROUPS)), full((1, GM_WIDTH)), full((1, GM_WIDTH))],
        out_specs=pl.BlockSpec((rows, GM_WIDTH), lambda i: (i, 0)),
        out_shape=jax.ShapeDtypeStruct((t, 2 * GM_WIDTH), BF16),
        compiler_params=_params(1),
    )(proj, proj, w_sp, b_sp_t, gv, hg)


def _gmlp_bwd(proj, dmerged, w_sp, b_sp_t, gv, hg, *, rows=512):
    t = proj.shape[0]
    rows = min(rows, t)
    n_c = rows // CHUNK
    n_steps = t // rows

    def body(u_ref, v_ref, dm_ref, w_ref, bt_ref, gv_ref, hg_ref, dp_ref, dw_ref, dbt_ref, dgv_ref, dhg_ref, db_acc):
        step = pl.program_id(0)
        keep = _tril(CHUNK)

        @pl.when(step == 0)
        def _():
            dw_ref[...] = jnp.zeros_like(dw_ref)
            db_acc[...] = jnp.zeros_like(db_acc)
            dgv_ref[...] = jnp.zeros_like(dgv_ref)
            dhg_ref[...] = jnp.zeros_like(dhg_ref)

        for g in range(GM_GROUPS):
            cols = slice(g * LANES, (g + 1) * LANES)
            wg = jnp.where(keep, w_ref[g], 0.0).astype(BF16)
            u, u_slope = _gelu_and_grad(u_ref[:, cols].astype(F32))
            v, v_slope = _gelu_and_grad(v_ref[:, cols].astype(F32))
            r = _rstd(v)
            xh = v * r
            gvg = gv_ref[:, cols]
            hgg = hg_ref[:, cols]
            vn = (xh * gvg).astype(BF16)
            bias = bt_ref[:, g:g + 1]
            dm = dm_ref[:, cols].astype(F32)
            chunks = [slice(c * CHUNK, (c + 1) * CHUNK) for c in range(n_c)]
            mixed = [_nn(wg, vn[rs]) + bias for rs in chunks]
            gated = [u[rs] * mx for rs, mx in zip(chunks, mixed)]
            scale = [_rstd(a) for a in gated]
            normed = [a * ra for a, ra in zip(gated, scale)]
            d_normed = [dm[rs] * hgg for rs in chunks]
            d_gated = [ra * (dan - an * jnp.mean(dan * an, axis=-1, keepdims=True)) for ra, dan, an in zip(scale, d_normed, normed)]
            d_mixed = [da * u[rs] for da, rs in zip(d_gated, chunks)]
            d_mixed_b = [dmx.astype(BF16) for dmx in d_mixed]
            du = jnp.concatenate([da * mx for da, mx in zip(d_gated, mixed)], axis=0)
            dvn = jnp.concatenate([_tn(wg, dmb) for dmb in d_mixed_b], axis=0)
            dw_ref[g] += sum(_nt(dmb, vn[rs]) for dmb, rs in zip(d_mixed_b, chunks))
            db_acc[g] += sum(d_mixed)
            dhg_ref[:, cols] += sum(jnp.sum(dm[rs] * an, axis=0, keepdims=True) for rs, an in zip(chunks, normed))
            dgv_ref[:, cols] += jnp.sum(dvn * xh, axis=0, keepdims=True)
            dxh = dvn * gvg
            dv = r * (dxh - xh * jnp.mean(dxh * xh, axis=-1, keepdims=True))
            dp_ref[:, cols] = (du * u_slope).astype(BF16)
            dp_ref[:, GM_WIDTH + g * LANES:GM_WIDTH + (g + 1) * LANES] = (dv * v_slope).astype(BF16)

        @pl.when(step == n_steps - 1)
        def _():
            for g in range(GM_GROUPS):
                dw_ref[g] = jnp.where(keep, dw_ref[g], 0.0)
                dbt_ref[:, g:g + 1] = jnp.sum(db_acc[g], axis=-1, keepdims=True)

    full = lambda shape: pl.BlockSpec(shape, lambda i: (0,) * len(shape))
    return pl.pallas_call(
        body, name="gmlp_bwd", grid=(n_steps,),
        in_specs=[pl.BlockSpec((rows, GM_WIDTH), lambda i: (i, 0)), pl.BlockSpec((rows, GM_WIDTH), lambda i: (i, 1)),
                  pl.BlockSpec((rows, GM_WIDTH), lambda i: (i, 0)),
                  full((GM_GROUPS, CHUNK, CHUNK)), full((CHUNK, GM_GROUPS)), full((1, GM_WIDTH)), full((1, GM_WIDTH))],
        out_specs=[pl.BlockSpec((rows, 2 * GM_WIDTH), lambda i: (i, 0)), full((GM_GROUPS, CHUNK, CHUNK)),
                   full((CHUNK, GM_GROUPS)), full((1, GM_WIDTH)), full((1, GM_WIDTH))],
        out_shape=[jax.ShapeDtypeStruct((t, proj.shape[1]), BF16), jax.ShapeDtypeStruct((GM_GROUPS, CHUNK, CHUNK), F32),
                   jax.ShapeDtypeStruct((CHUNK, GM_GROUPS), F32), jax.ShapeDtypeStruct((1, GM_WIDTH), F32),
                   jax.ShapeDtypeStruct((1, GM_WIDTH), F32)],
        scratch_shapes=[pltpu.VMEM((GM_GROUPS, CHUNK, LANES), F32)],
        compiler_params=_params(1),
    )(proj, proj, dmerged, w_sp, b_sp_t, gv, hg)


def _sb_logits(z, strict):
    ls = jnp.minimum(z, 0.0) - jnp.log(1.0 + jnp.exp(-jnp.abs(z)))
    l1m = ls - z
    if strict is not None:
        l1m = jnp.where(strict, l1m, 0.0)
    return ls, l1m


def _tri_sums(x, tri):
    hi, lo = _split_bf16(x)
    return _nn(jnp.concatenate([hi, lo], axis=1), jnp.concatenate([tri, tri], axis=0))


def _sb_weights(ls, in_tile, right, strict):
    a = jnp.exp(ls + in_tile + right)
    if strict is not None:
        a = jnp.where(strict, a, 0.0)
    return a


def _sb_masks(q_rows):
    row = lax.broadcasted_iota(jnp.int32, (SB_TILE, SB_TILE), 0)
    col = lax.broadcasted_iota(jnp.int32, (SB_TILE, SB_TILE), 1)
    lane = lax.broadcasted_iota(jnp.int32, (q_rows, LANES), 1)
    return row, col, lane < SB_HEAD_DIM


def _stack_heads(x, first):
    zero = jnp.zeros_like(x)
    return jnp.concatenate([jnp.where(first, x, zero), jnp.where(first, zero, x)], axis=0)


def _stack_heads_t(x_t):
    first_t = lax.broadcasted_iota(jnp.int32, x_t.shape, 0) < SB_HEAD_DIM
    zero = jnp.zeros_like(x_t)
    return jnp.concatenate([jnp.where(first_t, x_t, zero), jnp.where(first_t, zero, x_t)], axis=1).astype(BF16)


def _unstack_heads(x2, first):
    half = x2.shape[0] // 2
    return jnp.where(first, x2[:half], x2[half:])


def _live_rows(x, s, q_rows):
    if s == 0:
        return x
    return jnp.concatenate([x[s * SB_TILE:q_rows], x[q_rows + s * SB_TILE:]], axis=0)


def _spread_rows(x, s, q_rows):
    if s == 0:
        return x
    half = q_rows - s * SB_TILE
    zero = jnp.zeros((s * SB_TILE,) + x.shape[1:], x.dtype)
    return jnp.concatenate([zero, x[:half], zero, x[half:]], axis=0)


def _stacked_col_minus_row(q_rows):
    row = lax.broadcasted_iota(jnp.int32, (2 * q_rows, SB_TILE), 0)
    col = lax.broadcasted_iota(jnp.int32, (2 * q_rows, SB_TILE), 1)
    return col - (row & (q_rows - 1))


def _head_mean(x, first):
    s0 = jnp.sum(jnp.where(first, x, 0.0), axis=-1, keepdims=True)
    s1 = jnp.sum(jnp.where(first, 0.0, x), axis=-1, keepdims=True)
    return jnp.where(first, s0, s1) * (1.0 / SB_HEAD_DIM)


def _riding(exchange, refs, n_in, n_out, n_scratch):
    n_x = exchange.n if exchange is not None else 0
    ins, rest = refs[:n_in], refs[n_in:]
    x_src, rest = rest[:n_x], rest[n_x:]
    outs, rest = rest[:n_out], rest[n_out:]
    x_dst, rest = rest[:n_x], rest[n_x:]
    return ins, outs, rest[:n_scratch], (x_src, x_dst, rest[n_scratch:])


def _riding_specs(exchange):
    if exchange is None:
        return [], [], [], [], []
    return exchange.in_specs, exchange.out_specs, exchange.out_shape, exchange.scratch, exchange.arrays


def _ride(exchange, x_refs, first_step, last_step):
    if exchange is None:
        return lambda: None

    @pl.when(first_step)
    def _():
        exchange.start(*x_refs)

    def finish():
        @pl.when(last_step)
        def _():
            exchange.wait(*x_refs)

    return finish


def _sb_fwd(proj, merged_a, hg, *, batch, seq, exchange=None):
    q0, k0, v0 = 2 * GM_WIDTH // LANES, 2 * GM_WIDTH // LANES + SB_PAIRS, 2 * GM_WIDTH // LANES + 2 * SB_PAIRS
    q_rows = block_keys = min(SB_BLOCK, seq)
    assert seq % block_keys == 0
    n_q, n_sub, n_blocks = seq // q_rows, block_keys // SB_TILE, seq // block_keys

    def body(*refs):
        (q_ref, k_ref, v_ref, hg_ref, _), (m_ref, raw_ref, a_ref, l_ref), _, x_refs = _riding(exchange, refs, 5, 4, 0)
        b, p, i = pl.program_id(0), pl.program_id(1), pl.program_id(2)
        finish = _ride(exchange, x_refs, (b == 0) & (p == 0) & (i == 0), (b == batch - 1) & (p == SB_PAIRS - 1) & (i == n_q - 1))
        row, col, first = _sb_masks(q_rows)
        upper = (row > col).astype(BF16)
        q2 = _stack_heads((q_ref[...].astype(F32) * SB_SCALE).astype(BF16), first)
        diff = _stacked_col_minus_row(q_rows)
        last = i

        def key_tile(jb, s):
            return k_ref[pl.ds(pl.multiple_of((jb * n_sub + s) * SB_TILE, SB_TILE), SB_TILE), :]

        def scores(jb):
            return tuple(_nt(q2, key_tile(jb, s)) for s in range(n_sub))

        def keep_for_backward(ref, jb, s, stacked):
            ref[0, 0, jb, 0, :, s * SB_TILE:(s + 1) * SB_TILE] = stacked

        def weights_of(jb, z, right):
            logits = [_sb_logits(z[s], None) for s in reversed(range(n_sub))][::-1]
            totals = [jnp.sum(l1m, axis=-1, keepdims=True) for _, l1m in logits]
            sums = [_tri_sums(l1m, upper) for _, l1m in reversed(logits)][::-1]
            for s in reversed(range(n_sub)):
                keep_for_backward(a_ref, jb, s, _sb_weights(logits[s][0], sums[s], right, None).astype(BF16))
                keep_for_backward(l_ref, jb, s, logits[s][1].astype(BF16))
                right = right + totals[s]
            return right

        def diagonal_weights():
            keeps = [_live_rows(diff, s, q_rows) < -s * SB_TILE for s in range(n_sub)]
            logits = [_sb_logits(_nt(_live_rows(q2, s, q_rows), key_tile(last, s)), keeps[s]) for s in range(n_sub)]
            totals = [jnp.sum(l1m, axis=-1, keepdims=True) for _, l1m in logits]
            sums = [_tri_sums(l1m, upper) for _, l1m in logits]
            right = jnp.zeros((2 * q_rows, 1), F32)
            for s in reversed(range(n_sub)):
                live = _sb_weights(logits[s][0], sums[s], _live_rows(right, s, q_rows), keeps[s]).astype(BF16)
                keep_for_backward(a_ref, last, s, _spread_rows(live, s, q_rows))
                keep_for_backward(l_ref, last, s, _spread_rows(logits[s][1].astype(BF16), s, q_rows))
                right = right + _spread_rows(totals[s], s, q_rows)
            return right

        def weighted_values(jb):
            return _nn(a_ref[0, 0, jb, 0], v_ref[pl.ds(pl.multiple_of(jb * block_keys, block_keys), block_keys), :])

        right = diagonal_weights()
        z_next = scores(jnp.maximum(last - 1, 0))

        def step(k, carry):
            right, acc, z = carry
            jb = last - k
            acc = acc + weighted_values(jb + 1)
            z_next = scores(jnp.maximum(jb - 1, 0))
            return weights_of(jb, z, right), acc, z_next

        _, acc2, _ = lax.fori_loop(1, last + 1, step, (right, jnp.zeros((2 * q_rows, LANES), F32), z_next))
        acc = _unstack_heads(acc2 + weighted_values(0), first)
        raw_ref[...] = acc
        m_ref[...] = (acc * lax.rsqrt(_head_mean(acc * acc, first) + EPS) * hg_ref[...]).astype(BF16)
        finish()

    t = batch * seq
    blk = lambda c0: pl.BlockSpec((q_rows, LANES), lambda b, p, i: (b * n_q + i, c0 + p))
    kv = lambda c0: pl.BlockSpec((seq, LANES), lambda b, p, i: (b, c0 + p))
    kept = pl.BlockSpec((1, 1, n_blocks, 1, 2 * q_rows, block_keys), lambda b, p, i: (p, b, 0, i, 0, 0))
    kept_shape = jax.ShapeDtypeStruct((SB_PAIRS, batch, n_blocks, n_q, 2 * q_rows, block_keys), BF16)
    x_in, x_out, x_shape, x_scratch, x_arrays = _riding_specs(exchange)
    res = pl.pallas_call(
        body, name="sb_fwd", grid=(batch, SB_PAIRS, n_q),
        in_specs=[blk(q0), kv(k0), kv(v0), pl.BlockSpec((1, LANES), lambda b, p, i: (0, SB_PAIRS + p)),
                  pl.BlockSpec(memory_space=pl.ANY)] + x_in,
        out_specs=[blk(SB_PAIRS), blk(0), kept, kept] + x_out,
        out_shape=[jax.ShapeDtypeStruct((t, 2 * GM_WIDTH), BF16), jax.ShapeDtypeStruct((t, GM_WIDTH), F32), kept_shape, kept_shape] + x_shape,
        scratch_shapes=x_scratch,
        input_output_aliases={4: 0},
        compiler_params=_params(3),
    )(proj, proj, proj, hg, merged_a, *x_arrays)
    return res[0], res[1], res[2], res[3], res[4:]


def _sb_bwd(proj, raw, weights, log_rest, dmerged, hg, *, batch, seq, exchange=None):
    q0, k0, v0 = 2 * GM_WIDTH // LANES, 2 * GM_WIDTH // LANES + SB_PAIRS, 2 * GM_WIDTH // LANES + 2 * SB_PAIRS
    q_rows = block_keys = min(SB_BLOCK, seq)
    assert seq % block_keys == 0
    n_q, n_sub, n_blocks = seq // q_rows, block_keys // SB_TILE, seq // block_keys

    def body(*refs):
        ins, outs, (dk_acc, dv_acc), x_refs = _riding(exchange, refs, 8, 4, 2)
        q_ref, k_ref, v_ref, raw_ref, a_ref, l_ref, dm_ref, hg_ref = ins
        dq_ref, dk_ref, dv_ref, dhg_ref = outs
        p, b, i = pl.program_id(0), pl.program_id(1), pl.program_id(2)
        finish = _ride(exchange, x_refs, (b == 0) & (p == 0) & (i == 0), (b == batch - 1) & (p == SB_PAIRS - 1) & (i == n_q - 1))
        row, col, first = _sb_masks(q_rows)
        lower = (row < col).astype(BF16)

        @pl.when(jnp.logical_and(b == 0, i == 0))
        def _():
            dhg_ref[...] = jnp.zeros_like(dhg_ref)

        @pl.when(i == 0)
        def _():
            dk_acc[...] = jnp.zeros_like(dk_acc)
            dv_acc[...] = jnp.zeros_like(dv_acc)

        raw_v = raw_ref[...]
        dm = dm_ref[...].astype(F32)
        r = lax.rsqrt(_head_mean(raw_v * raw_v, first) + EPS)
        nrm = raw_v * r
        dhg_ref[...] += jnp.sum(dm * nrm, axis=0, keepdims=True)
        dn = dm * hg_ref[...]
        dout = r * (dn - nrm * _head_mean(dn * nrm, first))
        dout2 = _stack_heads(dout.astype(BF16), first)
        q2_t = _stack_heads_t(q_ref[...].astype(F32).T)
        dout2_t = _stack_heads_t(dout.T)
        diff = _stacked_col_minus_row(q_rows)
        last = i

        def kept(ref, jb, cols):
            return ref[0, 0, jb, 0, :, cols]

        def block(jb, carry, diagonal):
            gleft, dq = carry
            live = (lambda x, s: _live_rows(x, s, q_rows)) if diagonal else (lambda x, s: x)
            spread = (lambda x, s: _spread_rows(x, s, q_rows)) if diagonal else (lambda x, s: x)
            tiles = [pl.ds(pl.multiple_of((jb * n_sub + s) * SB_TILE, SB_TILE), SB_TILE) for s in range(n_sub)]
            cols = [slice(s * SB_TILE, (s + 1) * SB_TILE) for s in range(n_sub)]
            gmats = [_nt(live(dout2, s), v_ref[tiles[s], :]) * live(kept(a_ref, jb, cols[s]), s).astype(F32) for s in range(n_sub)]
            prefixes = [_tri_sums(g, lower) for g in gmats]
            dzs = []
            for s in range(n_sub):
                one_minus = jnp.exp(live(kept(l_ref, jb, cols[s]), s).astype(F32))
                dz = (gmats[s] * one_minus - (live(gleft, s) + prefixes[s]) * (1.0 - one_minus)) * SB_SCALE
                gleft = gleft + spread(jnp.sum(gmats[s], axis=-1, keepdims=True), s)
                if diagonal:
                    dz = jnp.where(live(diff, s) < -s * SB_TILE, dz, 0.0)
                dzs.append(spread(dz.astype(BF16), s))
            dz_all = jnp.concatenate(dzs, axis=1)
            dk_acc[jb] += _nn(q2_t, dz_all)
            dv_acc[jb] += _nn(dout2_t, kept(a_ref, jb, slice(None)))
            return gleft, dq + _nn(dz_all, k_ref[pl.ds(pl.multiple_of(jb * block_keys, block_keys), block_keys), :])

        carry = (jnp.zeros((2 * q_rows, 1), F32), jnp.zeros((2 * q_rows, LANES), F32))
        carry = lax.fori_loop(0, last, lambda jb, c: block(jb, c, False), carry)
        dq_ref[...] = _unstack_heads(block(last, carry, True)[1], first).astype(BF16)

        @pl.when(i == n_q - 1)
        def _():
            for jb in range(n_blocks):
                for s in range(n_sub):
                    rows = slice((jb * n_sub + s) * SB_TILE, (jb * n_sub + s + 1) * SB_TILE)
                    cols = slice(s * SB_TILE, (s + 1) * SB_TILE)
                    dk_ref[rows, :] = dk_acc[jb, :, cols].T.astype(BF16)
                    dv_ref[rows, :] = dv_acc[jb, :, cols].T.astype(BF16)

        finish()

    t = batch * seq
    blk = lambda c0: pl.BlockSpec((q_rows, LANES), lambda p, b, i: (b * n_q + i, c0 + p))
    kv = lambda c0: pl.BlockSpec((seq, LANES), lambda p, b, i: (b, c0 + p))
    row_spec = pl.BlockSpec((1, LANES), lambda p, b, i: (0, SB_PAIRS + p))
    kept_spec = pl.BlockSpec((1, 1, n_blocks, 1, 2 * q_rows, block_keys), lambda p, b, i: (p, b, 0, i, 0, 0))
    x_in, x_out, x_shape, x_scratch, x_arrays = _riding_specs(exchange)
    res = pl.pallas_call(
        body, name="sb_bwd", grid=(SB_PAIRS, batch, n_q),
        in_specs=[blk(q0), kv(k0), kv(v0), blk(0), kept_spec, kept_spec, blk(SB_PAIRS), row_spec] + x_in,
        out_specs=[blk(0), kv(0), kv(0), pl.BlockSpec((1, LANES), lambda p, b, i: (0, p))] + x_out,
        out_shape=[jax.ShapeDtypeStruct((t, GM_WIDTH), BF16)] * 3 + [jax.ShapeDtypeStruct((1, GM_WIDTH), F32)] + x_shape,
        scratch_shapes=[pltpu.VMEM((n_blocks, LANES, block_keys), F32), pltpu.VMEM((n_blocks, LANES, block_keys), F32)] + x_scratch,
        compiler_params=_params(3),
    )(proj, proj, proj, raw, weights, log_rest, dmerged, hg, *x_arrays)
    return res[0], res[1], res[2], res[3], res[4:]


def _x_softmax(s):
    s = s * X_SCALE
    p = jnp.exp(s - jnp.max(s, axis=-1, keepdims=True))
    return p * (1.0 / jnp.sum(p, axis=-1, keepdims=True))


def _x_heads(width):
    return [slice(h * X_HEAD_DIM, (h + 1) * X_HEAD_DIM) for h in range(X_HEADS)], \
           [slice(width + h * X_HEAD_DIM, width + (h + 1) * X_HEAD_DIM) for h in range(X_HEADS)]


def _xattn_fwd(q, kv, *, batch, seq, n_mem, tq=512):
    tq = min(tq, seq)
    n_q = seq // tq
    d = X_HEADS * X_HEAD_DIM

    def body(q_ref, kv_ref, o_ref):
        kcols, vcols = _x_heads(d)
        scores = [_nt(q_ref[:, c], kv_ref[:, c]) for c in kcols]
        probs = [_x_softmax(s).astype(BF16) for s in scores]
        for p, c, vc in zip(probs, kcols, vcols):
            o_ref[:, c] = _nn(p, kv_ref[:, vc]).astype(BF16)

    return pl.pallas_call(
        body, name="xattn_fwd", grid=(batch, n_q),
        in_specs=[pl.BlockSpec((tq, d), lambda b, i: (b * n_q + i, 0)), pl.BlockSpec((n_mem, 2 * d), lambda b, i: (b, 0))],
        out_specs=pl.BlockSpec((tq, d), lambda b, i: (b * n_q + i, 0)),
        out_shape=jax.ShapeDtypeStruct((batch * seq, d), BF16),
        compiler_params=_params(2),
    )(q, kv)


def _xattn_bwd(q, kv, do, *, batch, seq, n_mem, tq=512):
    tq = min(tq, seq)
    n_q = seq // tq
    d = X_HEADS * X_HEAD_DIM

    def body(q_ref, kv_ref, do_ref, dq_ref, dkv_ref, acc):
        i = pl.program_id(1)

        @pl.when(i == 0)
        def _():
            acc[...] = jnp.zeros_like(acc)

        kcols, vcols = _x_heads(d)
        scores = [_nt(q_ref[:, c], kv_ref[:, c]) for c in kcols]
        d_probs = [_nt(do_ref[:, c], kv_ref[:, vc]) for c, vc in zip(kcols, vcols)]
        probs = [_x_softmax(s) for s in scores]
        d_scores = [(p * (dp - jnp.sum(dp * p, axis=-1, keepdims=True)) * X_SCALE).astype(BF16) for p, dp in zip(probs, d_probs)]
        for p, ds, c, vc in zip(probs, d_scores, kcols, vcols):
            acc[:, vc] += _tn(p.astype(BF16), do_ref[:, c])
            dq_ref[:, c] = _nn(ds, kv_ref[:, c]).astype(BF16)
            acc[:, c] += _tn(ds, q_ref[:, c])

        @pl.when(i == n_q - 1)
        def _():
            dkv_ref[...] = acc[...].astype(BF16)

    return pl.pallas_call(
        body, name="xattn_bwd", grid=(batch, n_q),
        in_specs=[pl.BlockSpec((tq, d), lambda b, i: (b * n_q + i, 0)), pl.BlockSpec((n_mem, 2 * d), lambda b, i: (b, 0)),
                  pl.BlockSpec((tq, d), lambda b, i: (b * n_q + i, 0))],
        out_specs=[pl.BlockSpec((tq, d), lambda b, i: (b * n_q + i, 0)), pl.BlockSpec((n_mem, 2 * d), lambda b, i: (b, 0))],
        out_shape=[jax.ShapeDtypeStruct((batch * seq, d), BF16), jax.ShapeDtypeStruct((batch * n_mem, 2 * d), BF16)],
        scratch_shapes=[pltpu.VMEM((n_mem, 2 * d), F32)],
        compiler_params=_params(2),
    )(q, kv, do)


def _my_index():
    return 4 * lax.axis_index("x") + 2 * lax.axis_index("y") + lax.axis_index("c")


def _peers():
    x, y, c = lax.axis_index("x"), lax.axis_index("y"), lax.axis_index("c")
    out = []
    for rel in range(1, N_DEV):
        dx, dy, dc = (rel >> 2) & 1, (rel >> 1) & 1, rel & 1
        px, py, pc = x ^ dx, y ^ dy, c ^ dc
        out.append(((px, py, pc), 4 * px + 2 * py + pc))
    return out


class _Exchange:
    def __init__(self, arrays, scatter):
        self.arrays, self.scatter, self.n = list(arrays), scatter, len(arrays)
        any_spec = pl.BlockSpec(memory_space=pl.ANY)
        self.in_specs = [any_spec] * self.n
        self.out_specs = [any_spec] * self.n
        self.out_shape = [jax.ShapeDtypeStruct((N_DEV,) + tuple(a.shape[-2:]), a.dtype) for a in self.arrays]
        n_peer = N_DEV - 1
        self.scratch = [pltpu.SemaphoreType.DMA((self.n, n_peer)), pltpu.SemaphoreType.DMA((self.n, n_peer)),
                        pltpu.SemaphoreType.DMA((self.n,))]

    def _copies(self, srcs, dsts, sems, arriving):
        send_sems, recv_sems, local_sems = sems
        me = _my_index()
        local, remote = [], []
        for w in range(self.n):
            if not arriving:
                local.append(pltpu.make_async_copy(srcs[w].at[me] if self.scatter else srcs[w], dsts[w].at[me], local_sems.at[w]))
            for rel, (pos, idx) in enumerate(_peers()):
                remote.append(pltpu.make_async_remote_copy(
                    src_ref=srcs[w].at[idx] if self.scatter else srcs[w], dst_ref=dsts[w].at[idx if arriving else me],
                    send_sem=send_sems.at[w, rel], recv_sem=recv_sems.at[w, rel], device_id=pos, device_id_type=MESH))
        return local, remote

    def start(self, srcs, dsts, sems):
        local, sends = self._copies(srcs, dsts, sems, arriving=False)
        for cp in local + sends:
            cp.start()

    def wait(self, srcs, dsts, sems):
        for cp in self._copies(srcs, dsts, sems, arriving=True)[1]:
            cp.wait_recv()
        local, sends = self._copies(srcs, dsts, sems, arriving=False)
        for cp in sends:
            cp.wait_send()
        for cp in local:
            cp.wait()


def _two_level_gather(x_ref, slots_ref, send_sems, recv_sems, local_sem):
    x, y, c = lax.axis_index("x"), lax.axis_index("y"), lax.axis_index("c")
    me, sibling = (x, y, c), (x, y, 1 - c)
    chips = [(1 - x, y), (x, 1 - y), (1 - x, 1 - y)]

    def slot(px, py, pc):
        return slots_ref.at[4 * px + 2 * py + pc]

    def copy(k, block, to, src=None):
        return pltpu.make_async_remote_copy(src_ref=slot(*block) if src is None else src, dst_ref=slot(*block),
                                            send_sem=send_sems.at[k], recv_sem=recv_sems.at[k], device_id=to, device_id_type=MESH)

    mine = pltpu.make_async_copy(x_ref, slot(*me), local_sem)
    mine.start()
    first = [copy(0, me, sibling, src=x_ref)] + [copy(1 + j, me, (*chip, c), src=x_ref) for j, chip in enumerate(chips)]
    for cp in first:
        cp.start()
    passed = [copy(4 + j, (*chip, c), sibling) for j, chip in enumerate(chips)]
    for j, chip in enumerate(chips):
        copy(1 + j, (*chip, c), me).wait_recv()
        passed[j].start()
    copy(0, sibling, me).wait_recv()
    for j, chip in enumerate(chips):
        copy(4 + j, (*chip, 1 - c), me).wait_recv()
    for cp in first + passed:
        cp.wait_send()
    mine.wait()


_TWO_LEVEL_SEMS = [pltpu.SemaphoreType.DMA((N_DEV - 1,)), pltpu.SemaphoreType.DMA((N_DEV - 1,)), pltpu.SemaphoreType.DMA(())]


def _gather_two_level(shard, *, name):
    any_spec = pl.BlockSpec(memory_space=pl.ANY)
    return pl.pallas_call(
        _two_level_gather_body(), name=name, in_specs=[any_spec], out_specs=any_spec,
        out_shape=jax.ShapeDtypeStruct((N_DEV,) + shard.shape, shard.dtype), scratch_shapes=_TWO_LEVEL_SEMS,
    )(shard)


def _two_level_gather_body():
    def body(x_ref, out_ref, send_sems, recv_sems, local_sem):
        _two_level_gather(x_ref, out_ref, send_sems, recv_sems, local_sem)
    return body


def _all_reduce_small(part, *, loss_rows, loss_scale):
    rows = part.shape[0]

    def body(p_ref, o_ref, buf, send_sems, recv_sems, local_sem):
        _two_level_gather(p_ref, buf, send_sems, recv_sems, local_sem)
        total = buf[0]
        for dev in range(1, N_DEV):
            total = total + buf[dev]
        o_ref[...] = total
        squares = total[rows - loss_rows:]
        loss = jnp.sum(jnp.sum(squares, axis=0, keepdims=True), axis=-1, keepdims=True) * loss_scale
        o_ref[rows - loss_rows:, :] = jnp.broadcast_to(loss, (loss_rows, LANES))

    vmem = pl.BlockSpec(memory_space=pltpu.VMEM)
    return pl.pallas_call(
        body, name="all_reduce_small", in_specs=[vmem], out_specs=vmem, out_shape=jax.ShapeDtypeStruct(part.shape, F32),
        scratch_shapes=[pltpu.VMEM((N_DEV, rows, LANES), F32)] + _TWO_LEVEL_SEMS,
        compiler_params=pltpu.CompilerParams(has_side_effects=True, vmem_limit_bytes=VMEM_LIMIT),
    )(part)


def _adamw_math(w, g, m, v):
    m_new = ADAM_B1 * m + (1.0 - ADAM_B1) * g
    v_new = ADAM_B2 * v + (1.0 - ADAM_B2) * (g * g)
    m_hat = m_new / (1.0 - ADAM_B1 ** ADAM_STEP)
    v_hat = v_new / (1.0 - ADAM_B2 ** ADAM_STEP)
    delta = -ADAM_LR * (m_hat / (jnp.sqrt(v_hat) + ADAM_EPS) + ADAM_WD * w)
    return delta, m_new, v_new


def _adamw(parts, w, m, v, *, name, tr=64, parts_transposed=False):
    rows, cols = w.shape
    tr = min(tr, rows)
    while rows % (2 * tr) == 0 and 2 * tr * cols <= 256 * 1024:
        tr *= 2
    assert rows % tr == 0 and (not parts_transposed or tr % LANES == 0)
    stacked = parts.ndim == 3

    def body(p_ref, w_ref, m_ref, v_ref, g_ref, d_ref, mo_ref, vo_ref):
        if stacked:
            g = p_ref[0].astype(F32)
            for dev in range(1, N_DEV):
                g = g + p_ref[dev].astype(F32)
        else:
            g = p_ref[...]
        if parts_transposed:
            g = g.T
        delta, m_new, v_new = _adamw_math(w_ref[...], g, m_ref[...], v_ref[...])
        g_ref[...] = g
        d_ref[...] = delta
        mo_ref[...] = m_new
        vo_ref[...] = v_new

    tile = pl.BlockSpec((tr, cols), lambda i: (i, 0))
    if parts_transposed:
        p_spec = pl.BlockSpec((N_DEV, cols, tr), lambda i: (0, 0, i))
    else:
        p_spec = pl.BlockSpec((N_DEV, tr, cols), lambda i: (0, i, 0)) if stacked else tile
    return pl.pallas_call(
        body, name=name, grid=(rows // tr,), in_specs=[p_spec, tile, tile, tile], out_specs=[tile] * 4,
        out_shape=[jax.ShapeDtypeStruct((rows, cols), F32)] * 4, compiler_params=_params(1),
    )(parts, w, m, v)


_LATER = ("w_out", "w_cq", "w_ckv", "w_co", "w_ff1", "w_ff2")


def _as_rows(stacked):
    return stacked.reshape(-1, stacked.shape[-1])


def _local_step(x, mem, target, small, shards):
    batch, seq, d = x.shape
    n_mem = mem.shape[1]
    t = batch * seq
    x2, mem2, tgt2 = x.reshape(t, d), mem.reshape(batch * n_mem, d), target.reshape(t, d)
    g_mix, g_cross, g_mem, g_ffn, g_final = (small[k] for k in ("norm_mix_g", "norm_cross_g", "norm_mem_g", "norm_ffn_g", "norm_final_g"))
    gv, hg, w_sp, b_sp_t = small["gm_v_norm_g"], small["head_norm_g"], small["w_spatial"], small["b_spatial_t"]

    win_t = _as_rows(_gather_two_level(shards["w_in"], name="gather_w_in"))
    proj, xn = _norm_mm(x2, g_mix, win_t, mode="nt", name="proj_fwd", tm=512, tn=win_t.shape[0])
    merged_a = _gmlp_fwd(proj, w_sp, b_sp_t, gv, hg)
    merged, sb_raw, sb_weights, sb_log_rest, gathered = _sb_fwd(proj, merged_a, hg, batch=batch, seq=seq,
                                                                exchange=_Exchange([shards[n] for n in _LATER], scatter=False))
    wout, wcq, wckv_t, wco, wff1_t, wff2 = (_as_rows(g) for g in gathered)
    h1, qx, hn1 = _mm(merged, wout, mode="nn", out_dtype=F32, name="mix_out_fwd_xq_fwd", tm=512, row_parts=2, epi=_epi_residual_then_norm_mm("nn"),
                      epi_ins=(x2,), vec_ins=(g_cross,), whole_ins=(wcq,), more_outs=((wcq.shape[1], BF16), (d, BF16)))
    kvx, memn = _norm_mm(mem2, g_mem, wckv_t, mode="nt", name="xkv_fwd", tm=512, tn=wckv_t.shape[0])
    o = _xattn_fwd(qx, kvx, batch=batch, seq=seq, n_mem=n_mem)
    h2, fpre, hn2 = _mm(o, wco, mode="nn", out_dtype=F32, name="xo_fwd_ff1_fwd", tm=512, row_parts=2, epi=_epi_residual_then_norm_mm("nt"),
                        epi_ins=(h1,), vec_ins=(g_ffn,), whole_ins=(wff1_t,), more_outs=((wff1_t.shape[0], BF16), (d, BF16)))
    dh3, final_rows = _mm(fpre, wff2, mode="nn", out_dtype=F32, name="ff2_fwd_loss", tm=512, row_parts=2, tk=wff2.shape[0], a_fn=_relu2,
                          epi=_epi_loss, epi_ins=(h2, tgt2), vec_ins=(g_final,), aux=2)
    dg_final, sq_err = final_rows[0:1], final_rows[1:2]

    dpre = _mm(dh3, wff2, mode="nt", out_dtype=BF16, name="ff2_bwd_x", tm=512, tn=wff2.shape[0], epi=_epi_relu2_grad,
               epi_ins=(fpre,))
    chunk = wff2.shape[0] // N_DEV
    d_wff2_t = _mm(dh3, fpre, mode="tn", out_dtype=BF16, name="ff2_bwd_w", tn=2048, b_fn=_relu2, col_chunk=chunk)
    d_wff1 = _mm(hn2, dpre, mode="tn", out_dtype=BF16, name="ff1_bwd_w", tn=2048, col_chunk=chunk)
    dh2, do, dg_ffn = _mm(dpre, wff1_t, mode="nn", out_dtype=F32, name="ff1_bwd_x_xo_bwd_x", tm=512, row_parts=2, tk=wff1_t.shape[0],
                          epi=_epi_rms_bwd_then_mm, epi_ins=(h2, dh3), vec_ins=(g_ffn,), whole_ins=(wco,),
                          more_outs=((wco.shape[0], BF16),), aux=True)
    d_wco = _mm(o, dh2, mode="tn", out_dtype=BF16, name="xo_bwd_w")
    dqx, dkvx = _xattn_bwd(qx, kvx, do, batch=batch, seq=seq, n_mem=n_mem)
    d_wcq = _mm(hn1, dqx, mode="tn", out_dtype=BF16, name="xq_bwd_w")
    dh1, dmerged, dg_cross = _mm(dqx, wcq, mode="nt", out_dtype=F32, name="xq_bwd_x_mix_out_bwd_x", tm=512, row_parts=2,
                                 epi=_epi_rms_bwd_then_mm, epi_ins=(h1, dh2), vec_ins=(g_cross,), whole_ins=(wout,),
                                 more_outs=((wout.shape[0], BF16),), aux=True)
    d_wckv_t = _mm(dkvx, memn, mode="tn", out_dtype=BF16, name="xkv_bwd_w")
    _, dg_mem = _mm(dkvx, wckv_t, mode="nn", out_dtype=BF16, name="xkv_bwd_x", tm=512, epi=_epi_rms_gain_only,
                    epi_ins=(mem2,), vec_ins=(g_mem,), aux=True)
    d_wout = _mm(merged, dh1, mode="tn", out_dtype=BF16, name="mix_out_bwd_w")
    dp_a, d_wsp, d_bsp_t, d_gv, d_hg_a = _gmlp_bwd(proj, dmerged, w_sp, b_sp_t, gv, hg)
    d_later = {"w_out": d_wout, "w_cq": d_wcq, "w_ckv": d_wckv_t, "w_co": d_wco, "w_ff1": d_wff1, "w_ff2": d_wff2_t}
    scatter = _Exchange([g if g.ndim == 3 else g.reshape(N_DEV, -1, d) for g in (d_later[n] for n in _LATER)], scatter=True)
    dq, dk, dv, d_hg_b, received = _sb_bwd(proj, sb_raw, sb_weights, sb_log_rest, dmerged, hg, batch=batch, seq=seq, exchange=scatter)
    dproj = dp_a
    for part, at in ((dq, 2 * GM_WIDTH), (dk, 3 * GM_WIDTH), (dv, 4 * GM_WIDTH)):
        dproj = lax.dynamic_update_slice(dproj, part, (0, at))
    d_win_t = _mm(xn, dproj, mode="tn", out_dtype=BF16, name="proj_bwd_w", tn=dproj.shape[1] // 2).T
    dx, dg_mix, d_win_received = _mm(dproj, win_t, mode="nn", out_dtype=F32, name="proj_bwd_x", tm=512, row_parts=2, tk=win_t.shape[0],
                                     epi=_epi_rms_bwd, epi_ins=(x2, dh1), vec_ins=(g_mix,), aux=True,
                                     exchange=_Exchange([d_win_t.reshape(N_DEV, -1, d)], scatter=True))

    d_small = {"norm_mix_g": dg_mix, "gm_v_norm_g": d_gv, "w_spatial": d_wsp, "b_spatial_t": d_bsp_t, "head_norm_g": jnp.concatenate([d_hg_a, d_hg_b], axis=1),
               "norm_cross_g": dg_cross, "norm_mem_g": dg_mem, "norm_ffn_g": dg_ffn, "norm_final_g": dg_final}
    d_big = dict(zip(_LATER, received))
    d_big["w_in"] = d_win_received
    return sq_err, dx.reshape(batch, seq, d), d_small, d_big


_BIG = ("w_in", "w_out", "w_cq", "w_ckv", "w_co", "w_ff1", "w_ff2")
_GATHERED_TRANSPOSED = ("w_in", "w_ckv", "w_ff1")
_UPDATED_TRANSPOSED = ("w_in", "w_ckv", "w_ff2")
_SMALL = ("norm_mix_g", "gm_v_norm_g", "w_spatial", "b_spatial", "head_norm_g", "norm_cross_g", "norm_mem_g", "norm_ffn_g", "norm_final_g")
_NAMES = ("norm_mix_g", "w_in", "gm_v_norm_g", "w_spatial", "b_spatial", "head_norm_g", "w_out", "norm_cross_g", "norm_mem_g",
          "w_cq", "w_ckv", "w_co", "norm_ffn_g", "w_ff1", "w_ff2", "norm_final_g")


def _rows_of(a):
    r = a.reshape(-1, LANES)
    pad = (-r.shape[0]) % 8
    return jnp.pad(r, ((0, pad), (0, 0))) if pad else r


def _shard2d(a, transposed):
    return a[0].T if transposed else a[0]


def kernel(x, mem, norm_mix_g, w_in, gm_v_norm_g, w_spatial, b_spatial, head_norm_g, w_out, norm_cross_g, norm_mem_g, w_cq, w_ckv, w_co, norm_ffn_g, w_ff1, w_ff2, norm_final_g, loss_target, m_norm_mix_g, m_w_in, m_gm_v_norm_g, m_w_spatial, m_b_spatial, m_head_norm_g, m_w_out, m_norm_cross_g, m_norm_mem_g, m_w_cq, m_w_ckv, m_w_co, m_norm_ffn_g, m_w_ff1, m_w_ff2, m_norm_final_g, v_norm_mix_g, v_w_in, v_gm_v_norm_g, v_w_spatial, v_b_spatial, v_head_norm_g, v_w_out, v_norm_cross_g, v_norm_mem_g, v_w_cq, v_w_ckv, v_w_co, v_norm_ffn_g, v_w_ff1, v_w_ff2, v_norm_final_g):
    weights = dict(norm_mix_g=norm_mix_g, w_in=w_in, gm_v_norm_g=gm_v_norm_g, w_spatial=w_spatial, b_spatial=b_spatial,
                   head_norm_g=head_norm_g, w_out=w_out, norm_cross_g=norm_cross_g, norm_mem_g=norm_mem_g, w_cq=w_cq, w_ckv=w_ckv,
                   w_co=w_co, norm_ffn_g=norm_ffn_g, w_ff1=w_ff1, w_ff2=w_ff2, norm_final_g=norm_final_g)
    mom1 = dict(norm_mix_g=m_norm_mix_g, w_in=m_w_in, gm_v_norm_g=m_gm_v_norm_g, w_spatial=m_w_spatial, b_spatial=m_b_spatial,
                head_norm_g=m_head_norm_g, w_out=m_w_out, norm_cross_g=m_norm_cross_g, norm_mem_g=m_norm_mem_g, w_cq=m_w_cq,
                w_ckv=m_w_ckv, w_co=m_w_co, norm_ffn_g=m_norm_ffn_g, w_ff1=m_w_ff1, w_ff2=m_w_ff2, norm_final_g=m_norm_final_g)
    mom2 = dict(norm_mix_g=v_norm_mix_g, w_in=v_w_in, gm_v_norm_g=v_gm_v_norm_g, w_spatial=v_w_spatial, b_spatial=v_b_spatial,
                head_norm_g=v_head_norm_g, w_out=v_w_out, norm_cross_g=v_norm_cross_g, norm_mem_g=v_norm_mem_g, w_cq=v_w_cq,
                w_ckv=v_w_ckv, w_co=v_w_co, norm_ffn_g=v_norm_ffn_g, w_ff1=v_w_ff1, w_ff2=v_w_ff2, norm_final_g=v_norm_final_g)

    shards = {n: _shard2d(weights[n], n in _GATHERED_TRANSPOSED).astype(BF16) for n in _BIG}
    small = {n: weights[n].reshape(1, -1) for n in _SMALL if n not in ("w_spatial", "b_spatial")}
    small["w_spatial"] = w_spatial[0]
    small["b_spatial_t"] = b_spatial[0].T
    sq_err, grad_x, d_small, d_big = _local_step(x, mem, loss_target, small, shards)

    d_small["b_spatial"] = d_small.pop("b_spatial_t").T
    sq_rows = _rows_of(sq_err)
    packed = jnp.concatenate([_rows_of(d_small[n]) for n in _SMALL] + [sq_rows], axis=0)
    summed = _all_reduce_small(packed, loss_rows=sq_rows.shape[0], loss_scale=0.5 / x.shape[-1])

    grads, deltas, new_m, new_v = {}, {}, {}, {}
    for n in _BIG:
        on_chip = n in _UPDATED_TRANSPOSED and weights[n].shape[-1] % LANES == 0
        flip = n in _UPDATED_TRANSPOSED and not on_chip
        outs = _adamw(d_big[n], _shard2d(weights[n], flip), _shard2d(mom1[n], flip), _shard2d(mom2[n], flip), name="adamw_" + n,
                      parts_transposed=on_chip)
        outs = [o.T if flip else o for o in outs]
        grads[n], deltas[n], new_m[n], new_v[n] = (o[None] for o in outs)
    pack = lambda src: jnp.concatenate([_rows_of(src[n]) for n in _SMALL], axis=0)
    n_small_rows = sum(_rows_of(weights[n]).shape[0] for n in _SMALL)
    outs = _adamw(summed[:n_small_rows], pack(weights), pack(mom1), pack(mom2), name="adamw_small", tr=n_small_rows)
    at = 0
    for n in _SMALL:
        used = weights[n].size // LANES
        for dst, o in zip((grads, deltas, new_m, new_v), outs):
            dst[n] = o[at:at + used].reshape(weights[n].shape)
        at += _rows_of(weights[n]).shape[0]
    loss = summed[n_small_rows, 0]
    return (loss, grad_x, *[grads[n] for n in _NAMES], *[deltas[n] for n in _NAMES], *[new_m[n] for n in _NAMES],
            *[new_v[n] for n in _NAMES])
```

```python
import math

import jax
import jax.numpy as jnp
from jax import lax
from jax.experimental import pallas as pl
from jax.experimental.pallas import tpu as pltpu

F32 = jnp.float32
BF16 = jnp.bfloat16
EPS = 1e-6
N_DEV = 8
LANES = 128
CHUNK = 128
GM_GROUPS = 4
GM_WIDTH = 512
SB_PAIRS = 4
SB_HEAD_DIM = 64
SB_SCALE = 0.125
SB_TILE = 128
SB_BLOCK = 512
X_HEADS = 4
X_HEAD_DIM = 256
X_SCALE = 1.0 / 16.0
VMEM_LIMIT = 56 * 1024 * 1024
ADAM_LR, ADAM_B1, ADAM_B2, ADAM_EPS, ADAM_WD, ADAM_STEP = 0.001, 0.9, 0.999, 1e-08, 0.01, 10
MESH = pl.DeviceIdType.MESH


def _params(n_axes):
    return pltpu.CompilerParams(dimension_semantics=("arbitrary",) * n_axes, vmem_limit_bytes=VMEM_LIMIT)


def _dot(a, b, dims):
    return lax.dot_general(a, b, (dims, ((), ())), preferred_element_type=F32)


def _nn(a, b):
    return _dot(a, b, ((1,), (0,)))


def _nt(a, b):
    return _dot(a, b, ((1,), (1,)))


def _tn(a, b):
    return _dot(a, b, ((0,), (0,)))


_MODES = {"nn": _nn, "nt": _nt, "tn": _tn}


def _rstd(x):
    return lax.rsqrt(jnp.mean(x * x, axis=-1, keepdims=True) + EPS)


def _gelu(x):
    c = math.sqrt(2.0 / math.pi)
    t = jnp.tanh(c * (x + 0.044715 * x * x * x))
    return 0.5 * x * (1.0 + t)


def _gelu_and_grad(x):
    c = math.sqrt(2.0 / math.pi)
    t = jnp.tanh(c * (x + 0.044715 * x * x * x))
    half = 0.5 * (1.0 + t)
    return x * half, half + 0.5 * x * (1.0 - t * t) * c * (1.0 + 3 * 0.044715 * x * x)


def _split_bf16(x):
    hi = x.astype(BF16)
    lo = (x - hi.astype(F32)).astype(BF16)
    return hi, lo


def _mm(a, b, *, mode, out_dtype, name, tm=1024, tn=1024, tk=1024, a_fn=None, b_fn=None, epi=None, epi_ins=(), vec_ins=(),
        whole_ins=(), more_outs=(), aux=False, col_chunk=None, row_parts=1, exchange=None):
    if mode == "nn":
        (m, k), (k2, n) = a.shape, b.shape
    elif mode == "nt":
        (m, k), (n, k2) = a.shape, b.shape
    else:
        (k, m), (k2, n) = a.shape, b.shape
    assert k == k2, (a.shape, b.shape, mode)
    tm, tn, tk = min(tm, m), min(tn, n), min(tk, k)
    assert m % tm == 0 and n % tn == 0 and k % tk == 0, (m, n, k, tm, tn, tk)
    n_m, n_n, n_k = m // tm, n // tn, k // tk
    assert not (aux or more_outs) or n_n == 1
    assert row_parts == 1 or (n_k == 1 and mode != "tn" and epi is not None and tm % (8 * row_parts) == 0)
    dot = _MODES[mode]
    n_epi, n_vec, n_whole, n_more = len(epi_ins), len(vec_ins), len(whole_ins), len(more_outs)

    def body(*refs):
        ins, outs, scratch, x_refs = _riding(exchange, refs, 2 + n_epi + n_vec + n_whole, 1 + n_more + (1 if aux else 0),
                                             1 if n_k > 1 else 0)
        a_ref, b_ref, epi_refs = ins[0], ins[1], ins[2:]
        o_ref, more_refs = outs[0], outs[1:1 + n_more]
        aux_ref = outs[1 + n_more] if aux else None
        acc_ref = scratch[0] if n_k > 1 else None
        i, j, kk = pl.program_id(0), pl.program_id(1), pl.program_id(2)
        ride_done = _ride(exchange, x_refs, (i == 0) & (j == 0) & (kk == 0), (i == n_m - 1) & (j == n_n - 1) & (kk == n_k - 1))
        def product(rows=slice(None)):
            av, bv = a_ref[...] if mode == "tn" else a_ref[rows, :], b_ref[...]
            if a_fn is not None:
                av = a_fn(av)
            if b_fn is not None:
                bv = b_fn(bv)
            return dot(av.astype(BF16), bv.astype(BF16))

        def finish(accs, parts=(slice(None),)):
            if col_chunk is not None:
                for ch in range(tn // col_chunk):
                    o_ref[ch] = accs[0][:, ch * col_chunk:(ch + 1) * col_chunk].astype(out_dtype)
                return
            if epi is None:
                o_ref[...] = accs[0].astype(out_dtype)
                return
            row_sums = []
            for acc, rows in zip(accs, parts):
                res = epi(acc, *[r[rows, :] for r in epi_refs[:n_epi]], *[r[...] for r in epi_refs[n_epi:]])
                if n_more:
                    for more_ref, value in zip(more_refs, res[1:1 + n_more]):
                        more_ref[rows, :] = value.astype(more_ref.dtype)
                    res = (res[0],) + tuple(res[1 + n_more:]) if aux else res[0]
                if aux:
                    res, sums = res[0], res[1:]
                    row_sums.append(sums[0] if len(sums) == 1 else jnp.concatenate(sums, axis=0))
                o_ref[rows, :] = res.astype(out_dtype)
            if aux:
                total = sum(row_sums)

                @pl.when(i == 0)
                def _():
                    aux_ref[...] = total

                @pl.when(i != 0)
                def _():
                    aux_ref[...] += total

        if n_k == 1:
            parts = [slice(h * (tm // row_parts), (h + 1) * (tm // row_parts)) for h in range(row_parts)]
            finish([product(rows) for rows in parts], parts)
        else:
            @pl.when(kk == 0)
            def _():
                acc_ref[...] = product()

            @pl.when(kk != 0)
            def _():
                acc_ref[...] += product()

            @pl.when(kk == n_k - 1)
            def _():
                finish([acc_ref[...]])

        ride_done()

    if mode == "tn":
        a_spec = pl.BlockSpec((tk, tm), lambda i, j, kk: (kk, i))
    else:
        a_spec = pl.BlockSpec((tm, tk), lambda i, j, kk: (i, kk))
    if mode == "nt":
        b_spec = pl.BlockSpec((tn, tk), lambda i, j, kk: (j, kk))
    else:
        b_spec = pl.BlockSpec((tk, tn), lambda i, j, kk: (kk, j))
    tile_spec = pl.BlockSpec((tm, tn), lambda i, j, kk: (i, j))
    row_spec = pl.BlockSpec((1, tn), lambda i, j, kk: (0, j))
    out_shape = [jax.ShapeDtypeStruct((m, n), out_dtype)]
    out_specs = [tile_spec]
    if col_chunk is not None:
        assert epi is None and not aux and tn % col_chunk == 0
        out_shape = [jax.ShapeDtypeStruct((n // col_chunk, m, col_chunk), out_dtype)]
        out_specs = [pl.BlockSpec((tn // col_chunk, tm, col_chunk), lambda i, j, kk: (j, i, 0))]
    for columns, dtype in more_outs:
        out_shape.append(jax.ShapeDtypeStruct((m, columns), dtype))
        out_specs.append(pl.BlockSpec((tm, columns), lambda i, j, kk: (i, 0)))
    if aux:
        out_shape.append(jax.ShapeDtypeStruct((int(aux), n), F32))
        out_specs.append(pl.BlockSpec((int(aux), tn), lambda i, j, kk: (0, j)))
    whole_specs = [pl.BlockSpec(w.shape, lambda i, j, kk, nd=w.ndim: (0,) * nd) for w in whole_ins]
    x_in, x_out, x_shape, x_scratch, x_arrays = _riding_specs(exchange)
    res = pl.pallas_call(
        body, name=name, grid=(n_m, n_n, n_k),
        in_specs=[a_spec, b_spec] + [tile_spec] * n_epi + [row_spec] * n_vec + whole_specs + x_in,
        out_specs=out_specs + x_out, out_shape=out_shape + x_shape,
        scratch_shapes=([pltpu.VMEM((tm, tn), F32)] if n_k > 1 else []) + x_scratch,
        compiler_params=_params(3),
    )(a, b, *epi_ins, *vec_ins, *whole_ins, *x_arrays)
    if exchange is not None or more_outs:
        return tuple(res)
    return res if aux else res[0]


def _norm_mm(x, g, w, *, mode, name, tm=1024, tn=1024):
    m, d = x.shape
    n = w.shape[0] if mode == "nt" else w.shape[1]
    tm, tn = min(tm, m), min(tn, n)
    assert m % tm == 0 and n % tn == 0
    dot = _MODES[mode]

    def body(x_ref, g_ref, w_ref, o_ref, xn_ref, xn_s):
        @pl.when(pl.program_id(1) == 0)
        def _():
            xv = x_ref[...]
            xn = (xv * _rstd(xv) * g_ref[...]).astype(BF16)
            xn_s[...] = xn
            xn_ref[...] = xn

        o_ref[...] = dot(xn_s[...], w_ref[...]).astype(BF16)

    w_spec = pl.BlockSpec((tn, d), lambda i, j: (j, 0)) if mode == "nt" else pl.BlockSpec((d, tn), lambda i, j: (0, j))
    return pl.pallas_call(
        body, name=name, grid=(m // tm, n // tn),
        in_specs=[pl.BlockSpec((tm, d), lambda i, j: (i, 0)), pl.BlockSpec((1, d), lambda i, j: (0, 0)), w_spec],
        out_specs=[pl.BlockSpec((tm, tn), lambda i, j: (i, j)), pl.BlockSpec((tm, d), lambda i, j: (i, 0))],
        out_shape=[jax.ShapeDtypeStruct((m, n), BF16), jax.ShapeDtypeStruct((m, d), BF16)],
        scratch_shapes=[pltpu.VMEM((tm, d), BF16)],
        compiler_params=_params(2),
    )(x, g, w)


def _epi_residual(acc, res):
    return res + acc


def _epi_relu2_grad(acc, pre):
    return acc * (2.0 * jnp.maximum(pre.astype(F32), 0.0))


def _relu2(pre):
    r = jnp.maximum(pre.astype(F32), 0.0)
    return r * r


def _epi_rms_bwd(acc, h, dres, g):
    r = _rstd(h)
    xh = h * r
    dxh = acc * g
    dh = dres + r * (dxh - xh * jnp.mean(dxh * xh, axis=-1, keepdims=True))
    return dh, jnp.sum(acc * xh, axis=0, keepdims=True)


def _epi_residual_then_norm_mm(mode):
    dot = _MODES[mode]

    def epi(acc, res, g, w):
        h = res + acc
        hn = (h * _rstd(h) * g).astype(BF16)
        return h, dot(hn, w), hn

    return epi


def _epi_rms_bwd_then_mm(acc, h, dres, g, w):
    dh, row = _epi_rms_bwd(acc, h, dres, g)
    return dh, _nt(dh.astype(BF16), w), row


def _epi_loss(acc, h_in, target, g):
    h = h_in + acc
    r = _rstd(h)
    xh = h * r
    err = xh * g - target
    dy = err * (1.0 / h.shape[-1])
    dxh = dy * g
    dh = r * (dxh - xh * jnp.mean(dxh * xh, axis=-1, keepdims=True))
    return dh, jnp.sum(dy * xh, axis=0, keepdims=True), jnp.sum(err * err, axis=0, keepdims=True)


def _epi_rms_gain_only(acc, h, g):
    return acc, jnp.sum(acc * (h * _rstd(h)), axis=0, keepdims=True)


def _tril(n):
    row = lax.broadcasted_iota(jnp.int32, (n, n), 0)
    col = lax.broadcasted_iota(jnp.int32, (n, n), 1)
    return col <= row


def _gmlp_fwd(proj, w_sp, b_sp_t, gv, hg, *, rows=512):
    t = proj.shape[0]
    rows = min(rows, t)
    n_c = rows // CHUNK

    def body(u_ref, v_ref, w_ref, bt_ref, gv_ref, hg_ref, m_ref):
        keep = _tril(CHUNK)
        for g in range(GM_GROUPS):
            cols = slice(g * LANES, (g + 1) * LANES)
            wg = jnp.where(keep, w_ref[g], 0.0).astype(BF16)
            u = _gelu(u_ref[:, cols].astype(F32))
            v = _gelu(v_ref[:, cols].astype(F32))
            vn = (v * _rstd(v) * gv_ref[:, cols]).astype(BF16)
            bias = bt_ref[:, g:g + 1]
            chunks = [slice(c * CHUNK, (c + 1) * CHUNK) for c in range(n_c)]
            gated = [u[rs] * (_nn(wg, vn[rs]) + bias) for rs in chunks]
            for rs, a in zip(chunks, gated):
                m_ref[rs, cols] = (a * _rstd(a) * hg_ref[:, cols]).astype(BF16)

    full = lambda shape: pl.BlockSpec(shape, lambda i: (0,) * len(shape))
    return pl.pallas_call(
        body, name="gmlp_fwd", grid=(t // rows,),
        in_specs=[pl.BlockSpec((rows, GM_WIDTH), lambda i: (i, 0)), pl.BlockSpec((rows, GM_WIDTH), lambda i: (i, 1)),
                  full((GM_GROUPS, CHUNK, CHUNK)), full((CHUNK, GM_GROUPS)), full((1, GM_WIDTH)), full((1, GM_WIDTH))],
        out_specs=pl.BlockSpec((rows, GM_WIDTH), lambda i: (i, 0)),
        out_shape=jax.ShapeDtypeStruct((t, 2 * GM_WIDTH), BF16),
        compiler_params=_params(1),
    )(proj, proj, w_sp, b_sp_t, gv, hg)


def _gmlp_bwd(proj, dmerged, w_sp, b_sp_t, gv, hg, *, rows=512):
    t = proj.shape[0]
    rows = min(rows, t)
    n_c = rows // CHUNK
    n_steps = t // rows

    def body(u_ref, v_ref, dm_ref, w_ref, bt_ref, gv_ref, hg_ref, dp_ref, dw_ref, dbt_ref, dgv_ref, dhg_ref, db_acc):
        step = pl.program_id(0)
        keep = _tril(CHUNK)

        @pl.when(step == 0)
        def _():
            dw_ref[...] = jnp.zeros_like(dw_ref)
            db_acc[...] = jnp.zeros_like(db_acc)
            dgv_ref[...] = jnp.zeros_like(dgv_ref)
            dhg_ref[...] = jnp.zeros_like(dhg_ref)

        for g in range(GM_GROUPS):
            cols = slice(g * LANES, (g + 1) * LANES)
            wg = jnp.where(keep, w_ref[g], 0.0).astype(BF16)
            u, u_slope = _gelu_and_grad(u_ref[:, cols].astype(F32))
            v, v_slope = _gelu_and_grad(v_ref[:, cols].astype(F32))
            r = _rstd(v)
            xh = v * r
            gvg = gv_ref[:, cols]
            hgg = hg_ref[:, cols]
            vn = (xh * gvg).astype(BF16)
            bias = bt_ref[:, g:g + 1]
            dm = dm_ref[:, cols].astype(F32)
            chunks = [slice(c * CHUNK, (c + 1) * CHUNK) for c in range(n_c)]
            mixed = [_nn(wg, vn[rs]) + bias for rs in chunks]
            gated = [u[rs] * mx for rs, mx in zip(chunks, mixed)]
            scale = [_rstd(a) for a in gated]
            normed = [a * ra for a, ra in zip(gated, scale)]
            d_normed = [dm[rs] * hgg for rs in chunks]
            d_gated = [ra * (dan - an * jnp.mean(dan * an, axis=-1, keepdims=True)) for ra, dan, an in zip(scale, d_normed, normed)]
            d_mixed = [da * u[rs] for da, rs in zip(d_gated, chunks)]
            d_mixed_b = [dmx.astype(BF16) for dmx in d_mixed]
            du = jnp.concatenate([da * mx for da, mx in zip(d_gated, mixed)], axis=0)
            dvn = jnp.concatenate([_tn(wg, dmb) for dmb in d_mixed_b], axis=0)
            dw_ref[g] += sum(_nt(dmb, vn[rs]) for dmb, rs in zip(d_mixed_b, chunks))
            db_acc[g] += sum(d_mixed)
            dhg_ref[:, cols] += sum(jnp.sum(dm[rs] * an, axis=0, keepdims=True) for rs, an in zip(chunks, normed))
            dgv_ref[:, cols] += jnp.sum(dvn * xh, axis=0, keepdims=True)
            dxh = dvn * gvg
            dv = r * (dxh - xh * jnp.mean(dxh * xh, axis=-1, keepdims=True))
            dp_ref[:, cols] = (du * u_slope).astype(BF16)
            dp_ref[:, GM_WIDTH + g * LANES:GM_WIDTH + (g + 1) * LANES] = (dv * v_slope).astype(BF16)

        @pl.when(step == n_steps - 1)
        def _():
            for g in range(GM_GROUPS):
                dw_ref[g] = jnp.where(keep, dw_ref[g], 0.0)
                dbt_ref[:, g:g + 1] = jnp.sum(db_acc[g], axis=-1, keepdims=True)

    full = lambda shape: pl.BlockSpec(shape, lambda i: (0,) * len(shape))
    return pl.pallas_call(
        body, name="gmlp_bwd", grid=(n_steps,),
        in_specs=[pl.BlockSpec((rows, GM_WIDTH), lambda i: (i, 0)), pl.BlockSpec((rows, GM_WIDTH), lambda i: (i, 1)),
                  pl.BlockSpec((rows, GM_WIDTH), lambda i: (i, 0)),
                  full((GM_GROUPS, CHUNK, CHUNK)), full((CHUNK, GM_GROUPS)), full((1, GM_WIDTH)), full((1, GM_WIDTH))],
        out_specs=[pl.BlockSpec((rows, 2 * GM_WIDTH), lambda i: (i, 0)), full((GM_GROUPS, CHUNK, CHUNK)),
                   full((CHUNK, GM_GROUPS)), full((1, GM_WIDTH)), full((1, GM_WIDTH))],
        out_shape=[jax.ShapeDtypeStruct((t, proj.shape[1]), BF16), jax.ShapeDtypeStruct((GM_GROUPS, CHUNK, CHUNK), F32),
                   jax.ShapeDtypeStruct((CHUNK, GM_GROUPS), F32), jax.ShapeDtypeStruct((1, GM_WIDTH), F32),
                   jax.ShapeDtypeStruct((1, GM_WIDTH), F32)],
        scratch_shapes=[pltpu.VMEM((GM_GROUPS, CHUNK, LANES), F32)],
        compiler_params=_params(1),
    )(proj, proj, dmerged, w_sp, b_sp_t, gv, hg)


def _sb_logits(z, strict):
    ls = jnp.minimum(z, 0.0) - jnp.log(1.0 + jnp.exp(-jnp.abs(z)))
    l1m = ls - z
    if strict is not None:
        l1m = jnp.where(strict, l1m, 0.0)
    return ls, l1m


def _tri_sums(x, tri):
    hi, lo = _split_bf16(x)
    return _nn(jnp.concatenate([hi, lo], axis=1), jnp.concatenate([tri, tri], axis=0))


def _sb_weights(ls, in_tile, right, strict):
    a = jnp.exp(ls + in_tile + right)
    if strict is not None:
        a = jnp.where(strict, a, 0.0)
    return a


def _sb_masks(q_rows):
    row = lax.broadcasted_iota(jnp.int32, (SB_TILE, SB_TILE), 0)
    col = lax.broadcasted_iota(jnp.int32, (SB_TILE, SB_TILE), 1)
    lane = lax.broadcasted_iota(jnp.int32, (q_rows, LANES), 1)
    return row, col, lane < SB_HEAD_DIM


def _stack_heads(x, first):
    zero = jnp.zeros_like(x)
    return jnp.concatenate([jnp.where(first, x, zero), jnp.where(first, zero, x)], axis=0)


def _stack_heads_t(x_t):
    first_t = lax.broadcasted_iota(jnp.int32, x_t.shape, 0) < SB_HEAD_DIM
    zero = jnp.zeros_like(x_t)
    return jnp.concatenate([jnp.where(first_t, x_t, zero), jnp.where(first_t, zero, x_t)], axis=1).astype(BF16)


def _unstack_heads(x2, first):
    half = x2.shape[0] // 2
    return jnp.where(first, x2[:half], x2[half:])


def _live_rows(x, s, q_rows):
    if s == 0:
        return x
    return jnp.concatenate([x[s * SB_TILE:q_rows], x[q_rows + s * SB_TILE:]], axis=0)


def _spread_rows(x, s, q_rows):
    if s == 0:
        return x
    half = q_rows - s * SB_TILE
    zero = jnp.zeros((s * SB_TILE,) + x.shape[1:], x.dtype)
    return jnp.concatenate([zero, x[:half], zero, x[half:]], axis=0)


def _stacked_col_minus_row(q_rows):
    row = lax.broadcasted_iota(jnp.int32, (2 * q_rows, SB_TILE), 0)
    col = lax.broadcasted_iota(jnp.int32, (2 * q_rows, SB_TILE), 1)
    return col - (row & (q_rows - 1))


def _head_mean(x, first):
    s0 = jnp.sum(jnp.where(first, x, 0.0), axis=-1, keepdims=True)
    s1 = jnp.sum(jnp.where(first, 0.0, x), axis=-1, keepdims=True)
    return jnp.where(first, s0, s1) * (1.0 / SB_HEAD_DIM)


def _riding(exchange, refs, n_in, n_out, n_scratch):
    n_x = exchange.n if exchange is not None else 0
    ins, rest = refs[:n_in], refs[n_in:]
    x_src, rest = rest[:n_x], rest[n_x:]
    outs, rest = rest[:n_out], rest[n_out:]
    x_dst, rest = rest[:n_x], rest[n_x:]
    return ins, outs, rest[:n_scratch], (x_src, x_dst, rest[n_scratch:])


def _riding_specs(exchange):
    if exchange is None:
        return [], [], [], [], []
    return exchange.in_specs, exchange.out_specs, exchange.out_shape, exchange.scratch, exchange.arrays


def _ride(exchange, x_refs, first_step, last_step):
    if exchange is None:
        return lambda: None

    @pl.when(first_step)
    def _():
        exchange.start(*x_refs)

    def finish():
        @pl.when(last_step)
        def _():
            exchange.wait(*x_refs)

    return finish


def _sb_fwd(proj, merged_a, hg, *, batch, seq, exchange=None):
    q0, k0, v0 = 2 * GM_WIDTH // LANES, 2 * GM_WIDTH // LANES + SB_PAIRS, 2 * GM_WIDTH // LANES + 2 * SB_PAIRS
    q_rows = block_keys = min(SB_BLOCK, seq)
    assert seq % block_keys == 0
    n_q, n_sub, n_blocks = seq // q_rows, block_keys // SB_TILE, seq // block_keys

    def body(*refs):
        (q_ref, k_ref, v_ref, hg_ref, _), (m_ref, raw_ref, a_ref, l_ref), _, x_refs = _riding(exchange, refs, 5, 4, 0)
        b, p, i = pl.program_id(0), pl.program_id(1), pl.program_id(2)
        finish = _ride(exchange, x_refs, (b == 0) & (p == 0) & (i == 0), (b == batch - 1) & (p == SB_PAIRS - 1) & (i == n_q - 1))
        row, col, first = _sb_masks(q_rows)
        upper = (row > col).astype(BF16)
        q2 = _stack_heads((q_ref[...].astype(F32) * SB_SCALE).astype(BF16), first)
        diff = _stacked_col_minus_row(q_rows)
        last = i

        def key_tile(jb, s):
            return k_ref[pl.ds(pl.multiple_of((jb * n_sub + s) * SB_TILE, SB_TILE), SB_TILE), :]

        def scores(jb):
            return tuple(_nt(q2, key_tile(jb, s)) for s in range(n_sub))

        def keep_for_backward(ref, jb, s, stacked):
            ref[0, 0, jb, 0, :, s * SB_TILE:(s + 1) * SB_TILE] = stacked

        def weights_of(jb, z, right):
            logits = [_sb_logits(z[s], None) for s in reversed(range(n_sub))][::-1]
            totals = [jnp.sum(l1m, axis=-1, keepdims=True) for _, l1m in logits]
            sums = [_tri_sums(l1m, upper) for _, l1m in reversed(logits)][::-1]
            for s in reversed(range(n_sub)):
                keep_for_backward(a_ref, jb, s, _sb_weights(logits[s][0], sums[s], right, None).astype(BF16))
                keep_for_backward(l_ref, jb, s, logits[s][1].astype(BF16))
                right = right + totals[s]
            return right

        def diagonal_weights():
            keeps = [_live_rows(diff, s, q_rows) < -s * SB_TILE for s in range(n_sub)]
            logits = [_sb_logits(_nt(_live_rows(q2, s, q_rows), key_tile(last, s)), keeps[s]) for s in range(n_sub)]
            totals = [jnp.sum(l1m, axis=-1, keepdims=True) for _, l1m in logits]
            sums = [_tri_sums(l1m, upper) for _, l1m in logits]
            right = jnp.zeros((2 * q_rows, 1), F32)
            for s in reversed(range(n_sub)):
                live = _sb_weights(logits[s][0], sums[s], _live_rows(right, s, q_rows), keeps[s]).astype(BF16)
                keep_for_backward(a_ref, last, s, _spread_rows(live, s, q_rows))
                keep_for_backward(l_ref, last, s, _spread_rows(logits[s][1].astype(BF16), s, q_rows))
                right = right + _spread_rows(totals[s], s, q_rows)
            return right

        def weighted_values(jb):
            return _nn(a_ref[0, 0, jb, 0], v_ref[pl.ds(pl.multiple_of(jb * block_keys, block_keys), block_keys), :])

        right = diagonal_weights()
        z_next = scores(jnp.maximum(last - 1, 0))

        def step(k, carry):
            right, acc, z = carry
            jb = last - k
            acc = acc + weighted_values(jb + 1)
            z_next = scores(jnp.maximum(jb - 1, 0))
            return weights_of(jb, z, right), acc, z_next

        _, acc2, _ = lax.fori_loop(1, last + 1, step, (right, jnp.zeros((2 * q_rows, LANES), F32), z_next))
        acc = _unstack_heads(acc2 + weighted_values(0), first)
        raw_ref[...] = acc
        m_ref[...] = (acc * lax.rsqrt(_head_mean(acc * acc, first) + EPS) * hg_ref[...]).astype(BF16)
        finish()

    t = batch * seq
    blk = lambda c0: pl.BlockSpec((q_rows, LANES), lambda b, p, i: (b * n_q + i, c0 + p))
    kv = lambda c0: pl.BlockSpec((seq, LANES), lambda b, p, i: (b, c0 + p))
    kept = pl.BlockSpec((1, 1, n_blocks, 1, 2 * q_rows, block_keys), lambda b, p, i: (p, b, 0, i, 0, 0))
    kept_shape = jax.ShapeDtypeStruct((SB_PAIRS, batch, n_blocks, n_q, 2 * q_rows, block_keys), BF16)
    x_in, x_out, x_shape, x_scratch, x_arrays = _riding_specs(exchange)
    res = pl.pallas_call(
        body, name="sb_fwd", grid=(batch, SB_PAIRS, n_q),
        in_specs=[blk(q0), kv(k0), kv(v0), pl.BlockSpec((1, LANES), lambda b, p, i: (0, SB_PAIRS + p)),
                  pl.BlockSpec(memory_space=pl.ANY)] + x_in,
        out_specs=[blk(SB_PAIRS), blk(0), kept, kept] + x_out,
        out_shape=[jax.ShapeDtypeStruct((t, 2 * GM_WIDTH), BF16), jax.ShapeDtypeStruct((t, GM_WIDTH), F32), kept_shape, kept_shape] + x_shape,
        scratch_shapes=x_scratch,
        input_output_aliases={4: 0},
        compiler_params=_params(3),
    )(proj, proj, proj, hg, merged_a, *x_arrays)
    return res[0], res[1], res[2], res[3], res[4:]


def _sb_bwd(proj, raw, weights, log_rest, dmerged, hg, *, batch, seq, exchange=None):
    q0, k0, v0 = 2 * GM_WIDTH // LANES, 2 * GM_WIDTH // LANES + SB_PAIRS, 2 * GM_WIDTH // LANES + 2 * SB_PAIRS
    q_rows = block_keys = min(SB_BLOCK, seq)
    assert seq % block_keys == 0
    n_q, n_sub, n_blocks = seq // q_rows, block_keys // SB_TILE, seq // block_keys

    def body(*refs):
        ins, outs, (dk_acc, dv_acc), x_refs = _riding(exchange, refs, 8, 4, 2)
        q_ref, k_ref, v_ref, raw_ref, a_ref, l_ref, dm_ref, hg_ref = ins
        dq_ref, dk_ref, dv_ref, dhg_ref = outs
        p, b, i = pl.program_id(0), pl.program_id(1), pl.program_id(2)
        finish = _ride(exchange, x_refs, (b == 0) & (p == 0) & (i == 0), (b == batch - 1) & (p == SB_PAIRS - 1) & (i == n_q - 1))
        row, col, first = _sb_masks(q_rows)
        lower = (row < col).astype(BF16)

        @pl.when(jnp.logical_and(b == 0, i == 0))
        def _():
            dhg_ref[...] = jnp.zeros_like(dhg_ref)

        @pl.when(i == 0)
        def _():
            dk_acc[...] = jnp.zeros_like(dk_acc)
            dv_acc[...] = jnp.zeros_like(dv_acc)

        raw_v = raw_ref[...]
        dm = dm_ref[...].astype(F32)
        r = lax.rsqrt(_head_mean(raw_v * raw_v, first) + EPS)
        nrm = raw_v * r
        dhg_ref[...] += jnp.sum(dm * nrm, axis=0, keepdims=True)
        dn = dm * hg_ref[...]
        dout = r * (dn - nrm * _head_mean(dn * nrm, first))
        dout2 = _stack_heads(dout.astype(BF16), first)
        q2_t = _stack_heads_t(q_ref[...].astype(F32).T)
        dout2_t = _stack_heads_t(dout.T)
        diff = _stacked_col_minus_row(q_rows)
        last = i

        def kept(ref, jb, cols):
            return ref[0, 0, jb, 0, :, cols]

        def block(jb, carry, diagonal):
            gleft, dq = carry
            live = (lambda x, s: _live_rows(x, s, q_rows)) if diagonal else (lambda x, s: x)
            spread = (lambda x, s: _spread_rows(x, s, q_rows)) if diagonal else (lambda x, s: x)
            tiles = [pl.ds(pl.multiple_of((jb * n_sub + s) * SB_TILE, SB_TILE), SB_TILE) for s in range(n_sub)]
            cols = [slice(s * SB_TILE, (s + 1) * SB_TILE) for s in range(n_sub)]
            gmats = [_nt(live(dout2, s), v_ref[tiles[s], :]) * live(kept(a_ref, jb, cols[s]), s).astype(F32) for s in range(n_sub)]
            prefixes = [_tri_sums(g, lower) for g in gmats]
            dzs = []
            for s in range(n_sub):
                one_minus = jnp.exp(live(kept(l_ref, jb, cols[s]), s).astype(F32))
                dz = (gmats[s] * one_minus - (live(gleft, s) + prefixes[s]) * (1.0 - one_minus)) * SB_SCALE
                gleft = gleft + spread(jnp.sum(gmats[s], axis=-1, keepdims=True), s)
                if diagonal:
                    dz = jnp.where(live(diff, s) < -s * SB_TILE, dz, 0.0)
                dzs.append(spread(dz.astype(BF16), s))
            dz_all = jnp.concatenate(dzs, axis=1)
            dk_acc[jb] += _nn(q2_t, dz_all)
            dv_acc[jb] += _nn(dout2_t, kept(a_ref, jb, slice(None)))
            return gleft, dq + _nn(dz_all, k_ref[pl.ds(pl.multiple_of(jb * block_keys, block_keys), block_keys), :])

        carry = (jnp.zeros((2 * q_rows, 1), F32), jnp.zeros((2 * q_rows, LANES), F32))
        carry = lax.fori_loop(0, last, lambda jb, c: block(jb, c, False), carry)
        dq_ref[...] = _unstack_heads(block(last, carry, True)[1], first).astype(BF16)

        @pl.when(i == n_q - 1)
        def _():
            for jb in range(n_blocks):
                for s in range(n_sub):
                    rows = slice((jb * n_sub + s) * SB_TILE, (jb * n_sub + s + 1) * SB_TILE)
                    cols = slice(s * SB_TILE, (s + 1) * SB_TILE)
                    dk_ref[rows, :] = dk_acc[jb, :, cols].T.astype(BF16)
                    dv_ref[rows, :] = dv_acc[jb, :, cols].T.astype(BF16)

        finish()

    t = batch * seq
    blk = lambda c0: pl.BlockSpec((q_rows, LANES), lambda p, b, i: (b * n_q + i, c0 + p))
    kv = lambda c0: pl.BlockSpec((seq, LANES), lambda p, b, i: (b, c0 + p))
    row_spec = pl.BlockSpec((1, LANES), lambda p, b, i: (0, SB_PAIRS + p))
    kept_spec = pl.BlockSpec((1, 1, n_blocks, 1, 2 * q_rows, block_keys), lambda p, b, i: (p, b, 0, i, 0, 0))
    x_in, x_out, x_shape, x_scratch, x_arrays = _riding_specs(exchange)
    res = pl.pallas_call(
        body, name="sb_bwd", grid=(SB_PAIRS, batch, n_q),
        in_specs=[blk(q0), kv(k0), kv(v0), blk(0), kept_spec, kept_spec, blk(SB_PAIRS), row_spec] + x_in,
        out_specs=[blk(0), kv(0), kv(0), pl.BlockSpec((1, LANES), lambda p, b, i: (0, p))] + x_out,
        out_shape=[jax.ShapeDtypeStruct((t, GM_WIDTH), BF16)] * 3 + [jax.ShapeDtypeStruct((1, GM_WIDTH), F32)] + x_shape,
        scratch_shapes=[pltpu.VMEM((n_blocks, LANES, block_keys), F32), pltpu.VMEM((n_blocks, LANES, block_keys), F32)] + x_scratch,
        compiler_params=_params(3),
    )(proj, proj, proj, raw, weights, log_rest, dmerged, hg, *x_arrays)
    return res[0], res[1], res[2], res[3], res[4:]


def _x_softmax(s):
    s = s * X_SCALE
    p = jnp.exp(s - jnp.max(s, axis=-1, keepdims=True))
    return p * (1.0 / jnp.sum(p, axis=-1, keepdims=True))


def _x_heads(width):
    return [slice(h * X_HEAD_DIM, (h + 1) * X_HEAD_DIM) for h in range(X_HEADS)], \
           [slice(width + h * X_HEAD_DIM, width + (h + 1) * X_HEAD_DIM) for h in range(X_HEADS)]


def _xattn_fwd(q, kv, *, batch, seq, n_mem, tq=512):
    tq = min(tq, seq)
    n_q = seq // tq
    d = X_HEADS * X_HEAD_DIM

    def body(q_ref, kv_ref, o_ref):
        kcols, vcols = _x_heads(d)
        scores = [_nt(q_ref[:, c], kv_ref[:, c]) for c in kcols]
        probs = [_x_softmax(s).astype(BF16) for s in scores]
        for p, c, vc in zip(probs, kcols, vcols):
            o_ref[:, c] = _nn(p, kv_ref[:, vc]).astype(BF16)

    return pl.pallas_call(
        body, name="xattn_fwd", grid=(batch, n_q),
        in_specs=[pl.BlockSpec((tq, d), lambda b, i: (b * n_q + i, 0)), pl.BlockSpec((n_mem, 2 * d), lambda b, i: (b, 0))],
        out_specs=pl.BlockSpec((tq, d), lambda b, i: (b * n_q + i, 0)),
        out_shape=jax.ShapeDtypeStruct((batch * seq, d), BF16),
        compiler_params=_params(2),
    )(q, kv)


def _xattn_bwd(q, kv, do, *, batch, seq, n_mem, tq=512):
    tq = min(tq, seq)
    n_q = seq // tq
    d = X_HEADS * X_HEAD_DIM

    def body(q_ref, kv_ref, do_ref, dq_ref, dkv_ref, acc):
        i = pl.program_id(1)

        @pl.when(i == 0)
        def _():
            acc[...] = jnp.zeros_like(acc)

        kcols, vcols = _x_heads(d)
        scores = [_nt(q_ref[:, c], kv_ref[:, c]) for c in kcols]
        d_probs = [_nt(do_ref[:, c], kv_ref[:, vc]) for c, vc in zip(kcols, vcols)]
        probs = [_x_softmax(s) for s in scores]
        d_scores = [(p * (dp - jnp.sum(dp * p, axis=-1, keepdims=True)) * X_SCALE).astype(BF16) for p, dp in zip(probs, d_probs)]
        for p, ds, c, vc in zip(probs, d_scores, kcols, vcols):
            acc[:, vc] += _tn(p.astype(BF16), do_ref[:, c])
            dq_ref[:, c] = _nn(ds, kv_ref[:, c]).astype(BF16)
            acc[:, c] += _tn(ds, q_ref[:, c])

        @pl.when(i == n_q - 1)
        def _():
            dkv_ref[...] = acc[...].astype(BF16)

    return pl.pallas_call(
        body, name="xattn_bwd", grid=(batch, n_q),
        in_specs=[pl.BlockSpec((tq, d), lambda b, i: (b * n_q + i, 0)), pl.BlockSpec((n_mem, 2 * d), lambda b, i: (b, 0)),
                  pl.BlockSpec((tq, d), lambda b, i: (b * n_q + i, 0))],
        out_specs=[pl.BlockSpec((tq, d), lambda b, i: (b * n_q + i, 0)), pl.BlockSpec((n_mem, 2 * d), lambda b, i: (b, 0))],
        out_shape=[jax.ShapeDtypeStruct((batch * seq, d), BF16), jax.ShapeDtypeStruct((batch * n_mem, 2 * d), BF16)],
        scratch_shapes=[pltpu.VMEM((n_mem, 2 * d), F32)],
        compiler_params=_params(2),
    )(q, kv, do)


def _my_index():
    return 4 * lax.axis_index("x") + 2 * lax.axis_index("y") + lax.axis_index("c")


def _peers():
    x, y, c = lax.axis_index("x"), lax.axis_index("y"), lax.axis_index("c")
    out = []
    for rel in range(1, N_DEV):
        dx, dy, dc = (rel >> 2) & 1, (rel >> 1) & 1, rel & 1
        px, py, pc = x ^ dx, y ^ dy, c ^ dc
        out.append(((px, py, pc), 4 * px + 2 * py + pc))
    return out


class _Exchange:
    def __init__(self, arrays, scatter):
        self.arrays, self.scatter, self.n = list(arrays), scatter, len(arrays)
        any_spec = pl.BlockSpec(memory_space=pl.ANY)
        self.in_specs = [any_spec] * self.n
        self.out_specs = [any_spec] * self.n
        self.out_shape = [jax.ShapeDtypeStruct((N_DEV,) + tuple(a.shape[-2:]), a.dtype) for a in self.arrays]
        n_peer = N_DEV - 1
        self.scratch = [pltpu.SemaphoreType.DMA((self.n, n_peer)), pltpu.SemaphoreType.DMA((self.n, n_peer)),
                        pltpu.SemaphoreType.DMA((self.n,))]

    def _copies(self, srcs, dsts, sems, arriving):
        send_sems, recv_sems, local_sems = sems
        me = _my_index()
        local, remote = [], []
        for w in range(self.n):
            if not arriving:
                local.append(pltpu.make_async_copy(srcs[w].at[me] if self.scatter else srcs[w], dsts[w].at[me], local_sems.at[w]))
            for rel, (pos, idx) in enumerate(_peers()):
                remote.append(pltpu.make_async_remote_copy(
                    src_ref=srcs[w].at[idx] if self.scatter else srcs[w], dst_ref=dsts[w].at[idx if arriving else me],
                    send_sem=send_sems.at[w, rel], recv_sem=recv_sems.at[w, rel], device_id=pos, device_id_type=MESH))
        return local, remote

    def start(self, srcs, dsts, sems):
        local, sends = self._copies(srcs, dsts, sems, arriving=False)
        for cp in local + sends:
            cp.start()

    def wait(self, srcs, dsts, sems):
        for cp in self._copies(srcs, dsts, sems, arriving=True)[1]:
            cp.wait_recv()
        local, sends = self._copies(srcs, dsts, sems, arriving=False)
        for cp in sends:
            cp.wait_send()
        for cp in local:
            cp.wait()


def _two_level_gather(x_ref, slots_ref, send_sems, recv_sems, local_sem):
    x, y, c = lax.axis_index("x"), lax.axis_index("y"), lax.axis_index("c")
    me, sibling = (x, y, c), (x, y, 1 - c)
    chips = [(1 - x, y), (x, 1 - y), (1 - x, 1 - y)]

    def slot(px, py, pc):
        return slots_ref.at[4 * px + 2 * py + pc]

    def copy(k, block, to, src=None):
        return pltpu.make_async_remote_copy(src_ref=slot(*block) if src is None else src, dst_ref=slot(*block),
                                            send_sem=send_sems.at[k], recv_sem=recv_sems.at[k], device_id=to, device_id_type=MESH)

    mine = pltpu.make_async_copy(x_ref, slot(*me), local_sem)
    mine.start()
    first = [copy(0, me, sibling, src=x_ref)] + [copy(1 + j, me, (*chip, c), src=x_ref) for j, chip in enumerate(chips)]
    for cp in first:
        cp.start()
    passed = [copy(4 + j, (*chip, c), sibling) for j, chip in enumerate(chips)]
    for j, chip in enumerate(chips):
        copy(1 + j, (*chip, c), me).wait_recv()
        passed[j].start()
    copy(0, sibling, me).wait_recv()
    for j, chip in enumerate(chips):
        copy(4 + j, (*chip, 1 - c), me).wait_recv()
    for cp in first + passed:
        cp.wait_send()
    mine.wait()


_TWO_LEVEL_SEMS = [pltpu.SemaphoreType.DMA((N_DEV - 1,)), pltpu.SemaphoreType.DMA((N_DEV - 1,)), pltpu.SemaphoreType.DMA(())]


def _gather_two_level(shard, *, name):
    any_spec = pl.BlockSpec(memory_space=pl.ANY)
    return pl.pallas_call(
        _two_level_gather_body(), name=name, in_specs=[any_spec], out_specs=any_spec,
        out_shape=jax.ShapeDtypeStruct((N_DEV,) + shard.shape, shard.dtype), scratch_shapes=_TWO_LEVEL_SEMS,
    )(shard)


def _two_level_gather_body():
    def body(x_ref, out_ref, send_sems, recv_sems, local_sem):
        _two_level_gather(x_ref, out_ref, send_sems, recv_sems, local_sem)
    return body


def _all_reduce_small(part, *, loss_rows, loss_scale):
    rows = part.shape[0]

    def body(p_ref, o_ref, buf, send_sems, recv_sems, local_sem):
        _two_level_gather(p_ref, buf, send_sems, recv_sems, local_sem)
        total = buf[0]
        for dev in range(1, N_DEV):
            total = total + buf[dev]
        o_ref[...] = total
        squares = total[rows - loss_rows:]
        loss = jnp.sum(jnp.sum(squares, axis=0, keepdims=True), axis=-1, keepdims=True) * loss_scale
        o_ref[rows - loss_rows:, :] = jnp.broadcast_to(loss, (loss_rows, LANES))

    vmem = pl.BlockSpec(memory_space=pltpu.VMEM)
    return pl.pallas_call(
        body, name="all_reduce_small", in_specs=[vmem], out_specs=vmem, out_shape=jax.ShapeDtypeStruct(part.shape, F32),
        scratch_shapes=[pltpu.VMEM((N_DEV, rows, LANES), F32)] + _TWO_LEVEL_SEMS,
        compiler_params=pltpu.CompilerParams(has_side_effects=True, vmem_limit_bytes=VMEM_LIMIT),
    )(part)


def _adamw_math(w, g, m, v):
    m_new = ADAM_B1 * m + (1.0 - ADAM_B1) * g
    v_new = ADAM_B2 * v + (1.0 - ADAM_B2) * (g * g)
    m_hat = m_new / (1.0 - ADAM_B1 ** ADAM_STEP)
    v_hat = v_new / (1.0 - ADAM_B2 ** ADAM_STEP)
    delta = -ADAM_LR * (m_hat / (jnp.sqrt(v_hat) + ADAM_EPS) + ADAM_WD * w)
    return delta, m_new, v_new


def _adamw(parts, w, m, v, *, name, tr=64, parts_transposed=False):
    rows, cols = w.shape
    tr = min(tr, rows)
    while rows % (2 * tr) == 0 and 2 * tr * cols <= 256 * 1024:
        tr *= 2
    assert rows % tr == 0 and (not parts_transposed or tr % LANES == 0)
    stacked = parts.ndim == 3

    def body(p_ref, w_ref, m_ref, v_ref, g_ref, d_ref, mo_ref, vo_ref):
        if stacked:
            g = p_ref[0].astype(F32)
            for dev in range(1, N_DEV):
                g = g + p_ref[dev].astype(F32)
        else:
            g = p_ref[...]
        if parts_transposed:
            g = g.T
        delta, m_new, v_new = _adamw_math(w_ref[...], g, m_ref[...], v_ref[...])
        g_ref[...] = g
        d_ref[...] = delta
        mo_ref[...] = m_new
        vo_ref[...] = v_new

    tile = pl.BlockSpec((tr, cols), lambda i: (i, 0))
    if parts_transposed:
        p_spec = pl.BlockSpec((N_DEV, cols, tr), lambda i: (0, 0, i))
    else:
        p_spec = pl.BlockSpec((N_DEV, tr, cols), lambda i: (0, i, 0)) if stacked else tile
    return pl.pallas_call(
        body, name=name, grid=(rows // tr,), in_specs=[p_spec, tile, tile, tile], out_specs=[tile] * 4,
        out_shape=[jax.ShapeDtypeStruct((rows, cols), F32)] * 4, compiler_params=_params(1),
    )(parts, w, m, v)


_LATER = ("w_out", "w_cq", "w_ckv", "w_co", "w_ff1", "w_ff2")


def _as_rows(stacked):
    return stacked.reshape(-1, stacked.shape[-1])


def _local_step(x, mem, target, small, shards):
    batch, seq, d = x.shape
    n_mem = mem.shape[1]
    t = batch * seq
    x2, mem2, tgt2 = x.reshape(t, d), mem.reshape(batch * n_mem, d), target.reshape(t, d)
    g_mix, g_cross, g_mem, g_ffn, g_final = (small[k] for k in ("norm_mix_g", "norm_cross_g", "norm_mem_g", "norm_ffn_g", "norm_final_g"))
    gv, hg, w_sp, b_sp_t = small["gm_v_norm_g"], small["head_norm_g"], small["w_spatial"], small["b_spatial_t"]

    win_t = _as_rows(_gather_two_level(shards["w_in"], name="gather_w_in"))
    proj, xn = _norm_mm(x2, g_mix, win_t, mode="nt", name="proj_fwd", tm=512, tn=win_t.shape[0])
    merged_a = _gmlp_fwd(proj, w_sp, b_sp_t, gv, hg)
    merged, sb_raw, sb_weights, sb_log_rest, gathered = _sb_fwd(proj, merged_a, hg, batch=batch, seq=seq,
                                                                exchange=_Exchange([shards[n] for n in _LATER], scatter=False))
    wout, wcq, wckv_t, wco, wff1_t, wff2 = (_as_rows(g) for g in gathered)
    h1, qx, hn1 = _mm(merged, wout, mode="nn", out_dtype=F32, name="mix_out_fwd_xq_fwd", tm=512, row_parts=2, epi=_epi_residual_then_norm_mm("nn"),
                      epi_ins=(x2,), vec_ins=(g_cross,), whole_ins=(wcq,), more_outs=((wcq.shape[1], BF16), (d, BF16)))
    kvx, memn = _norm_mm(mem2, g_mem, wckv_t, mode="nt", name="xkv_fwd", tm=512, tn=wckv_t.shape[0])
    o = _xattn_fwd(qx, kvx, batch=batch, seq=seq, n_mem=n_mem)
    h2, fpre, hn2 = _mm(o, wco, mode="nn", out_dtype=F32, name="xo_fwd_ff1_fwd", tm=512, row_parts=2, epi=_epi_residual_then_norm_mm("nt"),
                        epi_ins=(h1,), vec_ins=(g_ffn,), whole_ins=(wff1_t,), more_outs=((wff1_t.shape[0], BF16), (d, BF16)))
    dh3, final_rows = _mm(fpre, wff2, mode="nn", out_dtype=F32, name="ff2_fwd_loss", tm=512, row_parts=2, tk=wff2.shape[0], a_fn=_relu2,
                          epi=_epi_loss, epi_ins=(h2, tgt2), vec_ins=(g_final,), aux=2)
    dg_final, sq_err = final_rows[0:1], final_rows[1:2]

    dpre = _mm(dh3, wff2, mode="nt", out_dtype=BF16, name="ff2_bwd_x", tm=512, tn=wff2.shape[0], epi=_epi_relu2_grad,
               epi_ins=(fpre,))
    chunk = wff2.shape[0] // N_DEV
    d_wff2_t = _mm(dh3, fpre, mode="tn", out_dtype=BF16, name="ff2_bwd_w", tn=2048, b_fn=_relu2, col_chunk=chunk)
    d_wff1 = _mm(hn2, dpre, mode="tn", out_dtype=BF16, name="ff1_bwd_w", tn=2048, col_chunk=chunk)
    dh2, do, dg_ffn = _mm(dpre, wff1_t, mode="nn", out_dtype=F32, name="ff1_bwd_x_xo_bwd_x", tm=512, row_parts=2, tk=wff1_t.shape[0],
                          epi=_epi_rms_bwd_then_mm, epi_ins=(h2, dh3), vec_ins=(g_ffn,), whole_ins=(wco,),
                          more_outs=((wco.shape[0], BF16),), aux=True)
    d_wco = _mm(o, dh2, mode="tn", out_dtype=BF16, name="xo_bwd_w")
    dqx, dkvx = _xattn_bwd(qx, kvx, do, batch=batch, seq=seq, n_mem=n_mem)
    d_wcq = _mm(hn1, dqx, mode="tn", out_dtype=BF16, name="xq_bwd_w")
    dh1, dmerged, dg_cross = _mm(dqx, wcq, mode="nt", out_dtype=F32, name="xq_bwd_x_mix_out_bwd_x", tm=512, row_parts=2,
                                 epi=_epi_rms_bwd_then_mm, epi_ins=(h1, dh2), vec_ins=(g_cross,), whole_ins=(wout,),
                                 more_outs=((wout.shape[0], BF16),), aux=True)
    d_wckv_t = _mm(dkvx, memn, mode="tn", out_dtype=BF16, name="xkv_bwd_w")
    _, dg_mem = _mm(dkvx, wckv_t, mode="nn", out_dtype=BF16, name="xkv_bwd_x", tm=512, epi=_epi_rms_gain_only,
                    epi_ins=(mem2,), vec_ins=(g_mem,), aux=True)
    d_wout = _mm(merged, dh1, mode="tn", out_dtype=BF16, name="mix_out_bwd_w")
    dp_a, d_wsp, d_bsp_t, d_gv, d_hg_a = _gmlp_bwd(proj, dmerged, w_sp, b_sp_t, gv, hg)
    d_later = {"w_out": d_wout, "w_cq": d_wcq, "w_ckv": d_wckv_t, "w_co": d_wco, "w_ff1": d_wff1, "w_ff2": d_wff2_t}
    scatter = _Exchange([g if g.ndim == 3 else g.reshape(N_DEV, -1, d) for g in (d_later[n] for n in _LATER)], scatter=True)
    dq, dk, dv, d_hg_b, received = _sb_bwd(proj, sb_raw, sb_weights, sb_log_rest, dmerged, hg, batch=batch, seq=seq, exchange=scatter)
    dproj = dp_a
    for part, at in ((dq, 2 * GM_WIDTH), (dk, 3 * GM_WIDTH), (dv, 4 * GM_WIDTH)):
        dproj = lax.dynamic_update_slice(dproj, part, (0, at))
    d_win_t = _mm(xn, dproj, mode="tn", out_dtype=BF16, name="proj_bwd_w", tn=dproj.shape[1] // 2).T
    dx, dg_mix, d_win_received = _mm(dproj, win_t, mode="nn", out_dtype=F32, name="proj_bwd_x", tm=512, row_parts=2, tk=win_t.shape[0],
                                     epi=_epi_rms_bwd, epi_ins=(x2, dh1), vec_ins=(g_mix,), aux=True,
                                     exchange=_Exchange([d_win_t.reshape(N_DEV, -1, d)], scatter=True))

    d_small = {"norm_mix_g": dg_mix, "gm_v_norm_g": d_gv, "w_spatial": d_wsp, "b_spatial_t": d_bsp_t, "head_norm_g": jnp.concatenate([d_hg_a, d_hg_b], axis=1),
               "norm_cross_g": dg_cross, "norm_mem_g": dg_mem, "norm_ffn_g": dg_ffn, "norm_final_g": dg_final}
    d_big = dict(zip(_LATER, received))
    d_big["w_in"] = d_win_received
    return sq_err, dx.reshape(batch, seq, d), d_small, d_big


_BIG = ("w_in", "w_out", "w_cq", "w_ckv", "w_co", "w_ff1", "w_ff2")
_GATHERED_TRANSPOSED = ("w_in", "w_ckv", "w_ff1")
_UPDATED_TRANSPOSED = ("w_in", "w_ckv", "w_ff2")
_SMALL = ("norm_mix_g", "gm_v_norm_g", "w_spatial", "b_spatial", "head_norm_g", "norm_cross_g", "norm_mem_g", "norm_ffn_g", "norm_final_g")
_NAMES = ("norm_mix_g", "w_in", "gm_v_norm_g", "w_spatial", "b_spatial", "head_norm_g", "w_out", "norm_cross_g", "norm_mem_g",
          "w_cq", "w_ckv", "w_co", "norm_ffn_g", "w_ff1", "w_ff2", "norm_final_g")


def _rows_of(a):
    r = a.reshape(-1, LANES)
    pad = (-r.shape[0]) % 8
    return jnp.pad(r, ((0, pad), (0, 0))) if pad else r


def _shard2d(a, transposed):
    return a[0].T if transposed else a[0]


def kernel(x, mem, norm_mix_g, w_in, gm_v_norm_g, w_spatial, b_spatial, head_norm_g, w_out, norm_cross_g, norm_mem_g, w_cq, w_ckv, w_co, norm_ffn_g, w_ff1, w_ff2, norm_final_g, loss_target, m_norm_mix_g, m_w_in, m_gm_v_norm_g, m_w_spatial, m_b_spatial, m_head_norm_g, m_w_out, m_norm_cross_g, m_norm_mem_g, m_w_cq, m_w_ckv, m_w_co, m_norm_ffn_g, m_w_ff1, m_w_ff2, m_norm_final_g, v_norm_mix_g, v_w_in, v_gm_v_norm_g, v_w_spatial, v_b_spatial, v_head_norm_g, v_w_out, v_norm_cross_g, v_norm_mem_g, v_w_cq, v_w_ckv, v_w_co, v_norm_ffn_g, v_w_ff1, v_w_ff2, v_norm_final_g):
    weights = dict(norm_mix_g=norm_mix_g, w_in=w_in, gm_v_norm_g=gm_v_norm_g, w_spatial=w_spatial, b_spatial=b_spatial,
                   head_norm_g=head_norm_g, w_out=w_out, norm_cross_g=norm_cross_g, norm_mem_g=norm_mem_g, w_cq=w_cq, w_ckv=w_ckv,
                   w_co=w_co, norm_ffn_g=norm_ffn_g, w_ff1=w_ff1, w_ff2=w_ff2, norm_final_g=norm_final_g)
    mom1 = dict(norm_mix_g=m_norm_mix_g, w_in=m_w_in, gm_v_norm_g=m_gm_v_norm_g, w_spatial=m_w_spatial, b_spatial=m_b_spatial,
                head_norm_g=m_head_norm_g, w_out=m_w_out, norm_cross_g=m_norm_cross_g, norm_mem_g=m_norm_mem_g, w_cq=m_w_cq,
                w_ckv=m_w_ckv, w_co=m_w_co, norm_ffn_g=m_norm_ffn_g, w_ff1=m_w_ff1, w_ff2=m_w_ff2, norm_final_g=m_norm_final_g)
    mom2 = dict(norm_mix_g=v_norm_mix_g, w_in=v_w_in, gm_v_norm_g=v_gm_v_norm_g, w_spatial=v_w_spatial, b_spatial=v_b_spatial,
                head_norm_g=v_head_norm_g, w_out=v_w_out, norm_cross_g=v_norm_cross_g, norm_mem_g=v_norm_mem_g, w_cq=v_w_cq,
                w_ckv=v_w_ckv, w_co=v_w_co, norm_ffn_g=v_norm_ffn_g, w_ff1=v_w_ff1, w_ff2=v_w_ff2, norm_final_g=v_norm_final_g)

    shards = {n: _shard2d(weights[n], n in _GATHERED_TRANSPOSED).astype(BF16) for n in _BIG}
    small = {n: weights[n].reshape(1, -1) for n in _SMALL if n not in ("w_spatial", "b_spatial")}
    small["w_spatial"] = w_spatial[0]
    small["b_spatial_t"] = b_spatial[0].T
    sq_err, grad_x, d_small, d_big = _local_step(x, mem, loss_target, small, shards)

    d_small["b_spatial"] = d_small.pop("b_spatial_t").T
    sq_rows = _rows_of(sq_err)
    packed = jnp.concatenate([_rows_of(d_small[n]) for n in _SMALL] + [sq_rows], axis=0)
    summed = _all_reduce_small(packed, loss_rows=sq_rows.shape[0], loss_scale=0.5 / x.shape[-1])

    grads, deltas, new_m, new_v = {}, {}, {}, {}
    for n in _BIG:
        outs = _adamw(d_big[n], weights[n][0], mom1[n][0], mom2[n][0], name="adamw_" + n, parts_transposed=n in _UPDATED_TRANSPOSED)
        grads[n], deltas[n], new_m[n], new_v[n] = (o[None] for o in outs)
    pack = lambda src: jnp.concatenate([_rows_of(src[n]) for n in _SMALL], axis=0)
    n_small_rows = sum(_rows_of(weights[n]).shape[0] for n in _SMALL)
    outs = _adamw(summed[:n_small_rows], pack(weights), pack(mom1), pack(mom2), name="adamw_small", tr=n_small_rows)
    at = 0
    for n in _SMALL:
        used = weights[n].size // LANES
        for dst, o in zip((grads, deltas, new_m, new_v), outs):
            dst[n] = o[at:at + used].reshape(weights[n].shape)
        at += _rows_of(weights[n]).shape[0]
    loss = summed[n_small_rows, 0]
    return (loss, grad_x, *[grads[n] for n in _NAMES], *[deltas[n] for n in _NAMES], *[new_m[n] for n in _NAMES],
            *[new_v[n] for n in _NAMES])
```

```python
import math

import jax
import jax.numpy as jnp
from jax import lax
from jax.experimental import pallas as pl
from jax.experimental.pallas import tpu as pltpu

F32 = jnp.float32
BF16 = jnp.bfloat16
EPS = 1e-6
N_DEV = 8
LANES = 128
CHUNK = 128
GM_GROUPS = 4
GM_WIDTH = 512
SB_PAIRS = 4
SB_HEAD_DIM = 64
SB_SCALE = 0.125
SB_TILE = 128
SB_BLOCK = 512
X_HEADS = 4
X_HEAD_DIM = 256
X_SCALE = 1.0 / 16.0
VMEM_LIMIT = 56 * 1024 * 1024
ADAM_LR, ADAM_B1, ADAM_B2, ADAM_EPS, ADAM_WD, ADAM_STEP = 0.001, 0.9, 0.999, 1e-08, 0.01, 10
MESH = pl.DeviceIdType.MESH


def _params(n_axes):
    return pltpu.CompilerParams(dimension_semantics=("arbitrary",) * n_axes, vmem_limit_bytes=VMEM_LIMIT)


def _dot(a, b, dims):
    return lax.dot_general(a, b, (dims, ((), ())), preferred_element_type=F32)


def _nn(a, b):
    return _dot(a, b, ((1,), (0,)))


def _nt(a, b):
    return _dot(a, b, ((1,), (1,)))


def _tn(a, b):
    return _dot(a, b, ((0,), (0,)))


_MODES = {"nn": _nn, "nt": _nt, "tn": _tn}


def _rstd(x):
    return lax.rsqrt(jnp.mean(x * x, axis=-1, keepdims=True) + EPS)


def _gelu(x):
    c = math.sqrt(2.0 / math.pi)
    t = jnp.tanh(c * (x + 0.044715 * x * x * x))
    return 0.5 * x * (1.0 + t)


def _gelu_and_grad(x):
    c = math.sqrt(2.0 / math.pi)
    t = jnp.tanh(c * (x + 0.044715 * x * x * x))
    half = 0.5 * (1.0 + t)
    return x * half, half + 0.5 * x * (1.0 - t * t) * c * (1.0 + 3 * 0.044715 * x * x)


def _split_bf16(x):
    hi = x.astype(BF16)
    lo = (x - hi.astype(F32)).astype(BF16)
    return hi, lo


def _mm(a, b, *, mode, out_dtype, name, tm=1024, tn=1024, tk=1024, a_fn=None, b_fn=None, epi=None, epi_ins=(), vec_ins=(),
        whole_ins=(), more_outs=(), aux=False, col_chunk=None, row_parts=1, exchange=None):
    if mode == "nn":
        (m, k), (k2, n) = a.shape, b.shape
    elif mode == "nt":
        (m, k), (n, k2) = a.shape, b.shape
    else:
        (k, m), (k2, n) = a.shape, b.shape
    assert k == k2, (a.shape, b.shape, mode)
    tm, tn, tk = min(tm, m), min(tn, n), min(tk, k)
    assert m % tm == 0 and n % tn == 0 and k % tk == 0, (m, n, k, tm, tn, tk)
    n_m, n_n, n_k = m // tm, n // tn, k // tk
    assert not (aux or more_outs) or n_n == 1
    assert row_parts == 1 or (n_k == 1 and mode != "tn" and epi is not None and tm % (8 * row_parts) == 0)
    dot = _MODES[mode]
    n_epi, n_vec, n_whole, n_more = len(epi_ins), len(vec_ins), len(whole_ins), len(more_outs)

    def body(*refs):
        ins, outs, scratch, x_refs = _riding(exchange, refs, 2 + n_epi + n_vec + n_whole, 1 + n_more + (1 if aux else 0),
                                             1 if n_k > 1 else 0)
        a_ref, b_ref, epi_refs = ins[0], ins[1], ins[2:]
        o_ref, more_refs = outs[0], outs[1:1 + n_more]
        aux_ref = outs[1 + n_more] if aux else None
        acc_ref = scratch[0] if n_k > 1 else None
        i, j, kk = pl.program_id(0), pl.program_id(1), pl.program_id(2)
        ride_done = _ride(exchange, x_refs, (i == 0) & (j == 0) & (kk == 0), (i == n_m - 1) & (j == n_n - 1) & (kk == n_k - 1))
        def product(rows=slice(None)):
            av, bv = a_ref[...] if mode == "tn" else a_ref[rows, :], b_ref[...]
            if a_fn is not None:
                av = a_fn(av)
            if b_fn is not None:
                bv = b_fn(bv)
            return dot(av.astype(BF16), bv.astype(BF16))

        def finish(accs, parts=(slice(None),)):
            if col_chunk is not None:
                for ch in range(tn // col_chunk):
                    o_ref[ch] = accs[0][:, ch * col_chunk:(ch + 1) * col_chunk].astype(out_dtype)
                return
            if epi is None:
                o_ref[...] = accs[0].astype(out_dtype)
                return
            row_sums = []
            for acc, rows in zip(accs, parts):
                res = epi(acc, *[r[rows, :] for r in epi_refs[:n_epi]], *[r[...] for r in epi_refs[n_epi:]])
                if n_more:
                    for more_ref, value in zip(more_refs, res[1:1 + n_more]):
                        more_ref[rows, :] = value.astype(more_ref.dtype)
                    res = (res[0],) + tuple(res[1 + n_more:]) if aux else res[0]
                if aux:
                    res, sums = res[0], res[1:]
                    row_sums.append(sums[0] if len(sums) == 1 else jnp.concatenate(sums, axis=0))
                o_ref[rows, :] = res.astype(out_dtype)
            if aux:
                total = sum(row_sums)

                @pl.when(i == 0)
                def _():
                    aux_ref[...] = total

                @pl.when(i != 0)
                def _():
                    aux_ref[...] += total

        if n_k == 1:
            parts = [slice(h * (tm // row_parts), (h + 1) * (tm // row_parts)) for h in range(row_parts)]
            finish([product(rows) for rows in parts], parts)
        else:
            @pl.when(kk == 0)
            def _():
                acc_ref[...] = product()

            @pl.when(kk != 0)
            def _():
                acc_ref[...] += product()

            @pl.when(kk == n_k - 1)
            def _():
                finish([acc_ref[...]])

        ride_done()

    if mode == "tn":
        a_spec = pl.BlockSpec((tk, tm), lambda i, j, kk: (kk, i))
    else:
        a_spec = pl.BlockSpec((tm, tk), lambda i, j, kk: (i, kk))
    if mode == "nt":
        b_spec = pl.BlockSpec((tn, tk), lambda i, j, kk: (j, kk))
    else:
        b_spec = pl.BlockSpec((tk, tn), lambda i, j, kk: (kk, j))
    tile_spec = pl.BlockSpec((tm, tn), lambda i, j, kk: (i, j))
    row_spec = pl.BlockSpec((1, tn), lambda i, j, kk: (0, j))
    out_shape = [jax.ShapeDtypeStruct((m, n), out_dtype)]
    out_specs = [tile_spec]
    if col_chunk is not None:
        assert epi is None and not aux and tn % col_chunk == 0
        out_shape = [jax.ShapeDtypeStruct((n // col_chunk, m, col_chunk), out_dtype)]
        out_specs = [pl.BlockSpec((tn // col_chunk, tm, col_chunk), lambda i, j, kk: (j, i, 0))]
    for columns, dtype in more_outs:
        out_shape.append(jax.ShapeDtypeStruct((m, columns), dtype))
        out_specs.append(pl.BlockSpec((tm, columns), lambda i, j, kk: (i, 0)))
    if aux:
        out_shape.append(jax.ShapeDtypeStruct((int(aux), n), F32))
        out_specs.append(pl.BlockSpec((int(aux), tn), lambda i, j, kk: (0, j)))
    whole_specs = [pl.BlockSpec(w.shape, lambda i, j, kk, nd=w.ndim: (0,) * nd) for w in whole_ins]
    x_in, x_out, x_shape, x_scratch, x_arrays = _riding_specs(exchange)
    res = pl.pallas_call(
        body, name=name, grid=(n_m, n_n, n_k),
        in_specs=[a_spec, b_spec] + [tile_spec] * n_epi + [row_spec] * n_vec + whole_specs + x_in,
        out_specs=out_specs + x_out, out_shape=out_shape + x_shape,
        scratch_shapes=([pltpu.VMEM((tm, tn), F32)] if n_k > 1 else []) + x_scratch,
        compiler_params=_params(3),
    )(a, b, *epi_ins, *vec_ins, *whole_ins, *x_arrays)
    if exchange is not None or more_outs:
        return tuple(res)
    return res if aux else res[0]


def _norm_mm(x, g, w, *, mode, name, tm=512, row_parts=2):
    m, d = x.shape
    n = w.shape[0] if mode == "nt" else w.shape[1]
    tm = min(tm, m)
    assert m % tm == 0 and tm % (16 * row_parts) == 0
    dot = _MODES[mode]

    def body(x_ref, g_ref, w_ref, o_ref, xn_ref):
        parts = [slice(h * (tm // row_parts), (h + 1) * (tm // row_parts)) for h in range(row_parts)]
        normed = []
        for rows in parts:
            xv = x_ref[rows, :]
            normed.append((xv * _rstd(xv) * g_ref[...]).astype(BF16))
        for rows, xn in zip(parts, normed):
            xn_ref[rows, :] = xn
            o_ref[rows, :] = dot(xn, w_ref[...]).astype(BF16)

    return pl.pallas_call(
        body, name=name, grid=(m // tm,),
        in_specs=[pl.BlockSpec((tm, d), lambda i: (i, 0)), pl.BlockSpec((1, d), lambda i: (0, 0)), pl.BlockSpec(w.shape, lambda i: (0, 0))],
        out_specs=[pl.BlockSpec((tm, n), lambda i: (i, 0)), pl.BlockSpec((tm, d), lambda i: (i, 0))],
        out_shape=[jax.ShapeDtypeStruct((m, n), BF16), jax.ShapeDtypeStruct((m, d), BF16)],
        compiler_params=_params(1),
    )(x, g, w)


def _epi_relu2_grad(acc, pre):
    return acc * (2.0 * jnp.maximum(pre.astype(F32), 0.0))


def _relu2(pre):
    r = jnp.maximum(pre.astype(F32), 0.0)
    return r * r


def _epi_rms_bwd(acc, h, dres, g):
    r = _rstd(h)
    xh = h * r
    dxh = acc * g
    dh = dres + r * (dxh - xh * jnp.mean(dxh * xh, axis=-1, keepdims=True))
    return dh, jnp.sum(acc * xh, axis=0, keepdims=True)


def _epi_residual_then_norm_mm(mode):
    dot = _MODES[mode]

    def epi(acc, res, g, w):
        h = res + acc
        hn = (h * _rstd(h) * g).astype(BF16)
        return h, dot(hn, w), hn

    return epi


def _epi_rms_bwd_then_mm(acc, h, dres, g, w):
    dh, row = _epi_rms_bwd(acc, h, dres, g)
    return dh, _nt(dh.astype(BF16), w), row


def _epi_loss(acc, h_in, target, g):
    h = h_in + acc
    r = _rstd(h)
    xh = h * r
    err = xh * g - target
    dy = err * (1.0 / h.shape[-1])
    dxh = dy * g
    dh = r * (dxh - xh * jnp.mean(dxh * xh, axis=-1, keepdims=True))
    return dh, jnp.sum(dy * xh, axis=0, keepdims=True), jnp.sum(err * err, axis=0, keepdims=True)


def _epi_rms_gain_only(acc, h, g):
    return acc, jnp.sum(acc * (h * _rstd(h)), axis=0, keepdims=True)


def _tril(n):
    row = lax.broadcasted_iota(jnp.int32, (n, n), 0)
    col = lax.broadcasted_iota(jnp.int32, (n, n), 1)
    return col <= row


def _gmlp_fwd(proj, w_sp, b_sp_t, gv, hg, *, rows=512):
    t = proj.shape[0]
    rows = min(rows, t)
    n_c = rows // CHUNK

    def body(u_ref, v_ref, w_ref, bt_ref, gv_ref, hg_ref, m_ref):
        keep = _tril(CHUNK)
        for g in range(GM_GROUPS):
            cols = slice(g * LANES, (g + 1) * LANES)
            wg = jnp.where(keep, w_ref[g], 0.0).astype(BF16)
            u = _gelu(u_ref[:, cols].astype(F32))
            v = _gelu(v_ref[:, cols].astype(F32))
            vn = (v * _rstd(v) * gv_ref[:, cols]).astype(BF16)
            bias = bt_ref[:, g:g + 1]
            chunks = [slice(c * CHUNK, (c + 1) * CHUNK) for c in range(n_c)]
            gated = [u[rs] * (_nn(wg, vn[rs]) + bias) for rs in chunks]
            for rs, a in zip(chunks, gated):
                m_ref[rs, cols] = (a * _rstd(a) * hg_ref[:, cols]).astype(BF16)

    full = lambda shape: pl.BlockSpec(shape, lambda i: (0,) * len(shape))
    return pl.pallas_call(
        body, name="gmlp_fwd", grid=(t // rows,),
        in_specs=[pl.BlockSpec((rows, GM_WIDTH), lambda i: (i, 0)), pl.BlockSpec((rows, GM_WIDTH), lambda i: (i, 1)),
                  full((GM_GROUPS, CHUNK, CHUNK)), full((CHUNK, GM_GROUPS)), full((1, GM_WIDTH)), full((1, GM_WIDTH))],
        out_specs=pl.BlockSpec((rows, GM_WIDTH), lambda i: (i, 0)),
        out_shape=jax.ShapeDtypeStruct((t, 2 * GM_WIDTH), BF16),
        compiler_params=_params(1),
    )(proj, proj, w_sp, b_sp_t, gv, hg)


def _gmlp_bwd(proj, dmerged, w_sp, b_sp_t, gv, hg, *, rows=512):
    t = proj.shape[0]
    rows = min(rows, t)
    n_c = rows // CHUNK
    n_steps = t // rows

    def body(u_ref, v_ref, dm_ref, w_ref, bt_ref, gv_ref, hg_ref, dp_ref, dw_ref, dbt_ref, dgv_ref, dhg_ref, db_acc):
        step = pl.program_id(0)
        keep = _tril(CHUNK)

        @pl.when(step == 0)
        def _():
            dw_ref[...] = jnp.zeros_like(dw_ref)
            db_acc[...] = jnp.zeros_like(db_acc)
            dgv_ref[...] = jnp.zeros_like(dgv_ref)
            dhg_ref[...] = jnp.zeros_like(dhg_ref)

        for g in range(GM_GROUPS):
            cols = slice(g * LANES, (g + 1) * LANES)
            wg = jnp.where(keep, w_ref[g], 0.0).astype(BF16)
            u, u_slope = _gelu_and_grad(u_ref[:, cols].astype(F32))
            v, v_slope = _gelu_and_grad(v_ref[:, cols].astype(F32))
            r = _rstd(v)
            xh = v * r
            gvg = gv_ref[:, cols]
            hgg = hg_ref[:, cols]
            vn = (xh * gvg).astype(BF16)
            bias = bt_ref[:, g:g + 1]
            dm = dm_ref[:, cols].astype(F32)
            chunks = [slice(c * CHUNK, (c + 1) * CHUNK) for c in range(n_c)]
            mixed = [_nn(wg, vn[rs]) + bias for rs in chunks]
            gated = [u[rs] * mx for rs, mx in zip(chunks, mixed)]
            scale = [_rstd(a) for a in gated]
            normed = [a * ra for a, ra in zip(gated, scale)]
            d_normed = [dm[rs] * hgg for rs in chunks]
            d_gated = [ra * (dan - an * jnp.mean(dan * an, axis=-1, keepdims=True)) for ra, dan, an in zip(scale, d_normed, normed)]
            d_mixed = [da * u[rs] for da, rs in zip(d_gated, chunks)]
            d_mixed_b = [dmx.astype(BF16) for dmx in d_mixed]
            du = jnp.concatenate([da * mx for da, mx in zip(d_gated, mixed)], axis=0)
            dvn = jnp.concatenate([_tn(wg, dmb) for dmb in d_mixed_b], axis=0)
            dw_ref[g] += sum(_nt(dmb, vn[rs]) for dmb, rs in zip(d_mixed_b, chunks))
            db_acc[g] += sum(d_mixed)
            dhg_ref[:, cols] += sum(jnp.sum(dm[rs] * an, axis=0, keepdims=True) for rs, an in zip(chunks, normed))
            dgv_ref[:, cols] += jnp.sum(dvn * xh, axis=0, keepdims=True)
            dxh = dvn * gvg
            dv = r * (dxh - xh * jnp.mean(dxh * xh, axis=-1, keepdims=True))
            dp_ref[:, cols] = (du * u_slope).astype(BF16)
            dp_ref[:, GM_WIDTH + g * LANES:GM_WIDTH + (g + 1) * LANES] = (dv * v_slope).astype(BF16)

        @pl.when(step == n_steps - 1)
        def _():
            for g in range(GM_GROUPS):
                dw_ref[g] = jnp.where(keep, dw_ref[g], 0.0)
                dbt_ref[:, g:g + 1] = jnp.sum(db_acc[g], axis=-1, keepdims=True)

    full = lambda shape: pl.BlockSpec(shape, lambda i: (0,) * len(shape))
    return pl.pallas_call(
        body, name="gmlp_bwd", grid=(n_steps,),
        in_specs=[pl.BlockSpec((rows, GM_WIDTH), lambda i: (i, 0)), pl.BlockSpec((rows, GM_WIDTH), lambda i: (i, 1)),
                  pl.BlockSpec((rows, GM_WIDTH), lambda i: (i, 0)),
                  full((GM_GROUPS, CHUNK, CHUNK)), full((CHUNK, GM_GROUPS)), full((1, GM_WIDTH)), full((1, GM_WIDTH))],
        out_specs=[pl.BlockSpec((rows, 2 * GM_WIDTH), lambda i: (i, 0)), full((GM_GROUPS, CHUNK, CHUNK)),
                   full((CHUNK, GM_GROUPS)), full((1, GM_WIDTH)), full((1, GM_WIDTH))],
        out_shape=[jax.ShapeDtypeStruct((t, proj.shape[1]), BF16), jax.ShapeDtypeStruct((GM_GROUPS, CHUNK, CHUNK), F32),
                   jax.ShapeDtypeStruct((CHUNK, GM_GROUPS), F32), jax.ShapeDtypeStruct((1, GM_WIDTH), F32),
                   jax.ShapeDtypeStruct((1, GM_WIDTH), F32)],
        scratch_shapes=[pltpu.VMEM((GM_GROUPS, CHUNK, LANES), F32)],
        compiler_params=_params(1),
    )(proj, proj, dmerged, w_sp, b_sp_t, gv, hg)


def _sb_logits(z, strict):
    ls = jnp.minimum(z, 0.0) - jnp.log(1.0 + jnp.exp(-jnp.abs(z)))
    l1m = ls - z
    if strict is not None:
        l1m = jnp.where(strict, l1m, 0.0)
    return ls, l1m


def _tri_sums(x, tri):
    hi, lo = _split_bf16(x)
    return _nn(jnp.concatenate([hi, lo], axis=1), jnp.concatenate([tri, tri], axis=0))


def _sb_weights(ls, in_tile, right, strict):
    a = jnp.exp(ls + in_tile + right)
    if strict is not None:
        a = jnp.where(strict, a, 0.0)
    return a


def _sb_masks(q_rows):
    row = lax.broadcasted_iota(jnp.int32, (SB_TILE, SB_TILE), 0)
    col = lax.broadcasted_iota(jnp.int32, (SB_TILE, SB_TILE), 1)
    lane = lax.broadcasted_iota(jnp.int32, (q_rows, LANES), 1)
    return row, col, lane < SB_HEAD_DIM


def _stack_heads(x, first):
    zero = jnp.zeros_like(x)
    return jnp.concatenate([jnp.where(first, x, zero), jnp.where(first, zero, x)], axis=0)


def _stack_heads_t(x_t):
    first_t = lax.broadcasted_iota(jnp.int32, x_t.shape, 0) < SB_HEAD_DIM
    zero = jnp.zeros_like(x_t)
    return jnp.concatenate([jnp.where(first_t, x_t, zero), jnp.where(first_t, zero, x_t)], axis=1).astype(BF16)


def _unstack_heads(x2, first):
    half = x2.shape[0] // 2
    return jnp.where(first, x2[:half], x2[half:])


def _live_rows(x, s, q_rows):
    if s == 0:
        return x
    return jnp.concatenate([x[s * SB_TILE:q_rows], x[q_rows + s * SB_TILE:]], axis=0)


def _spread_rows(x, s, q_rows):
    if s == 0:
        return x
    half = q_rows - s * SB_TILE
    zero = jnp.zeros((s * SB_TILE,) + x.shape[1:], x.dtype)
    return jnp.concatenate([zero, x[:half], zero, x[half:]], axis=0)


def _stacked_col_minus_row(q_rows):
    row = lax.broadcasted_iota(jnp.int32, (2 * q_rows, SB_TILE), 0)
    col = lax.broadcasted_iota(jnp.int32, (2 * q_rows, SB_TILE), 1)
    return col - (row & (q_rows - 1))


def _head_mean(x, first):
    s0 = jnp.sum(jnp.where(first, x, 0.0), axis=-1, keepdims=True)
    s1 = jnp.sum(jnp.where(first, 0.0, x), axis=-1, keepdims=True)
    return jnp.where(first, s0, s1) * (1.0 / SB_HEAD_DIM)


def _riding(exchange, refs, n_in, n_out, n_scratch):
    n_x = exchange.n if exchange is not None else 0
    ins, rest = refs[:n_in], refs[n_in:]
    x_src, rest = rest[:n_x], rest[n_x:]
    outs, rest = rest[:n_out], rest[n_out:]
    x_dst, rest = rest[:n_x], rest[n_x:]
    return ins, outs, rest[:n_scratch], (x_src, x_dst, rest[n_scratch:])


def _riding_specs(exchange):
    if exchange is None:
        return [], [], [], [], []
    return exchange.in_specs, exchange.out_specs, exchange.out_shape, exchange.scratch, exchange.arrays


def _ride(exchange, x_refs, first_step, last_step):
    if exchange is None:
        return lambda: None

    @pl.when(first_step)
    def _():
        exchange.start(*x_refs)

    def finish():
        @pl.when(last_step)
        def _():
            exchange.wait(*x_refs)

    return finish


def _sb_fwd(proj, merged_a, hg, *, batch, seq, exchange=None):
    q0, k0, v0 = 2 * GM_WIDTH // LANES, 2 * GM_WIDTH // LANES + SB_PAIRS, 2 * GM_WIDTH // LANES + 2 * SB_PAIRS
    q_rows = block_keys = min(SB_BLOCK, seq)
    assert seq % block_keys == 0
    n_q, n_sub, n_blocks = seq // q_rows, block_keys // SB_TILE, seq // block_keys

    def body(*refs):
        (q_ref, k_ref, v_ref, hg_ref, _), (m_ref, raw_ref, a_ref, l_ref), _, x_refs = _riding(exchange, refs, 5, 4, 0)
        b, p, i = pl.program_id(0), pl.program_id(1), pl.program_id(2)
        finish = _ride(exchange, x_refs, (b == 0) & (p == 0) & (i == 0), (b == batch - 1) & (p == SB_PAIRS - 1) & (i == n_q - 1))
        row, col, first = _sb_masks(q_rows)
        upper = (row > col).astype(BF16)
        q2 = _stack_heads((q_ref[...].astype(F32) * SB_SCALE).astype(BF16), first)
        diff = _stacked_col_minus_row(q_rows)
        last = i

        def key_tile(jb, s):
            return k_ref[pl.ds(pl.multiple_of((jb * n_sub + s) * SB_TILE, SB_TILE), SB_TILE), :]

        def scores(jb):
            return tuple(_nt(q2, key_tile(jb, s)) for s in range(n_sub))

        def keep_for_backward(ref, jb, s, stacked):
            ref[0, 0, jb, 0, :, s * SB_TILE:(s + 1) * SB_TILE] = stacked

        def weights_of(jb, z, right):
            logits = [_sb_logits(z[s], None) for s in reversed(range(n_sub))][::-1]
            totals = [jnp.sum(l1m, axis=-1, keepdims=True) for _, l1m in logits]
            sums = [_tri_sums(l1m, upper) for _, l1m in reversed(logits)][::-1]
            for s in reversed(range(n_sub)):
                keep_for_backward(a_ref, jb, s, _sb_weights(logits[s][0], sums[s], right, None).astype(BF16))
                keep_for_backward(l_ref, jb, s, logits[s][1].astype(BF16))
                right = right + totals[s]
            return right

        def diagonal_weights():
            keeps = [_live_rows(diff, s, q_rows) < -s * SB_TILE for s in range(n_sub)]
            logits = [_sb_logits(_nt(_live_rows(q2, s, q_rows), key_tile(last, s)), keeps[s]) for s in range(n_sub)]
            totals = [jnp.sum(l1m, axis=-1, keepdims=True) for _, l1m in logits]
            sums = [_tri_sums(l1m, upper) for _, l1m in logits]
            right = jnp.zeros((2 * q_rows, 1), F32)
            for s in reversed(range(n_sub)):
                live = _sb_weights(logits[s][0], sums[s], _live_rows(right, s, q_rows), keeps[s]).astype(BF16)
                keep_for_backward(a_ref, last, s, _spread_rows(live, s, q_rows))
                keep_for_backward(l_ref, last, s, _spread_rows(logits[s][1].astype(BF16), s, q_rows))
                right = right + _spread_rows(totals[s], s, q_rows)
            return right

        def weighted_values(jb):
            return _nn(a_ref[0, 0, jb, 0], v_ref[pl.ds(pl.multiple_of(jb * block_keys, block_keys), block_keys), :])

        right = diagonal_weights()
        z_next = scores(jnp.maximum(last - 1, 0))

        def step(k, carry):
            right, acc, z = carry
            jb = last - k
            acc = acc + weighted_values(jb + 1)
            z_next = scores(jnp.maximum(jb - 1, 0))
            return weights_of(jb, z, right), acc, z_next

        _, acc2, _ = lax.fori_loop(1, last + 1, step, (right, jnp.zeros((2 * q_rows, LANES), F32), z_next))
        acc = _unstack_heads(acc2 + weighted_values(0), first)
        raw_ref[...] = acc
        m_ref[...] = (acc * lax.rsqrt(_head_mean(acc * acc, first) + EPS) * hg_ref[...]).astype(BF16)
        finish()

    t = batch * seq
    blk = lambda c0: pl.BlockSpec((q_rows, LANES), lambda b, p, i: (b * n_q + i, c0 + p))
    kv = lambda c0: pl.BlockSpec((seq, LANES), lambda b, p, i: (b, c0 + p))
    kept = pl.BlockSpec((1, 1, n_blocks, 1, 2 * q_rows, block_keys), lambda b, p, i: (p, b, 0, i, 0, 0))
    kept_shape = jax.ShapeDtypeStruct((SB_PAIRS, batch, n_blocks, n_q, 2 * q_rows, block_keys), BF16)
    x_in, x_out, x_shape, x_scratch, x_arrays = _riding_specs(exchange)
    res = pl.pallas_call(
        body, name="sb_fwd", grid=(batch, SB_PAIRS, n_q),
        in_specs=[blk(q0), kv(k0), kv(v0), pl.BlockSpec((1, LANES), lambda b, p, i: (0, SB_PAIRS + p)),
                  pl.BlockSpec(memory_space=pl.ANY)] + x_in,
        out_specs=[blk(SB_PAIRS), blk(0), kept, kept] + x_out,
        out_shape=[jax.ShapeDtypeStruct((t, 2 * GM_WIDTH), BF16), jax.ShapeDtypeStruct((t, GM_WIDTH), F32), kept_shape, kept_shape] + x_shape,
        scratch_shapes=x_scratch,
        input_output_aliases={4: 0},
        compiler_params=_params(3),
    )(proj, proj, proj, hg, merged_a, *x_arrays)
    return res[0], res[1], res[2], res[3], res[4:]


def _sb_bwd(proj, raw, weights, log_rest, dmerged, hg, *, batch, seq, exchange=None):
    q0, k0, v0 = 2 * GM_WIDTH // LANES, 2 * GM_WIDTH // LANES + SB_PAIRS, 2 * GM_WIDTH // LANES + 2 * SB_PAIRS
    q_rows = block_keys = min(SB_BLOCK, seq)
    assert seq % block_keys == 0
    n_q, n_sub, n_blocks = seq // q_rows, block_keys // SB_TILE, seq // block_keys

    def body(*refs):
        ins, outs, (dk_acc, dv_acc), x_refs = _riding(exchange, refs, 8, 4, 2)
        q_ref, k_ref, v_ref, raw_ref, a_ref, l_ref, dm_ref, hg_ref = ins
        dq_ref, dk_ref, dv_ref, dhg_ref = outs
        p, b, i = pl.program_id(0), pl.program_id(1), pl.program_id(2)
        finish = _ride(exchange, x_refs, (b == 0) & (p == 0) & (i == 0), (b == batch - 1) & (p == SB_PAIRS - 1) & (i == n_q - 1))
        row, col, first = _sb_masks(q_rows)
        lower = (row < col).astype(BF16)

        @pl.when(jnp.logical_and(b == 0, i == 0))
        def _():
            dhg_ref[...] = jnp.zeros_like(dhg_ref)

        @pl.when(i == 0)
        def _():
            dk_acc[...] = jnp.zeros_like(dk_acc)
            dv_acc[...] = jnp.zeros_like(dv_acc)

        raw_v = raw_ref[...]
        dm = dm_ref[...].astype(F32)
        r = lax.rsqrt(_head_mean(raw_v * raw_v, first) + EPS)
        nrm = raw_v * r
        dhg_ref[...] += jnp.sum(dm * nrm, axis=0, keepdims=True)
        dn = dm * hg_ref[...]
        dout = r * (dn - nrm * _head_mean(dn * nrm, first))
        dout2 = _stack_heads(dout.astype(BF16), first)
        q2_t = _stack_heads_t(q_ref[...].astype(F32).T)
        dout2_t = _stack_heads_t(dout.T)
        diff = _stacked_col_minus_row(q_rows)
        last = i

        def kept(ref, jb, cols):
            return ref[0, 0, jb, 0, :, cols]

        def block(jb, carry, diagonal):
            gleft, dq = carry
            live = (lambda x, s: _live_rows(x, s, q_rows)) if diagonal else (lambda x, s: x)
            spread = (lambda x, s: _spread_rows(x, s, q_rows)) if diagonal else (lambda x, s: x)
            tiles = [pl.ds(pl.multiple_of((jb * n_sub + s) * SB_TILE, SB_TILE), SB_TILE) for s in range(n_sub)]
            cols = [slice(s * SB_TILE, (s + 1) * SB_TILE) for s in range(n_sub)]
            gmats = [_nt(live(dout2, s), v_ref[tiles[s], :]) * live(kept(a_ref, jb, cols[s]), s).astype(F32) for s in range(n_sub)]
            prefixes = [_tri_sums(g, lower) for g in gmats]
            dzs = []
            for s in range(n_sub):
                one_minus = jnp.exp(live(kept(l_ref, jb, cols[s]), s).astype(F32))
                dz = (gmats[s] * one_minus - (live(gleft, s) + prefixes[s]) * (1.0 - one_minus)) * SB_SCALE
                gleft = gleft + spread(jnp.sum(gmats[s], axis=-1, keepdims=True), s)
                if diagonal:
                    dz = jnp.where(live(diff, s) < -s * SB_TILE, dz, 0.0)
                dzs.append(spread(dz.astype(BF16), s))
            dz_all = jnp.concatenate(dzs, axis=1)
            dk_acc[jb] += _nn(q2_t, dz_all)
            dv_acc[jb] += _nn(dout2_t, kept(a_ref, jb, slice(None)))
            return gleft, dq + _nn(dz_all, k_ref[pl.ds(pl.multiple_of(jb * block_keys, block_keys), block_keys), :])

        carry = (jnp.zeros((2 * q_rows, 1), F32), jnp.zeros((2 * q_rows, LANES), F32))
        carry = lax.fori_loop(0, last, lambda jb, c: block(jb, c, False), carry)
        dq_ref[...] = _unstack_heads(block(last, carry, True)[1], first).astype(BF16)

        @pl.when(i == n_q - 1)
        def _():
            for jb in range(n_blocks):
                for s in range(n_sub):
                    rows = slice((jb * n_sub + s) * SB_TILE, (jb * n_sub + s + 1) * SB_TILE)
                    cols = slice(s * SB_TILE, (s + 1) * SB_TILE)
                    dk_ref[rows, :] = dk_acc[jb, :, cols].T.astype(BF16)
                    dv_ref[rows, :] = dv_acc[jb, :, cols].T.astype(BF16)

        finish()

    t = batch * seq
    blk = lambda c0: pl.BlockSpec((q_rows, LANES), lambda p, b, i: (b * n_q + i, c0 + p))
    kv = lambda c0: pl.BlockSpec((seq, LANES), lambda p, b, i: (b, c0 + p))
    row_spec = pl.BlockSpec((1, LANES), lambda p, b, i: (0, SB_PAIRS + p))
    kept_spec = pl.BlockSpec((1, 1, n_blocks, 1, 2 * q_rows, block_keys), lambda p, b, i: (p, b, 0, i, 0, 0))
    x_in, x_out, x_shape, x_scratch, x_arrays = _riding_specs(exchange)
    res = pl.pallas_call(
        body, name="sb_bwd", grid=(SB_PAIRS, batch, n_q),
        in_specs=[blk(q0), kv(k0), kv(v0), blk(0), kept_spec, kept_spec, blk(SB_PAIRS), row_spec] + x_in,
        out_specs=[blk(0), kv(0), kv(0), pl.BlockSpec((1, LANES), lambda p, b, i: (0, p))] + x_out,
        out_shape=[jax.ShapeDtypeStruct((t, GM_WIDTH), BF16)] * 3 + [jax.ShapeDtypeStruct((1, GM_WIDTH), F32)] + x_shape,
        scratch_shapes=[pltpu.VMEM((n_blocks, LANES, block_keys), F32), pltpu.VMEM((n_blocks, LANES, block_keys), F32)] + x_scratch,
        compiler_params=_params(3),
    )(proj, proj, proj, raw, weights, log_rest, dmerged, hg, *x_arrays)
    return res[0], res[1], res[2], res[3], res[4:]


def _x_softmax(s):
    s = s * X_SCALE
    p = jnp.exp(s - jnp.max(s, axis=-1, keepdims=True))
    return p * (1.0 / jnp.sum(p, axis=-1, keepdims=True))


def _x_heads(width):
    return [slice(h * X_HEAD_DIM, (h + 1) * X_HEAD_DIM) for h in range(X_HEADS)], \
           [slice(width + h * X_HEAD_DIM, width + (h + 1) * X_HEAD_DIM) for h in range(X_HEADS)]


def _xattn_fwd(q, kv, *, batch, seq, n_mem, tq=512):
    tq = min(tq, seq)
    n_q = seq // tq
    d = X_HEADS * X_HEAD_DIM

    def body(q_ref, kv_ref, o_ref):
        kcols, vcols = _x_heads(d)
        scores = [_nt(q_ref[:, c], kv_ref[:, c]) for c in kcols]
        probs = [_x_softmax(s).astype(BF16) for s in scores]
        for p, c, vc in zip(probs, kcols, vcols):
            o_ref[:, c] = _nn(p, kv_ref[:, vc]).astype(BF16)

    return pl.pallas_call(
        body, name="xattn_fwd", grid=(batch, n_q),
        in_specs=[pl.BlockSpec((tq, d), lambda b, i: (b * n_q + i, 0)), pl.BlockSpec((n_mem, 2 * d), lambda b, i: (b, 0))],
        out_specs=pl.BlockSpec((tq, d), lambda b, i: (b * n_q + i, 0)),
        out_shape=jax.ShapeDtypeStruct((batch * seq, d), BF16),
        compiler_params=_params(2),
    )(q, kv)


def _xattn_bwd(q, kv, do, *, batch, seq, n_mem, tq=512):
    tq = min(tq, seq)
    n_q = seq // tq
    d = X_HEADS * X_HEAD_DIM

    def body(q_ref, kv_ref, do_ref, dq_ref, dkv_ref, acc):
        i = pl.program_id(1)

        @pl.when(i == 0)
        def _():
            acc[...] = jnp.zeros_like(acc)

        kcols, vcols = _x_heads(d)
        scores = [_nt(q_ref[:, c], kv_ref[:, c]) for c in kcols]
        d_probs = [_nt(do_ref[:, c], kv_ref[:, vc]) for c, vc in zip(kcols, vcols)]
        probs = [_x_softmax(s) for s in scores]
        d_scores = [(p * (dp - jnp.sum(dp * p, axis=-1, keepdims=True)) * X_SCALE).astype(BF16) for p, dp in zip(probs, d_probs)]
        for p, ds, c, vc in zip(probs, d_scores, kcols, vcols):
            acc[:, vc] += _tn(p.astype(BF16), do_ref[:, c])
            dq_ref[:, c] = _nn(ds, kv_ref[:, c]).astype(BF16)
            acc[:, c] += _tn(ds, q_ref[:, c])

        @pl.when(i == n_q - 1)
        def _():
            dkv_ref[...] = acc[...].astype(BF16)

    return pl.pallas_call(
        body, name="xattn_bwd", grid=(batch, n_q),
        in_specs=[pl.BlockSpec((tq, d), lambda b, i: (b * n_q + i, 0)), pl.BlockSpec((n_mem, 2 * d), lambda b, i: (b, 0)),
                  pl.BlockSpec((tq, d), lambda b, i: (b * n_q + i, 0))],
        out_specs=[pl.BlockSpec((tq, d), lambda b, i: (b * n_q + i, 0)), pl.BlockSpec((n_mem, 2 * d), lambda b, i: (b, 0))],
        out_shape=[jax.ShapeDtypeStruct((batch * seq, d), BF16), jax.ShapeDtypeStruct((batch * n_mem, 2 * d), BF16)],
        scratch_shapes=[pltpu.VMEM((n_mem, 2 * d), F32)],
        compiler_params=_params(2),
    )(q, kv, do)


def _my_index():
    return 4 * lax.axis_index("x") + 2 * lax.axis_index("y") + lax.axis_index("c")


def _peers():
    x, y, c = lax.axis_index("x"), lax.axis_index("y"), lax.axis_index("c")
    out = []
    for rel in range(1, N_DEV):
        dx, dy, dc = (rel >> 2) & 1, (rel >> 1) & 1, rel & 1
        px, py, pc = x ^ dx, y ^ dy, c ^ dc
        out.append(((px, py, pc), 4 * px + 2 * py + pc))
    return out


class _Exchange:
    def __init__(self, arrays, scatter):
        self.arrays, self.scatter, self.n = list(arrays), scatter, len(arrays)
        any_spec = pl.BlockSpec(memory_space=pl.ANY)
        self.in_specs = [any_spec] * self.n
        self.out_specs = [any_spec] * self.n
        self.out_shape = [jax.ShapeDtypeStruct((N_DEV,) + tuple(a.shape[-2:]), a.dtype) for a in self.arrays]
        n_peer = N_DEV - 1
        self.scratch = [pltpu.SemaphoreType.DMA((self.n, n_peer)), pltpu.SemaphoreType.DMA((self.n, n_peer)),
                        pltpu.SemaphoreType.DMA((self.n,))]

    def _copies(self, srcs, dsts, sems, arriving):
        send_sems, recv_sems, local_sems = sems
        me = _my_index()
        local, remote = [], []
        for w in range(self.n):
            if not arriving:
                local.append(pltpu.make_async_copy(srcs[w].at[me] if self.scatter else srcs[w], dsts[w].at[me], local_sems.at[w]))
            for rel, (pos, idx) in enumerate(_peers()):
                remote.append(pltpu.make_async_remote_copy(
                    src_ref=srcs[w].at[idx] if self.scatter else srcs[w], dst_ref=dsts[w].at[idx if arriving else me],
                    send_sem=send_sems.at[w, rel], recv_sem=recv_sems.at[w, rel], device_id=pos, device_id_type=MESH))
        return local, remote

    def start(self, srcs, dsts, sems):
        local, sends = self._copies(srcs, dsts, sems, arriving=False)
        for cp in local + sends:
            cp.start()

    def wait(self, srcs, dsts, sems):
        for cp in self._copies(srcs, dsts, sems, arriving=True)[1]:
            cp.wait_recv()
        local, sends = self._copies(srcs, dsts, sems, arriving=False)
        for cp in sends:
            cp.wait_send()
        for cp in local:
            cp.wait()


def _two_level_gather(x_ref, slots_ref, send_sems, recv_sems, local_sem):
    x, y, c = lax.axis_index("x"), lax.axis_index("y"), lax.axis_index("c")
    me, sibling = (x, y, c), (x, y, 1 - c)
    chips = [(1 - x, y), (x, 1 - y), (1 - x, 1 - y)]

    def slot(px, py, pc):
        return slots_ref.at[4 * px + 2 * py + pc]

    def copy(k, block, to, src=None):
        return pltpu.make_async_remote_copy(src_ref=slot(*block) if src is None else src, dst_ref=slot(*block),
                                            send_sem=send_sems.at[k], recv_sem=recv_sems.at[k], device_id=to, device_id_type=MESH)

    mine = pltpu.make_async_copy(x_ref, slot(*me), local_sem)
    mine.start()
    first = [copy(0, me, sibling, src=x_ref)] + [copy(1 + j, me, (*chip, c), src=x_ref) for j, chip in enumerate(chips)]
    for cp in first:
        cp.start()
    passed = [copy(4 + j, (*chip, c), sibling) for j, chip in enumerate(chips)]
    for j, chip in enumerate(chips):
        copy(1 + j, (*chip, c), me).wait_recv()
        passed[j].start()
    copy(0, sibling, me).wait_recv()
    for j, chip in enumerate(chips):
        copy(4 + j, (*chip, 1 - c), me).wait_recv()
    for cp in first + passed:
        cp.wait_send()
    mine.wait()


_TWO_LEVEL_SEMS = [pltpu.SemaphoreType.DMA((N_DEV - 1,)), pltpu.SemaphoreType.DMA((N_DEV - 1,)), pltpu.SemaphoreType.DMA(())]


def _gather_two_level(shard, *, name):
    any_spec = pl.BlockSpec(memory_space=pl.ANY)
    return pl.pallas_call(
        _two_level_gather_body(), name=name, in_specs=[any_spec], out_specs=any_spec,
        out_shape=jax.ShapeDtypeStruct((N_DEV,) + shard.shape, shard.dtype), scratch_shapes=_TWO_LEVEL_SEMS,
    )(shard)


def _two_level_gather_body():
    def body(x_ref, out_ref, send_sems, recv_sems, local_sem):
        _two_level_gather(x_ref, out_ref, send_sems, recv_sems, local_sem)
    return body


def _all_reduce_small(part, *, loss_rows, loss_scale):
    rows = part.shape[0]

    def body(p_ref, o_ref, buf, send_sems, recv_sems, local_sem):
        _two_level_gather(p_ref, buf, send_sems, recv_sems, local_sem)
        total = buf[0]
        for dev in range(1, N_DEV):
            total = total + buf[dev]
        o_ref[...] = total
        squares = total[rows - loss_rows:]
        loss = jnp.sum(jnp.sum(squares, axis=0, keepdims=True), axis=-1, keepdims=True) * loss_scale
        o_ref[rows - loss_rows:, :] = jnp.broadcast_to(loss, (loss_rows, LANES))

    vmem = pl.BlockSpec(memory_space=pltpu.VMEM)
    return pl.pallas_call(
        body, name="all_reduce_small", in_specs=[vmem], out_specs=vmem, out_shape=jax.ShapeDtypeStruct(part.shape, F32),
        scratch_shapes=[pltpu.VMEM((N_DEV, rows, LANES), F32)] + _TWO_LEVEL_SEMS,
        compiler_params=pltpu.CompilerParams(has_side_effects=True, vmem_limit_bytes=VMEM_LIMIT),
    )(part)


def _adamw_math(w, g, m, v):
    m_new = ADAM_B1 * m + (1.0 - ADAM_B1) * g
    v_new = ADAM_B2 * v + (1.0 - ADAM_B2) * (g * g)
    m_hat = m_new / (1.0 - ADAM_B1 ** ADAM_STEP)
    v_hat = v_new / (1.0 - ADAM_B2 ** ADAM_STEP)
    delta = -ADAM_LR * (m_hat / (jnp.sqrt(v_hat) + ADAM_EPS) + ADAM_WD * w)
    return delta, m_new, v_new


def _adamw(parts, w, m, v, *, name, tr=64, parts_transposed=False):
    rows, cols = w.shape
    tr = min(tr, rows)
    while rows % (2 * tr) == 0 and 2 * tr * cols <= 256 * 1024:
        tr *= 2
    assert rows % tr == 0 and (not parts_transposed or tr % LANES == 0)
    stacked = parts.ndim == 3

    def body(p_ref, w_ref, m_ref, v_ref, g_ref, d_ref, mo_ref, vo_ref):
        if stacked:
            g = p_ref[0].astype(F32)
            for dev in range(1, N_DEV):
                g = g + p_ref[dev].astype(F32)
        else:
            g = p_ref[...]
        if parts_transposed:
            g = g.T
        delta, m_new, v_new = _adamw_math(w_ref[...], g, m_ref[...], v_ref[...])
        g_ref[...] = g
        d_ref[...] = delta
        mo_ref[...] = m_new
        vo_ref[...] = v_new

    tile = pl.BlockSpec((tr, cols), lambda i: (i, 0))
    if parts_transposed:
        p_spec = pl.BlockSpec((N_DEV, cols, tr), lambda i: (0, 0, i))
    else:
        p_spec = pl.BlockSpec((N_DEV, tr, cols), lambda i: (0, i, 0)) if stacked else tile
    return pl.pallas_call(
        body, name=name, grid=(rows // tr,), in_specs=[p_spec, tile, tile, tile], out_specs=[tile] * 4,
        out_shape=[jax.ShapeDtypeStruct((rows, cols), F32)] * 4, compiler_params=_params(1),
    )(parts, w, m, v)


_LATER = ("w_out", "w_cq", "w_ckv", "w_co", "w_ff1", "w_ff2")


def _as_rows(stacked):
    return stacked.reshape(-1, stacked.shape[-1])


def _local_step(x, mem, target, small, shards):
    batch, seq, d = x.shape
    n_mem = mem.shape[1]
    t = batch * seq
    x2, mem2, tgt2 = x.reshape(t, d), mem.reshape(batch * n_mem, d), target.reshape(t, d)
    g_mix, g_cross, g_mem, g_ffn, g_final = (small[k] for k in ("norm_mix_g", "norm_cross_g", "norm_mem_g", "norm_ffn_g", "norm_final_g"))
    gv, hg, w_sp, b_sp_t = small["gm_v_norm_g"], small["head_norm_g"], small["w_spatial"], small["b_spatial_t"]

    win_t = _as_rows(_gather_two_level(shards["w_in"], name="gather_w_in"))
    proj, xn = _norm_mm(x2, g_mix, win_t, mode="nt", name="proj_fwd")
    merged_a = _gmlp_fwd(proj, w_sp, b_sp_t, gv, hg)
    merged, sb_raw, sb_weights, sb_log_rest, gathered = _sb_fwd(proj, merged_a, hg, batch=batch, seq=seq,
                                                                exchange=_Exchange([shards[n] for n in _LATER], scatter=False))
    wout, wcq, wckv_t, wco, wff1_t, wff2 = (_as_rows(g) for g in gathered)
    h1, qx, hn1 = _mm(merged, wout, mode="nn", out_dtype=F32, name="mix_out_fwd_xq_fwd", tm=512, row_parts=2, epi=_epi_residual_then_norm_mm("nn"),
                      epi_ins=(x2,), vec_ins=(g_cross,), whole_ins=(wcq,), more_outs=((wcq.shape[1], BF16), (d, BF16)))
    kvx, memn = _norm_mm(mem2, g_mem, wckv_t, mode="nt", name="xkv_fwd")
    o = _xattn_fwd(qx, kvx, batch=batch, seq=seq, n_mem=n_mem)
    h2, fpre, hn2 = _mm(o, wco, mode="nn", out_dtype=F32, name="xo_fwd_ff1_fwd", tm=512, row_parts=2, epi=_epi_residual_then_norm_mm("nt"),
                        epi_ins=(h1,), vec_ins=(g_ffn,), whole_ins=(wff1_t,), more_outs=((wff1_t.shape[0], BF16), (d, BF16)))
    dh3, final_rows = _mm(fpre, wff2, mode="nn", out_dtype=F32, name="ff2_fwd_loss", tm=512, row_parts=2, tk=wff2.shape[0], a_fn=_relu2,
                          epi=_epi_loss, epi_ins=(h2, tgt2), vec_ins=(g_final,), aux=2)
    dg_final, sq_err = final_rows[0:1], final_rows[1:2]

    dpre = _mm(dh3, wff2, mode="nt", out_dtype=BF16, name="ff2_bwd_x", tm=512, tn=wff2.shape[0], row_parts=2, epi=_epi_relu2_grad,
               epi_ins=(fpre,))
    chunk = wff2.shape[0] // N_DEV
    d_wff2_t = _mm(dh3, fpre, mode="tn", out_dtype=BF16, name="ff2_bwd_w", tn=2048, b_fn=_relu2, col_chunk=chunk)
    d_wff1 = _mm(hn2, dpre, mode="tn", out_dtype=BF16, name="ff1_bwd_w", tn=2048, col_chunk=chunk)
    dh2, do, dg_ffn = _mm(dpre, wff1_t, mode="nn", out_dtype=F32, name="ff1_bwd_x_xo_bwd_x", tm=512, row_parts=2, tk=wff1_t.shape[0],
                          epi=_epi_rms_bwd_then_mm, epi_ins=(h2, dh3), vec_ins=(g_ffn,), whole_ins=(wco,),
                          more_outs=((wco.shape[0], BF16),), aux=True)
    d_wco = _mm(o, dh2, mode="tn", out_dtype=BF16, name="xo_bwd_w")
    dqx, dkvx = _xattn_bwd(qx, kvx, do, batch=batch, seq=seq, n_mem=n_mem)
    d_wcq = _mm(hn1, dqx, mode="tn", out_dtype=BF16, name="xq_bwd_w")
    dh1, dmerged, dg_cross = _mm(dqx, wcq, mode="nt", out_dtype=F32, name="xq_bwd_x_mix_out_bwd_x", tm=512, row_parts=2,
                                 epi=_epi_rms_bwd_then_mm, epi_ins=(h1, dh2), vec_ins=(g_cross,), whole_ins=(wout,),
                                 more_outs=((wout.shape[0], BF16),), aux=True)
    d_wckv_t = _mm(dkvx, memn, mode="tn", out_dtype=BF16, name="xkv_bwd_w")
    _, dg_mem = _mm(dkvx, wckv_t, mode="nn", out_dtype=BF16, name="xkv_bwd_x", tm=512, epi=_epi_rms_gain_only,
                    epi_ins=(mem2,), vec_ins=(g_mem,), aux=True)
    d_wout = _mm(merged, dh1, mode="tn", out_dtype=BF16, name="mix_out_bwd_w")
    dp_a, d_wsp, d_bsp_t, d_gv, d_hg_a = _gmlp_bwd(proj, dmerged, w_sp, b_sp_t, gv, hg)
    d_later = {"w_out": d_wout, "w_cq": d_wcq, "w_ckv": d_wckv_t, "w_co": d_wco, "w_ff1": d_wff1, "w_ff2": d_wff2_t}
    scatter = _Exchange([g if g.ndim == 3 else g.reshape(N_DEV, -1, d) for g in (d_later[n] for n in _LATER)], scatter=True)
    dq, dk, dv, d_hg_b, received = _sb_bwd(proj, sb_raw, sb_weights, sb_log_rest, dmerged, hg, batch=batch, seq=seq, exchange=scatter)
    dproj = dp_a
    for part, at in ((dq, 2 * GM_WIDTH), (dk, 3 * GM_WIDTH), (dv, 4 * GM_WIDTH)):
        dproj = lax.dynamic_update_slice(dproj, part, (0, at))
    d_win_t = _mm(xn, dproj, mode="tn", out_dtype=BF16, name="proj_bwd_w", tn=dproj.shape[1] // 2).T
    dx, dg_mix, d_win_received = _mm(dproj, win_t, mode="nn", out_dtype=F32, name="proj_bwd_x", tm=512, row_parts=2, tk=win_t.shape[0],
                                     epi=_epi_rms_bwd, epi_ins=(x2, dh1), vec_ins=(g_mix,), aux=True,
                                     exchange=_Exchange([d_win_t.reshape(N_DEV, -1, d)], scatter=True))

    d_small = {"norm_mix_g": dg_mix, "gm_v_norm_g": d_gv, "w_spatial": d_wsp, "b_spatial_t": d_bsp_t, "head_norm_g": jnp.concatenate([d_hg_a, d_hg_b], axis=1),
               "norm_cross_g": dg_cross, "norm_mem_g": dg_mem, "norm_ffn_g": dg_ffn, "norm_final_g": dg_final}
    d_big = dict(zip(_LATER, received))
    d_big["w_in"] = d_win_received
    return sq_err, dx.reshape(batch, seq, d), d_small, d_big


_BIG = ("w_in", "w_out", "w_cq", "w_ckv", "w_co", "w_ff1", "w_ff2")
_GATHERED_TRANSPOSED = ("w_in", "w_ckv", "w_ff1")
_UPDATED_TRANSPOSED = ("w_in", "w_ckv", "w_ff2")
_SMALL = ("norm_mix_g", "gm_v_norm_g", "w_spatial", "b_spatial", "head_norm_g", "norm_cross_g", "norm_mem_g", "norm_ffn_g", "norm_final_g")
_NAMES = ("norm_mix_g", "w_in", "gm_v_norm_g", "w_spatial", "b_spatial", "head_norm_g", "w_out", "norm_cross_g", "norm_mem_g",
          "w_cq", "w_ckv", "w_co", "norm_ffn_g", "w_ff1", "w_ff2", "norm_final_g")


def _rows_of(a):
    r = a.reshape(-1, LANES)
    pad = (-r.shape[0]) % 8
    return jnp.pad(r, ((0, pad), (0, 0))) if pad else r


def _shard2d(a, transposed):
    return a[0].T if transposed else a[0]


def kernel(x, mem, norm_mix_g, w_in, gm_v_norm_g, w_spatial, b_spatial, head_norm_g, w_out, norm_cross_g, norm_mem_g, w_cq, w_ckv, w_co, norm_ffn_g, w_ff1, w_ff2, norm_final_g, loss_target, m_norm_mix_g, m_w_in, m_gm_v_norm_g, m_w_spatial, m_b_spatial, m_head_norm_g, m_w_out, m_norm_cross_g, m_norm_mem_g, m_w_cq, m_w_ckv, m_w_co, m_norm_ffn_g, m_w_ff1, m_w_ff2, m_norm_final_g, v_norm_mix_g, v_w_in, v_gm_v_norm_g, v_w_spatial, v_b_spatial, v_head_norm_g, v_w_out, v_norm_cross_g, v_norm_mem_g, v_w_cq, v_w_ckv, v_w_co, v_norm_ffn_g, v_w_ff1, v_w_ff2, v_norm_final_g):
    weights = dict(norm_mix_g=norm_mix_g, w_in=w_in, gm_v_norm_g=gm_v_norm_g, w_spatial=w_spatial, b_spatial=b_spatial,
                   head_norm_g=head_norm_g, w_out=w_out, norm_cross_g=norm_cross_g, norm_mem_g=norm_mem_g, w_cq=w_cq, w_ckv=w_ckv,
                   w_co=w_co, norm_ffn_g=norm_ffn_g, w_ff1=w_ff1, w_ff2=w_ff2, norm_final_g=norm_final_g)
    mom1 = dict(norm_mix_g=m_norm_mix_g, w_in=m_w_in, gm_v_norm_g=m_gm_v_norm_g, w_spatial=m_w_spatial, b_spatial=m_b_spatial,
                head_norm_g=m_head_norm_g, w_out=m_w_out, norm_cross_g=m_norm_cross_g, norm_mem_g=m_norm_mem_g, w_cq=m_w_cq,
                w_ckv=m_w_ckv, w_co=m_w_co, norm_ffn_g=m_norm_ffn_g, w_ff1=m_w_ff1, w_ff2=m_w_ff2, norm_final_g=m_norm_final_g)
    mom2 = dict(norm_mix_g=v_norm_mix_g, w_in=v_w_in, gm_v_norm_g=v_gm_v_norm_g, w_spatial=v_w_spatial, b_spatial=v_b_spatial,
                head_norm_g=v_head_norm_g, w_out=v_w_out, norm_cross_g=v_norm_cross_g, norm_mem_g=v_norm_mem_g, w_cq=v_w_cq,
                w_ckv=v_w_ckv, w_co=v_w_co, norm_ffn_g=v_norm_ffn_g, w_ff1=v_w_ff1, w_ff2=v_w_ff2, norm_final_g=v_norm_final_g)

    shards = {n: _shard2d(weights[n], n in _GATHERED_TRANSPOSED).astype(BF16) for n in _BIG}
    small = {n: weights[n].reshape(1, -1) for n in _SMALL if n not in ("w_spatial", "b_spatial")}
    small["w_spatial"] = w_spatial[0]
    small["b_spatial_t"] = b_spatial[0].T
    sq_err, grad_x, d_small, d_big = _local_step(x, mem, loss_target, small, shards)

    d_small["b_spatial"] = d_small.pop("b_spatial_t").T
    sq_rows = _rows_of(sq_err)
    packed = jnp.concatenate([_rows_of(d_small[n]) for n in _SMALL] + [sq_rows], axis=0)
    summed = _all_reduce_small(packed, loss_rows=sq_rows.shape[0], loss_scale=0.5 / x.shape[-1])

    grads, deltas, new_m, new_v = {}, {}, {}, {}
    for n in _BIG:
        on_chip = n in _UPDATED_TRANSPOSED and weights[n].shape[-1] % LANES == 0
        flip = n in _UPDATED_TRANSPOSED and not on_chip
        outs = _adamw(d_big[n], _shard2d(weights[n], flip), _shard2d(mom1[n], flip), _shard2d(mom2[n], flip), name="adamw_" + n,
                      parts_transposed=on_chip)
        outs = [o.T if flip else o for o in outs]
        grads[n], deltas[n], new_m[n], new_v[n] = (o[None] for o in outs)
    pack = lambda src: jnp.concatenate([_rows_of(src[n]) for n in _SMALL], axis=0)
    n_small_rows = sum(_rows_of(weights[n]).shape[0] for n in _SMALL)
    outs = _adamw(summed[:n_small_rows], pack(weights), pack(mom1), pack(mom2), name="adamw_small", tr=n_small_rows)
    at = 0
    for n in _SMALL:
        used = weights[n].size // LANES
        for dst, o in zip((grads, deltas, new_m, new_v), outs):
            dst[n] = o[at:at + used].reshape(weights[n].shape)
        at += _rows_of(weights[n]).shape[0]
    loss = summed[n_small_rows, 0]
    return (loss, grad_x, *[grads[n] for n in _NAMES], *[deltas[n] for n in _NAMES], *[new_m[n] for n in _NAMES],
            *[new_v[n] for n in _NAMES])
```

```python
import math

import jax
import jax.numpy as jnp
from jax import lax
from jax.experimental import pallas as pl
from jax.experimental.pallas import tpu as pltpu

F32 = jnp.float32
BF16 = jnp.bfloat16
EPS = 1e-6
N_DEV = 8
LANES = 128
CHUNK = 128
GM_GROUPS = 4
GM_WIDTH = 512
SB_PAIRS = 4
SB_HEAD_DIM = 64
SB_SCALE = 0.125
SB_TILE = 128
SB_BLOCK = 512
X_HEADS = 4
X_HEAD_DIM = 256
X_SCALE = 1.0 / 16.0
VMEM_LIMIT = 56 * 1024 * 1024
ADAM_LR, ADAM_B1, ADAM_B2, ADAM_EPS, ADAM_WD, ADAM_STEP = 0.001, 0.9, 0.999, 1e-08, 0.01, 10
MESH = pl.DeviceIdType.MESH


def _params(n_axes):
    return pltpu.CompilerParams(dimension_semantics=("arbitrary",) * n_axes, vmem_limit_bytes=VMEM_LIMIT)


def _dot(a, b, dims):
    return lax.dot_general(a, b, (dims, ((), ())), preferred_element_type=F32)


def _nn(a, b):
    return _dot(a, b, ((1,), (0,)))


def _nt(a, b):
    return _dot(a, b, ((1,), (1,)))


def _tn(a, b):
    return _dot(a, b, ((0,), (0,)))


_MODES = {"nn": _nn, "nt": _nt, "tn": _tn}


def _rstd(x):
    return lax.rsqrt(jnp.mean(x * x, axis=-1, keepdims=True) + EPS)


def _gelu(x):
    c = math.sqrt(2.0 / math.pi)
    t = jnp.tanh(c * (x + 0.044715 * x * x * x))
    return 0.5 * x * (1.0 + t)


def _gelu_and_grad(x):
    c = math.sqrt(2.0 / math.pi)
    t = jnp.tanh(c * (x + 0.044715 * x * x * x))
    half = 0.5 * (1.0 + t)
    return x * half, half + 0.5 * x * (1.0 - t * t) * c * (1.0 + 3 * 0.044715 * x * x)


def _split_bf16(x):
    hi = x.astype(BF16)
    lo = (x - hi.astype(F32)).astype(BF16)
    return hi, lo


def _mm(a, b, *, mode, out_dtype, name, tm=1024, tn=1024, tk=1024, a_fn=None, b_fn=None, epi=None, epi_ins=(), vec_ins=(),
        whole_ins=(), more_outs=(), aux=False, col_chunk=None, row_parts=1, exchange=None):
    if mode == "nn":
        (m, k), (k2, n) = a.shape, b.shape
    elif mode == "nt":
        (m, k), (n, k2) = a.shape, b.shape
    else:
        (k, m), (k2, n) = a.shape, b.shape
    assert k == k2, (a.shape, b.shape, mode)
    tm, tn, tk = min(tm, m), min(tn, n), min(tk, k)
    assert m % tm == 0 and n % tn == 0 and k % tk == 0, (m, n, k, tm, tn, tk)
    n_m, n_n, n_k = m // tm, n // tn, k // tk
    assert not (aux or more_outs) or n_n == 1
    assert row_parts == 1 or (n_k == 1 and mode != "tn" and epi is not None and tm % (8 * row_parts) == 0)
    dot = _MODES[mode]
    n_epi, n_vec, n_whole, n_more = len(epi_ins), len(vec_ins), len(whole_ins), len(more_outs)

    def body(*refs):
        ins, outs, scratch, x_refs = _riding(exchange, refs, 2 + n_epi + n_vec + n_whole, 1 + n_more + (1 if aux else 0),
                                             1 if n_k > 1 else 0)
        a_ref, b_ref, epi_refs = ins[0], ins[1], ins[2:]
        o_ref, more_refs = outs[0], outs[1:1 + n_more]
        aux_ref = outs[1 + n_more] if aux else None
        acc_ref = scratch[0] if n_k > 1 else None
        i, j, kk = pl.program_id(0), pl.program_id(1), pl.program_id(2)
        ride_done = _ride(exchange, x_refs, (i == 0) & (j == 0) & (kk == 0), (i == n_m - 1) & (j == n_n - 1) & (kk == n_k - 1))
        def product(rows=slice(None)):
            av, bv = a_ref[...] if mode == "tn" else a_ref[rows, :], b_ref[...]
            if a_fn is not None:
                av = a_fn(av)
            if b_fn is not None:
                bv = b_fn(bv)
            return dot(av.astype(BF16), bv.astype(BF16))

        def finish(accs, parts=(slice(None),)):
            if col_chunk is not None:
                for ch in range(tn // col_chunk):
                    o_ref[ch] = accs[0][:, ch * col_chunk:(ch + 1) * col_chunk].astype(out_dtype)
                return
            if epi is None:
                o_ref[...] = accs[0].astype(out_dtype)
                return
            row_sums = []
            for acc, rows in zip(accs, parts):
                res = epi(acc, *[r[rows, :] for r in epi_refs[:n_epi]], *[r[...] for r in epi_refs[n_epi:]])
                if n_more:
                    for more_ref, value in zip(more_refs, res[1:1 + n_more]):
                        more_ref[rows, :] = value.astype(more_ref.dtype)
                    res = (res[0],) + tuple(res[1 + n_more:]) if aux else res[0]
                if aux:
                    res, sums = res[0], res[1:]
                    row_sums.append(sums[0] if len(sums) == 1 else jnp.concatenate(sums, axis=0))
                o_ref[rows, :] = res.astype(out_dtype)
            if aux:
                total = sum(row_sums)

                @pl.when(i == 0)
                def _():
                    aux_ref[...] = total

                @pl.when(i != 0)
                def _():
                    aux_ref[...] += total

        if n_k == 1:
            parts = [slice(h * (tm // row_parts), (h + 1) * (tm // row_parts)) for h in range(row_parts)]
            finish([product(rows) for rows in parts], parts)
        else:
            @pl.when(kk == 0)
            def _():
                acc_ref[...] = product()

            @pl.when(kk != 0)
            def _():
                acc_ref[...] += product()

            @pl.when(kk == n_k - 1)
            def _():
                finish([acc_ref[...]])

        ride_done()

    if mode == "tn":
        a_spec = pl.BlockSpec((tk, tm), lambda i, j, kk: (kk, i))
    else:
        a_spec = pl.BlockSpec((tm, tk), lambda i, j, kk: (i, kk))
    if mode == "nt":
        b_spec = pl.BlockSpec((tn, tk), lambda i, j, kk: (j, kk))
    else:
        b_spec = pl.BlockSpec((tk, tn), lambda i, j, kk: (kk, j))
    once = pl.Buffered(1)
    if n_n == 1 and n_k == 1:
        b_spec = pl.BlockSpec(b.shape, lambda i, j, kk: (0, 0), pipeline_mode=once)
    tile_spec = pl.BlockSpec((tm, tn), lambda i, j, kk: (i, j))
    row_spec = pl.BlockSpec((1, tn), lambda i, j, kk: (0, j))
    out_shape = [jax.ShapeDtypeStruct((m, n), out_dtype)]
    out_specs = [tile_spec]
    if col_chunk is not None:
        assert epi is None and not aux and tn % col_chunk == 0
        out_shape = [jax.ShapeDtypeStruct((n // col_chunk, m, col_chunk), out_dtype)]
        out_specs = [pl.BlockSpec((tn // col_chunk, tm, col_chunk), lambda i, j, kk: (j, i, 0))]
    for columns, dtype in more_outs:
        out_shape.append(jax.ShapeDtypeStruct((m, columns), dtype))
        out_specs.append(pl.BlockSpec((tm, columns), lambda i, j, kk: (i, 0)))
    if aux:
        out_shape.append(jax.ShapeDtypeStruct((int(aux), n), F32))
        out_specs.append(pl.BlockSpec((int(aux), tn), lambda i, j, kk: (0, j)))
    whole_specs = [pl.BlockSpec(w.shape, lambda i, j, kk, nd=w.ndim: (0,) * nd, pipeline_mode=once) for w in whole_ins]
    x_in, x_out, x_shape, x_scratch, x_arrays = _riding_specs(exchange)
    res = pl.pallas_call(
        body, name=name, grid=(n_m, n_n, n_k),
        in_specs=[a_spec, b_spec] + [tile_spec] * n_epi + [row_spec] * n_vec + whole_specs + x_in,
        out_specs=out_specs + x_out, out_shape=out_shape + x_shape,
        scratch_shapes=([pltpu.VMEM((tm, tn), F32)] if n_k > 1 else []) + x_scratch,
        compiler_params=_params(3),
    )(a, b, *epi_ins, *vec_ins, *whole_ins, *x_arrays)
    if exchange is not None or more_outs:
        return tuple(res)
    return res if aux else res[0]


def _norm_mm(x, g, w, *, mode, name, tm=1024, row_parts=2):
    m, d = x.shape
    n = w.shape[0] if mode == "nt" else w.shape[1]
    tm = min(tm, m)
    assert m % tm == 0 and tm % (16 * row_parts) == 0
    dot = _MODES[mode]

    def body(x_ref, g_ref, w_ref, o_ref, xn_ref):
        parts = [slice(h * (tm // row_parts), (h + 1) * (tm // row_parts)) for h in range(row_parts)]
        normed = []
        for rows in parts:
            xv = x_ref[rows, :]
            normed.append((xv * _rstd(xv) * g_ref[...]).astype(BF16))
        for rows, xn in zip(parts, normed):
            xn_ref[rows, :] = xn
            o_ref[rows, :] = dot(xn, w_ref[...]).astype(BF16)

    return pl.pallas_call(
        body, name=name, grid=(m // tm,),
        in_specs=[pl.BlockSpec((tm, d), lambda i: (i, 0)), pl.BlockSpec((1, d), lambda i: (0, 0)),
                  pl.BlockSpec(w.shape, lambda i: (0, 0), pipeline_mode=pl.Buffered(1))],
        out_specs=[pl.BlockSpec((tm, n), lambda i: (i, 0)), pl.BlockSpec((tm, d), lambda i: (i, 0))],
        out_shape=[jax.ShapeDtypeStruct((m, n), BF16), jax.ShapeDtypeStruct((m, d), BF16)],
        compiler_params=_params(1),
    )(x, g, w)


def _epi_relu2_grad(acc, pre):
    return acc * (2.0 * jnp.maximum(pre.astype(F32), 0.0))


def _relu2(pre):
    r = jnp.maximum(pre.astype(F32), 0.0)
    return r * r


def _epi_rms_bwd(acc, h, dres, g):
    r = _rstd(h)
    xh = h * r
    dxh = acc * g
    dh = dres + r * (dxh - xh * jnp.mean(dxh * xh, axis=-1, keepdims=True))
    return dh, jnp.sum(acc * xh, axis=0, keepdims=True)


def _epi_residual_then_norm_mm(mode):
    dot = _MODES[mode]

    def epi(acc, res, g, w):
        h = res + acc
        hn = (h * _rstd(h) * g).astype(BF16)
        return h, dot(hn, w), hn

    return epi


def _epi_rms_bwd_then_mm(acc, h, dres, g, w):
    dh, row = _epi_rms_bwd(acc, h, dres, g)
    return dh, _nt(dh.astype(BF16), w), row


def _epi_loss(acc, h_in, target, g):
    h = h_in + acc
    r = _rstd(h)
    xh = h * r
    err = xh * g - target
    dy = err * (1.0 / h.shape[-1])
    dxh = dy * g
    dh = r * (dxh - xh * jnp.mean(dxh * xh, axis=-1, keepdims=True))
    return dh, jnp.sum(dy * xh, axis=0, keepdims=True), jnp.sum(err * err, axis=0, keepdims=True)


def _epi_rms_gain_only(acc, h, g):
    return acc, jnp.sum(acc * (h * _rstd(h)), axis=0, keepdims=True)


def _tril(n):
    row = lax.broadcasted_iota(jnp.int32, (n, n), 0)
    col = lax.broadcasted_iota(jnp.int32, (n, n), 1)
    return col <= row


def _gmlp_fwd(proj, w_sp, b_sp_t, gv, hg, *, rows=512):
    t = proj.shape[0]
    rows = min(rows, t)
    n_c = rows // CHUNK

    def body(u_ref, v_ref, w_ref, bt_ref, gv_ref, hg_ref, m_ref):
        keep = _tril(CHUNK)
        for g in range(GM_GROUPS):
            cols = slice(g * LANES, (g + 1) * LANES)
            wg = jnp.where(keep, w_ref[g], 0.0).astype(BF16)
            u = _gelu(u_ref[:, cols].astype(F32))
            v = _gelu(v_ref[:, cols].astype(F32))
            vn = (v * _rstd(v) * gv_ref[:, cols]).astype(BF16)
            bias = bt_ref[:, g:g + 1]
            chunks = [slice(c * CHUNK, (c + 1) * CHUNK) for c in range(n_c)]
            gated = [u[rs] * (_nn(wg, vn[rs]) + bias) for rs in chunks]
            for rs, a in zip(chunks, gated):
                m_ref[rs, cols] = (a * _rstd(a) * hg_ref[:, cols]).astype(BF16)

    full = lambda shape: pl.BlockSpec(shape, lambda i: (0,) * len(shape))
    return pl.pallas_call(
        body, name="gmlp_fwd", grid=(t // rows,),
        in_specs=[pl.BlockSpec((rows, GM_WIDTH), lambda i: (i, 0)), pl.BlockSpec((rows, GM_WIDTH), lambda i: (i, 1)),
                  full((GM_GROUPS, CHUNK, CHUNK)), full((CHUNK, GM_GROUPS)), full((1, GM_WIDTH)), full((1, GM_WIDTH))],
        out_specs=pl.BlockSpec((rows, GM_WIDTH), lambda i: (i, 0)),
        out_shape=jax.ShapeDtypeStruct((t, 2 * GM_WIDTH), BF16),
        compiler_params=_params(1),
    )(proj, proj, w_sp, b_sp_t, gv, hg)


def _gmlp_bwd(proj, dmerged, w_sp, b_sp_t, gv, hg, *, rows=512):
    t = proj.shape[0]
    rows = min(rows, t)
    n_c = rows // CHUNK
    n_steps = t // rows

    def body(u_ref, v_ref, dm_ref, w_ref, bt_ref, gv_ref, hg_ref, dp_ref, dw_ref, dbt_ref, dgv_ref, dhg_ref, db_acc):
        step = pl.program_id(0)
        keep = _tril(CHUNK)

        @pl.when(step == 0)
        def _():
            dw_ref[...] = jnp.zeros_like(dw_ref)
            db_acc[...] = jnp.zeros_like(db_acc)
            dgv_ref[...] = jnp.zeros_like(dgv_ref)
            dhg_ref[...] = jnp.zeros_like(dhg_ref)

        for g in range(GM_GROUPS):
            cols = slice(g * LANES, (g + 1) * LANES)
            wg = jnp.where(keep, w_ref[g], 0.0).astype(BF16)
            u, u_slope = _gelu_and_grad(u_ref[:, cols].astype(F32))
            v, v_slope = _gelu_and_grad(v_ref[:, cols].astype(F32))
            r = _rstd(v)
            xh = v * r
            gvg = gv_ref[:, cols]
            hgg = hg_ref[:, cols]
            vn = (xh * gvg).astype(BF16)
            bias = bt_ref[:, g:g + 1]
            dm = dm_ref[:, cols].astype(F32)
            chunks = [slice(c * CHUNK, (c + 1) * CHUNK) for c in range(n_c)]
            mixed = [_nn(wg, vn[rs]) + bias for rs in chunks]
            gated = [u[rs] * mx for rs, mx in zip(chunks, mixed)]
            scale = [_rstd(a) for a in gated]
            normed = [a * ra for a, ra in zip(gated, scale)]
            d_normed = [dm[rs] * hgg for rs in chunks]
            d_gated = [ra * (dan - an * jnp.mean(dan * an, axis=-1, keepdims=True)) for ra, dan, an in zip(scale, d_normed, normed)]
            d_mixed = [da * u[rs] for da, rs in zip(d_gated, chunks)]
            d_mixed_b = [dmx.astype(BF16) for dmx in d_mixed]
            du = jnp.concatenate([da * mx for da, mx in zip(d_gated, mixed)], axis=0)
            dvn = jnp.concatenate([_tn(wg, dmb) for dmb in d_mixed_b], axis=0)
            dw_ref[g] += sum(_nt(dmb, vn[rs]) for dmb, rs in zip(d_mixed_b, chunks))
            db_acc[g] += sum(d_mixed)
            dhg_ref[:, cols] += sum(jnp.sum(dm[rs] * an, axis=0, keepdims=True) for rs, an in zip(chunks, normed))
            dgv_ref[:, cols] += jnp.sum(dvn * xh, axis=0, keepdims=True)
            dxh = dvn * gvg
            dv = r * (dxh - xh * jnp.mean(dxh * xh, axis=-1, keepdims=True))
            dp_ref[:, cols] = (du * u_slope).astype(BF16)
            dp_ref[:, GM_WIDTH + g * LANES:GM_WIDTH + (g + 1) * LANES] = (dv * v_slope).astype(BF16)

        @pl.when(step == n_steps - 1)
        def _():
            for g in range(GM_GROUPS):
                dw_ref[g] = jnp.where(keep, dw_ref[g], 0.0)
                dbt_ref[:, g:g + 1] = jnp.sum(db_acc[g], axis=-1, keepdims=True)

    full = lambda shape: pl.BlockSpec(shape, lambda i: (0,) * len(shape))
    return pl.pallas_call(
        body, name="gmlp_bwd", grid=(n_steps,),
        in_specs=[pl.BlockSpec((rows, GM_WIDTH), lambda i: (i, 0)), pl.BlockSpec((rows, GM_WIDTH), lambda i: (i, 1)),
                  pl.BlockSpec((rows, GM_WIDTH), lambda i: (i, 0)),
                  full((GM_GROUPS, CHUNK, CHUNK)), full((CHUNK, GM_GROUPS)), full((1, GM_WIDTH)), full((1, GM_WIDTH))],
        out_specs=[pl.BlockSpec((rows, 2 * GM_WIDTH), lambda i: (i, 0)), full((GM_GROUPS, CHUNK, CHUNK)),
                   full((CHUNK, GM_GROUPS)), full((1, GM_WIDTH)), full((1, GM_WIDTH))],
        out_shape=[jax.ShapeDtypeStruct((t, proj.shape[1]), BF16), jax.ShapeDtypeStruct((GM_GROUPS, CHUNK, CHUNK), F32),
                   jax.ShapeDtypeStruct((CHUNK, GM_GROUPS), F32), jax.ShapeDtypeStruct((1, GM_WIDTH), F32),
                   jax.ShapeDtypeStruct((1, GM_WIDTH), F32)],
        scratch_shapes=[pltpu.VMEM((GM_GROUPS, CHUNK, LANES), F32)],
        compiler_params=_params(1),
    )(proj, proj, dmerged, w_sp, b_sp_t, gv, hg)


def _sb_logits(z, strict):
    ls = jnp.minimum(z, 0.0) - jnp.log(1.0 + jnp.exp(-jnp.abs(z)))
    l1m = ls - z
    if strict is not None:
        l1m = jnp.where(strict, l1m, 0.0)
    return ls, l1m


def _tri_sums(x, tri):
    hi, lo = _split_bf16(x)
    return _nn(jnp.concatenate([hi, lo], axis=1), jnp.concatenate([tri, tri], axis=0))


def _sb_weights(ls, in_tile, right, strict):
    a = jnp.exp(ls + in_tile + right)
    if strict is not None:
        a = jnp.where(strict, a, 0.0)
    return a


def _sb_masks(q_rows):
    row = lax.broadcasted_iota(jnp.int32, (SB_TILE, SB_TILE), 0)
    col = lax.broadcasted_iota(jnp.int32, (SB_TILE, SB_TILE), 1)
    lane = lax.broadcasted_iota(jnp.int32, (q_rows, LANES), 1)
    return row, col, lane < SB_HEAD_DIM


def _stack_heads(x, first):
    zero = jnp.zeros_like(x)
    return jnp.concatenate([jnp.where(first, x, zero), jnp.where(first, zero, x)], axis=0)


def _stack_heads_t(x_t):
    first_t = lax.broadcasted_iota(jnp.int32, x_t.shape, 0) < SB_HEAD_DIM
    zero = jnp.zeros_like(x_t)
    return jnp.concatenate([jnp.where(first_t, x_t, zero), jnp.where(first_t, zero, x_t)], axis=1).astype(BF16)


def _unstack_heads(x2, first):
    half = x2.shape[0] // 2
    return jnp.where(first, x2[:half], x2[half:])


def _live_rows(x, s, q_rows):
    if s == 0:
        return x
    return jnp.concatenate([x[s * SB_TILE:q_rows], x[q_rows + s * SB_TILE:]], axis=0)


def _spread_rows(x, s, q_rows):
    if s == 0:
        return x
    half = q_rows - s * SB_TILE
    zero = jnp.zeros((s * SB_TILE,) + x.shape[1:], x.dtype)
    return jnp.concatenate([zero, x[:half], zero, x[half:]], axis=0)


def _stacked_col_minus_row(q_rows):
    row = lax.broadcasted_iota(jnp.int32, (2 * q_rows, SB_TILE), 0)
    col = lax.broadcasted_iota(jnp.int32, (2 * q_rows, SB_TILE), 1)
    return col - (row & (q_rows - 1))


def _head_mean(x, first):
    s0 = jnp.sum(jnp.where(first, x, 0.0), axis=-1, keepdims=True)
    s1 = jnp.sum(jnp.where(first, 0.0, x), axis=-1, keepdims=True)
    return jnp.where(first, s0, s1) * (1.0 / SB_HEAD_DIM)


def _riding(exchange, refs, n_in, n_out, n_scratch):
    n_x = exchange.n if exchange is not None else 0
    ins, rest = refs[:n_in], refs[n_in:]
    x_src, rest = rest[:n_x], rest[n_x:]
    outs, rest = rest[:n_out], rest[n_out:]
    x_dst, rest = rest[:n_x], rest[n_x:]
    return ins, outs, rest[:n_scratch], (x_src, x_dst, rest[n_scratch:])


def _riding_specs(exchange):
    if exchange is None:
        return [], [], [], [], []
    return exchange.in_specs, exchange.out_specs, exchange.out_shape, exchange.scratch, exchange.arrays


def _ride(exchange, x_refs, first_step, last_step):
    if exchange is None:
        return lambda: None

    @pl.when(first_step)
    def _():
        exchange.start(*x_refs)

    def finish():
        @pl.when(last_step)
        def _():
            exchange.wait(*x_refs)

    return finish


def _sb_fwd(proj, merged_a, hg, *, batch, seq, exchange=None):
    q0, k0, v0 = 2 * GM_WIDTH // LANES, 2 * GM_WIDTH // LANES + SB_PAIRS, 2 * GM_WIDTH // LANES + 2 * SB_PAIRS
    q_rows = block_keys = min(SB_BLOCK, seq)
    assert seq % block_keys == 0
    n_q, n_sub, n_blocks = seq // q_rows, block_keys // SB_TILE, seq // block_keys

    def body(*refs):
        (q_ref, k_ref, v_ref, hg_ref, _), (m_ref, raw_ref, a_ref, l_ref), _, x_refs = _riding(exchange, refs, 5, 4, 0)
        b, p, i = pl.program_id(0), pl.program_id(1), pl.program_id(2)
        finish = _ride(exchange, x_refs, (b == 0) & (p == 0) & (i == 0), (b == batch - 1) & (p == SB_PAIRS - 1) & (i == n_q - 1))
        row, col, first = _sb_masks(q_rows)
        upper = (row > col).astype(BF16)
        q2 = _stack_heads((q_ref[...].astype(F32) * SB_SCALE).astype(BF16), first)
        diff = _stacked_col_minus_row(q_rows)
        last = i

        def key_tile(jb, s):
            return k_ref[pl.ds(pl.multiple_of((jb * n_sub + s) * SB_TILE, SB_TILE), SB_TILE), :]

        def scores(jb):
            return tuple(_nt(q2, key_tile(jb, s)) for s in range(n_sub))

        def keep_for_backward(ref, jb, s, stacked):
            ref[0, 0, jb, 0, :, s * SB_TILE:(s + 1) * SB_TILE] = stacked

        def weights_of(jb, z, right):
            logits = [_sb_logits(z[s], None) for s in reversed(range(n_sub))][::-1]
            totals = [jnp.sum(l1m, axis=-1, keepdims=True) for _, l1m in logits]
            sums = [_tri_sums(l1m, upper) for _, l1m in reversed(logits)][::-1]
            for s in reversed(range(n_sub)):
                keep_for_backward(a_ref, jb, s, _sb_weights(logits[s][0], sums[s], right, None).astype(BF16))
                keep_for_backward(l_ref, jb, s, logits[s][1].astype(BF16))
                right = right + totals[s]
            return right

        def diagonal_weights():
            keeps = [_live_rows(diff, s, q_rows) < -s * SB_TILE for s in range(n_sub)]
            logits = [_sb_logits(_nt(_live_rows(q2, s, q_rows), key_tile(last, s)), keeps[s]) for s in range(n_sub)]
            totals = [jnp.sum(l1m, axis=-1, keepdims=True) for _, l1m in logits]
            sums = [_tri_sums(l1m, upper) for _, l1m in logits]
            right = jnp.zeros((2 * q_rows, 1), F32)
            for s in reversed(range(n_sub)):
                live = _sb_weights(logits[s][0], sums[s], _live_rows(right, s, q_rows), keeps[s]).astype(BF16)
                keep_for_backward(a_ref, last, s, _spread_rows(live, s, q_rows))
                keep_for_backward(l_ref, last, s, _spread_rows(logits[s][1].astype(BF16), s, q_rows))
                right = right + _spread_rows(totals[s], s, q_rows)
            return right

        def weighted_values(jb):
            return _nn(a_ref[0, 0, jb, 0], v_ref[pl.ds(pl.multiple_of(jb * block_keys, block_keys), block_keys), :])

        right = diagonal_weights()
        z_next = scores(jnp.maximum(last - 1, 0))

        def step(k, carry):
            right, acc, z = carry
            jb = last - k
            acc = acc + weighted_values(jb + 1)
            z_next = scores(jnp.maximum(jb - 1, 0))
            return weights_of(jb, z, right), acc, z_next

        _, acc2, _ = lax.fori_loop(1, last + 1, step, (right, jnp.zeros((2 * q_rows, LANES), F32), z_next))
        acc = _unstack_heads(acc2 + weighted_values(0), first)
        raw_ref[...] = acc
        m_ref[...] = (acc * lax.rsqrt(_head_mean(acc * acc, first) + EPS) * hg_ref[...]).astype(BF16)
        finish()

    t = batch * seq
    blk = lambda c0: pl.BlockSpec((q_rows, LANES), lambda b, p, i: (b * n_q + i, c0 + p))
    kv = lambda c0: pl.BlockSpec((seq, LANES), lambda b, p, i: (b, c0 + p))
    kept = pl.BlockSpec((1, 1, n_blocks, 1, 2 * q_rows, block_keys), lambda b, p, i: (p, b, 0, i, 0, 0))
    kept_shape = jax.ShapeDtypeStruct((SB_PAIRS, batch, n_blocks, n_q, 2 * q_rows, block_keys), BF16)
    x_in, x_out, x_shape, x_scratch, x_arrays = _riding_specs(exchange)
    res = pl.pallas_call(
        body, name="sb_fwd", grid=(batch, SB_PAIRS, n_q),
        in_specs=[blk(q0), kv(k0), kv(v0), pl.BlockSpec((1, LANES), lambda b, p, i: (0, SB_PAIRS + p)),
                  pl.BlockSpec(memory_space=pl.ANY)] + x_in,
        out_specs=[blk(SB_PAIRS), blk(0), kept, kept] + x_out,
        out_shape=[jax.ShapeDtypeStruct((t, 2 * GM_WIDTH), BF16), jax.ShapeDtypeStruct((t, GM_WIDTH), F32), kept_shape, kept_shape] + x_shape,
        scratch_shapes=x_scratch,
        input_output_aliases={4: 0},
        compiler_params=_params(3),
    )(proj, proj, proj, hg, merged_a, *x_arrays)
    return res[0], res[1], res[2], res[3], res[4:]


def _sb_bwd(proj, raw, weights, log_rest, dmerged, hg, *, batch, seq, exchange=None):
    q0, k0, v0 = 2 * GM_WIDTH // LANES, 2 * GM_WIDTH // LANES + SB_PAIRS, 2 * GM_WIDTH // LANES + 2 * SB_PAIRS
    q_rows = block_keys = min(SB_BLOCK, seq)
    assert seq % block_keys == 0
    n_q, n_sub, n_blocks = seq // q_rows, block_keys // SB_TILE, seq // block_keys

    def body(*refs):
        ins, outs, (dk_acc, dv_acc), x_refs = _riding(exchange, refs, 8, 4, 2)
        q_ref, k_ref, v_ref, raw_ref, a_ref, l_ref, dm_ref, hg_ref = ins
        dq_ref, dk_ref, dv_ref, dhg_ref = outs
        p, b, i = pl.program_id(0), pl.program_id(1), pl.program_id(2)
        finish = _ride(exchange, x_refs, (b == 0) & (p == 0) & (i == 0), (b == batch - 1) & (p == SB_PAIRS - 1) & (i == n_q - 1))
        row, col, first = _sb_masks(q_rows)
        lower = (row < col).astype(BF16)

        @pl.when(jnp.logical_and(b == 0, i == 0))
        def _():
            dhg_ref[...] = jnp.zeros_like(dhg_ref)

        @pl.when(i == 0)
        def _():
            dk_acc[...] = jnp.zeros_like(dk_acc)
            dv_acc[...] = jnp.zeros_like(dv_acc)

        raw_v = raw_ref[...]
        dm = dm_ref[...].astype(F32)
        r = lax.rsqrt(_head_mean(raw_v * raw_v, first) + EPS)
        nrm = raw_v * r
        dhg_ref[...] += jnp.sum(dm * nrm, axis=0, keepdims=True)
        dn = dm * hg_ref[...]
        dout = r * (dn - nrm * _head_mean(dn * nrm, first))
        dout2 = _stack_heads(dout.astype(BF16), first)
        q2_t = _stack_heads_t(q_ref[...].astype(F32).T)
        dout2_t = _stack_heads_t(dout.T)
        diff = _stacked_col_minus_row(q_rows)
        last = i

        def kept(ref, jb, cols):
            return ref[0, 0, jb, 0, :, cols]

        def block(jb, carry, diagonal):
            gleft, dq = carry
            live = (lambda x, s: _live_rows(x, s, q_rows)) if diagonal else (lambda x, s: x)
            spread = (lambda x, s: _spread_rows(x, s, q_rows)) if diagonal else (lambda x, s: x)
            tiles = [pl.ds(pl.multiple_of((jb * n_sub + s) * SB_TILE, SB_TILE), SB_TILE) for s in range(n_sub)]
            cols = [slice(s * SB_TILE, (s + 1) * SB_TILE) for s in range(n_sub)]
            gmats = [_nt(live(dout2, s), v_ref[tiles[s], :]) * live(kept(a_ref, jb, cols[s]), s).astype(F32) for s in range(n_sub)]
            prefixes = [_tri_sums(g, lower) for g in gmats]
            dzs = []
            for s in range(n_sub):
                one_minus = jnp.exp(live(kept(l_ref, jb, cols[s]), s).astype(F32))
                dz = (gmats[s] * one_minus - (live(gleft, s) + prefixes[s]) * (1.0 - one_minus)) * SB_SCALE
                gleft = gleft + spread(jnp.sum(gmats[s], axis=-1, keepdims=True), s)
                if diagonal:
                    dz = jnp.where(live(diff, s) < -s * SB_TILE, dz, 0.0)
                dzs.append(spread(dz.astype(BF16), s))
            dz_all = jnp.concatenate(dzs, axis=1)
            dk_acc[jb] += _nn(q2_t, dz_all)
            dv_acc[jb] += _nn(dout2_t, kept(a_ref, jb, slice(None)))
            return gleft, dq + _nn(dz_all, k_ref[pl.ds(pl.multiple_of(jb * block_keys, block_keys), block_keys), :])

        carry = (jnp.zeros((2 * q_rows, 1), F32), jnp.zeros((2 * q_rows, LANES), F32))
        carry = lax.fori_loop(0, last, lambda jb, c: block(jb, c, False), carry)
        dq_ref[...] = _unstack_heads(block(last, carry, True)[1], first).astype(BF16)

        @pl.when(i == n_q - 1)
        def _():
            for jb in range(n_blocks):
                for s in range(n_sub):
                    rows = slice((jb * n_sub + s) * SB_TILE, (jb * n_sub + s + 1) * SB_TILE)
                    cols = slice(s * SB_TILE, (s + 1) * SB_TILE)
                    dk_ref[rows, :] = dk_acc[jb, :, cols].T.astype(BF16)
                    dv_ref[rows, :] = dv_acc[jb, :, cols].T.astype(BF16)

        finish()

    t = batch * seq
    blk = lambda c0: pl.BlockSpec((q_rows, LANES), lambda p, b, i: (b * n_q + i, c0 + p))
    kv = lambda c0: pl.BlockSpec((seq, LANES), lambda p, b, i: (b, c0 + p))
    row_spec = pl.BlockSpec((1, LANES), lambda p, b, i: (0, SB_PAIRS + p))
    kept_spec = pl.BlockSpec((1, 1, n_blocks, 1, 2 * q_rows, block_keys), lambda p, b, i: (p, b, 0, i, 0, 0))
    x_in, x_out, x_shape, x_scratch, x_arrays = _riding_specs(exchange)
    res = pl.pallas_call(
        body, name="sb_bwd", grid=(SB_PAIRS, batch, n_q),
        in_specs=[blk(q0), kv(k0), kv(v0), blk(0), kept_spec, kept_spec, blk(SB_PAIRS), row_spec] + x_in,
        out_specs=[blk(0), kv(0), kv(0), pl.BlockSpec((1, LANES), lambda p, b, i: (0, p))] + x_out,
        out_shape=[jax.ShapeDtypeStruct((t, GM_WIDTH), BF16)] * 3 + [jax.ShapeDtypeStruct((1, GM_WIDTH), F32)] + x_shape,
        scratch_shapes=[pltpu.VMEM((n_blocks, LANES, block_keys), F32), pltpu.VMEM((n_blocks, LANES, block_keys), F32)] + x_scratch,
        compiler_params=_params(3),
    )(proj, proj, proj, raw, weights, log_rest, dmerged, hg, *x_arrays)
    return res[0], res[1], res[2], res[3], res[4:]


def _x_softmax(s):
    s = s * X_SCALE
    p = jnp.exp(s - jnp.max(s, axis=-1, keepdims=True))
    return p * (1.0 / jnp.sum(p, axis=-1, keepdims=True))


def _x_heads(width):
    return [slice(h * X_HEAD_DIM, (h + 1) * X_HEAD_DIM) for h in range(X_HEADS)], \
           [slice(width + h * X_HEAD_DIM, width + (h + 1) * X_HEAD_DIM) for h in range(X_HEADS)]


def _xattn_fwd(q, kv, *, batch, seq, n_mem, tq=512):
    tq = min(tq, seq)
    n_q = seq // tq
    d = X_HEADS * X_HEAD_DIM

    def body(q_ref, kv_ref, o_ref):
        kcols, vcols = _x_heads(d)
        scores = [_nt(q_ref[:, c], kv_ref[:, c]) for c in kcols]
        probs = [_x_softmax(s).astype(BF16) for s in scores]
        for p, c, vc in zip(probs, kcols, vcols):
            o_ref[:, c] = _nn(p, kv_ref[:, vc]).astype(BF16)

    return pl.pallas_call(
        body, name="xattn_fwd", grid=(batch, n_q),
        in_specs=[pl.BlockSpec((tq, d), lambda b, i: (b * n_q + i, 0)), pl.BlockSpec((n_mem, 2 * d), lambda b, i: (b, 0))],
        out_specs=pl.BlockSpec((tq, d), lambda b, i: (b * n_q + i, 0)),
        out_shape=jax.ShapeDtypeStruct((batch * seq, d), BF16),
        compiler_params=_params(2),
    )(q, kv)


def _xattn_bwd(q, kv, do, *, batch, seq, n_mem, tq=512):
    tq = min(tq, seq)
    n_q = seq // tq
    d = X_HEADS * X_HEAD_DIM

    def body(q_ref, kv_ref, do_ref, dq_ref, dkv_ref, acc):
        i = pl.program_id(1)

        @pl.when(i == 0)
        def _():
            acc[...] = jnp.zeros_like(acc)

        kcols, vcols = _x_heads(d)
        scores = [_nt(q_ref[:, c], kv_ref[:, c]) for c in kcols]
        d_probs = [_nt(do_ref[:, c], kv_ref[:, vc]) for c, vc in zip(kcols, vcols)]
        probs = [_x_softmax(s) for s in scores]
        d_scores = [(p * (dp - jnp.sum(dp * p, axis=-1, keepdims=True)) * X_SCALE).astype(BF16) for p, dp in zip(probs, d_probs)]
        for p, ds, c, vc in zip(probs, d_scores, kcols, vcols):
            acc[:, vc] += _tn(p.astype(BF16), do_ref[:, c])
            dq_ref[:, c] = _nn(ds, kv_ref[:, c]).astype(BF16)
            acc[:, c] += _tn(ds, q_ref[:, c])

        @pl.when(i == n_q - 1)
        def _():
            dkv_ref[...] = acc[...].astype(BF16)

    return pl.pallas_call(
        body, name="xattn_bwd", grid=(batch, n_q),
        in_specs=[pl.BlockSpec((tq, d), lambda b, i: (b * n_q + i, 0)), pl.BlockSpec((n_mem, 2 * d), lambda b, i: (b, 0)),
                  pl.BlockSpec((tq, d), lambda b, i: (b * n_q + i, 0))],
        out_specs=[pl.BlockSpec((tq, d), lambda b, i: (b * n_q + i, 0)), pl.BlockSpec((n_mem, 2 * d), lambda b, i: (b, 0))],
        out_shape=[jax.ShapeDtypeStruct((batch * seq, d), BF16), jax.ShapeDtypeStruct((batch * n_mem, 2 * d), BF16)],
        scratch_shapes=[pltpu.VMEM((n_mem, 2 * d), F32)],
        compiler_params=_params(2),
    )(q, kv, do)


def _my_index():
    return 4 * lax.axis_index("x") + 2 * lax.axis_index("y") + lax.axis_index("c")


def _peers():
    x, y, c = lax.axis_index("x"), lax.axis_index("y"), lax.axis_index("c")
    out = []
    for rel in range(1, N_DEV):
        dx, dy, dc = (rel >> 2) & 1, (rel >> 1) & 1, rel & 1
        px, py, pc = x ^ dx, y ^ dy, c ^ dc
        out.append(((px, py, pc), 4 * px + 2 * py + pc))
    return out


class _Exchange:
    def __init__(self, arrays, scatter):
        self.arrays, self.scatter, self.n = list(arrays), scatter, len(arrays)
        any_spec = pl.BlockSpec(memory_space=pl.ANY)
        self.in_specs = [any_spec] * self.n
        self.out_specs = [any_spec] * self.n
        self.out_shape = [jax.ShapeDtypeStruct((N_DEV,) + tuple(a.shape[-2:]), a.dtype) for a in self.arrays]
        n_peer = N_DEV - 1
        self.scratch = [pltpu.SemaphoreType.DMA((self.n, n_peer)), pltpu.SemaphoreType.DMA((self.n, n_peer)),
                        pltpu.SemaphoreType.DMA((self.n,))]

    def _copies(self, srcs, dsts, sems, arriving):
        send_sems, recv_sems, local_sems = sems
        me = _my_index()
        local, remote = [], []
        for w in range(self.n):
            if not arriving:
                local.append(pltpu.make_async_copy(srcs[w].at[me] if self.scatter else srcs[w], dsts[w].at[me], local_sems.at[w]))
            for rel, (pos, idx) in enumerate(_peers()):
                remote.append(pltpu.make_async_remote_copy(
                    src_ref=srcs[w].at[idx] if self.scatter else srcs[w], dst_ref=dsts[w].at[idx if arriving else me],
                    send_sem=send_sems.at[w, rel], recv_sem=recv_sems.at[w, rel], device_id=pos, device_id_type=MESH))
        return local, remote

    def start(self, srcs, dsts, sems):
        local, sends = self._copies(srcs, dsts, sems, arriving=False)
        for cp in local + sends:
            cp.start()

    def wait(self, srcs, dsts, sems):
        for cp in self._copies(srcs, dsts, sems, arriving=True)[1]:
            cp.wait_recv()
        local, sends = self._copies(srcs, dsts, sems, arriving=False)
        for cp in sends:
            cp.wait_send()
        for cp in local:
            cp.wait()


def _two_level_gather(x_ref, slots_ref, send_sems, recv_sems, local_sem):
    x, y, c = lax.axis_index("x"), lax.axis_index("y"), lax.axis_index("c")
    me, sibling = (x, y, c), (x, y, 1 - c)
    chips = [(1 - x, y), (x, 1 - y), (1 - x, 1 - y)]

    def slot(px, py, pc):
        return slots_ref.at[4 * px + 2 * py + pc]

    def copy(k, block, to, src=None):
        return pltpu.make_async_remote_copy(src_ref=slot(*block) if src is None else src, dst_ref=slot(*block),
                                            send_sem=send_sems.at[k], recv_sem=recv_sems.at[k], device_id=to, device_id_type=MESH)

    mine = pltpu.make_async_copy(x_ref, slot(*me), local_sem)
    mine.start()
    first = [copy(0, me, sibling, src=x_ref)] + [copy(1 + j, me, (*chip, c), src=x_ref) for j, chip in enumerate(chips)]
    for cp in first:
        cp.start()
    passed = [copy(4 + j, (*chip, c), sibling) for j, chip in enumerate(chips)]
    for j, chip in enumerate(chips):
        copy(1 + j, (*chip, c), me).wait_recv()
        passed[j].start()
    copy(0, sibling, me).wait_recv()
    for j, chip in enumerate(chips):
        copy(4 + j, (*chip, 1 - c), me).wait_recv()
    for cp in first + passed:
        cp.wait_send()
    mine.wait()


_TWO_LEVEL_SEMS = [pltpu.SemaphoreType.DMA((N_DEV - 1,)), pltpu.SemaphoreType.DMA((N_DEV - 1,)), pltpu.SemaphoreType.DMA(())]


def _gather_two_level(shard, *, name):
    any_spec = pl.BlockSpec(memory_space=pl.ANY)
    return pl.pallas_call(
        _two_level_gather_body(), name=name, in_specs=[any_spec], out_specs=any_spec,
        out_shape=jax.ShapeDtypeStruct((N_DEV,) + shard.shape, shard.dtype), scratch_shapes=_TWO_LEVEL_SEMS,
    )(shard)


def _two_level_gather_body():
    def body(x_ref, out_ref, send_sems, recv_sems, local_sem):
        _two_level_gather(x_ref, out_ref, send_sems, recv_sems, local_sem)
    return body


def _all_reduce_small(part, *, loss_rows, loss_scale):
    rows = part.shape[0]

    def body(p_ref, o_ref, buf, send_sems, recv_sems, local_sem):
        _two_level_gather(p_ref, buf, send_sems, recv_sems, local_sem)
        total = buf[0]
        for dev in range(1, N_DEV):
            total = total + buf[dev]
        o_ref[...] = total
        squares = total[rows - loss_rows:]
        loss = jnp.sum(jnp.sum(squares, axis=0, keepdims=True), axis=-1, keepdims=True) * loss_scale
        o_ref[rows - loss_rows:, :] = jnp.broadcast_to(loss, (loss_rows, LANES))

    vmem = pl.BlockSpec(memory_space=pltpu.VMEM)
    return pl.pallas_call(
        body, name="all_reduce_small", in_specs=[vmem], out_specs=vmem, out_shape=jax.ShapeDtypeStruct(part.shape, F32),
        scratch_shapes=[pltpu.VMEM((N_DEV, rows, LANES), F32)] + _TWO_LEVEL_SEMS,
        compiler_params=pltpu.CompilerParams(has_side_effects=True, vmem_limit_bytes=VMEM_LIMIT),
    )(part)


def _adamw_math(w, g, m, v):
    m_new = ADAM_B1 * m + (1.0 - ADAM_B1) * g
    v_new = ADAM_B2 * v + (1.0 - ADAM_B2) * (g * g)
    m_hat = m_new / (1.0 - ADAM_B1 ** ADAM_STEP)
    v_hat = v_new / (1.0 - ADAM_B2 ** ADAM_STEP)
    delta = -ADAM_LR * (m_hat / (jnp.sqrt(v_hat) + ADAM_EPS) + ADAM_WD * w)
    return delta, m_new, v_new


def _adamw(parts, w, m, v, *, name, tr=64, parts_transposed=False):
    rows, cols = w.shape
    tr = min(tr, rows)
    while rows % (2 * tr) == 0 and 2 * tr * cols <= 256 * 1024:
        tr *= 2
    assert rows % tr == 0 and (not parts_transposed or tr % LANES == 0)
    stacked = parts.ndim == 3

    def body(p_ref, w_ref, m_ref, v_ref, g_ref, d_ref, mo_ref, vo_ref):
        if stacked:
            g = p_ref[0].astype(F32)
            for dev in range(1, N_DEV):
                g = g + p_ref[dev].astype(F32)
        else:
            g = p_ref[...]
        if parts_transposed:
            g = g.T
        delta, m_new, v_new = _adamw_math(w_ref[...], g, m_ref[...], v_ref[...])
        g_ref[...] = g
        d_ref[...] = delta
        mo_ref[...] = m_new
        vo_ref[...] = v_new

    tile = pl.BlockSpec((tr, cols), lambda i: (i, 0))
    if parts_transposed:
        p_spec = pl.BlockSpec((N_DEV, cols, tr), lambda i: (0, 0, i))
    else:
        p_spec = pl.BlockSpec((N_DEV, tr, cols), lambda i: (0, i, 0)) if stacked else tile
    return pl.pallas_call(
        body, name=name, grid=(rows // tr,), in_specs=[p_spec, tile, tile, tile], out_specs=[tile] * 4,
        out_shape=[jax.ShapeDtypeStruct((rows, cols), F32)] * 4, compiler_params=_params(1),
    )(parts, w, m, v)


_LATER = ("w_out", "w_cq", "w_ckv", "w_co", "w_ff1", "w_ff2")


def _as_rows(stacked):
    return stacked.reshape(-1, stacked.shape[-1])


def _local_step(x, mem, target, small, shards):
    batch, seq, d = x.shape
    n_mem = mem.shape[1]
    t = batch * seq
    x2, mem2, tgt2 = x.reshape(t, d), mem.reshape(batch * n_mem, d), target.reshape(t, d)
    g_mix, g_cross, g_mem, g_ffn, g_final = (small[k] for k in ("norm_mix_g", "norm_cross_g", "norm_mem_g", "norm_ffn_g", "norm_final_g"))
    gv, hg, w_sp, b_sp_t = small["gm_v_norm_g"], small["head_norm_g"], small["w_spatial"], small["b_spatial_t"]

    win_t = _as_rows(_gather_two_level(shards["w_in"], name="gather_w_in"))
    proj, xn = _norm_mm(x2, g_mix, win_t, mode="nt", name="proj_fwd")
    merged_a = _gmlp_fwd(proj, w_sp, b_sp_t, gv, hg)
    merged, sb_raw, sb_weights, sb_log_rest, gathered = _sb_fwd(proj, merged_a, hg, batch=batch, seq=seq,
                                                                exchange=_Exchange([shards[n] for n in _LATER], scatter=False))
    wout, wcq, wckv_t, wco, wff1_t, wff2 = (_as_rows(g) for g in gathered)
    h1, qx, hn1 = _mm(merged, wout, mode="nn", out_dtype=F32, name="mix_out_fwd_xq_fwd", tm=1024, row_parts=2, epi=_epi_residual_then_norm_mm("nn"),
                      epi_ins=(x2,), vec_ins=(g_cross,), whole_ins=(wcq,), more_outs=((wcq.shape[1], BF16), (d, BF16)))
    kvx, memn = _norm_mm(mem2, g_mem, wckv_t, mode="nt", name="xkv_fwd")
    o = _xattn_fwd(qx, kvx, batch=batch, seq=seq, n_mem=n_mem)
    h2, fpre, hn2 = _mm(o, wco, mode="nn", out_dtype=F32, name="xo_fwd_ff1_fwd", tm=512, row_parts=2, epi=_epi_residual_then_norm_mm("nt"),
                        epi_ins=(h1,), vec_ins=(g_ffn,), whole_ins=(wff1_t,), more_outs=((wff1_t.shape[0], BF16), (d, BF16)))
    dh3, final_rows = _mm(fpre, wff2, mode="nn", out_dtype=F32, name="ff2_fwd_loss", tm=512, row_parts=2, tk=wff2.shape[0], a_fn=_relu2,
                          epi=_epi_loss, epi_ins=(h2, tgt2), vec_ins=(g_final,), aux=2)
    dg_final, sq_err = final_rows[0:1], final_rows[1:2]

    dpre = _mm(dh3, wff2, mode="nt", out_dtype=BF16, name="ff2_bwd_x", tm=512, tn=wff2.shape[0], row_parts=2, epi=_epi_relu2_grad,
               epi_ins=(fpre,))
    chunk = wff2.shape[0] // N_DEV
    d_wff2_t = _mm(dh3, fpre, mode="tn", out_dtype=BF16, name="ff2_bwd_w", tn=2048, b_fn=_relu2, col_chunk=chunk)
    d_wff1 = _mm(hn2, dpre, mode="tn", out_dtype=BF16, name="ff1_bwd_w", tn=2048, col_chunk=chunk)
    dh2, do, dg_ffn = _mm(dpre, wff1_t, mode="nn", out_dtype=F32, name="ff1_bwd_x_xo_bwd_x", tm=512, row_parts=2, tk=wff1_t.shape[0],
                          epi=_epi_rms_bwd_then_mm, epi_ins=(h2, dh3), vec_ins=(g_ffn,), whole_ins=(wco,),
                          more_outs=((wco.shape[0], BF16),), aux=True)
    d_wco = _mm(o, dh2, mode="tn", out_dtype=BF16, name="xo_bwd_w")
    dqx, dkvx = _xattn_bwd(qx, kvx, do, batch=batch, seq=seq, n_mem=n_mem)
    d_wcq = _mm(hn1, dqx, mode="tn", out_dtype=BF16, name="xq_bwd_w")
    dh1, dmerged, dg_cross = _mm(dqx, wcq, mode="nt", out_dtype=F32, name="xq_bwd_x_mix_out_bwd_x", tm=1024, row_parts=2,
                                 epi=_epi_rms_bwd_then_mm, epi_ins=(h1, dh2), vec_ins=(g_cross,), whole_ins=(wout,),
                                 more_outs=((wout.shape[0], BF16),), aux=True)
    d_wckv_t = _mm(dkvx, memn, mode="tn", out_dtype=BF16, name="xkv_bwd_w")
    _, dg_mem = _mm(dkvx, wckv_t, mode="nn", out_dtype=BF16, name="xkv_bwd_x", tm=512, epi=_epi_rms_gain_only,
                    epi_ins=(mem2,), vec_ins=(g_mem,), aux=True)
    d_wout = _mm(merged, dh1, mode="tn", out_dtype=BF16, name="mix_out_bwd_w")
    dp_a, d_wsp, d_bsp_t, d_gv, d_hg_a = _gmlp_bwd(proj, dmerged, w_sp, b_sp_t, gv, hg)
    d_later = {"w_out": d_wout, "w_cq": d_wcq, "w_ckv": d_wckv_t, "w_co": d_wco, "w_ff1": d_wff1, "w_ff2": d_wff2_t}
    scatter = _Exchange([g if g.ndim == 3 else g.reshape(N_DEV, -1, d) for g in (d_later[n] for n in _LATER)], scatter=True)
    dq, dk, dv, d_hg_b, received = _sb_bwd(proj, sb_raw, sb_weights, sb_log_rest, dmerged, hg, batch=batch, seq=seq, exchange=scatter)
    dproj = dp_a
    for part, at in ((dq, 2 * GM_WIDTH), (dk, 3 * GM_WIDTH), (dv, 4 * GM_WIDTH)):
        dproj = lax.dynamic_update_slice(dproj, part, (0, at))
    d_win_t = _mm(xn, dproj, mode="tn", out_dtype=BF16, name="proj_bwd_w", tn=dproj.shape[1] // 2).T
    dx, dg_mix, d_win_received = _mm(dproj, win_t, mode="nn", out_dtype=F32, name="proj_bwd_x", tm=512, row_parts=2, tk=win_t.shape[0],
                                     epi=_epi_rms_bwd, epi_ins=(x2, dh1), vec_ins=(g_mix,), aux=True,
                                     exchange=_Exchange([d_win_t.reshape(N_DEV, -1, d)], scatter=True))

    d_small = {"norm_mix_g": dg_mix, "gm_v_norm_g": d_gv, "w_spatial": d_wsp, "b_spatial_t": d_bsp_t, "head_norm_g": jnp.concatenate([d_hg_a, d_hg_b], axis=1),
               "norm_cross_g": dg_cross, "norm_mem_g": dg_mem, "norm_ffn_g": dg_ffn, "norm_final_g": dg_final}
    d_big = dict(zip(_LATER, received))
    d_big["w_in"] = d_win_received
    return sq_err, dx.reshape(batch, seq, d), d_small, d_big


_BIG = ("w_in", "w_out", "w_cq", "w_ckv", "w_co", "w_ff1", "w_ff2")
_GATHERED_TRANSPOSED = ("w_in", "w_ckv", "w_ff1")
_UPDATED_TRANSPOSED = ("w_in", "w_ckv", "w_ff2")
_SMALL = ("norm_mix_g", "gm_v_norm_g", "w_spatial", "b_spatial", "head_norm_g", "norm_cross_g", "norm_mem_g", "norm_ffn_g", "norm_final_g")
_NAMES = ("norm_mix_g", "w_in", "gm_v_norm_g", "w_spatial", "b_spatial", "head_norm_g", "w_out", "norm_cross_g", "norm_mem_g",
          "w_cq", "w_ckv", "w_co", "norm_ffn_g", "w_ff1", "w_ff2", "norm_final_g")


def _rows_of(a):
    r = a.reshape(-1, LANES)
    pad = (-r.shape[0]) % 8
    return jnp.pad(r, ((0, pad), (0, 0))) if pad else r


def _shard2d(a, transposed):
    return a[0].T if transposed else a[0]


def kernel(x, mem, norm_mix_g, w_in, gm_v_norm_g, w_spatial, b_spatial, head_norm_g, w_out, norm_cross_g, norm_mem_g, w_cq, w_ckv, w_co, norm_ffn_g, w_ff1, w_ff2, norm_final_g, loss_target, m_norm_mix_g, m_w_in, m_gm_v_norm_g, m_w_spatial, m_b_spatial, m_head_norm_g, m_w_out, m_norm_cross_g, m_norm_mem_g, m_w_cq, m_w_ckv, m_w_co, m_norm_ffn_g, m_w_ff1, m_w_ff2, m_norm_final_g, v_norm_mix_g, v_w_in, v_gm_v_norm_g, v_w_spatial, v_b_spatial, v_head_norm_g, v_w_out, v_norm_cross_g, v_norm_mem_g, v_w_cq, v_w_ckv, v_w_co, v_norm_ffn_g, v_w_ff1, v_w_ff2, v_norm_final_g):
    weights = dict(norm_mix_g=norm_mix_g, w_in=w_in, gm_v_norm_g=gm_v_norm_g, w_spatial=w_spatial, b_spatial=b_spatial,
                   head_norm_g=head_norm_g, w_out=w_out, norm_cross_g=norm_cross_g, norm_mem_g=norm_mem_g, w_cq=w_cq, w_ckv=w_ckv,
                   w_co=w_co, norm_ffn_g=norm_ffn_g, w_ff1=w_ff1, w_ff2=w_ff2, norm_final_g=norm_final_g)
    mom1 = dict(norm_mix_g=m_norm_mix_g, w_in=m_w_in, gm_v_norm_g=m_gm_v_norm_g, w_spatial=m_w_spatial, b_spatial=m_b_spatial,
                head_norm_g=m_head_norm_g, w_out=m_w_out, norm_cross_g=m_norm_cross_g, norm_mem_g=m_norm_mem_g, w_cq=m_w_cq,
                w_ckv=m_w_ckv, w_co=m_w_co, norm_ffn_g=m_norm_ffn_g, w_ff1=m_w_ff1, w_ff2=m_w_ff2, norm_final_g=m_norm_final_g)
    mom2 = dict(norm_mix_g=v_norm_mix_g, w_in=v_w_in, gm_v_norm_g=v_gm_v_norm_g, w_spatial=v_w_spatial, b_spatial=v_b_spatial,
                head_norm_g=v_head_norm_g, w_out=v_w_out, norm_cross_g=v_norm_cross_g, norm_mem_g=v_norm_mem_g, w_cq=v_w_cq,
                w_ckv=v_w_ckv, w_co=v_w_co, norm_ffn_g=v_norm_ffn_g, w_ff1=v_w_ff1, w_ff2=v_w_ff2, norm_final_g=v_norm_final_g)

    shards = {n: _shard2d(weights[n], n in _GATHERED_TRANSPOSED).astype(BF16) for n in _BIG}
    small = {n: weights[n].reshape(1, -1) for n in _SMALL if n not in ("w_spatial", "b_spatial")}
    small["w_spatial"] = w_spatial[0]
    small["b_spatial_t"] = b_spatial[0].T
    sq_err, grad_x, d_small, d_big = _local_step(x, mem, loss_target, small, shards)

    d_small["b_spatial"] = d_small.pop("b_spatial_t").T
    sq_rows = _rows_of(sq_err)
    packed = jnp.concatenate([_rows_of(d_small[n]) for n in _SMALL] + [sq_rows], axis=0)
    summed = _all_reduce_small(packed, loss_rows=sq_rows.shape[0], loss_scale=0.5 / x.shape[-1])

    grads, deltas, new_m, new_v = {}, {}, {}, {}
    for n in _BIG:
        on_chip = n in _UPDATED_TRANSPOSED and weights[n].shape[-1] % LANES == 0
        flip = n in _UPDATED_TRANSPOSED and not on_chip
        outs = _adamw(d_big[n], _shard2d(weights[n], flip), _shard2d(mom1[n], flip), _shard2d(mom2[n], flip), name="adamw_" + n,
                      parts_transposed=on_chip)
        outs = [o.T if flip else o for o in outs]
        grads[n], deltas[n], new_m[n], new_v[n] = (o[None] for o in outs)
    pack = lambda src: jnp.concatenate([_rows_of(src[n]) for n in _SMALL], axis=0)
    n_small_rows = sum(_rows_of(weights[n]).shape[0] for n in _SMALL)
    outs = _adamw(summed[:n_small_rows], pack(weights), pack(mom1), pack(mom2), name="adamw_small", tr=n_small_rows)
    at = 0
    for n in _SMALL:
        used = weights[n].size // LANES
        for dst, o in zip((grads, deltas, new_m, new_v), outs):
            dst[n] = o[at:at + used].reshape(weights[n].shape)
        at += _rows_of(weights[n]).shape[0]
    loss = summed[n_small_rows, 0]
    return (loss, grad_x, *[grads[n] for n in _NAMES], *[deltas[n] for n in _NAMES], *[new_m[n] for n in _NAMES],
            *[new_v[n] for n in _NAMES])
```
